```python
import jax, jax.numpy as jnp
from jax import lax
import numpy as np

D_MODEL = 1024
BATCH = 8
SEQ = 16384
DEPTH = 2

N_MEM = 256
POOL_WIDTH = D_MODEL
POOL_WINDOWS = (2, 4, 8, 16)
POOL_GROUPS = len(POOL_WINDOWS)
POOL_GROUP_DIM = POOL_WIDTH // POOL_GROUPS
LRU_WIDTH = D_MODEL
LRU_HEADS = 8
LRU_HEAD_DIM = LRU_WIDTH // LRU_HEADS
LRU_CONV = 4
LRU_C = 8.0
SCONV_WIDTH = D_MODEL
SCONV_K = 3
N_BRANCH = 3
SPLIT_POINTS = (POOL_WIDTH,
                POOL_WIDTH + LRU_WIDTH,
                POOL_WIDTH + LRU_WIDTH + SCONV_WIDTH,
                POOL_WIDTH + LRU_WIDTH + 2 * SCONV_WIDTH,
                POOL_WIDTH + LRU_WIDTH + 3 * SCONV_WIDTH)
IN_COLS = POOL_WIDTH + LRU_WIDTH + 3 * SCONV_WIDTH + N_BRANCH * D_MODEL
X_HEADS = 4
X_HEAD_DIM = D_MODEL // X_HEADS
D_FF = -(-8 * D_MODEL // (3 * 256)) * 256
ALPHA = (2 * DEPTH) ** 0.25
BETA = (8 * DEPTH) ** -0.25
LN_EPS = 1e-5

kernel_name = "hybrid_pool_rglru_shortconv_deepnorm"


def layer_norm(x, g, b):
    xf = x.astype(jnp.float32)
    mu = jnp.mean(xf, axis=-1, keepdims=True)
    var = jnp.mean(jnp.square(xf - mu), axis=-1, keepdims=True)
    y = (xf - mu) * lax.rsqrt(var + LN_EPS)
    return (y * g.astype(jnp.float32) + b.astype(jnp.float32)).astype(x.dtype)


def causal_depthwise_conv(u, w):
    k, c = w.shape
    return lax.conv_general_dilated(
        u, w[:, None, :].astype(u.dtype), window_strides=(1,),
        padding=((k - 1, 0),), dimension_numbers=("NWC", "WIO", "NWC"),
        feature_group_count=c)


def multiscale_pool(u, pool_w, pool_scale):
    bsz, s, _ = u.shape
    uf = u.astype(jnp.float32)
    cs = jnp.cumsum(uf, axis=1)
    count = jnp.arange(1, s + 1, dtype=jnp.float32)[None, :, None]
    outs = []
    for g, win in enumerate(POOL_WINDOWS):
        sl = slice(g * POOL_GROUP_DIM, (g + 1) * POOL_GROUP_DIM)
        c = cs[..., sl]
        lag = jnp.pad(c, ((0, 0), (win, 0), (0, 0)))[:, :s]
        outs.append((c - lag) / jnp.minimum(count, float(win)) - uf[..., sl])
    p = jnp.stack(outs, axis=2).astype(u.dtype)
    y = jnp.einsum("bsgc,gcd->bsgd", p, pool_w).reshape(bsz, s, POOL_WIDTH)
    return y * pool_scale


def rg_lru(u, conv_w, conv_b, w_r, b_r, w_i, b_i, lam):
    v = causal_depthwise_conv(u, conv_w) + conv_b
    bsz, s, _ = v.shape
    vh = v.reshape(bsz, s, LRU_HEADS, LRU_HEAD_DIM)
    r = jax.nn.sigmoid(jnp.einsum("bshc,hcd->bshd", vh, w_r).reshape(bsz, s, LRU_WIDTH) + b_r)
    i = jax.nn.sigmoid(jnp.einsum("bshc,hcd->bshd", vh, w_i).reshape(bsz, s, LRU_WIDTH) + b_i)
    log_a = -LRU_C * r.astype(jnp.float32) * jax.nn.softplus(-lam.astype(jnp.float32))
    a = jnp.exp(log_a)
    bterm = jnp.sqrt(-jnp.expm1(2.0 * log_a)) * (i * v).astype(jnp.float32)

    def combine(left, right):
        a1, b1 = left
        a2, b2 = right
        return a1 * a2, a2 * b1 + b2

    _, h = lax.associative_scan(combine, (a, bterm), axis=1)
    return h.astype(u.dtype)


def gated_short_conv(b_gate, c_gate, h, conv_w, w_out):
    return (b_gate * causal_depthwise_conv(c_gate * h, conv_w)) @ w_out


def hybrid_mixer(x, w_in, b_in, pool_w, pool_scale, lru_conv_w, lru_conv_b,
                 lru_w_r, lru_b_r, lru_w_i, lru_b_i, lru_lambda, lru_w_out,
                 sconv_w, sconv_w_out, w_mix_out):
    bsz, s, _ = x.shape
    z = x @ w_in + b_in
    z_pool, z_lru, z_b, z_c, z_h, z_gate = jnp.split(z, SPLIT_POINTS, axis=-1)
    y_pool = multiscale_pool(z_pool, pool_w, pool_scale)
    y_lru = rg_lru(z_lru, lru_conv_w, lru_conv_b, lru_w_r, lru_b_r,
                   lru_w_i, lru_b_i, lru_lambda) @ lru_w_out
    y_conv = gated_short_conv(z_b, z_c, z_h, sconv_w, sconv_w_out)
    gates = jax.nn.sigmoid(z_gate).reshape(bsz, s, N_BRANCH, D_MODEL)
    merged = (gates[:, :, 0] * y_pool + gates[:, :, 1] * y_lru
              + gates[:, :, 2] * y_conv)
    return merged @ w_mix_out


def memory_cross_attention(x, mem, w_q, w_k, w_v, w_o):
    bsz, s, _ = x.shape
    m = mem.shape[1]
    q = (x @ w_q).reshape(bsz, s, X_HEADS, X_HEAD_DIM)
    k = (mem @ w_k).reshape(bsz, m, X_HEADS, X_HEAD_DIM)
    v = (mem @ w_v).reshape(bsz, m, X_HEADS, X_HEAD_DIM)
    scores = jnp.einsum("bshd,bmhd->bhsm", q, k).astype(jnp.float32) * (X_HEAD_DIM ** -0.5)
    p = jax.nn.softmax(scores, axis=-1).astype(x.dtype)
    o = jnp.einsum("bhsm,bmhd->bshd", p, v).reshape(bsz, s, D_MODEL)
    return o @ w_o


def swiglu(x, w_gate, w_up, w_down):
    return (jax.nn.silu(x @ w_gate) * (x @ w_up)) @ w_down


def _fwd_setup_inputs(seed: int = 0) -> dict:
    key = jax.random.key(seed)
    ks = jax.random.split(key, 28)

    def nrm(k, shape, scale):
        return jax.random.normal(k, shape, jnp.float32) * scale

    a0 = jax.random.uniform(ks[12], (DEPTH, LRU_WIDTH), jnp.float32, 0.9, 0.999)
    sig = a0 ** (1.0 / LRU_C)
    lru_lambda = jnp.log(sig) - jnp.log1p(-sig)
    return {
        "x": nrm(ks[0], (BATCH, SEQ, D_MODEL), 1.0),
        "mem": nrm(ks[1], (BATCH, N_MEM, D_MODEL), 1.0),
        "w_in": nrm(ks[2], (DEPTH, D_MODEL, IN_COLS), D_MODEL ** -0.5),
        "b_in": nrm(ks[3], (DEPTH, IN_COLS), 0.02),
        "pool_w": nrm(ks[4], (DEPTH, POOL_GROUPS, POOL_GROUP_DIM, POOL_GROUP_DIM), POOL_GROUP_DIM ** -0.5),
        "pool_scale": 1.0 + nrm(ks[5], (DEPTH, POOL_WIDTH), 0.1),
        "lru_conv_w": nrm(ks[6], (DEPTH, LRU_CONV, LRU_WIDTH), LRU_CONV ** -0.5),
        "lru_conv_b": nrm(ks[7], (DEPTH, LRU_WIDTH), 0.02),
        "lru_w_r": nrm(ks[8], (DEPTH, LRU_HEADS, LRU_HEAD_DIM, LRU_HEAD_DIM), LRU_HEAD_DIM ** -0.5),
        "lru_b_r": nrm(ks[9], (DEPTH, LRU_WIDTH), 0.02),
        "lru_w_i": nrm(ks[10], (DEPTH, LRU_HEADS, LRU_HEAD_DIM, LRU_HEAD_DIM), LRU_HEAD_DIM ** -0.5),
        "lru_b_i": nrm(ks[11], (DEPTH, LRU_WIDTH), 0.02),
        "lru_lambda": lru_lambda,
        "lru_w_out": nrm(ks[13], (DEPTH, LRU_WIDTH, D_MODEL), LRU_WIDTH ** -0.5),
        "sconv_w": nrm(ks[14], (DEPTH, SCONV_K, SCONV_WIDTH), SCONV_K ** -0.5),
        "sconv_w_out": nrm(ks[15], (DEPTH, SCONV_WIDTH, D_MODEL), SCONV_WIDTH ** -0.5),
        "w_mix_out": nrm(ks[16], (DEPTH, D_MODEL, D_MODEL), BETA * D_MODEL ** -0.5),
        "xa_w_q": nrm(ks[17], (DEPTH, D_MODEL, D_MODEL), D_MODEL ** -0.5),
        "xa_w_k": nrm(ks[18], (DEPTH, D_MODEL, D_MODEL), D_MODEL ** -0.5),
        "xa_w_v": nrm(ks[19], (DEPTH, D_MODEL, D_MODEL), BETA * D_MODEL ** -0.5),
        "xa_w_o": nrm(ks[20], (DEPTH, D_MODEL, D_MODEL), BETA * D_MODEL ** -0.5),
        "ffn_w_gate": nrm(ks[21], (DEPTH, D_MODEL, D_FF), D_MODEL ** -0.5),
        "ffn_w_up": nrm(ks[22], (DEPTH, D_MODEL, D_FF), D_MODEL ** -0.5),
        "ffn_w_down": nrm(ks[23], (DEPTH, D_FF, D_MODEL), BETA * D_FF ** -0.5),
        "ln_g": 1.0 + nrm(ks[24], (DEPTH, 3, D_MODEL), 0.05),
        "ln_b": nrm(ks[25], (DEPTH, 3, D_MODEL), 0.02),
    }


def _fwd_reference(x, mem, w_in, b_in, pool_w, pool_scale, lru_conv_w, lru_conv_b,
              lru_w_r, lru_b_r, lru_w_i, lru_b_i, lru_lambda, lru_w_out,
              sconv_w, sconv_w_out, w_mix_out, xa_w_q, xa_w_k, xa_w_v, xa_w_o,
              ffn_w_gate, ffn_w_up, ffn_w_down, ln_g, ln_b):
    for l in range(DEPTH):
        mix = hybrid_mixer(x, w_in[l], b_in[l], pool_w[l], pool_scale[l],
                           lru_conv_w[l], lru_conv_b[l], lru_w_r[l], lru_b_r[l],
                           lru_w_i[l], lru_b_i[l], lru_lambda[l], lru_w_out[l],
                           sconv_w[l], sconv_w_out[l], w_mix_out[l])
        x = layer_norm(ALPHA * x + mix, ln_g[l, 0], ln_b[l, 0])
        xa = memory_cross_attention(x, mem, xa_w_q[l], xa_w_k[l], xa_w_v[l], xa_w_o[l])
        x = layer_norm(ALPHA * x + xa, ln_g[l, 1], ln_b[l, 1])
        ff = swiglu(x, ffn_w_gate[l], ffn_w_up[l], ffn_w_down[l])
        x = layer_norm(ALPHA * x + ff, ln_g[l, 2], ln_b[l, 2])
    return x


import jax as _jax
import jax.numpy as _jnp

TWIN_FORMAT = 'train_step'
FWD_PARAMS = ['x', 'mem', 'w_in', 'b_in', 'pool_w', 'pool_scale', 'lru_conv_w', 'lru_conv_b', 'lru_w_r', 'lru_b_r', 'lru_w_i', 'lru_b_i', 'lru_lambda', 'lru_w_out', 'sconv_w', 'sconv_w_out', 'w_mix_out', 'xa_w_q', 'xa_w_k', 'xa_w_v', 'xa_w_o', 'ffn_w_gate', 'ffn_w_up', 'ffn_w_down', 'ln_g', 'ln_b']
TWIN_WEIGHTS = ['w_in', 'b_in', 'pool_w', 'pool_scale', 'lru_conv_w', 'lru_conv_b', 'lru_w_r', 'lru_b_r', 'lru_w_i', 'lru_b_i', 'lru_lambda', 'lru_w_out', 'sconv_w', 'sconv_w_out', 'w_mix_out', 'xa_w_q', 'xa_w_k', 'xa_w_v', 'xa_w_o', 'ffn_w_gate', 'ffn_w_up', 'ffn_w_down', 'ln_g', 'ln_b']
TWIN_DIFF_INPUT = 'x'
TWIN_INPUTS = ['x', 'mem', 'w_in', 'b_in', 'pool_w', 'pool_scale', 'lru_conv_w', 'lru_conv_b', 'lru_w_r', 'lru_b_r', 'lru_w_i', 'lru_b_i', 'lru_lambda', 'lru_w_out', 'sconv_w', 'sconv_w_out', 'w_mix_out', 'xa_w_q', 'xa_w_k', 'xa_w_v', 'xa_w_o', 'ffn_w_gate', 'ffn_w_up', 'ffn_w_down', 'ln_g', 'ln_b', 'loss_target', 'm_w_in', 'm_b_in', 'm_pool_w', 'm_pool_scale', 'm_lru_conv_w', 'm_lru_conv_b', 'm_lru_w_r', 'm_lru_b_r', 'm_lru_w_i', 'm_lru_b_i', 'm_lru_lambda', 'm_lru_w_out', 'm_sconv_w', 'm_sconv_w_out', 'm_w_mix_out', 'm_xa_w_q', 'm_xa_w_k', 'm_xa_w_v', 'm_xa_w_o', 'm_ffn_w_gate', 'm_ffn_w_up', 'm_ffn_w_down', 'm_ln_g', 'm_ln_b', 'v_w_in', 'v_b_in', 'v_pool_w', 'v_pool_scale', 'v_lru_conv_w', 'v_lru_conv_b', 'v_lru_w_r', 'v_lru_b_r', 'v_lru_w_i', 'v_lru_b_i', 'v_lru_lambda', 'v_lru_w_out', 'v_sconv_w', 'v_sconv_w_out', 'v_w_mix_out', 'v_xa_w_q', 'v_xa_w_k', 'v_xa_w_v', 'v_xa_w_o', 'v_ffn_w_gate', 'v_ffn_w_up', 'v_ffn_w_down', 'v_ln_g', 'v_ln_b']
TWIN_OUTPUTS = ['loss', 'grad_x', 'grad_w_in', 'grad_b_in', 'grad_pool_w', 'grad_pool_scale', 'grad_lru_conv_w', 'grad_lru_conv_b', 'grad_lru_w_r', 'grad_lru_b_r', 'grad_lru_w_i', 'grad_lru_b_i', 'grad_lru_lambda', 'grad_lru_w_out', 'grad_sconv_w', 'grad_sconv_w_out', 'grad_w_mix_out', 'grad_xa_w_q', 'grad_xa_w_k', 'grad_xa_w_v', 'grad_xa_w_o', 'grad_ffn_w_gate', 'grad_ffn_w_up', 'grad_ffn_w_down', 'grad_ln_g', 'grad_ln_b', 'delta_w_in', 'delta_b_in', 'delta_pool_w', 'delta_pool_scale', 'delta_lru_conv_w', 'delta_lru_conv_b', 'delta_lru_w_r', 'delta_lru_b_r', 'delta_lru_w_i', 'delta_lru_b_i', 'delta_lru_lambda', 'delta_lru_w_out', 'delta_sconv_w', 'delta_sconv_w_out', 'delta_w_mix_out', 'delta_xa_w_q', 'delta_xa_w_k', 'delta_xa_w_v', 'delta_xa_w_o', 'delta_ffn_w_gate', 'delta_ffn_w_up', 'delta_ffn_w_down', 'delta_ln_g', 'delta_ln_b', 'new_m_w_in', 'new_m_b_in', 'new_m_pool_w', 'new_m_pool_scale', 'new_m_lru_conv_w', 'new_m_lru_conv_b', 'new_m_lru_w_r', 'new_m_lru_b_r', 'new_m_lru_w_i', 'new_m_lru_b_i', 'new_m_lru_lambda', 'new_m_lru_w_out', 'new_m_sconv_w', 'new_m_sconv_w_out', 'new_m_w_mix_out', 'new_m_xa_w_q', 'new_m_xa_w_k', 'new_m_xa_w_v', 'new_m_xa_w_o', 'new_m_ffn_w_gate', 'new_m_ffn_w_up', 'new_m_ffn_w_down', 'new_m_ln_g', 'new_m_ln_b', 'new_v_w_in', 'new_v_b_in', 'new_v_pool_w', 'new_v_pool_scale', 'new_v_lru_conv_w', 'new_v_lru_conv_b', 'new_v_lru_w_r', 'new_v_lru_b_r', 'new_v_lru_w_i', 'new_v_lru_b_i', 'new_v_lru_lambda', 'new_v_lru_w_out', 'new_v_sconv_w', 'new_v_sconv_w_out', 'new_v_w_mix_out', 'new_v_xa_w_q', 'new_v_xa_w_k', 'new_v_xa_w_v', 'new_v_xa_w_o', 'new_v_ffn_w_gate', 'new_v_ffn_w_up', 'new_v_ffn_w_down', 'new_v_ln_g', 'new_v_ln_b']
TWIN_LEAF_KINDS = {'loss': 'loss', 'grad_x': 'grad_x', 'grad_w_in': 'grad_w', 'grad_b_in': 'grad_w', 'grad_pool_w': 'grad_w', 'grad_pool_scale': 'grad_w', 'grad_lru_conv_w': 'grad_w', 'grad_lru_conv_b': 'grad_w', 'grad_lru_w_r': 'grad_w', 'grad_lru_b_r': 'grad_w', 'grad_lru_w_i': 'grad_w', 'grad_lru_b_i': 'grad_w', 'grad_lru_lambda': 'grad_w', 'grad_lru_w_out': 'grad_w', 'grad_sconv_w': 'grad_w', 'grad_sconv_w_out': 'grad_w', 'grad_w_mix_out': 'grad_w', 'grad_xa_w_q': 'grad_w', 'grad_xa_w_k': 'grad_w', 'grad_xa_w_v': 'grad_w', 'grad_xa_w_o': 'grad_w', 'grad_ffn_w_gate': 'grad_w', 'grad_ffn_w_up': 'grad_w', 'grad_ffn_w_down': 'grad_w', 'grad_ln_g': 'grad_w', 'grad_ln_b': 'grad_w', 'delta_w_in': 'delta_w', 'delta_b_in': 'delta_w', 'delta_pool_w': 'delta_w', 'delta_pool_scale': 'delta_w', 'delta_lru_conv_w': 'delta_w', 'delta_lru_conv_b': 'delta_w', 'delta_lru_w_r': 'delta_w', 'delta_lru_b_r': 'delta_w', 'delta_lru_w_i': 'delta_w', 'delta_lru_b_i': 'delta_w', 'delta_lru_lambda': 'delta_w', 'delta_lru_w_out': 'delta_w', 'delta_sconv_w': 'delta_w', 'delta_sconv_w_out': 'delta_w', 'delta_w_mix_out': 'delta_w', 'delta_xa_w_q': 'delta_w', 'delta_xa_w_k': 'delta_w', 'delta_xa_w_v': 'delta_w', 'delta_xa_w_o': 'delta_w', 'delta_ffn_w_gate': 'delta_w', 'delta_ffn_w_up': 'delta_w', 'delta_ffn_w_down': 'delta_w', 'delta_ln_g': 'delta_w', 'delta_ln_b': 'delta_w', 'new_m_w_in': 'new_m', 'new_m_b_in': 'new_m', 'new_m_pool_w': 'new_m', 'new_m_pool_scale': 'new_m', 'new_m_lru_conv_w': 'new_m', 'new_m_lru_conv_b': 'new_m', 'new_m_lru_w_r': 'new_m', 'new_m_lru_b_r': 'new_m', 'new_m_lru_w_i': 'new_m', 'new_m_lru_b_i': 'new_m', 'new_m_lru_lambda': 'new_m', 'new_m_lru_w_out': 'new_m', 'new_m_sconv_w': 'new_m', 'new_m_sconv_w_out': 'new_m', 'new_m_w_mix_out': 'new_m', 'new_m_xa_w_q': 'new_m', 'new_m_xa_w_k': 'new_m', 'new_m_xa_w_v': 'new_m', 'new_m_xa_w_o': 'new_m', 'new_m_ffn_w_gate': 'new_m', 'new_m_ffn_w_up': 'new_m', 'new_m_ffn_w_down': 'new_m', 'new_m_ln_g': 'new_m', 'new_m_ln_b': 'new_m', 'new_v_w_in': 'new_v', 'new_v_b_in': 'new_v', 'new_v_pool_w': 'new_v', 'new_v_pool_scale': 'new_v', 'new_v_lru_conv_w': 'new_v', 'new_v_lru_conv_b': 'new_v', 'new_v_lru_w_r': 'new_v', 'new_v_lru_b_r': 'new_v', 'new_v_lru_w_i': 'new_v', 'new_v_lru_b_i': 'new_v', 'new_v_lru_lambda': 'new_v', 'new_v_lru_w_out': 'new_v', 'new_v_sconv_w': 'new_v', 'new_v_sconv_w_out': 'new_v', 'new_v_w_mix_out': 'new_v', 'new_v_xa_w_q': 'new_v', 'new_v_xa_w_k': 'new_v', 'new_v_xa_w_v': 'new_v', 'new_v_xa_w_o': 'new_v', 'new_v_ffn_w_gate': 'new_v', 'new_v_ffn_w_up': 'new_v', 'new_v_ffn_w_down': 'new_v', 'new_v_ln_g': 'new_v', 'new_v_ln_b': 'new_v'}


def _forward(args):
    return _fwd_reference(*[args[k] for k in FWD_PARAMS])


def _output_shape():
    def fwd():
        inp = _fwd_setup_inputs(0)
        return _fwd_reference(*[inp[k] for k in FWD_PARAMS])
    out = _jax.eval_shape(fwd)
    return out.shape, out.dtype

N_MICROBATCH = 1
ADAM_LR = 0.001
ADAM_B1 = 0.9
ADAM_B2 = 0.999
ADAM_EPS = 1e-08
ADAM_WD = 0.01
ADAM_STEP = 10
PER_EXAMPLE_BATCH_AXIS = {'x': 0, 'mem': 0, 'loss_target': 0}
SHARED_INPUTS = []
_WEIGHT_DTYPES = {'w_in': _jnp.float32, 'b_in': _jnp.float32, 'pool_w': _jnp.float32, 'pool_scale': _jnp.float32, 'lru_conv_w': _jnp.float32, 'lru_conv_b': _jnp.float32, 'lru_w_r': _jnp.float32, 'lru_b_r': _jnp.float32, 'lru_w_i': _jnp.float32, 'lru_b_i': _jnp.float32, 'lru_lambda': _jnp.float32, 'lru_w_out': _jnp.float32, 'sconv_w': _jnp.float32, 'sconv_w_out': _jnp.float32, 'w_mix_out': _jnp.float32, 'xa_w_q': _jnp.float32, 'xa_w_k': _jnp.float32, 'xa_w_v': _jnp.float32, 'xa_w_o': _jnp.float32, 'ffn_w_gate': _jnp.float32, 'ffn_w_up': _jnp.float32, 'ffn_w_down': _jnp.float32, 'ln_g': _jnp.float32, 'ln_b': _jnp.float32}
MOMENT_SCALE = {'w_in': 5.641207e-02, 'b_in': 9.970456e-01, 'pool_w': 6.378414e-02, 'pool_scale': 6.526249e-02, 'lru_conv_w': 1.398296e-01, 'lru_conv_b': 2.680901e+00, 'lru_w_r': 6.937999e-02, 'lru_b_r': 5.512854e-02, 'lru_w_i': 1.327498e-01, 'lru_b_i': 5.041733e-02, 'lru_lambda': 8.830308e-02, 'lru_w_out': 1.868849e-01, 'sconv_w': 6.988126e-02, 'sconv_w_out': 6.888700e-02, 'w_mix_out': 3.023320e-01, 'xa_w_q': 7.020212e-03, 'xa_w_k': 7.008248e-03, 'xa_w_v': 1.955679e-02, 'xa_w_o': 1.963843e-02, 'ffn_w_gate': 4.716255e-02, 'ffn_w_up': 4.652908e-02, 'ffn_w_down': 1.546663e-01, 'ln_g': 5.428162e+01, 'ln_b': 5.968030e+00}


def _to_microbatches(a, axis):
    t = _jnp.moveaxis(a, axis, 0)
    t = t.reshape((N_MICROBATCH, t.shape[0] // N_MICROBATCH) + t.shape[1:])
    return _jnp.moveaxis(t, 1, axis + 1)


def setup_inputs(seed: int = 0) -> dict:
    inp = _fwd_setup_inputs(seed)
    key = _jax.random.fold_in(_jax.random.key(seed), 7919)
    shape, _ = _output_shape()
    out = dict(inp)
    out["loss_target"] = _jax.random.normal(_jax.random.fold_in(key, 0), shape, _jnp.float32)
    for i, name in enumerate(TWIN_WEIGHTS):
        w = inp[name].astype(_jnp.float32)
        if MOMENT_SCALE is None:
            s = _jnp.sqrt(_jnp.mean(_jnp.square(w)) + 1e-30)
        else:
            s = MOMENT_SCALE[name]
        km, kv = _jax.random.split(_jax.random.fold_in(key, i + 1))
        out[name] = w
        out["m_" + name] = s * _jax.random.normal(km, w.shape, _jnp.float32)
        out["v_" + name] = (s * s) * _jax.random.uniform(kv, w.shape, _jnp.float32, 0.5, 1.5)
    if N_MICROBATCH > 1:
        for name, axis in PER_EXAMPLE_BATCH_AXIS.items():
            out[name] = _to_microbatches(out[name], axis)
    return {'x': out['x'], 'mem': out['mem'], 'w_in': out['w_in'], 'b_in': out['b_in'], 'pool_w': out['pool_w'], 'pool_scale': out['pool_scale'], 'lru_conv_w': out['lru_conv_w'], 'lru_conv_b': out['lru_conv_b'], 'lru_w_r': out['lru_w_r'], 'lru_b_r': out['lru_b_r'], 'lru_w_i': out['lru_w_i'], 'lru_b_i': out['lru_b_i'], 'lru_lambda': out['lru_lambda'], 'lru_w_out': out['lru_w_out'], 'sconv_w': out['sconv_w'], 'sconv_w_out': out['sconv_w_out'], 'w_mix_out': out['w_mix_out'], 'xa_w_q': out['xa_w_q'], 'xa_w_k': out['xa_w_k'], 'xa_w_v': out['xa_w_v'], 'xa_w_o': out['xa_w_o'], 'ffn_w_gate': out['ffn_w_gate'], 'ffn_w_up': out['ffn_w_up'], 'ffn_w_down': out['ffn_w_down'], 'ln_g': out['ln_g'], 'ln_b': out['ln_b'], 'loss_target': out['loss_target'], 'm_w_in': out['m_w_in'], 'm_b_in': out['m_b_in'], 'm_pool_w': out['m_pool_w'], 'm_pool_scale': out['m_pool_scale'], 'm_lru_conv_w': out['m_lru_conv_w'], 'm_lru_conv_b': out['m_lru_conv_b'], 'm_lru_w_r': out['m_lru_w_r'], 'm_lru_b_r': out['m_lru_b_r'], 'm_lru_w_i': out['m_lru_w_i'], 'm_lru_b_i': out['m_lru_b_i'], 'm_lru_lambda': out['m_lru_lambda'], 'm_lru_w_out': out['m_lru_w_out'], 'm_sconv_w': out['m_sconv_w'], 'm_sconv_w_out': out['m_sconv_w_out'], 'm_w_mix_out': out['m_w_mix_out'], 'm_xa_w_q': out['m_xa_w_q'], 'm_xa_w_k': out['m_xa_w_k'], 'm_xa_w_v': out['m_xa_w_v'], 'm_xa_w_o': out['m_xa_w_o'], 'm_ffn_w_gate': out['m_ffn_w_gate'], 'm_ffn_w_up': out['m_ffn_w_up'], 'm_ffn_w_down': out['m_ffn_w_down'], 'm_ln_g': out['m_ln_g'], 'm_ln_b': out['m_ln_b'], 'v_w_in': out['v_w_in'], 'v_b_in': out['v_b_in'], 'v_pool_w': out['v_pool_w'], 'v_pool_scale': out['v_pool_scale'], 'v_lru_conv_w': out['v_lru_conv_w'], 'v_lru_conv_b': out['v_lru_conv_b'], 'v_lru_w_r': out['v_lru_w_r'], 'v_lru_b_r': out['v_lru_b_r'], 'v_lru_w_i': out['v_lru_w_i'], 'v_lru_b_i': out['v_lru_b_i'], 'v_lru_lambda': out['v_lru_lambda'], 'v_lru_w_out': out['v_lru_w_out'], 'v_sconv_w': out['v_sconv_w'], 'v_sconv_w_out': out['v_sconv_w_out'], 'v_w_mix_out': out['v_w_mix_out'], 'v_xa_w_q': out['v_xa_w_q'], 'v_xa_w_k': out['v_xa_w_k'], 'v_xa_w_v': out['v_xa_w_v'], 'v_xa_w_o': out['v_xa_w_o'], 'v_ffn_w_gate': out['v_ffn_w_gate'], 'v_ffn_w_up': out['v_ffn_w_up'], 'v_ffn_w_down': out['v_ffn_w_down'], 'v_ln_g': out['v_ln_g'], 'v_ln_b': out['v_ln_b']}


def _loss(weights, diff, rest, loss_target):
    with _jax.named_scope("forward"):
        args = {**rest, TWIN_DIFF_INPUT: diff, **{k: w.astype(_WEIGHT_DTYPES[k]) for k, w in weights.items()}}
        y = _forward(args)
    with _jax.named_scope("loss_head"):
        err = _jnp.square(y.astype(_jnp.float32) - loss_target)
        return 0.5 * _jnp.sum(_jnp.mean(err, axis=-1)) if err.ndim else 0.5 * err


def _adamw(w, g, m, v):
    m = ADAM_B1 * m + (1.0 - ADAM_B1) * g
    v = ADAM_B2 * v + (1.0 - ADAM_B2) * _jnp.square(g)
    m_hat = m / (1.0 - ADAM_B1 ** ADAM_STEP)
    v_hat = v / (1.0 - ADAM_B2 ** ADAM_STEP)
    delta = -ADAM_LR * (m_hat / (_jnp.sqrt(v_hat) + ADAM_EPS) + ADAM_WD * w)
    return delta, m, v


def reference(x, mem, w_in, b_in, pool_w, pool_scale, lru_conv_w, lru_conv_b, lru_w_r, lru_b_r, lru_w_i, lru_b_i, lru_lambda, lru_w_out, sconv_w, sconv_w_out, w_mix_out, xa_w_q, xa_w_k, xa_w_v, xa_w_o, ffn_w_gate, ffn_w_up, ffn_w_down, ln_g, ln_b, loss_target, m_w_in, m_b_in, m_pool_w, m_pool_scale, m_lru_conv_w, m_lru_conv_b, m_lru_w_r, m_lru_b_r, m_lru_w_i, m_lru_b_i, m_lru_lambda, m_lru_w_out, m_sconv_w, m_sconv_w_out, m_w_mix_out, m_xa_w_q, m_xa_w_k, m_xa_w_v, m_xa_w_o, m_ffn_w_gate, m_ffn_w_up, m_ffn_w_down, m_ln_g, m_ln_b, v_w_in, v_b_in, v_pool_w, v_pool_scale, v_lru_conv_w, v_lru_conv_b, v_lru_w_r, v_lru_b_r, v_lru_w_i, v_lru_b_i, v_lru_lambda, v_lru_w_out, v_sconv_w, v_sconv_w_out, v_w_mix_out, v_xa_w_q, v_xa_w_k, v_xa_w_v, v_xa_w_o, v_ffn_w_gate, v_ffn_w_up, v_ffn_w_down, v_ln_g, v_ln_b):
    given = dict(x=x, mem=mem, w_in=w_in, b_in=b_in, pool_w=pool_w, pool_scale=pool_scale, lru_conv_w=lru_conv_w, lru_conv_b=lru_conv_b, lru_w_r=lru_w_r, lru_b_r=lru_b_r, lru_w_i=lru_w_i, lru_b_i=lru_b_i, lru_lambda=lru_lambda, lru_w_out=lru_w_out, sconv_w=sconv_w, sconv_w_out=sconv_w_out, w_mix_out=w_mix_out, xa_w_q=xa_w_q, xa_w_k=xa_w_k, xa_w_v=xa_w_v, xa_w_o=xa_w_o, ffn_w_gate=ffn_w_gate, ffn_w_up=ffn_w_up, ffn_w_down=ffn_w_down, ln_g=ln_g, ln_b=ln_b, loss_target=loss_target, m_w_in=m_w_in, m_b_in=m_b_in, m_pool_w=m_pool_w, m_pool_scale=m_pool_scale, m_lru_conv_w=m_lru_conv_w, m_lru_conv_b=m_lru_conv_b, m_lru_w_r=m_lru_w_r, m_lru_b_r=m_lru_b_r, m_lru_w_i=m_lru_w_i, m_lru_b_i=m_lru_b_i, m_lru_lambda=m_lru_lambda, m_lru_w_out=m_lru_w_out, m_sconv_w=m_sconv_w, m_sconv_w_out=m_sconv_w_out, m_w_mix_out=m_w_mix_out, m_xa_w_q=m_xa_w_q, m_xa_w_k=m_xa_w_k, m_xa_w_v=m_xa_w_v, m_xa_w_o=m_xa_w_o, m_ffn_w_gate=m_ffn_w_gate, m_ffn_w_up=m_ffn_w_up, m_ffn_w_down=m_ffn_w_down, m_ln_g=m_ln_g, m_ln_b=m_ln_b, v_w_in=v_w_in, v_b_in=v_b_in, v_pool_w=v_pool_w, v_pool_scale=v_pool_scale, v_lru_conv_w=v_lru_conv_w, v_lru_conv_b=v_lru_conv_b, v_lru_w_r=v_lru_w_r, v_lru_b_r=v_lru_b_r, v_lru_w_i=v_lru_w_i, v_lru_b_i=v_lru_b_i, v_lru_lambda=v_lru_lambda, v_lru_w_out=v_lru_w_out, v_sconv_w=v_sconv_w, v_sconv_w_out=v_sconv_w_out, v_w_mix_out=v_w_mix_out, v_xa_w_q=v_xa_w_q, v_xa_w_k=v_xa_w_k, v_xa_w_v=v_xa_w_v, v_xa_w_o=v_xa_w_o, v_ffn_w_gate=v_ffn_w_gate, v_ffn_w_up=v_ffn_w_up, v_ffn_w_down=v_ffn_w_down, v_ln_g=v_ln_g, v_ln_b=v_ln_b)
    weights = {n: given[n] for n in TWIN_WEIGHTS}
    shared = {n: given[n] for n in SHARED_INPUTS}
    per_example = {n: given[n] for n in ['x', 'mem']}
    grad_fn = _jax.value_and_grad(_loss, argnums=(0, 1))

    def one_microbatch(ex, loss_target):
        ex = dict(ex)
        diff = ex.pop(TWIN_DIFF_INPUT)
        return grad_fn(weights, diff, {**shared, **ex}, loss_target)

    if N_MICROBATCH == 1:
        loss, (grad_w, grad_x) = one_microbatch(per_example, given["loss_target"])
    else:
        def body(carry, xs):
            loss_sum, grad_sum = carry
            l_k, (gw_k, gx_k) = one_microbatch(xs[0], xs[1])
            with _jax.named_scope("update"):
                return (loss_sum + l_k, _jax.tree.map(_jnp.add, grad_sum, gw_k)), gx_k

        init = (_jnp.zeros((), _jnp.float32), _jax.tree.map(_jnp.zeros_like, weights))
        (loss, grad_w), grad_x = _jax.lax.scan(body, init, (per_example, given["loss_target"]))
    with _jax.named_scope("update"):
        delta_w, new_m, new_v = {}, {}, {}
        for n in TWIN_WEIGHTS:
            delta_w[n], new_m[n], new_v[n] = _adamw(weights[n], grad_w[n], given["m_" + n], given["v_" + n])
    return (loss, grad_x, *[grad_w[n] for n in TWIN_WEIGHTS], *[delta_w[n] for n in TWIN_WEIGHTS],
            *[new_m[n] for n in TWIN_WEIGHTS], *[new_v[n] for n in TWIN_WEIGHTS])
```

```python
import functools

import jax
import jax.numpy as jnp
from jax import lax
from jax.experimental import pallas as pl
from jax.experimental.pallas import tpu as pltpu

F32 = jnp.float32
BF = jnp.bfloat16
MESH = pl.DeviceIdType.MESH

D = 1024
DEPTH = 2
N_MEM = 256
POOL_WINDOWS = (2, 4, 8, 16)
POOL_GD = 256
LRU_HEADS = 8
LRU_HD = 128
LRU_C = 8.0
X_HEADS = 4
X_HD = 256
D_FF = 2816
IN_COLS = 8 * D
ALPHA = (2 * DEPTH) ** 0.25
LN_EPS = 1e-5
ADAM_LR = 0.001
ADAM_B1 = 0.9
ADAM_B2 = 0.999
ADAM_EPS = 1e-08
ADAM_WD = 0.01
ADAM_STEP = 10

Z_PERM = (2, 3, 4, 5, 6, 7, 0, 1)
ZB_SCONV, ZB_GATE, ZB_POOL, ZB_LRU = 0, 1, 6, 7
HALO = 16
VMEM_LIMIT = 56 * 1024 * 1024

WEIGHTS = ['w_in', 'b_in', 'pool_w', 'pool_scale', 'lru_conv_w', 'lru_conv_b', 'lru_w_r', 'lru_b_r', 'lru_w_i',
           'lru_b_i', 'lru_lambda', 'lru_w_out', 'sconv_w', 'sconv_w_out', 'w_mix_out', 'xa_w_q', 'xa_w_k', 'xa_w_v',
           'xa_w_o', 'ffn_w_gate', 'ffn_w_up', 'ffn_w_down', 'ln_g', 'ln_b']
BIG_SHARDED = {'w_in': 2, 'pool_w': 2, 'lru_w_out': 1, 'sconv_w_out': 1, 'w_mix_out': 1, 'xa_w_q': 1, 'xa_w_k': 1,
               'xa_w_v': 1, 'xa_w_o': 1, 'ffn_w_gate': 2, 'ffn_w_up': 2, 'ffn_w_down': 1}
SMALL_SHARDED = {'lru_conv_w': 2, 'sconv_w': 2, 'ln_g': 2, 'ln_b': 2}
REPLICATED = ['b_in', 'pool_scale', 'lru_conv_b', 'lru_w_r', 'lru_b_r', 'lru_w_i', 'lru_b_i', 'lru_lambda']
SMALL_ALL = ['b_in', 'pool_scale', 'lru_conv_w', 'lru_conv_b', 'lru_w_r', 'lru_b_r', 'lru_w_i', 'lru_b_i',
             'lru_lambda', 'sconv_w', 'ln_g', 'ln_b']


def _cp(*sem):
    return pltpu.CompilerParams(dimension_semantics=sem, vmem_limit_bytes=VMEM_LIMIT)


def _sigmoid(x):
    return 1.0 / (1.0 + jnp.exp(-x))


def mm_nn(a, w, bias, *, out_dtype, tm, tn, name):
    T, K = a.shape
    N = w.shape[1]
    tm, tn = min(tm, T), min(tn, N)

    def body(*refs):
        if bias is None:
            a_ref, w_ref, o_ref = refs
        else:
            a_ref, w_ref, b_ref, o_ref = refs
        acc = jnp.dot(a_ref[...].astype(BF), w_ref[...], preferred_element_type=F32)
        if bias is not None:
            acc = acc + b_ref[...]
        o_ref[...] = acc.astype(o_ref.dtype)

    in_specs = [pl.BlockSpec((tm, K), lambda j, i: (i, 0)), pl.BlockSpec((K, tn), lambda j, i: (0, j))]
    args = [a, w]
    if bias is not None:
        in_specs.append(pl.BlockSpec((1, tn), lambda j, i: (0, j)))
        args.append(bias)
    return pl.pallas_call(
        body, name=name, grid=(N // tn, T // tm), in_specs=in_specs,
        out_specs=pl.BlockSpec((tm, tn), lambda j, i: (i, j)),
        out_shape=jax.ShapeDtypeStruct((T, N), out_dtype), compiler_params=_cp("arbitrary", "arbitrary"))(*args)


def mm_nt(a, w, res, *, out_dtype, tm, tc, name):
    T, C = a.shape
    K = w.shape[0]
    tm, tc = min(tm, T), min(tc, C)
    nc = C // tc

    def body(*refs):
        if res is None:
            a_ref, w_ref, o_ref, acc_ref = refs
        else:
            a_ref, w_ref, r_ref, o_ref, acc_ref = refs
        c = pl.program_id(1)
        part = lax.dot_general(a_ref[...].astype(BF), w_ref[...], (((1,), (1,)), ((), ())),
                               preferred_element_type=F32)

        @pl.when(c == 0)
        def _():
            acc_ref[...] = part

        @pl.when(c > 0)
        def _():
            acc_ref[...] += part

        @pl.when(c == nc - 1)
        def _():
            out = acc_ref[...]
            if res is not None:
                out = out + ALPHA * r_ref[...]
            o_ref[...] = out.astype(o_ref.dtype)

    in_specs = [pl.BlockSpec((tm, tc), lambda i, c: (i, c)), pl.BlockSpec((K, tc), lambda i, c: (0, c))]
    args = [a, w]
    if res is not None:
        in_specs.append(pl.BlockSpec((tm, K), lambda i, c: (i, 0)))
        args.append(res)
    return pl.pallas_call(
        body, name=name, grid=(T // tm, nc), in_specs=in_specs,
        out_specs=pl.BlockSpec((tm, K), lambda i, c: (i, 0)),
        out_shape=jax.ShapeDtypeStruct((T, K), out_dtype),
        scratch_shapes=[pltpu.VMEM((tm, K), F32)], compiler_params=_cp("arbitrary", "arbitrary"))(*args)


def mm_tn(a, b, *, out_dtype, tk, tn, tt, name, colsum=False):
    T, K = a.shape
    N = b.shape[1]
    tk, tn, tt = min(tk, K), min(tn, N), min(tt, T)
    nt = T // tt

    def body(*refs):
        if colsum:
            a_ref, b_ref, o_ref, cs_ref, acc_ref = refs
        else:
            a_ref, b_ref, o_ref, acc_ref = refs
        i, t = pl.program_id(1), pl.program_id(2)
        bb = b_ref[...]
        part = lax.dot_general(a_ref[...].astype(BF), bb.astype(BF), (((0,), (0,)), ((), ())),
                               preferred_element_type=F32)

        @pl.when(t == 0)
        def _():
            acc_ref[...] = part

        @pl.when(t > 0)
        def _():
            acc_ref[...] += part

        @pl.when(t == nt - 1)
        def _():
            o_ref[...] = acc_ref[...].astype(o_ref.dtype)

        if colsum:
            s = jnp.sum(bb.astype(F32), axis=0, keepdims=True)

            @pl.when((i == 0) & (t == 0))
            def _():
                cs_ref[...] = s

            @pl.when((i == 0) & (t > 0))
            def _():
                cs_ref[...] += s

    out_specs = [pl.BlockSpec((tk, tn), lambda j, i, t: (i, j))]
    out_shape = [jax.ShapeDtypeStruct((K, N), out_dtype)]
    if colsum:
        out_specs.append(pl.BlockSpec((1, tn), lambda j, i, t: (0, j)))
        out_shape.append(jax.ShapeDtypeStruct((1, N), F32))
    res = pl.pallas_call(
        body, name=name, grid=(N // tn, K // tk, nt),
        in_specs=[pl.BlockSpec((tt, tk), lambda j, i, t: (t, i)), pl.BlockSpec((tt, tn), lambda j, i, t: (t, j))],
        out_specs=out_specs, out_shape=out_shape, scratch_shapes=[pltpu.VMEM((tk, tn), F32)],
        compiler_params=_cp("arbitrary", "arbitrary", "arbitrary"))(a, b)
    return res if colsum else res[0]


def mm_res_ln(a, w, res, g, b, *, tm, name):
    T, K = a.shape
    tm = min(tm, T)

    def body(a_ref, w_ref, r_ref, g_ref, b_ref, y_ref, yb_ref, xh_ref, rs_ref):
        pre = ALPHA * r_ref[...] + jnp.dot(a_ref[...].astype(BF), w_ref[...], preferred_element_type=F32)
        mu = jnp.mean(pre, axis=-1, keepdims=True)
        cen = pre - mu
        var = jnp.mean(cen * cen, axis=-1, keepdims=True)
        rstd = lax.rsqrt(var + LN_EPS)
        xhat = cen * rstd
        y = xhat * g_ref[...] + b_ref[...]
        y_ref[...] = y
        yb_ref[...] = y.astype(BF)
        xh_ref[...] = xhat
        rs_ref[...] = rstd

    row = lambda i: (i, 0)
    fix = lambda i: (0, 0)
    return pl.pallas_call(
        body, name=name, grid=(T // tm,),
        in_specs=[pl.BlockSpec((tm, K), row), pl.BlockSpec((K, D), fix), pl.BlockSpec((tm, D), row),
                  pl.BlockSpec((1, D), fix), pl.BlockSpec((1, D), fix)],
        out_specs=[pl.BlockSpec((tm, D), row), pl.BlockSpec((tm, D), row), pl.BlockSpec((tm, D), row),
                   pl.BlockSpec((tm, 1), row)],
        out_shape=[jax.ShapeDtypeStruct((T, D), F32), jax.ShapeDtypeStruct((T, D), BF),
                   jax.ShapeDtypeStruct((T, D), F32), jax.ShapeDtypeStruct((T, 1), F32)],
        compiler_params=_cp("arbitrary"))(a, w, res, g, b)


def ln_bwd(dy, xhat, rstd, g, *, tm, name, loss_from=None):
    T = xhat.shape[0]
    tm = min(tm, T)
    with_loss = loss_from is not None

    def body(*refs):
        if with_loss:
            xh_ref, rs_ref, g_ref, b_ref, t_ref, dp_ref, dpb_ref, dg_ref, db_ref, ls_ref = refs
        else:
            dy_ref, xh_ref, rs_ref, g_ref, dp_ref, dpb_ref, dg_ref, db_ref = refs
        i = pl.program_id(0)
        xhat_ = xh_ref[...]
        gg = g_ref[...]
        if with_loss:
            err = xhat_ * gg + b_ref[...] - t_ref[...]
            dyv = err * (1.0 / D)
            lpart = 0.5 * jnp.sum(jnp.sum(err * err, axis=-1, keepdims=True) * (1.0 / D))
        else:
            dyv = dy_ref[...]
        dxh = dyv * gg
        m1 = jnp.mean(dxh, axis=-1, keepdims=True)
        m2 = jnp.mean(dxh * xhat_, axis=-1, keepdims=True)
        dpre = rs_ref[...] * (dxh - m1 - xhat_ * m2)
        dp_ref[...] = dpre
        dpb_ref[...] = dpre.astype(BF)
        dgp = jnp.sum(dyv * xhat_, axis=0, keepdims=True)
        dbp = jnp.sum(dyv, axis=0, keepdims=True)

        @pl.when(i == 0)
        def _():
            dg_ref[...] = dgp
            db_ref[...] = dbp
            if with_loss:
                ls_ref[...] = jnp.full((8, 128), lpart, F32)

        @pl.when(i > 0)
        def _():
            dg_ref[...] += dgp
            db_ref[...] += dbp
            if with_loss:
                ls_ref[...] += jnp.full((8, 128), lpart, F32)

    row = lambda i: (i, 0)
    fix = lambda i: (0, 0)
    if with_loss:
        in_specs = [pl.BlockSpec((tm, D), row), pl.BlockSpec((tm, 1), row), pl.BlockSpec((1, D), fix),
                    pl.BlockSpec((1, D), fix), pl.BlockSpec((tm, D), row)]
        args = [xhat, rstd, g, loss_from[0], loss_from[1]]
    else:
        in_specs = [pl.BlockSpec((tm, D), row), pl.BlockSpec((tm, D), row), pl.BlockSpec((tm, 1), row),
                    pl.BlockSpec((1, D), fix)]
        args = [dy, xhat, rstd, g]
    out_specs = [pl.BlockSpec((tm, D), row), pl.BlockSpec((tm, D), row), pl.BlockSpec((1, D), fix),
                 pl.BlockSpec((1, D), fix)]
    out_shape = [jax.ShapeDtypeStruct((T, D), F32), jax.ShapeDtypeStruct((T, D), BF),
                 jax.ShapeDtypeStruct((1, D), F32), jax.ShapeDtypeStruct((1, D), F32)]
    if with_loss:
        out_specs.append(pl.BlockSpec((8, 128), fix))
        out_shape.append(jax.ShapeDtypeStruct((8, 128), F32))
    return pl.pallas_call(body, name=name, grid=(T // tm,), in_specs=in_specs, out_specs=out_specs,
                          out_shape=out_shape, compiler_params=_cp("arbitrary"))(*args)


def _prev_halo(tm, blk):
    return lambda i: (jnp.maximum(i * (tm // HALO) - 1, 0), blk)


def _next_halo(tm, T, blk):
    return lambda i: (jnp.minimum((i + 1) * (tm // HALO), T // HALO - 1), blk)


def _pool_p(ext, t, g):
    e = ext[:, g * POOL_GD:(g + 1) * POOL_GD]
    s = e
    for sh in (1, 2, 4, 8)[:g + 1]:
        s = s + pltpu.roll(s, sh, axis=0)
    inv = 1.0 / jnp.minimum(t + 1, POOL_WINDOWS[g]).astype(F32)
    return s[HALO:] * inv - e[HALO:]


def pool_fwd(z, pw, *, tm, name):
    T = z.shape[0]
    tm = min(tm, T)

    def body(zm_ref, zh_ref, pw_ref, o_ref):
        i = pl.program_id(0)
        keep = jnp.where(i == 0, 0.0, 1.0).astype(F32)
        ext = jnp.concatenate([zh_ref[...].astype(F32) * keep, zm_ref[...].astype(F32)], axis=0)
        t = i * tm + lax.broadcasted_iota(jnp.int32, (tm, 1), 0)
        outs = [jnp.dot(_pool_p(ext, t, g).astype(BF), pw_ref[g], preferred_element_type=F32) for g in range(4)]
        o_ref[...] = jnp.concatenate(outs, axis=1).astype(o_ref.dtype)

    return pl.pallas_call(
        body, name=name, grid=(T // tm,),
        in_specs=[pl.BlockSpec((tm, D), lambda i: (i, ZB_POOL)), pl.BlockSpec((HALO, D), _prev_halo(tm, ZB_POOL)),
                  pl.BlockSpec((4, POOL_GD, POOL_GD), lambda i: (0, 0, 0))],
        out_specs=pl.BlockSpec((tm, D), lambda i: (i, 0)),
        out_shape=jax.ShapeDtypeStruct((T, D), BF), compiler_params=_cp("arbitrary"))(z, z, pw)


def pool_bwd(dz, dyp, yp_pre, z, pw, ps, *, tm, name):
    T = z.shape[0]
    tm = min(tm, T)
    nt = T // tm

    def body(dz_in, dy_ref, dyn_ref, yp_ref, zm_ref, zh_ref, pw_ref, ps_ref, dz_ref, dpw_ref, dps_ref):
        del dz_in
        i = pl.program_id(0)
        keep_p = jnp.where(i == 0, 0.0, 1.0).astype(F32)
        keep_n = jnp.where(i == nt - 1, 0.0, 1.0).astype(F32)
        ext = jnp.concatenate([zh_ref[...].astype(F32) * keep_p, zm_ref[...].astype(F32)], axis=0)
        t = i * tm + lax.broadcasted_iota(jnp.int32, (tm, 1), 0)
        psv = ps_ref[...]
        dy = dy_ref[...].astype(F32)
        dyp_ext = jnp.concatenate([dy, dyn_ref[...].astype(F32) * keep_n], axis=0) * psv
        t_ext = i * tm + lax.broadcasted_iota(jnp.int32, (tm + HALO, 1), 0)
        dps = jnp.sum(dy * yp_ref[...].astype(F32), axis=0, keepdims=True)
        dzs, dpws = [], []
        for g in range(4):
            sl = slice(g * POOL_GD, (g + 1) * POOL_GD)
            dyg = dyp_ext[:, sl].astype(BF)
            dp = lax.dot_general(dyg, pw_ref[g], (((1,), (1,)), ((), ())), preferred_element_type=F32)
            q = dp * (1.0 / jnp.minimum(t_ext + 1, POOL_WINDOWS[g]).astype(F32))
            s = q
            for sh in (1, 2, 4, 8)[:g + 1]:
                s = s + pltpu.roll(s, tm + HALO - sh, axis=0)
            dzs.append(s[:tm] - dp[:tm])
            p = _pool_p(ext, t, g).astype(BF)
            dpws.append(lax.dot_general(p, dyg[:tm], (((0,), (0,)), ((), ())), preferred_element_type=F32))
        dz_ref[...] = jnp.concatenate(dzs, axis=1).astype(dz_ref.dtype)

        @pl.when(i == 0)
        def _():
            for g in range(4):
                dpw_ref[g] = dpws[g]
            dps_ref[...] = dps

        @pl.when(i > 0)
        def _():
            for g in range(4):
                dpw_ref[g] += dpws[g]
            dps_ref[...] += dps

    row = lambda i: (i, 0)
    return pl.pallas_call(
        body, name=name, grid=(nt,),
        in_specs=[pl.BlockSpec(memory_space=pl.ANY),
                  pl.BlockSpec((tm, D), row), pl.BlockSpec((HALO, D), _next_halo(tm, T, 0)),
                  pl.BlockSpec((tm, D), row),
                  pl.BlockSpec((tm, D), lambda i: (i, ZB_POOL)), pl.BlockSpec((HALO, D), _prev_halo(tm, ZB_POOL)),
                  pl.BlockSpec((4, POOL_GD, POOL_GD), lambda i: (0, 0, 0)), pl.BlockSpec((1, D), lambda i: (0, 0))],
        out_specs=[pl.BlockSpec((tm, D), lambda i: (i, ZB_POOL)),
                   pl.BlockSpec((4, POOL_GD, POOL_GD), lambda i: (0, 0, 0)), pl.BlockSpec((1, D), lambda i: (0, 0))],
        out_shape=[jax.ShapeDtypeStruct(dz.shape, dz.dtype), jax.ShapeDtypeStruct((4, POOL_GD, POOL_GD), F32),
                   jax.ShapeDtypeStruct((1, D), F32)],
        input_output_aliases={0: 0}, compiler_params=_cp("arbitrary"))(dz, dyp, dyp, yp_pre, z, z, pw, ps)


def _lru_gates(zl_ext, cw, cb, wr_ref, br, wi_ref, bi, lam):
    shifted = []
    v = cb
    for k in range(4):
        zs = zl_ext if k == 3 else pltpu.roll(zl_ext, 3 - k, axis=0)
        zs = zs[HALO:]
        shifted.append(zs)
        v = v + cw[k:k + 1, :] * zs
    vb = v.astype(BF)
    rp, ip = [], []
    for h in range(LRU_HEADS):
        sl = slice(h * LRU_HD, (h + 1) * LRU_HD)
        rp.append(jnp.dot(vb[:, sl], wr_ref[h], preferred_element_type=F32))
        ip.append(jnp.dot(vb[:, sl], wi_ref[h], preferred_element_type=F32))
    r = _sigmoid(jnp.concatenate(rp, axis=1) + br)
    ig = _sigmoid(jnp.concatenate(ip, axis=1) + bi)
    sp = jnp.maximum(-lam, 0.0) + jnp.log(1.0 + jnp.exp(-jnp.abs(lam)))
    a = jnp.exp(-LRU_C * r * sp)
    mult = jnp.sqrt(1.0 - a * a)
    return v, vb, r, ig, a, mult, sp, shifted


def lru_fwd(z, cw, cb, wr, br, wi, bi, lam, wlo, *, tm, name):
    T = z.shape[0]
    tm = min(tm, T)
    nch = tm // 8

    def body(zm_ref, zh_ref, cw_ref, cb_ref, wr_ref, br_ref, wi_ref, bi_ref, lam_ref, wlo_ref, h_ref, y_ref,
             a_s, b_s, carry):
        i = pl.program_id(0)

        @pl.when(i == 0)
        def _():
            carry[...] = jnp.zeros_like(carry)

        keep = jnp.where(i == 0, 0.0, 1.0).astype(F32)
        ext = jnp.concatenate([zh_ref[...].astype(F32) * keep, zm_ref[...].astype(F32)], axis=0)
        v, _, _, ig, a, mult, _, _ = _lru_gates(ext, cw_ref[...], cb_ref[...], wr_ref, br_ref[...], wi_ref,
                                                bi_ref[...], lam_ref[...])
        a_s[...] = a
        b_s[...] = mult * (ig * v)
        row = lax.broadcasted_iota(jnp.int32, (8, D), 0)

        def step(ci, hprev):
            sl = pl.ds(pl.multiple_of(ci * 8, 8), 8)
            aa, bb = a_s[sl, :], b_s[sl, :]
            for s in (1, 2, 4):
                m = row >= s
                bb = bb + aa * jnp.where(m, pltpu.roll(bb, s, axis=0), 0.0)
                aa = aa * jnp.where(m, pltpu.roll(aa, s, axis=0), 1.0)
            h = bb + aa * hprev
            h_ref[sl, :] = h
            return jnp.broadcast_to(h[7:8, :], (8, D))

        carry[...] = lax.fori_loop(0, nch, step, carry[...])
        y_ref[...] = jnp.dot(h_ref[...].astype(BF), wlo_ref[...], preferred_element_type=F32).astype(BF)

    fix2 = lambda i: (0, 0)
    fix3 = lambda i: (0, 0, 0)
    return pl.pallas_call(
        body, name=name, grid=(T // tm,),
        in_specs=[pl.BlockSpec((tm, D), lambda i: (i, ZB_LRU)), pl.BlockSpec((HALO, D), _prev_halo(tm, ZB_LRU)),
                  pl.BlockSpec((4, D), fix2), pl.BlockSpec((1, D), fix2),
                  pl.BlockSpec((LRU_HEADS, LRU_HD, LRU_HD), fix3), pl.BlockSpec((1, D), fix2),
                  pl.BlockSpec((LRU_HEADS, LRU_HD, LRU_HD), fix3), pl.BlockSpec((1, D), fix2),
                  pl.BlockSpec((1, D), fix2), pl.BlockSpec((D, D), fix2)],
        out_specs=[pl.BlockSpec((tm, D), lambda i: (i, 0)), pl.BlockSpec((tm, D), lambda i: (i, 0))],
        out_shape=[jax.ShapeDtypeStruct((T, D), F32), jax.ShapeDtypeStruct((T, D), BF)],
        scratch_shapes=[pltpu.VMEM((tm, D), F32), pltpu.VMEM((tm, D), F32), pltpu.VMEM((8, D), F32)],
        compiler_params=_cp("arbitrary"))(z, z, cw, cb, wr, br, wi, bi, lam, wlo)


def lru_bwd(dz, dyl, z, h, cw, cb, wr, br, wi, bi, lam, wlo, *, tm, name):
    T = z.shape[0]
    tm = min(tm, T)
    nt = T // tm
    nch = tm // 8

    def body(dz_in, dy_ref, zm_ref, zh_ref, h_ref, hh_ref, cw_ref, cb_ref, wr_ref, br_ref, wi_ref, bi_ref, lam_ref,
             wlo_ref, dz_ref, dcw_ref, dcb_ref, dwr_ref, dbr_ref, dwi_ref, dbi_ref, dlam_ref,
             c_s, g_s, dh_s, dh_carry, a_carry, dv_carry):
        del dz_in
        i = pl.program_id(0)
        ti = nt - 1 - i

        @pl.when(i == 0)
        def _():
            dh_carry[...] = jnp.zeros_like(dh_carry)
            a_carry[...] = jnp.zeros_like(a_carry)
            dv_carry[...] = jnp.zeros_like(dv_carry)

        keep = jnp.where(ti == 0, 0.0, 1.0).astype(F32)
        ext = jnp.concatenate([zh_ref[...].astype(F32) * keep, zm_ref[...].astype(F32)], axis=0)
        cw_ = cw_ref[...]
        lam_ = lam_ref[...]
        v, vb, r, ig, a, mult, sp, shifted = _lru_gates(ext, cw_, cb_ref[...], wr_ref, br_ref[...], wi_ref,
                                                         bi_ref[...], lam_)
        rowt = lax.broadcasted_iota(jnp.int32, (tm, 1), 0)
        c_s[...] = jnp.where(rowt == tm - 1, a_carry[0:1, :], pltpu.roll(a, tm - 1, axis=0))
        g_s[...] = lax.dot_general(dy_ref[...], wlo_ref[...], (((1,), (1,)), ((), ())), preferred_element_type=F32)
        row = lax.broadcasted_iota(jnp.int32, (8, D), 0)

        def step(k, nxt):
            ci = nch - 1 - k
            sl = pl.ds(pl.multiple_of(ci * 8, 8), 8)
            cc, gg = c_s[sl, :], g_s[sl, :]
            for s in (1, 2, 4):
                m = row < 8 - s
                gg = gg + cc * jnp.where(m, pltpu.roll(gg, 8 - s, axis=0), 0.0)
                cc = cc * jnp.where(m, pltpu.roll(cc, 8 - s, axis=0), 1.0)
            dh = gg + cc * nxt
            dh_s[sl, :] = dh
            return jnp.broadcast_to(dh[0:1, :], (8, D))

        dh_carry[...] = lax.fori_loop(0, nch, step, dh_carry[...])
        a_carry[...] = jnp.broadcast_to(a[0:1, :], (8, D))
        dh = dh_s[...]
        hv = h_ref[...]
        hprev = jnp.where(rowt == 0, hh_ref[7:8, :] * keep, pltpu.roll(hv, 1, axis=0))
        iv = ig * v
        da = dh * hprev
        dmult = dh * iv
        div = dh * mult
        dlog = da * a - dmult * (a * a) / mult
        dr = dlog * (-LRU_C * sp)
        dlam = jnp.sum(dlog * r, axis=0, keepdims=True) * (LRU_C * _sigmoid(-lam_))
        di = div * v
        dv = div * ig
        drp = dr * r * (1.0 - r)
        dip = di * ig * (1.0 - ig)
        drb, dib = drp.astype(BF), dip.astype(BF)
        dvh, dwr, dwi = [], [], []
        nt_dims = (((1,), (1,)), ((), ()))
        tn_dims = (((0,), (0,)), ((), ()))
        for hd in range(LRU_HEADS):
            sl = slice(hd * LRU_HD, (hd + 1) * LRU_HD)
            dvh.append(lax.dot_general(drb[:, sl], wr_ref[hd], nt_dims, preferred_element_type=F32)
                       + lax.dot_general(dib[:, sl], wi_ref[hd], nt_dims, preferred_element_type=F32))
            dwr.append(lax.dot_general(vb[:, sl], drb[:, sl], tn_dims, preferred_element_type=F32))
            dwi.append(lax.dot_general(vb[:, sl], dib[:, sl], tn_dims, preferred_element_type=F32))
        dv = dv + jnp.concatenate(dvh, axis=1)
        dv_ext = jnp.concatenate([dv, dv_carry[...]], axis=0)
        dzl = cw_[3:4, :] * dv
        for k in range(3):
            dzl = dzl + cw_[k:k + 1, :] * pltpu.roll(dv_ext, tm + HALO - (3 - k), axis=0)[:tm]
        dz_ref[...] = dzl.astype(dz_ref.dtype)
        dv_carry[...] = dv[:HALO]
        dcw = jnp.concatenate([jnp.sum(dv * shifted[k], axis=0, keepdims=True) for k in range(4)], axis=0)
        dcb = jnp.sum(dv, axis=0, keepdims=True)
        dbr = jnp.sum(drp, axis=0, keepdims=True)
        dbi = jnp.sum(dip, axis=0, keepdims=True)

        @pl.when(i == 0)
        def _():
            dcw_ref[...] = dcw
            dcb_ref[...] = dcb
            dbr_ref[...] = dbr
            dbi_ref[...] = dbi
            dlam_ref[...] = dlam
            for hd in range(LRU_HEADS):
                dwr_ref[hd] = dwr[hd]
                dwi_ref[hd] = dwi[hd]

        @pl.when(i > 0)
        def _():
            dcw_ref[...] += dcw
            dcb_ref[...] += dcb
            dbr_ref[...] += dbr
            dbi_ref[...] += dbi
            dlam_ref[...] += dlam
            for hd in range(LRU_HEADS):
                dwr_ref[hd] += dwr[hd]
                dwi_ref[hd] += dwi[hd]

    fix2 = lambda i: (0, 0)
    fix3 = lambda i: (0, 0, 0)
    rev = lambda i: (nt - 1 - i, 0)
    vec = pl.BlockSpec((1, D), fix2)
    hw = pl.BlockSpec((LRU_HEADS, LRU_HD, LRU_HD), fix3)
    return pl.pallas_call(
        body, name=name, grid=(nt,),
        in_specs=[pl.BlockSpec(memory_space=pl.ANY),
                  pl.BlockSpec((tm, D), rev),
                  pl.BlockSpec((tm, D), lambda i: (nt - 1 - i, ZB_LRU)),
                  pl.BlockSpec((HALO, D), lambda i: (jnp.maximum((nt - 1 - i) * (tm // HALO) - 1, 0), ZB_LRU)),
                  pl.BlockSpec((tm, D), rev),
                  pl.BlockSpec((8, D), lambda i: (jnp.maximum((nt - 1 - i) * (tm // 8) - 1, 0), 0)),
                  pl.BlockSpec((4, D), fix2), vec, hw, vec, hw, vec, vec, pl.BlockSpec((D, D), fix2)],
        out_specs=[pl.BlockSpec((tm, D), lambda i: (nt - 1 - i, ZB_LRU)),
                   pl.BlockSpec((4, D), fix2), vec, hw, vec, hw, vec, vec],
        out_shape=[jax.ShapeDtypeStruct(dz.shape, dz.dtype), jax.ShapeDtypeStruct((4, D), F32),
                   jax.ShapeDtypeStruct((1, D), F32), jax.ShapeDtypeStruct((LRU_HEADS, LRU_HD, LRU_HD), F32),
                   jax.ShapeDtypeStruct((1, D), F32), jax.ShapeDtypeStruct((LRU_HEADS, LRU_HD, LRU_HD), F32),
                   jax.ShapeDtypeStruct((1, D), F32), jax.ShapeDtypeStruct((1, D), F32)],
        scratch_shapes=[pltpu.VMEM((tm, D), F32), pltpu.VMEM((tm, D), F32), pltpu.VMEM((tm, D), F32),
                        pltpu.VMEM((8, D), F32), pltpu.VMEM((8, D), F32), pltpu.VMEM((HALO, D), F32)],
        input_output_aliases={0: 0},
        compiler_params=_cp("arbitrary"))(dz, dyl, z, z, h, h, cw, cb, wr, br, wi, bi, lam, wlo)


def _sconv_cv(u_ext, sw):
    shifted = []
    cv = None
    for k in range(3):
        us = (u_ext if k == 2 else pltpu.roll(u_ext, 2 - k, axis=0))[HALO:]
        shifted.append(us)
        term = sw[k:k + 1, :] * us
        cv = term if cv is None else cv + term
    return cv, shifted


def sconv_fwd(z, sw, wso, *, tm, name):
    T = z.shape[0]
    tm = min(tm, T)

    def body(zm_ref, zh_ref, sw_ref, wso_ref, s_ref, y_ref):
        i = pl.program_id(0)
        keep = jnp.where(i == 0, 0.0, 1.0).astype(F32)
        zm = zm_ref[...].astype(F32)
        zh = zh_ref[...].astype(F32)
        u_ext = jnp.concatenate([zh[:, D:2 * D] * zh[:, 2 * D:] * keep, zm[:, D:2 * D] * zm[:, 2 * D:]], axis=0)
        cv, _ = _sconv_cv(u_ext, sw_ref[...])
        s = (zm[:, :D] * cv).astype(BF)
        s_ref[...] = s
        y_ref[...] = jnp.dot(s, wso_ref[...], preferred_element_type=F32).astype(BF)

    return pl.pallas_call(
        body, name=name, grid=(T // tm,),
        in_specs=[pl.BlockSpec((tm, 3 * D), lambda i: (i, ZB_SCONV)),
                  pl.BlockSpec((HALO, 3 * D), _prev_halo(tm, ZB_SCONV)),
                  pl.BlockSpec((3, D), lambda i: (0, 0)), pl.BlockSpec((D, D), lambda i: (0, 0))],
        out_specs=[pl.BlockSpec((tm, D), lambda i: (i, 0)), pl.BlockSpec((tm, D), lambda i: (i, 0))],
        out_shape=[jax.ShapeDtypeStruct((T, D), BF), jax.ShapeDtypeStruct((T, D), BF)],
        compiler_params=_cp("arbitrary"))(z, z, sw, wso)


def sconv_bwd(dz, dyc, z, sw, wso, *, tm, name):
    T = z.shape[0]
    tm = min(tm, T)
    nt = T // tm

    def body(dz_in, dy_ref, dyn_ref, zm_ref, zp_ref, zn_ref, sw_ref, wso_ref, dz_ref, dsw_ref):
        del dz_in
        i = pl.program_id(0)
        keep_p = jnp.where(i == 0, 0.0, 1.0).astype(F32)
        keep_n = jnp.where(i == nt - 1, 0.0, 1.0).astype(F32)
        sw_ = sw_ref[...]
        zm = zm_ref[...].astype(F32)
        zp = zp_ref[...].astype(F32)
        zb, zc, zh = zm[:, :D], zm[:, D:2 * D], zm[:, 2 * D:]
        u_ext = jnp.concatenate([zp[:, D:2 * D] * zp[:, 2 * D:] * keep_p, zc * zh], axis=0)
        cv, shifted = _sconv_cv(u_ext, sw_)
        dy_ext = jnp.concatenate([dy_ref[...], dyn_ref[...]], axis=0)
        ds_ext = lax.dot_general(dy_ext, wso_ref[...], (((1,), (1,)), ((), ())), preferred_element_type=F32)
        zb_ext = jnp.concatenate([zb, zn_ref[...][:, :D].astype(F32) * keep_n], axis=0)
        dcv_ext = ds_ext * zb_ext
        du = sw_[2:3, :] * dcv_ext[:tm]
        for k in range(2):
            du = du + sw_[k:k + 1, :] * pltpu.roll(dcv_ext, tm + HALO - (2 - k), axis=0)[:tm]
        dz_ref[...] = jnp.concatenate([ds_ext[:tm] * cv, du * zh, du * zc], axis=1).astype(dz_ref.dtype)
        dcv = dcv_ext[:tm]
        dsw = jnp.concatenate([jnp.sum(dcv * shifted[k], axis=0, keepdims=True) for k in range(3)], axis=0)

        @pl.when(i == 0)
        def _():
            dsw_ref[...] = dsw

        @pl.when(i > 0)
        def _():
            dsw_ref[...] += dsw

    return pl.pallas_call(
        body, name=name, grid=(nt,),
        in_specs=[pl.BlockSpec(memory_space=pl.ANY),
                  pl.BlockSpec((tm, D), lambda i: (i, 0)), pl.BlockSpec((HALO, D), _next_halo(tm, T, 0)),
                  pl.BlockSpec((tm, 3 * D), lambda i: (i, ZB_SCONV)),
                  pl.BlockSpec((HALO, 3 * D), _prev_halo(tm, ZB_SCONV)),
                  pl.BlockSpec((HALO, 3 * D), _next_halo(tm, T, ZB_SCONV)),
                  pl.BlockSpec((3, D), lambda i: (0, 0)), pl.BlockSpec((D, D), lambda i: (0, 0))],
        out_specs=[pl.BlockSpec((tm, 3 * D), lambda i: (i, ZB_SCONV)), pl.BlockSpec((3, D), lambda i: (0, 0))],
        out_shape=[jax.ShapeDtypeStruct(dz.shape, dz.dtype), jax.ShapeDtypeStruct((3, D), F32)],
        input_output_aliases={0: 0}, compiler_params=_cp("arbitrary"))(dz, dyc, dyc, z, z, z, sw, wso)


def merge_fwd(z, yp_pre, yl, yc, ps, *, tm, name):
    T = z.shape[0]
    tm = min(tm, T)

    def body(zg_ref, yp_ref, yl_ref, yc_ref, ps_ref, o_ref):
        gts = _sigmoid(zg_ref[...].astype(F32))
        m = (gts[:, :D] * (yp_ref[...].astype(F32) * ps_ref[...]) + gts[:, D:2 * D] * yl_ref[...].astype(F32)
             + gts[:, 2 * D:] * yc_ref[...].astype(F32))
        o_ref[...] = m.astype(o_ref.dtype)

    row = lambda i: (i, 0)
    return pl.pallas_call(
        body, name=name, grid=(T // tm,),
        in_specs=[pl.BlockSpec((tm, 3 * D), lambda i: (i, ZB_GATE)), pl.BlockSpec((tm, D), row),
                  pl.BlockSpec((tm, D), row), pl.BlockSpec((tm, D), row), pl.BlockSpec((1, D), lambda i: (0, 0))],
        out_specs=pl.BlockSpec((tm, D), row), out_shape=jax.ShapeDtypeStruct((T, D), BF),
        compiler_params=_cp("arbitrary"))(z, yp_pre, yl, yc, ps)


def merge_bwd(dm, z, yp_pre, yl, yc, ps, *, tm, name):
    T = z.shape[0]
    tm = min(tm, T)

    def body(dm_ref, zg_ref, yp_ref, yl_ref, yc_ref, ps_ref, dz_ref, dyp_ref, dyl_ref, dyc_ref):
        gts = _sigmoid(zg_ref[...].astype(F32))
        dmv = dm_ref[...].astype(F32)
        ys = (yp_ref[...].astype(F32) * ps_ref[...], yl_ref[...].astype(F32), yc_ref[...].astype(F32))
        outs = (dyp_ref, dyl_ref, dyc_ref)
        dgs = []
        for j in range(3):
            gj = gts[:, j * D:(j + 1) * D]
            outs[j][...] = (dmv * gj).astype(BF)
            dgs.append(dmv * ys[j] * gj * (1.0 - gj))
        dz_ref[...] = jnp.concatenate(dgs, axis=1).astype(dz_ref.dtype)

    row = lambda i: (i, 0)
    return pl.pallas_call(
        body, name=name, grid=(T // tm,),
        in_specs=[pl.BlockSpec((tm, D), row), pl.BlockSpec((tm, 3 * D), lambda i: (i, ZB_GATE)),
                  pl.BlockSpec((tm, D), row), pl.BlockSpec((tm, D), row), pl.BlockSpec((tm, D), row),
                  pl.BlockSpec((1, D), lambda i: (0, 0))],
        out_specs=[pl.BlockSpec((tm, 3 * D), lambda i: (i, ZB_GATE)), pl.BlockSpec((tm, D), row),
                   pl.BlockSpec((tm, D), row), pl.BlockSpec((tm, D), row)],
        out_shape=[jax.ShapeDtypeStruct((T, IN_COLS), BF), jax.ShapeDtypeStruct((T, D), BF),
                   jax.ShapeDtypeStruct((T, D), BF), jax.ShapeDtypeStruct((T, D), BF)],
        compiler_params=_cp("arbitrary"))(dm, z, yp_pre, yl, yc, ps)


def _attn_probs(qh, kh):
    s = lax.dot_general(qh, kh, (((1,), (1,)), ((), ())), preferred_element_type=F32) * (X_HD ** -0.5)
    e = jnp.exp(s - jnp.max(s, axis=-1, keepdims=True))
    return e / jnp.sum(e, axis=-1, keepdims=True)


def attn_fwd(xb, wq, kb, vb, *, tm, name):
    T = xb.shape[0]
    tm = min(tm, T)

    def body(x_ref, wq_ref, k_ref, v_ref, q_ref, o_ref):
        q = jnp.dot(x_ref[...], wq_ref[...], preferred_element_type=F32).astype(BF)
        q_ref[...] = q
        outs = []
        for h in range(X_HEADS):
            sl = slice(h * X_HD, (h + 1) * X_HD)
            p = _attn_probs(q[:, sl], k_ref[:, sl])
            outs.append(jnp.dot(p.astype(BF), v_ref[:, sl], preferred_element_type=F32))
        o_ref[...] = jnp.concatenate(outs, axis=1).astype(BF)

    row = lambda i: (i, 0)
    fix = lambda i: (0, 0)
    return pl.pallas_call(
        body, name=name, grid=(T // tm,),
        in_specs=[pl.BlockSpec((tm, D), row), pl.BlockSpec((D, D), fix), pl.BlockSpec((N_MEM, D), fix),
                  pl.BlockSpec((N_MEM, D), fix)],
        out_specs=[pl.BlockSpec((tm, D), row), pl.BlockSpec((tm, D), row)],
        out_shape=[jax.ShapeDtypeStruct((T, D), BF), jax.ShapeDtypeStruct((T, D), BF)],
        compiler_params=_cp("arbitrary"))(xb, wq, kb, vb)


def attn_bwd(dxa, wo, q, kb, vb, *, tm, name):
    T = q.shape[0]
    tm = min(tm, T)

    def body(d_ref, wo_ref, q_ref, k_ref, v_ref, dq_ref, dk_ref, dv_ref):
        i = pl.program_id(0)
        do = lax.dot_general(d_ref[...], wo_ref[...], (((1,), (1,)), ((), ())),
                             preferred_element_type=F32).astype(BF)
        q = q_ref[...]
        dqs, dks, dvs = [], [], []
        for h in range(X_HEADS):
            sl = slice(h * X_HD, (h + 1) * X_HD)
            kh, vh = k_ref[:, sl], v_ref[:, sl]
            p = _attn_probs(q[:, sl], kh)
            dp = lax.dot_general(do[:, sl], vh, (((1,), (1,)), ((), ())), preferred_element_type=F32)
            ds = (p * (dp - jnp.sum(dp * p, axis=-1, keepdims=True)) * (X_HD ** -0.5)).astype(BF)
            dqs.append(jnp.dot(ds, kh, preferred_element_type=F32))
            dks.append(lax.dot_general(ds, q[:, sl], (((0,), (0,)), ((), ())), preferred_element_type=F32))
            dvs.append(lax.dot_general(p.astype(BF), do[:, sl], (((0,), (0,)), ((), ())),
                                       preferred_element_type=F32))
        dq_ref[...] = jnp.concatenate(dqs, axis=1).astype(BF)
        dk = jnp.concatenate(dks, axis=1)
        dv = jnp.concatenate(dvs, axis=1)

        @pl.when(i == 0)
        def _():
            dk_ref[...] = dk
            dv_ref[...] = dv

        @pl.when(i > 0)
        def _():
            dk_ref[...] += dk
            dv_ref[...] += dv

    row = lambda i: (i, 0)
    fix = lambda i: (0, 0)
    return pl.pallas_call(
        body, name=name, grid=(T // tm,),
        in_specs=[pl.BlockSpec((tm, D), row), pl.BlockSpec((D, D), fix), pl.BlockSpec((tm, D), row),
                  pl.BlockSpec((N_MEM, D), fix), pl.BlockSpec((N_MEM, D), fix)],
        out_specs=[pl.BlockSpec((tm, D), row), pl.BlockSpec((N_MEM, D), fix), pl.BlockSpec((N_MEM, D), fix)],
        out_shape=[jax.ShapeDtypeStruct((T, D), BF), jax.ShapeDtypeStruct((N_MEM, D), F32),
                   jax.ShapeDtypeStruct((N_MEM, D), F32)],
        compiler_params=_cp("arbitrary"))(dxa, wo, q, kb, vb)


def swiglu_fwd(gu, *, tm, name):
    T = gu.shape[0]
    tm = min(tm, T)

    def body(g_ref, u_ref, o_ref):
        g = g_ref[...].astype(F32)
        o_ref[...] = (g * _sigmoid(g) * u_ref[...].astype(F32)).astype(BF)

    return pl.pallas_call(
        body, name=name, grid=(T // tm,),
        in_specs=[pl.BlockSpec((tm, D_FF), lambda i: (i, 0)), pl.BlockSpec((tm, D_FF), lambda i: (i, 1))],
        out_specs=pl.BlockSpec((tm, D_FF), lambda i: (i, 0)), out_shape=jax.ShapeDtypeStruct((T, D_FF), BF),
        compiler_params=_cp("arbitrary"))(gu, gu)


def swiglu_bwd(dh, gu, *, tm, name):
    T = gu.shape[0]
    tm = min(tm, T)

    def body(dh_ref, g_ref, u_ref, o_ref):
        g = g_ref[...].astype(F32)
        u = u_ref[...].astype(F32)
        dhv = dh_ref[...].astype(F32)
        sg = _sigmoid(g)
        o_ref[:, :D_FF] = (dhv * u * sg * (1.0 + g * (1.0 - sg))).astype(BF)
        o_ref[:, D_FF:] = (dhv * g * sg).astype(BF)

    return pl.pallas_call(
        body, name=name, grid=(T // tm,),
        in_specs=[pl.BlockSpec((tm, D_FF), lambda i: (i, 0)), pl.BlockSpec((tm, D_FF), lambda i: (i, 0)),
                  pl.BlockSpec((tm, D_FF), lambda i: (i, 1))],
        out_specs=pl.BlockSpec((tm, 2 * D_FF), lambda i: (i, 0)),
        out_shape=jax.ShapeDtypeStruct((T, 2 * D_FF), BF), compiler_params=_cp("arbitrary"))(dh, gu, gu)


TM_MM = 1024
TM_EW = 512
TM_SEQ = 256


def _layer_fwd(l, x, xb, kb, vb, W):
    n = f"l{l}_"
    sv = {'x0': x if xb is None else xb}
    z = mm_nn(sv['x0'], W['w_in'][l], W['b_in'][l], out_dtype=BF, tm=TM_MM, tn=1024, name=n + "in_proj")
    yp = pool_fwd(z, W['pool_w'][l], tm=TM_SEQ, name=n + "pool_fwd")
    h, yl = lru_fwd(z, W['lru_conv_w'][l], W['lru_conv_b'][l], W['lru_w_r'][l], W['lru_b_r'][l], W['lru_w_i'][l],
                    W['lru_b_i'][l], W['lru_lambda'][l], W['lru_w_out'][l], tm=TM_SEQ, name=n + "lru_fwd")
    s, yc = sconv_fwd(z, W['sconv_w'][l], W['sconv_w_out'][l], tm=TM_SEQ, name=n + "sconv_fwd")
    merged = merge_fwd(z, yp, yl, yc, W['pool_scale'][l], tm=TM_EW, name=n + "merge_fwd")
    x1, x1b, xh1, rs1 = mm_res_ln(merged, W['w_mix_out'][l], x, W['ln_g'][l][0:1], W['ln_b'][l][0:1], tm=TM_EW,
                                  name=n + "mix_out_ln")
    q, o = attn_fwd(x1b, W['xa_w_q'][l], kb, vb, tm=TM_EW, name=n + "attn_fwd")
    x2, x2b, xh2, rs2 = mm_res_ln(o, W['xa_w_o'][l], x1, W['ln_g'][l][1:2], W['ln_b'][l][1:2], tm=TM_EW,
                                  name=n + "attn_out_ln")
    gu = mm_nn(x2b, W['ffn_w_gu'][l], None, out_dtype=BF, tm=TM_MM, tn=1408, name=n + "ffn_in")
    hdn = swiglu_fwd(gu, tm=TM_EW, name=n + "swiglu_fwd")
    x3, x3b, xh3, rs3 = mm_res_ln(hdn, W['ffn_w_down'][l], x2, W['ln_g'][l][2:3], W['ln_b'][l][2:3], tm=TM_EW,
                                  name=n + "ffn_out_ln")
    sv.update(z=z, yp=yp, h=h, yl=yl, s=s, yc=yc, merged=merged, x1b=x1b, xh1=xh1, rs1=rs1, q=q, o=o, x2b=x2b,
              xh2=xh2, rs2=rs2, gu=gu, hdn=hdn, xh3=xh3, rs3=rs3)
    return x3, x3b, sv


def _layer_bwd(l, dx3, sv, memb, kb, vb, W, loss_from=None):
    n = f"l{l}_"
    G = {}
    res = ln_bwd(dx3, sv['xh3'], sv['rs3'], W['ln_g'][l][2:3], tm=TM_EW, name=n + "ln3_bwd", loss_from=loss_from)
    dp3, dp3b, dg3, db3 = res[:4]
    loss = res[4] if loss_from is not None else None
    dhdn = mm_nt(dp3b, W['ffn_w_down'][l], None, out_dtype=BF, tm=TM_EW, tc=D, name=n + "ffn_down_dx")
    dgu = swiglu_bwd(dhdn, sv['gu'], tm=TM_EW, name=n + "swiglu_bwd")
    dx2 = mm_nt(dgu, W['ffn_w_gu'][l], dp3, out_dtype=F32, tm=TM_MM, tc=1408, name=n + "ffn_in_dx")
    G['ffn_w_gu'] = mm_tn(sv['x2b'], dgu, out_dtype=BF, tk=1024, tn=1408, tt=512, name=n + "ffn_in_dw")
    G['ffn_w_down'] = mm_tn(sv['hdn'], dp3b, out_dtype=BF, tk=1408, tn=1024, tt=512, name=n + "ffn_down_dw")

    dp2, dp2b, dg2, db2 = ln_bwd(dx2, sv['xh2'], sv['rs2'], W['ln_g'][l][1:2], tm=TM_EW, name=n + "ln2_bwd")
    dq, dk, dv = attn_bwd(dp2b, W['xa_w_o'][l], sv['q'], kb, vb, tm=TM_EW, name=n + "attn_bwd")
    dx1 = mm_nt(dq, W['xa_w_q'][l], dp2, out_dtype=F32, tm=TM_MM, tc=D, name=n + "attn_q_dx")
    G['xa_w_o'] = mm_tn(sv['o'], dp2b, out_dtype=BF, tk=1024, tn=1024, tt=512, name=n + "attn_o_dw")
    G['xa_w_q'] = mm_tn(sv['x1b'], dq, out_dtype=BF, tk=1024, tn=1024, tt=512, name=n + "attn_q_dw")
    G['xa_w_k'] = mm_tn(memb, dk, out_dtype=BF, tk=1024, tn=1024, tt=N_MEM, name=n + "attn_k_dw")
    G['xa_w_v'] = mm_tn(memb, dv, out_dtype=BF, tk=1024, tn=1024, tt=N_MEM, name=n + "attn_v_dw")

    dp1, dp1b, dg1, db1 = ln_bwd(dx1, sv['xh1'], sv['rs1'], W['ln_g'][l][0:1], tm=TM_EW, name=n + "ln1_bwd")
    dmerged = mm_nt(dp1b, W['w_mix_out'][l], None, out_dtype=BF, tm=TM_MM, tc=D, name=n + "mix_out_dx")
    G['w_mix_out'] = mm_tn(sv['merged'], dp1b, out_dtype=BF, tk=1024, tn=1024, tt=512, name=n + "mix_out_dw")
    z = sv['z']
    dz, dyp, dyl, dyc = merge_bwd(dmerged, z, sv['yp'], sv['yl'], sv['yc'], W['pool_scale'][l], tm=TM_EW,
                                  name=n + "merge_bwd")
    dz, G['pool_w'], G['pool_scale'] = pool_bwd(dz, dyp, sv['yp'], z, W['pool_w'][l], W['pool_scale'][l],
                                                tm=TM_SEQ, name=n + "pool_bwd")
    (dz, G['lru_conv_w'], G['lru_conv_b'], G['lru_w_r'], G['lru_b_r'], G['lru_w_i'], G['lru_b_i'],
     G['lru_lambda']) = lru_bwd(dz, dyl, z, sv['h'], W['lru_conv_w'][l], W['lru_conv_b'][l], W['lru_w_r'][l],
                                W['lru_b_r'][l], W['lru_w_i'][l], W['lru_b_i'][l], W['lru_lambda'][l],
                                W['lru_w_out'][l], tm=TM_SEQ, name=n + "lru_bwd")
    dz, G['sconv_w'] = sconv_bwd(dz, dyc, z, W['sconv_w'][l], W['sconv_w_out'][l], tm=TM_SEQ, name=n + "sconv_bwd")
    G['lru_w_out'] = mm_tn(sv['h'], dyl, out_dtype=BF, tk=1024, tn=1024, tt=512, name=n + "lru_out_dw")
    G['sconv_w_out'] = mm_tn(sv['s'], dyc, out_dtype=BF, tk=1024, tn=1024, tt=512, name=n + "sconv_out_dw")
    dx0 = mm_nt(dz, W['w_in'][l], dp1, out_dtype=F32, tm=TM_MM, tc=1024, name=n + "in_proj_dx")
    G['w_in'], G['b_in'] = mm_tn(sv['x0'], dz, out_dtype=BF, tk=1024, tn=1024, tt=512, name=n + "in_proj_dw",
                                 colsum=True)
    G['ln_g'] = jnp.concatenate([dg1, dg2, dg3], axis=0)
    G['ln_b'] = jnp.concatenate([db1, db2, db3], axis=0)
    return dx0, G, loss


def local_step(x, mem, target, W):
    memb = mem.astype(BF)
    saves, kvs = [], []
    xf, xb = x, None
    for l in range(DEPTH):
        kb = mm_nn(memb, W['xa_w_k'][l], None, out_dtype=BF, tm=N_MEM, tn=1024, name=f"l{l}_mem_k")
        vb = mm_nn(memb, W['xa_w_v'][l], None, out_dtype=BF, tm=N_MEM, tn=1024, name=f"l{l}_mem_v")
        xf, xb, sv = _layer_fwd(l, xf, xb, kb, vb, W)
        saves.append(sv)
        kvs.append((kb, vb))
    grads = [None] * DEPTH
    dx, loss = None, None
    for l in reversed(range(DEPTH)):
        lf = (W['ln_b'][l][2:3], target) if l == DEPTH - 1 else None
        dx, grads[l], ls = _layer_bwd(l, dx, saves[l], memb, kvs[l][0], kvs[l][1], W, loss_from=lf)
        if ls is not None:
            loss = ls
    return loss, dx, grads


def _coords():
    return lax.axis_index("x"), lax.axis_index("y"), lax.axis_index("c")


def gather_xy(big, small):
    def body(big_ref, small_ref, gb_ref, gs_ref, send_sems, recv_sems, local_sems):
        x, y, c = _coords()
        me = 2 * x + y
        chips = [(1 - x, y), (x, 1 - y), (1 - x, 1 - y)]
        pairs = [(big_ref, gb_ref), (small_ref, gs_ref)]
        local = [pltpu.make_async_copy(src, dst.at[me], local_sems.at[t]) for t, (src, dst) in enumerate(pairs)]
        for cp in local:
            cp.start()
        sends = []
        for j, (cx, cy) in enumerate(chips):
            for t, (src, dst) in enumerate(pairs):
                sends.append(pltpu.make_async_remote_copy(
                    src_ref=src, dst_ref=dst.at[me], send_sem=send_sems.at[2 * j + t],
                    recv_sem=recv_sems.at[2 * j + t], device_id=(cx, cy, c), device_id_type=MESH))
        for cp in sends:
            cp.start()
        for j, (cx, cy) in enumerate(chips):
            for t, (src, dst) in enumerate(pairs):
                pltpu.make_async_remote_copy(
                    src_ref=src, dst_ref=dst.at[2 * cx + cy], send_sem=send_sems.at[2 * j + t],
                    recv_sem=recv_sems.at[2 * j + t], device_id=(cx, cy, c), device_id_type=MESH).wait_recv()
        for cp in sends:
            cp.wait_send()
        for cp in local:
            cp.wait()

    hbm = pl.BlockSpec(memory_space=pl.ANY)
    return pl.pallas_call(
        body, name="gather_weights", in_specs=[hbm, hbm], out_specs=[hbm, hbm],
        out_shape=[jax.ShapeDtypeStruct((4,) + big.shape, big.dtype),
                   jax.ShapeDtypeStruct((4,) + small.shape, small.dtype)],
        scratch_shapes=[pltpu.SemaphoreType.DMA((6,)), pltpu.SemaphoreType.DMA((6,)), pltpu.SemaphoreType.DMA((2,))],
        )(big, small)


def exchange_grads(parts, small):
    def body(p_ref, s_ref, recv_ref, sg_ref, send_sems, recv_sems, local_sems):
        x, y, c = _coords()
        me = 2 * x + y
        dev = 4 * x + 2 * y + c
        local = [pltpu.make_async_copy(p_ref.at[me], recv_ref.at[me], local_sems.at[0]),
                 pltpu.make_async_copy(s_ref, sg_ref.at[dev], local_sems.at[1])]
        for cp in local:
            cp.start()
        sends, waits = [], []
        k = 0
        for fx, fy in [(1, 0), (0, 1), (1, 1)]:
            px, py = x ^ fx, y ^ fy
            sends.append(pltpu.make_async_remote_copy(
                src_ref=p_ref.at[2 * px + py], dst_ref=recv_ref.at[me], send_sem=send_sems.at[k],
                recv_sem=recv_sems.at[k], device_id=(px, py, c), device_id_type=MESH))
            waits.append(pltpu.make_async_remote_copy(
                src_ref=p_ref.at[me], dst_ref=recv_ref.at[2 * px + py], send_sem=send_sems.at[k],
                recv_sem=recv_sems.at[k], device_id=(px, py, c), device_id_type=MESH))
            k += 1
        for f in range(1, 8):
            px, py, pc = x ^ (f >> 2), y ^ ((f >> 1) & 1), c ^ (f & 1)
            sends.append(pltpu.make_async_remote_copy(
                src_ref=s_ref, dst_ref=sg_ref.at[dev], send_sem=send_sems.at[k], recv_sem=recv_sems.at[k],
                device_id=(px, py, pc), device_id_type=MESH))
            waits.append(pltpu.make_async_remote_copy(
                src_ref=s_ref, dst_ref=sg_ref.at[4 * px + 2 * py + pc], send_sem=send_sems.at[k],
                recv_sem=recv_sems.at[k], device_id=(px, py, pc), device_id_type=MESH))
            k += 1
        for cp in sends:
            cp.start()
        for cp in waits:
            cp.wait_recv()
        for cp in sends:
            cp.wait_send()
        for cp in local:
            cp.wait()

    hbm = pl.BlockSpec(memory_space=pl.ANY)
    return pl.pallas_call(
        body, name="exchange_grads", in_specs=[hbm, hbm], out_specs=[hbm, hbm],
        out_shape=[jax.ShapeDtypeStruct(parts.shape, parts.dtype),
                   jax.ShapeDtypeStruct((8,) + small.shape, small.dtype)],
        scratch_shapes=[pltpu.SemaphoreType.DMA((10,)), pltpu.SemaphoreType.DMA((10,)),
                        pltpu.SemaphoreType.DMA((2,))])(parts, small)


def sibling_swap(s):
    def body(s_ref, o_ref, send_sem, recv_sem):
        x, y, c = _coords()
        cp = pltpu.make_async_remote_copy(src_ref=s_ref, dst_ref=o_ref, send_sem=send_sem, recv_sem=recv_sem,
                                          device_id=(x, y, 1 - c), device_id_type=MESH)
        cp.start()
        cp.wait()

    hbm = pl.BlockSpec(memory_space=pl.ANY)
    return pl.pallas_call(
        body, name="sibling_swap", in_specs=[hbm], out_specs=hbm, out_shape=jax.ShapeDtypeStruct(s.shape, s.dtype),
        scratch_shapes=[pltpu.SemaphoreType.DMA, pltpu.SemaphoreType.DMA])(s)


def sum_slots(a, *, tr, name):
    n, R, C = a.shape
    tr = min(tr, R)

    def body(a_ref, o_ref):
        acc = a_ref[0].astype(F32)
        for k in range(1, n):
            acc = acc + a_ref[k].astype(F32)
        o_ref[...] = acc

    return pl.pallas_call(
        body, name=name, grid=(R // tr,), in_specs=[pl.BlockSpec((n, tr, C), lambda i: (0, i, 0))],
        out_specs=pl.BlockSpec((tr, C), lambda i: (i, 0)), out_shape=jax.ShapeDtypeStruct((R, C), F32),
        compiler_params=_cp("arbitrary"))(a)


def adamw(w, g1, g2, m, v, *, name):
    R, C = w.shape
    tr = R
    for cand in (512, 256, 128, 64, 32, 16, 8):
        if R % cand == 0 and cand * C * 4 <= 2 * 1024 * 1024:
            tr = cand
            break

    def body(*refs):
        if g2 is None:
            w_ref, g_ref, m_ref, v_ref, go_ref, d_ref, mo_ref, vo_ref = refs
            g = g_ref[...]
        else:
            w_ref, g_ref, g2_ref, m_ref, v_ref, go_ref, d_ref, mo_ref, vo_ref = refs
            g = g_ref[...] + g2_ref[...]
        mn = ADAM_B1 * m_ref[...] + (1.0 - ADAM_B1) * g
        vn = ADAM_B2 * v_ref[...] + (1.0 - ADAM_B2) * (g * g)
        m_hat = mn / (1.0 - ADAM_B1 ** ADAM_STEP)
        v_hat = vn / (1.0 - ADAM_B2 ** ADAM_STEP)
        go_ref[...] = g
        d_ref[...] = -ADAM_LR * (m_hat / (jnp.sqrt(v_hat) + ADAM_EPS) + ADAM_WD * w_ref[...])
        mo_ref[...] = mn
        vo_ref[...] = vn

    spec = pl.BlockSpec((tr, C), lambda i: (i, 0))
    args = [w, g1] + ([] if g2 is None else [g2]) + [m, v]
    return pl.pallas_call(
        body, name=name, grid=(R // tr,), in_specs=[spec] * len(args), out_specs=[spec] * 4,
        out_shape=[jax.ShapeDtypeStruct((R, C), F32)] * 4, compiler_params=_cp("arbitrary"))(*args)


def _local_shape(name, full_shape):
    shp = list(full_shape)
    ax = BIG_SHARDED.get(name, SMALL_SHARDED.get(name))
    if ax is not None:
        shp[ax] //= 4
    return tuple(shp)


FULL_SHAPES = {
    'w_in': (DEPTH, D, IN_COLS), 'b_in': (DEPTH, IN_COLS), 'pool_w': (DEPTH, 4, POOL_GD, POOL_GD),
    'pool_scale': (DEPTH, D), 'lru_conv_w': (DEPTH, 4, D), 'lru_conv_b': (DEPTH, D),
    'lru_w_r': (DEPTH, LRU_HEADS, LRU_HD, LRU_HD), 'lru_b_r': (DEPTH, D),
    'lru_w_i': (DEPTH, LRU_HEADS, LRU_HD, LRU_HD), 'lru_b_i': (DEPTH, D), 'lru_lambda': (DEPTH, D),
    'lru_w_out': (DEPTH, D, D), 'sconv_w': (DEPTH, 3, D), 'sconv_w_out': (DEPTH, D, D), 'w_mix_out': (DEPTH, D, D),
    'xa_w_q': (DEPTH, D, D), 'xa_w_k': (DEPTH, D, D), 'xa_w_v': (DEPTH, D, D), 'xa_w_o': (DEPTH, D, D),
    'ffn_w_gate': (DEPTH, D, D_FF), 'ffn_w_up': (DEPTH, D, D_FF), 'ffn_w_down': (DEPTH, D_FF, D),
    'ln_g': (DEPTH, 3, D), 'ln_b': (DEPTH, 3, D)}


def _pack(arrs, names, width, dtype, row_mult):
    flat = jnp.concatenate([arrs[n].astype(dtype).reshape(-1) for n in names])
    pad = (-flat.shape[0]) % (width * row_mult)
    if pad:
        flat = jnp.concatenate([flat, jnp.zeros((pad,), dtype)])
    return flat.reshape(-1, width)


def _unpack(flat2d, names, shapes):
    flat = flat2d.reshape(-1)
    out, off = {}, 0
    for n in names:
        size = 1
        for s in shapes[n]:
            size *= s
        out[n] = flat[off:off + size].reshape(shapes[n])
        off += size
    return out


def _gathered_full(g4, names, sharded_axis):
    loc_shapes = {n: _local_shape(n, FULL_SHAPES[n]) for n in names}
    per = [_unpack(g4[k], names, loc_shapes) for k in range(4)]
    return {n: jnp.concatenate([per[k][n] for k in range(4)], axis=sharded_axis[n]) for n in names}


def _shard_of(full, name, k, sharded_axis):
    ax = sharded_axis[name]
    size = full.shape[ax] // 4
    return lax.slice_in_dim(full, k * size, (k + 1) * size, axis=ax)


def _perm_cols(a, perm, axis):
    blocks = [lax.slice_in_dim(a, p * D, (p + 1) * D, axis=axis) for p in perm]
    return jnp.concatenate(blocks, axis=axis)


Z_INV = tuple(Z_PERM.index(j) for j in range(8))
BIG_NAMES = list(BIG_SHARDED)
SMALL_SH_NAMES = list(SMALL_SHARDED)


def kernel(x, mem, w_in, b_in, pool_w, pool_scale, lru_conv_w, lru_conv_b, lru_w_r, lru_b_r, lru_w_i, lru_b_i, lru_lambda, lru_w_out, sconv_w, sconv_w_out, w_mix_out, xa_w_q, xa_w_k, xa_w_v, xa_w_o, ffn_w_gate, ffn_w_up, ffn_w_down, ln_g, ln_b, loss_target, m_w_in, m_b_in, m_pool_w, m_pool_scale, m_lru_conv_w, m_lru_conv_b, m_lru_w_r, m_lru_b_r, m_lru_w_i, m_lru_b_i, m_lru_lambda, m_lru_w_out, m_sconv_w, m_sconv_w_out, m_w_mix_out, m_xa_w_q, m_xa_w_k, m_xa_w_v, m_xa_w_o, m_ffn_w_gate, m_ffn_w_up, m_ffn_w_down, m_ln_g, m_ln_b, v_w_in, v_b_in, v_pool_w, v_pool_scale, v_lru_conv_w, v_lru_conv_b, v_lru_w_r, v_lru_b_r, v_lru_w_i, v_lru_b_i, v_lru_lambda, v_lru_w_out, v_sconv_w, v_sconv_w_out, v_w_mix_out, v_xa_w_q, v_xa_w_k, v_xa_w_v, v_xa_w_o, v_ffn_w_gate, v_ffn_w_up, v_ffn_w_down, v_ln_g, v_ln_b):
    loc = dict(w_in=w_in, b_in=b_in, pool_w=pool_w, pool_scale=pool_scale, lru_conv_w=lru_conv_w,
               lru_conv_b=lru_conv_b, lru_w_r=lru_w_r, lru_b_r=lru_b_r, lru_w_i=lru_w_i, lru_b_i=lru_b_i,
               lru_lambda=lru_lambda, lru_w_out=lru_w_out, sconv_w=sconv_w, sconv_w_out=sconv_w_out,
               w_mix_out=w_mix_out, xa_w_q=xa_w_q, xa_w_k=xa_w_k, xa_w_v=xa_w_v, xa_w_o=xa_w_o,
               ffn_w_gate=ffn_w_gate, ffn_w_up=ffn_w_up, ffn_w_down=ffn_w_down, ln_g=ln_g, ln_b=ln_b)
    mom = dict(w_in=m_w_in, b_in=m_b_in, pool_w=m_pool_w, pool_scale=m_pool_scale, lru_conv_w=m_lru_conv_w,
               lru_conv_b=m_lru_conv_b, lru_w_r=m_lru_w_r, lru_b_r=m_lru_b_r, lru_w_i=m_lru_w_i, lru_b_i=m_lru_b_i,
               lru_lambda=m_lru_lambda, lru_w_out=m_lru_w_out, sconv_w=m_sconv_w, sconv_w_out=m_sconv_w_out,
               w_mix_out=m_w_mix_out, xa_w_q=m_xa_w_q, xa_w_k=m_xa_w_k, xa_w_v=m_xa_w_v, xa_w_o=m_xa_w_o,
               ffn_w_gate=m_ffn_w_gate, ffn_w_up=m_ffn_w_up, ffn_w_down=m_ffn_w_down, ln_g=m_ln_g, ln_b=m_ln_b)
    var = dict(w_in=v_w_in, b_in=v_b_in, pool_w=v_pool_w, pool_scale=v_pool_scale, lru_conv_w=v_lru_conv_w,
               lru_conv_b=v_lru_conv_b, lru_w_r=v_lru_w_r, lru_b_r=v_lru_b_r, lru_w_i=v_lru_w_i, lru_b_i=v_lru_b_i,
               lru_lambda=v_lru_lambda, lru_w_out=v_lru_w_out, sconv_w=v_sconv_w, sconv_w_out=v_sconv_w_out,
               w_mix_out=v_w_mix_out, xa_w_q=v_xa_w_q, xa_w_k=v_xa_w_k, xa_w_v=v_xa_w_v, xa_w_o=v_xa_w_o,
               ffn_w_gate=v_ffn_w_gate, ffn_w_up=v_ffn_w_up, ffn_w_down=v_ffn_w_down, ln_g=v_ln_g, ln_b=v_ln_b)

    big = _pack(loc, BIG_NAMES, D, BF, 16)
    small = _pack(loc, SMALL_SH_NAMES, 256, F32, 8)
    gbig, gsmall = gather_xy(big, small)
    full = _gathered_full(gbig, BIG_NAMES, BIG_SHARDED)
    full.update(_gathered_full(gsmall, SMALL_SH_NAMES, SMALL_SHARDED))
    W = {}
    W['w_in'] = [_perm_cols(full['w_in'][l], Z_PERM, 1) for l in range(DEPTH)]
    W['b_in'] = [_perm_cols(b_in[l:l + 1], Z_PERM, 1) for l in range(DEPTH)]
    W['ffn_w_gu'] = [jnp.concatenate([full['ffn_w_gate'][l], full['ffn_w_up'][l]], axis=1) for l in range(DEPTH)]
    for n in ('pool_w', 'lru_w_out', 'sconv_w_out', 'w_mix_out', 'xa_w_q', 'xa_w_k', 'xa_w_v', 'xa_w_o',
              'ffn_w_down', 'lru_conv_w', 'sconv_w', 'ln_g', 'ln_b'):
        W[n] = [full[n][l] for l in range(DEPTH)]
    for n in ('lru_w_r', 'lru_w_i'):
        W[n] = [loc[n][l].astype(BF) for l in range(DEPTH)]
    for n in ('pool_scale', 'lru_conv_b', 'lru_b_r', 'lru_b_i', 'lru_lambda'):
        W[n] = [loc[n][l:l + 1] for l in range(DEPTH)]

    loss_blk, grad_x, G = local_step(x[0], mem[0], loss_target[0], W)
    loss = lax.psum(loss_blk[0, 0], ("x", "y", "c"))

    gfull = {}
    gfull['w_in'] = jnp.stack([_perm_cols(G[l]['w_in'], Z_INV, 1) for l in range(DEPTH)])
    gfull['b_in'] = jnp.stack([_perm_cols(G[l]['b_in'], Z_INV, 1)[0] for l in range(DEPTH)])
    gfull['ffn_w_gate'] = jnp.stack([G[l]['ffn_w_gu'][:, :D_FF] for l in range(DEPTH)])
    gfull['ffn_w_up'] = jnp.stack([G[l]['ffn_w_gu'][:, D_FF:] for l in range(DEPTH)])
    for n in ('pool_w', 'lru_w_out', 'sconv_w_out', 'w_mix_out', 'xa_w_q', 'xa_w_k', 'xa_w_v', 'xa_w_o',
              'ffn_w_down', 'lru_conv_w', 'lru_w_r', 'lru_w_i', 'sconv_w', 'ln_g', 'ln_b'):
        gfull[n] = jnp.stack([G[l][n] for l in range(DEPTH)])
    for n in ('pool_scale', 'lru_conv_b', 'lru_b_r', 'lru_b_i', 'lru_lambda'):
        gfull[n] = jnp.stack([G[l][n][0] for l in range(DEPTH)])

    parts = jnp.stack([_pack({n: _shard_of(gfull[n], n, k, BIG_SHARDED) for n in BIG_NAMES}, BIG_NAMES, D, BF, 16)
                       for k in range(4)])
    smallp = _pack(gfull, SMALL_ALL, D, F32, 64)
    recv, sgath = exchange_grads(parts, smallp)
    mine = sum_slots(recv, tr=256, name="sum_chip_partials")
    theirs = sibling_swap(mine)
    small_sum = sum_slots(sgath, tr=64, name="sum_small_partials")

    big_shapes = {n: _local_shape(n, FULL_SHAPES[n]) for n in BIG_NAMES}
    g_mine = _unpack(mine, BIG_NAMES, big_shapes)
    g_theirs = _unpack(theirs, BIG_NAMES, big_shapes)
    g_small = _unpack(small_sum, SMALL_ALL, {n: FULL_SHAPES[n] for n in SMALL_ALL})
    xi, yi = lax.axis_index("x"), lax.axis_index("y")
    chip = 2 * xi + yi

    out_g, out_d, out_m, out_v = {}, {}, {}, {}
    for n in WEIGHTS:
        shp = loc[n].shape
        two_d = (-1, shp[-1])
        if n in BIG_SHARDED:
            g1, g2 = g_mine[n].reshape(two_d), g_theirs[n].reshape(two_d)
        else:
            g1 = g_small[n]
            if n in SMALL_SHARDED:
                size = shp[SMALL_SHARDED[n]]
                g1 = lax.dynamic_slice_in_dim(g1, chip * size, size, axis=SMALL_SHARDED[n])
            g1, g2 = g1.reshape(two_d), None
        res = adamw(loc[n].reshape(two_d), g1, g2, mom[n].reshape(two_d), var[n].reshape(two_d), name="adamw_" + n)
        out_g[n], out_d[n], out_m[n], out_v[n] = [r.reshape(shp) for r in res]

    return (loss, grad_x[None], *[out_g[n] for n in WEIGHTS], *[out_d[n] for n in WEIGHTS],
            *[out_m[n] for n in WEIGHTS], *[out_v[n] for n in WEIGHTS])
```

```python
import functools

import jax
import jax.numpy as jnp
from jax import lax
from jax.experimental import pallas as pl
from jax.experimental.pallas import tpu as pltpu

F32 = jnp.float32
BF = jnp.bfloat16
MESH = pl.DeviceIdType.MESH

D = 1024
DEPTH = 2
N_MEM = 256
POOL_WINDOWS = (2, 4, 8, 16)
POOL_GD = 256
LRU_HEADS = 8
LRU_HD = 128
LRU_C = 8.0
X_HEADS = 4
X_HD = 256
D_FF = 2816
IN_COLS = 8 * D
ALPHA = (2 * DEPTH) ** 0.25
LN_EPS = 1e-5
ADAM_LR = 0.001
ADAM_B1 = 0.9
ADAM_B2 = 0.999
ADAM_EPS = 1e-08
ADAM_WD = 0.01
ADAM_STEP = 10

Z_PERM = (2, 3, 4, 5, 6, 7, 0, 1)
ZB_SCONV, ZB_GATE, ZB_POOL, ZB_LRU = 0, 1, 6, 7
HALO = 16
VMEM_LIMIT = 56 * 1024 * 1024

WEIGHTS = ['w_in', 'b_in', 'pool_w', 'pool_scale', 'lru_conv_w', 'lru_conv_b', 'lru_w_r', 'lru_b_r', 'lru_w_i',
           'lru_b_i', 'lru_lambda', 'lru_w_out', 'sconv_w', 'sconv_w_out', 'w_mix_out', 'xa_w_q', 'xa_w_k', 'xa_w_v',
           'xa_w_o', 'ffn_w_gate', 'ffn_w_up', 'ffn_w_down', 'ln_g', 'ln_b']
BIG_SHARDED = {'w_in': 2, 'pool_w': 2, 'lru_w_out': 1, 'sconv_w_out': 1, 'w_mix_out': 1, 'xa_w_q': 1, 'xa_w_k': 1,
               'xa_w_v': 1, 'xa_w_o': 1, 'ffn_w_gate': 2, 'ffn_w_up': 2, 'ffn_w_down': 1}
SMALL_SHARDED = {'lru_conv_w': 2, 'sconv_w': 2, 'ln_g': 2, 'ln_b': 2}
REPLICATED = ['b_in', 'pool_scale', 'lru_conv_b', 'lru_w_r', 'lru_b_r', 'lru_w_i', 'lru_b_i', 'lru_lambda']
SMALL_ALL = ['b_in', 'pool_scale', 'lru_conv_w', 'lru_conv_b', 'lru_w_r', 'lru_b_r', 'lru_w_i', 'lru_b_i',
             'lru_lambda', 'sconv_w', 'ln_g', 'ln_b']


def _cp(*sem):
    return pltpu.CompilerParams(dimension_semantics=sem, vmem_limit_bytes=VMEM_LIMIT)


def _sigmoid(x):
    return 1.0 / (1.0 + jnp.exp(-x))


def mm_nn(a, w, bias, *, out_dtype, tm, tn, name):
    T, K = a.shape
    N = w.shape[1]
    tm, tn = min(tm, T), min(tn, N)

    def body(*refs):
        if bias is None:
            a_ref, w_ref, o_ref = refs
        else:
            a_ref, w_ref, b_ref, o_ref = refs
        acc = jnp.dot(a_ref[...].astype(BF), w_ref[...], preferred_element_type=F32)
        if bias is not None:
            acc = acc + b_ref[...]
        o_ref[...] = acc.astype(o_ref.dtype)

    in_specs = [pl.BlockSpec((tm, K), lambda j, i: (i, 0)), pl.BlockSpec((K, tn), lambda j, i: (0, j))]
    args = [a, w]
    if bias is not None:
        in_specs.append(pl.BlockSpec((1, tn), lambda j, i: (0, j)))
        args.append(bias)
    return pl.pallas_call(
        body, name=name, grid=(N // tn, T // tm), in_specs=in_specs,
        out_specs=pl.BlockSpec((tm, tn), lambda j, i: (i, j)),
        out_shape=jax.ShapeDtypeStruct((T, N), out_dtype), compiler_params=_cp("arbitrary", "arbitrary"))(*args)


def mm_nt(a, w, res, *, out_dtype, tm, tc, name):
    T, C = a.shape
    K = w.shape[0]
    tm, tc = min(tm, T), min(tc, C)
    nc = C // tc

    def body(*refs):
        if res is None:
            a_ref, w_ref, o_ref, acc_ref = refs
        else:
            a_ref, w_ref, r_ref, o_ref, acc_ref = refs
        c = pl.program_id(1)
        part = lax.dot_general(a_ref[...].astype(BF), w_ref[...], (((1,), (1,)), ((), ())),
                               preferred_element_type=F32)

        @pl.when(c == 0)
        def _():
            acc_ref[...] = part

        @pl.when(c > 0)
        def _():
            acc_ref[...] += part

        @pl.when(c == nc - 1)
        def _():
            out = acc_ref[...]
            if res is not None:
                out = out + ALPHA * r_ref[...]
            o_ref[...] = out.astype(o_ref.dtype)

    in_specs = [pl.BlockSpec((tm, tc), lambda i, c: (i, c)), pl.BlockSpec((K, tc), lambda i, c: (0, c))]
    args = [a, w]
    if res is not None:
        in_specs.append(pl.BlockSpec((tm, K), lambda i, c: (i, 0)))
        args.append(res)
    return pl.pallas_call(
        body, name=name, grid=(T // tm, nc), in_specs=in_specs,
        out_specs=pl.BlockSpec((tm, K), lambda i, c: (i, 0)),
        out_shape=jax.ShapeDtypeStruct((T, K), out_dtype),
        scratch_shapes=[pltpu.VMEM((tm, K), F32)], compiler_params=_cp("arbitrary", "arbitrary"))(*args)


def mm_tn(a, b, *, out_dtype, tk, tn, tt, name, colsum=False):
    T, K = a.shape
    N = b.shape[1]
    tk, tn, tt = min(tk, K), min(tn, N), min(tt, T)
    nt = T // tt

    def body(*refs):
        if colsum:
            a_ref, b_ref, o_ref, cs_ref, acc_ref = refs
        else:
            a_ref, b_ref, o_ref, acc_ref = refs
        i, t = pl.program_id(1), pl.program_id(2)
        bb = b_ref[...]
        part = lax.dot_general(a_ref[...].astype(BF), bb.astype(BF), (((0,), (0,)), ((), ())),
                               preferred_element_type=F32)

        @pl.when(t == 0)
        def _():
            acc_ref[...] = part

        @pl.when(t > 0)
        def _():
            acc_ref[...] += part

        @pl.when(t == nt - 1)
        def _():
            o_ref[...] = acc_ref[...].astype(o_ref.dtype)

        if colsum:
            s = jnp.sum(bb.astype(F32), axis=0, keepdims=True)

            @pl.when((i == 0) & (t == 0))
            def _():
                cs_ref[...] = s

            @pl.when((i == 0) & (t > 0))
            def _():
                cs_ref[...] += s

    out_specs = [pl.BlockSpec((tk, tn), lambda j, i, t: (i, j))]
    out_shape = [jax.ShapeDtypeStruct((K, N), out_dtype)]
    if colsum:
        out_specs.append(pl.BlockSpec((1, tn), lambda j, i, t: (0, j)))
        out_shape.append(jax.ShapeDtypeStruct((1, N), F32))
    res = pl.pallas_call(
        body, name=name, grid=(N // tn, K // tk, nt),
        in_specs=[pl.BlockSpec((tt, tk), lambda j, i, t: (t, i)), pl.BlockSpec((tt, tn), lambda j, i, t: (t, j))],
        out_specs=out_specs, out_shape=out_shape, scratch_shapes=[pltpu.VMEM((tk, tn), F32)],
        compiler_params=_cp("arbitrary", "arbitrary", "arbitrary"))(a, b)
    return res if colsum else res[0]


def mm_res_ln(a, w, res, g, b, *, tm, name):
    T, K = a.shape
    tm = min(tm, T)

    def body(a_ref, w_ref, r_ref, g_ref, b_ref, y_ref, yb_ref, xh_ref, rs_ref):
        pre = ALPHA * r_ref[...] + jnp.dot(a_ref[...].astype(BF), w_ref[...], preferred_element_type=F32)
        mu = jnp.mean(pre, axis=-1, keepdims=True)
        cen = pre - mu
        var = jnp.mean(cen * cen, axis=-1, keepdims=True)
        rstd = lax.rsqrt(var + LN_EPS)
        xhat = cen * rstd
        y = xhat * g_ref[...] + b_ref[...]
        y_ref[...] = y
        yb_ref[...] = y.astype(BF)
        xh_ref[...] = xhat
        rs_ref[...] = rstd

    row = lambda i: (i, 0)
    fix = lambda i: (0, 0)
    return pl.pallas_call(
        body, name=name, grid=(T // tm,),
        in_specs=[pl.BlockSpec((tm, K), row), pl.BlockSpec((K, D), fix), pl.BlockSpec((tm, D), row),
                  pl.BlockSpec((1, D), fix), pl.BlockSpec((1, D), fix)],
        out_specs=[pl.BlockSpec((tm, D), row), pl.BlockSpec((tm, D), row), pl.BlockSpec((tm, D), row),
                   pl.BlockSpec((tm, 1), row)],
        out_shape=[jax.ShapeDtypeStruct((T, D), F32), jax.ShapeDtypeStruct((T, D), BF),
                   jax.ShapeDtypeStruct((T, D), F32), jax.ShapeDtypeStruct((T, 1), F32)],
        compiler_params=_cp("arbitrary"))(a, w, res, g, b)


def ln_bwd(dy, xhat, rstd, g, *, tm, name, loss_from=None):
    T = xhat.shape[0]
    tm = min(tm, T)
    with_loss = loss_from is not None

    def body(*refs):
        if with_loss:
            xh_ref, rs_ref, g_ref, b_ref, t_ref, dp_ref, dpb_ref, dg_ref, db_ref, ls_ref = refs
        else:
            dy_ref, xh_ref, rs_ref, g_ref, dp_ref, dpb_ref, dg_ref, db_ref = refs
        i = pl.program_id(0)
        xhat_ = xh_ref[...]
        gg = g_ref[...]
        if with_loss:
            err = xhat_ * gg + b_ref[...] - t_ref[...]
            dyv = err * (1.0 / D)
            lpart = 0.5 * jnp.sum(jnp.sum(err * err, axis=-1, keepdims=True) * (1.0 / D))
        else:
            dyv = dy_ref[...]
        dxh = dyv * gg
        m1 = jnp.mean(dxh, axis=-1, keepdims=True)
        m2 = jnp.mean(dxh * xhat_, axis=-1, keepdims=True)
        dpre = rs_ref[...] * (dxh - m1 - xhat_ * m2)
        dp_ref[...] = dpre
        dpb_ref[...] = dpre.astype(BF)
        dgp = jnp.sum(dyv * xhat_, axis=0, keepdims=True)
        dbp = jnp.sum(dyv, axis=0, keepdims=True)

        @pl.when(i == 0)
        def _():
            dg_ref[...] = dgp
            db_ref[...] = dbp
            if with_loss:
                ls_ref[...] = jnp.full((8, 128), lpart, F32)

        @pl.when(i > 0)
        def _():
            dg_ref[...] += dgp
            db_ref[...] += dbp
            if with_loss:
                ls_ref[...] += jnp.full((8, 128), lpart, F32)

    row = lambda i: (i, 0)
    fix = lambda i: (0, 0)
    if with_loss:
        in_specs = [pl.BlockSpec((tm, D), row), pl.BlockSpec((tm, 1), row), pl.BlockSpec((1, D), fix),
                    pl.BlockSpec((1, D), fix), pl.BlockSpec((tm, D), row)]
        args = [xhat, rstd, g, loss_from[0], loss_from[1]]
    else:
        in_specs = [pl.BlockSpec((tm, D), row), pl.BlockSpec((tm, D), row), pl.BlockSpec((tm, 1), row),
                    pl.BlockSpec((1, D), fix)]
        args = [dy, xhat, rstd, g]
    out_specs = [pl.BlockSpec((tm, D), row), pl.BlockSpec((tm, D), row), pl.BlockSpec((1, D), fix),
                 pl.BlockSpec((1, D), fix)]
    out_shape = [jax.ShapeDtypeStruct((T, D), F32), jax.ShapeDtypeStruct((T, D), BF),
                 jax.ShapeDtypeStruct((1, D), F32), jax.ShapeDtypeStruct((1, D), F32)]
    if with_loss:
        out_specs.append(pl.BlockSpec((8, 128), fix))
        out_shape.append(jax.ShapeDtypeStruct((8, 128), F32))
    return pl.pallas_call(body, name=name, grid=(T // tm,), in_specs=in_specs, out_specs=out_specs,
                          out_shape=out_shape, compiler_params=_cp("arbitrary"))(*args)


def _prev_halo(tm, blk):
    return lambda i: (jnp.maximum(i * (tm // HALO) - 1, 0), blk)


def _next_halo(tm, T, blk):
    return lambda i: (jnp.minimum((i + 1) * (tm // HALO), T // HALO - 1), blk)


def _pool_p(ext, t, g):
    e = ext[:, g * POOL_GD:(g + 1) * POOL_GD]
    s = e
    for sh in (1, 2, 4, 8)[:g + 1]:
        s = s + pltpu.roll(s, sh, axis=0)
    inv = 1.0 / jnp.minimum(t + 1, POOL_WINDOWS[g]).astype(F32)
    return s[HALO:] * inv - e[HALO:]


def pool_fwd(z, pw, *, tm, name):
    T = z.shape[0]
    tm = min(tm, T)

    def body(zm_ref, zh_ref, pw_ref, o_ref):
        i = pl.program_id(0)
        keep = jnp.where(i == 0, 0.0, 1.0).astype(F32)
        ext = jnp.concatenate([zh_ref[...].astype(F32) * keep, zm_ref[...].astype(F32)], axis=0)
        t = i * tm + lax.broadcasted_iota(jnp.int32, (tm, 1), 0)
        outs = [jnp.dot(_pool_p(ext, t, g).astype(BF), pw_ref[g], preferred_element_type=F32) for g in range(4)]
        o_ref[...] = jnp.concatenate(outs, axis=1).astype(o_ref.dtype)

    return pl.pallas_call(
        body, name=name, grid=(T // tm,),
        in_specs=[pl.BlockSpec((tm, D), lambda i: (i, ZB_POOL)), pl.BlockSpec((HALO, D), _prev_halo(tm, ZB_POOL)),
                  pl.BlockSpec((4, POOL_GD, POOL_GD), lambda i: (0, 0, 0))],
        out_specs=pl.BlockSpec((tm, D), lambda i: (i, 0)),
        out_shape=jax.ShapeDtypeStruct((T, D), BF), compiler_params=_cp("arbitrary"))(z, z, pw)


def pool_bwd(dz, dyp, yp_pre, z, pw, ps, *, tm, name):
    T = z.shape[0]
    tm = min(tm, T)
    nt = T // tm

    def body(dz_in, dy_ref, dyn_ref, yp_ref, zm_ref, zh_ref, pw_ref, ps_ref, dz_ref, dpw_ref, dps_ref):
        del dz_in
        i = pl.program_id(0)
        keep_p = jnp.where(i == 0, 0.0, 1.0).astype(F32)
        keep_n = jnp.where(i == nt - 1, 0.0, 1.0).astype(F32)
        ext = jnp.concatenate([zh_ref[...].astype(F32) * keep_p, zm_ref[...].astype(F32)], axis=0)
        t = i * tm + lax.broadcasted_iota(jnp.int32, (tm, 1), 0)
        psv = ps_ref[...]
        dy = dy_ref[...].astype(F32)
        dyp_ext = jnp.concatenate([dy, dyn_ref[...].astype(F32) * keep_n], axis=0) * psv
        t_ext = i * tm + lax.broadcasted_iota(jnp.int32, (tm + HALO, 1), 0)
        dps = jnp.sum(dy * yp_ref[...].astype(F32), axis=0, keepdims=True)
        dzs, dpws = [], []
        for g in range(4):
            sl = slice(g * POOL_GD, (g + 1) * POOL_GD)
            dyg = dyp_ext[:, sl].astype(BF)
            dp = lax.dot_general(dyg, pw_ref[g], (((1,), (1,)), ((), ())), preferred_element_type=F32)
            q = dp * (1.0 / jnp.minimum(t_ext + 1, POOL_WINDOWS[g]).astype(F32))
            s = q
            for sh in (1, 2, 4, 8)[:g + 1]:
                s = s + pltpu.roll(s, tm + HALO - sh, axis=0)
            dzs.append(s[:tm] - dp[:tm])
            p = _pool_p(ext, t, g).astype(BF)
            dpws.append(lax.dot_general(p, dyg[:tm], (((0,), (0,)), ((), ())), preferred_element_type=F32))
        dz_ref[...] = jnp.concatenate(dzs, axis=1).astype(dz_ref.dtype)

        @pl.when(i == 0)
        def _():
            for g in range(4):
                dpw_ref[g] = dpws[g]
            dps_ref[...] = dps

        @pl.when(i > 0)
        def _():
            for g in range(4):
                dpw_ref[g] += dpws[g]
            dps_ref[...] += dps

    row = lambda i: (i, 0)
    return pl.pallas_call(
        body, name=name, grid=(nt,),
        in_specs=[pl.BlockSpec(memory_space=pl.ANY),
                  pl.BlockSpec((tm, D), row), pl.BlockSpec((HALO, D), _next_halo(tm, T, 0)),
                  pl.BlockSpec((tm, D), row),
                  pl.BlockSpec((tm, D), lambda i: (i, ZB_POOL)), pl.BlockSpec((HALO, D), _prev_halo(tm, ZB_POOL)),
                  pl.BlockSpec((4, POOL_GD, POOL_GD), lambda i: (0, 0, 0)), pl.BlockSpec((1, D), lambda i: (0, 0))],
        out_specs=[pl.BlockSpec((tm, D), lambda i: (i, ZB_POOL)),
                   pl.BlockSpec((4, POOL_GD, POOL_GD), lambda i: (0, 0, 0)), pl.BlockSpec((1, D), lambda i: (0, 0))],
        out_shape=[jax.ShapeDtypeStruct(dz.shape, dz.dtype), jax.ShapeDtypeStruct((4, POOL_GD, POOL_GD), F32),
                   jax.ShapeDtypeStruct((1, D), F32)],
        input_output_aliases={0: 0}, compiler_params=_cp("arbitrary"))(dz, dyp, dyp, yp_pre, z, z, pw, ps)


def _lru_gates(zl_ext, cw, cb, wr_ref, br, wi_ref, bi, lam):
    shifted = []
    v = cb
    for k in range(4):
        zs = zl_ext if k == 3 else pltpu.roll(zl_ext, 3 - k, axis=0)
        zs = zs[HALO:]
        shifted.append(zs)
        v = v + cw[k:k + 1, :] * zs
    vb = v.astype(BF)
    rp, ip = [], []
    for h in range(LRU_HEADS):
        sl = slice(h * LRU_HD, (h + 1) * LRU_HD)
        rp.append(jnp.dot(vb[:, sl], wr_ref[h], preferred_element_type=F32))
        ip.append(jnp.dot(vb[:, sl], wi_ref[h], preferred_element_type=F32))
    r = _sigmoid(jnp.concatenate(rp, axis=1) + br)
    ig = _sigmoid(jnp.concatenate(ip, axis=1) + bi)
    sp = jnp.maximum(-lam, 0.0) + jnp.log(1.0 + jnp.exp(-jnp.abs(lam)))
    a = jnp.exp(-LRU_C * r * sp)
    mult = jnp.sqrt(1.0 - a * a)
    return v, vb, r, ig, a, mult, sp, shifted


def lru_fwd(z, cw, cb, wr, br, wi, bi, lam, wlo, *, tm, name):
    T = z.shape[0]
    tm = min(tm, T)
    nch = tm // 8

    def body(zm_ref, zh_ref, cw_ref, cb_ref, wr_ref, br_ref, wi_ref, bi_ref, lam_ref, wlo_ref, h_ref, y_ref,
             a_s, b_s, carry):
        i = pl.program_id(0)

        @pl.when(i == 0)
        def _():
            carry[...] = jnp.zeros_like(carry)

        keep = jnp.where(i == 0, 0.0, 1.0).astype(F32)
        ext = jnp.concatenate([zh_ref[...].astype(F32) * keep, zm_ref[...].astype(F32)], axis=0)
        v, _, _, ig, a, mult, _, _ = _lru_gates(ext, cw_ref[...], cb_ref[...], wr_ref, br_ref[...], wi_ref,
                                                bi_ref[...], lam_ref[...])
        a_s[...] = a
        b_s[...] = mult * (ig * v)
        row = lax.broadcasted_iota(jnp.int32, (8, D), 0)

        def step(ci, hprev):
            sl = pl.ds(pl.multiple_of(ci * 8, 8), 8)
            aa, bb = a_s[sl, :], b_s[sl, :]
            for s in (1, 2, 4):
                m = row >= s
                bb = bb + aa * jnp.where(m, pltpu.roll(bb, s, axis=0), 0.0)
                aa = aa * jnp.where(m, pltpu.roll(aa, s, axis=0), 1.0)
            h = bb + aa * hprev
            h_ref[sl, :] = h
            return jnp.broadcast_to(h[7:8, :], (8, D))

        carry[...] = lax.fori_loop(0, nch, step, carry[...])
        y_ref[...] = jnp.dot(h_ref[...].astype(BF), wlo_ref[...], preferred_element_type=F32).astype(BF)

    fix2 = lambda i: (0, 0)
    fix3 = lambda i: (0, 0, 0)
    return pl.pallas_call(
        body, name=name, grid=(T // tm,),
        in_specs=[pl.BlockSpec((tm, D), lambda i: (i, ZB_LRU)), pl.BlockSpec((HALO, D), _prev_halo(tm, ZB_LRU)),
                  pl.BlockSpec((4, D), fix2), pl.BlockSpec((1, D), fix2),
                  pl.BlockSpec((LRU_HEADS, LRU_HD, LRU_HD), fix3), pl.BlockSpec((1, D), fix2),
                  pl.BlockSpec((LRU_HEADS, LRU_HD, LRU_HD), fix3), pl.BlockSpec((1, D), fix2),
                  pl.BlockSpec((1, D), fix2), pl.BlockSpec((D, D), fix2)],
        out_specs=[pl.BlockSpec((tm, D), lambda i: (i, 0)), pl.BlockSpec((tm, D), lambda i: (i, 0))],
        out_shape=[jax.ShapeDtypeStruct((T, D), F32), jax.ShapeDtypeStruct((T, D), BF)],
        scratch_shapes=[pltpu.VMEM((tm, D), F32), pltpu.VMEM((tm, D), F32), pltpu.VMEM((8, D), F32)],
        compiler_params=_cp("arbitrary"))(z, z, cw, cb, wr, br, wi, bi, lam, wlo)


def lru_bwd(dz, dyl, z, h, cw, cb, wr, br, wi, bi, lam, wlo, *, tm, name):
    T = z.shape[0]
    tm = min(tm, T)
    nt = T // tm
    nch = tm // 8

    def body(dz_in, dy_ref, zm_ref, zh_ref, h_ref, hh_ref, cw_ref, cb_ref, wr_ref, br_ref, wi_ref, bi_ref, lam_ref,
             wlo_ref, dz_ref, dcw_ref, dcb_ref, dwr_ref, dbr_ref, dwi_ref, dbi_ref, dlam_ref,
             c_s, g_s, dh_s, dh_carry, a_carry, dv_carry):
        del dz_in
        i = pl.program_id(0)
        ti = nt - 1 - i

        @pl.when(i == 0)
        def _():
            dh_carry[...] = jnp.zeros_like(dh_carry)
            a_carry[...] = jnp.zeros_like(a_carry)
            dv_carry[...] = jnp.zeros_like(dv_carry)

        keep = jnp.where(ti == 0, 0.0, 1.0).astype(F32)
        ext = jnp.concatenate([zh_ref[...].astype(F32) * keep, zm_ref[...].astype(F32)], axis=0)
        cw_ = cw_ref[...]
        lam_ = lam_ref[...]
        v, vb, r, ig, a, mult, sp, shifted = _lru_gates(ext, cw_, cb_ref[...], wr_ref, br_ref[...], wi_ref,
                                                         bi_ref[...], lam_)
        rowt = lax.broadcasted_iota(jnp.int32, (tm, 1), 0)
        c_s[...] = jnp.where(rowt == tm - 1, a_carry[0:1, :], pltpu.roll(a, tm - 1, axis=0))
        g_s[...] = lax.dot_general(dy_ref[...], wlo_ref[...], (((1,), (1,)), ((), ())), preferred_element_type=F32)
        row = lax.broadcasted_iota(jnp.int32, (8, D), 0)

        def step(k, nxt):
            ci = nch - 1 - k
            sl = pl.ds(pl.multiple_of(ci * 8, 8), 8)
            cc, gg = c_s[sl, :], g_s[sl, :]
            for s in (1, 2, 4):
                m = row < 8 - s
                gg = gg + cc * jnp.where(m, pltpu.roll(gg, 8 - s, axis=0), 0.0)
                cc = cc * jnp.where(m, pltpu.roll(cc, 8 - s, axis=0), 1.0)
            dh = gg + cc * nxt
            dh_s[sl, :] = dh
            return jnp.broadcast_to(dh[0:1, :], (8, D))

        dh_carry[...] = lax.fori_loop(0, nch, step, dh_carry[...])
        a_carry[...] = jnp.broadcast_to(a[0:1, :], (8, D))
        dh = dh_s[...]
        hv = h_ref[...]
        hprev = jnp.where(rowt == 0, hh_ref[7:8, :] * keep, pltpu.roll(hv, 1, axis=0))
        iv = ig * v
        da = dh * hprev
        dmult = dh * iv
        div = dh * mult
        dlog = da * a - dmult * (a * a) / mult
        dr = dlog * (-LRU_C * sp)
        dlam = jnp.sum(dlog * r, axis=0, keepdims=True) * (LRU_C * _sigmoid(-lam_))
        di = div * v
        dv = div * ig
        drp = dr * r * (1.0 - r)
        dip = di * ig * (1.0 - ig)
        drb, dib = drp.astype(BF), dip.astype(BF)
        dvh, dwr, dwi = [], [], []
        nt_dims = (((1,), (1,)), ((), ()))
        tn_dims = (((0,), (0,)), ((), ()))
        for hd in range(LRU_HEADS):
            sl = slice(hd * LRU_HD, (hd + 1) * LRU_HD)
            dvh.append(lax.dot_general(drb[:, sl], wr_ref[hd], nt_dims, preferred_element_type=F32)
                       + lax.dot_general(dib[:, sl], wi_ref[hd], nt_dims, preferred_element_type=F32))
            dwr.append(lax.dot_general(vb[:, sl], drb[:, sl], tn_dims, preferred_element_type=F32))
            dwi.append(lax.dot_general(vb[:, sl], dib[:, sl], tn_dims, preferred_element_type=F32))
        dv = dv + jnp.concatenate(dvh, axis=1)
        dv_ext = jnp.concatenate([dv, dv_carry[...]], axis=0)
        dzl = cw_[3:4, :] * dv
        for k in range(3):
            dzl = dzl + cw_[k:k + 1, :] * pltpu.roll(dv_ext, tm + HALO - (3 - k), axis=0)[:tm]
        dz_ref[...] = dzl.astype(dz_ref.dtype)
        dv_carry[...] = dv[:HALO]
        dcw = jnp.concatenate([jnp.sum(dv * shifted[k], axis=0, keepdims=True) for k in range(4)], axis=0)
        dcb = jnp.sum(dv, axis=0, keepdims=True)
        dbr = jnp.sum(drp, axis=0, keepdims=True)
        dbi = jnp.sum(dip, axis=0, keepdims=True)

        @pl.when(i == 0)
        def _():
            dcw_ref[...] = dcw
            dcb_ref[...] = dcb
            dbr_ref[...] = dbr
            dbi_ref[...] = dbi
            dlam_ref[...] = dlam
            for hd in range(LRU_HEADS):
                dwr_ref[hd] = dwr[hd]
                dwi_ref[hd] = dwi[hd]

        @pl.when(i > 0)
        def _():
            dcw_ref[...] += dcw
            dcb_ref[...] += dcb
            dbr_ref[...] += dbr
            dbi_ref[...] += dbi
            dlam_ref[...] += dlam
            for hd in range(LRU_HEADS):
                dwr_ref[hd] += dwr[hd]
                dwi_ref[hd] += dwi[hd]

    fix2 = lambda i: (0, 0)
    fix3 = lambda i: (0, 0, 0)
    rev = lambda i: (nt - 1 - i, 0)
    vec = pl.BlockSpec((1, D), fix2)
    hw = pl.BlockSpec((LRU_HEADS, LRU_HD, LRU_HD), fix3)
    return pl.pallas_call(
        body, name=name, grid=(nt,),
        in_specs=[pl.BlockSpec(memory_space=pl.ANY),
                  pl.BlockSpec((tm, D), rev),
                  pl.BlockSpec((tm, D), lambda i: (nt - 1 - i, ZB_LRU)),
                  pl.BlockSpec((HALO, D), lambda i: (jnp.maximum((nt - 1 - i) * (tm // HALO) - 1, 0), ZB_LRU)),
                  pl.BlockSpec((tm, D), rev),
                  pl.BlockSpec((8, D), lambda i: (jnp.maximum((nt - 1 - i) * (tm // 8) - 1, 0), 0)),
                  pl.BlockSpec((4, D), fix2), vec, hw, vec, hw, vec, vec, pl.BlockSpec((D, D), fix2)],
        out_specs=[pl.BlockSpec((tm, D), lambda i: (nt - 1 - i, ZB_LRU)),
                   pl.BlockSpec((4, D), fix2), vec, hw, vec, hw, vec, vec],
        out_shape=[jax.ShapeDtypeStruct(dz.shape, dz.dtype), jax.ShapeDtypeStruct((4, D), F32),
                   jax.ShapeDtypeStruct((1, D), F32), jax.ShapeDtypeStruct((LRU_HEADS, LRU_HD, LRU_HD), F32),
                   jax.ShapeDtypeStruct((1, D), F32), jax.ShapeDtypeStruct((LRU_HEADS, LRU_HD, LRU_HD), F32),
                   jax.ShapeDtypeStruct((1, D), F32), jax.ShapeDtypeStruct((1, D), F32)],
        scratch_shapes=[pltpu.VMEM((tm, D), F32), pltpu.VMEM((tm, D), F32), pltpu.VMEM((tm, D), F32),
                        pltpu.VMEM((8, D), F32), pltpu.VMEM((8, D), F32), pltpu.VMEM((HALO, D), F32)],
        input_output_aliases={0: 0},
        compiler_params=_cp("arbitrary"))(dz, dyl, z, z, h, h, cw, cb, wr, br, wi, bi, lam, wlo)


def _sconv_cv(u_ext, sw):
    shifted = []
    cv = None
    for k in range(3):
        us = (u_ext if k == 2 else pltpu.roll(u_ext, 2 - k, axis=0))[HALO:]
        shifted.append(us)
        term = sw[k:k + 1, :] * us
        cv = term if cv is None else cv + term
    return cv, shifted


def sconv_fwd(z, sw, wso, *, tm, name):
    T = z.shape[0]
    tm = min(tm, T)

    def body(zm_ref, zh_ref, sw_ref, wso_ref, s_ref, y_ref):
        i = pl.program_id(0)
        keep = jnp.where(i == 0, 0.0, 1.0).astype(F32)
        zm = zm_ref[...].astype(F32)
        zh = zh_ref[...].astype(F32)
        u_ext = jnp.concatenate([zh[:, D:2 * D] * zh[:, 2 * D:] * keep, zm[:, D:2 * D] * zm[:, 2 * D:]], axis=0)
        cv, _ = _sconv_cv(u_ext, sw_ref[...])
        s = (zm[:, :D] * cv).astype(BF)
        s_ref[...] = s
        y_ref[...] = jnp.dot(s, wso_ref[...], preferred_element_type=F32).astype(BF)

    return pl.pallas_call(
        body, name=name, grid=(T // tm,),
        in_specs=[pl.BlockSpec((tm, 3 * D), lambda i: (i, ZB_SCONV)),
                  pl.BlockSpec((HALO, 3 * D), _prev_halo(tm, ZB_SCONV)),
                  pl.BlockSpec((3, D), lambda i: (0, 0)), pl.BlockSpec((D, D), lambda i: (0, 0))],
        out_specs=[pl.BlockSpec((tm, D), lambda i: (i, 0)), pl.BlockSpec((tm, D), lambda i: (i, 0))],
        out_shape=[jax.ShapeDtypeStruct((T, D), BF), jax.ShapeDtypeStruct((T, D), BF)],
        compiler_params=_cp("arbitrary"))(z, z, sw, wso)


def sconv_bwd(dz, dyc, z, sw, wso, *, tm, name):
    T = z.shape[0]
    tm = min(tm, T)
    nt = T // tm

    def body(dz_in, dy_ref, dyn_ref, zm_ref, zp_ref, zn_ref, sw_ref, wso_ref, dz_ref, dsw_ref):
        del dz_in
        i = pl.program_id(0)
        keep_p = jnp.where(i == 0, 0.0, 1.0).astype(F32)
        keep_n = jnp.where(i == nt - 1, 0.0, 1.0).astype(F32)
        sw_ = sw_ref[...]
        zm = zm_ref[...].astype(F32)
        zp = zp_ref[...].astype(F32)
        zb, zc, zh = zm[:, :D], zm[:, D:2 * D], zm[:, 2 * D:]
        u_ext = jnp.concatenate([zp[:, D:2 * D] * zp[:, 2 * D:] * keep_p, zc * zh], axis=0)
        cv, shifted = _sconv_cv(u_ext, sw_)
        dy_ext = jnp.concatenate([dy_ref[...], dyn_ref[...]], axis=0)
        ds_ext = lax.dot_general(dy_ext, wso_ref[...], (((1,), (1,)), ((), ())), preferred_element_type=F32)
        zb_ext = jnp.concatenate([zb, zn_ref[...][:, :D].astype(F32) * keep_n], axis=0)
        dcv_ext = ds_ext * zb_ext
        du = sw_[2:3, :] * dcv_ext[:tm]
        for k in range(2):
            du = du + sw_[k:k + 1, :] * pltpu.roll(dcv_ext, tm + HALO - (2 - k), axis=0)[:tm]
        dz_ref[...] = jnp.concatenate([ds_ext[:tm] * cv, du * zh, du * zc], axis=1).astype(dz_ref.dtype)
        dcv = dcv_ext[:tm]
        dsw = jnp.concatenate([jnp.sum(dcv * shifted[k], axis=0, keepdims=True) for k in range(3)], axis=0)

        @pl.when(i == 0)
        def _():
            dsw_ref[...] = dsw

        @pl.when(i > 0)
        def _():
            dsw_ref[...] += dsw

    return pl.pallas_call(
        body, name=name, grid=(nt,),
        in_specs=[pl.BlockSpec(memory_space=pl.ANY),
                  pl.BlockSpec((tm, D), lambda i: (i, 0)), pl.BlockSpec((HALO, D), _next_halo(tm, T, 0)),
                  pl.BlockSpec((tm, 3 * D), lambda i: (i, ZB_SCONV)),
                  pl.BlockSpec((HALO, 3 * D), _prev_halo(tm, ZB_SCONV)),
                  pl.BlockSpec((HALO, 3 * D), _next_halo(tm, T, ZB_SCONV)),
                  pl.BlockSpec((3, D), lambda i: (0, 0)), pl.BlockSpec((D, D), lambda i: (0, 0))],
        out_specs=[pl.BlockSpec((tm, 3 * D), lambda i: (i, ZB_SCONV)), pl.BlockSpec((3, D), lambda i: (0, 0))],
        out_shape=[jax.ShapeDtypeStruct(dz.shape, dz.dtype), jax.ShapeDtypeStruct((3, D), F32)],
        input_output_aliases={0: 0}, compiler_params=_cp("arbitrary"))(dz, dyc, dyc, z, z, z, sw, wso)


def merge_fwd(z, yp_pre, yl, yc, ps, *, tm, name):
    T = z.shape[0]
    tm = min(tm, T)

    def body(zg_ref, yp_ref, yl_ref, yc_ref, ps_ref, o_ref):
        gts = _sigmoid(zg_ref[...].astype(F32))
        m = (gts[:, :D] * (yp_ref[...].astype(F32) * ps_ref[...]) + gts[:, D:2 * D] * yl_ref[...].astype(F32)
             + gts[:, 2 * D:] * yc_ref[...].astype(F32))
        o_ref[...] = m.astype(o_ref.dtype)

    row = lambda i: (i, 0)
    return pl.pallas_call(
        body, name=name, grid=(T // tm,),
        in_specs=[pl.BlockSpec((tm, 3 * D), lambda i: (i, ZB_GATE)), pl.BlockSpec((tm, D), row),
                  pl.BlockSpec((tm, D), row), pl.BlockSpec((tm, D), row), pl.BlockSpec((1, D), lambda i: (0, 0))],
        out_specs=pl.BlockSpec((tm, D), row), out_shape=jax.ShapeDtypeStruct((T, D), BF),
        compiler_params=_cp("arbitrary"))(z, yp_pre, yl, yc, ps)


def merge_bwd(dm, z, yp_pre, yl, yc, ps, *, tm, name):
    T = z.shape[0]
    tm = min(tm, T)

    def body(dm_ref, zg_ref, yp_ref, yl_ref, yc_ref, ps_ref, dz_ref, dyp_ref, dyl_ref, dyc_ref):
        gts = _sigmoid(zg_ref[...].astype(F32))
        dmv = dm_ref[...].astype(F32)
        ys = (yp_ref[...].astype(F32) * ps_ref[...], yl_ref[...].astype(F32), yc_ref[...].astype(F32))
        outs = (dyp_ref, dyl_ref, dyc_ref)
        dgs = []
        for j in range(3):
            gj = gts[:, j * D:(j + 1) * D]
            outs[j][...] = (dmv * gj).astype(BF)
            dgs.append(dmv * ys[j] * gj * (1.0 - gj))
        dz_ref[...] = jnp.concatenate(dgs, axis=1).astype(dz_ref.dtype)

    row = lambda i: (i, 0)
    return pl.pallas_call(
        body, name=name, grid=(T // tm,),
        in_specs=[pl.BlockSpec((tm, D), row), pl.BlockSpec((tm, 3 * D), lambda i: (i, ZB_GATE)),
                  pl.BlockSpec((tm, D), row), pl.BlockSpec((tm, D), row), pl.BlockSpec((tm, D), row),
                  pl.BlockSpec((1, D), lambda i: (0, 0))],
        out_specs=[pl.BlockSpec((tm, 3 * D), lambda i: (i, ZB_GATE)), pl.BlockSpec((tm, D), row),
                   pl.BlockSpec((tm, D), row), pl.BlockSpec((tm, D), row)],
        out_shape=[jax.ShapeDtypeStruct((T, IN_COLS), BF), jax.ShapeDtypeStruct((T, D), BF),
                   jax.ShapeDtypeStruct((T, D), BF), jax.ShapeDtypeStruct((T, D), BF)],
        compiler_params=_cp("arbitrary"))(dm, z, yp_pre, yl, yc, ps)


def _attn_probs(qh, kh):
    s = lax.dot_general(qh, kh, (((1,), (1,)), ((), ())), preferred_element_type=F32) * (X_HD ** -0.5)
    e = jnp.exp(s - jnp.max(s, axis=-1, keepdims=True))
    return e / jnp.sum(e, axis=-1, keepdims=True)


def attn_fwd(xb, wq, kb, vb, *, tm, name):
    T = xb.shape[0]
    tm = min(tm, T)

    def body(x_ref, wq_ref, k_ref, v_ref, q_ref, o_ref):
        q = jnp.dot(x_ref[...], wq_ref[...], preferred_element_type=F32).astype(BF)
        q_ref[...] = q
        outs = []
        for h in range(X_HEADS):
            sl = slice(h * X_HD, (h + 1) * X_HD)
            p = _attn_probs(q[:, sl], k_ref[:, sl])
            outs.append(jnp.dot(p.astype(BF), v_ref[:, sl], preferred_element_type=F32))
        o_ref[...] = jnp.concatenate(outs, axis=1).astype(BF)

    row = lambda i: (i, 0)
    fix = lambda i: (0, 0)
    return pl.pallas_call(
        body, name=name, grid=(T // tm,),
        in_specs=[pl.BlockSpec((tm, D), row), pl.BlockSpec((D, D), fix), pl.BlockSpec((N_MEM, D), fix),
                  pl.BlockSpec((N_MEM, D), fix)],
        out_specs=[pl.BlockSpec((tm, D), row), pl.BlockSpec((tm, D), row)],
        out_shape=[jax.ShapeDtypeStruct((T, D), BF), jax.ShapeDtypeStruct((T, D), BF)],
        compiler_params=_cp("arbitrary"))(xb, wq, kb, vb)


def attn_bwd(dxa, wo, q, kb, vb, *, tm, name):
    T = q.shape[0]
    tm = min(tm, T)

    def body(d_ref, wo_ref, q_ref, k_ref, v_ref, dq_ref, dk_ref, dv_ref):
        i = pl.program_id(0)
        do = lax.dot_general(d_ref[...], wo_ref[...], (((1,), (1,)), ((), ())),
                             preferred_element_type=F32).astype(BF)
        q = q_ref[...]
        dqs, dks, dvs = [], [], []
        for h in range(X_HEADS):
            sl = slice(h * X_HD, (h + 1) * X_HD)
            kh, vh = k_ref[:, sl], v_ref[:, sl]
            p = _attn_probs(q[:, sl], kh)
            dp = lax.dot_general(do[:, sl], vh, (((1,), (1,)), ((), ())), preferred_element_type=F32)
            ds = (p * (dp - jnp.sum(dp * p, axis=-1, keepdims=True)) * (X_HD ** -0.5)).astype(BF)
            dqs.append(jnp.dot(ds, kh, preferred_element_type=F32))
            dks.append(lax.dot_general(ds, q[:, sl], (((0,), (0,)), ((), ())), preferred_element_type=F32))
            dvs.append(lax.dot_general(p.astype(BF), do[:, sl], (((0,), (0,)), ((), ())),
                                       preferred_element_type=F32))
        dq_ref[...] = jnp.concatenate(dqs, axis=1).astype(BF)
        dk = jnp.concatenate(dks, axis=1)
        dv = jnp.concatenate(dvs, axis=1)

        @pl.when(i == 0)
        def _():
            dk_ref[...] = dk
            dv_ref[...] = dv

        @pl.when(i > 0)
        def _():
            dk_ref[...] += dk
            dv_ref[...] += dv

    row = lambda i: (i, 0)
    fix = lambda i: (0, 0)
    return pl.pallas_call(
        body, name=name, grid=(T // tm,),
        in_specs=[pl.BlockSpec((tm, D), row), pl.BlockSpec((D, D), fix), pl.BlockSpec((tm, D), row),
                  pl.BlockSpec((N_MEM, D), fix), pl.BlockSpec((N_MEM, D), fix)],
        out_specs=[pl.BlockSpec((tm, D), row), pl.BlockSpec((N_MEM, D), fix), pl.BlockSpec((N_MEM, D), fix)],
        out_shape=[jax.ShapeDtypeStruct((T, D), BF), jax.ShapeDtypeStruct((N_MEM, D), F32),
                   jax.ShapeDtypeStruct((N_MEM, D), F32)],
        compiler_params=_cp("arbitrary"))(dxa, wo, q, kb, vb)


def swiglu_fwd(gu, *, tm, name):
    T = gu.shape[0]
    tm = min(tm, T)

    def body(g_ref, u_ref, o_ref):
        g = g_ref[...].astype(F32)
        o_ref[...] = (g * _sigmoid(g) * u_ref[...].astype(F32)).astype(BF)

    return pl.pallas_call(
        body, name=name, grid=(T // tm,),
        in_specs=[pl.BlockSpec((tm, D_FF), lambda i: (i, 0)), pl.BlockSpec((tm, D_FF), lambda i: (i, 1))],
        out_specs=pl.BlockSpec((tm, D_FF), lambda i: (i, 0)), out_shape=jax.ShapeDtypeStruct((T, D_FF), BF),
        compiler_params=_cp("arbitrary"))(gu, gu)


def swiglu_bwd(dh, gu, *, tm, name):
    T = gu.shape[0]
    tm = min(tm, T)

    def body(dh_ref, g_ref, u_ref, o_ref):
        g = g_ref[...].astype(F32)
        u = u_ref[...].astype(F32)
        dhv = dh_ref[...].astype(F32)
        sg = _sigmoid(g)
        o_ref[:, :D_FF] = (dhv * u * sg * (1.0 + g * (1.0 - sg))).astype(BF)
        o_ref[:, D_FF:] = (dhv * g * sg).astype(BF)

    return pl.pallas_call(
        body, name=name, grid=(T // tm,),
        in_specs=[pl.BlockSpec((tm, D_FF), lambda i: (i, 0)), pl.BlockSpec((tm, D_FF), lambda i: (i, 0)),
                  pl.BlockSpec((tm, D_FF), lambda i: (i, 1))],
        out_specs=pl.BlockSpec((tm, 2 * D_FF), lambda i: (i, 0)),
        out_shape=jax.ShapeDtypeStruct((T, 2 * D_FF), BF), compiler_params=_cp("arbitrary"))(dh, gu, gu)


TM_MM = 1024
TM_EW = 512
TM_SEQ = 256
TT_DW = 2048


def _layer_fwd(l, x, xb, kb, vb, W):
    n = f"l{l}_"
    sv = {'x0': x if xb is None else xb}
    z = mm_nn(sv['x0'], W['w_in'][l], W['b_in'][l], out_dtype=BF, tm=TM_MM, tn=1024, name=n + "in_proj")
    yp = pool_fwd(z, W['pool_w'][l], tm=TM_SEQ, name=n + "pool_fwd")
    h, yl = lru_fwd(z, W['lru_conv_w'][l], W['lru_conv_b'][l], W['lru_w_r'][l], W['lru_b_r'][l], W['lru_w_i'][l],
                    W['lru_b_i'][l], W['lru_lambda'][l], W['lru_w_out'][l], tm=TM_SEQ, name=n + "lru_fwd")
    s, yc = sconv_fwd(z, W['sconv_w'][l], W['sconv_w_out'][l], tm=TM_SEQ, name=n + "sconv_fwd")
    merged = merge_fwd(z, yp, yl, yc, W['pool_scale'][l], tm=TM_EW, name=n + "merge_fwd")
    x1, x1b, xh1, rs1 = mm_res_ln(merged, W['w_mix_out'][l], x, W['ln_g'][l][0:1], W['ln_b'][l][0:1], tm=TM_EW,
                                  name=n + "mix_out_ln")
    q, o = attn_fwd(x1b, W['xa_w_q'][l], kb, vb, tm=TM_EW, name=n + "attn_fwd")
    x2, x2b, xh2, rs2 = mm_res_ln(o, W['xa_w_o'][l], x1, W['ln_g'][l][1:2], W['ln_b'][l][1:2], tm=TM_EW,
                                  name=n + "attn_out_ln")
    gu = mm_nn(x2b, W['ffn_w_gu'][l], None, out_dtype=BF, tm=TM_MM, tn=1408, name=n + "ffn_in")
    hdn = swiglu_fwd(gu, tm=TM_EW, name=n + "swiglu_fwd")
    x3, x3b, xh3, rs3 = mm_res_ln(hdn, W['ffn_w_down'][l], x2, W['ln_g'][l][2:3], W['ln_b'][l][2:3], tm=TM_EW,
                                  name=n + "ffn_out_ln")
    sv.update(z=z, yp=yp, h=h, yl=yl, s=s, yc=yc, merged=merged, x1b=x1b, xh1=xh1, rs1=rs1, q=q, o=o, x2b=x2b,
              xh2=xh2, rs2=rs2, gu=gu, hdn=hdn, xh3=xh3, rs3=rs3)
    return x3, x3b, sv


def _layer_bwd(l, dx3, sv, memb, kb, vb, W, loss_from=None):
    n = f"l{l}_"
    G = {}
    res = ln_bwd(dx3, sv['xh3'], sv['rs3'], W['ln_g'][l][2:3], tm=TM_EW, name=n + "ln3_bwd", loss_from=loss_from)
    dp3, dp3b, dg3, db3 = res[:4]
    loss = res[4] if loss_from is not None else None
    dhdn = mm_nt(dp3b, W['ffn_w_down'][l], None, out_dtype=BF, tm=TM_EW, tc=D, name=n + "ffn_down_dx")
    dgu = swiglu_bwd(dhdn, sv['gu'], tm=TM_EW, name=n + "swiglu_bwd")
    dx2 = mm_nt(dgu, W['ffn_w_gu'][l], dp3, out_dtype=F32, tm=TM_MM, tc=1408, name=n + "ffn_in_dx")
    G['ffn_w_gu'] = mm_tn(sv['x2b'], dgu, out_dtype=BF, tk=1024, tn=1408, tt=TT_DW,name=n + "ffn_in_dw")
    G['ffn_w_down'] = mm_tn(sv['hdn'], dp3b, out_dtype=BF, tk=1408, tn=1024, tt=TT_DW,name=n + "ffn_down_dw")

    dp2, dp2b, dg2, db2 = ln_bwd(dx2, sv['xh2'], sv['rs2'], W['ln_g'][l][1:2], tm=TM_EW, name=n + "ln2_bwd")
    dq, dk, dv = attn_bwd(dp2b, W['xa_w_o'][l], sv['q'], kb, vb, tm=TM_EW, name=n + "attn_bwd")
    dx1 = mm_nt(dq, W['xa_w_q'][l], dp2, out_dtype=F32, tm=TM_MM, tc=D, name=n + "attn_q_dx")
    G['xa_w_o'] = mm_tn(sv['o'], dp2b, out_dtype=BF, tk=1024, tn=1024, tt=TT_DW,name=n + "attn_o_dw")
    G['xa_w_q'] = mm_tn(sv['x1b'], dq, out_dtype=BF, tk=1024, tn=1024, tt=TT_DW,name=n + "attn_q_dw")
    G['xa_w_k'] = mm_tn(memb, dk, out_dtype=BF, tk=1024, tn=1024, tt=N_MEM, name=n + "attn_k_dw")
    G['xa_w_v'] = mm_tn(memb, dv, out_dtype=BF, tk=1024, tn=1024, tt=N_MEM, name=n + "attn_v_dw")

    dp1, dp1b, dg1, db1 = ln_bwd(dx1, sv['xh1'], sv['rs1'], W['ln_g'][l][0:1], tm=TM_EW, name=n + "ln1_bwd")
    dmerged = mm_nt(dp1b, W['w_mix_out'][l], None, out_dtype=BF, tm=TM_MM, tc=D, name=n + "mix_out_dx")
    G['w_mix_out'] = mm_tn(sv['merged'], dp1b, out_dtype=BF, tk=1024, tn=1024, tt=TT_DW,name=n + "mix_out_dw")
    z = sv['z']
    dz, dyp, dyl, dyc = merge_bwd(dmerged, z, sv['yp'], sv['yl'], sv['yc'], W['pool_scale'][l], tm=TM_EW,
                                  name=n + "merge_bwd")
    dz, G['pool_w'], G['pool_scale'] = pool_bwd(dz, dyp, sv['yp'], z, W['pool_w'][l], W['pool_scale'][l],
                                                tm=TM_SEQ, name=n + "pool_bwd")
    (dz, G['lru_conv_w'], G['lru_conv_b'], G['lru_w_r'], G['lru_b_r'], G['lru_w_i'], G['lru_b_i'],
     G['lru_lambda']) = lru_bwd(dz, dyl, z, sv['h'], W['lru_conv_w'][l], W['lru_conv_b'][l], W['lru_w_r'][l],
                                W['lru_b_r'][l], W['lru_w_i'][l], W['lru_b_i'][l], W['lru_lambda'][l],
                                W['lru_w_out'][l], tm=TM_SEQ, name=n + "lru_bwd")
    dz, G['sconv_w'] = sconv_bwd(dz, dyc, z, W['sconv_w'][l], W['sconv_w_out'][l], tm=TM_SEQ, name=n + "sconv_bwd")
    G['lru_w_out'] = mm_tn(sv['h'], dyl, out_dtype=BF, tk=1024, tn=1024, tt=TT_DW,name=n + "lru_out_dw")
    G['sconv_w_out'] = mm_tn(sv['s'], dyc, out_dtype=BF, tk=1024, tn=1024, tt=TT_DW,name=n + "sconv_out_dw")
    dx0 = mm_nt(dz, W['w_in'][l], dp1, out_dtype=F32, tm=TM_MM, tc=2048, name=n + "in_proj_dx")
    G['w_in'], G['b_in'] = mm_tn(sv['x0'], dz, out_dtype=BF, tk=1024, tn=1024, tt=TT_DW,name=n + "in_proj_dw",
                                 colsum=True)
    G['ln_g'] = jnp.concatenate([dg1, dg2, dg3], axis=0)
    G['ln_b'] = jnp.concatenate([db1, db2, db3], axis=0)
    return dx0, G, loss


def local_step(x, mem, target, W):
    memb = mem.astype(BF)
    saves, kvs = [], []
    xf, xb = x, None
    for l in range(DEPTH):
        kb = mm_nn(memb, W['xa_w_k'][l], None, out_dtype=BF, tm=N_MEM, tn=1024, name=f"l{l}_mem_k")
        vb = mm_nn(memb, W['xa_w_v'][l], None, out_dtype=BF, tm=N_MEM, tn=1024, name=f"l{l}_mem_v")
        xf, xb, sv = _layer_fwd(l, xf, xb, kb, vb, W)
        saves.append(sv)
        kvs.append((kb, vb))
    grads = [None] * DEPTH
    dx, loss = None, None
    for l in reversed(range(DEPTH)):
        lf = (W['ln_b'][l][2:3], target) if l == DEPTH - 1 else None
        dx, grads[l], ls = _layer_bwd(l, dx, saves[l], memb, kvs[l][0], kvs[l][1], W, loss_from=lf)
        if ls is not None:
            loss = ls
    return loss, dx, grads


def _coords():
    return lax.axis_index("x"), lax.axis_index("y"), lax.axis_index("c")


FLIPS = ((1, 0), (0, 1), (1, 1))
SQUARES = ('lru_w_out', 'sconv_w_out', 'w_mix_out', 'xa_w_q', 'xa_w_k', 'xa_w_v', 'xa_w_o')
LAYER_SHAPE = {'w_in': (D, IN_COLS), 'pool_w': (4, POOL_GD, POOL_GD), 'ffn_w_gate': (4, D, D_FF // 4),
               'ffn_w_up': (4, D, D_FF // 4), 'ffn_w_down': (D_FF, D), **{n: (D, D) for n in SQUARES}}
PIECES = ('w_in', 'pool_w') + SQUARES + ('ffn_w_gate', 'ffn_w_up', 'ffn_w_down')


def _win(name, ref, k):
    if name == 'w_in':
        return ref.at[:, pl.ds(pl.multiple_of(((2 * k + 6) % 8) * D, D), 2 * D)]
    if name == 'pool_w':
        return ref.at[:, pl.ds(pl.multiple_of(k * (POOL_GD // 4), POOL_GD // 4), POOL_GD // 4), :]
    if name in ('ffn_w_gate', 'ffn_w_up'):
        return ref.at[k]
    rows = LAYER_SHAPE[name][0] // 4
    return ref.at[pl.ds(pl.multiple_of(k * rows, 16), rows), :]


def gather_weights(shards, small):
    n_p = len(PIECES)

    def body(*refs):
        srcs = dict(zip(PIECES, refs[:n_p]))
        small_ref = refs[n_p]
        outs = [dict(zip(PIECES, refs[n_p + 1 + l * n_p:n_p + 1 + (l + 1) * n_p])) for l in range(DEPTH)]
        gs_ref = refs[n_p + 1 + DEPTH * n_p]
        ici_send, ici_recv, d2d_send, d2d_recv, local_sems = refs[n_p + 2 + DEPTH * n_p:]
        x, y, c = _coords()
        me = 2 * x + y
        sib = (x, y, 1 - c)
        pending = []
        li = 0
        for n in PIECES:
            for l in range(DEPTH):
                cp = pltpu.make_async_copy(srcs[n].at[l], _win(n, outs[l][n], me), local_sems.at[li])
                cp.start()
                pending.append(cp)
                li += 1
        cp = pltpu.make_async_copy(small_ref, gs_ref.at[me], local_sems.at[li])
        cp.start()
        pending.append(cp)

        def run(lc):
            sends = []
            for j, (fx, fy) in enumerate(FLIPS):
                peer = (x ^ fx, y ^ fy, c)
                for p, n in enumerate(PIECES):
                    k = 3 * p + j
                    sends.append(pltpu.make_async_remote_copy(
                        src_ref=srcs[n].at[lc], dst_ref=_win(n, outs[lc][n], me), send_sem=ici_send.at[k],
                        recv_sem=ici_recv.at[k], device_id=peer, device_id_type=MESH))
                k = 3 * n_p + j
                sends.append(pltpu.make_async_remote_copy(
                    src_ref=small_ref, dst_ref=gs_ref.at[me], send_sem=ici_send.at[k], recv_sem=ici_recv.at[k],
                    device_id=peer, device_id_type=MESH))
            for cp in sends:
                cp.start()
            for j, (fx, fy) in enumerate(FLIPS):
                other = 2 * (x ^ fx) + (y ^ fy)
                for p, n in enumerate(PIECES):
                    k = 3 * p + j
                    w = _win(n, outs[lc][n], other)
                    pltpu.make_async_remote_copy(src_ref=srcs[n].at[lc], dst_ref=w, send_sem=ici_send.at[k],
                                                 recv_sem=ici_recv.at[k], device_id=sib,
                                                 device_id_type=MESH).wait_recv()
                    fw = pltpu.make_async_remote_copy(src_ref=w, dst_ref=w, send_sem=d2d_send.at[k],
                                                      recv_sem=d2d_recv.at[k], device_id=sib, device_id_type=MESH)
                    fw.start()
                    sends.append(fw)
                k = 3 * n_p + j
                pltpu.make_async_remote_copy(src_ref=small_ref, dst_ref=gs_ref.at[other], send_sem=ici_send.at[k],
                                             recv_sem=ici_recv.at[k], device_id=sib, device_id_type=MESH).wait_recv()
            for j, (fx, fy) in enumerate(FLIPS):
                other = 2 * (x ^ fx) + (y ^ fy)
                for p, n in enumerate(PIECES):
                    k = 3 * p + j
                    w = _win(n, outs[1 - lc][n], other)
                    pltpu.make_async_remote_copy(src_ref=w, dst_ref=w, send_sem=d2d_send.at[k],
                                                 recv_sem=d2d_recv.at[k], device_id=sib,
                                                 device_id_type=MESH).wait_recv()
            for cp in sends:
                cp.wait_send()

        @pl.when(c == 0)
        def _():
            run(0)

        @pl.when(c == 1)
        def _():
            run(1)

        for cp in pending:
            cp.wait()

    hbm = pl.BlockSpec(memory_space=pl.ANY)
    n_out = DEPTH * n_p + 1
    res = pl.pallas_call(
        body, name="gather_weights", in_specs=[hbm] * (n_p + 1), out_specs=[hbm] * n_out,
        out_shape=[jax.ShapeDtypeStruct(LAYER_SHAPE[n], BF) for _ in range(DEPTH) for n in PIECES]
        + [jax.ShapeDtypeStruct((4,) + small.shape, small.dtype)],
        scratch_shapes=[pltpu.SemaphoreType.DMA((3 * n_p + 3,)), pltpu.SemaphoreType.DMA((3 * n_p + 3,)),
                        pltpu.SemaphoreType.DMA((3 * n_p,)), pltpu.SemaphoreType.DMA((3 * n_p,)),
                        pltpu.SemaphoreType.DMA((DEPTH * n_p + 1,))],
    )(*[shards[n] for n in PIECES], small)
    full = {n: [res[l * n_p + p] for l in range(DEPTH)] for p, n in enumerate(PIECES)}
    return full, res[DEPTH * n_p]


def split_layers(p0, p1):
    n = len(p0)

    def body(*refs):
        a0, a1 = refs[:n], refs[n:2 * n]
        mine, theirs = refs[2 * n:3 * n], refs[3 * n:4 * n]
        send_sems, recv_sems, local_sems = refs[4 * n:]
        x, y, c = _coords()
        sib = (x, y, 1 - c)

        def run(keep, give):
            for i in range(n):
                pltpu.make_async_copy(keep[i], mine[i], local_sems.at[i]).start()
                pltpu.make_async_remote_copy(src_ref=give[i], dst_ref=theirs[i], send_sem=send_sems.at[i],
                                             recv_sem=recv_sems.at[i], device_id=sib, device_id_type=MESH).start()

        @pl.when(c == 0)
        def _():
            run(a0, a1)

        @pl.when(c == 1)
        def _():
            run(a1, a0)

        for i in range(n):
            pltpu.make_async_remote_copy(src_ref=a0[i], dst_ref=theirs[i], send_sem=send_sems.at[i],
                                         recv_sem=recv_sems.at[i], device_id=sib, device_id_type=MESH).wait()
            pltpu.make_async_copy(a0[i], mine[i], local_sems.at[i]).wait()

    hbm = pl.BlockSpec(memory_space=pl.ANY)
    shapes = [jax.ShapeDtypeStruct(a.shape, a.dtype) for a in p0]
    res = pl.pallas_call(
        body, name="split_layers", in_specs=[hbm] * (2 * n), out_specs=[hbm] * (2 * n), out_shape=shapes + shapes,
        scratch_shapes=[pltpu.SemaphoreType.DMA((n,)), pltpu.SemaphoreType.DMA((n,)), pltpu.SemaphoreType.DMA((n,))],
    )(*p0, *p1)
    return res[:n], res[n:]


def exchange_chips(q, qsmall):
    n_p = len(PIECES)

    def body(*refs):
        srcs = dict(zip(PIECES, refs[:n_p]))
        s_ref = refs[n_p]
        outs = dict(zip(PIECES, refs[n_p + 1:2 * n_p + 1]))
        so_ref = refs[2 * n_p + 1]
        send_sems, recv_sems, local_sems = refs[2 * n_p + 2:]
        x, y, c = _coords()
        me = 2 * x + y
        pending = []
        for p, n in enumerate(PIECES):
            cp = pltpu.make_async_copy(_win(n, srcs[n], me), outs[n].at[me], local_sems.at[p])
            cp.start()
            pending.append(cp)
        cp = pltpu.make_async_copy(s_ref, so_ref.at[me], local_sems.at[n_p])
        cp.start()
        pending.append(cp)
        sends, waits = [], []
        for j, (fx, fy) in enumerate(FLIPS):
            peer = (x ^ fx, y ^ fy, c)
            other = 2 * (x ^ fx) + (y ^ fy)
            for p, n in enumerate(PIECES):
                k = 3 * p + j
                sends.append(pltpu.make_async_remote_copy(
                    src_ref=_win(n, srcs[n], other), dst_ref=outs[n].at[me], send_sem=send_sems.at[k],
                    recv_sem=recv_sems.at[k], device_id=peer, device_id_type=MESH))
                waits.append(pltpu.make_async_remote_copy(
                    src_ref=_win(n, srcs[n], me), dst_ref=outs[n].at[other], send_sem=send_sems.at[k],
                    recv_sem=recv_sems.at[k], device_id=peer, device_id_type=MESH))
            k = 3 * n_p + j
            sends.append(pltpu.make_async_remote_copy(
                src_ref=s_ref, dst_ref=so_ref.at[me], send_sem=send_sems.at[k], recv_sem=recv_sems.at[k],
                device_id=peer, device_id_type=MESH))
            waits.append(pltpu.make_async_remote_copy(
                src_ref=s_ref, dst_ref=so_ref.at[other], send_sem=send_sems.at[k], recv_sem=recv_sems.at[k],
                device_id=peer, device_id_type=MESH))
        for cp in sends:
            cp.start()
        for cp in waits:
            cp.wait_recv()
        for cp in sends:
            cp.wait_send()
        for cp in pending:
            cp.wait()

    def shard_shape(n):
        shp = LAYER_SHAPE[n]
        if n == 'w_in':
            return (shp[0], shp[1] // 4)
        if n == 'pool_w':
            return (shp[0], shp[1] // 4, shp[2])
        if n in ('ffn_w_gate', 'ffn_w_up'):
            return shp[1:]
        return (shp[0] // 4, shp[1])

    hbm = pl.BlockSpec(memory_space=pl.ANY)
    res = pl.pallas_call(
        body, name="exchange_chips", in_specs=[hbm] * (n_p + 1), out_specs=[hbm] * (n_p + 1),
        out_shape=[jax.ShapeDtypeStruct((4,) + shard_shape(n), BF) for n in PIECES]
        + [jax.ShapeDtypeStruct((4,) + qsmall.shape, qsmall.dtype)],
        scratch_shapes=[pltpu.SemaphoreType.DMA((3 * n_p + 3,)), pltpu.SemaphoreType.DMA((3 * n_p + 3,)),
                        pltpu.SemaphoreType.DMA((n_p + 1,))],
    )(*[q[n] for n in PIECES], qsmall)
    return dict(zip(PIECES, res[:n_p])), res[n_p]


def join_layers(s):
    n = len(s)

    def body(*refs):
        srcs, outs = refs[:n], refs[n:2 * n]
        send_sems, recv_sems, local_sems = refs[2 * n:]
        x, y, c = _coords()
        sib = (x, y, 1 - c)
        cps = []
        for i in range(n):
            lc = pltpu.make_async_copy(srcs[i], outs[i].at[c], local_sems.at[i])
            lc.start()
            cps.append(lc)
            pltpu.make_async_remote_copy(src_ref=srcs[i], dst_ref=outs[i].at[c], send_sem=send_sems.at[i],
                                         recv_sem=recv_sems.at[i], device_id=sib, device_id_type=MESH).start()
        for i in range(n):
            pltpu.make_async_remote_copy(src_ref=srcs[i], dst_ref=outs[i].at[1 - c], send_sem=send_sems.at[i],
                                         recv_sem=recv_sems.at[i], device_id=sib, device_id_type=MESH).wait()
            cps[i].wait()

    hbm = pl.BlockSpec(memory_space=pl.ANY)
    return pl.pallas_call(
        body, name="join_layers", in_specs=[hbm] * n, out_specs=[hbm] * n,
        out_shape=[jax.ShapeDtypeStruct((DEPTH,) + a.shape, a.dtype) for a in s],
        scratch_shapes=[pltpu.SemaphoreType.DMA((n,)), pltpu.SemaphoreType.DMA((n,)), pltpu.SemaphoreType.DMA((n,))],
    )(*s)


def add2(a, b, *, out_dtype, name):
    shp = a.shape
    a2, b2 = a.reshape(-1, shp[-1]), b.reshape(-1, shp[-1])
    R, C = a2.shape
    tr = _row_tile(R, C)

    def body(a_ref, b_ref, o_ref):
        o_ref[...] = (a_ref[...].astype(F32) + b_ref[...].astype(F32)).astype(o_ref.dtype)

    spec = pl.BlockSpec((tr, C), lambda i: (i, 0))
    out = pl.pallas_call(body, name=name, grid=(R // tr,), in_specs=[spec, spec], out_specs=spec,
                         out_shape=jax.ShapeDtypeStruct((R, C), out_dtype), compiler_params=_cp("arbitrary"))(a2, b2)
    return out.reshape(shp)


def _row_tile(R, C):
    for cand in (1024, 512, 256, 128, 64, 32, 16):
        if R % cand == 0 and cand * C * 4 <= 2 * 1024 * 1024:
            return cand
    return R


def sum_slots(a, *, name):
    n = a.shape[0]
    shp = a.shape[1:]
    a3 = a.reshape(n, -1, shp[-1])
    R, C = a3.shape[1:]
    tr = _row_tile(R, C * n // 2)

    def body(a_ref, o_ref):
        acc = a_ref[0].astype(F32)
        for k in range(1, n):
            acc = acc + a_ref[k].astype(F32)
        o_ref[...] = acc

    out = pl.pallas_call(
        body, name=name, grid=(R // tr,), in_specs=[pl.BlockSpec((n, tr, C), lambda i: (0, i, 0))],
        out_specs=pl.BlockSpec((tr, C), lambda i: (i, 0)), out_shape=jax.ShapeDtypeStruct((R, C), F32),
        compiler_params=_cp("arbitrary"))(a3)
    return out.reshape(shp)


def adamw(w, g, m, v, *, name):
    shp = w.shape
    args = [t.reshape(-1, shp[-1]) for t in (w, g, m, v)]
    R, C = args[0].shape
    tr = _row_tile(R, C)

    def body(w_ref, g_ref, m_ref, v_ref, d_ref, mo_ref, vo_ref):
        gv = g_ref[...]
        mn = ADAM_B1 * m_ref[...] + (1.0 - ADAM_B1) * gv
        vn = ADAM_B2 * v_ref[...] + (1.0 - ADAM_B2) * (gv * gv)
        m_hat = mn / (1.0 - ADAM_B1 ** ADAM_STEP)
        v_hat = vn / (1.0 - ADAM_B2 ** ADAM_STEP)
        d_ref[...] = -ADAM_LR * (m_hat / (jnp.sqrt(v_hat) + ADAM_EPS) + ADAM_WD * w_ref[...])
        mo_ref[...] = mn
        vo_ref[...] = vn

    spec = pl.BlockSpec((tr, C), lambda i: (i, 0))
    res = pl.pallas_call(
        body, name=name, grid=(R // tr,), in_specs=[spec] * 4, out_specs=[spec] * 3,
        out_shape=[jax.ShapeDtypeStruct((R, C), F32)] * 3, compiler_params=_cp("arbitrary"))(*args)
    return [r.reshape(shp) for r in res]


def _local_shape(name, full_shape):
    shp = list(full_shape)
    ax = BIG_SHARDED.get(name, SMALL_SHARDED.get(name))
    if ax is not None:
        shp[ax] //= 4
    return tuple(shp)


FULL_SHAPES = {
    'w_in': (DEPTH, D, IN_COLS), 'b_in': (DEPTH, IN_COLS), 'pool_w': (DEPTH, 4, POOL_GD, POOL_GD),
    'pool_scale': (DEPTH, D), 'lru_conv_w': (DEPTH, 4, D), 'lru_conv_b': (DEPTH, D),
    'lru_w_r': (DEPTH, LRU_HEADS, LRU_HD, LRU_HD), 'lru_b_r': (DEPTH, D),
    'lru_w_i': (DEPTH, LRU_HEADS, LRU_HD, LRU_HD), 'lru_b_i': (DEPTH, D), 'lru_lambda': (DEPTH, D),
    'lru_w_out': (DEPTH, D, D), 'sconv_w': (DEPTH, 3, D), 'sconv_w_out': (DEPTH, D, D), 'w_mix_out': (DEPTH, D, D),
    'xa_w_q': (DEPTH, D, D), 'xa_w_k': (DEPTH, D, D), 'xa_w_v': (DEPTH, D, D), 'xa_w_o': (DEPTH, D, D),
    'ffn_w_gate': (DEPTH, D, D_FF), 'ffn_w_up': (DEPTH, D, D_FF), 'ffn_w_down': (DEPTH, D_FF, D),
    'ln_g': (DEPTH, 3, D), 'ln_b': (DEPTH, 3, D)}


def _pack(arrs, names, width, dtype, row_mult):
    flat = jnp.concatenate([arrs[n].astype(dtype).reshape(-1) for n in names])
    pad = (-flat.shape[0]) % (width * row_mult)
    if pad:
        flat = jnp.concatenate([flat, jnp.zeros((pad,), dtype)])
    return flat.reshape(-1, width)


def _unpack(flat2d, names, shapes):
    flat = flat2d.reshape(-1)
    out, off = {}, 0
    for n in names:
        size = 1
        for s in shapes[n]:
            size *= s
        out[n] = flat[off:off + size].reshape(shapes[n])
        off += size
    return out


def _gathered_full(g4, names, sharded_axis):
    loc_shapes = {n: _local_shape(n, FULL_SHAPES[n]) for n in names}
    per = [_unpack(g4[k], names, loc_shapes) for k in range(4)]
    return {n: jnp.concatenate([per[k][n] for k in range(4)], axis=sharded_axis[n]) for n in names}


def _perm_cols(a, perm, axis):
    blocks = [lax.slice_in_dim(a, p * D, (p + 1) * D, axis=axis) for p in perm]
    return jnp.concatenate(blocks, axis=axis)


def _shards_apart(a):
    w = a.shape[1] // 4
    return jnp.stack([a[:, k * w:(k + 1) * w] for k in range(4)])


def _shards_joined(a4):
    return jnp.concatenate([a4[k] for k in range(4)], axis=1)


Z_INV = tuple(Z_PERM.index(j) for j in range(8))
SMALL_SH_NAMES = list(SMALL_SHARDED)
SMALL_ROWS = 32


def kernel(x, mem, w_in, b_in, pool_w, pool_scale, lru_conv_w, lru_conv_b, lru_w_r, lru_b_r, lru_w_i, lru_b_i, lru_lambda, lru_w_out, sconv_w, sconv_w_out, w_mix_out, xa_w_q, xa_w_k, xa_w_v, xa_w_o, ffn_w_gate, ffn_w_up, ffn_w_down, ln_g, ln_b, loss_target, m_w_in, m_b_in, m_pool_w, m_pool_scale, m_lru_conv_w, m_lru_conv_b, m_lru_w_r, m_lru_b_r, m_lru_w_i, m_lru_b_i, m_lru_lambda, m_lru_w_out, m_sconv_w, m_sconv_w_out, m_w_mix_out, m_xa_w_q, m_xa_w_k, m_xa_w_v, m_xa_w_o, m_ffn_w_gate, m_ffn_w_up, m_ffn_w_down, m_ln_g, m_ln_b, v_w_in, v_b_in, v_pool_w, v_pool_scale, v_lru_conv_w, v_lru_conv_b, v_lru_w_r, v_lru_b_r, v_lru_w_i, v_lru_b_i, v_lru_lambda, v_lru_w_out, v_sconv_w, v_sconv_w_out, v_w_mix_out, v_xa_w_q, v_xa_w_k, v_xa_w_v, v_xa_w_o, v_ffn_w_gate, v_ffn_w_up, v_ffn_w_down, v_ln_g, v_ln_b):
    loc = dict(w_in=w_in, b_in=b_in, pool_w=pool_w, pool_scale=pool_scale, lru_conv_w=lru_conv_w,
               lru_conv_b=lru_conv_b, lru_w_r=lru_w_r, lru_b_r=lru_b_r, lru_w_i=lru_w_i, lru_b_i=lru_b_i,
               lru_lambda=lru_lambda, lru_w_out=lru_w_out, sconv_w=sconv_w, sconv_w_out=sconv_w_out,
               w_mix_out=w_mix_out, xa_w_q=xa_w_q, xa_w_k=xa_w_k, xa_w_v=xa_w_v, xa_w_o=xa_w_o,
               ffn_w_gate=ffn_w_gate, ffn_w_up=ffn_w_up, ffn_w_down=ffn_w_down, ln_g=ln_g, ln_b=ln_b)
    mom = dict(w_in=m_w_in, b_in=m_b_in, pool_w=m_pool_w, pool_scale=m_pool_scale, lru_conv_w=m_lru_conv_w,
               lru_conv_b=m_lru_conv_b, lru_w_r=m_lru_w_r, lru_b_r=m_lru_b_r, lru_w_i=m_lru_w_i, lru_b_i=m_lru_b_i,
               lru_lambda=m_lru_lambda, lru_w_out=m_lru_w_out, sconv_w=m_sconv_w, sconv_w_out=m_sconv_w_out,
               w_mix_out=m_w_mix_out, xa_w_q=m_xa_w_q, xa_w_k=m_xa_w_k, xa_w_v=m_xa_w_v, xa_w_o=m_xa_w_o,
               ffn_w_gate=m_ffn_w_gate, ffn_w_up=m_ffn_w_up, ffn_w_down=m_ffn_w_down, ln_g=m_ln_g, ln_b=m_ln_b)
    var = dict(w_in=v_w_in, b_in=v_b_in, pool_w=v_pool_w, pool_scale=v_pool_scale, lru_conv_w=v_lru_conv_w,
               lru_conv_b=v_lru_conv_b, lru_w_r=v_lru_w_r, lru_b_r=v_lru_b_r, lru_w_i=v_lru_w_i, lru_b_i=v_lru_b_i,
               lru_lambda=v_lru_lambda, lru_w_out=v_lru_w_out, sconv_w=v_sconv_w, sconv_w_out=v_sconv_w_out,
               w_mix_out=v_w_mix_out, xa_w_q=v_xa_w_q, xa_w_k=v_xa_w_k, xa_w_v=v_xa_w_v, xa_w_o=v_xa_w_o,
               ffn_w_gate=v_ffn_w_gate, ffn_w_up=v_ffn_w_up, ffn_w_down=v_ffn_w_down, ln_g=v_ln_g, ln_b=v_ln_b)

    small = _pack(loc, SMALL_SH_NAMES, 256, F32, 8)
    full, gsmall = gather_weights({n: loc[n].astype(BF) for n in PIECES}, small)
    full.update(_gathered_full(gsmall, SMALL_SH_NAMES, SMALL_SHARDED))
    W = {}
    W['b_in'] = [jnp.roll(b_in[l:l + 1], -2 * D, axis=1) for l in range(DEPTH)]
    W['ffn_w_gu'] = [jnp.concatenate([_shards_joined(full['ffn_w_gate'][l]), _shards_joined(full['ffn_w_up'][l])],
                                     axis=1) for l in range(DEPTH)]
    for n in ('w_in', 'pool_w') + SQUARES + ('ffn_w_down', 'lru_conv_w', 'sconv_w', 'ln_g', 'ln_b'):
        W[n] = [full[n][l] for l in range(DEPTH)]
    for n in ('lru_w_r', 'lru_w_i'):
        W[n] = [loc[n][l].astype(BF) for l in range(DEPTH)]
    for n in ('pool_scale', 'lru_conv_b', 'lru_b_r', 'lru_b_i', 'lru_lambda'):
        W[n] = [loc[n][l:l + 1] for l in range(DEPTH)]

    loss_blk, grad_x, G = local_step(x[0], mem[0], loss_target[0], W)
    loss = lax.psum(loss_blk[0, 0], ("x", "y", "c"))

    small_shapes = {n: FULL_SHAPES[n][1:] for n in SMALL_ALL}
    parts = []
    for l in range(DEPTH):
        g = dict(G[l])
        g['ffn_w_gate'] = _shards_apart(g['ffn_w_gu'][:, :D_FF])
        g['ffn_w_up'] = _shards_apart(g['ffn_w_gu'][:, D_FF:])
        g['pool_w'] = g['pool_w'].astype(BF)
        g['b_in'] = jnp.roll(g['b_in'], 2 * D, axis=1)
        parts.append([g[n] for n in PIECES] + [_pack(g, SMALL_ALL, D, F32, SMALL_ROWS)])

    mine, theirs = split_layers(parts[0], parts[1])
    names = PIECES + ('small',)
    q = {n: add2(mine[i], theirs[i], out_dtype=F32 if n == 'small' else BF, name="add_cores_" + n)
         for i, n in enumerate(names)}
    recv, recv_small = exchange_chips(q, q['small'])
    sums = [sum_slots(recv[n], name="sum_chips_" + n) for n in PIECES] + [sum_slots(recv_small, name="sum_chips_small")]
    joined = join_layers(sums)
    grads = dict(zip(PIECES, joined[:-1]))
    per_layer = [_unpack(joined[-1][l], SMALL_ALL, small_shapes) for l in range(DEPTH)]
    chip = 2 * lax.axis_index("x") + lax.axis_index("y")
    for n in SMALL_ALL:
        gn = jnp.stack([per_layer[l][n] for l in range(DEPTH)])
        if n in SMALL_SHARDED:
            size = loc[n].shape[SMALL_SHARDED[n]]
            gn = lax.dynamic_slice_in_dim(gn, chip * size, size, axis=SMALL_SHARDED[n])
        grads[n] = gn

    out_d, out_m, out_v = {}, {}, {}
    for n in WEIGHTS:
        out_d[n], out_m[n], out_v[n] = adamw(loc[n], grads[n], mom[n], var[n], name="adamw_" + n)

    return (loss, grad_x[None], *[grads[n] for n in WEIGHTS], *[out_d[n] for n in WEIGHTS],
            *[out_m[n] for n in WEIGHTS], *[out_v[n] for n in WEIGHTS])
```

```python
import functools

import jax
import jax.numpy as jnp
from jax import lax
from jax.experimental import pallas as pl
from jax.experimental.pallas import tpu as pltpu

F32 = jnp.float32
BF = jnp.bfloat16
MESH = pl.DeviceIdType.MESH

D = 1024
DEPTH = 2
N_MEM = 256
POOL_WINDOWS = (2, 4, 8, 16)
POOL_GD = 256
LRU_HEADS = 8
LRU_HD = 128
LRU_C = 8.0
X_HEADS = 4
X_HD = 256
D_FF = 2816
IN_COLS = 8 * D
ALPHA = (2 * DEPTH) ** 0.25
LN_EPS = 1e-5
ADAM_LR = 0.001
ADAM_B1 = 0.9
ADAM_B2 = 0.999
ADAM_EPS = 1e-08
ADAM_WD = 0.01
ADAM_STEP = 10

Z_PERM = (2, 3, 4, 5, 6, 7, 0, 1)
ZB_SCONV, ZB_GATE, ZB_POOL, ZB_LRU = 0, 1, 6, 7
HALO = 16
VMEM_LIMIT = 56 * 1024 * 1024

WEIGHTS = ['w_in', 'b_in', 'pool_w', 'pool_scale', 'lru_conv_w', 'lru_conv_b', 'lru_w_r', 'lru_b_r', 'lru_w_i',
           'lru_b_i', 'lru_lambda', 'lru_w_out', 'sconv_w', 'sconv_w_out', 'w_mix_out', 'xa_w_q', 'xa_w_k', 'xa_w_v',
           'xa_w_o', 'ffn_w_gate', 'ffn_w_up', 'ffn_w_down', 'ln_g', 'ln_b']
BIG_SHARDED = {'w_in': 2, 'pool_w': 2, 'lru_w_out': 1, 'sconv_w_out': 1, 'w_mix_out': 1, 'xa_w_q': 1, 'xa_w_k': 1,
               'xa_w_v': 1, 'xa_w_o': 1, 'ffn_w_gate': 2, 'ffn_w_up': 2, 'ffn_w_down': 1}
SMALL_SHARDED = {'lru_conv_w': 2, 'sconv_w': 2, 'ln_g': 2, 'ln_b': 2}
REPLICATED = ['b_in', 'pool_scale', 'lru_conv_b', 'lru_w_r', 'lru_b_r', 'lru_w_i', 'lru_b_i', 'lru_lambda']
SMALL_ALL = ['b_in', 'pool_scale', 'lru_conv_w', 'lru_conv_b', 'lru_w_r', 'lru_b_r', 'lru_w_i', 'lru_b_i',
             'lru_lambda', 'sconv_w', 'ln_g', 'ln_b']


def _cp(*sem):
    return pltpu.CompilerParams(dimension_semantics=sem, vmem_limit_bytes=VMEM_LIMIT)


def _sigmoid(x):
    return 1.0 / (1.0 + jnp.exp(-x))


def mm_nn(a, w, bias, *, out_dtype, tm, tn, name):
    T, K = a.shape
    N = w.shape[1]
    tm, tn = min(tm, T), min(tn, N)

    def body(*refs):
        if bias is None:
            a_ref, w_ref, o_ref = refs
        else:
            a_ref, w_ref, b_ref, o_ref = refs
        acc = jnp.dot(a_ref[...].astype(BF), w_ref[...], preferred_element_type=F32)
        if bias is not None:
            acc = acc + b_ref[...]
        o_ref[...] = acc.astype(o_ref.dtype)

    in_specs = [pl.BlockSpec((tm, K), lambda j, i: (i, 0)), pl.BlockSpec((K, tn), lambda j, i: (0, j))]
    args = [a, w]
    if bias is not None:
        in_specs.append(pl.BlockSpec((1, tn), lambda j, i: (0, j)))
        args.append(bias)
    return pl.pallas_call(
        body, name=name, grid=(N // tn, T // tm), in_specs=in_specs,
        out_specs=pl.BlockSpec((tm, tn), lambda j, i: (i, j)),
        out_shape=jax.ShapeDtypeStruct((T, N), out_dtype), compiler_params=_cp("arbitrary", "arbitrary"))(*args)


def mm_nt(a, w, res, *, out_dtype, tm, tc, name):
    T, C = a.shape
    K = w.shape[0]
    tm, tc = min(tm, T), min(tc, C)
    nc = C // tc

    def body(*refs):
        if res is None:
            a_ref, w_ref, o_ref, acc_ref = refs
        else:
            a_ref, w_ref, r_ref, o_ref, acc_ref = refs
        c = pl.program_id(1)
        part = lax.dot_general(a_ref[...].astype(BF), w_ref[...], (((1,), (1,)), ((), ())),
                               preferred_element_type=F32)

        @pl.when(c == 0)
        def _():
            acc_ref[...] = part

        @pl.when(c > 0)
        def _():
            acc_ref[...] += part

        @pl.when(c == nc - 1)
        def _():
            out = acc_ref[...]
            if res is not None:
                out = out + ALPHA * r_ref[...]
            o_ref[...] = out.astype(o_ref.dtype)

    in_specs = [pl.BlockSpec((tm, tc), lambda i, c: (i, c)), pl.BlockSpec((K, tc), lambda i, c: (0, c))]
    args = [a, w]
    if res is not None:
        in_specs.append(pl.BlockSpec((tm, K), lambda i, c: (i, 0)))
        args.append(res)
    return pl.pallas_call(
        body, name=name, grid=(T // tm, nc), in_specs=in_specs,
        out_specs=pl.BlockSpec((tm, K), lambda i, c: (i, 0)),
        out_shape=jax.ShapeDtypeStruct((T, K), out_dtype),
        scratch_shapes=[pltpu.VMEM((tm, K), F32)], compiler_params=_cp("arbitrary", "arbitrary"))(*args)


def mm_tn(a, b, *, out_dtype, tk, tn, tt, name, colsum=False):
    T, K = a.shape
    N = b.shape[1]
    tk, tn, tt = min(tk, K), min(tn, N), min(tt, T)
    nt = T // tt

    def body(*refs):
        if colsum:
            a_ref, b_ref, o_ref, cs_ref, acc_ref = refs
        else:
            a_ref, b_ref, o_ref, acc_ref = refs
        i, t = pl.program_id(1), pl.program_id(2)
        bb = b_ref[...]
        part = lax.dot_general(a_ref[...].astype(BF), bb.astype(BF), (((0,), (0,)), ((), ())),
                               preferred_element_type=F32)

        @pl.when(t == 0)
        def _():
            acc_ref[...] = part

        @pl.when(t > 0)
        def _():
            acc_ref[...] += part

        @pl.when(t == nt - 1)
        def _():
            o_ref[...] = acc_ref[...].astype(o_ref.dtype)

        if colsum:
            s = jnp.sum(bb.astype(F32), axis=0, keepdims=True)

            @pl.when((i == 0) & (t == 0))
            def _():
                cs_ref[...] = s

            @pl.when((i == 0) & (t > 0))
            def _():
                cs_ref[...] += s

    out_specs = [pl.BlockSpec((tk, tn), lambda j, i, t: (i, j))]
    out_shape = [jax.ShapeDtypeStruct((K, N), out_dtype)]
    if colsum:
        out_specs.append(pl.BlockSpec((1, tn), lambda j, i, t: (0, j)))
        out_shape.append(jax.ShapeDtypeStruct((1, N), F32))
    res = pl.pallas_call(
        body, name=name, grid=(N // tn, K // tk, nt),
        in_specs=[pl.BlockSpec((tt, tk), lambda j, i, t: (t, i)), pl.BlockSpec((tt, tn), lambda j, i, t: (t, j))],
        out_specs=out_specs, out_shape=out_shape, scratch_shapes=[pltpu.VMEM((tk, tn), F32)],
        compiler_params=_cp("arbitrary", "arbitrary", "arbitrary"))(a, b)
    return res if colsum else res[0]


def mm_res_ln(a, w, res, g, b, *, tm, name):
    T, K = a.shape
    tm = min(tm, T)

    def body(a_ref, w_ref, r_ref, g_ref, b_ref, y_ref, yb_ref, xh_ref, rs_ref):
        pre = ALPHA * r_ref[...] + jnp.dot(a_ref[...].astype(BF), w_ref[...], preferred_element_type=F32)
        mu = jnp.mean(pre, axis=-1, keepdims=True)
        cen = pre - mu
        var = jnp.mean(cen * cen, axis=-1, keepdims=True)
        rstd = lax.rsqrt(var + LN_EPS)
        xhat = cen * rstd
        y = xhat * g_ref[...] + b_ref[...]
        y_ref[...] = y
        yb_ref[...] = y.astype(BF)
        xh_ref[...] = xhat
        rs_ref[...] = rstd

    row = lambda i: (i, 0)
    fix = lambda i: (0, 0)
    return pl.pallas_call(
        body, name=name, grid=(T // tm,),
        in_specs=[pl.BlockSpec((tm, K), row), pl.BlockSpec((K, D), fix), pl.BlockSpec((tm, D), row),
                  pl.BlockSpec((1, D), fix), pl.BlockSpec((1, D), fix)],
        out_specs=[pl.BlockSpec((tm, D), row), pl.BlockSpec((tm, D), row), pl.BlockSpec((tm, D), row),
                   pl.BlockSpec((tm, 1), row)],
        out_shape=[jax.ShapeDtypeStruct((T, D), F32), jax.ShapeDtypeStruct((T, D), BF),
                   jax.ShapeDtypeStruct((T, D), F32), jax.ShapeDtypeStruct((T, 1), F32)],
        compiler_params=_cp("arbitrary"))(a, w, res, g, b)


def ln_bwd(dy, xhat, rstd, g, *, tm, name, loss_from=None):
    T = xhat.shape[0]
    tm = min(tm, T)
    with_loss = loss_from is not None

    def body(*refs):
        if with_loss:
            xh_ref, rs_ref, g_ref, b_ref, t_ref, dp_ref, dpb_ref, dg_ref, db_ref, ls_ref = refs
        else:
            dy_ref, xh_ref, rs_ref, g_ref, dp_ref, dpb_ref, dg_ref, db_ref = refs
        i = pl.program_id(0)
        xhat_ = xh_ref[...]
        gg = g_ref[...]
        if with_loss:
            err = xhat_ * gg + b_ref[...] - t_ref[...]
            dyv = err * (1.0 / D)
            lpart = 0.5 * jnp.sum(jnp.sum(err * err, axis=-1, keepdims=True) * (1.0 / D))
        else:
            dyv = dy_ref[...]
        dxh = dyv * gg
        m1 = jnp.mean(dxh, axis=-1, keepdims=True)
        m2 = jnp.mean(dxh * xhat_, axis=-1, keepdims=True)
        dpre = rs_ref[...] * (dxh - m1 - xhat_ * m2)
        dp_ref[...] = dpre
        dpb_ref[...] = dpre.astype(BF)
        dgp = jnp.sum(dyv * xhat_, axis=0, keepdims=True)
        dbp = jnp.sum(dyv, axis=0, keepdims=True)

        @pl.when(i == 0)
        def _():
            dg_ref[...] = dgp
            db_ref[...] = dbp
            if with_loss:
                ls_ref[...] = jnp.full((8, 128), lpart, F32)

        @pl.when(i > 0)
        def _():
            dg_ref[...] += dgp
            db_ref[...] += dbp
            if with_loss:
                ls_ref[...] += jnp.full((8, 128), lpart, F32)

    row = lambda i: (i, 0)
    fix = lambda i: (0, 0)
    if with_loss:
        in_specs = [pl.BlockSpec((tm, D), row), pl.BlockSpec((tm, 1), row), pl.BlockSpec((1, D), fix),
                    pl.BlockSpec((1, D), fix), pl.BlockSpec((tm, D), row)]
        args = [xhat, rstd, g, loss_from[0], loss_from[1]]
    else:
        in_specs = [pl.BlockSpec((tm, D), row), pl.BlockSpec((tm, D), row), pl.BlockSpec((tm, 1), row),
                    pl.BlockSpec((1, D), fix)]
        args = [dy, xhat, rstd, g]
    out_specs = [pl.BlockSpec((tm, D), row), pl.BlockSpec((tm, D), row), pl.BlockSpec((1, D), fix),
                 pl.BlockSpec((1, D), fix)]
    out_shape = [jax.ShapeDtypeStruct((T, D), F32), jax.ShapeDtypeStruct((T, D), BF),
                 jax.ShapeDtypeStruct((1, D), F32), jax.ShapeDtypeStruct((1, D), F32)]
    if with_loss:
        out_specs.append(pl.BlockSpec((8, 128), fix))
        out_shape.append(jax.ShapeDtypeStruct((8, 128), F32))
    return pl.pallas_call(body, name=name, grid=(T // tm,), in_specs=in_specs, out_specs=out_specs,
                          out_shape=out_shape, compiler_params=_cp("arbitrary"))(*args)


def _prev_halo(tm, blk):
    return lambda i: (jnp.maximum(i * (tm // HALO) - 1, 0), blk)


def _next_halo(tm, T, blk):
    return lambda i: (jnp.minimum((i + 1) * (tm // HALO), T // HALO - 1), blk)


def _pool_p(ext, t, g):
    e = ext[:, g * POOL_GD:(g + 1) * POOL_GD]
    s = e
    for sh in (1, 2, 4, 8)[:g + 1]:
        s = s + pltpu.roll(s, sh, axis=0)
    inv = 1.0 / jnp.minimum(t + 1, POOL_WINDOWS[g]).astype(F32)
    return s[HALO:] * inv - e[HALO:]


def pool_fwd(z, pw, *, tm, name):
    T = z.shape[0]
    tm = min(tm, T)

    def body(zm_ref, zh_ref, pw_ref, o_ref):
        i = pl.program_id(0)
        keep = jnp.where(i == 0, 0.0, 1.0).astype(F32)
        ext = jnp.concatenate([zh_ref[...].astype(F32) * keep, zm_ref[...].astype(F32)], axis=0)
        t = i * tm + lax.broadcasted_iota(jnp.int32, (tm, 1), 0)
        outs = [jnp.dot(_pool_p(ext, t, g).astype(BF), pw_ref[g], preferred_element_type=F32) for g in range(4)]
        o_ref[...] = jnp.concatenate(outs, axis=1).astype(o_ref.dtype)

    return pl.pallas_call(
        body, name=name, grid=(T // tm,),
        in_specs=[pl.BlockSpec((tm, D), lambda i: (i, ZB_POOL)), pl.BlockSpec((HALO, D), _prev_halo(tm, ZB_POOL)),
                  pl.BlockSpec((4, POOL_GD, POOL_GD), lambda i: (0, 0, 0))],
        out_specs=pl.BlockSpec((tm, D), lambda i: (i, 0)),
        out_shape=jax.ShapeDtypeStruct((T, D), BF), compiler_params=_cp("arbitrary"))(z, z, pw)


def pool_bwd(dz, dyp, yp_pre, z, pw, ps, *, tm, name):
    T = z.shape[0]
    tm = min(tm, T)
    nt = T // tm

    def body(dz_in, dy_ref, dyn_ref, yp_ref, zm_ref, zh_ref, pw_ref, ps_ref, dz_ref, dpw_ref, dps_ref):
        del dz_in
        i = pl.program_id(0)
        keep_p = jnp.where(i == 0, 0.0, 1.0).astype(F32)
        keep_n = jnp.where(i == nt - 1, 0.0, 1.0).astype(F32)
        ext = jnp.concatenate([zh_ref[...].astype(F32) * keep_p, zm_ref[...].astype(F32)], axis=0)
        t = i * tm + lax.broadcasted_iota(jnp.int32, (tm, 1), 0)
        psv = ps_ref[...]
        dy = dy_ref[...].astype(F32)
        dyp_ext = jnp.concatenate([dy, dyn_ref[...].astype(F32) * keep_n], axis=0) * psv
        t_ext = i * tm + lax.broadcasted_iota(jnp.int32, (tm + HALO, 1), 0)
        dps = jnp.sum(dy * yp_ref[...].astype(F32), axis=0, keepdims=True)
        dzs, dpws = [], []
        for g in range(4):
            sl = slice(g * POOL_GD, (g + 1) * POOL_GD)
            dyg = dyp_ext[:, sl].astype(BF)
            dp = lax.dot_general(dyg, pw_ref[g], (((1,), (1,)), ((), ())), preferred_element_type=F32)
            q = dp * (1.0 / jnp.minimum(t_ext + 1, POOL_WINDOWS[g]).astype(F32))
            s = q
            for sh in (1, 2, 4, 8)[:g + 1]:
                s = s + pltpu.roll(s, tm + HALO - sh, axis=0)
            dzs.append(s[:tm] - dp[:tm])
            p = _pool_p(ext, t, g).astype(BF)
            dpws.append(lax.dot_general(p, dyg[:tm], (((0,), (0,)), ((), ())), preferred_element_type=F32))
        dz_ref[...] = jnp.concatenate(dzs, axis=1).astype(dz_ref.dtype)

        @pl.when(i == 0)
        def _():
            for g in range(4):
                dpw_ref[g] = dpws[g]
            dps_ref[...] = dps

        @pl.when(i > 0)
        def _():
            for g in range(4):
                dpw_ref[g] += dpws[g]
            dps_ref[...] += dps

    row = lambda i: (i, 0)
    return pl.pallas_call(
        body, name=name, grid=(nt,),
        in_specs=[pl.BlockSpec(memory_space=pl.ANY),
                  pl.BlockSpec((tm, D), row), pl.BlockSpec((HALO, D), _next_halo(tm, T, 0)),
                  pl.BlockSpec((tm, D), row),
                  pl.BlockSpec((tm, D), lambda i: (i, ZB_POOL)), pl.BlockSpec((HALO, D), _prev_halo(tm, ZB_POOL)),
                  pl.BlockSpec((4, POOL_GD, POOL_GD), lambda i: (0, 0, 0)), pl.BlockSpec((1, D), lambda i: (0, 0))],
        out_specs=[pl.BlockSpec((tm, D), lambda i: (i, ZB_POOL)),
                   pl.BlockSpec((4, POOL_GD, POOL_GD), lambda i: (0, 0, 0)), pl.BlockSpec((1, D), lambda i: (0, 0))],
        out_shape=[jax.ShapeDtypeStruct(dz.shape, dz.dtype), jax.ShapeDtypeStruct((4, POOL_GD, POOL_GD), F32),
                   jax.ShapeDtypeStruct((1, D), F32)],
        input_output_aliases={0: 0}, compiler_params=_cp("arbitrary"))(dz, dyp, dyp, yp_pre, z, z, pw, ps)


def _lru_gates(zl_ext, cw, cb, wr_ref, br, wi_ref, bi, lam):
    shifted = []
    v = cb
    for k in range(4):
        zs = zl_ext if k == 3 else pltpu.roll(zl_ext, 3 - k, axis=0)
        zs = zs[HALO:]
        shifted.append(zs)
        v = v + cw[k:k + 1, :] * zs
    vb = v.astype(BF)
    rp, ip = [], []
    for h in range(LRU_HEADS):
        sl = slice(h * LRU_HD, (h + 1) * LRU_HD)
        rp.append(jnp.dot(vb[:, sl], wr_ref[h], preferred_element_type=F32))
        ip.append(jnp.dot(vb[:, sl], wi_ref[h], preferred_element_type=F32))
    r = _sigmoid(jnp.concatenate(rp, axis=1) + br)
    ig = _sigmoid(jnp.concatenate(ip, axis=1) + bi)
    sp = jnp.maximum(-lam, 0.0) + jnp.log(1.0 + jnp.exp(-jnp.abs(lam)))
    a = jnp.exp(-LRU_C * r * sp)
    mult = jnp.sqrt(1.0 - a * a)
    return v, vb, r, ig, a, mult, sp, shifted


def lru_fwd(z, cw, cb, wr, br, wi, bi, lam, wlo, *, tm, name):
    T = z.shape[0]
    tm = min(tm, T)
    nch = tm // 8

    def body(zm_ref, zh_ref, cw_ref, cb_ref, wr_ref, br_ref, wi_ref, bi_ref, lam_ref, wlo_ref, h_ref, y_ref,
             a_s, b_s, carry):
        i = pl.program_id(0)

        @pl.when(i == 0)
        def _():
            carry[...] = jnp.zeros_like(carry)

        keep = jnp.where(i == 0, 0.0, 1.0).astype(F32)
        ext = jnp.concatenate([zh_ref[...].astype(F32) * keep, zm_ref[...].astype(F32)], axis=0)
        v, _, _, ig, a, mult, _, _ = _lru_gates(ext, cw_ref[...], cb_ref[...], wr_ref, br_ref[...], wi_ref,
                                                bi_ref[...], lam_ref[...])
        a_s[...] = a
        b_s[...] = mult * (ig * v)
        row = lax.broadcasted_iota(jnp.int32, (8, D), 0)

        def step(ci, hprev):
            sl = pl.ds(pl.multiple_of(ci * 8, 8), 8)
            aa, bb = a_s[sl, :], b_s[sl, :]
            for s in (1, 2, 4):
                m = row >= s
                bb = bb + aa * jnp.where(m, pltpu.roll(bb, s, axis=0), 0.0)
                aa = aa * jnp.where(m, pltpu.roll(aa, s, axis=0), 1.0)
            h = bb + aa * hprev
            h_ref[sl, :] = h
            return jnp.broadcast_to(h[7:8, :], (8, D))

        carry[...] = lax.fori_loop(0, nch, step, carry[...])
        y_ref[...] = jnp.dot(h_ref[...].astype(BF), wlo_ref[...], preferred_element_type=F32).astype(BF)

    fix2 = lambda i: (0, 0)
    fix3 = lambda i: (0, 0, 0)
    return pl.pallas_call(
        body, name=name, grid=(T // tm,),
        in_specs=[pl.BlockSpec((tm, D), lambda i: (i, ZB_LRU)), pl.BlockSpec((HALO, D), _prev_halo(tm, ZB_LRU)),
                  pl.BlockSpec((4, D), fix2), pl.BlockSpec((1, D), fix2),
                  pl.BlockSpec((LRU_HEADS, LRU_HD, LRU_HD), fix3), pl.BlockSpec((1, D), fix2),
                  pl.BlockSpec((LRU_HEADS, LRU_HD, LRU_HD), fix3), pl.BlockSpec((1, D), fix2),
                  pl.BlockSpec((1, D), fix2), pl.BlockSpec((D, D), fix2)],
        out_specs=[pl.BlockSpec((tm, D), lambda i: (i, 0)), pl.BlockSpec((tm, D), lambda i: (i, 0))],
        out_shape=[jax.ShapeDtypeStruct((T, D), F32), jax.ShapeDtypeStruct((T, D), BF)],
        scratch_shapes=[pltpu.VMEM((tm, D), F32), pltpu.VMEM((tm, D), F32), pltpu.VMEM((8, D), F32)],
        compiler_params=_cp("arbitrary"))(z, z, cw, cb, wr, br, wi, bi, lam, wlo)


def lru_bwd(dz, dyl, z, h, cw, cb, wr, br, wi, bi, lam, wlo, *, tm, name):
    T = z.shape[0]
    tm = min(tm, T)
    nt = T // tm
    nch = tm // 8

    def body(dz_in, dy_ref, zm_ref, zh_ref, h_ref, hh_ref, cw_ref, cb_ref, wr_ref, br_ref, wi_ref, bi_ref, lam_ref,
             wlo_ref, dz_ref, dcw_ref, dcb_ref, dwr_ref, dbr_ref, dwi_ref, dbi_ref, dlam_ref,
             c_s, g_s, dh_s, dh_carry, a_carry, dv_carry):
        del dz_in
        i = pl.program_id(0)
        ti = nt - 1 - i

        @pl.when(i == 0)
        def _():
            dh_carry[...] = jnp.zeros_like(dh_carry)
            a_carry[...] = jnp.zeros_like(a_carry)
            dv_carry[...] = jnp.zeros_like(dv_carry)

        keep = jnp.where(ti == 0, 0.0, 1.0).astype(F32)
        ext = jnp.concatenate([zh_ref[...].astype(F32) * keep, zm_ref[...].astype(F32)], axis=0)
        cw_ = cw_ref[...]
        lam_ = lam_ref[...]
        v, vb, r, ig, a, mult, sp, shifted = _lru_gates(ext, cw_, cb_ref[...], wr_ref, br_ref[...], wi_ref,
                                                         bi_ref[...], lam_)
        rowt = lax.broadcasted_iota(jnp.int32, (tm, 1), 0)
        c_s[...] = jnp.where(rowt == tm - 1, a_carry[0:1, :], pltpu.roll(a, tm - 1, axis=0))
        g_s[...] = lax.dot_general(dy_ref[...], wlo_ref[...], (((1,), (1,)), ((), ())), preferred_element_type=F32)
        row = lax.broadcasted_iota(jnp.int32, (8, D), 0)

        def step(k, nxt):
            ci = nch - 1 - k
            sl = pl.ds(pl.multiple_of(ci * 8, 8), 8)
            cc, gg = c_s[sl, :], g_s[sl, :]
            for s in (1, 2, 4):
                m = row < 8 - s
                gg = gg + cc * jnp.where(m, pltpu.roll(gg, 8 - s, axis=0), 0.0)
                cc = cc * jnp.where(m, pltpu.roll(cc, 8 - s, axis=0), 1.0)
            dh = gg + cc * nxt
            dh_s[sl, :] = dh
            return jnp.broadcast_to(dh[0:1, :], (8, D))

        dh_carry[...] = lax.fori_loop(0, nch, step, dh_carry[...])
        a_carry[...] = jnp.broadcast_to(a[0:1, :], (8, D))
        dh = dh_s[...]
        hv = h_ref[...]
        hprev = jnp.where(rowt == 0, hh_ref[7:8, :] * keep, pltpu.roll(hv, 1, axis=0))
        iv = ig * v
        da = dh * hprev
        dmult = dh * iv
        div = dh * mult
        dlog = da * a - dmult * (a * a) / mult
        dr = dlog * (-LRU_C * sp)
        dlam = jnp.sum(dlog * r, axis=0, keepdims=True) * (LRU_C * _sigmoid(-lam_))
        di = div * v
        dv = div * ig
        drp = dr * r * (1.0 - r)
        dip = di * ig * (1.0 - ig)
        drb, dib = drp.astype(BF), dip.astype(BF)
        dvh, dwr, dwi = [], [], []
        nt_dims = (((1,), (1,)), ((), ()))
        tn_dims = (((0,), (0,)), ((), ()))
        for hd in range(LRU_HEADS):
            sl = slice(hd * LRU_HD, (hd + 1) * LRU_HD)
            dvh.append(lax.dot_general(drb[:, sl], wr_ref[hd], nt_dims, preferred_element_type=F32)
                       + lax.dot_general(dib[:, sl], wi_ref[hd], nt_dims, preferred_element_type=F32))
            dwr.append(lax.dot_general(vb[:, sl], drb[:, sl], tn_dims, preferred_element_type=F32))
            dwi.append(lax.dot_general(vb[:, sl], dib[:, sl], tn_dims, preferred_element_type=F32))
        dv = dv + jnp.concatenate(dvh, axis=1)
        dv_ext = jnp.concatenate([dv, dv_carry[...]], axis=0)
        dzl = cw_[3:4, :] * dv
        for k in range(3):
            dzl = dzl + cw_[k:k + 1, :] * pltpu.roll(dv_ext, tm + HALO - (3 - k), axis=0)[:tm]
        dz_ref[...] = dzl.astype(dz_ref.dtype)
        dv_carry[...] = dv[:HALO]
        dcw = jnp.concatenate([jnp.sum(dv * shifted[k], axis=0, keepdims=True) for k in range(4)], axis=0)
        dcb = jnp.sum(dv, axis=0, keepdims=True)
        dbr = jnp.sum(drp, axis=0, keepdims=True)
        dbi = jnp.sum(dip, axis=0, keepdims=True)

        @pl.when(i == 0)
        def _():
            dcw_ref[...] = dcw
            dcb_ref[...] = dcb
            dbr_ref[...] = dbr
            dbi_ref[...] = dbi
            dlam_ref[...] = dlam
            for hd in range(LRU_HEADS):
                dwr_ref[hd] = dwr[hd]
                dwi_ref[hd] = dwi[hd]

        @pl.when(i > 0)
        def _():
            dcw_ref[...] += dcw
            dcb_ref[...] += dcb
            dbr_ref[...] += dbr
            dbi_ref[...] += dbi
            dlam_ref[...] += dlam
            for hd in range(LRU_HEADS):
                dwr_ref[hd] += dwr[hd]
                dwi_ref[hd] += dwi[hd]

    fix2 = lambda i: (0, 0)
    fix3 = lambda i: (0, 0, 0)
    rev = lambda i: (nt - 1 - i, 0)
    vec = pl.BlockSpec((1, D), fix2)
    hw = pl.BlockSpec((LRU_HEADS, LRU_HD, LRU_HD), fix3)
    return pl.pallas_call(
        body, name=name, grid=(nt,),
        in_specs=[pl.BlockSpec(memory_space=pl.ANY),
                  pl.BlockSpec((tm, D), rev),
                  pl.BlockSpec((tm, D), lambda i: (nt - 1 - i, ZB_LRU)),
                  pl.BlockSpec((HALO, D), lambda i: (jnp.maximum((nt - 1 - i) * (tm // HALO) - 1, 0), ZB_LRU)),
                  pl.BlockSpec((tm, D), rev),
                  pl.BlockSpec((8, D), lambda i: (jnp.maximum((nt - 1 - i) * (tm // 8) - 1, 0), 0)),
                  pl.BlockSpec((4, D), fix2), vec, hw, vec, hw, vec, vec, pl.BlockSpec((D, D), fix2)],
        out_specs=[pl.BlockSpec((tm, D), lambda i: (nt - 1 - i, ZB_LRU)),
                   pl.BlockSpec((4, D), fix2), vec, hw, vec, hw, vec, vec],
        out_shape=[jax.ShapeDtypeStruct(dz.shape, dz.dtype), jax.ShapeDtypeStruct((4, D), F32),
                   jax.ShapeDtypeStruct((1, D), F32), jax.ShapeDtypeStruct((LRU_HEADS, LRU_HD, LRU_HD), F32),
                   jax.ShapeDtypeStruct((1, D), F32), jax.ShapeDtypeStruct((LRU_HEADS, LRU_HD, LRU_HD), F32),
                   jax.ShapeDtypeStruct((1, D), F32), jax.ShapeDtypeStruct((1, D), F32)],
        scratch_shapes=[pltpu.VMEM((tm, D), F32), pltpu.VMEM((tm, D), F32), pltpu.VMEM((tm, D), F32),
                        pltpu.VMEM((8, D), F32), pltpu.VMEM((8, D), F32), pltpu.VMEM((HALO, D), F32)],
        input_output_aliases={0: 0},
        compiler_params=_cp("arbitrary"))(dz, dyl, z, z, h, h, cw, cb, wr, br, wi, bi, lam, wlo)


def _sconv_cv(u_ext, sw):
    shifted = []
    cv = None
    for k in range(3):
        us = (u_ext if k == 2 else pltpu.roll(u_ext, 2 - k, axis=0))[HALO:]
        shifted.append(us)
        term = sw[k:k + 1, :] * us
        cv = term if cv is None else cv + term
    return cv, shifted


def sconv_fwd(z, sw, wso, *, tm, name):
    T = z.shape[0]
    tm = min(tm, T)

    def body(zm_ref, zh_ref, sw_ref, wso_ref, s_ref, y_ref):
        i = pl.program_id(0)
        keep = jnp.where(i == 0, 0.0, 1.0).astype(F32)
        zm = zm_ref[...].astype(F32)
        zh = zh_ref[...].astype(F32)
        u_ext = jnp.concatenate([zh[:, D:2 * D] * zh[:, 2 * D:] * keep, zm[:, D:2 * D] * zm[:, 2 * D:]], axis=0)
        cv, _ = _sconv_cv(u_ext, sw_ref[...])
        s = (zm[:, :D] * cv).astype(BF)
        s_ref[...] = s
        y_ref[...] = jnp.dot(s, wso_ref[...], preferred_element_type=F32).astype(BF)

    return pl.pallas_call(
        body, name=name, grid=(T // tm,),
        in_specs=[pl.BlockSpec((tm, 3 * D), lambda i: (i, ZB_SCONV)),
                  pl.BlockSpec((HALO, 3 * D), _prev_halo(tm, ZB_SCONV)),
                  pl.BlockSpec((3, D), lambda i: (0, 0)), pl.BlockSpec((D, D), lambda i: (0, 0))],
        out_specs=[pl.BlockSpec((tm, D), lambda i: (i, 0)), pl.BlockSpec((tm, D), lambda i: (i, 0))],
        out_shape=[jax.ShapeDtypeStruct((T, D), BF), jax.ShapeDtypeStruct((T, D), BF)],
        compiler_params=_cp("arbitrary"))(z, z, sw, wso)


def sconv_bwd(dz, dyc, z, sw, wso, *, tm, name):
    T = z.shape[0]
    tm = min(tm, T)
    nt = T // tm

    def body(dz_in, dy_ref, dyn_ref, zm_ref, zp_ref, zn_ref, sw_ref, wso_ref, dz_ref, dsw_ref):
        del dz_in
        i = pl.program_id(0)
        keep_p = jnp.where(i == 0, 0.0, 1.0).astype(F32)
        keep_n = jnp.where(i == nt - 1, 0.0, 1.0).astype(F32)
        sw_ = sw_ref[...]
        zm = zm_ref[...].astype(F32)
        zp = zp_ref[...].astype(F32)
        zb, zc, zh = zm[:, :D], zm[:, D:2 * D], zm[:, 2 * D:]
        u_ext = jnp.concatenate([zp[:, D:2 * D] * zp[:, 2 * D:] * keep_p, zc * zh], axis=0)
        cv, shifted = _sconv_cv(u_ext, sw_)
        dy_ext = jnp.concatenate([dy_ref[...], dyn_ref[...]], axis=0)
        ds_ext = lax.dot_general(dy_ext, wso_ref[...], (((1,), (1,)), ((), ())), preferred_element_type=F32)
        zb_ext = jnp.concatenate([zb, zn_ref[...][:, :D].astype(F32) * keep_n], axis=0)
        dcv_ext = ds_ext * zb_ext
        du = sw_[2:3, :] * dcv_ext[:tm]
        for k in range(2):
            du = du + sw_[k:k + 1, :] * pltpu.roll(dcv_ext, tm + HALO - (2 - k), axis=0)[:tm]
        dz_ref[...] = jnp.concatenate([ds_ext[:tm] * cv, du * zh, du * zc], axis=1).astype(dz_ref.dtype)
        dcv = dcv_ext[:tm]
        dsw = jnp.concatenate([jnp.sum(dcv * shifted[k], axis=0, keepdims=True) for k in range(3)], axis=0)

        @pl.when(i == 0)
        def _():
            dsw_ref[...] = dsw

        @pl.when(i > 0)
        def _():
            dsw_ref[...] += dsw

    return pl.pallas_call(
        body, name=name, grid=(nt,),
        in_specs=[pl.BlockSpec(memory_space=pl.ANY),
                  pl.BlockSpec((tm, D), lambda i: (i, 0)), pl.BlockSpec((HALO, D), _next_halo(tm, T, 0)),
                  pl.BlockSpec((tm, 3 * D), lambda i: (i, ZB_SCONV)),
                  pl.BlockSpec((HALO, 3 * D), _prev_halo(tm, ZB_SCONV)),
                  pl.BlockSpec((HALO, 3 * D), _next_halo(tm, T, ZB_SCONV)),
                  pl.BlockSpec((3, D), lambda i: (0, 0)), pl.BlockSpec((D, D), lambda i: (0, 0))],
        out_specs=[pl.BlockSpec((tm, 3 * D), lambda i: (i, ZB_SCONV)), pl.BlockSpec((3, D), lambda i: (0, 0))],
        out_shape=[jax.ShapeDtypeStruct(dz.shape, dz.dtype), jax.ShapeDtypeStruct((3, D), F32)],
        input_output_aliases={0: 0}, compiler_params=_cp("arbitrary"))(dz, dyc, dyc, z, z, z, sw, wso)


def merge_fwd(z, yp_pre, yl, yc, ps, *, tm, name):
    T = z.shape[0]
    tm = min(tm, T)

    def body(zg_ref, yp_ref, yl_ref, yc_ref, ps_ref, o_ref):
        gts = _sigmoid(zg_ref[...].astype(F32))
        m = (gts[:, :D] * (yp_ref[...].astype(F32) * ps_ref[...]) + gts[:, D:2 * D] * yl_ref[...].astype(F32)
             + gts[:, 2 * D:] * yc_ref[...].astype(F32))
        o_ref[...] = m.astype(o_ref.dtype)

    row = lambda i: (i, 0)
    return pl.pallas_call(
        body, name=name, grid=(T // tm,),
        in_specs=[pl.BlockSpec((tm, 3 * D), lambda i: (i, ZB_GATE)), pl.BlockSpec((tm, D), row),
                  pl.BlockSpec((tm, D), row), pl.BlockSpec((tm, D), row), pl.BlockSpec((1, D), lambda i: (0, 0))],
        out_specs=pl.BlockSpec((tm, D), row), out_shape=jax.ShapeDtypeStruct((T, D), BF),
        compiler_params=_cp("arbitrary"))(z, yp_pre, yl, yc, ps)


def merge_bwd(dm, z, yp_pre, yl, yc, ps, *, tm, name):
    T = z.shape[0]
    tm = min(tm, T)

    def body(dm_ref, zg_ref, yp_ref, yl_ref, yc_ref, ps_ref, dz_ref, dyp_ref, dyl_ref, dyc_ref):
        gts = _sigmoid(zg_ref[...].astype(F32))
        dmv = dm_ref[...].astype(F32)
        ys = (yp_ref[...].astype(F32) * ps_ref[...], yl_ref[...].astype(F32), yc_ref[...].astype(F32))
        outs = (dyp_ref, dyl_ref, dyc_ref)
        dgs = []
        for j in range(3):
            gj = gts[:, j * D:(j + 1) * D]
            outs[j][...] = (dmv * gj).astype(BF)
            dgs.append(dmv * ys[j] * gj * (1.0 - gj))
        dz_ref[...] = jnp.concatenate(dgs, axis=1).astype(dz_ref.dtype)

    row = lambda i: (i, 0)
    return pl.pallas_call(
        body, name=name, grid=(T // tm,),
        in_specs=[pl.BlockSpec((tm, D), row), pl.BlockSpec((tm, 3 * D), lambda i: (i, ZB_GATE)),
                  pl.BlockSpec((tm, D), row), pl.BlockSpec((tm, D), row), pl.BlockSpec((tm, D), row),
                  pl.BlockSpec((1, D), lambda i: (0, 0))],
        out_specs=[pl.BlockSpec((tm, 3 * D), lambda i: (i, ZB_GATE)), pl.BlockSpec((tm, D), row),
                   pl.BlockSpec((tm, D), row), pl.BlockSpec((tm, D), row)],
        out_shape=[jax.ShapeDtypeStruct((T, IN_COLS), BF), jax.ShapeDtypeStruct((T, D), BF),
                   jax.ShapeDtypeStruct((T, D), BF), jax.ShapeDtypeStruct((T, D), BF)],
        compiler_params=_cp("arbitrary"))(dm, z, yp_pre, yl, yc, ps)


def _attn_probs(qh, kh):
    s = lax.dot_general(qh, kh, (((1,), (1,)), ((), ())), preferred_element_type=F32) * (X_HD ** -0.5)
    e = jnp.exp(s - jnp.max(s, axis=-1, keepdims=True))
    return e / jnp.sum(e, axis=-1, keepdims=True)


def attn_fwd(xb, wq, kb, vb, *, tm, name):
    T = xb.shape[0]
    tm = min(tm, T)

    def body(x_ref, wq_ref, k_ref, v_ref, q_ref, o_ref):
        q = jnp.dot(x_ref[...], wq_ref[...], preferred_element_type=F32).astype(BF)
        q_ref[...] = q
        outs = []
        for h in range(X_HEADS):
            sl = slice(h * X_HD, (h + 1) * X_HD)
            p = _attn_probs(q[:, sl], k_ref[:, sl])
            outs.append(jnp.dot(p.astype(BF), v_ref[:, sl], preferred_element_type=F32))
        o_ref[...] = jnp.concatenate(outs, axis=1).astype(BF)

    row = lambda i: (i, 0)
    fix = lambda i: (0, 0)
    return pl.pallas_call(
        body, name=name, grid=(T // tm,),
        in_specs=[pl.BlockSpec((tm, D), row), pl.BlockSpec((D, D), fix), pl.BlockSpec((N_MEM, D), fix),
                  pl.BlockSpec((N_MEM, D), fix)],
        out_specs=[pl.BlockSpec((tm, D), row), pl.BlockSpec((tm, D), row)],
        out_shape=[jax.ShapeDtypeStruct((T, D), BF), jax.ShapeDtypeStruct((T, D), BF)],
        compiler_params=_cp("arbitrary"))(xb, wq, kb, vb)


def attn_bwd(dxa, wo, q, kb, vb, *, tm, name):
    T = q.shape[0]
    tm = min(tm, T)

    def body(d_ref, wo_ref, q_ref, k_ref, v_ref, dq_ref, dk_ref, dv_ref):
        i = pl.program_id(0)
        do = lax.dot_general(d_ref[...], wo_ref[...], (((1,), (1,)), ((), ())),
                             preferred_element_type=F32).astype(BF)
        q = q_ref[...]
        dqs, dks, dvs = [], [], []
        for h in range(X_HEADS):
            sl = slice(h * X_HD, (h + 1) * X_HD)
            kh, vh = k_ref[:, sl], v_ref[:, sl]
            p = _attn_probs(q[:, sl], kh)
            dp = lax.dot_general(do[:, sl], vh, (((1,), (1,)), ((), ())), preferred_element_type=F32)
            ds = (p * (dp - jnp.sum(dp * p, axis=-1, keepdims=True)) * (X_HD ** -0.5)).astype(BF)
            dqs.append(jnp.dot(ds, kh, preferred_element_type=F32))
            dks.append(lax.dot_general(ds, q[:, sl], (((0,), (0,)), ((), ())), preferred_element_type=F32))
            dvs.append(lax.dot_general(p.astype(BF), do[:, sl], (((0,), (0,)), ((), ())),
                                       preferred_element_type=F32))
        dq_ref[...] = jnp.concatenate(dqs, axis=1).astype(BF)
        dk = jnp.concatenate(dks, axis=1)
        dv = jnp.concatenate(dvs, axis=1)

        @pl.when(i == 0)
        def _():
            dk_ref[...] = dk
            dv_ref[...] = dv

        @pl.when(i > 0)
        def _():
            dk_ref[...] += dk
            dv_ref[...] += dv

    row = lambda i: (i, 0)
    fix = lambda i: (0, 0)
    return pl.pallas_call(
        body, name=name, grid=(T // tm,),
        in_specs=[pl.BlockSpec((tm, D), row), pl.BlockSpec((D, D), fix), pl.BlockSpec((tm, D), row),
                  pl.BlockSpec((N_MEM, D), fix), pl.BlockSpec((N_MEM, D), fix)],
        out_specs=[pl.BlockSpec((tm, D), row), pl.BlockSpec((N_MEM, D), fix), pl.BlockSpec((N_MEM, D), fix)],
        out_shape=[jax.ShapeDtypeStruct((T, D), BF), jax.ShapeDtypeStruct((N_MEM, D), F32),
                   jax.ShapeDtypeStruct((N_MEM, D), F32)],
        compiler_params=_cp("arbitrary"))(dxa, wo, q, kb, vb)


def swiglu_fwd(gu, *, tm, name):
    T = gu.shape[0]
    tm = min(tm, T)

    def body(g_ref, u_ref, o_ref):
        g = g_ref[...].astype(F32)
        o_ref[...] = (g * _sigmoid(g) * u_ref[...].astype(F32)).astype(BF)

    return pl.pallas_call(
        body, name=name, grid=(T // tm,),
        in_specs=[pl.BlockSpec((tm, D_FF), lambda i: (i, 0)), pl.BlockSpec((tm, D_FF), lambda i: (i, 1))],
        out_specs=pl.BlockSpec((tm, D_FF), lambda i: (i, 0)), out_shape=jax.ShapeDtypeStruct((T, D_FF), BF),
        compiler_params=_cp("arbitrary"))(gu, gu)


def swiglu_bwd(dh, gu, *, tm, name):
    T = gu.shape[0]
    tm = min(tm, T)

    def body(dh_ref, g_ref, u_ref, o_ref):
        g = g_ref[...].astype(F32)
        u = u_ref[...].astype(F32)
        dhv = dh_ref[...].astype(F32)
        sg = _sigmoid(g)
        o_ref[:, :D_FF] = (dhv * u * sg * (1.0 + g * (1.0 - sg))).astype(BF)
        o_ref[:, D_FF:] = (dhv * g * sg).astype(BF)

    return pl.pallas_call(
        body, name=name, grid=(T // tm,),
        in_specs=[pl.BlockSpec((tm, D_FF), lambda i: (i, 0)), pl.BlockSpec((tm, D_FF), lambda i: (i, 0)),
                  pl.BlockSpec((tm, D_FF), lambda i: (i, 1))],
        out_specs=pl.BlockSpec((tm, 2 * D_FF), lambda i: (i, 0)),
        out_shape=jax.ShapeDtypeStruct((T, 2 * D_FF), BF), compiler_params=_cp("arbitrary"))(dh, gu, gu)


TM_MM = 1024
TM_EW = 512
TM_SEQ = 256
TT_DW = 2048


def _layer_fwd(l, x, xb, kb, vb, W):
    n = f"l{l}_"
    sv = {'x0': x if xb is None else xb}
    z = mm_nn(sv['x0'], W['w_in'][l], W['b_in'][l], out_dtype=BF, tm=TM_MM, tn=1024, name=n + "in_proj")
    yp = pool_fwd(z, W['pool_w'][l], tm=TM_SEQ, name=n + "pool_fwd")
    h, yl = lru_fwd(z, W['lru_conv_w'][l], W['lru_conv_b'][l], W['lru_w_r'][l], W['lru_b_r'][l], W['lru_w_i'][l],
                    W['lru_b_i'][l], W['lru_lambda'][l], W['lru_w_out'][l], tm=TM_SEQ, name=n + "lru_fwd")
    s, yc = sconv_fwd(z, W['sconv_w'][l], W['sconv_w_out'][l], tm=TM_SEQ, name=n + "sconv_fwd")
    merged = merge_fwd(z, yp, yl, yc, W['pool_scale'][l], tm=TM_EW, name=n + "merge_fwd")
    x1, x1b, xh1, rs1 = mm_res_ln(merged, W['w_mix_out'][l], x, W['ln_g'][l][0:1], W['ln_b'][l][0:1], tm=TM_EW,
                                  name=n + "mix_out_ln")
    q, o = attn_fwd(x1b, W['xa_w_q'][l], kb, vb, tm=TM_EW, name=n + "attn_fwd")
    x2, x2b, xh2, rs2 = mm_res_ln(o, W['xa_w_o'][l], x1, W['ln_g'][l][1:2], W['ln_b'][l][1:2], tm=TM_EW,
                                  name=n + "attn_out_ln")
    gu = mm_nn(x2b, W['ffn_w_gu'][l], None, out_dtype=BF, tm=TM_MM, tn=1408, name=n + "ffn_in")
    hdn = swiglu_fwd(gu, tm=TM_EW, name=n + "swiglu_fwd")
    x3, x3b, xh3, rs3 = mm_res_ln(hdn, W['ffn_w_down'][l], x2, W['ln_g'][l][2:3], W['ln_b'][l][2:3], tm=TM_EW,
                                  name=n + "ffn_out_ln")
    sv.update(z=z, yp=yp, h=h, yl=yl, s=s, yc=yc, merged=merged, x1b=x1b, xh1=xh1, rs1=rs1, q=q, o=o, x2b=x2b,
              xh2=xh2, rs2=rs2, gu=gu, hdn=hdn, xh3=xh3, rs3=rs3)
    return x3, x3b, sv


def _layer_bwd(l, dx3, sv, memb, kb, vb, W, loss_from=None):
    n = f"l{l}_"
    G = {}
    res = ln_bwd(dx3, sv['xh3'], sv['rs3'], W['ln_g'][l][2:3], tm=TM_EW, name=n + "ln3_bwd", loss_from=loss_from)
    dp3, dp3b, dg3, db3 = res[:4]
    loss = res[4] if loss_from is not None else None
    dhdn = mm_nt(dp3b, W['ffn_w_down'][l], None, out_dtype=BF, tm=TM_EW, tc=D, name=n + "ffn_down_dx")
    dgu = swiglu_bwd(dhdn, sv['gu'], tm=TM_EW, name=n + "swiglu_bwd")
    dx2 = mm_nt(dgu, W['ffn_w_gu'][l], dp3, out_dtype=F32, tm=TM_MM, tc=1408, name=n + "ffn_in_dx")
    G['ffn_w_gu'] = mm_tn(sv['x2b'], dgu, out_dtype=BF, tk=1024, tn=1408, tt=TT_DW,name=n + "ffn_in_dw")
    G['ffn_w_down'] = mm_tn(sv['hdn'], dp3b, out_dtype=BF, tk=1408, tn=1024, tt=TT_DW,name=n + "ffn_down_dw")

    dp2, dp2b, dg2, db2 = ln_bwd(dx2, sv['xh2'], sv['rs2'], W['ln_g'][l][1:2], tm=TM_EW, name=n + "ln2_bwd")
    dq, dk, dv = attn_bwd(dp2b, W['xa_w_o'][l], sv['q'], kb, vb, tm=TM_EW, name=n + "attn_bwd")
    dx1 = mm_nt(dq, W['xa_w_q'][l], dp2, out_dtype=F32, tm=TM_MM, tc=D, name=n + "attn_q_dx")
    G['xa_w_o'] = mm_tn(sv['o'], dp2b, out_dtype=BF, tk=1024, tn=1024, tt=TT_DW,name=n + "attn_o_dw")
    G['xa_w_q'] = mm_tn(sv['x1b'], dq, out_dtype=BF, tk=1024, tn=1024, tt=TT_DW,name=n + "attn_q_dw")
    G['xa_w_k'] = mm_tn(memb, dk, out_dtype=BF, tk=1024, tn=1024, tt=N_MEM, name=n + "attn_k_dw")
    G['xa_w_v'] = mm_tn(memb, dv, out_dtype=BF, tk=1024, tn=1024, tt=N_MEM, name=n + "attn_v_dw")

    dp1, dp1b, dg1, db1 = ln_bwd(dx1, sv['xh1'], sv['rs1'], W['ln_g'][l][0:1], tm=TM_EW, name=n + "ln1_bwd")
    dmerged = mm_nt(dp1b, W['w_mix_out'][l], None, out_dtype=BF, tm=TM_MM, tc=D, name=n + "mix_out_dx")
    G['w_mix_out'] = mm_tn(sv['merged'], dp1b, out_dtype=BF, tk=1024, tn=1024, tt=TT_DW,name=n + "mix_out_dw")
    z = sv['z']
    dz, dyp, dyl, dyc = merge_bwd(dmerged, z, sv['yp'], sv['yl'], sv['yc'], W['pool_scale'][l], tm=TM_EW,
                                  name=n + "merge_bwd")
    dz, G['pool_w'], G['pool_scale'] = pool_bwd(dz, dyp, sv['yp'], z, W['pool_w'][l], W['pool_scale'][l],
                                                tm=TM_SEQ, name=n + "pool_bwd")
    (dz, G['lru_conv_w'], G['lru_conv_b'], G['lru_w_r'], G['lru_b_r'], G['lru_w_i'], G['lru_b_i'],
     G['lru_lambda']) = lru_bwd(dz, dyl, z, sv['h'], W['lru_conv_w'][l], W['lru_conv_b'][l], W['lru_w_r'][l],
                                W['lru_b_r'][l], W['lru_w_i'][l], W['lru_b_i'][l], W['lru_lambda'][l],
                                W['lru_w_out'][l], tm=TM_SEQ, name=n + "lru_bwd")
    dz, G['sconv_w'] = sconv_bwd(dz, dyc, z, W['sconv_w'][l], W['sconv_w_out'][l], tm=TM_SEQ, name=n + "sconv_bwd")
    G['lru_w_out'] = mm_tn(sv['h'], dyl, out_dtype=BF, tk=1024, tn=1024, tt=TT_DW,name=n + "lru_out_dw")
    G['sconv_w_out'] = mm_tn(sv['s'], dyc, out_dtype=BF, tk=1024, tn=1024, tt=TT_DW,name=n + "sconv_out_dw")
    dx0 = mm_nt(dz, W['w_in'][l], dp1, out_dtype=F32, tm=TM_MM, tc=2048, name=n + "in_proj_dx")
    G['w_in'], G['b_in'] = mm_tn(sv['x0'], dz, out_dtype=BF, tk=1024, tn=1024, tt=TT_DW,name=n + "in_proj_dw",
                                 colsum=True)
    G['ln_g'] = jnp.concatenate([dg1, dg2, dg3], axis=0)
    G['ln_b'] = jnp.concatenate([db1, db2, db3], axis=0)
    return dx0, G, loss


def local_step(x, mem, target, W):
    memb = mem.astype(BF)
    saves, kvs = [], []
    xf, xb = x, None
    for l in range(DEPTH):
        kb = mm_nn(memb, W['xa_w_k'][l], None, out_dtype=BF, tm=N_MEM, tn=1024, name=f"l{l}_mem_k")
        vb = mm_nn(memb, W['xa_w_v'][l], None, out_dtype=BF, tm=N_MEM, tn=1024, name=f"l{l}_mem_v")
        xf, xb, sv = _layer_fwd(l, xf, xb, kb, vb, W)
        saves.append(sv)
        kvs.append((kb, vb))
    grads = [None] * DEPTH
    dx, loss = None, None
    for l in reversed(range(DEPTH)):
        lf = (W['ln_b'][l][2:3], target) if l == DEPTH - 1 else None
        dx, grads[l], ls = _layer_bwd(l, dx, saves[l], memb, kvs[l][0], kvs[l][1], W, loss_from=lf)
        if ls is not None:
            loss = ls
    return loss, dx, grads


def _coords():
    return lax.axis_index("x"), lax.axis_index("y"), lax.axis_index("c")


FLIPS = ((1, 0), (0, 1), (1, 1))
SQUARES = ('lru_w_out', 'sconv_w_out', 'w_mix_out', 'xa_w_q', 'xa_w_k', 'xa_w_v', 'xa_w_o')
LAYER_SHAPE = {'w_in': (D, IN_COLS), 'pool_w': (4, POOL_GD, POOL_GD), 'ffn_w_gate': (4, D, D_FF // 4),
               'ffn_w_up': (4, D, D_FF // 4), 'ffn_w_down': (D_FF, D), **{n: (D, D) for n in SQUARES}}
PIECES = ('w_in', 'pool_w') + SQUARES + ('ffn_w_gate', 'ffn_w_up', 'ffn_w_down')


def _win(name, ref, k):
    if name == 'w_in':
        return ref.at[:, pl.ds(pl.multiple_of(((2 * k + 6) % 8) * D, D), 2 * D)]
    if name == 'pool_w':
        return ref.at[:, pl.ds(pl.multiple_of(k * (POOL_GD // 4), POOL_GD // 4), POOL_GD // 4), :]
    if name in ('ffn_w_gate', 'ffn_w_up'):
        return ref.at[k]
    rows = LAYER_SHAPE[name][0] // 4
    return ref.at[pl.ds(pl.multiple_of(k * rows, 16), rows), :]


def gather_weights(shards, small):
    n_p = len(PIECES)

    def body(*refs):
        srcs = dict(zip(PIECES, refs[:n_p]))
        small_ref = refs[n_p]
        outs = [dict(zip(PIECES, refs[n_p + 1 + l * n_p:n_p + 1 + (l + 1) * n_p])) for l in range(DEPTH)]
        gs_ref = refs[n_p + 1 + DEPTH * n_p]
        ici_send, ici_recv, d2d_send, d2d_recv, own_send, own_recv = refs[n_p + 2 + DEPTH * n_p:]
        x, y, c = _coords()
        me = 2 * x + y
        sib = (x, y, 1 - c)

        def own_copies():
            cps = []
            li = 0
            for n in PIECES:
                for l in range(DEPTH):
                    cps.append(pltpu.make_async_remote_copy(
                        src_ref=srcs[n].at[l], dst_ref=_win(n, outs[l][n], me), send_sem=own_send.at[li],
                        recv_sem=own_recv.at[li], device_id=sib, device_id_type=MESH))
                    li += 1
            cps.append(pltpu.make_async_remote_copy(
                src_ref=small_ref, dst_ref=gs_ref.at[me], send_sem=own_send.at[li], recv_sem=own_recv.at[li],
                device_id=sib, device_id_type=MESH))
            return cps

        def run(lc):
            sends = []
            for j, (fx, fy) in enumerate(FLIPS):
                peer = (x ^ fx, y ^ fy, c)
                for p, n in enumerate(PIECES):
                    k = 3 * p + j
                    sends.append(pltpu.make_async_remote_copy(
                        src_ref=srcs[n].at[lc], dst_ref=_win(n, outs[lc][n], me), send_sem=ici_send.at[k],
                        recv_sem=ici_recv.at[k], device_id=peer, device_id_type=MESH))
                k = 3 * n_p + j
                sends.append(pltpu.make_async_remote_copy(
                    src_ref=small_ref, dst_ref=gs_ref.at[me], send_sem=ici_send.at[k], recv_sem=ici_recv.at[k],
                    device_id=peer, device_id_type=MESH))
            own = own_copies()
            for cp in sends + own:
                cp.start()
            for j, (fx, fy) in enumerate(FLIPS):
                other = 2 * (x ^ fx) + (y ^ fy)
                for p, n in enumerate(PIECES):
                    k = 3 * p + j
                    w = _win(n, outs[lc][n], other)
                    pltpu.make_async_remote_copy(src_ref=srcs[n].at[lc], dst_ref=w, send_sem=ici_send.at[k],
                                                 recv_sem=ici_recv.at[k], device_id=sib,
                                                 device_id_type=MESH).wait_recv()
                    fw = pltpu.make_async_remote_copy(src_ref=w, dst_ref=w, send_sem=d2d_send.at[k],
                                                      recv_sem=d2d_recv.at[k], device_id=sib, device_id_type=MESH)
                    fw.start()
                    sends.append(fw)
                k = 3 * n_p + j
                pltpu.make_async_remote_copy(src_ref=small_ref, dst_ref=gs_ref.at[other], send_sem=ici_send.at[k],
                                             recv_sem=ici_recv.at[k], device_id=sib, device_id_type=MESH).wait_recv()
            for j, (fx, fy) in enumerate(FLIPS):
                other = 2 * (x ^ fx) + (y ^ fy)
                for p, n in enumerate(PIECES):
                    k = 3 * p + j
                    w = _win(n, outs[1 - lc][n], other)
                    pltpu.make_async_remote_copy(src_ref=w, dst_ref=w, send_sem=d2d_send.at[k],
                                                 recv_sem=d2d_recv.at[k], device_id=sib,
                                                 device_id_type=MESH).wait_recv()
            for cp in sends:
                cp.wait_send()
            for cp in own:
                cp.wait()

        @pl.when(c == 0)
        def _():
            run(0)

        @pl.when(c == 1)
        def _():
            run(1)

    hbm = pl.BlockSpec(memory_space=pl.ANY)
    n_out = DEPTH * n_p + 1
    res = pl.pallas_call(
        body, name="gather_weights", in_specs=[hbm] * (n_p + 1), out_specs=[hbm] * n_out,
        out_shape=[jax.ShapeDtypeStruct(LAYER_SHAPE[n], BF) for _ in range(DEPTH) for n in PIECES]
        + [jax.ShapeDtypeStruct((4,) + small.shape, small.dtype)],
        scratch_shapes=[pltpu.SemaphoreType.DMA((3 * n_p + 3,)), pltpu.SemaphoreType.DMA((3 * n_p + 3,)),
                        pltpu.SemaphoreType.DMA((3 * n_p,)), pltpu.SemaphoreType.DMA((3 * n_p,)),
                        pltpu.SemaphoreType.DMA((DEPTH * n_p + 1,)), pltpu.SemaphoreType.DMA((DEPTH * n_p + 1,))],
    )(*[shards[n] for n in PIECES], small)
    full = {n: [res[l * n_p + p] for l in range(DEPTH)] for p, n in enumerate(PIECES)}
    return full, res[DEPTH * n_p]


def split_layers(p0, p1):
    n = len(p0)

    def body(*refs):
        a0, a1 = refs[:n], refs[n:2 * n]
        theirs = refs[2 * n:3 * n]
        send_sems, recv_sems = refs[3 * n:]
        x, y, c = _coords()
        sib = (x, y, 1 - c)

        def give(arrs):
            for i in range(n):
                pltpu.make_async_remote_copy(src_ref=arrs[i], dst_ref=theirs[i], send_sem=send_sems.at[i],
                                             recv_sem=recv_sems.at[i], device_id=sib, device_id_type=MESH).start()

        @pl.when(c == 0)
        def _():
            give(a1)

        @pl.when(c == 1)
        def _():
            give(a0)

        for i in range(n):
            pltpu.make_async_remote_copy(src_ref=a0[i], dst_ref=theirs[i], send_sem=send_sems.at[i],
                                         recv_sem=recv_sems.at[i], device_id=sib, device_id_type=MESH).wait()

    hbm = pl.BlockSpec(memory_space=pl.ANY)
    return pl.pallas_call(
        body, name="split_layers", in_specs=[hbm] * (2 * n), out_specs=[hbm] * n,
        out_shape=[jax.ShapeDtypeStruct(a.shape, a.dtype) for a in p0],
        scratch_shapes=[pltpu.SemaphoreType.DMA((n,)), pltpu.SemaphoreType.DMA((n,))],
    )(*p0, *p1)


def exchange_chips(q, qsmall):
    n_p = len(PIECES)

    def body(*refs):
        srcs = dict(zip(PIECES, refs[:n_p]))
        s_ref = refs[n_p]
        outs = dict(zip(PIECES, refs[n_p + 1:2 * n_p + 1]))
        so_ref = refs[2 * n_p + 1]
        send_sems, recv_sems = refs[2 * n_p + 2:]
        x, y, c = _coords()
        me = 2 * x + y
        sends = []
        for j, (fx, fy) in enumerate(FLIPS):
            peer = (x ^ fx, y ^ fy, c)
            other = 2 * (x ^ fx) + (y ^ fy)
            for p, n in enumerate(PIECES):
                k = 3 * p + j
                sends.append(pltpu.make_async_remote_copy(
                    src_ref=_win(n, srcs[n], other), dst_ref=outs[n].at[j], send_sem=send_sems.at[k],
                    recv_sem=recv_sems.at[k], device_id=peer, device_id_type=MESH))
            k = 3 * n_p + j
            sends.append(pltpu.make_async_remote_copy(
                src_ref=s_ref, dst_ref=so_ref.at[j], send_sem=send_sems.at[k], recv_sem=recv_sems.at[k],
                device_id=peer, device_id_type=MESH))
        for cp in sends:
            cp.start()
        for cp in sends:
            cp.wait()

    hbm = pl.BlockSpec(memory_space=pl.ANY)
    res = pl.pallas_call(
        body, name="exchange_chips", in_specs=[hbm] * (n_p + 1), out_specs=[hbm] * (n_p + 1),
        out_shape=[jax.ShapeDtypeStruct((3,) + _shard_shape(n), BF) for n in PIECES]
        + [jax.ShapeDtypeStruct((3,) + qsmall.shape, qsmall.dtype)],
        scratch_shapes=[pltpu.SemaphoreType.DMA((3 * n_p + 3,)), pltpu.SemaphoreType.DMA((3 * n_p + 3,))],
    )(*[q[n] for n in PIECES], qsmall)
    return dict(zip(PIECES, res[:n_p])), res[n_p]


def _shard_shape(n):
    shp = LAYER_SHAPE[n]
    if n == 'w_in':
        return (shp[0], shp[1] // 4)
    if n == 'pool_w':
        return (shp[0], shp[1] // 4, shp[2])
    if n in ('ffn_w_gate', 'ffn_w_up'):
        return shp[1:]
    return (shp[0] // 4, shp[1])


def swap_cores(s):
    n = len(s)

    def body(*refs):
        srcs, outs = refs[:n], refs[n:2 * n]
        send_sems, recv_sems = refs[2 * n:]
        x, y, c = _coords()
        cps = [pltpu.make_async_remote_copy(src_ref=srcs[i], dst_ref=outs[i], send_sem=send_sems.at[i],
                                            recv_sem=recv_sems.at[i], device_id=(x, y, 1 - c), device_id_type=MESH)
               for i in range(n)]
        for cp in cps:
            cp.start()
        for cp in cps:
            cp.wait()

    hbm = pl.BlockSpec(memory_space=pl.ANY)
    return pl.pallas_call(
        body, name="swap_cores", in_specs=[hbm] * n, out_specs=[hbm] * n,
        out_shape=[jax.ShapeDtypeStruct(a.shape, a.dtype) for a in s],
        scratch_shapes=[pltpu.SemaphoreType.DMA((n,)), pltpu.SemaphoreType.DMA((n,))],
    )(*s)


def add_cores(p0, p1, theirs, *, out_dtype, name):
    shp = p0.shape
    args = [t.reshape(-1, shp[-1]) for t in (p0, p1, theirs)]
    R, C = args[0].shape
    tr = _row_tile(R, C)

    def body(a0_ref, a1_ref, t_ref, o_ref):
        c = lax.axis_index("c")
        t = t_ref[...].astype(F32)

        @pl.when(c == 0)
        def _():
            o_ref[...] = (a0_ref[...].astype(F32) + t).astype(o_ref.dtype)

        @pl.when(c == 1)
        def _():
            o_ref[...] = (a1_ref[...].astype(F32) + t).astype(o_ref.dtype)

    spec = pl.BlockSpec((tr, C), lambda i: (i, 0))
    out = pl.pallas_call(body, name=name, grid=(R // tr,), in_specs=[spec] * 3, out_specs=spec,
                         out_shape=jax.ShapeDtypeStruct((R, C), out_dtype), compiler_params=_cp("arbitrary"))(*args)
    return out.reshape(shp)


def sum_chips(name, q, recv3, chip):
    shard = _shard_shape(name)
    zero = (0,) * len(shard)
    if name == 'w_in':
        tr = 128
        grid = (shard[0] // tr,)
        qspec = pl.BlockSpec((tr, shard[1]), lambda i, me: (i, (me[0] + 3) % 4))
        rspec = pl.BlockSpec((3, tr, shard[1]), lambda i, me: (0, i, 0))
        ospec = pl.BlockSpec((tr, shard[1]), lambda i, me: (i, 0))
    else:
        grid = (1,)
        rspec = pl.BlockSpec((3,) + shard, lambda i, me: (0,) + zero)
        ospec = pl.BlockSpec(shard, lambda i, me: zero)
        if name == 'pool_w':
            qspec = pl.BlockSpec(shard, lambda i, me: (0, me[0], 0))
        elif name in ('ffn_w_gate', 'ffn_w_up'):
            qspec = pl.BlockSpec((None,) + shard, lambda i, me: (me[0], 0, 0))
        else:
            qspec = pl.BlockSpec(shard, lambda i, me: (me[0], 0))

    def body(me_ref, q_ref, r_ref, o_ref):
        del me_ref
        acc = q_ref[...].astype(F32)
        for j in range(3):
            acc = acc + r_ref[j].astype(F32)
        o_ref[...] = acc

    return pl.pallas_call(
        body, name="sum_chips_" + name,
        grid_spec=pltpu.PrefetchScalarGridSpec(num_scalar_prefetch=1, grid=grid, in_specs=[qspec, rspec],
                                               out_specs=ospec),
        out_shape=jax.ShapeDtypeStruct(shard, F32), compiler_params=_cp("arbitrary"))(chip, q, recv3)


def sum_chips_small(q, recv3, chip):
    r, C = q.shape
    slot_of_xor = {2: 0, 1: 1, 3: 2}

    def body(me_ref, q_ref, r_ref, o_ref):
        me = me_ref[0]
        acc = None
        for k in range(4):
            kx = k ^ me
            term = q_ref[...]
            for xv, j in slot_of_xor.items():
                term = jnp.where(kx == xv, r_ref[j], term)
            acc = term if acc is None else acc + term
        o_ref[...] = acc

    return pl.pallas_call(
        body, name="sum_chips_small",
        grid_spec=pltpu.PrefetchScalarGridSpec(
            num_scalar_prefetch=1, grid=(1,),
            in_specs=[pl.BlockSpec((r, C), lambda i, me: (0, 0)), pl.BlockSpec((3, r, C), lambda i, me: (0, 0, 0))],
            out_specs=pl.BlockSpec((r, C), lambda i, me: (0, 0))),
        out_shape=jax.ShapeDtypeStruct((r, C), F32), compiler_params=_cp("arbitrary"))(chip, q, recv3)


def _row_tile(R, C):
    for cand in (1024, 512, 256, 128, 64, 32, 16):
        if R % cand == 0 and cand * C * 4 <= 2 * 1024 * 1024:
            return cand
    return R


def sum_slots(a, *, name):
    n = a.shape[0]
    shp = a.shape[1:]
    a3 = a.reshape(n, -1, shp[-1])
    R, C = a3.shape[1:]
    tr = _row_tile(R, C * n // 2)

    def body(a_ref, o_ref):
        acc = a_ref[0].astype(F32)
        for k in range(1, n):
            acc = acc + a_ref[k].astype(F32)
        o_ref[...] = acc

    out = pl.pallas_call(
        body, name=name, grid=(R // tr,), in_specs=[pl.BlockSpec((n, tr, C), lambda i: (0, i, 0))],
        out_specs=pl.BlockSpec((tr, C), lambda i: (i, 0)), out_shape=jax.ShapeDtypeStruct((R, C), F32),
        compiler_params=_cp("arbitrary"))(a3)
    return out.reshape(shp)


def _adamw_math(w, g, m, v):
    mn = ADAM_B1 * m + (1.0 - ADAM_B1) * g
    vn = ADAM_B2 * v + (1.0 - ADAM_B2) * (g * g)
    m_hat = mn / (1.0 - ADAM_B1 ** ADAM_STEP)
    v_hat = vn / (1.0 - ADAM_B2 ** ADAM_STEP)
    return -ADAM_LR * (m_hat / (jnp.sqrt(v_hat) + ADAM_EPS) + ADAM_WD * w), mn, vn


def adamw(w, g, m, v, *, name):
    shp = w.shape
    args = [t.reshape(-1, shp[-1]) for t in (w, g, m, v)]
    R, C = args[0].shape
    tr = _row_tile(R, C)

    def body(w_ref, g_ref, m_ref, v_ref, d_ref, mo_ref, vo_ref):
        d_ref[...], mo_ref[...], vo_ref[...] = _adamw_math(w_ref[...], g_ref[...], m_ref[...], v_ref[...])

    spec = pl.BlockSpec((tr, C), lambda i: (i, 0))
    res = pl.pallas_call(
        body, name=name, grid=(R // tr,), in_specs=[spec] * 4, out_specs=[spec] * 3,
        out_shape=[jax.ShapeDtypeStruct((R, C), F32)] * 3, compiler_params=_cp("arbitrary"))(*args)
    return [r.reshape(shp) for r in res]


def adamw_layers(w, g_mine, g_theirs, m, v, *, name):
    shp = w.shape
    three = (DEPTH, -1, shp[-1])
    w3, m3, v3 = [t.reshape(three) for t in (w, m, v)]
    ga, gb = [t.reshape(-1, shp[-1]) for t in (g_mine, g_theirs)]
    R, C = ga.shape
    tr = _row_tile(R, C)

    def body(w_ref, ga_ref, gb_ref, m_ref, v_ref, g_ref, d_ref, mo_ref, vo_ref):
        mine = pl.program_id(0) == lax.axis_index("c")
        g = jnp.where(mine, ga_ref[...], gb_ref[...])
        g_ref[...] = g
        d_ref[...], mo_ref[...], vo_ref[...] = _adamw_math(w_ref[...], g, m_ref[...], v_ref[...])

    lay = pl.BlockSpec((None, tr, C), lambda l, i: (l, i, 0))
    one = pl.BlockSpec((tr, C), lambda l, i: (i, 0))
    res = pl.pallas_call(
        body, name=name, grid=(DEPTH, R // tr), in_specs=[lay, one, one, lay, lay], out_specs=[lay] * 4,
        out_shape=[jax.ShapeDtypeStruct(w3.shape, F32)] * 4, compiler_params=_cp("arbitrary", "arbitrary"))(
            w3, ga, gb, m3, v3)
    return [r.reshape(shp) for r in res]


def _local_shape(name, full_shape):
    shp = list(full_shape)
    ax = BIG_SHARDED.get(name, SMALL_SHARDED.get(name))
    if ax is not None:
        shp[ax] //= 4
    return tuple(shp)


FULL_SHAPES = {
    'w_in': (DEPTH, D, IN_COLS), 'b_in': (DEPTH, IN_COLS), 'pool_w': (DEPTH, 4, POOL_GD, POOL_GD),
    'pool_scale': (DEPTH, D), 'lru_conv_w': (DEPTH, 4, D), 'lru_conv_b': (DEPTH, D),
    'lru_w_r': (DEPTH, LRU_HEADS, LRU_HD, LRU_HD), 'lru_b_r': (DEPTH, D),
    'lru_w_i': (DEPTH, LRU_HEADS, LRU_HD, LRU_HD), 'lru_b_i': (DEPTH, D), 'lru_lambda': (DEPTH, D),
    'lru_w_out': (DEPTH, D, D), 'sconv_w': (DEPTH, 3, D), 'sconv_w_out': (DEPTH, D, D), 'w_mix_out': (DEPTH, D, D),
    'xa_w_q': (DEPTH, D, D), 'xa_w_k': (DEPTH, D, D), 'xa_w_v': (DEPTH, D, D), 'xa_w_o': (DEPTH, D, D),
    'ffn_w_gate': (DEPTH, D, D_FF), 'ffn_w_up': (DEPTH, D, D_FF), 'ffn_w_down': (DEPTH, D_FF, D),
    'ln_g': (DEPTH, 3, D), 'ln_b': (DEPTH, 3, D)}


def _pack(arrs, names, width, dtype, row_mult):
    flat = jnp.concatenate([arrs[n].astype(dtype).reshape(-1) for n in names])
    pad = (-flat.shape[0]) % (width * row_mult)
    if pad:
        flat = jnp.concatenate([flat, jnp.zeros((pad,), dtype)])
    return flat.reshape(-1, width)


def _unpack(flat2d, names, shapes):
    flat = flat2d.reshape(-1)
    out, off = {}, 0
    for n in names:
        size = 1
        for s in shapes[n]:
            size *= s
        out[n] = flat[off:off + size].reshape(shapes[n])
        off += size
    return out


def _gathered_full(g4, names, sharded_axis):
    loc_shapes = {n: _local_shape(n, FULL_SHAPES[n]) for n in names}
    per = [_unpack(g4[k], names, loc_shapes) for k in range(4)]
    return {n: jnp.concatenate([per[k][n] for k in range(4)], axis=sharded_axis[n]) for n in names}


def _perm_cols(a, perm, axis):
    blocks = [lax.slice_in_dim(a, p * D, (p + 1) * D, axis=axis) for p in perm]
    return jnp.concatenate(blocks, axis=axis)


def _shards_apart(a):
    w = a.shape[1] // 4
    return jnp.stack([a[:, k * w:(k + 1) * w] for k in range(4)])


def _shards_joined(a4):
    return jnp.concatenate([a4[k] for k in range(4)], axis=1)


Z_INV = tuple(Z_PERM.index(j) for j in range(8))
SMALL_SH_NAMES = list(SMALL_SHARDED)
SMALL_ROWS = 32


def kernel(x, mem, w_in, b_in, pool_w, pool_scale, lru_conv_w, lru_conv_b, lru_w_r, lru_b_r, lru_w_i, lru_b_i, lru_lambda, lru_w_out, sconv_w, sconv_w_out, w_mix_out, xa_w_q, xa_w_k, xa_w_v, xa_w_o, ffn_w_gate, ffn_w_up, ffn_w_down, ln_g, ln_b, loss_target, m_w_in, m_b_in, m_pool_w, m_pool_scale, m_lru_conv_w, m_lru_conv_b, m_lru_w_r, m_lru_b_r, m_lru_w_i, m_lru_b_i, m_lru_lambda, m_lru_w_out, m_sconv_w, m_sconv_w_out, m_w_mix_out, m_xa_w_q, m_xa_w_k, m_xa_w_v, m_xa_w_o, m_ffn_w_gate, m_ffn_w_up, m_ffn_w_down, m_ln_g, m_ln_b, v_w_in, v_b_in, v_pool_w, v_pool_scale, v_lru_conv_w, v_lru_conv_b, v_lru_w_r, v_lru_b_r, v_lru_w_i, v_lru_b_i, v_lru_lambda, v_lru_w_out, v_sconv_w, v_sconv_w_out, v_w_mix_out, v_xa_w_q, v_xa_w_k, v_xa_w_v, v_xa_w_o, v_ffn_w_gate, v_ffn_w_up, v_ffn_w_down, v_ln_g, v_ln_b):
    loc = dict(w_in=w_in, b_in=b_in, pool_w=pool_w, pool_scale=pool_scale, lru_conv_w=lru_conv_w,
               lru_conv_b=lru_conv_b, lru_w_r=lru_w_r, lru_b_r=lru_b_r, lru_w_i=lru_w_i, lru_b_i=lru_b_i,
               lru_lambda=lru_lambda, lru_w_out=lru_w_out, sconv_w=sconv_w, sconv_w_out=sconv_w_out,
               w_mix_out=w_mix_out, xa_w_q=xa_w_q, xa_w_k=xa_w_k, xa_w_v=xa_w_v, xa_w_o=xa_w_o,
               ffn_w_gate=ffn_w_gate, ffn_w_up=ffn_w_up, ffn_w_down=ffn_w_down, ln_g=ln_g, ln_b=ln_b)
    mom = dict(w_in=m_w_in, b_in=m_b_in, pool_w=m_pool_w, pool_scale=m_pool_scale, lru_conv_w=m_lru_conv_w,
               lru_conv_b=m_lru_conv_b, lru_w_r=m_lru_w_r, lru_b_r=m_lru_b_r, lru_w_i=m_lru_w_i, lru_b_i=m_lru_b_i,
               lru_lambda=m_lru_lambda, lru_w_out=m_lru_w_out, sconv_w=m_sconv_w, sconv_w_out=m_sconv_w_out,
               w_mix_out=m_w_mix_out, xa_w_q=m_xa_w_q, xa_w_k=m_xa_w_k, xa_w_v=m_xa_w_v, xa_w_o=m_xa_w_o,
               ffn_w_gate=m_ffn_w_gate, ffn_w_up=m_ffn_w_up, ffn_w_down=m_ffn_w_down, ln_g=m_ln_g, ln_b=m_ln_b)
    var = dict(w_in=v_w_in, b_in=v_b_in, pool_w=v_pool_w, pool_scale=v_pool_scale, lru_conv_w=v_lru_conv_w,
               lru_conv_b=v_lru_conv_b, lru_w_r=v_lru_w_r, lru_b_r=v_lru_b_r, lru_w_i=v_lru_w_i, lru_b_i=v_lru_b_i,
               lru_lambda=v_lru_lambda, lru_w_out=v_lru_w_out, sconv_w=v_sconv_w, sconv_w_out=v_sconv_w_out,
               w_mix_out=v_w_mix_out, xa_w_q=v_xa_w_q, xa_w_k=v_xa_w_k, xa_w_v=v_xa_w_v, xa_w_o=v_xa_w_o,
               ffn_w_gate=v_ffn_w_gate, ffn_w_up=v_ffn_w_up, ffn_w_down=v_ffn_w_down, ln_g=v_ln_g, ln_b=v_ln_b)

    small = _pack(loc, SMALL_SH_NAMES, 256, F32, 8)
    full, gsmall = gather_weights({n: loc[n].astype(BF) for n in PIECES}, small)
    full.update(_gathered_full(gsmall, SMALL_SH_NAMES, SMALL_SHARDED))
    W = {}
    W['b_in'] = [jnp.roll(b_in[l:l + 1], -2 * D, axis=1) for l in range(DEPTH)]
    W['ffn_w_gu'] = [jnp.concatenate([_shards_joined(full['ffn_w_gate'][l]), _shards_joined(full['ffn_w_up'][l])],
                                     axis=1) for l in range(DEPTH)]
    for n in ('w_in', 'pool_w') + SQUARES + ('ffn_w_down', 'lru_conv_w', 'sconv_w', 'ln_g', 'ln_b'):
        W[n] = [full[n][l] for l in range(DEPTH)]
    for n in ('lru_w_r', 'lru_w_i'):
        W[n] = [loc[n][l].astype(BF) for l in range(DEPTH)]
    for n in ('pool_scale', 'lru_conv_b', 'lru_b_r', 'lru_b_i', 'lru_lambda'):
        W[n] = [loc[n][l:l + 1] for l in range(DEPTH)]

    loss_blk, grad_x, G = local_step(x[0], mem[0], loss_target[0], W)
    loss = lax.psum(loss_blk[0, 0], ("x", "y", "c"))

    small_shapes = {n: FULL_SHAPES[n][1:] for n in SMALL_ALL}
    parts = []
    for l in range(DEPTH):
        g = dict(G[l])
        g['ffn_w_gate'] = _shards_apart(g['ffn_w_gu'][:, :D_FF])
        g['ffn_w_up'] = _shards_apart(g['ffn_w_gu'][:, D_FF:])
        g['pool_w'] = g['pool_w'].astype(BF)
        g['b_in'] = jnp.roll(g['b_in'], 2 * D, axis=1)
        parts.append([g[n] for n in PIECES] + [_pack(g, SMALL_ALL, D, F32, SMALL_ROWS)])

    theirs = split_layers(parts[0], parts[1])
    names = PIECES + ('small',)
    q = {n: add_cores(parts[0][i], parts[1][i], theirs[i], out_dtype=F32 if n == 'small' else BF,
                      name="add_cores_" + n) for i, n in enumerate(names)}
    recv, recv_small = exchange_chips(q, q['small'])
    chip = 2 * lax.axis_index("x") + lax.axis_index("y")
    chip_arr = jnp.reshape(chip, (1,)).astype(jnp.int32)
    sums = [sum_chips(n, q[n], recv[n], chip_arr) for n in PIECES] + [sum_chips_small(q['small'], recv_small, chip_arr)]
    other = swap_cores(sums)
    core = lax.axis_index("c")
    small2 = jnp.where(core == 0, jnp.stack([sums[-1], other[-1]]), jnp.stack([other[-1], sums[-1]]))
    per_layer = [_unpack(small2[l], SMALL_ALL, small_shapes) for l in range(DEPTH)]
    grads = {}
    for n in SMALL_ALL:
        gn = jnp.stack([per_layer[l][n] for l in range(DEPTH)])
        if n in SMALL_SHARDED:
            size = loc[n].shape[SMALL_SHARDED[n]]
            gn = lax.dynamic_slice_in_dim(gn, chip * size, size, axis=SMALL_SHARDED[n])
        grads[n] = gn

    out_d, out_m, out_v = {}, {}, {}
    for i, n in enumerate(PIECES):
        grads[n], out_d[n], out_m[n], out_v[n] = adamw_layers(loc[n], sums[i], other[i], mom[n], var[n],
                                                              name="adamw_" + n)
    for n in SMALL_ALL:
        out_d[n], out_m[n], out_v[n] = adamw(loc[n], grads[n], mom[n], var[n], name="adamw_" + n)

    return (loss, grad_x[None], *[grads[n] for n in WEIGHTS], *[out_d[n] for n in WEIGHTS],
            *[out_m[n] for n in WEIGHTS], *[out_v[n] for n in WEIGHTS])
```

```python
import functools

import jax
import jax.numpy as jnp
from jax import lax
from jax.experimental import pallas as pl
from jax.experimental.pallas import tpu as pltpu

F32 = jnp.float32
BF = jnp.bfloat16
MESH = pl.DeviceIdType.MESH

D = 1024
DEPTH = 2
N_MEM = 256
POOL_WINDOWS = (2, 4, 8, 16)
POOL_GD = 256
LRU_HEADS = 8
LRU_HD = 128
LRU_C = 8.0
X_HEADS = 4
X_HD = 256
D_FF = 2816
IN_COLS = 8 * D
ALPHA = (2 * DEPTH) ** 0.25
LN_EPS = 1e-5
ADAM_LR = 0.001
ADAM_B1 = 0.9
ADAM_B2 = 0.999
ADAM_EPS = 1e-08
ADAM_WD = 0.01
ADAM_STEP = 10

Z_PERM = (2, 3, 4, 5, 6, 7, 0, 1)
ZB_SCONV, ZB_GATE, ZB_POOL, ZB_LRU = 0, 1, 6, 7
HALO = 16
VMEM_LIMIT = 56 * 1024 * 1024

WEIGHTS = ['w_in', 'b_in', 'pool_w', 'pool_scale', 'lru_conv_w', 'lru_conv_b', 'lru_w_r', 'lru_b_r', 'lru_w_i',
           'lru_b_i', 'lru_lambda', 'lru_w_out', 'sconv_w', 'sconv_w_out', 'w_mix_out', 'xa_w_q', 'xa_w_k', 'xa_w_v',
           'xa_w_o', 'ffn_w_gate', 'ffn_w_up', 'ffn_w_down', 'ln_g', 'ln_b']
BIG_SHARDED = {'w_in': 2, 'pool_w': 2, 'lru_w_out': 1, 'sconv_w_out': 1, 'w_mix_out': 1, 'xa_w_q': 1, 'xa_w_k': 1,
               'xa_w_v': 1, 'xa_w_o': 1, 'ffn_w_gate': 2, 'ffn_w_up': 2, 'ffn_w_down': 1}
SMALL_SHARDED = {'lru_conv_w': 2, 'sconv_w': 2, 'ln_g': 2, 'ln_b': 2}
REPLICATED = ['b_in', 'pool_scale', 'lru_conv_b', 'lru_w_r', 'lru_b_r', 'lru_w_i', 'lru_b_i', 'lru_lambda']
SMALL_ALL = ['b_in', 'pool_scale', 'lru_conv_w', 'lru_conv_b', 'lru_w_r', 'lru_b_r', 'lru_w_i', 'lru_b_i',
             'lru_lambda', 'sconv_w', 'ln_g', 'ln_b']


def _cp(*sem):
    return pltpu.CompilerParams(dimension_semantics=sem, vmem_limit_bytes=VMEM_LIMIT)


def _sigmoid(x):
    return 0.5 * jnp.tanh(0.5 * x) + 0.5


def mm_nn(a, w, bias, *, out_dtype, tm, tn, name):
    T, K = a.shape
    N = w.shape[1]
    tm, tn = min(tm, T), min(tn, N)

    def body(*refs):
        if bias is None:
            a_ref, w_ref, o_ref = refs
        else:
            a_ref, w_ref, b_ref, o_ref = refs
        acc = jnp.dot(a_ref[...].astype(BF), w_ref[...], preferred_element_type=F32)
        if bias is not None:
            acc = acc + b_ref[...]
        o_ref[...] = acc.astype(o_ref.dtype)

    in_specs = [pl.BlockSpec((tm, K), lambda j, i: (i, 0)), pl.BlockSpec((K, tn), lambda j, i: (0, j))]
    args = [a, w]
    if bias is not None:
        in_specs.append(pl.BlockSpec((1, tn), lambda j, i: (0, j)))
        args.append(bias)
    return pl.pallas_call(
        body, name=name, grid=(N // tn, T // tm), in_specs=in_specs,
        out_specs=pl.BlockSpec((tm, tn), lambda j, i: (i, j)),
        out_shape=jax.ShapeDtypeStruct((T, N), out_dtype), compiler_params=_cp("arbitrary", "arbitrary"))(*args)


def mm_nt(a, w, res, *, out_dtype, tm, tc, name):
    T, C = a.shape
    K = w.shape[0]
    tm, tc = min(tm, T), min(tc, C)
    nc = C // tc

    def body(*refs):
        if res is None:
            a_ref, w_ref, o_ref, acc_ref = refs
        else:
            a_ref, w_ref, r_ref, o_ref, acc_ref = refs
        c = pl.program_id(1)
        part = lax.dot_general(a_ref[...].astype(BF), w_ref[...], (((1,), (1,)), ((), ())),
                               preferred_element_type=F32)

        @pl.when(c == 0)
        def _():
            acc_ref[...] = part

        @pl.when(c > 0)
        def _():
            acc_ref[...] += part

        @pl.when(c == nc - 1)
        def _():
            out = acc_ref[...]
            if res is not None:
                out = out + ALPHA * r_ref[...]
            o_ref[...] = out.astype(o_ref.dtype)

    in_specs = [pl.BlockSpec((tm, tc), lambda i, c: (i, c)), pl.BlockSpec((K, tc), lambda i, c: (0, c))]
    args = [a, w]
    if res is not None:
        in_specs.append(pl.BlockSpec((tm, K), lambda i, c: (i, 0)))
        args.append(res)
    return pl.pallas_call(
        body, name=name, grid=(T // tm, nc), in_specs=in_specs,
        out_specs=pl.BlockSpec((tm, K), lambda i, c: (i, 0)),
        out_shape=jax.ShapeDtypeStruct((T, K), out_dtype),
        scratch_shapes=[pltpu.VMEM((tm, K), F32)], compiler_params=_cp("arbitrary", "arbitrary"))(*args)


def mm_tn(a, b, *, out_dtype, tk, tn, tt, name, colsum=False):
    T, K = a.shape
    N = b.shape[1]
    tk, tn, tt = min(tk, K), min(tn, N), min(tt, T)
    nt = T // tt

    def body(*refs):
        if colsum:
            a_ref, b_ref, o_ref, cs_ref, acc_ref = refs
        else:
            a_ref, b_ref, o_ref, acc_ref = refs
        i, t = pl.program_id(1), pl.program_id(2)
        bb = b_ref[...]
        part = lax.dot_general(a_ref[...].astype(BF), bb.astype(BF), (((0,), (0,)), ((), ())),
                               preferred_element_type=F32)

        @pl.when(t == 0)
        def _():
            acc_ref[...] = part

        @pl.when(t > 0)
        def _():
            acc_ref[...] += part

        @pl.when(t == nt - 1)
        def _():
            o_ref[...] = acc_ref[...].astype(o_ref.dtype)

        if colsum:
            s = jnp.sum(bb.astype(F32), axis=0, keepdims=True)

            @pl.when((i == 0) & (t == 0))
            def _():
                cs_ref[...] = s

            @pl.when((i == 0) & (t > 0))
            def _():
                cs_ref[...] += s

    out_specs = [pl.BlockSpec((tk, tn), lambda j, i, t: (i, j))]
    out_shape = [jax.ShapeDtypeStruct((K, N), out_dtype)]
    if colsum:
        out_specs.append(pl.BlockSpec((1, tn), lambda j, i, t: (0, j)))
        out_shape.append(jax.ShapeDtypeStruct((1, N), F32))
    res = pl.pallas_call(
        body, name=name, grid=(N // tn, K // tk, nt),
        in_specs=[pl.BlockSpec((tt, tk), lambda j, i, t: (t, i)), pl.BlockSpec((tt, tn), lambda j, i, t: (t, j))],
        out_specs=out_specs, out_shape=out_shape, scratch_shapes=[pltpu.VMEM((tk, tn), F32)],
        compiler_params=_cp("arbitrary", "arbitrary", "arbitrary"))(a, b)
    return res if colsum else res[0]


def mm_res_ln(a, w, res, g, b, *, tm, name):
    T, K = a.shape
    tm = min(tm, T)

    def body(a_ref, w_ref, r_ref, g_ref, b_ref, y_ref, yb_ref, xh_ref, rs_ref):
        pre = ALPHA * r_ref[...] + jnp.dot(a_ref[...].astype(BF), w_ref[...], preferred_element_type=F32)
        mu = jnp.mean(pre, axis=-1, keepdims=True)
        cen = pre - mu
        var = jnp.mean(cen * cen, axis=-1, keepdims=True)
        rstd = lax.rsqrt(var + LN_EPS)
        xhat = cen * rstd
        y = xhat * g_ref[...] + b_ref[...]
        y_ref[...] = y
        yb_ref[...] = y.astype(BF)
        xh_ref[...] = xhat
        rs_ref[...] = rstd

    row = lambda i: (i, 0)
    fix = lambda i: (0, 0)
    return pl.pallas_call(
        body, name=name, grid=(T // tm,),
        in_specs=[pl.BlockSpec((tm, K), row), pl.BlockSpec((K, D), fix), pl.BlockSpec((tm, D), row),
                  pl.BlockSpec((1, D), fix), pl.BlockSpec((1, D), fix)],
        out_specs=[pl.BlockSpec((tm, D), row), pl.BlockSpec((tm, D), row), pl.BlockSpec((tm, D), row),
                   pl.BlockSpec((tm, 1), row)],
        out_shape=[jax.ShapeDtypeStruct((T, D), F32), jax.ShapeDtypeStruct((T, D), BF),
                   jax.ShapeDtypeStruct((T, D), F32), jax.ShapeDtypeStruct((T, 1), F32)],
        compiler_params=_cp("arbitrary"))(a, w, res, g, b)


def ln_bwd(dy, xhat, rstd, g, *, tm, name, loss_from=None):
    T = xhat.shape[0]
    tm = min(tm, T)
    with_loss = loss_from is not None

    def body(*refs):
        if with_loss:
            xh_ref, rs_ref, g_ref, b_ref, t_ref, dp_ref, dpb_ref, dg_ref, db_ref, ls_ref = refs
        else:
            dy_ref, xh_ref, rs_ref, g_ref, dp_ref, dpb_ref, dg_ref, db_ref = refs
        i = pl.program_id(0)
        xhat_ = xh_ref[...]
        gg = g_ref[...]
        if with_loss:
            err = xhat_ * gg + b_ref[...] - t_ref[...]
            dyv = err * (1.0 / D)
            lpart = 0.5 * jnp.sum(jnp.sum(err * err, axis=-1, keepdims=True) * (1.0 / D))
        else:
            dyv = dy_ref[...]
        dxh = dyv * gg
        m1 = jnp.mean(dxh, axis=-1, keepdims=True)
        m2 = jnp.mean(dxh * xhat_, axis=-1, keepdims=True)
        dpre = rs_ref[...] * (dxh - m1 - xhat_ * m2)
        dp_ref[...] = dpre
        dpb_ref[...] = dpre.astype(BF)
        dgp = jnp.sum(dyv * xhat_, axis=0, keepdims=True)
        dbp = jnp.sum(dyv, axis=0, keepdims=True)

        @pl.when(i == 0)
        def _():
            dg_ref[...] = dgp
            db_ref[...] = dbp
            if with_loss:
                ls_ref[...] = jnp.full((8, 128), lpart, F32)

        @pl.when(i > 0)
        def _():
            dg_ref[...] += dgp
            db_ref[...] += dbp
            if with_loss:
                ls_ref[...] += jnp.full((8, 128), lpart, F32)

    row = lambda i: (i, 0)
    fix = lambda i: (0, 0)
    if with_loss:
        in_specs = [pl.BlockSpec((tm, D), row), pl.BlockSpec((tm, 1), row), pl.BlockSpec((1, D), fix),
                    pl.BlockSpec((1, D), fix), pl.BlockSpec((tm, D), row)]
        args = [xhat, rstd, g, loss_from[0], loss_from[1]]
    else:
        in_specs = [pl.BlockSpec((tm, D), row), pl.BlockSpec((tm, D), row), pl.BlockSpec((tm, 1), row),
                    pl.BlockSpec((1, D), fix)]
        args = [dy, xhat, rstd, g]
    out_specs = [pl.BlockSpec((tm, D), row), pl.BlockSpec((tm, D), row), pl.BlockSpec((1, D), fix),
                 pl.BlockSpec((1, D), fix)]
    out_shape = [jax.ShapeDtypeStruct((T, D), F32), jax.ShapeDtypeStruct((T, D), BF),
                 jax.ShapeDtypeStruct((1, D), F32), jax.ShapeDtypeStruct((1, D), F32)]
    if with_loss:
        out_specs.append(pl.BlockSpec((8, 128), fix))
        out_shape.append(jax.ShapeDtypeStruct((8, 128), F32))
    return pl.pallas_call(body, name=name, grid=(T // tm,), in_specs=in_specs, out_specs=out_specs,
                          out_shape=out_shape, compiler_params=_cp("arbitrary"))(*args)


def _prev_halo(tm, blk):
    return lambda i: (jnp.maximum(i * (tm // HALO) - 1, 0), blk)


def _next_halo(tm, T, blk):
    return lambda i: (jnp.minimum((i + 1) * (tm // HALO), T // HALO - 1), blk)


def _pool_p(ext, t, g):
    e = ext[:, g * POOL_GD:(g + 1) * POOL_GD]
    s = e
    for sh in (1, 2, 4, 8)[:g + 1]:
        s = s + pltpu.roll(s, sh, axis=0)
    inv = 1.0 / jnp.minimum(t + 1, POOL_WINDOWS[g]).astype(F32)
    return s[HALO:] * inv - e[HALO:]


def pool_fwd(z, pw, *, tm, name):
    T = z.shape[0]
    tm = min(tm, T)

    def body(zm_ref, zh_ref, pw_ref, o_ref):
        i = pl.program_id(0)
        keep = jnp.where(i == 0, 0.0, 1.0).astype(F32)
        ext = jnp.concatenate([zh_ref[...].astype(F32) * keep, zm_ref[...].astype(F32)], axis=0)
        t = i * tm + lax.broadcasted_iota(jnp.int32, (tm, 1), 0)
        outs = [jnp.dot(_pool_p(ext, t, g).astype(BF), pw_ref[g], preferred_element_type=F32) for g in range(4)]
        o_ref[...] = jnp.concatenate(outs, axis=1).astype(o_ref.dtype)

    return pl.pallas_call(
        body, name=name, grid=(T // tm,),
        in_specs=[pl.BlockSpec((tm, D), lambda i: (i, ZB_POOL)), pl.BlockSpec((HALO, D), _prev_halo(tm, ZB_POOL)),
                  pl.BlockSpec((4, POOL_GD, POOL_GD), lambda i: (0, 0, 0))],
        out_specs=pl.BlockSpec((tm, D), lambda i: (i, 0)),
        out_shape=jax.ShapeDtypeStruct((T, D), BF), compiler_params=_cp("arbitrary"))(z, z, pw)


def pool_bwd(dz, dyp, yp_pre, z, pw, ps, *, tm, name):
    T = z.shape[0]
    tm = min(tm, T)
    nt = T // tm

    def body(dz_in, dy_ref, dyn_ref, yp_ref, zm_ref, zh_ref, pw_ref, ps_ref, dz_ref, dpw_ref, dps_ref):
        del dz_in
        i = pl.program_id(0)
        keep_p = jnp.where(i == 0, 0.0, 1.0).astype(F32)
        keep_n = jnp.where(i == nt - 1, 0.0, 1.0).astype(F32)
        ext = jnp.concatenate([zh_ref[...].astype(F32) * keep_p, zm_ref[...].astype(F32)], axis=0)
        t = i * tm + lax.broadcasted_iota(jnp.int32, (tm, 1), 0)
        psv = ps_ref[...]
        dy = dy_ref[...].astype(F32)
        dyp_ext = jnp.concatenate([dy, dyn_ref[...].astype(F32) * keep_n], axis=0) * psv
        t_ext = i * tm + lax.broadcasted_iota(jnp.int32, (tm + HALO, 1), 0)
        dps = jnp.sum(dy * yp_ref[...].astype(F32), axis=0, keepdims=True)
        dzs, dpws = [], []
        for g in range(4):
            sl = slice(g * POOL_GD, (g + 1) * POOL_GD)
            dyg = dyp_ext[:, sl].astype(BF)
            dp = lax.dot_general(dyg, pw_ref[g], (((1,), (1,)), ((), ())), preferred_element_type=F32)
            q = dp * (1.0 / jnp.minimum(t_ext + 1, POOL_WINDOWS[g]).astype(F32))
            s = q
            for sh in (1, 2, 4, 8)[:g + 1]:
                s = s + pltpu.roll(s, tm + HALO - sh, axis=0)
            dzs.append(s[:tm] - dp[:tm])
            p = _pool_p(ext, t, g).astype(BF)
            dpws.append(lax.dot_general(p, dyg[:tm], (((0,), (0,)), ((), ())), preferred_element_type=F32))
        dz_ref[...] = jnp.concatenate(dzs, axis=1).astype(dz_ref.dtype)

        @pl.when(i == 0)
        def _():
            for g in range(4):
                dpw_ref[g] = dpws[g]
            dps_ref[...] = dps

        @pl.when(i > 0)
        def _():
            for g in range(4):
                dpw_ref[g] += dpws[g]
            dps_ref[...] += dps

    row = lambda i: (i, 0)
    return pl.pallas_call(
        body, name=name, grid=(nt,),
        in_specs=[pl.BlockSpec(memory_space=pl.ANY),
                  pl.BlockSpec((tm, D), row), pl.BlockSpec((HALO, D), _next_halo(tm, T, 0)),
                  pl.BlockSpec((tm, D), row),
                  pl.BlockSpec((tm, D), lambda i: (i, ZB_POOL)), pl.BlockSpec((HALO, D), _prev_halo(tm, ZB_POOL)),
                  pl.BlockSpec((4, POOL_GD, POOL_GD), lambda i: (0, 0, 0)), pl.BlockSpec((1, D), lambda i: (0, 0))],
        out_specs=[pl.BlockSpec((tm, D), lambda i: (i, ZB_POOL)),
                   pl.BlockSpec((4, POOL_GD, POOL_GD), lambda i: (0, 0, 0)), pl.BlockSpec((1, D), lambda i: (0, 0))],
        out_shape=[jax.ShapeDtypeStruct(dz.shape, dz.dtype), jax.ShapeDtypeStruct((4, POOL_GD, POOL_GD), F32),
                   jax.ShapeDtypeStruct((1, D), F32)],
        input_output_aliases={0: 0}, compiler_params=_cp("arbitrary"))(dz, dyp, dyp, yp_pre, z, z, pw, ps)


def _fill_ext(ext_s, halo, main, keep):
    ext_s[0:HALO, :] = halo * keep
    ext_s[HALO:, :] = main


def _lru_gates(ext_s, tm, cw, cb, wr_ref, br, wi_ref, bi, lam):
    shifted = []
    v = cb
    for k in range(4):
        zs = ext_s[pl.ds(HALO - 3 + k, tm), :]
        shifted.append(zs)
        v = v + cw[k:k + 1, :] * zs
    vb = v.astype(BF)
    rp, ip = [], []
    for h in range(LRU_HEADS):
        sl = slice(h * LRU_HD, (h + 1) * LRU_HD)
        rp.append(jnp.dot(vb[:, sl], wr_ref[h], preferred_element_type=F32))
        ip.append(jnp.dot(vb[:, sl], wi_ref[h], preferred_element_type=F32))
    r = _sigmoid(jnp.concatenate(rp, axis=1) + br)
    ig = _sigmoid(jnp.concatenate(ip, axis=1) + bi)
    sp = jnp.maximum(-lam, 0.0) + jnp.log(1.0 + jnp.exp(-jnp.abs(lam)))
    a = jnp.exp(-LRU_C * r * sp)
    om = 1.0 - a * a
    rs = lax.rsqrt(om)
    return v, vb, r, ig, a, om, rs, sp, shifted


def lru_fwd(z, cw, cb, wr, br, wi, bi, lam, wlo, *, tm, name):
    T = z.shape[0]
    tm = min(tm, T)
    nch = tm // 8

    def body(zm_ref, zh_ref, cw_ref, cb_ref, wr_ref, br_ref, wi_ref, bi_ref, lam_ref, wlo_ref, h_ref, y_ref,
             a_s, b_s, carry, ext_s):
        i = pl.program_id(0)

        @pl.when(i == 0)
        def _():
            carry[...] = jnp.zeros_like(carry)

        keep = jnp.where(i == 0, 0.0, 1.0).astype(F32)
        _fill_ext(ext_s, zh_ref[...].astype(F32), zm_ref[...].astype(F32), keep)
        v, _, _, ig, a, om, rs, _, _ = _lru_gates(ext_s, tm, cw_ref[...], cb_ref[...], wr_ref, br_ref[...], wi_ref,
                                                  bi_ref[...], lam_ref[...])
        a_s[...] = a
        b_s[...] = jnp.where(om > 0.0, om * rs, 0.0) * (ig * v)
        row = lax.broadcasted_iota(jnp.int32, (8, D), 0)

        def step(ci, hprev):
            sl = pl.ds(pl.multiple_of(ci * 8, 8), 8)
            aa, bb = a_s[sl, :], b_s[sl, :]
            for s in (1, 2, 4):
                m = row >= s
                bb = bb + aa * jnp.where(m, pltpu.roll(bb, s, axis=0), 0.0)
                aa = aa * jnp.where(m, pltpu.roll(aa, s, axis=0), 1.0)
            h = bb + aa * hprev
            h_ref[sl, :] = h
            return jnp.broadcast_to(h[7:8, :], (8, D))

        carry[...] = lax.fori_loop(0, nch, step, carry[...])
        y_ref[...] = jnp.dot(h_ref[...].astype(BF), wlo_ref[...], preferred_element_type=F32).astype(BF)

    fix2 = lambda i: (0, 0)
    fix3 = lambda i: (0, 0, 0)
    return pl.pallas_call(
        body, name=name, grid=(T // tm,),
        in_specs=[pl.BlockSpec((tm, D), lambda i: (i, ZB_LRU)), pl.BlockSpec((HALO, D), _prev_halo(tm, ZB_LRU)),
                  pl.BlockSpec((4, D), fix2), pl.BlockSpec((1, D), fix2),
                  pl.BlockSpec((LRU_HEADS, LRU_HD, LRU_HD), fix3), pl.BlockSpec((1, D), fix2),
                  pl.BlockSpec((LRU_HEADS, LRU_HD, LRU_HD), fix3), pl.BlockSpec((1, D), fix2),
                  pl.BlockSpec((1, D), fix2), pl.BlockSpec((D, D), fix2)],
        out_specs=[pl.BlockSpec((tm, D), lambda i: (i, 0)), pl.BlockSpec((tm, D), lambda i: (i, 0))],
        out_shape=[jax.ShapeDtypeStruct((T, D), F32), jax.ShapeDtypeStruct((T, D), BF)],
        scratch_shapes=[pltpu.VMEM((tm, D), F32), pltpu.VMEM((tm, D), F32), pltpu.VMEM((8, D), F32),
                        pltpu.VMEM((tm + HALO, D), F32)],
        compiler_params=_cp("arbitrary"))(z, z, cw, cb, wr, br, wi, bi, lam, wlo)


def lru_bwd(dz, dyl, z, h, cw, cb, wr, br, wi, bi, lam, wlo, *, tm, name):
    T = z.shape[0]
    tm = min(tm, T)
    nt = T // tm
    nch = tm // 8

    def body(dz_in, dy_ref, zm_ref, zh_ref, h_ref, hh_ref, cw_ref, cb_ref, wr_ref, br_ref, wi_ref, bi_ref, lam_ref,
             wlo_ref, dz_ref, dcw_ref, dcb_ref, dwr_ref, dbr_ref, dwi_ref, dbi_ref, dlam_ref,
             c_s, g_s, dh_s, dh_carry, a_ext, dv_ext, ext_s, h_ext):
        del dz_in
        i = pl.program_id(0)
        ti = nt - 1 - i

        @pl.when(i == 0)
        def _():
            dh_carry[...] = jnp.zeros_like(dh_carry)
            a_ext[tm:, :] = jnp.zeros((8, D), F32)
            dv_ext[tm:, :] = jnp.zeros((HALO, D), F32)

        keep = jnp.where(ti == 0, 0.0, 1.0).astype(F32)
        _fill_ext(ext_s, zh_ref[...].astype(F32), zm_ref[...].astype(F32), keep)
        cw_ = cw_ref[...]
        lam_ = lam_ref[...]
        v, vb, r, ig, a, om, rs, sp, shifted = _lru_gates(ext_s, tm, cw_, cb_ref[...], wr_ref, br_ref[...], wi_ref,
                                                          bi_ref[...], lam_)
        mult = jnp.where(om > 0.0, om * rs, 0.0)
        a_ext[0:tm, :] = a
        c_s[...] = a_ext[pl.ds(1, tm), :]
        g_s[...] = lax.dot_general(dy_ref[...], wlo_ref[...], (((1,), (1,)), ((), ())), preferred_element_type=F32)
        row = lax.broadcasted_iota(jnp.int32, (8, D), 0)

        def step(k, nxt):
            ci = nch - 1 - k
            sl = pl.ds(pl.multiple_of(ci * 8, 8), 8)
            cc, gg = c_s[sl, :], g_s[sl, :]
            for s in (1, 2, 4):
                m = row < 8 - s
                gg = gg + cc * jnp.where(m, pltpu.roll(gg, 8 - s, axis=0), 0.0)
                cc = cc * jnp.where(m, pltpu.roll(cc, 8 - s, axis=0), 1.0)
            dh = gg + cc * nxt
            dh_s[sl, :] = dh
            return jnp.broadcast_to(dh[0:1, :], (8, D))

        dh_carry[...] = lax.fori_loop(0, nch, step, dh_carry[...])
        a_ext[tm:, :] = a[0:8, :]
        dh = dh_s[...]
        h_ext[0:8, :] = hh_ref[...] * keep
        h_ext[8:, :] = h_ref[...]
        hprev = h_ext[pl.ds(7, tm), :]
        iv = ig * v
        da = dh * hprev
        dmult = dh * iv
        div = dh * mult
        dlog = da * a - dmult * (a * a) * rs
        dr = dlog * (-LRU_C * sp)
        dlam = jnp.sum(dlog * r, axis=0, keepdims=True) * (LRU_C * _sigmoid(-lam_))
        di = div * v
        dv = div * ig
        drp = dr * r * (1.0 - r)
        dip = di * ig * (1.0 - ig)
        drb, dib = drp.astype(BF), dip.astype(BF)
        dvh, dwr, dwi = [], [], []
        nt_dims = (((1,), (1,)), ((), ()))
        tn_dims = (((0,), (0,)), ((), ()))
        for hd in range(LRU_HEADS):
            sl = slice(hd * LRU_HD, (hd + 1) * LRU_HD)
            dvh.append(lax.dot_general(drb[:, sl], wr_ref[hd], nt_dims, preferred_element_type=F32)
                       + lax.dot_general(dib[:, sl], wi_ref[hd], nt_dims, preferred_element_type=F32))
            dwr.append(lax.dot_general(vb[:, sl], drb[:, sl], tn_dims, preferred_element_type=F32))
            dwi.append(lax.dot_general(vb[:, sl], dib[:, sl], tn_dims, preferred_element_type=F32))
        dv = dv + jnp.concatenate(dvh, axis=1)
        dv_ext[0:tm, :] = dv
        dzl = cw_[3:4, :] * dv
        for k in range(3):
            dzl = dzl + cw_[k:k + 1, :] * dv_ext[pl.ds(3 - k, tm), :]
        dz_ref[...] = dzl.astype(dz_ref.dtype)
        dv_ext[tm:, :] = dv[:HALO]
        dcw = jnp.concatenate([jnp.sum(dv * shifted[k], axis=0, keepdims=True) for k in range(4)], axis=0)
        dcb = jnp.sum(dv, axis=0, keepdims=True)
        dbr = jnp.sum(drp, axis=0, keepdims=True)
        dbi = jnp.sum(dip, axis=0, keepdims=True)

        @pl.when(i == 0)
        def _():
            dcw_ref[...] = dcw
            dcb_ref[...] = dcb
            dbr_ref[...] = dbr
            dbi_ref[...] = dbi
            dlam_ref[...] = dlam
            for hd in range(LRU_HEADS):
                dwr_ref[hd] = dwr[hd]
                dwi_ref[hd] = dwi[hd]

        @pl.when(i > 0)
        def _():
            dcw_ref[...] += dcw
            dcb_ref[...] += dcb
            dbr_ref[...] += dbr
            dbi_ref[...] += dbi
            dlam_ref[...] += dlam
            for hd in range(LRU_HEADS):
                dwr_ref[hd] += dwr[hd]
                dwi_ref[hd] += dwi[hd]

    fix2 = lambda i: (0, 0)
    fix3 = lambda i: (0, 0, 0)
    rev = lambda i: (nt - 1 - i, 0)
    vec = pl.BlockSpec((1, D), fix2)
    hw = pl.BlockSpec((LRU_HEADS, LRU_HD, LRU_HD), fix3)
    return pl.pallas_call(
        body, name=name, grid=(nt,),
        in_specs=[pl.BlockSpec(memory_space=pl.ANY),
                  pl.BlockSpec((tm, D), rev),
                  pl.BlockSpec((tm, D), lambda i: (nt - 1 - i, ZB_LRU)),
                  pl.BlockSpec((HALO, D), lambda i: (jnp.maximum((nt - 1 - i) * (tm // HALO) - 1, 0), ZB_LRU)),
                  pl.BlockSpec((tm, D), rev),
                  pl.BlockSpec((8, D), lambda i: (jnp.maximum((nt - 1 - i) * (tm // 8) - 1, 0), 0)),
                  pl.BlockSpec((4, D), fix2), vec, hw, vec, hw, vec, vec, pl.BlockSpec((D, D), fix2)],
        out_specs=[pl.BlockSpec((tm, D), lambda i: (nt - 1 - i, ZB_LRU)),
                   pl.BlockSpec((4, D), fix2), vec, hw, vec, hw, vec, vec],
        out_shape=[jax.ShapeDtypeStruct(dz.shape, dz.dtype), jax.ShapeDtypeStruct((4, D), F32),
                   jax.ShapeDtypeStruct((1, D), F32), jax.ShapeDtypeStruct((LRU_HEADS, LRU_HD, LRU_HD), F32),
                   jax.ShapeDtypeStruct((1, D), F32), jax.ShapeDtypeStruct((LRU_HEADS, LRU_HD, LRU_HD), F32),
                   jax.ShapeDtypeStruct((1, D), F32), jax.ShapeDtypeStruct((1, D), F32)],
        scratch_shapes=[pltpu.VMEM((tm, D), F32), pltpu.VMEM((tm, D), F32), pltpu.VMEM((tm, D), F32),
                        pltpu.VMEM((8, D), F32), pltpu.VMEM((tm + 8, D), F32), pltpu.VMEM((tm + HALO, D), F32),
                        pltpu.VMEM((tm + HALO, D), F32), pltpu.VMEM((tm + 8, D), F32)],
        input_output_aliases={0: 0},
        compiler_params=_cp("arbitrary"))(dz, dyl, z, z, h, h, cw, cb, wr, br, wi, bi, lam, wlo)


def _sconv_cv(u_ext, sw):
    shifted = []
    cv = None
    for k in range(3):
        us = (u_ext if k == 2 else pltpu.roll(u_ext, 2 - k, axis=0))[HALO:]
        shifted.append(us)
        term = sw[k:k + 1, :] * us
        cv = term if cv is None else cv + term
    return cv, shifted


def sconv_fwd(z, sw, wso, *, tm, name):
    T = z.shape[0]
    tm = min(tm, T)

    def body(zm_ref, zh_ref, sw_ref, wso_ref, s_ref, y_ref):
        i = pl.program_id(0)
        keep = jnp.where(i == 0, 0.0, 1.0).astype(F32)
        zm = zm_ref[...].astype(F32)
        zh = zh_ref[...].astype(F32)
        u_ext = jnp.concatenate([zh[:, D:2 * D] * zh[:, 2 * D:] * keep, zm[:, D:2 * D] * zm[:, 2 * D:]], axis=0)
        cv, _ = _sconv_cv(u_ext, sw_ref[...])
        s = (zm[:, :D] * cv).astype(BF)
        s_ref[...] = s
        y_ref[...] = jnp.dot(s, wso_ref[...], preferred_element_type=F32).astype(BF)

    return pl.pallas_call(
        body, name=name, grid=(T // tm,),
        in_specs=[pl.BlockSpec((tm, 3 * D), lambda i: (i, ZB_SCONV)),
                  pl.BlockSpec((HALO, 3 * D), _prev_halo(tm, ZB_SCONV)),
                  pl.BlockSpec((3, D), lambda i: (0, 0)), pl.BlockSpec((D, D), lambda i: (0, 0))],
        out_specs=[pl.BlockSpec((tm, D), lambda i: (i, 0)), pl.BlockSpec((tm, D), lambda i: (i, 0))],
        out_shape=[jax.ShapeDtypeStruct((T, D), BF), jax.ShapeDtypeStruct((T, D), BF)],
        compiler_params=_cp("arbitrary"))(z, z, sw, wso)


def sconv_bwd(dz, dyc, z, sw, wso, *, tm, name):
    T = z.shape[0]
    tm = min(tm, T)
    nt = T // tm

    def body(dz_in, dy_ref, dyn_ref, zm_ref, zp_ref, zn_ref, sw_ref, wso_ref, dz_ref, dsw_ref):
        del dz_in
        i = pl.program_id(0)
        keep_p = jnp.where(i == 0, 0.0, 1.0).astype(F32)
        keep_n = jnp.where(i == nt - 1, 0.0, 1.0).astype(F32)
        sw_ = sw_ref[...]
        zm = zm_ref[...].astype(F32)
        zp = zp_ref[...].astype(F32)
        zb, zc, zh = zm[:, :D], zm[:, D:2 * D], zm[:, 2 * D:]
        u_ext = jnp.concatenate([zp[:, D:2 * D] * zp[:, 2 * D:] * keep_p, zc * zh], axis=0)
        cv, shifted = _sconv_cv(u_ext, sw_)
        dy_ext = jnp.concatenate([dy_ref[...], dyn_ref[...]], axis=0)
        ds_ext = lax.dot_general(dy_ext, wso_ref[...], (((1,), (1,)), ((), ())), preferred_element_type=F32)
        zb_ext = jnp.concatenate([zb, zn_ref[...][:, :D].astype(F32) * keep_n], axis=0)
        dcv_ext = ds_ext * zb_ext
        du = sw_[2:3, :] * dcv_ext[:tm]
        for k in range(2):
            du = du + sw_[k:k + 1, :] * pltpu.roll(dcv_ext, tm + HALO - (2 - k), axis=0)[:tm]
        dz_ref[...] = jnp.concatenate([ds_ext[:tm] * cv, du * zh, du * zc], axis=1).astype(dz_ref.dtype)
        dcv = dcv_ext[:tm]
        dsw = jnp.concatenate([jnp.sum(dcv * shifted[k], axis=0, keepdims=True) for k in range(3)], axis=0)

        @pl.when(i == 0)
        def _():
            dsw_ref[...] = dsw

        @pl.when(i > 0)
        def _():
            dsw_ref[...] += dsw

    return pl.pallas_call(
        body, name=name, grid=(nt,),
        in_specs=[pl.BlockSpec(memory_space=pl.ANY),
                  pl.BlockSpec((tm, D), lambda i: (i, 0)), pl.BlockSpec((HALO, D), _next_halo(tm, T, 0)),
                  pl.BlockSpec((tm, 3 * D), lambda i: (i, ZB_SCONV)),
                  pl.BlockSpec((HALO, 3 * D), _prev_halo(tm, ZB_SCONV)),
                  pl.BlockSpec((HALO, 3 * D), _next_halo(tm, T, ZB_SCONV)),
                  pl.BlockSpec((3, D), lambda i: (0, 0)), pl.BlockSpec((D, D), lambda i: (0, 0))],
        out_specs=[pl.BlockSpec((tm, 3 * D), lambda i: (i, ZB_SCONV)), pl.BlockSpec((3, D), lambda i: (0, 0))],
        out_shape=[jax.ShapeDtypeStruct(dz.shape, dz.dtype), jax.ShapeDtypeStruct((3, D), F32)],
        input_output_aliases={0: 0}, compiler_params=_cp("arbitrary"))(dz, dyc, dyc, z, z, z, sw, wso)


def merge_fwd(z, yp_pre, yl, yc, ps, *, tm, name):
    T = z.shape[0]
    tm = min(tm, T)

    def body(zg_ref, yp_ref, yl_ref, yc_ref, ps_ref, o_ref):
        gts = _sigmoid(zg_ref[...].astype(F32))
        m = (gts[:, :D] * (yp_ref[...].astype(F32) * ps_ref[...]) + gts[:, D:2 * D] * yl_ref[...].astype(F32)
             + gts[:, 2 * D:] * yc_ref[...].astype(F32))
        o_ref[...] = m.astype(o_ref.dtype)

    row = lambda i: (i, 0)
    return pl.pallas_call(
        body, name=name, grid=(T // tm,),
        in_specs=[pl.BlockSpec((tm, 3 * D), lambda i: (i, ZB_GATE)), pl.BlockSpec((tm, D), row),
                  pl.BlockSpec((tm, D), row), pl.BlockSpec((tm, D), row), pl.BlockSpec((1, D), lambda i: (0, 0))],
        out_specs=pl.BlockSpec((tm, D), row), out_shape=jax.ShapeDtypeStruct((T, D), BF),
        compiler_params=_cp("arbitrary"))(z, yp_pre, yl, yc, ps)


def merge_bwd(dm, z, yp_pre, yl, yc, ps, *, tm, name):
    T = z.shape[0]
    tm = min(tm, T)

    def body(dm_ref, zg_ref, yp_ref, yl_ref, yc_ref, ps_ref, dz_ref, dyp_ref, dyl_ref, dyc_ref):
        gts = _sigmoid(zg_ref[...].astype(F32))
        dmv = dm_ref[...].astype(F32)
        ys = (yp_ref[...].astype(F32) * ps_ref[...], yl_ref[...].astype(F32), yc_ref[...].astype(F32))
        outs = (dyp_ref, dyl_ref, dyc_ref)
        dgs = []
        for j in range(3):
            gj = gts[:, j * D:(j + 1) * D]
            outs[j][...] = (dmv * gj).astype(BF)
            dgs.append(dmv * ys[j] * gj * (1.0 - gj))
        dz_ref[...] = jnp.concatenate(dgs, axis=1).astype(dz_ref.dtype)

    row = lambda i: (i, 0)
    return pl.pallas_call(
        body, name=name, grid=(T // tm,),
        in_specs=[pl.BlockSpec((tm, D), row), pl.BlockSpec((tm, 3 * D), lambda i: (i, ZB_GATE)),
                  pl.BlockSpec((tm, D), row), pl.BlockSpec((tm, D), row), pl.BlockSpec((tm, D), row),
                  pl.BlockSpec((1, D), lambda i: (0, 0))],
        out_specs=[pl.BlockSpec((tm, 3 * D), lambda i: (i, ZB_GATE)), pl.BlockSpec((tm, D), row),
                   pl.BlockSpec((tm, D), row), pl.BlockSpec((tm, D), row)],
        out_shape=[jax.ShapeDtypeStruct((T, IN_COLS), BF), jax.ShapeDtypeStruct((T, D), BF),
                   jax.ShapeDtypeStruct((T, D), BF), jax.ShapeDtypeStruct((T, D), BF)],
        compiler_params=_cp("arbitrary"))(dm, z, yp_pre, yl, yc, ps)


def _attn_probs(qh, kh):
    s = lax.dot_general(qh, kh, (((1,), (1,)), ((), ())), preferred_element_type=F32) * (X_HD ** -0.5)
    e = jnp.exp(s - jnp.max(s, axis=-1, keepdims=True))
    return e / jnp.sum(e, axis=-1, keepdims=True)


def attn_fwd(xb, wq, kb, vb, *, tm, name):
    T = xb.shape[0]
    tm = min(tm, T)

    def body(x_ref, wq_ref, k_ref, v_ref, q_ref, o_ref):
        q = jnp.dot(x_ref[...], wq_ref[...], preferred_element_type=F32).astype(BF)
        q_ref[...] = q
        outs = []
        for h in range(X_HEADS):
            sl = slice(h * X_HD, (h + 1) * X_HD)
            p = _attn_probs(q[:, sl], k_ref[:, sl])
            outs.append(jnp.dot(p.astype(BF), v_ref[:, sl], preferred_element_type=F32))
        o_ref[...] = jnp.concatenate(outs, axis=1).astype(BF)

    row = lambda i: (i, 0)
    fix = lambda i: (0, 0)
    return pl.pallas_call(
        body, name=name, grid=(T // tm,),
        in_specs=[pl.BlockSpec((tm, D), row), pl.BlockSpec((D, D), fix), pl.BlockSpec((N_MEM, D), fix),
                  pl.BlockSpec((N_MEM, D), fix)],
        out_specs=[pl.BlockSpec((tm, D), row), pl.BlockSpec((tm, D), row)],
        out_shape=[jax.ShapeDtypeStruct((T, D), BF), jax.ShapeDtypeStruct((T, D), BF)],
        compiler_params=_cp("arbitrary"))(xb, wq, kb, vb)


def attn_bwd(dxa, wo, q, kb, vb, *, tm, name):
    T = q.shape[0]
    tm = min(tm, T)

    def body(d_ref, wo_ref, q_ref, k_ref, v_ref, dq_ref, dk_ref, dv_ref):
        i = pl.program_id(0)
        do = lax.dot_general(d_ref[...], wo_ref[...], (((1,), (1,)), ((), ())),
                             preferred_element_type=F32).astype(BF)
        q = q_ref[...]
        dqs, dks, dvs = [], [], []
        for h in range(X_HEADS):
            sl = slice(h * X_HD, (h + 1) * X_HD)
            kh, vh = k_ref[:, sl], v_ref[:, sl]
            p = _attn_probs(q[:, sl], kh)
            dp = lax.dot_general(do[:, sl], vh, (((1,), (1,)), ((), ())), preferred_element_type=F32)
            ds = (p * (dp - jnp.sum(dp * p, axis=-1, keepdims=True)) * (X_HD ** -0.5)).astype(BF)
            dqs.append(jnp.dot(ds, kh, preferred_element_type=F32))
            dks.append(lax.dot_general(ds, q[:, sl], (((0,), (0,)), ((), ())), preferred_element_type=F32))
            dvs.append(lax.dot_general(p.astype(BF), do[:, sl], (((0,), (0,)), ((), ())),
                                       preferred_element_type=F32))
        dq_ref[...] = jnp.concatenate(dqs, axis=1).astype(BF)
        dk = jnp.concatenate(dks, axis=1)
        dv = jnp.concatenate(dvs, axis=1)

        @pl.when(i == 0)
        def _():
            dk_ref[...] = dk
            dv_ref[...] = dv

        @pl.when(i > 0)
        def _():
            dk_ref[...] += dk
            dv_ref[...] += dv

    row = lambda i: (i, 0)
    fix = lambda i: (0, 0)
    return pl.pallas_call(
        body, name=name, grid=(T // tm,),
        in_specs=[pl.BlockSpec((tm, D), row), pl.BlockSpec((D, D), fix), pl.BlockSpec((tm, D), row),
                  pl.BlockSpec((N_MEM, D), fix), pl.BlockSpec((N_MEM, D), fix)],
        out_specs=[pl.BlockSpec((tm, D), row), pl.BlockSpec((N_MEM, D), fix), pl.BlockSpec((N_MEM, D), fix)],
        out_shape=[jax.ShapeDtypeStruct((T, D), BF), jax.ShapeDtypeStruct((N_MEM, D), F32),
                   jax.ShapeDtypeStruct((N_MEM, D), F32)],
        compiler_params=_cp("arbitrary"))(dxa, wo, q, kb, vb)


def swiglu_fwd(gu, *, tm, name):
    T = gu.shape[0]
    tm = min(tm, T)

    def body(g_ref, u_ref, o_ref):
        g = g_ref[...].astype(F32)
        o_ref[...] = (g * _sigmoid(g) * u_ref[...].astype(F32)).astype(BF)

    return pl.pallas_call(
        body, name=name, grid=(T // tm,),
        in_specs=[pl.BlockSpec((tm, D_FF), lambda i: (i, 0)), pl.BlockSpec((tm, D_FF), lambda i: (i, 1))],
        out_specs=pl.BlockSpec((tm, D_FF), lambda i: (i, 0)), out_shape=jax.ShapeDtypeStruct((T, D_FF), BF),
        compiler_params=_cp("arbitrary"))(gu, gu)


def swiglu_bwd(dh, gu, *, tm, name):
    T = gu.shape[0]
    tm = min(tm, T)

    def body(dh_ref, g_ref, u_ref, o_ref):
        g = g_ref[...].astype(F32)
        u = u_ref[...].astype(F32)
        dhv = dh_ref[...].astype(F32)
        sg = _sigmoid(g)
        o_ref[:, :D_FF] = (dhv * u * sg * (1.0 + g * (1.0 - sg))).astype(BF)
        o_ref[:, D_FF:] = (dhv * g * sg).astype(BF)

    return pl.pallas_call(
        body, name=name, grid=(T // tm,),
        in_specs=[pl.BlockSpec((tm, D_FF), lambda i: (i, 0)), pl.BlockSpec((tm, D_FF), lambda i: (i, 0)),
                  pl.BlockSpec((tm, D_FF), lambda i: (i, 1))],
        out_specs=pl.BlockSpec((tm, 2 * D_FF), lambda i: (i, 0)),
        out_shape=jax.ShapeDtypeStruct((T, 2 * D_FF), BF), compiler_params=_cp("arbitrary"))(dh, gu, gu)


TM_MM = 1024
TM_EW = 512
TM_SEQ = 512
TT_DW = 2048


def _layer_fwd(l, x, xb, kb, vb, W):
    n = f"l{l}_"
    sv = {'x0': x if xb is None else xb}
    z = mm_nn(sv['x0'], W['w_in'][l], W['b_in'][l], out_dtype=BF, tm=TM_MM, tn=1024, name=n + "in_proj")
    yp = pool_fwd(z, W['pool_w'][l], tm=TM_SEQ, name=n + "pool_fwd")
    h, yl = lru_fwd(z, W['lru_conv_w'][l], W['lru_conv_b'][l], W['lru_w_r'][l], W['lru_b_r'][l], W['lru_w_i'][l],
                    W['lru_b_i'][l], W['lru_lambda'][l], W['lru_w_out'][l], tm=TM_SEQ, name=n + "lru_fwd")
    s, yc = sconv_fwd(z, W['sconv_w'][l], W['sconv_w_out'][l], tm=TM_SEQ, name=n + "sconv_fwd")
    merged = merge_fwd(z, yp, yl, yc, W['pool_scale'][l], tm=TM_EW, name=n + "merge_fwd")
    x1, x1b, xh1, rs1 = mm_res_ln(merged, W['w_mix_out'][l], x, W['ln_g'][l][0:1], W['ln_b'][l][0:1], tm=TM_EW,
                                  name=n + "mix_out_ln")
    q, o = attn_fwd(x1b, W['xa_w_q'][l], kb, vb, tm=TM_EW, name=n + "attn_fwd")
    x2, x2b, xh2, rs2 = mm_res_ln(o, W['xa_w_o'][l], x1, W['ln_g'][l][1:2], W['ln_b'][l][1:2], tm=TM_EW,
                                  name=n + "attn_out_ln")
    gu = mm_nn(x2b, W['ffn_w_gu'][l], None, out_dtype=BF, tm=TM_MM, tn=1408, name=n + "ffn_in")
    hdn = swiglu_fwd(gu, tm=TM_EW, name=n + "swiglu_fwd")
    x3, x3b, xh3, rs3 = mm_res_ln(hdn, W['ffn_w_down'][l], x2, W['ln_g'][l][2:3], W['ln_b'][l][2:3], tm=TM_EW,
                                  name=n + "ffn_out_ln")
    sv.update(z=z, yp=yp, h=h, yl=yl, s=s, yc=yc, merged=merged, x1b=x1b, xh1=xh1, rs1=rs1, q=q, o=o, x2b=x2b,
              xh2=xh2, rs2=rs2, gu=gu, hdn=hdn, xh3=xh3, rs3=rs3)
    return x3, x3b, sv


def _layer_bwd(l, dx3, sv, memb, kb, vb, W, loss_from=None):
    n = f"l{l}_"
    G = {}
    res = ln_bwd(dx3, sv['xh3'], sv['rs3'], W['ln_g'][l][2:3], tm=TM_EW, name=n + "ln3_bwd", loss_from=loss_from)
    dp3, dp3b, dg3, db3 = res[:4]
    loss = res[4] if loss_from is not None else None
    dhdn = mm_nt(dp3b, W['ffn_w_down'][l], None, out_dtype=BF, tm=TM_EW, tc=D, name=n + "ffn_down_dx")
    dgu = swiglu_bwd(dhdn, sv['gu'], tm=TM_EW, name=n + "swiglu_bwd")
    dx2 = mm_nt(dgu, W['ffn_w_gu'][l], dp3, out_dtype=F32, tm=TM_MM, tc=1408, name=n + "ffn_in_dx")
    G['ffn_w_gu'] = mm_tn(sv['x2b'], dgu, out_dtype=BF, tk=1024, tn=1408, tt=TT_DW,name=n + "ffn_in_dw")
    G['ffn_w_down'] = mm_tn(sv['hdn'], dp3b, out_dtype=BF, tk=1408, tn=1024, tt=TT_DW,name=n + "ffn_down_dw")

    dp2, dp2b, dg2, db2 = ln_bwd(dx2, sv['xh2'], sv['rs2'], W['ln_g'][l][1:2], tm=TM_EW, name=n + "ln2_bwd")
    dq, dk, dv = attn_bwd(dp2b, W['xa_w_o'][l], sv['q'], kb, vb, tm=TM_EW, name=n + "attn_bwd")
    dx1 = mm_nt(dq, W['xa_w_q'][l], dp2, out_dtype=F32, tm=TM_MM, tc=D, name=n + "attn_q_dx")
    G['xa_w_o'] = mm_tn(sv['o'], dp2b, out_dtype=BF, tk=1024, tn=1024, tt=TT_DW,name=n + "attn_o_dw")
    G['xa_w_q'] = mm_tn(sv['x1b'], dq, out_dtype=BF, tk=1024, tn=1024, tt=TT_DW,name=n + "attn_q_dw")
    G['xa_w_k'] = mm_tn(memb, dk, out_dtype=BF, tk=1024, tn=1024, tt=N_MEM, name=n + "attn_k_dw")
    G['xa_w_v'] = mm_tn(memb, dv, out_dtype=BF, tk=1024, tn=1024, tt=N_MEM, name=n + "attn_v_dw")

    dp1, dp1b, dg1, db1 = ln_bwd(dx1, sv['xh1'], sv['rs1'], W['ln_g'][l][0:1], tm=TM_EW, name=n + "ln1_bwd")
    dmerged = mm_nt(dp1b, W['w_mix_out'][l], None, out_dtype=BF, tm=TM_MM, tc=D, name=n + "mix_out_dx")
    G['w_mix_out'] = mm_tn(sv['merged'], dp1b, out_dtype=BF, tk=1024, tn=1024, tt=TT_DW,name=n + "mix_out_dw")
    z = sv['z']
    dz, dyp, dyl, dyc = merge_bwd(dmerged, z, sv['yp'], sv['yl'], sv['yc'], W['pool_scale'][l], tm=TM_EW,
                                  name=n + "merge_bwd")
    dz, G['pool_w'], G['pool_scale'] = pool_bwd(dz, dyp, sv['yp'], z, W['pool_w'][l], W['pool_scale'][l],
                                                tm=TM_SEQ, name=n + "pool_bwd")
    (dz, G['lru_conv_w'], G['lru_conv_b'], G['lru_w_r'], G['lru_b_r'], G['lru_w_i'], G['lru_b_i'],
     G['lru_lambda']) = lru_bwd(dz, dyl, z, sv['h'], W['lru_conv_w'][l], W['lru_conv_b'][l], W['lru_w_r'][l],
                                W['lru_b_r'][l], W['lru_w_i'][l], W['lru_b_i'][l], W['lru_lambda'][l],
                                W['lru_w_out'][l], tm=TM_SEQ, name=n + "lru_bwd")
    dz, G['sconv_w'] = sconv_bwd(dz, dyc, z, W['sconv_w'][l], W['sconv_w_out'][l], tm=TM_SEQ, name=n + "sconv_bwd")
    G['lru_w_out'] = mm_tn(sv['h'], dyl, out_dtype=BF, tk=1024, tn=1024, tt=TT_DW,name=n + "lru_out_dw")
    G['sconv_w_out'] = mm_tn(sv['s'], dyc, out_dtype=BF, tk=1024, tn=1024, tt=TT_DW,name=n + "sconv_out_dw")
    dx0 = mm_nt(dz, W['w_in'][l], dp1, out_dtype=F32, tm=TM_MM, tc=2048, name=n + "in_proj_dx")
    G['w_in'], G['b_in'] = mm_tn(sv['x0'], dz, out_dtype=BF, tk=1024, tn=1024, tt=TT_DW,name=n + "in_proj_dw",
                                 colsum=True)
    G['ln_g'] = jnp.concatenate([dg1, dg2, dg3], axis=0)
    G['ln_b'] = jnp.concatenate([db1, db2, db3], axis=0)
    return dx0, G, loss


def local_step(x, mem, target, W):
    memb = mem.astype(BF)
    saves, kvs = [], []
    xf, xb = x, None
    for l in range(DEPTH):
        kb = mm_nn(memb, W['xa_w_k'][l], None, out_dtype=BF, tm=N_MEM, tn=1024, name=f"l{l}_mem_k")
        vb = mm_nn(memb, W['xa_w_v'][l], None, out_dtype=BF, tm=N_MEM, tn=1024, name=f"l{l}_mem_v")
        xf, xb, sv = _layer_fwd(l, xf, xb, kb, vb, W)
        saves.append(sv)
        kvs.append((kb, vb))
    grads = [None] * DEPTH
    dx, loss = None, None
    for l in reversed(range(DEPTH)):
        lf = (W['ln_b'][l][2:3], target) if l == DEPTH - 1 else None
        dx, grads[l], ls = _layer_bwd(l, dx, saves[l], memb, kvs[l][0], kvs[l][1], W, loss_from=lf)
        if ls is not None:
            loss = ls
    return loss, dx, grads


def _coords():
    return lax.axis_index("x"), lax.axis_index("y"), lax.axis_index("c")


FLIPS = ((1, 0), (0, 1), (1, 1))
SQUARES = ('lru_w_out', 'sconv_w_out', 'w_mix_out', 'xa_w_q', 'xa_w_k', 'xa_w_v', 'xa_w_o')
LAYER_SHAPE = {'w_in': (D, IN_COLS), 'pool_w': (4, POOL_GD, POOL_GD), 'ffn_w_gate': (4, D, D_FF // 4),
               'ffn_w_up': (4, D, D_FF // 4), 'ffn_w_down': (D_FF, D), **{n: (D, D) for n in SQUARES}}
PIECES = ('w_in', 'pool_w') + SQUARES + ('ffn_w_gate', 'ffn_w_up', 'ffn_w_down')


def _win(name, ref, k):
    if name == 'w_in':
        return ref.at[:, pl.ds(pl.multiple_of(((2 * k + 6) % 8) * D, D), 2 * D)]
    if name == 'pool_w':
        return ref.at[:, pl.ds(pl.multiple_of(k * (POOL_GD // 4), POOL_GD // 4), POOL_GD // 4), :]
    if name in ('ffn_w_gate', 'ffn_w_up'):
        return ref.at[k]
    rows = LAYER_SHAPE[name][0] // 4
    return ref.at[pl.ds(pl.multiple_of(k * rows, 16), rows), :]


def gather_weights(shards, small):
    n_p = len(PIECES)

    def body(*refs):
        srcs = dict(zip(PIECES, refs[:n_p]))
        small_ref = refs[n_p]
        outs = [dict(zip(PIECES, refs[n_p + 1 + l * n_p:n_p + 1 + (l + 1) * n_p])) for l in range(DEPTH)]
        gs_ref = refs[n_p + 1 + DEPTH * n_p]
        ici_send, ici_recv, d2d_send, d2d_recv, own_send, own_recv = refs[n_p + 2 + DEPTH * n_p:]
        x, y, c = _coords()
        me = 2 * x + y
        sib = (x, y, 1 - c)

        def own_copies():
            cps = []
            li = 0
            for n in PIECES:
                for l in range(DEPTH):
                    cps.append(pltpu.make_async_remote_copy(
                        src_ref=srcs[n].at[l], dst_ref=_win(n, outs[l][n], me), send_sem=own_send.at[li],
                        recv_sem=own_recv.at[li], device_id=sib, device_id_type=MESH))
                    li += 1
            cps.append(pltpu.make_async_remote_copy(
                src_ref=small_ref, dst_ref=gs_ref.at[me], send_sem=own_send.at[li], recv_sem=own_recv.at[li],
                device_id=sib, device_id_type=MESH))
            return cps

        def run(lc):
            sends = []
            for j, (fx, fy) in enumerate(FLIPS):
                peer = (x ^ fx, y ^ fy, c)
                for p, n in enumerate(PIECES):
                    k = 3 * p + j
                    sends.append(pltpu.make_async_remote_copy(
                        src_ref=srcs[n].at[lc], dst_ref=_win(n, outs[lc][n], me), send_sem=ici_send.at[k],
                        recv_sem=ici_recv.at[k], device_id=peer, device_id_type=MESH))
                k = 3 * n_p + j
                sends.append(pltpu.make_async_remote_copy(
                    src_ref=small_ref, dst_ref=gs_ref.at[me], send_sem=ici_send.at[k], recv_sem=ici_recv.at[k],
                    device_id=peer, device_id_type=MESH))
            own = own_copies()
            for cp in sends + own:
                cp.start()
            for j, (fx, fy) in enumerate(FLIPS):
                other = 2 * (x ^ fx) + (y ^ fy)
                for p, n in enumerate(PIECES):
                    k = 3 * p + j
                    w = _win(n, outs[lc][n], other)
                    pltpu.make_async_remote_copy(src_ref=srcs[n].at[lc], dst_ref=w, send_sem=ici_send.at[k],
                                                 recv_sem=ici_recv.at[k], device_id=sib,
                                                 device_id_type=MESH).wait_recv()
                    fw = pltpu.make_async_remote_copy(src_ref=w, dst_ref=w, send_sem=d2d_send.at[k],
                                                      recv_sem=d2d_recv.at[k], device_id=sib, device_id_type=MESH)
                    fw.start()
                    sends.append(fw)
                k = 3 * n_p + j
                pltpu.make_async_remote_copy(src_ref=small_ref, dst_ref=gs_ref.at[other], send_sem=ici_send.at[k],
                                             recv_sem=ici_recv.at[k], device_id=sib, device_id_type=MESH).wait_recv()
            for j, (fx, fy) in enumerate(FLIPS):
                other = 2 * (x ^ fx) + (y ^ fy)
                for p, n in enumerate(PIECES):
                    k = 3 * p + j
                    w = _win(n, outs[1 - lc][n], other)
                    pltpu.make_async_remote_copy(src_ref=w, dst_ref=w, send_sem=d2d_send.at[k],
                                                 recv_sem=d2d_recv.at[k], device_id=sib,
                                                 device_id_type=MESH).wait_recv()
            for cp in sends:
                cp.wait_send()
            for cp in own:
                cp.wait()

        @pl.when(c == 0)
        def _():
            run(0)

        @pl.when(c == 1)
        def _():
            run(1)

    hbm = pl.BlockSpec(memory_space=pl.ANY)
    n_out = DEPTH * n_p + 1
    res = pl.pallas_call(
        body, name="gather_weights", in_specs=[hbm] * (n_p + 1), out_specs=[hbm] * n_out,
        out_shape=[jax.ShapeDtypeStruct(LAYER_SHAPE[n], BF) for _ in range(DEPTH) for n in PIECES]
        + [jax.ShapeDtypeStruct((4,) + small.shape, small.dtype)],
        scratch_shapes=[pltpu.SemaphoreType.DMA((3 * n_p + 3,)), pltpu.SemaphoreType.DMA((3 * n_p + 3,)),
                        pltpu.SemaphoreType.DMA((3 * n_p,)), pltpu.SemaphoreType.DMA((3 * n_p,)),
                        pltpu.SemaphoreType.DMA((DEPTH * n_p + 1,)), pltpu.SemaphoreType.DMA((DEPTH * n_p + 1,))],
    )(*[shards[n] for n in PIECES], small)
    full = {n: [res[l * n_p + p] for l in range(DEPTH)] for p, n in enumerate(PIECES)}
    return full, res[DEPTH * n_p]


def split_layers(p0, p1):
    n = len(p0)

    def body(*refs):
        a0, a1 = refs[:n], refs[n:2 * n]
        theirs = refs[2 * n:3 * n]
        send_sems, recv_sems = refs[3 * n:]
        x, y, c = _coords()
        sib = (x, y, 1 - c)

        def give(arrs):
            for i in range(n):
                pltpu.make_async_remote_copy(src_ref=arrs[i], dst_ref=theirs[i], send_sem=send_sems.at[i],
                                             recv_sem=recv_sems.at[i], device_id=sib, device_id_type=MESH).start()

        @pl.when(c == 0)
        def _():
            give(a1)

        @pl.when(c == 1)
        def _():
            give(a0)

        for i in range(n):
            pltpu.make_async_remote_copy(src_ref=a0[i], dst_ref=theirs[i], send_sem=send_sems.at[i],
                                         recv_sem=recv_sems.at[i], device_id=sib, device_id_type=MESH).wait()

    hbm = pl.BlockSpec(memory_space=pl.ANY)
    return pl.pallas_call(
        body, name="split_layers", in_specs=[hbm] * (2 * n), out_specs=[hbm] * n,
        out_shape=[jax.ShapeDtypeStruct(a.shape, a.dtype) for a in p0],
        scratch_shapes=[pltpu.SemaphoreType.DMA((n,)), pltpu.SemaphoreType.DMA((n,))],
    )(*p0, *p1)


def exchange_chips(q, qsmall):
    n_p = len(PIECES)

    def body(*refs):
        srcs = dict(zip(PIECES, refs[:n_p]))
        s_ref = refs[n_p]
        outs = dict(zip(PIECES, refs[n_p + 1:2 * n_p + 1]))
        so_ref = refs[2 * n_p + 1]
        send_sems, recv_sems = refs[2 * n_p + 2:]
        x, y, c = _coords()
        me = 2 * x + y
        sends = []
        for j, (fx, fy) in enumerate(FLIPS):
            peer = (x ^ fx, y ^ fy, c)
            other = 2 * (x ^ fx) + (y ^ fy)
            for p, n in enumerate(PIECES):
                k = 3 * p + j
                sends.append(pltpu.make_async_remote_copy(
                    src_ref=_win(n, srcs[n], other), dst_ref=outs[n].at[j], send_sem=send_sems.at[k],
                    recv_sem=recv_sems.at[k], device_id=peer, device_id_type=MESH))
            k = 3 * n_p + j
            sends.append(pltpu.make_async_remote_copy(
                src_ref=s_ref, dst_ref=so_ref.at[j], send_sem=send_sems.at[k], recv_sem=recv_sems.at[k],
                device_id=peer, device_id_type=MESH))
        for cp in sends:
            cp.start()
        for cp in sends:
            cp.wait()

    hbm = pl.BlockSpec(memory_space=pl.ANY)
    res = pl.pallas_call(
        body, name="exchange_chips", in_specs=[hbm] * (n_p + 1), out_specs=[hbm] * (n_p + 1),
        out_shape=[jax.ShapeDtypeStruct((3,) + _shard_shape(n), BF) for n in PIECES]
        + [jax.ShapeDtypeStruct((3,) + qsmall.shape, qsmall.dtype)],
        scratch_shapes=[pltpu.SemaphoreType.DMA((3 * n_p + 3,)), pltpu.SemaphoreType.DMA((3 * n_p + 3,))],
    )(*[q[n] for n in PIECES], qsmall)
    return dict(zip(PIECES, res[:n_p])), res[n_p]


def _shard_shape(n):
    shp = LAYER_SHAPE[n]
    if n == 'w_in':
        return (shp[0], shp[1] // 4)
    if n == 'pool_w':
        return (shp[0], shp[1] // 4, shp[2])
    if n in ('ffn_w_gate', 'ffn_w_up'):
        return shp[1:]
    return (shp[0] // 4, shp[1])


def swap_cores(s):
    n = len(s)

    def body(*refs):
        srcs, outs = refs[:n], refs[n:2 * n]
        send_sems, recv_sems = refs[2 * n:]
        x, y, c = _coords()
        cps = [pltpu.make_async_remote_copy(src_ref=srcs[i], dst_ref=outs[i], send_sem=send_sems.at[i],
                                            recv_sem=recv_sems.at[i], device_id=(x, y, 1 - c), device_id_type=MESH)
               for i in range(n)]
        for cp in cps:
            cp.start()
        for cp in cps:
            cp.wait()

    hbm = pl.BlockSpec(memory_space=pl.ANY)
    return pl.pallas_call(
        body, name="swap_cores", in_specs=[hbm] * n, out_specs=[hbm] * n,
        out_shape=[jax.ShapeDtypeStruct(a.shape, a.dtype) for a in s],
        scratch_shapes=[pltpu.SemaphoreType.DMA((n,)), pltpu.SemaphoreType.DMA((n,))],
    )(*s)


def add_cores(p0, p1, theirs, *, out_dtype, name):
    shp = p0.shape
    args = [t.reshape(-1, shp[-1]) for t in (p0, p1, theirs)]
    R, C = args[0].shape
    tr = _row_tile(R, C)

    def body(a0_ref, a1_ref, t_ref, o_ref):
        c = lax.axis_index("c")
        t = t_ref[...].astype(F32)

        @pl.when(c == 0)
        def _():
            o_ref[...] = (a0_ref[...].astype(F32) + t).astype(o_ref.dtype)

        @pl.when(c == 1)
        def _():
            o_ref[...] = (a1_ref[...].astype(F32) + t).astype(o_ref.dtype)

    spec = pl.BlockSpec((tr, C), lambda i: (i, 0))
    out = pl.pallas_call(body, name=name, grid=(R // tr,), in_specs=[spec] * 3, out_specs=spec,
                         out_shape=jax.ShapeDtypeStruct((R, C), out_dtype), compiler_params=_cp("arbitrary"))(*args)
    return out.reshape(shp)


def sum_chips(name, q, recv3, chip):
    shard = _shard_shape(name)
    zero = (0,) * len(shard)
    if name == 'w_in':
        tr = 128
        grid = (shard[0] // tr,)
        qspec = pl.BlockSpec((tr, shard[1]), lambda i, me: (i, (me[0] + 3) % 4))
        rspec = pl.BlockSpec((3, tr, shard[1]), lambda i, me: (0, i, 0))
        ospec = pl.BlockSpec((tr, shard[1]), lambda i, me: (i, 0))
    else:
        grid = (1,)
        rspec = pl.BlockSpec((3,) + shard, lambda i, me: (0,) + zero)
        ospec = pl.BlockSpec(shard, lambda i, me: zero)
        if name == 'pool_w':
            qspec = pl.BlockSpec(shard, lambda i, me: (0, me[0], 0))
        elif name in ('ffn_w_gate', 'ffn_w_up'):
            qspec = pl.BlockSpec((None,) + shard, lambda i, me: (me[0], 0, 0))
        else:
            qspec = pl.BlockSpec(shard, lambda i, me: (me[0], 0))

    def body(me_ref, q_ref, r_ref, o_ref):
        del me_ref
        acc = q_ref[...].astype(F32)
        for j in range(3):
            acc = acc + r_ref[j].astype(F32)
        o_ref[...] = acc

    return pl.pallas_call(
        body, name="sum_chips_" + name,
        grid_spec=pltpu.PrefetchScalarGridSpec(num_scalar_prefetch=1, grid=grid, in_specs=[qspec, rspec],
                                               out_specs=ospec),
        out_shape=jax.ShapeDtypeStruct(shard, F32), compiler_params=_cp("arbitrary"))(chip, q, recv3)


def sum_chips_small(q, recv3, chip):
    r, C = q.shape
    slot_of_xor = {2: 0, 1: 1, 3: 2}

    def body(me_ref, q_ref, r_ref, o_ref):
        me = me_ref[0]
        acc = None
        for k in range(4):
            kx = k ^ me
            term = q_ref[...]
            for xv, j in slot_of_xor.items():
                term = jnp.where(kx == xv, r_ref[j], term)
            acc = term if acc is None else acc + term
        o_ref[...] = acc

    return pl.pallas_call(
        body, name="sum_chips_small",
        grid_spec=pltpu.PrefetchScalarGridSpec(
            num_scalar_prefetch=1, grid=(1,),
            in_specs=[pl.BlockSpec((r, C), lambda i, me: (0, 0)), pl.BlockSpec((3, r, C), lambda i, me: (0, 0, 0))],
            out_specs=pl.BlockSpec((r, C), lambda i, me: (0, 0))),
        out_shape=jax.ShapeDtypeStruct((r, C), F32), compiler_params=_cp("arbitrary"))(chip, q, recv3)


def _row_tile(R, C):
    for cand in (1024, 512, 256, 128, 64, 32, 16):
        if R % cand == 0 and cand * C * 4 <= 2 * 1024 * 1024:
            return cand
    return R


def sum_slots(a, *, name):
    n = a.shape[0]
    shp = a.shape[1:]
    a3 = a.reshape(n, -1, shp[-1])
    R, C = a3.shape[1:]
    tr = _row_tile(R, C * n // 2)

    def body(a_ref, o_ref):
        acc = a_ref[0].astype(F32)
        for k in range(1, n):
            acc = acc + a_ref[k].astype(F32)
        o_ref[...] = acc

    out = pl.pallas_call(
        body, name=name, grid=(R // tr,), in_specs=[pl.BlockSpec((n, tr, C), lambda i: (0, i, 0))],
        out_specs=pl.BlockSpec((tr, C), lambda i: (i, 0)), out_shape=jax.ShapeDtypeStruct((R, C), F32),
        compiler_params=_cp("arbitrary"))(a3)
    return out.reshape(shp)


def _adamw_math(w, g, m, v):
    mn = ADAM_B1 * m + (1.0 - ADAM_B1) * g
    vn = ADAM_B2 * v + (1.0 - ADAM_B2) * (g * g)
    m_hat = mn / (1.0 - ADAM_B1 ** ADAM_STEP)
    v_hat = vn / (1.0 - ADAM_B2 ** ADAM_STEP)
    return -ADAM_LR * (m_hat / (jnp.sqrt(v_hat) + ADAM_EPS) + ADAM_WD * w), mn, vn


def adamw(w, g, m, v, *, name):
    shp = w.shape
    args = [t.reshape(-1, shp[-1]) for t in (w, g, m, v)]
    R, C = args[0].shape
    tr = _row_tile(R, C)

    def body(w_ref, g_ref, m_ref, v_ref, d_ref, mo_ref, vo_ref):
        d_ref[...], mo_ref[...], vo_ref[...] = _adamw_math(w_ref[...], g_ref[...], m_ref[...], v_ref[...])

    spec = pl.BlockSpec((tr, C), lambda i: (i, 0))
    res = pl.pallas_call(
        body, name=name, grid=(R // tr,), in_specs=[spec] * 4, out_specs=[spec] * 3,
        out_shape=[jax.ShapeDtypeStruct((R, C), F32)] * 3, compiler_params=_cp("arbitrary"))(*args)
    return [r.reshape(shp) for r in res]


def adamw_layers(w, g_mine, g_theirs, m, v, *, name):
    shp = w.shape
    three = (DEPTH, -1, shp[-1])
    w3, m3, v3 = [t.reshape(three) for t in (w, m, v)]
    ga, gb = [t.reshape(-1, shp[-1]) for t in (g_mine, g_theirs)]
    R, C = ga.shape
    tr = _row_tile(R, C)

    def body(w_ref, ga_ref, gb_ref, m_ref, v_ref, g_ref, d_ref, mo_ref, vo_ref):
        mine = pl.program_id(0) == lax.axis_index("c")
        g = jnp.where(mine, ga_ref[...], gb_ref[...])
        g_ref[...] = g
        d_ref[...], mo_ref[...], vo_ref[...] = _adamw_math(w_ref[...], g, m_ref[...], v_ref[...])

    lay = pl.BlockSpec((None, tr, C), lambda l, i: (l, i, 0))
    one = pl.BlockSpec((tr, C), lambda l, i: (i, 0))
    res = pl.pallas_call(
        body, name=name, grid=(DEPTH, R // tr), in_specs=[lay, one, one, lay, lay], out_specs=[lay] * 4,
        out_shape=[jax.ShapeDtypeStruct(w3.shape, F32)] * 4, compiler_params=_cp("arbitrary", "arbitrary"))(
            w3, ga, gb, m3, v3)
    return [r.reshape(shp) for r in res]


def _local_shape(name, full_shape):
    shp = list(full_shape)
    ax = BIG_SHARDED.get(name, SMALL_SHARDED.get(name))
    if ax is not None:
        shp[ax] //= 4
    return tuple(shp)


FULL_SHAPES = {
    'w_in': (DEPTH, D, IN_COLS), 'b_in': (DEPTH, IN_COLS), 'pool_w': (DEPTH, 4, POOL_GD, POOL_GD),
    'pool_scale': (DEPTH, D), 'lru_conv_w': (DEPTH, 4, D), 'lru_conv_b': (DEPTH, D),
    'lru_w_r': (DEPTH, LRU_HEADS, LRU_HD, LRU_HD), 'lru_b_r': (DEPTH, D),
    'lru_w_i': (DEPTH, LRU_HEADS, LRU_HD, LRU_HD), 'lru_b_i': (DEPTH, D), 'lru_lambda': (DEPTH, D),
    'lru_w_out': (DEPTH, D, D), 'sconv_w': (DEPTH, 3, D), 'sconv_w_out': (DEPTH, D, D), 'w_mix_out': (DEPTH, D, D),
    'xa_w_q': (DEPTH, D, D), 'xa_w_k': (DEPTH, D, D), 'xa_w_v': (DEPTH, D, D), 'xa_w_o': (DEPTH, D, D),
    'ffn_w_gate': (DEPTH, D, D_FF), 'ffn_w_up': (DEPTH, D, D_FF), 'ffn_w_down': (DEPTH, D_FF, D),
    'ln_g': (DEPTH, 3, D), 'ln_b': (DEPTH, 3, D)}


def _pack(arrs, names, width, dtype, row_mult):
    flat = jnp.concatenate([arrs[n].astype(dtype).reshape(-1) for n in names])
    pad = (-flat.shape[0]) % (width * row_mult)
    if pad:
        flat = jnp.concatenate([flat, jnp.zeros((pad,), dtype)])
    return flat.reshape(-1, width)


def _unpack(flat2d, names, shapes):
    flat = flat2d.reshape(-1)
    out, off = {}, 0
    for n in names:
        size = 1
        for s in shapes[n]:
            size *= s
        out[n] = flat[off:off + size].reshape(shapes[n])
        off += size
    return out


def _gathered_full(g4, names, sharded_axis):
    loc_shapes = {n: _local_shape(n, FULL_SHAPES[n]) for n in names}
    per = [_unpack(g4[k], names, loc_shapes) for k in range(4)]
    return {n: jnp.concatenate([per[k][n] for k in range(4)], axis=sharded_axis[n]) for n in names}


def _perm_cols(a, perm, axis):
    blocks = [lax.slice_in_dim(a, p * D, (p + 1) * D, axis=axis) for p in perm]
    return jnp.concatenate(blocks, axis=axis)


def _shards_apart(a):
    w = a.shape[1] // 4
    return jnp.stack([a[:, k * w:(k + 1) * w] for k in range(4)])


def _shards_joined(a4):
    return jnp.concatenate([a4[k] for k in range(4)], axis=1)


Z_INV = tuple(Z_PERM.index(j) for j in range(8))
SMALL_SH_NAMES = list(SMALL_SHARDED)
SMALL_ROWS = 32


def kernel(x, mem, w_in, b_in, pool_w, pool_scale, lru_conv_w, lru_conv_b, lru_w_r, lru_b_r, lru_w_i, lru_b_i, lru_lambda, lru_w_out, sconv_w, sconv_w_out, w_mix_out, xa_w_q, xa_w_k, xa_w_v, xa_w_o, ffn_w_gate, ffn_w_up, ffn_w_down, ln_g, ln_b, loss_target, m_w_in, m_b_in, m_pool_w, m_pool_scale, m_lru_conv_w, m_lru_conv_b, m_lru_w_r, m_lru_b_r, m_lru_w_i, m_lru_b_i, m_lru_lambda, m_lru_w_out, m_sconv_w, m_sconv_w_out, m_w_mix_out, m_xa_w_q, m_xa_w_k, m_xa_w_v, m_xa_w_o, m_ffn_w_gate, m_ffn_w_up, m_ffn_w_down, m_ln_g, m_ln_b, v_w_in, v_b_in, v_pool_w, v_pool_scale, v_lru_conv_w, v_lru_conv_b, v_lru_w_r, v_lru_b_r, v_lru_w_i, v_lru_b_i, v_lru_lambda, v_lru_w_out, v_sconv_w, v_sconv_w_out, v_w_mix_out, v_xa_w_q, v_xa_w_k, v_xa_w_v, v_xa_w_o, v_ffn_w_gate, v_ffn_w_up, v_ffn_w_down, v_ln_g, v_ln_b):
    loc = dict(w_in=w_in, b_in=b_in, pool_w=pool_w, pool_scale=pool_scale, lru_conv_w=lru_conv_w,
               lru_conv_b=lru_conv_b, lru_w_r=lru_w_r, lru_b_r=lru_b_r, lru_w_i=lru_w_i, lru_b_i=lru_b_i,
               lru_lambda=lru_lambda, lru_w_out=lru_w_out, sconv_w=sconv_w, sconv_w_out=sconv_w_out,
               w_mix_out=w_mix_out, xa_w_q=xa_w_q, xa_w_k=xa_w_k, xa_w_v=xa_w_v, xa_w_o=xa_w_o,
               ffn_w_gate=ffn_w_gate, ffn_w_up=ffn_w_up, ffn_w_down=ffn_w_down, ln_g=ln_g, ln_b=ln_b)
    mom = dict(w_in=m_w_in, b_in=m_b_in, pool_w=m_pool_w, pool_scale=m_pool_scale, lru_conv_w=m_lru_conv_w,
               lru_conv_b=m_lru_conv_b, lru_w_r=m_lru_w_r, lru_b_r=m_lru_b_r, lru_w_i=m_lru_w_i, lru_b_i=m_lru_b_i,
               lru_lambda=m_lru_lambda, lru_w_out=m_lru_w_out, sconv_w=m_sconv_w, sconv_w_out=m_sconv_w_out,
               w_mix_out=m_w_mix_out, xa_w_q=m_xa_w_q, xa_w_k=m_xa_w_k, xa_w_v=m_xa_w_v, xa_w_o=m_xa_w_o,
               ffn_w_gate=m_ffn_w_gate, ffn_w_up=m_ffn_w_up, ffn_w_down=m_ffn_w_down, ln_g=m_ln_g, ln_b=m_ln_b)
    var = dict(w_in=v_w_in, b_in=v_b_in, pool_w=v_pool_w, pool_scale=v_pool_scale, lru_conv_w=v_lru_conv_w,
               lru_conv_b=v_lru_conv_b, lru_w_r=v_lru_w_r, lru_b_r=v_lru_b_r, lru_w_i=v_lru_w_i, lru_b_i=v_lru_b_i,
               lru_lambda=v_lru_lambda, lru_w_out=v_lru_w_out, sconv_w=v_sconv_w, sconv_w_out=v_sconv_w_out,
               w_mix_out=v_w_mix_out, xa_w_q=v_xa_w_q, xa_w_k=v_xa_w_k, xa_w_v=v_xa_w_v, xa_w_o=v_xa_w_o,
               ffn_w_gate=v_ffn_w_gate, ffn_w_up=v_ffn_w_up, ffn_w_down=v_ffn_w_down, ln_g=v_ln_g, ln_b=v_ln_b)

    small = _pack(loc, SMALL_SH_NAMES, 256, F32, 8)
    full, gsmall = gather_weights({n: loc[n].astype(BF) for n in PIECES}, small)
    full.update(_gathered_full(gsmall, SMALL_SH_NAMES, SMALL_SHARDED))
    W = {}
    W['b_in'] = [jnp.roll(b_in[l:l + 1], -2 * D, axis=1) for l in range(DEPTH)]
    W['ffn_w_gu'] = [jnp.concatenate([_shards_joined(full['ffn_w_gate'][l]), _shards_joined(full['ffn_w_up'][l])],
                                     axis=1) for l in range(DEPTH)]
    for n in ('w_in', 'pool_w') + SQUARES + ('ffn_w_down', 'lru_conv_w', 'sconv_w', 'ln_g', 'ln_b'):
        W[n] = [full[n][l] for l in range(DEPTH)]
    for n in ('lru_w_r', 'lru_w_i'):
        W[n] = [loc[n][l].astype(BF) for l in range(DEPTH)]
    for n in ('pool_scale', 'lru_conv_b', 'lru_b_r', 'lru_b_i', 'lru_lambda'):
        W[n] = [loc[n][l:l + 1] for l in range(DEPTH)]

    loss_blk, grad_x, G = local_step(x[0], mem[0], loss_target[0], W)
    loss = lax.psum(loss_blk[0, 0], ("x", "y", "c"))

    small_shapes = {n: FULL_SHAPES[n][1:] for n in SMALL_ALL}
    parts = []
    for l in range(DEPTH):
        g = dict(G[l])
        g['ffn_w_gate'] = _shards_apart(g['ffn_w_gu'][:, :D_FF])
        g['ffn_w_up'] = _shards_apart(g['ffn_w_gu'][:, D_FF:])
        g['pool_w'] = g['pool_w'].astype(BF)
        g['b_in'] = jnp.roll(g['b_in'], 2 * D, axis=1)
        parts.append([g[n] for n in PIECES] + [_pack(g, SMALL_ALL, D, F32, SMALL_ROWS)])

    theirs = split_layers(parts[0], parts[1])
    names = PIECES + ('small',)
    q = {n: add_cores(parts[0][i], parts[1][i], theirs[i], out_dtype=F32 if n == 'small' else BF,
                      name="add_cores_" + n) for i, n in enumerate(names)}
    recv, recv_small = exchange_chips(q, q['small'])
    chip = 2 * lax.axis_index("x") + lax.axis_index("y")
    chip_arr = jnp.reshape(chip, (1,)).astype(jnp.int32)
    sums = [sum_chips(n, q[n], recv[n], chip_arr) for n in PIECES] + [sum_chips_small(q['small'], recv_small, chip_arr)]
    other = swap_cores(sums)
    core = lax.axis_index("c")
    small2 = jnp.where(core == 0, jnp.stack([sums[-1], other[-1]]), jnp.stack([other[-1], sums[-1]]))
    per_layer = [_unpack(small2[l], SMALL_ALL, small_shapes) for l in range(DEPTH)]
    grads = {}
    for n in SMALL_ALL:
        gn = jnp.stack([per_layer[l][n] for l in range(DEPTH)])
        if n in SMALL_SHARDED:
            size = loc[n].shape[SMALL_SHARDED[n]]
            gn = lax.dynamic_slice_in_dim(gn, chip * size, size, axis=SMALL_SHARDED[n])
        grads[n] = gn

    out_d, out_m, out_v = {}, {}, {}
    for i, n in enumerate(PIECES):
        grads[n], out_d[n], out_m[n], out_v[n] = adamw_layers(loc[n], sums[i], other[i], mom[n], var[n],
                                                              name="adamw_" + n)
    for n in SMALL_ALL:
        out_d[n], out_m[n], out_v[n] = adamw(loc[n], grads[n], mom[n], var[n], name="adamw_" + n)

    return (loss, grad_x[None], *[grads[n] for n in WEIGHTS], *[out_d[n] for n in WEIGHTS],
            *[out_m[n] for n in WEIGHTS], *[out_v[n] for n in WEIGHTS])
```

```python
import functools

import jax
import jax.numpy as jnp
from jax import lax
from jax.experimental import pallas as pl
from jax.experimental.pallas import tpu as pltpu

F32 = jnp.float32
BF = jnp.bfloat16
MESH = pl.DeviceIdType.MESH

D = 1024
DEPTH = 2
N_MEM = 256
POOL_WINDOWS = (2, 4, 8, 16)
POOL_GD = 256
LRU_HEADS = 8
LRU_HD = 128
LRU_C = 8.0
X_HEADS = 4
X_HD = 256
D_FF = 2816
IN_COLS = 8 * D
ALPHA = (2 * DEPTH) ** 0.25
LN_EPS = 1e-5
ADAM_LR = 0.001
ADAM_B1 = 0.9
ADAM_B2 = 0.999
ADAM_EPS = 1e-08
ADAM_WD = 0.01
ADAM_STEP = 10

Z_PERM = (2, 3, 4, 5, 6, 7, 0, 1)
ZB_SCONV, ZB_GATE, ZB_POOL, ZB_LRU = 0, 1, 6, 7
HALO = 16
VMEM_LIMIT = 56 * 1024 * 1024

WEIGHTS = ['w_in', 'b_in', 'pool_w', 'pool_scale', 'lru_conv_w', 'lru_conv_b', 'lru_w_r', 'lru_b_r', 'lru_w_i',
           'lru_b_i', 'lru_lambda', 'lru_w_out', 'sconv_w', 'sconv_w_out', 'w_mix_out', 'xa_w_q', 'xa_w_k', 'xa_w_v',
           'xa_w_o', 'ffn_w_gate', 'ffn_w_up', 'ffn_w_down', 'ln_g', 'ln_b']
BIG_SHARDED = {'w_in': 2, 'pool_w': 2, 'lru_w_out': 1, 'sconv_w_out': 1, 'w_mix_out': 1, 'xa_w_q': 1, 'xa_w_k': 1,
               'xa_w_v': 1, 'xa_w_o': 1, 'ffn_w_gate': 2, 'ffn_w_up': 2, 'ffn_w_down': 1}
SMALL_SHARDED = {'lru_conv_w': 2, 'sconv_w': 2, 'ln_g': 2, 'ln_b': 2}
REPLICATED = ['b_in', 'pool_scale', 'lru_conv_b', 'lru_w_r', 'lru_b_r', 'lru_w_i', 'lru_b_i', 'lru_lambda']
SMALL_ALL = ['b_in', 'pool_scale', 'lru_conv_w', 'lru_conv_b', 'lru_w_r', 'lru_b_r', 'lru_w_i', 'lru_b_i',
             'lru_lambda', 'sconv_w', 'ln_g', 'ln_b']


def _cp(*sem):
    return pltpu.CompilerParams(dimension_semantics=sem, vmem_limit_bytes=VMEM_LIMIT)


def _sigmoid(x):
    return 0.5 * jnp.tanh(0.5 * x) + 0.5


class Hosted:
    def __init__(self, arrays, out_shape, sems, start, finish):
        self.arrays, self.out_shape, self.sems, self.start, self.finish = arrays, out_shape, sems, start, finish


def _with_host(host, n_in, n_out, refs):
    if host is None:
        return refs[:n_in], (), refs[n_in:n_in + n_out], (), refs[n_in + n_out:], ()
    hi, ho, hs = len(host.arrays), len(host.out_shape), len(host.sems)
    ins, h_in = refs[:n_in], refs[n_in:n_in + hi]
    outs = refs[n_in + hi:n_in + hi + n_out]
    h_out = refs[n_in + hi + n_out:n_in + hi + n_out + ho]
    rest = refs[n_in + hi + n_out + ho:]
    return ins, h_in, outs, h_out, rest[:len(rest) - hs], rest[len(rest) - hs:]


HBM_SPEC = pl.BlockSpec(memory_space=pl.ANY)


def mm_nn(a, w, bias, *, out_dtype, tm, tn, name, host=None):
    T, K = a.shape
    N = w.shape[1]
    tm, tn = min(tm, T), min(tn, N)
    nj, ni = N // tn, T // tm
    n_in = 2 if bias is None else 3

    def body(*refs):
        ins, h_in, (o_ref,), h_out, _, h_sems = _with_host(host, n_in, 1, refs)
        a_ref, w_ref = ins[:2]
        j, i = pl.program_id(0), pl.program_id(1)
        if host is not None:
            @pl.when((j == 0) & (i == 0))
            def _():
                host.start(h_in, h_out, h_sems)
        acc = jnp.dot(a_ref[...].astype(BF), w_ref[...], preferred_element_type=F32)
        if bias is not None:
            acc = acc + ins[2][...]
        o_ref[...] = acc.astype(o_ref.dtype)
        if host is not None:
            @pl.when((j == nj - 1) & (i == ni - 1))
            def _():
                host.finish(h_in, h_out, h_sems)

    in_specs = [pl.BlockSpec((tm, K), lambda j, i: (i, 0)), pl.BlockSpec((K, tn), lambda j, i: (0, j))]
    args = [a, w]
    if bias is not None:
        in_specs.append(pl.BlockSpec((1, tn), lambda j, i: (0, j)))
        args.append(bias)
    out_specs = [pl.BlockSpec((tm, tn), lambda j, i: (i, j))]
    out_shape = [jax.ShapeDtypeStruct((T, N), out_dtype)]
    scratch = []
    if host is not None:
        in_specs += [HBM_SPEC] * len(host.arrays)
        args += list(host.arrays)
        out_specs += [HBM_SPEC] * len(host.out_shape)
        out_shape += list(host.out_shape)
        scratch = list(host.sems)
    res = pl.pallas_call(
        body, name=name, grid=(nj, ni), in_specs=in_specs, out_specs=out_specs, out_shape=out_shape,
        scratch_shapes=scratch, compiler_params=_cp("arbitrary", "arbitrary"))(*args)
    return res[0] if host is None else (res[0], res[1:])


def mm_nt(a, w, res, *, out_dtype, tm, tc, name, host=None):
    T, C = a.shape
    K = w.shape[0]
    tm, tc = min(tm, T), min(tc, C)
    nc = C // tc
    ni = T // tm
    n_in = 2 if res is None else 3

    def body(*refs):
        ins, h_in, (o_ref,), h_out, (acc_ref,), h_sems = _with_host(host, n_in, 1, refs)
        a_ref, w_ref = ins[:2]
        r_ref = ins[2] if res is not None else None
        c = pl.program_id(1)
        if host is not None:
            @pl.when((pl.program_id(0) == 0) & (c == 0))
            def _():
                host.start(h_in, h_out, h_sems)

            @pl.when((pl.program_id(0) == ni - 1) & (c == nc - 1))
            def _():
                host.finish(h_in, h_out, h_sems)
        part = lax.dot_general(a_ref[...].astype(BF), w_ref[...], (((1,), (1,)), ((), ())),
                               preferred_element_type=F32)

        @pl.when(c == 0)
        def _():
            acc_ref[...] = part

        @pl.when(c > 0)
        def _():
            acc_ref[...] += part

        @pl.when(c == nc - 1)
        def _():
            out = acc_ref[...]
            if res is not None:
                out = out + ALPHA * r_ref[...]
            o_ref[...] = out.astype(o_ref.dtype)

    in_specs = [pl.BlockSpec((tm, tc), lambda i, c: (i, c)), pl.BlockSpec((K, tc), lambda i, c: (0, c))]
    args = [a, w]
    if res is not None:
        in_specs.append(pl.BlockSpec((tm, K), lambda i, c: (i, 0)))
        args.append(res)
    out_specs = [pl.BlockSpec((tm, K), lambda i, c: (i, 0))]
    out_shape = [jax.ShapeDtypeStruct((T, K), out_dtype)]
    scratch = [pltpu.VMEM((tm, K), F32)]
    if host is not None:
        in_specs += [HBM_SPEC] * len(host.arrays)
        args += list(host.arrays)
        out_specs += [HBM_SPEC] * len(host.out_shape)
        out_shape += list(host.out_shape)
        scratch += list(host.sems)
    out = pl.pallas_call(
        body, name=name, grid=(ni, nc), in_specs=in_specs, out_specs=out_specs, out_shape=out_shape,
        scratch_shapes=scratch, compiler_params=_cp("arbitrary", "arbitrary"))(*args)
    return out[0] if host is None else (out[0], out[1:])


def mm_tn(a, b, *, out_dtype, tk, tn, tt, name, colsum=False):
    T, K = a.shape
    N = b.shape[1]
    tk, tn, tt = min(tk, K), min(tn, N), min(tt, T)
    nt = T // tt

    def body(*refs):
        if colsum:
            a_ref, b_ref, o_ref, cs_ref, acc_ref = refs
        else:
            a_ref, b_ref, o_ref, acc_ref = refs
        i, t = pl.program_id(1), pl.program_id(2)
        bb = b_ref[...]
        part = lax.dot_general(a_ref[...].astype(BF), bb.astype(BF), (((0,), (0,)), ((), ())),
                               preferred_element_type=F32)

        @pl.when(t == 0)
        def _():
            acc_ref[...] = part

        @pl.when(t > 0)
        def _():
            acc_ref[...] += part

        @pl.when(t == nt - 1)
        def _():
            o_ref[...] = acc_ref[...].astype(o_ref.dtype)

        if colsum:
            s = jnp.sum(bb.astype(F32), axis=0, keepdims=True)

            @pl.when((i == 0) & (t == 0))
            def _():
                cs_ref[...] = s

            @pl.when((i == 0) & (t > 0))
            def _():
                cs_ref[...] += s

    out_specs = [pl.BlockSpec((tk, tn), lambda j, i, t: (i, j))]
    out_shape = [jax.ShapeDtypeStruct((K, N), out_dtype)]
    if colsum:
        out_specs.append(pl.BlockSpec((1, tn), lambda j, i, t: (0, j)))
        out_shape.append(jax.ShapeDtypeStruct((1, N), F32))
    res = pl.pallas_call(
        body, name=name, grid=(N // tn, K // tk, nt),
        in_specs=[pl.BlockSpec((tt, tk), lambda j, i, t: (t, i)), pl.BlockSpec((tt, tn), lambda j, i, t: (t, j))],
        out_specs=out_specs, out_shape=out_shape, scratch_shapes=[pltpu.VMEM((tk, tn), F32)],
        compiler_params=_cp("arbitrary", "arbitrary", "arbitrary"))(a, b)
    return res if colsum else res[0]


def mm_res_ln(a, w, res, g, b, *, tm, name):
    T, K = a.shape
    tm = min(tm, T)

    def body(a_ref, w_ref, r_ref, g_ref, b_ref, y_ref, yb_ref, xh_ref, rs_ref):
        pre = ALPHA * r_ref[...] + jnp.dot(a_ref[...].astype(BF), w_ref[...], preferred_element_type=F32)
        mu = jnp.mean(pre, axis=-1, keepdims=True)
        cen = pre - mu
        var = jnp.mean(cen * cen, axis=-1, keepdims=True)
        rstd = lax.rsqrt(var + LN_EPS)
        xhat = cen * rstd
        y = xhat * g_ref[...] + b_ref[...]
        y_ref[...] = y
        yb_ref[...] = y.astype(BF)
        xh_ref[...] = xhat
        rs_ref[...] = rstd

    row = lambda i: (i, 0)
    fix = lambda i: (0, 0)
    return pl.pallas_call(
        body, name=name, grid=(T // tm,),
        in_specs=[pl.BlockSpec((tm, K), row), pl.BlockSpec((K, D), fix), pl.BlockSpec((tm, D), row),
                  pl.BlockSpec((1, D), fix), pl.BlockSpec((1, D), fix)],
        out_specs=[pl.BlockSpec((tm, D), row), pl.BlockSpec((tm, D), row), pl.BlockSpec((tm, D), row),
                   pl.BlockSpec((tm, 1), row)],
        out_shape=[jax.ShapeDtypeStruct((T, D), F32), jax.ShapeDtypeStruct((T, D), BF),
                   jax.ShapeDtypeStruct((T, D), F32), jax.ShapeDtypeStruct((T, 1), F32)],
        compiler_params=_cp("arbitrary"))(a, w, res, g, b)


def ln_bwd(dy, xhat, rstd, g, *, tm, name, loss_from=None):
    T = xhat.shape[0]
    tm = min(tm, T)
    with_loss = loss_from is not None

    def body(*refs):
        if with_loss:
            xh_ref, rs_ref, g_ref, b_ref, t_ref, dp_ref, dpb_ref, dg_ref, db_ref, ls_ref = refs
        else:
            dy_ref, xh_ref, rs_ref, g_ref, dp_ref, dpb_ref, dg_ref, db_ref = refs
        i = pl.program_id(0)
        xhat_ = xh_ref[...]
        gg = g_ref[...]
        if with_loss:
            err = xhat_ * gg + b_ref[...] - t_ref[...]
            dyv = err * (1.0 / D)
            lpart = 0.5 * jnp.sum(jnp.sum(err * err, axis=-1, keepdims=True) * (1.0 / D))
        else:
            dyv = dy_ref[...]
        dxh = dyv * gg
        m1 = jnp.mean(dxh, axis=-1, keepdims=True)
        m2 = jnp.mean(dxh * xhat_, axis=-1, keepdims=True)
        dpre = rs_ref[...] * (dxh - m1 - xhat_ * m2)
        dp_ref[...] = dpre
        dpb_ref[...] = dpre.astype(BF)
        dgp = jnp.sum(dyv * xhat_, axis=0, keepdims=True)
        dbp = jnp.sum(dyv, axis=0, keepdims=True)

        @pl.when(i == 0)
        def _():
            dg_ref[...] = dgp
            db_ref[...] = dbp
            if with_loss:
                ls_ref[...] = jnp.full((8, 128), lpart, F32)

        @pl.when(i > 0)
        def _():
            dg_ref[...] += dgp
            db_ref[...] += dbp
            if with_loss:
                ls_ref[...] += jnp.full((8, 128), lpart, F32)

    row = lambda i: (i, 0)
    fix = lambda i: (0, 0)
    if with_loss:
        in_specs = [pl.BlockSpec((tm, D), row), pl.BlockSpec((tm, 1), row), pl.BlockSpec((1, D), fix),
                    pl.BlockSpec((1, D), fix), pl.BlockSpec((tm, D), row)]
        args = [xhat, rstd, g, loss_from[0], loss_from[1]]
    else:
        in_specs = [pl.BlockSpec((tm, D), row), pl.BlockSpec((tm, D), row), pl.BlockSpec((tm, 1), row),
                    pl.BlockSpec((1, D), fix)]
        args = [dy, xhat, rstd, g]
    out_specs = [pl.BlockSpec((tm, D), row), pl.BlockSpec((tm, D), row), pl.BlockSpec((1, D), fix),
                 pl.BlockSpec((1, D), fix)]
    out_shape = [jax.ShapeDtypeStruct((T, D), F32), jax.ShapeDtypeStruct((T, D), BF),
                 jax.ShapeDtypeStruct((1, D), F32), jax.ShapeDtypeStruct((1, D), F32)]
    if with_loss:
        out_specs.append(pl.BlockSpec((8, 128), fix))
        out_shape.append(jax.ShapeDtypeStruct((8, 128), F32))
    return pl.pallas_call(body, name=name, grid=(T // tm,), in_specs=in_specs, out_specs=out_specs,
                          out_shape=out_shape, compiler_params=_cp("arbitrary"))(*args)


def _prev_halo(tm, blk):
    return lambda i: (jnp.maximum(i * (tm // HALO) - 1, 0), blk)


def _next_halo(tm, T, blk):
    return lambda i: (jnp.minimum((i + 1) * (tm // HALO), T // HALO - 1), blk)


def _pool_p(ext, t, g):
    e = ext[:, g * POOL_GD:(g + 1) * POOL_GD]
    s = e
    for sh in (1, 2, 4, 8)[:g + 1]:
        s = s + pltpu.roll(s, sh, axis=0)
    inv = 1.0 / jnp.minimum(t + 1, POOL_WINDOWS[g]).astype(F32)
    return s[HALO:] * inv - e[HALO:]


def pool_fwd(z, pw, *, tm, name):
    T = z.shape[0]
    tm = min(tm, T)

    def body(zm_ref, zh_ref, pw_ref, o_ref):
        i = pl.program_id(0)
        keep = jnp.where(i == 0, 0.0, 1.0).astype(F32)
        ext = jnp.concatenate([zh_ref[...].astype(F32) * keep, zm_ref[...].astype(F32)], axis=0)
        t = i * tm + lax.broadcasted_iota(jnp.int32, (tm, 1), 0)
        outs = [jnp.dot(_pool_p(ext, t, g).astype(BF), pw_ref[g], preferred_element_type=F32) for g in range(4)]
        o_ref[...] = jnp.concatenate(outs, axis=1).astype(o_ref.dtype)

    return pl.pallas_call(
        body, name=name, grid=(T // tm,),
        in_specs=[pl.BlockSpec((tm, D), lambda i: (i, ZB_POOL)), pl.BlockSpec((HALO, D), _prev_halo(tm, ZB_POOL)),
                  pl.BlockSpec((4, POOL_GD, POOL_GD), lambda i: (0, 0, 0))],
        out_specs=pl.BlockSpec((tm, D), lambda i: (i, 0)),
        out_shape=jax.ShapeDtypeStruct((T, D), BF), compiler_params=_cp("arbitrary"))(z, z, pw)


def pool_bwd(dz, dyp, yp_pre, z, pw, ps, *, tm, name):
    T = z.shape[0]
    tm = min(tm, T)
    nt = T // tm

    def body(dz_in, dy_ref, dyn_ref, yp_ref, zm_ref, zh_ref, pw_ref, ps_ref, dz_ref, dpw_ref, dps_ref):
        del dz_in
        i = pl.program_id(0)
        keep_p = jnp.where(i == 0, 0.0, 1.0).astype(F32)
        keep_n = jnp.where(i == nt - 1, 0.0, 1.0).astype(F32)
        ext = jnp.concatenate([zh_ref[...].astype(F32) * keep_p, zm_ref[...].astype(F32)], axis=0)
        t = i * tm + lax.broadcasted_iota(jnp.int32, (tm, 1), 0)
        psv = ps_ref[...]
        dy = dy_ref[...].astype(F32)
        dyp_ext = jnp.concatenate([dy, dyn_ref[...].astype(F32) * keep_n], axis=0) * psv
        t_ext = i * tm + lax.broadcasted_iota(jnp.int32, (tm + HALO, 1), 0)
        dps = jnp.sum(dy * yp_ref[...].astype(F32), axis=0, keepdims=True)
        dzs, dpws = [], []
        for g in range(4):
            sl = slice(g * POOL_GD, (g + 1) * POOL_GD)
            dyg = dyp_ext[:, sl].astype(BF)
            dp = lax.dot_general(dyg, pw_ref[g], (((1,), (1,)), ((), ())), preferred_element_type=F32)
            q = dp * (1.0 / jnp.minimum(t_ext + 1, POOL_WINDOWS[g]).astype(F32))
            s = q
            for sh in (1, 2, 4, 8)[:g + 1]:
                s = s + pltpu.roll(s, tm + HALO - sh, axis=0)
            dzs.append(s[:tm] - dp[:tm])
            p = _pool_p(ext, t, g).astype(BF)
            dpws.append(lax.dot_general(p, dyg[:tm], (((0,), (0,)), ((), ())), preferred_element_type=F32))
        dz_ref[...] = jnp.concatenate(dzs, axis=1).astype(dz_ref.dtype)

        @pl.when(i == 0)
        def _():
            for g in range(4):
                dpw_ref[g] = dpws[g]
            dps_ref[...] = dps

        @pl.when(i > 0)
        def _():
            for g in range(4):
                dpw_ref[g] += dpws[g]
            dps_ref[...] += dps

    row = lambda i: (i, 0)
    return pl.pallas_call(
        body, name=name, grid=(nt,),
        in_specs=[pl.BlockSpec(memory_space=pl.ANY),
                  pl.BlockSpec((tm, D), row), pl.BlockSpec((HALO, D), _next_halo(tm, T, 0)),
                  pl.BlockSpec((tm, D), row),
                  pl.BlockSpec((tm, D), lambda i: (i, ZB_POOL)), pl.BlockSpec((HALO, D), _prev_halo(tm, ZB_POOL)),
                  pl.BlockSpec((4, POOL_GD, POOL_GD), lambda i: (0, 0, 0)), pl.BlockSpec((1, D), lambda i: (0, 0))],
        out_specs=[pl.BlockSpec((tm, D), lambda i: (i, ZB_POOL)),
                   pl.BlockSpec((4, POOL_GD, POOL_GD), lambda i: (0, 0, 0)), pl.BlockSpec((1, D), lambda i: (0, 0))],
        out_shape=[jax.ShapeDtypeStruct(dz.shape, dz.dtype), jax.ShapeDtypeStruct((4, POOL_GD, POOL_GD), F32),
                   jax.ShapeDtypeStruct((1, D), F32)],
        input_output_aliases={0: 0}, compiler_params=_cp("arbitrary"))(dz, dyp, dyp, yp_pre, z, z, pw, ps)


def _fill_ext(ext_s, halo, main, keep):
    ext_s[0:HALO, :] = halo * keep
    ext_s[HALO:, :] = main


def _lru_gates(ext_s, tm, cw, cb, wr_ref, br, wi_ref, bi, lam):
    shifted = []
    v = cb
    for k in range(4):
        zs = ext_s[pl.ds(HALO - 3 + k, tm), :]
        shifted.append(zs)
        v = v + cw[k:k + 1, :] * zs
    vb = v.astype(BF)
    rp, ip = [], []
    for h in range(LRU_HEADS):
        sl = slice(h * LRU_HD, (h + 1) * LRU_HD)
        rp.append(jnp.dot(vb[:, sl], wr_ref[h], preferred_element_type=F32))
        ip.append(jnp.dot(vb[:, sl], wi_ref[h], preferred_element_type=F32))
    r = _sigmoid(jnp.concatenate(rp, axis=1) + br)
    ig = _sigmoid(jnp.concatenate(ip, axis=1) + bi)
    sp = jnp.maximum(-lam, 0.0) + jnp.log(1.0 + jnp.exp(-jnp.abs(lam)))
    a = jnp.exp(-LRU_C * r * sp)
    om = 1.0 - a * a
    rs = lax.rsqrt(om)
    return v, vb, r, ig, a, om, rs, sp, shifted


def lru_fwd(z, cw, cb, wr, br, wi, bi, lam, wlo, *, tm, name):
    T = z.shape[0]
    tm = min(tm, T)
    nch = tm // 8

    def body(zm_ref, zh_ref, cw_ref, cb_ref, wr_ref, br_ref, wi_ref, bi_ref, lam_ref, wlo_ref, h_ref, y_ref,
             a_s, b_s, carry, ext_s):
        i = pl.program_id(0)

        @pl.when(i == 0)
        def _():
            carry[...] = jnp.zeros_like(carry)

        keep = jnp.where(i == 0, 0.0, 1.0).astype(F32)
        _fill_ext(ext_s, zh_ref[...].astype(F32), zm_ref[...].astype(F32), keep)
        v, _, _, ig, a, om, rs, _, _ = _lru_gates(ext_s, tm, cw_ref[...], cb_ref[...], wr_ref, br_ref[...], wi_ref,
                                                  bi_ref[...], lam_ref[...])
        a_s[...] = a
        b_s[...] = jnp.where(om > 0.0, om * rs, 0.0) * (ig * v)
        row = lax.broadcasted_iota(jnp.int32, (8, D), 0)

        def step(ci, hprev):
            sl = pl.ds(pl.multiple_of(ci * 8, 8), 8)
            aa, bb = a_s[sl, :], b_s[sl, :]
            for s in (1, 2, 4):
                m = row >= s
                bb = bb + aa * jnp.where(m, pltpu.roll(bb, s, axis=0), 0.0)
                aa = aa * jnp.where(m, pltpu.roll(aa, s, axis=0), 1.0)
            h = bb + aa * hprev
            h_ref[sl, :] = h
            return jnp.broadcast_to(h[7:8, :], (8, D))

        carry[...] = lax.fori_loop(0, nch, step, carry[...])
        y_ref[...] = jnp.dot(h_ref[...].astype(BF), wlo_ref[...], preferred_element_type=F32).astype(BF)

    fix2 = lambda i: (0, 0)
    fix3 = lambda i: (0, 0, 0)
    return pl.pallas_call(
        body, name=name, grid=(T // tm,),
        in_specs=[pl.BlockSpec((tm, D), lambda i: (i, ZB_LRU)), pl.BlockSpec((HALO, D), _prev_halo(tm, ZB_LRU)),
                  pl.BlockSpec((4, D), fix2), pl.BlockSpec((1, D), fix2),
                  pl.BlockSpec((LRU_HEADS, LRU_HD, LRU_HD), fix3), pl.BlockSpec((1, D), fix2),
                  pl.BlockSpec((LRU_HEADS, LRU_HD, LRU_HD), fix3), pl.BlockSpec((1, D), fix2),
                  pl.BlockSpec((1, D), fix2), pl.BlockSpec((D, D), fix2)],
        out_specs=[pl.BlockSpec((tm, D), lambda i: (i, 0)), pl.BlockSpec((tm, D), lambda i: (i, 0))],
        out_shape=[jax.ShapeDtypeStruct((T, D), F32), jax.ShapeDtypeStruct((T, D), BF)],
        scratch_shapes=[pltpu.VMEM((tm, D), F32), pltpu.VMEM((tm, D), F32), pltpu.VMEM((8, D), F32),
                        pltpu.VMEM((tm + HALO, D), F32)],
        compiler_params=_cp("arbitrary"))(z, z, cw, cb, wr, br, wi, bi, lam, wlo)


def lru_bwd(dz, dyl, z, h, cw, cb, wr, br, wi, bi, lam, wlo, *, tm, name):
    T = z.shape[0]
    tm = min(tm, T)
    nt = T // tm
    nch = tm // 8

    def body(dz_in, dy_ref, zm_ref, zh_ref, h_ref, hh_ref, cw_ref, cb_ref, wr_ref, br_ref, wi_ref, bi_ref, lam_ref,
             wlo_ref, dz_ref, dcw_ref, dcb_ref, dwr_ref, dbr_ref, dwi_ref, dbi_ref, dlam_ref,
             c_s, g_s, dh_s, dh_carry, a_ext, dv_ext, ext_s, h_ext):
        del dz_in
        i = pl.program_id(0)
        ti = nt - 1 - i

        @pl.when(i == 0)
        def _():
            dh_carry[...] = jnp.zeros_like(dh_carry)
            a_ext[tm:, :] = jnp.zeros((8, D), F32)
            dv_ext[tm:, :] = jnp.zeros((HALO, D), F32)

        keep = jnp.where(ti == 0, 0.0, 1.0).astype(F32)
        _fill_ext(ext_s, zh_ref[...].astype(F32), zm_ref[...].astype(F32), keep)
        cw_ = cw_ref[...]
        lam_ = lam_ref[...]
        v, vb, r, ig, a, om, rs, sp, shifted = _lru_gates(ext_s, tm, cw_, cb_ref[...], wr_ref, br_ref[...], wi_ref,
                                                          bi_ref[...], lam_)
        mult = jnp.where(om > 0.0, om * rs, 0.0)
        a_ext[0:tm, :] = a
        c_s[...] = a_ext[pl.ds(1, tm), :]
        g_s[...] = lax.dot_general(dy_ref[...], wlo_ref[...], (((1,), (1,)), ((), ())), preferred_element_type=F32)
        row = lax.broadcasted_iota(jnp.int32, (8, D), 0)

        def step(k, nxt):
            ci = nch - 1 - k
            sl = pl.ds(pl.multiple_of(ci * 8, 8), 8)
            cc, gg = c_s[sl, :], g_s[sl, :]
            for s in (1, 2, 4):
                m = row < 8 - s
                gg = gg + cc * jnp.where(m, pltpu.roll(gg, 8 - s, axis=0), 0.0)
                cc = cc * jnp.where(m, pltpu.roll(cc, 8 - s, axis=0), 1.0)
            dh = gg + cc * nxt
            dh_s[sl, :] = dh
            return jnp.broadcast_to(dh[0:1, :], (8, D))

        dh_carry[...] = lax.fori_loop(0, nch, step, dh_carry[...])
        a_ext[tm:, :] = a[0:8, :]
        dh = dh_s[...]
        h_ext[0:8, :] = hh_ref[...] * keep
        h_ext[8:, :] = h_ref[...]
        hprev = h_ext[pl.ds(7, tm), :]
        iv = ig * v
        da = dh * hprev
        dmult = dh * iv
        div = dh * mult
        dlog = da * a - dmult * (a * a) * rs
        dr = dlog * (-LRU_C * sp)
        dlam = jnp.sum(dlog * r, axis=0, keepdims=True) * (LRU_C * _sigmoid(-lam_))
        di = div * v
        dv = div * ig
        drp = dr * r * (1.0 - r)
        dip = di * ig * (1.0 - ig)
        drb, dib = drp.astype(BF), dip.astype(BF)
        dvh, dwr, dwi = [], [], []
        nt_dims = (((1,), (1,)), ((), ()))
        tn_dims = (((0,), (0,)), ((), ()))
        for hd in range(LRU_HEADS):
            sl = slice(hd * LRU_HD, (hd + 1) * LRU_HD)
            dvh.append(lax.dot_general(drb[:, sl], wr_ref[hd], nt_dims, preferred_element_type=F32)
                       + lax.dot_general(dib[:, sl], wi_ref[hd], nt_dims, preferred_element_type=F32))
            dwr.append(lax.dot_general(vb[:, sl], drb[:, sl], tn_dims, preferred_element_type=F32))
            dwi.append(lax.dot_general(vb[:, sl], dib[:, sl], tn_dims, preferred_element_type=F32))
        dv = dv + jnp.concatenate(dvh, axis=1)
        dv_ext[0:tm, :] = dv
        dzl = cw_[3:4, :] * dv
        for k in range(3):
            dzl = dzl + cw_[k:k + 1, :] * dv_ext[pl.ds(3 - k, tm), :]
        dz_ref[...] = dzl.astype(dz_ref.dtype)
        dv_ext[tm:, :] = dv[:HALO]
        dcw = jnp.concatenate([jnp.sum(dv * shifted[k], axis=0, keepdims=True) for k in range(4)], axis=0)
        dcb = jnp.sum(dv, axis=0, keepdims=True)
        dbr = jnp.sum(drp, axis=0, keepdims=True)
        dbi = jnp.sum(dip, axis=0, keepdims=True)

        @pl.when(i == 0)
        def _():
            dcw_ref[...] = dcw
            dcb_ref[...] = dcb
            dbr_ref[...] = dbr
            dbi_ref[...] = dbi
            dlam_ref[...] = dlam
            for hd in range(LRU_HEADS):
                dwr_ref[hd] = dwr[hd]
                dwi_ref[hd] = dwi[hd]

        @pl.when(i > 0)
        def _():
            dcw_ref[...] += dcw
            dcb_ref[...] += dcb
            dbr_ref[...] += dbr
            dbi_ref[...] += dbi
            dlam_ref[...] += dlam
            for hd in range(LRU_HEADS):
                dwr_ref[hd] += dwr[hd]
                dwi_ref[hd] += dwi[hd]

    fix2 = lambda i: (0, 0)
    fix3 = lambda i: (0, 0, 0)
    rev = lambda i: (nt - 1 - i, 0)
    vec = pl.BlockSpec((1, D), fix2)
    hw = pl.BlockSpec((LRU_HEADS, LRU_HD, LRU_HD), fix3)
    return pl.pallas_call(
        body, name=name, grid=(nt,),
        in_specs=[pl.BlockSpec(memory_space=pl.ANY),
                  pl.BlockSpec((tm, D), rev),
                  pl.BlockSpec((tm, D), lambda i: (nt - 1 - i, ZB_LRU)),
                  pl.BlockSpec((HALO, D), lambda i: (jnp.maximum((nt - 1 - i) * (tm // HALO) - 1, 0), ZB_LRU)),
                  pl.BlockSpec((tm, D), rev),
                  pl.BlockSpec((8, D), lambda i: (jnp.maximum((nt - 1 - i) * (tm // 8) - 1, 0), 0)),
                  pl.BlockSpec((4, D), fix2), vec, hw, vec, hw, vec, vec, pl.BlockSpec((D, D), fix2)],
        out_specs=[pl.BlockSpec((tm, D), lambda i: (nt - 1 - i, ZB_LRU)),
                   pl.BlockSpec((4, D), fix2), vec, hw, vec, hw, vec, vec],
        out_shape=[jax.ShapeDtypeStruct(dz.shape, dz.dtype), jax.ShapeDtypeStruct((4, D), F32),
                   jax.ShapeDtypeStruct((1, D), F32), jax.ShapeDtypeStruct((LRU_HEADS, LRU_HD, LRU_HD), F32),
                   jax.ShapeDtypeStruct((1, D), F32), jax.ShapeDtypeStruct((LRU_HEADS, LRU_HD, LRU_HD), F32),
                   jax.ShapeDtypeStruct((1, D), F32), jax.ShapeDtypeStruct((1, D), F32)],
        scratch_shapes=[pltpu.VMEM((tm, D), F32), pltpu.VMEM((tm, D), F32), pltpu.VMEM((tm, D), F32),
                        pltpu.VMEM((8, D), F32), pltpu.VMEM((tm + 8, D), F32), pltpu.VMEM((tm + HALO, D), F32),
                        pltpu.VMEM((tm + HALO, D), F32), pltpu.VMEM((tm + 8, D), F32)],
        input_output_aliases={0: 0},
        compiler_params=_cp("arbitrary"))(dz, dyl, z, z, h, h, cw, cb, wr, br, wi, bi, lam, wlo)


def _sconv_cv(u_ext, sw):
    shifted = []
    cv = None
    for k in range(3):
        us = (u_ext if k == 2 else pltpu.roll(u_ext, 2 - k, axis=0))[HALO:]
        shifted.append(us)
        term = sw[k:k + 1, :] * us
        cv = term if cv is None else cv + term
    return cv, shifted


def sconv_fwd(z, sw, wso, *, tm, name):
    T = z.shape[0]
    tm = min(tm, T)

    def body(zm_ref, zh_ref, sw_ref, wso_ref, s_ref, y_ref):
        i = pl.program_id(0)
        keep = jnp.where(i == 0, 0.0, 1.0).astype(F32)
        zm = zm_ref[...].astype(F32)
        zh = zh_ref[...].astype(F32)
        u_ext = jnp.concatenate([zh[:, D:2 * D] * zh[:, 2 * D:] * keep, zm[:, D:2 * D] * zm[:, 2 * D:]], axis=0)
        cv, _ = _sconv_cv(u_ext, sw_ref[...])
        s = (zm[:, :D] * cv).astype(BF)
        s_ref[...] = s
        y_ref[...] = jnp.dot(s, wso_ref[...], preferred_element_type=F32).astype(BF)

    return pl.pallas_call(
        body, name=name, grid=(T // tm,),
        in_specs=[pl.BlockSpec((tm, 3 * D), lambda i: (i, ZB_SCONV)),
                  pl.BlockSpec((HALO, 3 * D), _prev_halo(tm, ZB_SCONV)),
                  pl.BlockSpec((3, D), lambda i: (0, 0)), pl.BlockSpec((D, D), lambda i: (0, 0))],
        out_specs=[pl.BlockSpec((tm, D), lambda i: (i, 0)), pl.BlockSpec((tm, D), lambda i: (i, 0))],
        out_shape=[jax.ShapeDtypeStruct((T, D), BF), jax.ShapeDtypeStruct((T, D), BF)],
        compiler_params=_cp("arbitrary"))(z, z, sw, wso)


def sconv_bwd(dz, dyc, z, sw, wso, *, tm, name):
    T = z.shape[0]
    tm = min(tm, T)
    nt = T // tm

    def body(dz_in, dy_ref, dyn_ref, zm_ref, zp_ref, zn_ref, sw_ref, wso_ref, dz_ref, dsw_ref):
        del dz_in
        i = pl.program_id(0)
        keep_p = jnp.where(i == 0, 0.0, 1.0).astype(F32)
        keep_n = jnp.where(i == nt - 1, 0.0, 1.0).astype(F32)
        sw_ = sw_ref[...]
        zm = zm_ref[...].astype(F32)
        zp = zp_ref[...].astype(F32)
        zb, zc, zh = zm[:, :D], zm[:, D:2 * D], zm[:, 2 * D:]
        u_ext = jnp.concatenate([zp[:, D:2 * D] * zp[:, 2 * D:] * keep_p, zc * zh], axis=0)
        cv, shifted = _sconv_cv(u_ext, sw_)
        dy_ext = jnp.concatenate([dy_ref[...], dyn_ref[...]], axis=0)
        ds_ext = lax.dot_general(dy_ext, wso_ref[...], (((1,), (1,)), ((), ())), preferred_element_type=F32)
        zb_ext = jnp.concatenate([zb, zn_ref[...][:, :D].astype(F32) * keep_n], axis=0)
        dcv_ext = ds_ext * zb_ext
        du = sw_[2:3, :] * dcv_ext[:tm]
        for k in range(2):
            du = du + sw_[k:k + 1, :] * pltpu.roll(dcv_ext, tm + HALO - (2 - k), axis=0)[:tm]
        dz_ref[...] = jnp.concatenate([ds_ext[:tm] * cv, du * zh, du * zc], axis=1).astype(dz_ref.dtype)
        dcv = dcv_ext[:tm]
        dsw = jnp.concatenate([jnp.sum(dcv * shifted[k], axis=0, keepdims=True) for k in range(3)], axis=0)

        @pl.when(i == 0)
        def _():
            dsw_ref[...] = dsw

        @pl.when(i > 0)
        def _():
            dsw_ref[...] += dsw

    return pl.pallas_call(
        body, name=name, grid=(nt,),
        in_specs=[pl.BlockSpec(memory_space=pl.ANY),
                  pl.BlockSpec((tm, D), lambda i: (i, 0)), pl.BlockSpec((HALO, D), _next_halo(tm, T, 0)),
                  pl.BlockSpec((tm, 3 * D), lambda i: (i, ZB_SCONV)),
                  pl.BlockSpec((HALO, 3 * D), _prev_halo(tm, ZB_SCONV)),
                  pl.BlockSpec((HALO, 3 * D), _next_halo(tm, T, ZB_SCONV)),
                  pl.BlockSpec((3, D), lambda i: (0, 0)), pl.BlockSpec((D, D), lambda i: (0, 0))],
        out_specs=[pl.BlockSpec((tm, 3 * D), lambda i: (i, ZB_SCONV)), pl.BlockSpec((3, D), lambda i: (0, 0))],
        out_shape=[jax.ShapeDtypeStruct(dz.shape, dz.dtype), jax.ShapeDtypeStruct((3, D), F32)],
        input_output_aliases={0: 0}, compiler_params=_cp("arbitrary"))(dz, dyc, dyc, z, z, z, sw, wso)


def merge_fwd(z, yp_pre, yl, yc, ps, *, tm, name):
    T = z.shape[0]
    tm = min(tm, T)

    def body(zg_ref, yp_ref, yl_ref, yc_ref, ps_ref, o_ref):
        gts = _sigmoid(zg_ref[...].astype(F32))
        m = (gts[:, :D] * (yp_ref[...].astype(F32) * ps_ref[...]) + gts[:, D:2 * D] * yl_ref[...].astype(F32)
             + gts[:, 2 * D:] * yc_ref[...].astype(F32))
        o_ref[...] = m.astype(o_ref.dtype)

    row = lambda i: (i, 0)
    return pl.pallas_call(
        body, name=name, grid=(T // tm,),
        in_specs=[pl.BlockSpec((tm, 3 * D), lambda i: (i, ZB_GATE)), pl.BlockSpec((tm, D), row),
                  pl.BlockSpec((tm, D), row), pl.BlockSpec((tm, D), row), pl.BlockSpec((1, D), lambda i: (0, 0))],
        out_specs=pl.BlockSpec((tm, D), row), out_shape=jax.ShapeDtypeStruct((T, D), BF),
        compiler_params=_cp("arbitrary"))(z, yp_pre, yl, yc, ps)


def merge_bwd(dm, z, yp_pre, yl, yc, ps, *, tm, name):
    T = z.shape[0]
    tm = min(tm, T)

    def body(dm_ref, zg_ref, yp_ref, yl_ref, yc_ref, ps_ref, dz_ref, dyp_ref, dyl_ref, dyc_ref):
        gts = _sigmoid(zg_ref[...].astype(F32))
        dmv = dm_ref[...].astype(F32)
        ys = (yp_ref[...].astype(F32) * ps_ref[...], yl_ref[...].astype(F32), yc_ref[...].astype(F32))
        outs = (dyp_ref, dyl_ref, dyc_ref)
        dgs = []
        for j in range(3):
            gj = gts[:, j * D:(j + 1) * D]
            outs[j][...] = (dmv * gj).astype(BF)
            dgs.append(dmv * ys[j] * gj * (1.0 - gj))
        dz_ref[...] = jnp.concatenate(dgs, axis=1).astype(dz_ref.dtype)

    row = lambda i: (i, 0)
    return pl.pallas_call(
        body, name=name, grid=(T // tm,),
        in_specs=[pl.BlockSpec((tm, D), row), pl.BlockSpec((tm, 3 * D), lambda i: (i, ZB_GATE)),
                  pl.BlockSpec((tm, D), row), pl.BlockSpec((tm, D), row), pl.BlockSpec((tm, D), row),
                  pl.BlockSpec((1, D), lambda i: (0, 0))],
        out_specs=[pl.BlockSpec((tm, 3 * D), lambda i: (i, ZB_GATE)), pl.BlockSpec((tm, D), row),
                   pl.BlockSpec((tm, D), row), pl.BlockSpec((tm, D), row)],
        out_shape=[jax.ShapeDtypeStruct((T, IN_COLS), BF), jax.ShapeDtypeStruct((T, D), BF),
                   jax.ShapeDtypeStruct((T, D), BF), jax.ShapeDtypeStruct((T, D), BF)],
        compiler_params=_cp("arbitrary"))(dm, z, yp_pre, yl, yc, ps)


def _attn_probs(qh, kh):
    s = lax.dot_general(qh, kh, (((1,), (1,)), ((), ())), preferred_element_type=F32) * (X_HD ** -0.5)
    e = jnp.exp(s - jnp.max(s, axis=-1, keepdims=True))
    return e / jnp.sum(e, axis=-1, keepdims=True)


def attn_fwd(xb, wq, kb, vb, *, tm, name):
    T = xb.shape[0]
    tm = min(tm, T)

    def body(x_ref, wq_ref, k_ref, v_ref, q_ref, o_ref):
        q = jnp.dot(x_ref[...], wq_ref[...], preferred_element_type=F32).astype(BF)
        q_ref[...] = q
        outs = []
        for h in range(X_HEADS):
            sl = slice(h * X_HD, (h + 1) * X_HD)
            p = _attn_probs(q[:, sl], k_ref[:, sl])
            outs.append(jnp.dot(p.astype(BF), v_ref[:, sl], preferred_element_type=F32))
        o_ref[...] = jnp.concatenate(outs, axis=1).astype(BF)

    row = lambda i: (i, 0)
    fix = lambda i: (0, 0)
    return pl.pallas_call(
        body, name=name, grid=(T // tm,),
        in_specs=[pl.BlockSpec((tm, D), row), pl.BlockSpec((D, D), fix), pl.BlockSpec((N_MEM, D), fix),
                  pl.BlockSpec((N_MEM, D), fix)],
        out_specs=[pl.BlockSpec((tm, D), row), pl.BlockSpec((tm, D), row)],
        out_shape=[jax.ShapeDtypeStruct((T, D), BF), jax.ShapeDtypeStruct((T, D), BF)],
        compiler_params=_cp("arbitrary"))(xb, wq, kb, vb)


def attn_bwd(dxa, wo, q, kb, vb, *, tm, name):
    T = q.shape[0]
    tm = min(tm, T)

    def body(d_ref, wo_ref, q_ref, k_ref, v_ref, dq_ref, dk_ref, dv_ref):
        i = pl.program_id(0)
        do = lax.dot_general(d_ref[...], wo_ref[...], (((1,), (1,)), ((), ())),
                             preferred_element_type=F32).astype(BF)
        q = q_ref[...]
        dqs, dks, dvs = [], [], []
        for h in range(X_HEADS):
            sl = slice(h * X_HD, (h + 1) * X_HD)
            kh, vh = k_ref[:, sl], v_ref[:, sl]
            p = _attn_probs(q[:, sl], kh)
            dp = lax.dot_general(do[:, sl], vh, (((1,), (1,)), ((), ())), preferred_element_type=F32)
            ds = (p * (dp - jnp.sum(dp * p, axis=-1, keepdims=True)) * (X_HD ** -0.5)).astype(BF)
            dqs.append(jnp.dot(ds, kh, preferred_element_type=F32))
            dks.append(lax.dot_general(ds, q[:, sl], (((0,), (0,)), ((), ())), preferred_element_type=F32))
            dvs.append(lax.dot_general(p.astype(BF), do[:, sl], (((0,), (0,)), ((), ())),
                                       preferred_element_type=F32))
        dq_ref[...] = jnp.concatenate(dqs, axis=1).astype(BF)
        dk = jnp.concatenate(dks, axis=1)
        dv = jnp.concatenate(dvs, axis=1)

        @pl.when(i == 0)
        def _():
            dk_ref[...] = dk
            dv_ref[...] = dv

        @pl.when(i > 0)
        def _():
            dk_ref[...] += dk
            dv_ref[...] += dv

    row = lambda i: (i, 0)
    fix = lambda i: (0, 0)
    return pl.pallas_call(
        body, name=name, grid=(T // tm,),
        in_specs=[pl.BlockSpec((tm, D), row), pl.BlockSpec((D, D), fix), pl.BlockSpec((tm, D), row),
                  pl.BlockSpec((N_MEM, D), fix), pl.BlockSpec((N_MEM, D), fix)],
        out_specs=[pl.BlockSpec((tm, D), row), pl.BlockSpec((N_MEM, D), fix), pl.BlockSpec((N_MEM, D), fix)],
        out_shape=[jax.ShapeDtypeStruct((T, D), BF), jax.ShapeDtypeStruct((N_MEM, D), F32),
                   jax.ShapeDtypeStruct((N_MEM, D), F32)],
        compiler_params=_cp("arbitrary"))(dxa, wo, q, kb, vb)


def swiglu_fwd(gu, *, tm, name):
    T = gu.shape[0]
    tm = min(tm, T)

    def body(g_ref, u_ref, o_ref):
        g = g_ref[...].astype(F32)
        o_ref[...] = (g * _sigmoid(g) * u_ref[...].astype(F32)).astype(BF)

    return pl.pallas_call(
        body, name=name, grid=(T // tm,),
        in_specs=[pl.BlockSpec((tm, D_FF), lambda i: (i, 0)), pl.BlockSpec((tm, D_FF), lambda i: (i, 1))],
        out_specs=pl.BlockSpec((tm, D_FF), lambda i: (i, 0)), out_shape=jax.ShapeDtypeStruct((T, D_FF), BF),
        compiler_params=_cp("arbitrary"))(gu, gu)


def swiglu_bwd(dh, gu, *, tm, name):
    T = gu.shape[0]
    tm = min(tm, T)

    def body(dh_ref, g_ref, u_ref, o_ref):
        g = g_ref[...].astype(F32)
        u = u_ref[...].astype(F32)
        dhv = dh_ref[...].astype(F32)
        sg = _sigmoid(g)
        o_ref[:, :D_FF] = (dhv * u * sg * (1.0 + g * (1.0 - sg))).astype(BF)
        o_ref[:, D_FF:] = (dhv * g * sg).astype(BF)

    return pl.pallas_call(
        body, name=name, grid=(T // tm,),
        in_specs=[pl.BlockSpec((tm, D_FF), lambda i: (i, 0)), pl.BlockSpec((tm, D_FF), lambda i: (i, 0)),
                  pl.BlockSpec((tm, D_FF), lambda i: (i, 1))],
        out_specs=pl.BlockSpec((tm, 2 * D_FF), lambda i: (i, 0)),
        out_shape=jax.ShapeDtypeStruct((T, 2 * D_FF), BF), compiler_params=_cp("arbitrary"))(dh, gu, gu)


TM_MM = 1024
TM_EW = 512
TM_SEQ = 512
TT_DW = 2048


def _layer_fwd(l, x, xb, kb, vb, W, host=None):
    n = f"l{l}_"
    sv = {'x0': x if xb is None else xb}
    z = mm_nn(sv['x0'], W['w_in'][l], W['b_in'][l], out_dtype=BF, tm=TM_MM, tn=1024, name=n + "in_proj", host=host)
    if host is not None:
        z, sv['hosted'] = z
    yp = pool_fwd(z, W['pool_w'][l], tm=TM_SEQ, name=n + "pool_fwd")
    h, yl = lru_fwd(z, W['lru_conv_w'][l], W['lru_conv_b'][l], W['lru_w_r'][l], W['lru_b_r'][l], W['lru_w_i'][l],
                    W['lru_b_i'][l], W['lru_lambda'][l], W['lru_w_out'][l], tm=TM_SEQ, name=n + "lru_fwd")
    s, yc = sconv_fwd(z, W['sconv_w'][l], W['sconv_w_out'][l], tm=TM_SEQ, name=n + "sconv_fwd")
    merged = merge_fwd(z, yp, yl, yc, W['pool_scale'][l], tm=TM_EW, name=n + "merge_fwd")
    x1, x1b, xh1, rs1 = mm_res_ln(merged, W['w_mix_out'][l], x, W['ln_g'][l][0:1], W['ln_b'][l][0:1], tm=TM_EW,
                                  name=n + "mix_out_ln")
    q, o = attn_fwd(x1b, W['xa_w_q'][l], kb, vb, tm=TM_EW, name=n + "attn_fwd")
    x2, x2b, xh2, rs2 = mm_res_ln(o, W['xa_w_o'][l], x1, W['ln_g'][l][1:2], W['ln_b'][l][1:2], tm=TM_EW,
                                  name=n + "attn_out_ln")
    gu = mm_nn(x2b, W['ffn_w_gu'][l], None, out_dtype=BF, tm=TM_MM, tn=1408, name=n + "ffn_in")
    hdn = swiglu_fwd(gu, tm=TM_EW, name=n + "swiglu_fwd")
    x3, x3b, xh3, rs3 = mm_res_ln(hdn, W['ffn_w_down'][l], x2, W['ln_g'][l][2:3], W['ln_b'][l][2:3], tm=TM_EW,
                                  name=n + "ffn_out_ln")
    sv.update(z=z, yp=yp, h=h, yl=yl, s=s, yc=yc, merged=merged, x1b=x1b, xh1=xh1, rs1=rs1, q=q, o=o, x2b=x2b,
              xh2=xh2, rs2=rs2, gu=gu, hdn=hdn, xh3=xh3, rs3=rs3)
    return x3, x3b, sv


def _layer_bwd(l, dx3, sv, memb, kb, vb, W, loss_from=None, host=None):
    n = f"l{l}_"
    G = {}
    res = ln_bwd(dx3, sv['xh3'], sv['rs3'], W['ln_g'][l][2:3], tm=TM_EW, name=n + "ln3_bwd", loss_from=loss_from)
    dp3, dp3b, dg3, db3 = res[:4]
    loss = res[4] if loss_from is not None else None
    dhdn = mm_nt(dp3b, W['ffn_w_down'][l], None, out_dtype=BF, tm=TM_EW, tc=D, name=n + "ffn_down_dx")
    dgu = swiglu_bwd(dhdn, sv['gu'], tm=TM_EW, name=n + "swiglu_bwd")
    dx2 = mm_nt(dgu, W['ffn_w_gu'][l], dp3, out_dtype=F32, tm=TM_MM, tc=1408, name=n + "ffn_in_dx", host=host)
    if host is not None:
        dx2, G['hosted'] = dx2
    G['ffn_w_gu'] = mm_tn(sv['x2b'], dgu, out_dtype=BF, tk=1024, tn=1408, tt=TT_DW,name=n + "ffn_in_dw")
    G['ffn_w_down'] = mm_tn(sv['hdn'], dp3b, out_dtype=BF, tk=1408, tn=1024, tt=TT_DW,name=n + "ffn_down_dw")

    dp2, dp2b, dg2, db2 = ln_bwd(dx2, sv['xh2'], sv['rs2'], W['ln_g'][l][1:2], tm=TM_EW, name=n + "ln2_bwd")
    dq, dk, dv = attn_bwd(dp2b, W['xa_w_o'][l], sv['q'], kb, vb, tm=TM_EW, name=n + "attn_bwd")
    dx1 = mm_nt(dq, W['xa_w_q'][l], dp2, out_dtype=F32, tm=TM_MM, tc=D, name=n + "attn_q_dx")
    G['xa_w_o'] = mm_tn(sv['o'], dp2b, out_dtype=BF, tk=1024, tn=1024, tt=TT_DW,name=n + "attn_o_dw")
    G['xa_w_q'] = mm_tn(sv['x1b'], dq, out_dtype=BF, tk=1024, tn=1024, tt=TT_DW,name=n + "attn_q_dw")
    G['xa_w_k'] = mm_tn(memb, dk, out_dtype=BF, tk=1024, tn=1024, tt=N_MEM, name=n + "attn_k_dw")
    G['xa_w_v'] = mm_tn(memb, dv, out_dtype=BF, tk=1024, tn=1024, tt=N_MEM, name=n + "attn_v_dw")

    dp1, dp1b, dg1, db1 = ln_bwd(dx1, sv['xh1'], sv['rs1'], W['ln_g'][l][0:1], tm=TM_EW, name=n + "ln1_bwd")
    dmerged = mm_nt(dp1b, W['w_mix_out'][l], None, out_dtype=BF, tm=TM_MM, tc=D, name=n + "mix_out_dx")
    G['w_mix_out'] = mm_tn(sv['merged'], dp1b, out_dtype=BF, tk=1024, tn=1024, tt=TT_DW,name=n + "mix_out_dw")
    z = sv['z']
    dz, dyp, dyl, dyc = merge_bwd(dmerged, z, sv['yp'], sv['yl'], sv['yc'], W['pool_scale'][l], tm=TM_EW,
                                  name=n + "merge_bwd")
    dz, G['pool_w'], G['pool_scale'] = pool_bwd(dz, dyp, sv['yp'], z, W['pool_w'][l], W['pool_scale'][l],
                                                tm=TM_SEQ, name=n + "pool_bwd")
    (dz, G['lru_conv_w'], G['lru_conv_b'], G['lru_w_r'], G['lru_b_r'], G['lru_w_i'], G['lru_b_i'],
     G['lru_lambda']) = lru_bwd(dz, dyl, z, sv['h'], W['lru_conv_w'][l], W['lru_conv_b'][l], W['lru_w_r'][l],
                                W['lru_b_r'][l], W['lru_w_i'][l], W['lru_b_i'][l], W['lru_lambda'][l],
                                W['lru_w_out'][l], tm=TM_SEQ, name=n + "lru_bwd")
    dz, G['sconv_w'] = sconv_bwd(dz, dyc, z, W['sconv_w'][l], W['sconv_w_out'][l], tm=TM_SEQ, name=n + "sconv_bwd")
    G['lru_w_out'] = mm_tn(sv['h'], dyl, out_dtype=BF, tk=1024, tn=1024, tt=TT_DW,name=n + "lru_out_dw")
    G['sconv_w_out'] = mm_tn(sv['s'], dyc, out_dtype=BF, tk=1024, tn=1024, tt=TT_DW,name=n + "sconv_out_dw")
    dx0 = mm_nt(dz, W['w_in'][l], dp1, out_dtype=F32, tm=TM_MM, tc=2048, name=n + "in_proj_dx")
    G['w_in'], G['b_in'] = mm_tn(sv['x0'], dz, out_dtype=BF, tk=1024, tn=1024, tt=TT_DW,name=n + "in_proj_dw",
                                 colsum=True)
    G['ln_g'] = jnp.concatenate([dg1, dg2, dg3], axis=0)
    G['ln_b'] = jnp.concatenate([db1, db2, db3], axis=0)
    return dx0, G, loss


def local_step(x, mem, target, W):
    memb = mem.astype(BF)
    saves, kvs = [], []
    xf, xb = x, None
    for l in range(DEPTH):
        kb = mm_nn(memb, W['xa_w_k'][l], None, out_dtype=BF, tm=N_MEM, tn=1024, name=f"l{l}_mem_k")
        vb = mm_nn(memb, W['xa_w_v'][l], None, out_dtype=BF, tm=N_MEM, tn=1024, name=f"l{l}_mem_v")
        xf, xb, sv = _layer_fwd(l, xf, xb, kb, vb, W)
        saves.append(sv)
        kvs.append((kb, vb))
    grads = [None] * DEPTH
    dx, loss = None, None
    for l in reversed(range(DEPTH)):
        lf = (W['ln_b'][l][2:3], target) if l == DEPTH - 1 else None
        dx, grads[l], ls = _layer_bwd(l, dx, saves[l], memb, kvs[l][0], kvs[l][1], W, loss_from=lf)
        if ls is not None:
            loss = ls
    return loss, dx, grads


def _coords():
    return lax.axis_index("x"), lax.axis_index("y"), lax.axis_index("c")


FLIPS = ((1, 0), (0, 1), (1, 1))
SQUARES = ('lru_w_out', 'sconv_w_out', 'w_mix_out', 'xa_w_q', 'xa_w_k', 'xa_w_v', 'xa_w_o')
LAYER_SHAPE = {'w_in': (D, IN_COLS), 'pool_w': (4, POOL_GD, POOL_GD), 'ffn_w_gate': (4, D, D_FF // 4),
               'ffn_w_up': (4, D, D_FF // 4), 'ffn_w_down': (D_FF, D), **{n: (D, D) for n in SQUARES}}
PIECES = ('w_in', 'pool_w') + SQUARES + ('ffn_w_gate', 'ffn_w_up', 'ffn_w_down')


def _mult(v, m):
    return v if isinstance(v, int) else pl.multiple_of(v, m)


def _win(name, ref, k):
    if name == 'w_in':
        return ref.at[:, pl.ds(_mult(((2 * k + 6) % 8) * D, D), 2 * D)]
    if name == 'pool_w':
        return ref.at[:, pl.ds(_mult(k * (POOL_GD // 4), POOL_GD // 4), POOL_GD // 4), :]
    if name in ('ffn_w_gate', 'ffn_w_up'):
        return ref.at[k]
    rows = LAYER_SHAPE[name][0] // 4
    return ref.at[pl.ds(_mult(k * rows, 16), rows), :]


def _half_shape(name):
    shard = _shard_shape(name)
    return (shard[0] // 2,) + shard[1:]


def _half(name, ref, h):
    rows = _shard_shape(name)[0] // 2
    if name == 'pool_w':
        return ref.at[pl.ds(h * rows, rows)]
    return ref.at[pl.ds(_mult(h * rows, 16), rows), :]


def gather_weights(shards, small):
    n_p = len(PIECES)

    def body(*refs):
        srcs = dict(zip(PIECES, refs[:n_p]))
        small_ref = refs[n_p]
        outs = [dict(zip(PIECES, refs[n_p + 1 + l * n_p:n_p + 1 + (l + 1) * n_p])) for l in range(DEPTH)]
        gs_ref = refs[n_p + 1 + DEPTH * n_p]
        ici_send, ici_recv, d2d_send, d2d_recv, own_send, own_recv = refs[n_p + 2 + DEPTH * n_p:]
        x, y, c = _coords()
        me = 2 * x + y
        sib = (x, y, 1 - c)

        def own_copies():
            cps = []
            li = 0
            for n in PIECES:
                for l in range(DEPTH):
                    cps.append(pltpu.make_async_remote_copy(
                        src_ref=srcs[n].at[l], dst_ref=_win(n, outs[l][n], me), send_sem=own_send.at[li],
                        recv_sem=own_recv.at[li], device_id=sib, device_id_type=MESH))
                    li += 1
            cps.append(pltpu.make_async_remote_copy(
                src_ref=small_ref, dst_ref=gs_ref.at[me], send_sem=own_send.at[li], recv_sem=own_recv.at[li],
                device_id=sib, device_id_type=MESH))
            return cps

        def run(lc):
            sends = []
            for j, (fx, fy) in enumerate(FLIPS):
                peer = (x ^ fx, y ^ fy, c)
                for p, n in enumerate(PIECES):
                    k = 3 * p + j
                    sends.append(pltpu.make_async_remote_copy(
                        src_ref=srcs[n].at[lc], dst_ref=_win(n, outs[lc][n], me), send_sem=ici_send.at[k],
                        recv_sem=ici_recv.at[k], device_id=peer, device_id_type=MESH))
                k = 3 * n_p + j
                sends.append(pltpu.make_async_remote_copy(
                    src_ref=small_ref, dst_ref=gs_ref.at[me], send_sem=ici_send.at[k], recv_sem=ici_recv.at[k],
                    device_id=peer, device_id_type=MESH))
            own = own_copies()
            for cp in sends + own:
                cp.start()
            for j, (fx, fy) in enumerate(FLIPS):
                other = 2 * (x ^ fx) + (y ^ fy)
                for p, n in enumerate(PIECES):
                    k = 3 * p + j
                    w = _win(n, outs[lc][n], other)
                    pltpu.make_async_remote_copy(src_ref=srcs[n].at[lc], dst_ref=w, send_sem=ici_send.at[k],
                                                 recv_sem=ici_recv.at[k], device_id=sib,
                                                 device_id_type=MESH).wait_recv()
                    fw = pltpu.make_async_remote_copy(src_ref=w, dst_ref=w, send_sem=d2d_send.at[k],
                                                      recv_sem=d2d_recv.at[k], device_id=sib, device_id_type=MESH)
                    fw.start()
                    sends.append(fw)
                k = 3 * n_p + j
                pltpu.make_async_remote_copy(src_ref=small_ref, dst_ref=gs_ref.at[other], send_sem=ici_send.at[k],
                                             recv_sem=ici_recv.at[k], device_id=sib, device_id_type=MESH).wait_recv()
            for j, (fx, fy) in enumerate(FLIPS):
                other = 2 * (x ^ fx) + (y ^ fy)
                for p, n in enumerate(PIECES):
                    k = 3 * p + j
                    w = _win(n, outs[1 - lc][n], other)
                    pltpu.make_async_remote_copy(src_ref=w, dst_ref=w, send_sem=d2d_send.at[k],
                                                 recv_sem=d2d_recv.at[k], device_id=sib,
                                                 device_id_type=MESH).wait_recv()
            for cp in sends:
                cp.wait_send()
            for cp in own:
                cp.wait()

        @pl.when(c == 0)
        def _():
            run(0)

        @pl.when(c == 1)
        def _():
            run(1)

    hbm = pl.BlockSpec(memory_space=pl.ANY)
    n_out = DEPTH * n_p + 1
    res = pl.pallas_call(
        body, name="gather_weights", in_specs=[hbm] * (n_p + 1), out_specs=[hbm] * n_out,
        out_shape=[jax.ShapeDtypeStruct(LAYER_SHAPE[n], BF) for _ in range(DEPTH) for n in PIECES]
        + [jax.ShapeDtypeStruct((4,) + small.shape, small.dtype)],
        scratch_shapes=[pltpu.SemaphoreType.DMA((3 * n_p + 3,)), pltpu.SemaphoreType.DMA((3 * n_p + 3,)),
                        pltpu.SemaphoreType.DMA((3 * n_p,)), pltpu.SemaphoreType.DMA((3 * n_p,)),
                        pltpu.SemaphoreType.DMA((DEPTH * n_p + 1,)), pltpu.SemaphoreType.DMA((DEPTH * n_p + 1,))],
    )(*[shards[n] for n in PIECES], small)
    full = {n: [res[l * n_p + p] for l in range(DEPTH)] for p, n in enumerate(PIECES)}
    return full, res[DEPTH * n_p]


def split_layers(p0, p1):
    n = len(p0)

    def body(*refs):
        a0, a1 = refs[:n], refs[n:2 * n]
        theirs = refs[2 * n:3 * n]
        send_sems, recv_sems = refs[3 * n:]
        x, y, c = _coords()
        sib = (x, y, 1 - c)

        def give(arrs):
            for i in range(n):
                pltpu.make_async_remote_copy(src_ref=arrs[i], dst_ref=theirs[i], send_sem=send_sems.at[i],
                                             recv_sem=recv_sems.at[i], device_id=sib, device_id_type=MESH).start()

        @pl.when(c == 0)
        def _():
            give(a1)

        @pl.when(c == 1)
        def _():
            give(a0)

        for i in range(n):
            pltpu.make_async_remote_copy(src_ref=a0[i], dst_ref=theirs[i], send_sem=send_sems.at[i],
                                         recv_sem=recv_sems.at[i], device_id=sib, device_id_type=MESH).wait()

    hbm = pl.BlockSpec(memory_space=pl.ANY)
    return pl.pallas_call(
        body, name="split_layers", in_specs=[hbm] * (2 * n), out_specs=[hbm] * n,
        out_shape=[jax.ShapeDtypeStruct(a.shape, a.dtype) for a in p0],
        scratch_shapes=[pltpu.SemaphoreType.DMA((n,)), pltpu.SemaphoreType.DMA((n,))],
    )(*p0, *p1)


def exchange_chips(q, qsmall):
    n_p = len(PIECES)

    def body(*refs):
        srcs = dict(zip(PIECES, refs[:n_p]))
        s_ref = refs[n_p]
        outs = dict(zip(PIECES, refs[n_p + 1:2 * n_p + 1]))
        so_ref = refs[2 * n_p + 1]
        send_sems, recv_sems = refs[2 * n_p + 2:]
        x, y, c = _coords()
        me = 2 * x + y
        sends = []
        for j, (fx, fy) in enumerate(FLIPS):
            peer = (x ^ fx, y ^ fy, c)
            other = 2 * (x ^ fx) + (y ^ fy)
            for p, n in enumerate(PIECES):
                k = 3 * p + j
                sends.append(pltpu.make_async_remote_copy(
                    src_ref=_win(n, srcs[n], other), dst_ref=outs[n].at[j], send_sem=send_sems.at[k],
                    recv_sem=recv_sems.at[k], device_id=peer, device_id_type=MESH))
            k = 3 * n_p + j
            sends.append(pltpu.make_async_remote_copy(
                src_ref=s_ref, dst_ref=so_ref.at[j], send_sem=send_sems.at[k], recv_sem=recv_sems.at[k],
                device_id=peer, device_id_type=MESH))
        for cp in sends:
            cp.start()
        for cp in sends:
            cp.wait()

    hbm = pl.BlockSpec(memory_space=pl.ANY)
    res = pl.pallas_call(
        body, name="exchange_chips", in_specs=[hbm] * (n_p + 1), out_specs=[hbm] * (n_p + 1),
        out_shape=[jax.ShapeDtypeStruct((3,) + _shard_shape(n), BF) for n in PIECES]
        + [jax.ShapeDtypeStruct((3,) + qsmall.shape, qsmall.dtype)],
        scratch_shapes=[pltpu.SemaphoreType.DMA((3 * n_p + 3,)), pltpu.SemaphoreType.DMA((3 * n_p + 3,))],
    )(*[q[n] for n in PIECES], qsmall)
    return dict(zip(PIECES, res[:n_p])), res[n_p]


def _shard_shape(n):
    shp = LAYER_SHAPE[n]
    if n == 'w_in':
        return (shp[0], shp[1] // 4)
    if n == 'pool_w':
        return (shp[0], shp[1] // 4, shp[2])
    if n in ('ffn_w_gate', 'ffn_w_up'):
        return shp[1:]
    return (shp[0] // 4, shp[1])


def swap_cores(s):
    n = len(s)

    def body(*refs):
        srcs, outs = refs[:n], refs[n:2 * n]
        send_sems, recv_sems = refs[2 * n:]
        x, y, c = _coords()
        cps = [pltpu.make_async_remote_copy(src_ref=srcs[i], dst_ref=outs[i], send_sem=send_sems.at[i],
                                            recv_sem=recv_sems.at[i], device_id=(x, y, 1 - c), device_id_type=MESH)
               for i in range(n)]
        for cp in cps:
            cp.start()
        for cp in cps:
            cp.wait()

    hbm = pl.BlockSpec(memory_space=pl.ANY)
    return pl.pallas_call(
        body, name="swap_cores", in_specs=[hbm] * n, out_specs=[hbm] * n,
        out_shape=[jax.ShapeDtypeStruct(a.shape, a.dtype) for a in s],
        scratch_shapes=[pltpu.SemaphoreType.DMA((n,)), pltpu.SemaphoreType.DMA((n,))],
    )(*s)


def add_cores(p0, p1, theirs, *, out_dtype, name):
    shp = p0.shape
    args = [t.reshape(-1, shp[-1]) for t in (p0, p1, theirs)]
    R, C = args[0].shape
    tr = _row_tile(R, C)

    def body(a0_ref, a1_ref, t_ref, o_ref):
        c = lax.axis_index("c")
        t = t_ref[...].astype(F32)

        @pl.when(c == 0)
        def _():
            o_ref[...] = (a0_ref[...].astype(F32) + t).astype(o_ref.dtype)

        @pl.when(c == 1)
        def _():
            o_ref[...] = (a1_ref[...].astype(F32) + t).astype(o_ref.dtype)

    spec = pl.BlockSpec((tr, C), lambda i: (i, 0))
    out = pl.pallas_call(body, name=name, grid=(R // tr,), in_specs=[spec] * 3, out_specs=spec,
                         out_shape=jax.ShapeDtypeStruct((R, C), out_dtype), compiler_params=_cp("arbitrary"))(*args)
    return out.reshape(shp)


def sum_chips(name, q, recv3, chip):
    shard = _shard_shape(name)
    zero = (0,) * len(shard)
    if name == 'w_in':
        tr = 128
        grid = (shard[0] // tr,)
        qspec = pl.BlockSpec((tr, shard[1]), lambda i, me: (i, (me[0] + 3) % 4))
        rspec = pl.BlockSpec((3, tr, shard[1]), lambda i, me: (0, i, 0))
        ospec = pl.BlockSpec((tr, shard[1]), lambda i, me: (i, 0))
    else:
        grid = (1,)
        rspec = pl.BlockSpec((3,) + shard, lambda i, me: (0,) + zero)
        ospec = pl.BlockSpec(shard, lambda i, me: zero)
        if name == 'pool_w':
            qspec = pl.BlockSpec(shard, lambda i, me: (0, me[0], 0))
        elif name in ('ffn_w_gate', 'ffn_w_up'):
            qspec = pl.BlockSpec((None,) + shard, lambda i, me: (me[0], 0, 0))
        else:
            qspec = pl.BlockSpec(shard, lambda i, me: (me[0], 0))

    def body(me_ref, q_ref, r_ref, o_ref):
        del me_ref
        acc = q_ref[...].astype(F32)
        for j in range(3):
            acc = acc + r_ref[j].astype(F32)
        o_ref[...] = acc

    return pl.pallas_call(
        body, name="sum_chips_" + name,
        grid_spec=pltpu.PrefetchScalarGridSpec(num_scalar_prefetch=1, grid=grid, in_specs=[qspec, rspec],
                                               out_specs=ospec),
        out_shape=jax.ShapeDtypeStruct(shard, F32), compiler_params=_cp("arbitrary"))(chip, q, recv3)


def sum_chips_small(q, recv3, chip):
    r, C = q.shape
    slot_of_xor = {2: 0, 1: 1, 3: 2}

    def body(me_ref, q_ref, r_ref, o_ref):
        me = me_ref[0]
        acc = None
        for k in range(4):
            kx = k ^ me
            term = q_ref[...]
            for xv, j in slot_of_xor.items():
                term = jnp.where(kx == xv, r_ref[j], term)
            acc = term if acc is None else acc + term
        o_ref[...] = acc

    return pl.pallas_call(
        body, name="sum_chips_small",
        grid_spec=pltpu.PrefetchScalarGridSpec(
            num_scalar_prefetch=1, grid=(1,),
            in_specs=[pl.BlockSpec((r, C), lambda i, me: (0, 0)), pl.BlockSpec((3, r, C), lambda i, me: (0, 0, 0))],
            out_specs=pl.BlockSpec((r, C), lambda i, me: (0, 0))),
        out_shape=jax.ShapeDtypeStruct((r, C), F32), compiler_params=_cp("arbitrary"))(chip, q, recv3)


def _row_tile(R, C):
    for cand in (1024, 512, 256, 128, 64, 32, 16):
        if R % cand == 0 and cand * C * 4 <= 2 * 1024 * 1024:
            return cand
    return R


def sum_slots(a, *, name):
    n = a.shape[0]
    shp = a.shape[1:]
    a3 = a.reshape(n, -1, shp[-1])
    R, C = a3.shape[1:]
    tr = _row_tile(R, C * n // 2)

    def body(a_ref, o_ref):
        acc = a_ref[0].astype(F32)
        for k in range(1, n):
            acc = acc + a_ref[k].astype(F32)
        o_ref[...] = acc

    out = pl.pallas_call(
        body, name=name, grid=(R // tr,), in_specs=[pl.BlockSpec((n, tr, C), lambda i: (0, i, 0))],
        out_specs=pl.BlockSpec((tr, C), lambda i: (i, 0)), out_shape=jax.ShapeDtypeStruct((R, C), F32),
        compiler_params=_cp("arbitrary"))(a3)
    return out.reshape(shp)


def _adamw_math(w, g, m, v):
    mn = ADAM_B1 * m + (1.0 - ADAM_B1) * g
    vn = ADAM_B2 * v + (1.0 - ADAM_B2) * (g * g)
    m_hat = mn / (1.0 - ADAM_B1 ** ADAM_STEP)
    v_hat = vn / (1.0 - ADAM_B2 ** ADAM_STEP)
    return -ADAM_LR * (m_hat / (jnp.sqrt(v_hat) + ADAM_EPS) + ADAM_WD * w), mn, vn


def adamw(w, g, m, v, *, name):
    shp = w.shape
    args = [t.reshape(-1, shp[-1]) for t in (w, g, m, v)]
    R, C = args[0].shape
    tr = _row_tile(R, C)

    def body(w_ref, g_ref, m_ref, v_ref, d_ref, mo_ref, vo_ref):
        d_ref[...], mo_ref[...], vo_ref[...] = _adamw_math(w_ref[...], g_ref[...], m_ref[...], v_ref[...])

    spec = pl.BlockSpec((tr, C), lambda i: (i, 0))
    res = pl.pallas_call(
        body, name=name, grid=(R // tr,), in_specs=[spec] * 4, out_specs=[spec] * 3,
        out_shape=[jax.ShapeDtypeStruct((R, C), F32)] * 3, compiler_params=_cp("arbitrary"))(*args)
    return [r.reshape(shp) for r in res]


def adamw_layers(w, g_mine, g_theirs, m, v, *, name):
    shp = w.shape
    three = (DEPTH, -1, shp[-1])
    w3, m3, v3 = [t.reshape(three) for t in (w, m, v)]
    ga, gb = [t.reshape(-1, shp[-1]) for t in (g_mine, g_theirs)]
    R, C = ga.shape
    tr = _row_tile(R, C)

    def body(w_ref, ga_ref, gb_ref, m_ref, v_ref, g_ref, d_ref, mo_ref, vo_ref):
        mine = pl.program_id(0) == lax.axis_index("c")
        g = jnp.where(mine, ga_ref[...], gb_ref[...])
        g_ref[...] = g
        d_ref[...], mo_ref[...], vo_ref[...] = _adamw_math(w_ref[...], g, m_ref[...], v_ref[...])

    lay = pl.BlockSpec((None, tr, C), lambda l, i: (l, i, 0))
    one = pl.BlockSpec((tr, C), lambda l, i: (i, 0))
    res = pl.pallas_call(
        body, name=name, grid=(DEPTH, R // tr), in_specs=[lay, one, one, lay, lay], out_specs=[lay] * 4,
        out_shape=[jax.ShapeDtypeStruct(w3.shape, F32)] * 4, compiler_params=_cp("arbitrary", "arbitrary"))(
            w3, ga, gb, m3, v3)
    return [r.reshape(shp) for r in res]


def _remote(src, dst, send_sems, recv_sems, k, peer):
    return pltpu.make_async_remote_copy(src_ref=src, dst_ref=dst, send_sem=send_sems.at[k], recv_sem=recv_sems.at[k],
                                        device_id=peer, device_id_type=MESH)


def gather_layer(l, shards, small=None):
    n_p = len(PIECES)
    with_small = small is not None

    def descs(h_in, h_out, sems):
        srcs = dict(zip(PIECES, h_in[:n_p]))
        outs = dict(zip(PIECES, h_out[:n_p]))
        ici_s, ici_r, d2d_s, d2d_r, own_s, own_r = sems
        x, y, c = _coords()
        me = 2 * x + y
        sib = (x, y, 1 - c)
        ici, fwd, fwd_in, own = [], [], [], []
        for j, (fx, fy) in enumerate(FLIPS):
            peer = (x ^ fx, y ^ fy, c)
            other = 2 * (x ^ fx) + (y ^ fy)
            for p, n in enumerate(PIECES):
                k = 3 * p + j
                mine = _half(n, _win(n, outs[n], me), c)
                landed = _half(n, _win(n, outs[n], other), c)
                sib_half = _half(n, _win(n, outs[n], other), 1 - c)
                ici.append((_remote(_half(n, srcs[n].at[l], c), mine, ici_s, ici_r, k, peer),
                            _remote(_half(n, srcs[n].at[l], c), landed, ici_s, ici_r, k, peer)))
                fwd.append(_remote(landed, landed, d2d_s, d2d_r, k, sib))
                fwd_in.append(_remote(sib_half, sib_half, d2d_s, d2d_r, k, sib))
            if with_small:
                k = 3 * n_p + j
                ici.append((_remote(h_in[n_p], h_out[n_p].at[me], ici_s, ici_r, k, peer),
                            _remote(h_in[n_p], h_out[n_p].at[other], ici_s, ici_r, k, peer)))
        for p, n in enumerate(PIECES):
            own.append(_remote(srcs[n].at[l], _win(n, outs[n], me), own_s, own_r, p, sib))
        if with_small:
            own.append(_remote(h_in[n_p], h_out[n_p].at[me], own_s, own_r, n_p, sib))
        return ici, fwd, fwd_in, own

    def start(h_in, h_out, sems):
        ici, _, _, own = descs(h_in, h_out, sems)
        for send, _ in ici:
            send.start()
        for cp in own:
            cp.start()

    def finish(h_in, h_out, sems):
        ici, fwd, fwd_in, own = descs(h_in, h_out, sems)
        per_chip = n_p + (1 if with_small else 0)
        for j in range(3):
            for p in range(n_p):
                ici[j * per_chip + p][1].wait_recv()
                fwd[j * n_p + p].start()
            if with_small:
                ici[j * per_chip + n_p][1].wait_recv()
        for cp in fwd_in:
            cp.wait_recv()
        for send, _ in ici:
            send.wait_send()
        for cp in fwd:
            cp.wait_send()
        for cp in own:
            cp.wait()

    arrays = [shards[n] for n in PIECES] + ([small] if with_small else [])
    out_shape = [jax.ShapeDtypeStruct(LAYER_SHAPE[n], BF) for n in PIECES]
    if with_small:
        out_shape.append(jax.ShapeDtypeStruct((4,) + small.shape, small.dtype))
    sems = [pltpu.SemaphoreType.DMA((3 * n_p + 3,)), pltpu.SemaphoreType.DMA((3 * n_p + 3,)),
            pltpu.SemaphoreType.DMA((3 * n_p,)), pltpu.SemaphoreType.DMA((3 * n_p,)),
            pltpu.SemaphoreType.DMA((n_p + 1,)), pltpu.SemaphoreType.DMA((n_p + 1,))]
    return Hosted(arrays, out_shape, sems, start, finish)


def run_hosted(host, name):
    n_in, n_out = len(host.arrays), len(host.out_shape)

    def body(*refs):
        h_in, h_out, sems = refs[:n_in], refs[n_in:n_in + n_out], refs[n_in + n_out:]
        host.start(h_in, h_out, sems)
        host.finish(h_in, h_out, sems)

    return pl.pallas_call(body, name=name, in_specs=[HBM_SPEC] * n_in, out_specs=[HBM_SPEC] * n_out,
                          out_shape=list(host.out_shape), scratch_shapes=list(host.sems))(*host.arrays)


def split_halves(parts, small):
    n_p = len(PIECES)
    rh = small.shape[0] // 2

    def body(*refs):
        srcs = dict(zip(PIECES, refs[:n_p]))
        s_ref = refs[n_p]
        outs = dict(zip(PIECES, refs[n_p + 1:2 * n_p + 1]))
        so_ref = refs[2 * n_p + 1]
        send_sems, recv_sems = refs[2 * n_p + 2:]
        x, y, c = _coords()
        sib = (x, y, 1 - c)
        cps = []
        for p, n in enumerate(PIECES):
            for k in range(4):
                cps.append(_remote(_half(n, _win(n, srcs[n], k), 1 - c), outs[n].at[k], send_sems, recv_sems,
                                   4 * p + k, sib))
        cps.append(_remote(s_ref.at[pl.ds(_mult((1 - c) * rh, 8), rh), :], so_ref, send_sems, recv_sems, 4 * n_p, sib))
        for cp in cps:
            cp.start()
        for cp in cps:
            cp.wait()

    res = pl.pallas_call(
        body, name="split_halves", in_specs=[HBM_SPEC] * (n_p + 1), out_specs=[HBM_SPEC] * (n_p + 1),
        out_shape=[jax.ShapeDtypeStruct((4,) + _half_shape(n), BF) for n in PIECES]
        + [jax.ShapeDtypeStruct((rh, small.shape[1]), small.dtype)],
        scratch_shapes=[pltpu.SemaphoreType.DMA((4 * n_p + 1,)), pltpu.SemaphoreType.DMA((4 * n_p + 1,))],
    )(*[parts[n] for n in PIECES], small)
    return dict(zip(PIECES, res[:n_p])), res[n_p]


def add_halves(name, part, theirs, core):
    half = _half_shape(name)
    zero = (0,) * len(half)
    if name == 'w_in':
        pspec = pl.BlockSpec(half, lambda k, cc: (cc[0], (k + 3) % 4))
    elif name == 'pool_w':
        pspec = pl.BlockSpec(half, lambda k, cc: (cc[0], k, 0))
    elif name in ('ffn_w_gate', 'ffn_w_up'):
        pspec = pl.BlockSpec((None,) + half, lambda k, cc: (k, cc[0], 0))
    else:
        pspec = pl.BlockSpec(half, lambda k, cc: (2 * k + cc[0], 0))
    slot = pl.BlockSpec((None,) + half, lambda k, cc: (k,) + zero)

    def body(cc_ref, p_ref, t_ref, o_ref):
        del cc_ref
        o_ref[...] = (p_ref[...].astype(F32) + t_ref[...].astype(F32)).astype(o_ref.dtype)

    return pl.pallas_call(
        body, name="add_cores_" + name,
        grid_spec=pltpu.PrefetchScalarGridSpec(num_scalar_prefetch=1, grid=(4,), in_specs=[pspec, slot],
                                               out_specs=slot),
        out_shape=jax.ShapeDtypeStruct((4,) + half, BF), compiler_params=_cp("arbitrary"))(core, part, theirs)


def add_halves_small(small, theirs, core):
    rh, C = theirs.shape

    def body(cc_ref, p_ref, t_ref, o_ref):
        del cc_ref
        o_ref[...] = p_ref[...] + t_ref[...]

    blk = pl.BlockSpec((rh, C), lambda i, cc: (0, 0))
    return pl.pallas_call(
        body, name="add_cores_small",
        grid_spec=pltpu.PrefetchScalarGridSpec(
            num_scalar_prefetch=1, grid=(1,),
            in_specs=[pl.BlockSpec((rh, C), lambda i, cc: (cc[0], 0)), blk], out_specs=blk),
        out_shape=jax.ShapeDtypeStruct((rh, C), F32), compiler_params=_cp("arbitrary"))(core, small, theirs)


def exchange_halves(q, qsmall):
    n_p = len(PIECES)

    def descs(h_in, h_out, sems):
        send_sems, recv_sems = sems
        x, y, c = _coords()
        cps = []
        for j, (fx, fy) in enumerate(FLIPS):
            peer = (x ^ fx, y ^ fy, c)
            other = 2 * (x ^ fx) + (y ^ fy)
            for p in range(n_p):
                cps.append(_remote(h_in[p].at[other], h_out[p].at[j], send_sems, recv_sems, 3 * p + j, peer))
            cps.append(_remote(h_in[n_p], h_out[n_p].at[j], send_sems, recv_sems, 3 * n_p + j, peer))
        return cps

    def start(h_in, h_out, sems):
        for cp in descs(h_in, h_out, sems):
            cp.start()

    def finish(h_in, h_out, sems):
        for cp in descs(h_in, h_out, sems):
            cp.wait()

    arrays = [q[n] for n in PIECES] + [qsmall]
    out_shape = [jax.ShapeDtypeStruct((3,) + _half_shape(n), BF) for n in PIECES]
    out_shape.append(jax.ShapeDtypeStruct((3,) + qsmall.shape, qsmall.dtype))
    sems = [pltpu.SemaphoreType.DMA((3 * n_p + 3,)), pltpu.SemaphoreType.DMA((3 * n_p + 3,))]
    return Hosted(arrays, out_shape, sems, start, finish)


def sum_halves(name, q, recv3, chip):
    half = _half_shape(name)
    zero = (0,) * len(half)

    def body(me_ref, q_ref, r_ref, o_ref):
        del me_ref
        acc = q_ref[...].astype(F32)
        for j in range(3):
            acc = acc + r_ref[j].astype(F32)
        o_ref[...] = acc

    return pl.pallas_call(
        body, name="sum_chips_" + name,
        grid_spec=pltpu.PrefetchScalarGridSpec(
            num_scalar_prefetch=1, grid=(1,),
            in_specs=[pl.BlockSpec((None,) + half, lambda i, me: (me[0],) + zero),
                      pl.BlockSpec((3,) + half, lambda i, me: (0,) + zero)],
            out_specs=pl.BlockSpec(half, lambda i, me: zero)),
        out_shape=jax.ShapeDtypeStruct(half, F32), compiler_params=_cp("arbitrary"))(chip, q, recv3)


def adamw_halves(w, g0, m, v, *, name):
    shp = w.shape
    C = shp[-1]
    four = (DEPTH, 2, -1, C)
    w4, m4, v4 = [t.reshape(four) for t in (w, m, v)]
    gs = [t.reshape(-1, C) for pair in g0 for t in pair]
    Rh = gs[0].shape[0]
    tr = _row_tile(Rh, C)

    def body(w_ref, a0_ref, b0_ref, a1_ref, b1_ref, m_ref, v_ref, g_ref, d_ref, mo_ref, vo_ref):
        mine = pl.program_id(1) == lax.axis_index("c")
        g_l0 = jnp.where(mine, a0_ref[...], b0_ref[...])
        g_l1 = jnp.where(mine, a1_ref[...], b1_ref[...])
        g = jnp.where(pl.program_id(0) == 0, g_l0, g_l1)
        g_ref[...] = g
        d_ref[...], mo_ref[...], vo_ref[...] = _adamw_math(w_ref[...], g, m_ref[...], v_ref[...])

    lay = pl.BlockSpec((None, None, tr, C), lambda l, h, i: (l, h, i, 0))
    one = pl.BlockSpec((tr, C), lambda l, h, i: (i, 0))
    res = pl.pallas_call(
        body, name=name, grid=(DEPTH, 2, Rh // tr), in_specs=[lay, one, one, one, one, lay, lay],
        out_specs=[lay] * 4, out_shape=[jax.ShapeDtypeStruct(w4.shape, F32)] * 4,
        compiler_params=_cp("arbitrary", "arbitrary", "arbitrary"))(w4, *gs, m4, v4)
    return [r.reshape(shp) for r in res]


def _local_shape(name, full_shape):
    shp = list(full_shape)
    ax = BIG_SHARDED.get(name, SMALL_SHARDED.get(name))
    if ax is not None:
        shp[ax] //= 4
    return tuple(shp)


FULL_SHAPES = {
    'w_in': (DEPTH, D, IN_COLS), 'b_in': (DEPTH, IN_COLS), 'pool_w': (DEPTH, 4, POOL_GD, POOL_GD),
    'pool_scale': (DEPTH, D), 'lru_conv_w': (DEPTH, 4, D), 'lru_conv_b': (DEPTH, D),
    'lru_w_r': (DEPTH, LRU_HEADS, LRU_HD, LRU_HD), 'lru_b_r': (DEPTH, D),
    'lru_w_i': (DEPTH, LRU_HEADS, LRU_HD, LRU_HD), 'lru_b_i': (DEPTH, D), 'lru_lambda': (DEPTH, D),
    'lru_w_out': (DEPTH, D, D), 'sconv_w': (DEPTH, 3, D), 'sconv_w_out': (DEPTH, D, D), 'w_mix_out': (DEPTH, D, D),
    'xa_w_q': (DEPTH, D, D), 'xa_w_k': (DEPTH, D, D), 'xa_w_v': (DEPTH, D, D), 'xa_w_o': (DEPTH, D, D),
    'ffn_w_gate': (DEPTH, D, D_FF), 'ffn_w_up': (DEPTH, D, D_FF), 'ffn_w_down': (DEPTH, D_FF, D),
    'ln_g': (DEPTH, 3, D), 'ln_b': (DEPTH, 3, D)}


def _pack(arrs, names, width, dtype, row_mult):
    flat = jnp.concatenate([arrs[n].astype(dtype).reshape(-1) for n in names])
    pad = (-flat.shape[0]) % (width * row_mult)
    if pad:
        flat = jnp.concatenate([flat, jnp.zeros((pad,), dtype)])
    return flat.reshape(-1, width)


def _unpack(flat2d, names, shapes):
    flat = flat2d.reshape(-1)
    out, off = {}, 0
    for n in names:
        size = 1
        for s in shapes[n]:
            size *= s
        out[n] = flat[off:off + size].reshape(shapes[n])
        off += size
    return out


def _gathered_full(g4, names, sharded_axis):
    loc_shapes = {n: _local_shape(n, FULL_SHAPES[n]) for n in names}
    per = [_unpack(g4[k], names, loc_shapes) for k in range(4)]
    return {n: jnp.concatenate([per[k][n] for k in range(4)], axis=sharded_axis[n]) for n in names}


def _perm_cols(a, perm, axis):
    blocks = [lax.slice_in_dim(a, p * D, (p + 1) * D, axis=axis) for p in perm]
    return jnp.concatenate(blocks, axis=axis)


def _shards_apart(a):
    w = a.shape[1] // 4
    return jnp.stack([a[:, k * w:(k + 1) * w] for k in range(4)])


def _shards_joined(a4):
    return jnp.concatenate([a4[k] for k in range(4)], axis=1)


Z_INV = tuple(Z_PERM.index(j) for j in range(8))
SMALL_SH_NAMES = list(SMALL_SHARDED)
SMALL_ROWS = 32


def kernel(x, mem, w_in, b_in, pool_w, pool_scale, lru_conv_w, lru_conv_b, lru_w_r, lru_b_r, lru_w_i, lru_b_i, lru_lambda, lru_w_out, sconv_w, sconv_w_out, w_mix_out, xa_w_q, xa_w_k, xa_w_v, xa_w_o, ffn_w_gate, ffn_w_up, ffn_w_down, ln_g, ln_b, loss_target, m_w_in, m_b_in, m_pool_w, m_pool_scale, m_lru_conv_w, m_lru_conv_b, m_lru_w_r, m_lru_b_r, m_lru_w_i, m_lru_b_i, m_lru_lambda, m_lru_w_out, m_sconv_w, m_sconv_w_out, m_w_mix_out, m_xa_w_q, m_xa_w_k, m_xa_w_v, m_xa_w_o, m_ffn_w_gate, m_ffn_w_up, m_ffn_w_down, m_ln_g, m_ln_b, v_w_in, v_b_in, v_pool_w, v_pool_scale, v_lru_conv_w, v_lru_conv_b, v_lru_w_r, v_lru_b_r, v_lru_w_i, v_lru_b_i, v_lru_lambda, v_lru_w_out, v_sconv_w, v_sconv_w_out, v_w_mix_out, v_xa_w_q, v_xa_w_k, v_xa_w_v, v_xa_w_o, v_ffn_w_gate, v_ffn_w_up, v_ffn_w_down, v_ln_g, v_ln_b):
    loc = dict(w_in=w_in, b_in=b_in, pool_w=pool_w, pool_scale=pool_scale, lru_conv_w=lru_conv_w,
               lru_conv_b=lru_conv_b, lru_w_r=lru_w_r, lru_b_r=lru_b_r, lru_w_i=lru_w_i, lru_b_i=lru_b_i,
               lru_lambda=lru_lambda, lru_w_out=lru_w_out, sconv_w=sconv_w, sconv_w_out=sconv_w_out,
               w_mix_out=w_mix_out, xa_w_q=xa_w_q, xa_w_k=xa_w_k, xa_w_v=xa_w_v, xa_w_o=xa_w_o,
               ffn_w_gate=ffn_w_gate, ffn_w_up=ffn_w_up, ffn_w_down=ffn_w_down, ln_g=ln_g, ln_b=ln_b)
    mom = dict(w_in=m_w_in, b_in=m_b_in, pool_w=m_pool_w, pool_scale=m_pool_scale, lru_conv_w=m_lru_conv_w,
               lru_conv_b=m_lru_conv_b, lru_w_r=m_lru_w_r, lru_b_r=m_lru_b_r, lru_w_i=m_lru_w_i, lru_b_i=m_lru_b_i,
               lru_lambda=m_lru_lambda, lru_w_out=m_lru_w_out, sconv_w=m_sconv_w, sconv_w_out=m_sconv_w_out,
               w_mix_out=m_w_mix_out, xa_w_q=m_xa_w_q, xa_w_k=m_xa_w_k, xa_w_v=m_xa_w_v, xa_w_o=m_xa_w_o,
               ffn_w_gate=m_ffn_w_gate, ffn_w_up=m_ffn_w_up, ffn_w_down=m_ffn_w_down, ln_g=m_ln_g, ln_b=m_ln_b)
    var = dict(w_in=v_w_in, b_in=v_b_in, pool_w=v_pool_w, pool_scale=v_pool_scale, lru_conv_w=v_lru_conv_w,
               lru_conv_b=v_lru_conv_b, lru_w_r=v_lru_w_r, lru_b_r=v_lru_b_r, lru_w_i=v_lru_w_i, lru_b_i=v_lru_b_i,
               lru_lambda=v_lru_lambda, lru_w_out=v_lru_w_out, sconv_w=v_sconv_w, sconv_w_out=v_sconv_w_out,
               w_mix_out=v_w_mix_out, xa_w_q=v_xa_w_q, xa_w_k=v_xa_w_k, xa_w_v=v_xa_w_v, xa_w_o=v_xa_w_o,
               ffn_w_gate=v_ffn_w_gate, ffn_w_up=v_ffn_w_up, ffn_w_down=v_ffn_w_down, ln_g=v_ln_g, ln_b=v_ln_b)

    chip = 2 * lax.axis_index("x") + lax.axis_index("y")
    core = lax.axis_index("c")
    chip_arr = jnp.reshape(chip, (1,)).astype(jnp.int32)
    core_arr = jnp.reshape(core, (1,)).astype(jnp.int32)
    n_p = len(PIECES)

    shards = {n: loc[n].astype(BF) for n in PIECES}
    small = _pack(loc, SMALL_SH_NAMES, 256, F32, 8)
    got = run_hosted(gather_layer(0, shards, small), "gather_layer0")
    vec = _gathered_full(got[n_p], SMALL_SH_NAMES, SMALL_SHARDED)
    W = {n: [None] * DEPTH for n in ('w_in', 'pool_w', 'ffn_w_down', 'ffn_w_gu') + SQUARES}
    W['b_in'] = [jnp.roll(b_in[l:l + 1], -2 * D, axis=1) for l in range(DEPTH)]
    for n in ('lru_conv_w', 'sconv_w', 'ln_g', 'ln_b'):
        W[n] = [vec[n][l] for l in range(DEPTH)]
    for n in ('lru_w_r', 'lru_w_i'):
        W[n] = [loc[n][l].astype(BF) for l in range(DEPTH)]
    for n in ('pool_scale', 'lru_conv_b', 'lru_b_r', 'lru_b_i', 'lru_lambda'):
        W[n] = [loc[n][l:l + 1] for l in range(DEPTH)]

    def take_layer(l, arrays):
        full = dict(zip(PIECES, arrays))
        W['ffn_w_gu'][l] = jnp.concatenate([_shards_joined(full['ffn_w_gate']), _shards_joined(full['ffn_w_up'])],
                                           axis=1)
        for n in ('w_in', 'pool_w', 'ffn_w_down') + SQUARES:
            W[n][l] = full[n]

    take_layer(0, got[:n_p])

    xs, memb = x[0], mem[0].astype(BF)
    saves, kvs = [], []
    xf, xb = xs, None
    for l in range(DEPTH):
        kb = mm_nn(memb, W['xa_w_k'][l], None, out_dtype=BF, tm=N_MEM, tn=1024, name=f"l{l}_mem_k")
        vb = mm_nn(memb, W['xa_w_v'][l], None, out_dtype=BF, tm=N_MEM, tn=1024, name=f"l{l}_mem_v")
        host = gather_layer(1, shards) if l == 0 else None
        xf, xb, sv = _layer_fwd(l, xf, xb, kb, vb, W, host=host)
        if l == 0:
            take_layer(1, sv['hosted'][:n_p])
        saves.append(sv)
        kvs.append((kb, vb))

    def packed(G):
        g = dict(G)
        g['ffn_w_gate'] = _shards_apart(g['ffn_w_gu'][:, :D_FF])
        g['ffn_w_up'] = _shards_apart(g['ffn_w_gu'][:, D_FF:])
        g['pool_w'] = g['pool_w'].astype(BF)
        g['b_in'] = jnp.roll(g['b_in'], 2 * D, axis=1)
        return {n: g[n] for n in PIECES}, _pack(g, SMALL_ALL, D, F32, SMALL_ROWS)

    def reduce_start(G):
        parts, smallp = packed(G)
        theirs, theirs_small = split_halves(parts, smallp)
        q = {n: add_halves(n, parts[n], theirs[n], core_arr) for n in PIECES}
        qs = add_halves_small(smallp, theirs_small, core_arr)
        return q, qs, exchange_halves(q, qs)

    def reduce_finish(q, qs, recv):
        sums = [sum_halves(n, q[n], recv[p], chip_arr) for p, n in enumerate(PIECES)]
        sums.append(sum_chips_small(qs, recv[n_p], chip_arr))
        other = swap_cores(sums)
        return sums, other

    lf = (W['ln_b'][1][2:3], loss_target[0])
    dx, G1, loss_blk = _layer_bwd(1, None, saves[1], memb, kvs[1][0], kvs[1][1], W, loss_from=lf)
    q1, qs1, host1 = reduce_start(G1)
    grad_x, G0, _ = _layer_bwd(0, dx, saves[0], memb, kvs[0][0], kvs[0][1], W, host=host1)
    red = [None, reduce_finish(q1, qs1, G0['hosted'])]
    q0, qs0, host0 = reduce_start(G0)
    red[0] = reduce_finish(q0, qs0, run_hosted(host0, "exchange_layer0"))
    loss = lax.psum(loss_blk[0, 0], ("x", "y", "c"))

    small_shapes = {n: FULL_SHAPES[n][1:] for n in SMALL_ALL}
    per_layer = []
    for l in range(DEPTH):
        mine, theirs = red[l][0][-1], red[l][1][-1]
        whole = jnp.where(core == 0, jnp.concatenate([mine, theirs]), jnp.concatenate([theirs, mine]))
        per_layer.append(_unpack(whole, SMALL_ALL, small_shapes))
    grads = {}
    for n in SMALL_ALL:
        gn = jnp.stack([per_layer[l][n] for l in range(DEPTH)])
        if n in SMALL_SHARDED:
            size = loc[n].shape[SMALL_SHARDED[n]]
            gn = lax.dynamic_slice_in_dim(gn, chip * size, size, axis=SMALL_SHARDED[n])
        grads[n] = gn

    out_d, out_m, out_v = {}, {}, {}
    for p, n in enumerate(PIECES):
        pairs = [(red[l][0][p], red[l][1][p]) for l in range(DEPTH)]
        grads[n], out_d[n], out_m[n], out_v[n] = adamw_halves(loc[n], pairs, mom[n], var[n], name="adamw_" + n)
    for n in SMALL_ALL:
        out_d[n], out_m[n], out_v[n] = adamw(loc[n], grads[n], mom[n], var[n], name="adamw_" + n)

    return (loss, grad_x[None], *[grads[n] for n in WEIGHTS], *[out_d[n] for n in WEIGHTS],
            *[out_m[n] for n in WEIGHTS], *[out_v[n] for n in WEIGHTS])
```

```python
import functools

import jax
import jax.numpy as jnp
from jax import lax
from jax.experimental import pallas as pl
from jax.experimental.pallas import tpu as pltpu

F32 = jnp.float32
BF = jnp.bfloat16
MESH = pl.DeviceIdType.MESH

D = 1024
DEPTH = 2
N_MEM = 256
POOL_WINDOWS = (2, 4, 8, 16)
POOL_GD = 256
LRU_HEADS = 8
LRU_HD = 128
LRU_C = 8.0
X_HEADS = 4
X_HD = 256
D_FF = 2816
IN_COLS = 8 * D
ALPHA = (2 * DEPTH) ** 0.25
LN_EPS = 1e-5
ADAM_LR = 0.001
ADAM_B1 = 0.9
ADAM_B2 = 0.999
ADAM_EPS = 1e-08
ADAM_WD = 0.01
ADAM_STEP = 10

Z_PERM = (2, 3, 4, 5, 6, 7, 0, 1)
ZB_SCONV, ZB_GATE, ZB_POOL, ZB_LRU = 0, 1, 6, 7
HALO = 16
VMEM_LIMIT = 56 * 1024 * 1024

WEIGHTS = ['w_in', 'b_in', 'pool_w', 'pool_scale', 'lru_conv_w', 'lru_conv_b', 'lru_w_r', 'lru_b_r', 'lru_w_i',
           'lru_b_i', 'lru_lambda', 'lru_w_out', 'sconv_w', 'sconv_w_out', 'w_mix_out', 'xa_w_q', 'xa_w_k', 'xa_w_v',
           'xa_w_o', 'ffn_w_gate', 'ffn_w_up', 'ffn_w_down', 'ln_g', 'ln_b']
BIG_SHARDED = {'w_in': 2, 'pool_w': 2, 'lru_w_out': 1, 'sconv_w_out': 1, 'w_mix_out': 1, 'xa_w_q': 1, 'xa_w_k': 1,
               'xa_w_v': 1, 'xa_w_o': 1, 'ffn_w_gate': 2, 'ffn_w_up': 2, 'ffn_w_down': 1}
SMALL_SHARDED = {'lru_conv_w': 2, 'sconv_w': 2, 'ln_g': 2, 'ln_b': 2}
REPLICATED = ['b_in', 'pool_scale', 'lru_conv_b', 'lru_w_r', 'lru_b_r', 'lru_w_i', 'lru_b_i', 'lru_lambda']
SMALL_ALL = ['b_in', 'pool_scale', 'lru_conv_w', 'lru_conv_b', 'lru_w_r', 'lru_b_r', 'lru_w_i', 'lru_b_i',
             'lru_lambda', 'sconv_w', 'ln_g', 'ln_b']


def _cp(*sem):
    return pltpu.CompilerParams(dimension_semantics=sem, vmem_limit_bytes=VMEM_LIMIT)


def _sigmoid(x):
    return 0.5 * jnp.tanh(0.5 * x) + 0.5


class Hosted:
    def __init__(self, arrays, out_shape, sems, start, finish):
        self.arrays, self.out_shape, self.sems, self.start, self.finish = arrays, out_shape, sems, start, finish


def _with_host(host, n_in, n_out, refs):
    if host is None:
        return refs[:n_in], (), refs[n_in:n_in + n_out], (), refs[n_in + n_out:], ()
    hi, ho, hs = len(host.arrays), len(host.out_shape), len(host.sems)
    ins, h_in = refs[:n_in], refs[n_in:n_in + hi]
    outs = refs[n_in + hi:n_in + hi + n_out]
    h_out = refs[n_in + hi + n_out:n_in + hi + n_out + ho]
    rest = refs[n_in + hi + n_out + ho:]
    return ins, h_in, outs, h_out, rest[:len(rest) - hs], rest[len(rest) - hs:]


HBM_SPEC = pl.BlockSpec(memory_space=pl.ANY)


def mm_nn(a, w, bias, *, out_dtype, tm, tn, name, host=None):
    T, K = a.shape
    N = w.shape[1]
    tm, tn = min(tm, T), min(tn, N)
    nj, ni = N // tn, T // tm
    n_in = 2 if bias is None else 3

    def body(*refs):
        ins, h_in, (o_ref,), h_out, _, h_sems = _with_host(host, n_in, 1, refs)
        a_ref, w_ref = ins[:2]
        j, i = pl.program_id(0), pl.program_id(1)
        if host is not None:
            @pl.when((j == 0) & (i == 0))
            def _():
                host.start(h_in, h_out, h_sems)
        acc = jnp.dot(a_ref[...].astype(BF), w_ref[...], preferred_element_type=F32)
        if bias is not None:
            acc = acc + ins[2][...]
        o_ref[...] = acc.astype(o_ref.dtype)
        if host is not None:
            @pl.when((j == nj - 1) & (i == ni - 1))
            def _():
                host.finish(h_in, h_out, h_sems)

    in_specs = [pl.BlockSpec((tm, K), lambda j, i: (i, 0)), pl.BlockSpec((K, tn), lambda j, i: (0, j))]
    args = [a, w]
    if bias is not None:
        in_specs.append(pl.BlockSpec((1, tn), lambda j, i: (0, j)))
        args.append(bias)
    out_specs = [pl.BlockSpec((tm, tn), lambda j, i: (i, j))]
    out_shape = [jax.ShapeDtypeStruct((T, N), out_dtype)]
    scratch = []
    if host is not None:
        in_specs += [HBM_SPEC] * len(host.arrays)
        args += list(host.arrays)
        out_specs += [HBM_SPEC] * len(host.out_shape)
        out_shape += list(host.out_shape)
        scratch = list(host.sems)
    res = pl.pallas_call(
        body, name=name, grid=(nj, ni), in_specs=in_specs, out_specs=out_specs, out_shape=out_shape,
        scratch_shapes=scratch, compiler_params=_cp("arbitrary", "arbitrary"))(*args)
    return res[0] if host is None else (res[0], res[1:])


def mm_nt(a, w, res, *, out_dtype, tm, tc, name, host=None):
    T, C = a.shape
    K = w.shape[0]
    tm, tc = min(tm, T), min(tc, C)
    nc = C // tc
    ni = T // tm
    n_in = 2 if res is None else 3

    def body(*refs):
        ins, h_in, (o_ref,), h_out, (acc_ref,), h_sems = _with_host(host, n_in, 1, refs)
        a_ref, w_ref = ins[:2]
        r_ref = ins[2] if res is not None else None
        c = pl.program_id(1)
        if host is not None:
            @pl.when((pl.program_id(0) == 0) & (c == 0))
            def _():
                host.start(h_in, h_out, h_sems)

            @pl.when((pl.program_id(0) == ni - 1) & (c == nc - 1))
            def _():
                host.finish(h_in, h_out, h_sems)
        part = lax.dot_general(a_ref[...].astype(BF), w_ref[...], (((1,), (1,)), ((), ())),
                               preferred_element_type=F32)

        @pl.when(c == 0)
        def _():
            acc_ref[...] = part

        @pl.when(c > 0)
        def _():
            acc_ref[...] += part

        @pl.when(c == nc - 1)
        def _():
            out = acc_ref[...]
            if res is not None:
                out = out + ALPHA * r_ref[...]
            o_ref[...] = out.astype(o_ref.dtype)

    in_specs = [pl.BlockSpec((tm, tc), lambda i, c: (i, c)), pl.BlockSpec((K, tc), lambda i, c: (0, c))]
    args = [a, w]
    if res is not None:
        in_specs.append(pl.BlockSpec((tm, K), lambda i, c: (i, 0)))
        args.append(res)
    out_specs = [pl.BlockSpec((tm, K), lambda i, c: (i, 0))]
    out_shape = [jax.ShapeDtypeStruct((T, K), out_dtype)]
    scratch = [pltpu.VMEM((tm, K), F32)]
    if host is not None:
        in_specs += [HBM_SPEC] * len(host.arrays)
        args += list(host.arrays)
        out_specs += [HBM_SPEC] * len(host.out_shape)
        out_shape += list(host.out_shape)
        scratch += list(host.sems)
    out = pl.pallas_call(
        body, name=name, grid=(ni, nc), in_specs=in_specs, out_specs=out_specs, out_shape=out_shape,
        scratch_shapes=scratch, compiler_params=_cp("arbitrary", "arbitrary"))(*args)
    return out[0] if host is None else (out[0], out[1:])


def mm_tn(a, b, *, out_dtype, tk, tn, tt, name, colsum=False):
    T, K = a.shape
    N = b.shape[1]
    tk, tn, tt = min(tk, K), min(tn, N), min(tt, T)
    nt = T // tt

    def body(*refs):
        if colsum:
            a_ref, b_ref, o_ref, cs_ref, acc_ref = refs
        else:
            a_ref, b_ref, o_ref, acc_ref = refs
        i, t = pl.program_id(1), pl.program_id(2)
        bb = b_ref[...]
        part = lax.dot_general(a_ref[...].astype(BF), bb.astype(BF), (((0,), (0,)), ((), ())),
                               preferred_element_type=F32)

        @pl.when(t == 0)
        def _():
            acc_ref[...] = part

        @pl.when(t > 0)
        def _():
            acc_ref[...] += part

        @pl.when(t == nt - 1)
        def _():
            o_ref[...] = acc_ref[...].astype(o_ref.dtype)

        if colsum:
            s = jnp.sum(bb.astype(F32), axis=0, keepdims=True)

            @pl.when((i == 0) & (t == 0))
            def _():
                cs_ref[...] = s

            @pl.when((i == 0) & (t > 0))
            def _():
                cs_ref[...] += s

    out_specs = [pl.BlockSpec((tk, tn), lambda j, i, t: (i, j))]
    out_shape = [jax.ShapeDtypeStruct((K, N), out_dtype)]
    if colsum:
        out_specs.append(pl.BlockSpec((1, tn), lambda j, i, t: (0, j)))
        out_shape.append(jax.ShapeDtypeStruct((1, N), F32))
    res = pl.pallas_call(
        body, name=name, grid=(N // tn, K // tk, nt),
        in_specs=[pl.BlockSpec((tt, tk), lambda j, i, t: (t, i)), pl.BlockSpec((tt, tn), lambda j, i, t: (t, j))],
        out_specs=out_specs, out_shape=out_shape, scratch_shapes=[pltpu.VMEM((tk, tn), F32)],
        compiler_params=_cp("arbitrary", "arbitrary", "arbitrary"))(a, b)
    return res if colsum else res[0]


def mm_res_ln(a, w, res, g, b, *, tm, name):
    T, K = a.shape
    tm = min(tm, T)

    def body(a_ref, w_ref, r_ref, g_ref, b_ref, y_ref, yb_ref, xh_ref, rs_ref):
        pre = ALPHA * r_ref[...] + jnp.dot(a_ref[...].astype(BF), w_ref[...], preferred_element_type=F32)
        mu = jnp.mean(pre, axis=-1, keepdims=True)
        cen = pre - mu
        var = jnp.mean(cen * cen, axis=-1, keepdims=True)
        rstd = lax.rsqrt(var + LN_EPS)
        xhat = cen * rstd
        y = xhat * g_ref[...] + b_ref[...]
        y_ref[...] = y
        yb_ref[...] = y.astype(BF)
        xh_ref[...] = xhat
        rs_ref[...] = rstd

    row = lambda i: (i, 0)
    fix = lambda i: (0, 0)
    return pl.pallas_call(
        body, name=name, grid=(T // tm,),
        in_specs=[pl.BlockSpec((tm, K), row), pl.BlockSpec((K, D), fix), pl.BlockSpec((tm, D), row),
                  pl.BlockSpec((1, D), fix), pl.BlockSpec((1, D), fix)],
        out_specs=[pl.BlockSpec((tm, D), row), pl.BlockSpec((tm, D), row), pl.BlockSpec((tm, D), row),
                   pl.BlockSpec((tm, 1), row)],
        out_shape=[jax.ShapeDtypeStruct((T, D), F32), jax.ShapeDtypeStruct((T, D), BF),
                   jax.ShapeDtypeStruct((T, D), F32), jax.ShapeDtypeStruct((T, 1), F32)],
        compiler_params=_cp("arbitrary"))(a, w, res, g, b)


def ln_bwd(dy, xhat, rstd, g, *, tm, name, loss_from=None):
    T = xhat.shape[0]
    tm = min(tm, T)
    with_loss = loss_from is not None

    def body(*refs):
        if with_loss:
            xh_ref, rs_ref, g_ref, b_ref, t_ref, dp_ref, dpb_ref, dg_ref, db_ref, ls_ref = refs
        else:
            dy_ref, xh_ref, rs_ref, g_ref, dp_ref, dpb_ref, dg_ref, db_ref = refs
        i = pl.program_id(0)
        xhat_ = xh_ref[...]
        gg = g_ref[...]
        if with_loss:
            err = xhat_ * gg + b_ref[...] - t_ref[...]
            dyv = err * (1.0 / D)
            lpart = 0.5 * jnp.sum(jnp.sum(err * err, axis=-1, keepdims=True) * (1.0 / D))
        else:
            dyv = dy_ref[...]
        dxh = dyv * gg
        m1 = jnp.mean(dxh, axis=-1, keepdims=True)
        m2 = jnp.mean(dxh * xhat_, axis=-1, keepdims=True)
        dpre = rs_ref[...] * (dxh - m1 - xhat_ * m2)
        dp_ref[...] = dpre
        dpb_ref[...] = dpre.astype(BF)
        dgp = jnp.sum(dyv * xhat_, axis=0, keepdims=True)
        dbp = jnp.sum(dyv, axis=0, keepdims=True)

        @pl.when(i == 0)
        def _():
            dg_ref[...] = dgp
            db_ref[...] = dbp
            if with_loss:
                ls_ref[...] = jnp.full((8, 128), lpart, F32)

        @pl.when(i > 0)
        def _():
            dg_ref[...] += dgp
            db_ref[...] += dbp
            if with_loss:
                ls_ref[...] += jnp.full((8, 128), lpart, F32)

    row = lambda i: (i, 0)
    fix = lambda i: (0, 0)
    if with_loss:
        in_specs = [pl.BlockSpec((tm, D), row), pl.BlockSpec((tm, 1), row), pl.BlockSpec((1, D), fix),
                    pl.BlockSpec((1, D), fix), pl.BlockSpec((tm, D), row)]
        args = [xhat, rstd, g, loss_from[0], loss_from[1]]
    else:
        in_specs = [pl.BlockSpec((tm, D), row), pl.BlockSpec((tm, D), row), pl.BlockSpec((tm, 1), row),
                    pl.BlockSpec((1, D), fix)]
        args = [dy, xhat, rstd, g]
    out_specs = [pl.BlockSpec((tm, D), row), pl.BlockSpec((tm, D), row), pl.BlockSpec((1, D), fix),
                 pl.BlockSpec((1, D), fix)]
    out_shape = [jax.ShapeDtypeStruct((T, D), F32), jax.ShapeDtypeStruct((T, D), BF),
                 jax.ShapeDtypeStruct((1, D), F32), jax.ShapeDtypeStruct((1, D), F32)]
    if with_loss:
        out_specs.append(pl.BlockSpec((8, 128), fix))
        out_shape.append(jax.ShapeDtypeStruct((8, 128), F32))
    return pl.pallas_call(body, name=name, grid=(T // tm,), in_specs=in_specs, out_specs=out_specs,
                          out_shape=out_shape, compiler_params=_cp("arbitrary"))(*args)


def _prev_halo(tm, blk):
    return lambda i: (jnp.maximum(i * (tm // HALO) - 1, 0), blk)


def _next_halo(tm, T, blk):
    return lambda i: (jnp.minimum((i + 1) * (tm // HALO), T // HALO - 1), blk)


def _pool_p(ext, t, g):
    e = ext[:, g * POOL_GD:(g + 1) * POOL_GD]
    s = e
    for sh in (1, 2, 4, 8)[:g + 1]:
        s = s + pltpu.roll(s, sh, axis=0)
    inv = 1.0 / jnp.minimum(t + 1, POOL_WINDOWS[g]).astype(F32)
    return s[HALO:] * inv - e[HALO:]


def pool_fwd(z, pw, *, tm, name):
    T = z.shape[0]
    tm = min(tm, T)

    def body(zm_ref, zh_ref, pw_ref, o_ref):
        i = pl.program_id(0)
        keep = jnp.where(i == 0, 0.0, 1.0).astype(F32)
        ext = jnp.concatenate([zh_ref[...].astype(F32) * keep, zm_ref[...].astype(F32)], axis=0)
        t = i * tm + lax.broadcasted_iota(jnp.int32, (tm, 1), 0)
        outs = [jnp.dot(_pool_p(ext, t, g).astype(BF), pw_ref[g], preferred_element_type=F32) for g in range(4)]
        o_ref[...] = jnp.concatenate(outs, axis=1).astype(o_ref.dtype)

    return pl.pallas_call(
        body, name=name, grid=(T // tm,),
        in_specs=[pl.BlockSpec((tm, D), lambda i: (i, ZB_POOL)), pl.BlockSpec((HALO, D), _prev_halo(tm, ZB_POOL)),
                  pl.BlockSpec((4, POOL_GD, POOL_GD), lambda i: (0, 0, 0))],
        out_specs=pl.BlockSpec((tm, D), lambda i: (i, 0)),
        out_shape=jax.ShapeDtypeStruct((T, D), BF), compiler_params=_cp("arbitrary"))(z, z, pw)


def pool_bwd(dz, dyp, yp_pre, z, pw, ps, *, tm, name):
    T = z.shape[0]
    tm = min(tm, T)
    nt = T // tm

    def body(dz_in, dy_ref, dyn_ref, yp_ref, zm_ref, zh_ref, pw_ref, ps_ref, dz_ref, dpw_ref, dps_ref):
        del dz_in
        i = pl.program_id(0)
        keep_p = jnp.where(i == 0, 0.0, 1.0).astype(F32)
        keep_n = jnp.where(i == nt - 1, 0.0, 1.0).astype(F32)
        ext = jnp.concatenate([zh_ref[...].astype(F32) * keep_p, zm_ref[...].astype(F32)], axis=0)
        t = i * tm + lax.broadcasted_iota(jnp.int32, (tm, 1), 0)
        psv = ps_ref[...]
        dy = dy_ref[...].astype(F32)
        dyp_ext = jnp.concatenate([dy, dyn_ref[...].astype(F32) * keep_n], axis=0) * psv
        t_ext = i * tm + lax.broadcasted_iota(jnp.int32, (tm + HALO, 1), 0)
        dps = jnp.sum(dy * yp_ref[...].astype(F32), axis=0, keepdims=True)
        dzs, dpws = [], []
        for g in range(4):
            sl = slice(g * POOL_GD, (g + 1) * POOL_GD)
            dyg = dyp_ext[:, sl].astype(BF)
            dp = lax.dot_general(dyg, pw_ref[g], (((1,), (1,)), ((), ())), preferred_element_type=F32)
            q = dp * (1.0 / jnp.minimum(t_ext + 1, POOL_WINDOWS[g]).astype(F32))
            s = q
            for sh in (1, 2, 4, 8)[:g + 1]:
                s = s + pltpu.roll(s, tm + HALO - sh, axis=0)
            dzs.append(s[:tm] - dp[:tm])
            p = _pool_p(ext, t, g).astype(BF)
            dpws.append(lax.dot_general(p, dyg[:tm], (((0,), (0,)), ((), ())), preferred_element_type=F32))
        dz_ref[...] = jnp.concatenate(dzs, axis=1).astype(dz_ref.dtype)

        @pl.when(i == 0)
        def _():
            for g in range(4):
                dpw_ref[g] = dpws[g]
            dps_ref[...] = dps

        @pl.when(i > 0)
        def _():
            for g in range(4):
                dpw_ref[g] += dpws[g]
            dps_ref[...] += dps

    row = lambda i: (i, 0)
    return pl.pallas_call(
        body, name=name, grid=(nt,),
        in_specs=[pl.BlockSpec(memory_space=pl.ANY),
                  pl.BlockSpec((tm, D), row), pl.BlockSpec((HALO, D), _next_halo(tm, T, 0)),
                  pl.BlockSpec((tm, D), row),
                  pl.BlockSpec((tm, D), lambda i: (i, ZB_POOL)), pl.BlockSpec((HALO, D), _prev_halo(tm, ZB_POOL)),
                  pl.BlockSpec((4, POOL_GD, POOL_GD), lambda i: (0, 0, 0)), pl.BlockSpec((1, D), lambda i: (0, 0))],
        out_specs=[pl.BlockSpec((tm, D), lambda i: (i, ZB_POOL)),
                   pl.BlockSpec((4, POOL_GD, POOL_GD), lambda i: (0, 0, 0)), pl.BlockSpec((1, D), lambda i: (0, 0))],
        out_shape=[jax.ShapeDtypeStruct(dz.shape, dz.dtype), jax.ShapeDtypeStruct((4, POOL_GD, POOL_GD), F32),
                   jax.ShapeDtypeStruct((1, D), F32)],
        input_output_aliases={0: 0}, compiler_params=_cp("arbitrary"))(dz, dyp, dyp, yp_pre, z, z, pw, ps)


def _fill_ext(ext_s, halo, main, keep):
    ext_s[0:HALO, :] = halo * keep
    ext_s[HALO:, :] = main


def _lru_gates(ext_s, tm, cw, cb, wr_ref, br, wi_ref, bi, lam):
    shifted = []
    v = cb
    for k in range(4):
        zs = ext_s[pl.ds(HALO - 3 + k, tm), :]
        shifted.append(zs)
        v = v + cw[k:k + 1, :] * zs
    vb = v.astype(BF)
    rp, ip = [], []
    for h in range(LRU_HEADS):
        sl = slice(h * LRU_HD, (h + 1) * LRU_HD)
        rp.append(jnp.dot(vb[:, sl], wr_ref[h], preferred_element_type=F32))
        ip.append(jnp.dot(vb[:, sl], wi_ref[h], preferred_element_type=F32))
    r = _sigmoid(jnp.concatenate(rp, axis=1) + br)
    ig = _sigmoid(jnp.concatenate(ip, axis=1) + bi)
    sp = jnp.maximum(-lam, 0.0) + jnp.log(1.0 + jnp.exp(-jnp.abs(lam)))
    a = jnp.exp(-LRU_C * r * sp)
    om = 1.0 - a * a
    rs = lax.rsqrt(om)
    return v, vb, r, ig, a, om, rs, sp, shifted


def lru_fwd(z, cw, cb, wr, br, wi, bi, lam, wlo, *, tm, name):
    T = z.shape[0]
    tm = min(tm, T)
    nch = tm // 8

    def body(zm_ref, zh_ref, cw_ref, cb_ref, wr_ref, br_ref, wi_ref, bi_ref, lam_ref, wlo_ref, h_ref, y_ref,
             a_s, b_s, carry, ext_s):
        i = pl.program_id(0)

        @pl.when(i == 0)
        def _():
            carry[...] = jnp.zeros_like(carry)

        keep = jnp.where(i == 0, 0.0, 1.0).astype(F32)
        _fill_ext(ext_s, zh_ref[...].astype(F32), zm_ref[...].astype(F32), keep)
        v, _, _, ig, a, om, rs, _, _ = _lru_gates(ext_s, tm, cw_ref[...], cb_ref[...], wr_ref, br_ref[...], wi_ref,
                                                  bi_ref[...], lam_ref[...])
        a_s[...] = a
        b_s[...] = jnp.where(om > 0.0, om * rs, 0.0) * (ig * v)
        row = lax.broadcasted_iota(jnp.int32, (8, D), 0)

        def step(ci, hprev):
            sl = pl.ds(pl.multiple_of(ci * 8, 8), 8)
            aa, bb = a_s[sl, :], b_s[sl, :]
            for s in (1, 2, 4):
                m = row >= s
                bb = bb + aa * jnp.where(m, pltpu.roll(bb, s, axis=0), 0.0)
                aa = aa * jnp.where(m, pltpu.roll(aa, s, axis=0), 1.0)
            h = bb + aa * hprev
            h_ref[sl, :] = h
            return jnp.broadcast_to(h[7:8, :], (8, D))

        carry[...] = lax.fori_loop(0, nch, step, carry[...])
        y_ref[...] = jnp.dot(h_ref[...].astype(BF), wlo_ref[...], preferred_element_type=F32).astype(BF)

    fix2 = lambda i: (0, 0)
    fix3 = lambda i: (0, 0, 0)
    return pl.pallas_call(
        body, name=name, grid=(T // tm,),
        in_specs=[pl.BlockSpec((tm, D), lambda i: (i, ZB_LRU)), pl.BlockSpec((HALO, D), _prev_halo(tm, ZB_LRU)),
                  pl.BlockSpec((4, D), fix2), pl.BlockSpec((1, D), fix2),
                  pl.BlockSpec((LRU_HEADS, LRU_HD, LRU_HD), fix3), pl.BlockSpec((1, D), fix2),
                  pl.BlockSpec((LRU_HEADS, LRU_HD, LRU_HD), fix3), pl.BlockSpec((1, D), fix2),
                  pl.BlockSpec((1, D), fix2), pl.BlockSpec((D, D), fix2)],
        out_specs=[pl.BlockSpec((tm, D), lambda i: (i, 0)), pl.BlockSpec((tm, D), lambda i: (i, 0))],
        out_shape=[jax.ShapeDtypeStruct((T, D), F32), jax.ShapeDtypeStruct((T, D), BF)],
        scratch_shapes=[pltpu.VMEM((tm, D), F32), pltpu.VMEM((tm, D), F32), pltpu.VMEM((8, D), F32),
                        pltpu.VMEM((tm + HALO, D), F32)],
        compiler_params=_cp("arbitrary"))(z, z, cw, cb, wr, br, wi, bi, lam, wlo)


def lru_bwd(dz, dyl, z, h, cw, cb, wr, br, wi, bi, lam, wlo, *, tm, name):
    T = z.shape[0]
    tm = min(tm, T)
    nt = T // tm
    nch = tm // 8

    def body(dz_in, dy_ref, zm_ref, zh_ref, h_ref, hh_ref, cw_ref, cb_ref, wr_ref, br_ref, wi_ref, bi_ref, lam_ref,
             wlo_ref, dz_ref, dcw_ref, dcb_ref, dwr_ref, dbr_ref, dwi_ref, dbi_ref, dlam_ref,
             c_s, g_s, dh_s, dh_carry, a_ext, dv_ext, ext_s, h_ext):
        del dz_in
        i = pl.program_id(0)
        ti = nt - 1 - i

        @pl.when(i == 0)
        def _():
            dh_carry[...] = jnp.zeros_like(dh_carry)
            a_ext[tm:, :] = jnp.zeros((8, D), F32)
            dv_ext[tm:, :] = jnp.zeros((HALO, D), F32)

        keep = jnp.where(ti == 0, 0.0, 1.0).astype(F32)
        _fill_ext(ext_s, zh_ref[...].astype(F32), zm_ref[...].astype(F32), keep)
        cw_ = cw_ref[...]
        lam_ = lam_ref[...]
        v, vb, r, ig, a, om, rs, sp, shifted = _lru_gates(ext_s, tm, cw_, cb_ref[...], wr_ref, br_ref[...], wi_ref,
                                                          bi_ref[...], lam_)
        mult = jnp.where(om > 0.0, om * rs, 0.0)
        a_ext[0:tm, :] = a
        c_s[...] = a_ext[pl.ds(1, tm), :]
        g_s[...] = lax.dot_general(dy_ref[...], wlo_ref[...], (((1,), (1,)), ((), ())), preferred_element_type=F32)
        row = lax.broadcasted_iota(jnp.int32, (8, D), 0)

        def step(k, nxt):
            ci = nch - 1 - k
            sl = pl.ds(pl.multiple_of(ci * 8, 8), 8)
            cc, gg = c_s[sl, :], g_s[sl, :]
            for s in (1, 2, 4):
                m = row < 8 - s
                gg = gg + cc * jnp.where(m, pltpu.roll(gg, 8 - s, axis=0), 0.0)
                cc = cc * jnp.where(m, pltpu.roll(cc, 8 - s, axis=0), 1.0)
            dh = gg + cc * nxt
            dh_s[sl, :] = dh
            return jnp.broadcast_to(dh[0:1, :], (8, D))

        dh_carry[...] = lax.fori_loop(0, nch, step, dh_carry[...])
        a_ext[tm:, :] = a[0:8, :]
        dh = dh_s[...]
        h_ext[0:8, :] = hh_ref[...] * keep
        h_ext[8:, :] = h_ref[...]
        hprev = h_ext[pl.ds(7, tm), :]
        iv = ig * v
        da = dh * hprev
        dmult = dh * iv
        div = dh * mult
        dlog = da * a - dmult * (a * a) * rs
        dr = dlog * (-LRU_C * sp)
        dlam = jnp.sum(dlog * r, axis=0, keepdims=True) * (LRU_C * _sigmoid(-lam_))
        di = div * v
        dv = div * ig
        drp = dr * r * (1.0 - r)
        dip = di * ig * (1.0 - ig)
        drb, dib = drp.astype(BF), dip.astype(BF)
        dvh, dwr, dwi = [], [], []
        nt_dims = (((1,), (1,)), ((), ()))
        tn_dims = (((0,), (0,)), ((), ()))
        for hd in range(LRU_HEADS):
            sl = slice(hd * LRU_HD, (hd + 1) * LRU_HD)
            dvh.append(lax.dot_general(drb[:, sl], wr_ref[hd], nt_dims, preferred_element_type=F32)
                       + lax.dot_general(dib[:, sl], wi_ref[hd], nt_dims, preferred_element_type=F32))
            dwr.append(lax.dot_general(vb[:, sl], drb[:, sl], tn_dims, preferred_element_type=F32))
            dwi.append(lax.dot_general(vb[:, sl], dib[:, sl], tn_dims, preferred_element_type=F32))
        dv = dv + jnp.concatenate(dvh, axis=1)
        dv_ext[0:tm, :] = dv
        dzl = cw_[3:4, :] * dv
        for k in range(3):
            dzl = dzl + cw_[k:k + 1, :] * dv_ext[pl.ds(3 - k, tm), :]
        dz_ref[...] = dzl.astype(dz_ref.dtype)
        dv_ext[tm:, :] = dv[:HALO]
        dcw = jnp.concatenate([jnp.sum(dv * shifted[k], axis=0, keepdims=True) for k in range(4)], axis=0)
        dcb = jnp.sum(dv, axis=0, keepdims=True)
        dbr = jnp.sum(drp, axis=0, keepdims=True)
        dbi = jnp.sum(dip, axis=0, keepdims=True)

        @pl.when(i == 0)
        def _():
            dcw_ref[...] = dcw
            dcb_ref[...] = dcb
            dbr_ref[...] = dbr
            dbi_ref[...] = dbi
            dlam_ref[...] = dlam
            for hd in range(LRU_HEADS):
                dwr_ref[hd] = dwr[hd]
                dwi_ref[hd] = dwi[hd]

        @pl.when(i > 0)
        def _():
            dcw_ref[...] += dcw
            dcb_ref[...] += dcb
            dbr_ref[...] += dbr
            dbi_ref[...] += dbi
            dlam_ref[...] += dlam
            for hd in range(LRU_HEADS):
                dwr_ref[hd] += dwr[hd]
                dwi_ref[hd] += dwi[hd]

    fix2 = lambda i: (0, 0)
    fix3 = lambda i: (0, 0, 0)
    rev = lambda i: (nt - 1 - i, 0)
    vec = pl.BlockSpec((1, D), fix2)
    hw = pl.BlockSpec((LRU_HEADS, LRU_HD, LRU_HD), fix3)
    return pl.pallas_call(
        body, name=name, grid=(nt,),
        in_specs=[pl.BlockSpec(memory_space=pl.ANY),
                  pl.BlockSpec((tm, D), rev),
                  pl.BlockSpec((tm, D), lambda i: (nt - 1 - i, ZB_LRU)),
                  pl.BlockSpec((HALO, D), lambda i: (jnp.maximum((nt - 1 - i) * (tm // HALO) - 1, 0), ZB_LRU)),
                  pl.BlockSpec((tm, D), rev),
                  pl.BlockSpec((8, D), lambda i: (jnp.maximum((nt - 1 - i) * (tm // 8) - 1, 0), 0)),
                  pl.BlockSpec((4, D), fix2), vec, hw, vec, hw, vec, vec, pl.BlockSpec((D, D), fix2)],
        out_specs=[pl.BlockSpec((tm, D), lambda i: (nt - 1 - i, ZB_LRU)),
                   pl.BlockSpec((4, D), fix2), vec, hw, vec, hw, vec, vec],
        out_shape=[jax.ShapeDtypeStruct(dz.shape, dz.dtype), jax.ShapeDtypeStruct((4, D), F32),
                   jax.ShapeDtypeStruct((1, D), F32), jax.ShapeDtypeStruct((LRU_HEADS, LRU_HD, LRU_HD), F32),
                   jax.ShapeDtypeStruct((1, D), F32), jax.ShapeDtypeStruct((LRU_HEADS, LRU_HD, LRU_HD), F32),
                   jax.ShapeDtypeStruct((1, D), F32), jax.ShapeDtypeStruct((1, D), F32)],
        scratch_shapes=[pltpu.VMEM((tm, D), F32), pltpu.VMEM((tm, D), F32), pltpu.VMEM((tm, D), F32),
                        pltpu.VMEM((8, D), F32), pltpu.VMEM((tm + 8, D), F32), pltpu.VMEM((tm + HALO, D), F32),
                        pltpu.VMEM((tm + HALO, D), F32), pltpu.VMEM((tm + 8, D), F32)],
        input_output_aliases={0: 0},
        compiler_params=_cp("arbitrary"))(dz, dyl, z, z, h, h, cw, cb, wr, br, wi, bi, lam, wlo)


def _sconv_cv(u_ext, sw):
    shifted = []
    cv = None
    for k in range(3):
        us = (u_ext if k == 2 else pltpu.roll(u_ext, 2 - k, axis=0))[HALO:]
        shifted.append(us)
        term = sw[k:k + 1, :] * us
        cv = term if cv is None else cv + term
    return cv, shifted


def sconv_fwd(z, sw, wso, *, tm, name):
    T = z.shape[0]
    tm = min(tm, T)

    def body(zm_ref, zh_ref, sw_ref, wso_ref, s_ref, y_ref):
        i = pl.program_id(0)
        keep = jnp.where(i == 0, 0.0, 1.0).astype(F32)
        zm = zm_ref[...].astype(F32)
        zh = zh_ref[...].astype(F32)
        u_ext = jnp.concatenate([zh[:, D:2 * D] * zh[:, 2 * D:] * keep, zm[:, D:2 * D] * zm[:, 2 * D:]], axis=0)
        cv, _ = _sconv_cv(u_ext, sw_ref[...])
        s = (zm[:, :D] * cv).astype(BF)
        s_ref[...] = s
        y_ref[...] = jnp.dot(s, wso_ref[...], preferred_element_type=F32).astype(BF)

    return pl.pallas_call(
        body, name=name, grid=(T // tm,),
        in_specs=[pl.BlockSpec((tm, 3 * D), lambda i: (i, ZB_SCONV)),
                  pl.BlockSpec((HALO, 3 * D), _prev_halo(tm, ZB_SCONV)),
                  pl.BlockSpec((3, D), lambda i: (0, 0)), pl.BlockSpec((D, D), lambda i: (0, 0))],
        out_specs=[pl.BlockSpec((tm, D), lambda i: (i, 0)), pl.BlockSpec((tm, D), lambda i: (i, 0))],
        out_shape=[jax.ShapeDtypeStruct((T, D), BF), jax.ShapeDtypeStruct((T, D), BF)],
        compiler_params=_cp("arbitrary"))(z, z, sw, wso)


def sconv_bwd(dz, dyc, z, sw, wso, *, tm, name):
    T = z.shape[0]
    tm = min(tm, T)
    nt = T // tm

    def body(dz_in, dy_ref, dyn_ref, zm_ref, zp_ref, zn_ref, sw_ref, wso_ref, dz_ref, dsw_ref):
        del dz_in
        i = pl.program_id(0)
        keep_p = jnp.where(i == 0, 0.0, 1.0).astype(F32)
        keep_n = jnp.where(i == nt - 1, 0.0, 1.0).astype(F32)
        sw_ = sw_ref[...]
        zm = zm_ref[...].astype(F32)
        zp = zp_ref[...].astype(F32)
        zb, zc, zh = zm[:, :D], zm[:, D:2 * D], zm[:, 2 * D:]
        u_ext = jnp.concatenate([zp[:, D:2 * D] * zp[:, 2 * D:] * keep_p, zc * zh], axis=0)
        cv, shifted = _sconv_cv(u_ext, sw_)
        dy_ext = jnp.concatenate([dy_ref[...], dyn_ref[...]], axis=0)
        ds_ext = lax.dot_general(dy_ext, wso_ref[...], (((1,), (1,)), ((), ())), preferred_element_type=F32)
        zb_ext = jnp.concatenate([zb, zn_ref[...][:, :D].astype(F32) * keep_n], axis=0)
        dcv_ext = ds_ext * zb_ext
        du = sw_[2:3, :] * dcv_ext[:tm]
        for k in range(2):
            du = du + sw_[k:k + 1, :] * pltpu.roll(dcv_ext, tm + HALO - (2 - k), axis=0)[:tm]
        dz_ref[...] = jnp.concatenate([ds_ext[:tm] * cv, du * zh, du * zc], axis=1).astype(dz_ref.dtype)
        dcv = dcv_ext[:tm]
        dsw = jnp.concatenate([jnp.sum(dcv * shifted[k], axis=0, keepdims=True) for k in range(3)], axis=0)

        @pl.when(i == 0)
        def _():
            dsw_ref[...] = dsw

        @pl.when(i > 0)
        def _():
            dsw_ref[...] += dsw

    return pl.pallas_call(
        body, name=name, grid=(nt,),
        in_specs=[pl.BlockSpec(memory_space=pl.ANY),
                  pl.BlockSpec((tm, D), lambda i: (i, 0)), pl.BlockSpec((HALO, D), _next_halo(tm, T, 0)),
                  pl.BlockSpec((tm, 3 * D), lambda i: (i, ZB_SCONV)),
                  pl.BlockSpec((HALO, 3 * D), _prev_halo(tm, ZB_SCONV)),
                  pl.BlockSpec((HALO, 3 * D), _next_halo(tm, T, ZB_SCONV)),
                  pl.BlockSpec((3, D), lambda i: (0, 0)), pl.BlockSpec((D, D), lambda i: (0, 0))],
        out_specs=[pl.BlockSpec((tm, 3 * D), lambda i: (i, ZB_SCONV)), pl.BlockSpec((3, D), lambda i: (0, 0))],
        out_shape=[jax.ShapeDtypeStruct(dz.shape, dz.dtype), jax.ShapeDtypeStruct((3, D), F32)],
        input_output_aliases={0: 0}, compiler_params=_cp("arbitrary"))(dz, dyc, dyc, z, z, z, sw, wso)


def merge_fwd(z, yp_pre, yl, yc, ps, *, tm, name):
    T = z.shape[0]
    tm = min(tm, T)

    def body(zg_ref, yp_ref, yl_ref, yc_ref, ps_ref, o_ref):
        gts = _sigmoid(zg_ref[...].astype(F32))
        m = (gts[:, :D] * (yp_ref[...].astype(F32) * ps_ref[...]) + gts[:, D:2 * D] * yl_ref[...].astype(F32)
             + gts[:, 2 * D:] * yc_ref[...].astype(F32))
        o_ref[...] = m.astype(o_ref.dtype)

    row = lambda i: (i, 0)
    return pl.pallas_call(
        body, name=name, grid=(T // tm,),
        in_specs=[pl.BlockSpec((tm, 3 * D), lambda i: (i, ZB_GATE)), pl.BlockSpec((tm, D), row),
                  pl.BlockSpec((tm, D), row), pl.BlockSpec((tm, D), row), pl.BlockSpec((1, D), lambda i: (0, 0))],
        out_specs=pl.BlockSpec((tm, D), row), out_shape=jax.ShapeDtypeStruct((T, D), BF),
        compiler_params=_cp("arbitrary"))(z, yp_pre, yl, yc, ps)


def merge_bwd(dm, z, yp_pre, yl, yc, ps, *, tm, name):
    T = z.shape[0]
    tm = min(tm, T)

    def body(dm_ref, zg_ref, yp_ref, yl_ref, yc_ref, ps_ref, dz_ref, dyp_ref, dyl_ref, dyc_ref):
        gts = _sigmoid(zg_ref[...].astype(F32))
        dmv = dm_ref[...].astype(F32)
        ys = (yp_ref[...].astype(F32) * ps_ref[...], yl_ref[...].astype(F32), yc_ref[...].astype(F32))
        outs = (dyp_ref, dyl_ref, dyc_ref)
        dgs = []
        for j in range(3):
            gj = gts[:, j * D:(j + 1) * D]
            outs[j][...] = (dmv * gj).astype(BF)
            dgs.append(dmv * ys[j] * gj * (1.0 - gj))
        dz_ref[...] = jnp.concatenate(dgs, axis=1).astype(dz_ref.dtype)

    row = lambda i: (i, 0)
    return pl.pallas_call(
        body, name=name, grid=(T // tm,),
        in_specs=[pl.BlockSpec((tm, D), row), pl.BlockSpec((tm, 3 * D), lambda i: (i, ZB_GATE)),
                  pl.BlockSpec((tm, D), row), pl.BlockSpec((tm, D), row), pl.BlockSpec((tm, D), row),
                  pl.BlockSpec((1, D), lambda i: (0, 0))],
        out_specs=[pl.BlockSpec((tm, 3 * D), lambda i: (i, ZB_GATE)), pl.BlockSpec((tm, D), row),
                   pl.BlockSpec((tm, D), row), pl.BlockSpec((tm, D), row)],
        out_shape=[jax.ShapeDtypeStruct((T, IN_COLS), BF), jax.ShapeDtypeStruct((T, D), BF),
                   jax.ShapeDtypeStruct((T, D), BF), jax.ShapeDtypeStruct((T, D), BF)],
        compiler_params=_cp("arbitrary"))(dm, z, yp_pre, yl, yc, ps)


def _attn_probs(qh, kh):
    s = lax.dot_general(qh, kh, (((1,), (1,)), ((), ())), preferred_element_type=F32) * (X_HD ** -0.5)
    e = jnp.exp(s - jnp.max(s, axis=-1, keepdims=True))
    return e / jnp.sum(e, axis=-1, keepdims=True)


def attn_fwd(xb, wq, kb, vb, *, tm, name):
    T = xb.shape[0]
    tm = min(tm, T)

    def body(x_ref, wq_ref, k_ref, v_ref, q_ref, o_ref):
        q = jnp.dot(x_ref[...], wq_ref[...], preferred_element_type=F32).astype(BF)
        q_ref[...] = q
        outs = []
        for h in range(X_HEADS):
            sl = slice(h * X_HD, (h + 1) * X_HD)
            p = _attn_probs(q[:, sl], k_ref[:, sl])
            outs.append(jnp.dot(p.astype(BF), v_ref[:, sl], preferred_element_type=F32))
        o_ref[...] = jnp.concatenate(outs, axis=1).astype(BF)

    row = lambda i: (i, 0)
    fix = lambda i: (0, 0)
    return pl.pallas_call(
        body, name=name, grid=(T // tm,),
        in_specs=[pl.BlockSpec((tm, D), row), pl.BlockSpec((D, D), fix), pl.BlockSpec((N_MEM, D), fix),
                  pl.BlockSpec((N_MEM, D), fix)],
        out_specs=[pl.BlockSpec((tm, D), row), pl.BlockSpec((tm, D), row)],
        out_shape=[jax.ShapeDtypeStruct((T, D), BF), jax.ShapeDtypeStruct((T, D), BF)],
        compiler_params=_cp("arbitrary"))(xb, wq, kb, vb)


def attn_bwd(dxa, wo, q, kb, vb, *, tm, name):
    T = q.shape[0]
    tm = min(tm, T)

    def body(d_ref, wo_ref, q_ref, k_ref, v_ref, dq_ref, dk_ref, dv_ref):
        i = pl.program_id(0)
        do = lax.dot_general(d_ref[...], wo_ref[...], (((1,), (1,)), ((), ())),
                             preferred_element_type=F32).astype(BF)
        q = q_ref[...]
        dqs, dks, dvs = [], [], []
        for h in range(X_HEADS):
            sl = slice(h * X_HD, (h + 1) * X_HD)
            kh, vh = k_ref[:, sl], v_ref[:, sl]
            p = _attn_probs(q[:, sl], kh)
            dp = lax.dot_general(do[:, sl], vh, (((1,), (1,)), ((), ())), preferred_element_type=F32)
            ds = (p * (dp - jnp.sum(dp * p, axis=-1, keepdims=True)) * (X_HD ** -0.5)).astype(BF)
            dqs.append(jnp.dot(ds, kh, preferred_element_type=F32))
            dks.append(lax.dot_general(ds, q[:, sl], (((0,), (0,)), ((), ())), preferred_element_type=F32))
            dvs.append(lax.dot_general(p.astype(BF), do[:, sl], (((0,), (0,)), ((), ())),
                                       preferred_element_type=F32))
        dq_ref[...] = jnp.concatenate(dqs, axis=1).astype(BF)
        dk = jnp.concatenate(dks, axis=1)
        dv = jnp.concatenate(dvs, axis=1)

        @pl.when(i == 0)
        def _():
            dk_ref[...] = dk
            dv_ref[...] = dv

        @pl.when(i > 0)
        def _():
            dk_ref[...] += dk
            dv_ref[...] += dv

    row = lambda i: (i, 0)
    fix = lambda i: (0, 0)
    return pl.pallas_call(
        body, name=name, grid=(T // tm,),
        in_specs=[pl.BlockSpec((tm, D), row), pl.BlockSpec((D, D), fix), pl.BlockSpec((tm, D), row),
                  pl.BlockSpec((N_MEM, D), fix), pl.BlockSpec((N_MEM, D), fix)],
        out_specs=[pl.BlockSpec((tm, D), row), pl.BlockSpec((N_MEM, D), fix), pl.BlockSpec((N_MEM, D), fix)],
        out_shape=[jax.ShapeDtypeStruct((T, D), BF), jax.ShapeDtypeStruct((N_MEM, D), F32),
                   jax.ShapeDtypeStruct((N_MEM, D), F32)],
        compiler_params=_cp("arbitrary"))(dxa, wo, q, kb, vb)


def swiglu_fwd(gu, *, tm, name):
    T = gu.shape[0]
    tm = min(tm, T)

    def body(g_ref, u_ref, o_ref):
        g = g_ref[...].astype(F32)
        o_ref[...] = (g * _sigmoid(g) * u_ref[...].astype(F32)).astype(BF)

    return pl.pallas_call(
        body, name=name, grid=(T // tm,),
        in_specs=[pl.BlockSpec((tm, D_FF), lambda i: (i, 0)), pl.BlockSpec((tm, D_FF), lambda i: (i, 1))],
        out_specs=pl.BlockSpec((tm, D_FF), lambda i: (i, 0)), out_shape=jax.ShapeDtypeStruct((T, D_FF), BF),
        compiler_params=_cp("arbitrary"))(gu, gu)


def swiglu_bwd(dh, gu, *, tm, name):
    T = gu.shape[0]
    tm = min(tm, T)

    def body(dh_ref, g_ref, u_ref, o_ref):
        g = g_ref[...].astype(F32)
        u = u_ref[...].astype(F32)
        dhv = dh_ref[...].astype(F32)
        sg = _sigmoid(g)
        o_ref[:, :D_FF] = (dhv * u * sg * (1.0 + g * (1.0 - sg))).astype(BF)
        o_ref[:, D_FF:] = (dhv * g * sg).astype(BF)

    return pl.pallas_call(
        body, name=name, grid=(T // tm,),
        in_specs=[pl.BlockSpec((tm, D_FF), lambda i: (i, 0)), pl.BlockSpec((tm, D_FF), lambda i: (i, 0)),
                  pl.BlockSpec((tm, D_FF), lambda i: (i, 1))],
        out_specs=pl.BlockSpec((tm, 2 * D_FF), lambda i: (i, 0)),
        out_shape=jax.ShapeDtypeStruct((T, 2 * D_FF), BF), compiler_params=_cp("arbitrary"))(dh, gu, gu)


TM_MM = 1024
TM_EW = 512
TM_SEQ = 512
TT_DW = 2048


def _mem_kv(l, memb, W):
    kb = mm_nn(memb, W['xa_w_k'][l], None, out_dtype=BF, tm=N_MEM, tn=1024, name=f"l{l}_mem_k")
    vb = mm_nn(memb, W['xa_w_v'][l], None, out_dtype=BF, tm=N_MEM, tn=1024, name=f"l{l}_mem_v")
    return kb, vb


def _layer_fwd(l, x, xb, kb, vb, W, host=None, host2=None, after_in_proj=None):
    n = f"l{l}_"
    sv = {'x0': x if xb is None else xb}
    z = mm_nn(sv['x0'], W['w_in'][l], W['b_in'][l], out_dtype=BF, tm=TM_MM, tn=1024, name=n + "in_proj", host=host)
    if host is not None:
        z, sv['hosted'] = z
        if after_in_proj is not None:
            after_in_proj(sv['hosted'])
    if kb is None:
        kb, vb = _mem_kv(l, vb, W)
    sv['kb'], sv['vb'] = kb, vb
    yp = pool_fwd(z, W['pool_w'][l], tm=TM_SEQ, name=n + "pool_fwd")
    h, yl = lru_fwd(z, W['lru_conv_w'][l], W['lru_conv_b'][l], W['lru_w_r'][l], W['lru_b_r'][l], W['lru_w_i'][l],
                    W['lru_b_i'][l], W['lru_lambda'][l], W['lru_w_out'][l], tm=TM_SEQ, name=n + "lru_fwd")
    s, yc = sconv_fwd(z, W['sconv_w'][l], W['sconv_w_out'][l], tm=TM_SEQ, name=n + "sconv_fwd")
    merged = merge_fwd(z, yp, yl, yc, W['pool_scale'][l], tm=TM_EW, name=n + "merge_fwd")
    x1, x1b, xh1, rs1 = mm_res_ln(merged, W['w_mix_out'][l], x, W['ln_g'][l][0:1], W['ln_b'][l][0:1], tm=TM_EW,
                                  name=n + "mix_out_ln")
    q, o = attn_fwd(x1b, W['xa_w_q'][l], kb, vb, tm=TM_EW, name=n + "attn_fwd")
    x2, x2b, xh2, rs2 = mm_res_ln(o, W['xa_w_o'][l], x1, W['ln_g'][l][1:2], W['ln_b'][l][1:2], tm=TM_EW,
                                  name=n + "attn_out_ln")
    gu = mm_nn(x2b, W['ffn_w_gu'][l], None, out_dtype=BF, tm=TM_MM, tn=1408, name=n + "ffn_in", host=host2)
    if host2 is not None:
        gu, sv['hosted2'] = gu
    hdn = swiglu_fwd(gu, tm=TM_EW, name=n + "swiglu_fwd")
    x3, x3b, xh3, rs3 = mm_res_ln(hdn, W['ffn_w_down'][l], x2, W['ln_g'][l][2:3], W['ln_b'][l][2:3], tm=TM_EW,
                                  name=n + "ffn_out_ln")
    sv.update(z=z, yp=yp, h=h, yl=yl, s=s, yc=yc, merged=merged, x1b=x1b, xh1=xh1, rs1=rs1, q=q, o=o, x2b=x2b,
              xh2=xh2, rs2=rs2, gu=gu, hdn=hdn, xh3=xh3, rs3=rs3)
    return x3, x3b, sv


def _layer_bwd(l, dx3, sv, memb, kb, vb, W, loss_from=None, host=None, last_host_fn=None):
    n = f"l{l}_"
    G = {}
    res = ln_bwd(dx3, sv['xh3'], sv['rs3'], W['ln_g'][l][2:3], tm=TM_EW, name=n + "ln3_bwd", loss_from=loss_from)
    dp3, dp3b, dg3, db3 = res[:4]
    loss = res[4] if loss_from is not None else None
    dhdn = mm_nt(dp3b, W['ffn_w_down'][l], None, out_dtype=BF, tm=TM_EW, tc=D, name=n + "ffn_down_dx")
    dgu = swiglu_bwd(dhdn, sv['gu'], tm=TM_EW, name=n + "swiglu_bwd")
    dx2 = mm_nt(dgu, W['ffn_w_gu'][l], dp3, out_dtype=F32, tm=TM_MM, tc=1408, name=n + "ffn_in_dx", host=host)
    if host is not None:
        dx2, G['hosted'] = dx2
    G['ffn_w_gu'] = mm_tn(sv['x2b'], dgu, out_dtype=BF, tk=1024, tn=1408, tt=TT_DW,name=n + "ffn_in_dw")
    G['ffn_w_down'] = mm_tn(sv['hdn'], dp3b, out_dtype=BF, tk=1408, tn=1024, tt=TT_DW,name=n + "ffn_down_dw")

    dp2, dp2b, dg2, db2 = ln_bwd(dx2, sv['xh2'], sv['rs2'], W['ln_g'][l][1:2], tm=TM_EW, name=n + "ln2_bwd")
    dq, dk, dv = attn_bwd(dp2b, W['xa_w_o'][l], sv['q'], kb, vb, tm=TM_EW, name=n + "attn_bwd")
    dx1 = mm_nt(dq, W['xa_w_q'][l], dp2, out_dtype=F32, tm=TM_MM, tc=D, name=n + "attn_q_dx")
    G['xa_w_o'] = mm_tn(sv['o'], dp2b, out_dtype=BF, tk=1024, tn=1024, tt=TT_DW,name=n + "attn_o_dw")
    G['xa_w_q'] = mm_tn(sv['x1b'], dq, out_dtype=BF, tk=1024, tn=1024, tt=TT_DW,name=n + "attn_q_dw")
    G['xa_w_k'] = mm_tn(memb, dk, out_dtype=BF, tk=1024, tn=1024, tt=N_MEM, name=n + "attn_k_dw")
    G['xa_w_v'] = mm_tn(memb, dv, out_dtype=BF, tk=1024, tn=1024, tt=N_MEM, name=n + "attn_v_dw")

    dp1, dp1b, dg1, db1 = ln_bwd(dx1, sv['xh1'], sv['rs1'], W['ln_g'][l][0:1], tm=TM_EW, name=n + "ln1_bwd")
    dmerged = mm_nt(dp1b, W['w_mix_out'][l], None, out_dtype=BF, tm=TM_MM, tc=D, name=n + "mix_out_dx")
    G['w_mix_out'] = mm_tn(sv['merged'], dp1b, out_dtype=BF, tk=1024, tn=1024, tt=TT_DW,name=n + "mix_out_dw")
    z = sv['z']
    dz, dyp, dyl, dyc = merge_bwd(dmerged, z, sv['yp'], sv['yl'], sv['yc'], W['pool_scale'][l], tm=TM_EW,
                                  name=n + "merge_bwd")
    dz, G['pool_w'], G['pool_scale'] = pool_bwd(dz, dyp, sv['yp'], z, W['pool_w'][l], W['pool_scale'][l],
                                                tm=TM_SEQ, name=n + "pool_bwd")
    (dz, G['lru_conv_w'], G['lru_conv_b'], G['lru_w_r'], G['lru_b_r'], G['lru_w_i'], G['lru_b_i'],
     G['lru_lambda']) = lru_bwd(dz, dyl, z, sv['h'], W['lru_conv_w'][l], W['lru_conv_b'][l], W['lru_w_r'][l],
                                W['lru_b_r'][l], W['lru_w_i'][l], W['lru_b_i'][l], W['lru_lambda'][l],
                                W['lru_w_out'][l], tm=TM_SEQ, name=n + "lru_bwd")
    dz, G['sconv_w'] = sconv_bwd(dz, dyc, z, W['sconv_w'][l], W['sconv_w_out'][l], tm=TM_SEQ, name=n + "sconv_bwd")
    G['lru_w_out'] = mm_tn(sv['h'], dyl, out_dtype=BF, tk=1024, tn=1024, tt=TT_DW,name=n + "lru_out_dw")
    G['sconv_w_out'] = mm_tn(sv['s'], dyc, out_dtype=BF, tk=1024, tn=1024, tt=TT_DW,name=n + "sconv_out_dw")
    G['w_in'], G['b_in'] = mm_tn(sv['x0'], dz, out_dtype=BF, tk=1024, tn=1024, tt=TT_DW,name=n + "in_proj_dw",
                                 colsum=True)
    G['ln_g'] = jnp.concatenate([dg1, dg2, dg3], axis=0)
    G['ln_b'] = jnp.concatenate([db1, db2, db3], axis=0)
    last_host = None if last_host_fn is None else last_host_fn(G)
    dx0 = mm_nt(dz, W['w_in'][l], dp1, out_dtype=F32, tm=TM_MM, tc=2048, name=n + "in_proj_dx", host=last_host)
    if last_host is not None:
        dx0, G['hosted_last'] = dx0
    return dx0, G, loss


def local_step(x, mem, target, W):
    memb = mem.astype(BF)
    saves, kvs = [], []
    xf, xb = x, None
    for l in range(DEPTH):
        kb = mm_nn(memb, W['xa_w_k'][l], None, out_dtype=BF, tm=N_MEM, tn=1024, name=f"l{l}_mem_k")
        vb = mm_nn(memb, W['xa_w_v'][l], None, out_dtype=BF, tm=N_MEM, tn=1024, name=f"l{l}_mem_v")
        xf, xb, sv = _layer_fwd(l, xf, xb, kb, vb, W)
        saves.append(sv)
        kvs.append((kb, vb))
    grads = [None] * DEPTH
    dx, loss = None, None
    for l in reversed(range(DEPTH)):
        lf = (W['ln_b'][l][2:3], target) if l == DEPTH - 1 else None
        dx, grads[l], ls = _layer_bwd(l, dx, saves[l], memb, kvs[l][0], kvs[l][1], W, loss_from=lf)
        if ls is not None:
            loss = ls
    return loss, dx, grads


def _coords():
    return lax.axis_index("x"), lax.axis_index("y"), lax.axis_index("c")


FLIPS = ((1, 0), (0, 1), (1, 1))
SQUARES = ('lru_w_out', 'sconv_w_out', 'w_mix_out', 'xa_w_q', 'xa_w_k', 'xa_w_v', 'xa_w_o')
LAYER_SHAPE = {'w_in': (D, IN_COLS), 'pool_w': (4, POOL_GD, POOL_GD), 'ffn_w_gate': (4, D, D_FF // 4),
               'ffn_w_up': (4, D, D_FF // 4), 'ffn_w_down': (D_FF, D), **{n: (D, D) for n in SQUARES}}
PIECES = ('w_in', 'pool_w') + SQUARES + ('ffn_w_gate', 'ffn_w_up', 'ffn_w_down')


def _mult(v, m):
    return v if isinstance(v, int) else pl.multiple_of(v, m)


def _win(name, ref, k):
    if name == 'w_in':
        return ref.at[:, pl.ds(_mult(((2 * k + 6) % 8) * D, D), 2 * D)]
    if name == 'pool_w':
        return ref.at[:, pl.ds(_mult(k * (POOL_GD // 4), POOL_GD // 4), POOL_GD // 4), :]
    if name in ('ffn_w_gate', 'ffn_w_up'):
        return ref.at[k]
    rows = LAYER_SHAPE[name][0] // 4
    return ref.at[pl.ds(_mult(k * rows, 16), rows), :]


def _half_shape(name):
    shard = _shard_shape(name)
    return (shard[0] // 2,) + shard[1:]


def _half(name, ref, h):
    rows = _shard_shape(name)[0] // 2
    if name == 'pool_w':
        return ref.at[pl.ds(h * rows, rows)]
    return ref.at[pl.ds(_mult(h * rows, 16), rows), :]


def gather_weights(shards, small):
    n_p = len(PIECES)

    def body(*refs):
        srcs = dict(zip(PIECES, refs[:n_p]))
        small_ref = refs[n_p]
        outs = [dict(zip(PIECES, refs[n_p + 1 + l * n_p:n_p + 1 + (l + 1) * n_p])) for l in range(DEPTH)]
        gs_ref = refs[n_p + 1 + DEPTH * n_p]
        ici_send, ici_recv, d2d_send, d2d_recv, own_send, own_recv = refs[n_p + 2 + DEPTH * n_p:]
        x, y, c = _coords()
        me = 2 * x + y
        sib = (x, y, 1 - c)

        def own_copies():
            cps = []
            li = 0
            for n in PIECES:
                for l in range(DEPTH):
                    cps.append(pltpu.make_async_remote_copy(
                        src_ref=srcs[n].at[l], dst_ref=_win(n, outs[l][n], me), send_sem=own_send.at[li],
                        recv_sem=own_recv.at[li], device_id=sib, device_id_type=MESH))
                    li += 1
            cps.append(pltpu.make_async_remote_copy(
                src_ref=small_ref, dst_ref=gs_ref.at[me], send_sem=own_send.at[li], recv_sem=own_recv.at[li],
                device_id=sib, device_id_type=MESH))
            return cps

        def run(lc):
            sends = []
            for j, (fx, fy) in enumerate(FLIPS):
                peer = (x ^ fx, y ^ fy, c)
                for p, n in enumerate(PIECES):
                    k = 3 * p + j
                    sends.append(pltpu.make_async_remote_copy(
                        src_ref=srcs[n].at[lc], dst_ref=_win(n, outs[lc][n], me), send_sem=ici_send.at[k],
                        recv_sem=ici_recv.at[k], device_id=peer, device_id_type=MESH))
                k = 3 * n_p + j
                sends.append(pltpu.make_async_remote_copy(
                    src_ref=small_ref, dst_ref=gs_ref.at[me], send_sem=ici_send.at[k], recv_sem=ici_recv.at[k],
                    device_id=peer, device_id_type=MESH))
            own = own_copies()
            for cp in sends + own:
                cp.start()
            for j, (fx, fy) in enumerate(FLIPS):
                other = 2 * (x ^ fx) + (y ^ fy)
                for p, n in enumerate(PIECES):
                    k = 3 * p + j
                    w = _win(n, outs[lc][n], other)
                    pltpu.make_async_remote_copy(src_ref=srcs[n].at[lc], dst_ref=w, send_sem=ici_send.at[k],
                                                 recv_sem=ici_recv.at[k], device_id=sib,
                                                 device_id_type=MESH).wait_recv()
                    fw = pltpu.make_async_remote_copy(src_ref=w, dst_ref=w, send_sem=d2d_send.at[k],
                                                      recv_sem=d2d_recv.at[k], device_id=sib, device_id_type=MESH)
                    fw.start()
                    sends.append(fw)
                k = 3 * n_p + j
                pltpu.make_async_remote_copy(src_ref=small_ref, dst_ref=gs_ref.at[other], send_sem=ici_send.at[k],
                                             recv_sem=ici_recv.at[k], device_id=sib, device_id_type=MESH).wait_recv()
            for j, (fx, fy) in enumerate(FLIPS):
                other = 2 * (x ^ fx) + (y ^ fy)
                for p, n in enumerate(PIECES):
                    k = 3 * p + j
                    w = _win(n, outs[1 - lc][n], other)
                    pltpu.make_async_remote_copy(src_ref=w, dst_ref=w, send_sem=d2d_send.at[k],
                                                 recv_sem=d2d_recv.at[k], device_id=sib,
                                                 device_id_type=MESH).wait_recv()
            for cp in sends:
                cp.wait_send()
            for cp in own:
                cp.wait()

        @pl.when(c == 0)
        def _():
            run(0)

        @pl.when(c == 1)
        def _():
            run(1)

    hbm = pl.BlockSpec(memory_space=pl.ANY)
    n_out = DEPTH * n_p + 1
    res = pl.pallas_call(
        body, name="gather_weights", in_specs=[hbm] * (n_p + 1), out_specs=[hbm] * n_out,
        out_shape=[jax.ShapeDtypeStruct(LAYER_SHAPE[n], BF) for _ in range(DEPTH) for n in PIECES]
        + [jax.ShapeDtypeStruct((4,) + small.shape, small.dtype)],
        scratch_shapes=[pltpu.SemaphoreType.DMA((3 * n_p + 3,)), pltpu.SemaphoreType.DMA((3 * n_p + 3,)),
                        pltpu.SemaphoreType.DMA((3 * n_p,)), pltpu.SemaphoreType.DMA((3 * n_p,)),
                        pltpu.SemaphoreType.DMA((DEPTH * n_p + 1,)), pltpu.SemaphoreType.DMA((DEPTH * n_p + 1,))],
    )(*[shards[n] for n in PIECES], small)
    full = {n: [res[l * n_p + p] for l in range(DEPTH)] for p, n in enumerate(PIECES)}
    return full, res[DEPTH * n_p]


def split_layers(p0, p1):
    n = len(p0)

    def body(*refs):
        a0, a1 = refs[:n], refs[n:2 * n]
        theirs = refs[2 * n:3 * n]
        send_sems, recv_sems = refs[3 * n:]
        x, y, c = _coords()
        sib = (x, y, 1 - c)

        def give(arrs):
            for i in range(n):
                pltpu.make_async_remote_copy(src_ref=arrs[i], dst_ref=theirs[i], send_sem=send_sems.at[i],
                                             recv_sem=recv_sems.at[i], device_id=sib, device_id_type=MESH).start()

        @pl.when(c == 0)
        def _():
            give(a1)

        @pl.when(c == 1)
        def _():
            give(a0)

        for i in range(n):
            pltpu.make_async_remote_copy(src_ref=a0[i], dst_ref=theirs[i], send_sem=send_sems.at[i],
                                         recv_sem=recv_sems.at[i], device_id=sib, device_id_type=MESH).wait()

    hbm = pl.BlockSpec(memory_space=pl.ANY)
    return pl.pallas_call(
        body, name="split_layers", in_specs=[hbm] * (2 * n), out_specs=[hbm] * n,
        out_shape=[jax.ShapeDtypeStruct(a.shape, a.dtype) for a in p0],
        scratch_shapes=[pltpu.SemaphoreType.DMA((n,)), pltpu.SemaphoreType.DMA((n,))],
    )(*p0, *p1)


def exchange_chips(q, qsmall):
    n_p = len(PIECES)

    def body(*refs):
        srcs = dict(zip(PIECES, refs[:n_p]))
        s_ref = refs[n_p]
        outs = dict(zip(PIECES, refs[n_p + 1:2 * n_p + 1]))
        so_ref = refs[2 * n_p + 1]
        send_sems, recv_sems = refs[2 * n_p + 2:]
        x, y, c = _coords()
        me = 2 * x + y
        sends = []
        for j, (fx, fy) in enumerate(FLIPS):
            peer = (x ^ fx, y ^ fy, c)
            other = 2 * (x ^ fx) + (y ^ fy)
            for p, n in enumerate(PIECES):
                k = 3 * p + j
                sends.append(pltpu.make_async_remote_copy(
                    src_ref=_win(n, srcs[n], other), dst_ref=outs[n].at[j], send_sem=send_sems.at[k],
                    recv_sem=recv_sems.at[k], device_id=peer, device_id_type=MESH))
            k = 3 * n_p + j
            sends.append(pltpu.make_async_remote_copy(
                src_ref=s_ref, dst_ref=so_ref.at[j], send_sem=send_sems.at[k], recv_sem=recv_sems.at[k],
                device_id=peer, device_id_type=MESH))
        for cp in sends:
            cp.start()
        for cp in sends:
            cp.wait()

    hbm = pl.BlockSpec(memory_space=pl.ANY)
    res = pl.pallas_call(
        body, name="exchange_chips", in_specs=[hbm] * (n_p + 1), out_specs=[hbm] * (n_p + 1),
        out_shape=[jax.ShapeDtypeStruct((3,) + _shard_shape(n), BF) for n in PIECES]
        + [jax.ShapeDtypeStruct((3,) + qsmall.shape, qsmall.dtype)],
        scratch_shapes=[pltpu.SemaphoreType.DMA((3 * n_p + 3,)), pltpu.SemaphoreType.DMA((3 * n_p + 3,))],
    )(*[q[n] for n in PIECES], qsmall)
    return dict(zip(PIECES, res[:n_p])), res[n_p]


def _shard_shape(n):
    shp = LAYER_SHAPE[n]
    if n == 'w_in':
        return (shp[0], shp[1] // 4)
    if n == 'pool_w':
        return (shp[0], shp[1] // 4, shp[2])
    if n in ('ffn_w_gate', 'ffn_w_up'):
        return shp[1:]
    return (shp[0] // 4, shp[1])


def swap_cores(s):
    n = len(s)

    def body(*refs):
        srcs, outs = refs[:n], refs[n:2 * n]
        send_sems, recv_sems = refs[2 * n:]
        x, y, c = _coords()
        cps = [pltpu.make_async_remote_copy(src_ref=srcs[i], dst_ref=outs[i], send_sem=send_sems.at[i],
                                            recv_sem=recv_sems.at[i], device_id=(x, y, 1 - c), device_id_type=MESH)
               for i in range(n)]
        for cp in cps:
            cp.start()
        for cp in cps:
            cp.wait()

    hbm = pl.BlockSpec(memory_space=pl.ANY)
    return pl.pallas_call(
        body, name="swap_cores", in_specs=[hbm] * n, out_specs=[hbm] * n,
        out_shape=[jax.ShapeDtypeStruct(a.shape, a.dtype) for a in s],
        scratch_shapes=[pltpu.SemaphoreType.DMA((n,)), pltpu.SemaphoreType.DMA((n,))],
    )(*s)


def add_cores(p0, p1, theirs, *, out_dtype, name):
    shp = p0.shape
    args = [t.reshape(-1, shp[-1]) for t in (p0, p1, theirs)]
    R, C = args[0].shape
    tr = _row_tile(R, C)

    def body(a0_ref, a1_ref, t_ref, o_ref):
        c = lax.axis_index("c")
        t = t_ref[...].astype(F32)

        @pl.when(c == 0)
        def _():
            o_ref[...] = (a0_ref[...].astype(F32) + t).astype(o_ref.dtype)

        @pl.when(c == 1)
        def _():
            o_ref[...] = (a1_ref[...].astype(F32) + t).astype(o_ref.dtype)

    spec = pl.BlockSpec((tr, C), lambda i: (i, 0))
    out = pl.pallas_call(body, name=name, grid=(R // tr,), in_specs=[spec] * 3, out_specs=spec,
                         out_shape=jax.ShapeDtypeStruct((R, C), out_dtype), compiler_params=_cp("arbitrary"))(*args)
    return out.reshape(shp)


def sum_chips(name, q, recv3, chip):
    shard = _shard_shape(name)
    zero = (0,) * len(shard)
    if name == 'w_in':
        tr = 128
        grid = (shard[0] // tr,)
        qspec = pl.BlockSpec((tr, shard[1]), lambda i, me: (i, (me[0] + 3) % 4))
        rspec = pl.BlockSpec((3, tr, shard[1]), lambda i, me: (0, i, 0))
        ospec = pl.BlockSpec((tr, shard[1]), lambda i, me: (i, 0))
    else:
        grid = (1,)
        rspec = pl.BlockSpec((3,) + shard, lambda i, me: (0,) + zero)
        ospec = pl.BlockSpec(shard, lambda i, me: zero)
        if name == 'pool_w':
            qspec = pl.BlockSpec(shard, lambda i, me: (0, me[0], 0))
        elif name in ('ffn_w_gate', 'ffn_w_up'):
            qspec = pl.BlockSpec((None,) + shard, lambda i, me: (me[0], 0, 0))
        else:
            qspec = pl.BlockSpec(shard, lambda i, me: (me[0], 0))

    def body(me_ref, q_ref, r_ref, o_ref):
        del me_ref
        acc = q_ref[...].astype(F32)
        for j in range(3):
            acc = acc + r_ref[j].astype(F32)
        o_ref[...] = acc

    return pl.pallas_call(
        body, name="sum_chips_" + name,
        grid_spec=pltpu.PrefetchScalarGridSpec(num_scalar_prefetch=1, grid=grid, in_specs=[qspec, rspec],
                                               out_specs=ospec),
        out_shape=jax.ShapeDtypeStruct(shard, F32), compiler_params=_cp("arbitrary"))(chip, q, recv3)


def sum_chips_small(q, recv3, chip):
    r, C = q.shape
    slot_of_xor = {2: 0, 1: 1, 3: 2}

    def body(me_ref, q_ref, r_ref, o_ref):
        me = me_ref[0]
        acc = None
        for k in range(4):
            kx = k ^ me
            term = q_ref[...]
            for xv, j in slot_of_xor.items():
                term = jnp.where(kx == xv, r_ref[j], term)
            acc = term if acc is None else acc + term
        o_ref[...] = acc

    return pl.pallas_call(
        body, name="sum_chips_small",
        grid_spec=pltpu.PrefetchScalarGridSpec(
            num_scalar_prefetch=1, grid=(1,),
            in_specs=[pl.BlockSpec((r, C), lambda i, me: (0, 0)), pl.BlockSpec((3, r, C), lambda i, me: (0, 0, 0))],
            out_specs=pl.BlockSpec((r, C), lambda i, me: (0, 0))),
        out_shape=jax.ShapeDtypeStruct((r, C), F32), compiler_params=_cp("arbitrary"))(chip, q, recv3)


def _row_tile(R, C):
    for cand in (1024, 512, 256, 128, 64, 32, 16):
        if R % cand == 0 and cand * C * 4 <= 2 * 1024 * 1024:
            return cand
    return R


def sum_slots(a, *, name):
    n = a.shape[0]
    shp = a.shape[1:]
    a3 = a.reshape(n, -1, shp[-1])
    R, C = a3.shape[1:]
    tr = _row_tile(R, C * n // 2)

    def body(a_ref, o_ref):
        acc = a_ref[0].astype(F32)
        for k in range(1, n):
            acc = acc + a_ref[k].astype(F32)
        o_ref[...] = acc

    out = pl.pallas_call(
        body, name=name, grid=(R // tr,), in_specs=[pl.BlockSpec((n, tr, C), lambda i: (0, i, 0))],
        out_specs=pl.BlockSpec((tr, C), lambda i: (i, 0)), out_shape=jax.ShapeDtypeStruct((R, C), F32),
        compiler_params=_cp("arbitrary"))(a3)
    return out.reshape(shp)


def _adamw_math(w, g, m, v):
    mn = ADAM_B1 * m + (1.0 - ADAM_B1) * g
    vn = ADAM_B2 * v + (1.0 - ADAM_B2) * (g * g)
    m_hat = mn / (1.0 - ADAM_B1 ** ADAM_STEP)
    v_hat = vn / (1.0 - ADAM_B2 ** ADAM_STEP)
    return -ADAM_LR * (m_hat / (jnp.sqrt(v_hat) + ADAM_EPS) + ADAM_WD * w), mn, vn


def adamw(w, g, m, v, *, name):
    shp = w.shape
    args = [t.reshape(-1, shp[-1]) for t in (w, g, m, v)]
    R, C = args[0].shape
    tr = _row_tile(R, C)

    def body(w_ref, g_ref, m_ref, v_ref, d_ref, mo_ref, vo_ref):
        d_ref[...], mo_ref[...], vo_ref[...] = _adamw_math(w_ref[...], g_ref[...], m_ref[...], v_ref[...])

    spec = pl.BlockSpec((tr, C), lambda i: (i, 0))
    res = pl.pallas_call(
        body, name=name, grid=(R // tr,), in_specs=[spec] * 4, out_specs=[spec] * 3,
        out_shape=[jax.ShapeDtypeStruct((R, C), F32)] * 3, compiler_params=_cp("arbitrary"))(*args)
    return [r.reshape(shp) for r in res]


def adamw_layers(w, g_mine, g_theirs, m, v, *, name):
    shp = w.shape
    three = (DEPTH, -1, shp[-1])
    w3, m3, v3 = [t.reshape(three) for t in (w, m, v)]
    ga, gb = [t.reshape(-1, shp[-1]) for t in (g_mine, g_theirs)]
    R, C = ga.shape
    tr = _row_tile(R, C)

    def body(w_ref, ga_ref, gb_ref, m_ref, v_ref, g_ref, d_ref, mo_ref, vo_ref):
        mine = pl.program_id(0) == lax.axis_index("c")
        g = jnp.where(mine, ga_ref[...], gb_ref[...])
        g_ref[...] = g
        d_ref[...], mo_ref[...], vo_ref[...] = _adamw_math(w_ref[...], g, m_ref[...], v_ref[...])

    lay = pl.BlockSpec((None, tr, C), lambda l, i: (l, i, 0))
    one = pl.BlockSpec((tr, C), lambda l, i: (i, 0))
    res = pl.pallas_call(
        body, name=name, grid=(DEPTH, R // tr), in_specs=[lay, one, one, lay, lay], out_specs=[lay] * 4,
        out_shape=[jax.ShapeDtypeStruct(w3.shape, F32)] * 4, compiler_params=_cp("arbitrary", "arbitrary"))(
            w3, ga, gb, m3, v3)
    return [r.reshape(shp) for r in res]


def _remote(src, dst, send_sems, recv_sems, k, peer):
    return pltpu.make_async_remote_copy(src_ref=src, dst_ref=dst, send_sem=send_sems.at[k], recv_sem=recv_sems.at[k],
                                        device_id=peer, device_id_type=MESH)


def gather_layer(l, shards, small=None, pieces=PIECES):
    n_p = len(pieces)
    with_small = small is not None

    def descs(h_in, h_out, sems):
        srcs = dict(zip(pieces, h_in[:n_p]))
        outs = dict(zip(pieces, h_out[:n_p]))
        ici_s, ici_r, d2d_s, d2d_r, own_s, own_r = sems
        x, y, c = _coords()
        me = 2 * x + y
        sib = (x, y, 1 - c)
        ici, fwd, fwd_in, own = [], [], [], []
        for j, (fx, fy) in enumerate(FLIPS):
            peer = (x ^ fx, y ^ fy, c)
            other = 2 * (x ^ fx) + (y ^ fy)
            for p, n in enumerate(pieces):
                k = 3 * p + j
                mine = _half(n, _win(n, outs[n], me), c)
                landed = _half(n, _win(n, outs[n], other), c)
                sib_half = _half(n, _win(n, outs[n], other), 1 - c)
                ici.append((_remote(_half(n, srcs[n].at[l], c), mine, ici_s, ici_r, k, peer),
                            _remote(_half(n, srcs[n].at[l], c), landed, ici_s, ici_r, k, peer)))
                fwd.append(_remote(landed, landed, d2d_s, d2d_r, k, sib))
                fwd_in.append(_remote(sib_half, sib_half, d2d_s, d2d_r, k, sib))
            if with_small:
                k = 3 * n_p + j
                ici.append((_remote(h_in[n_p], h_out[n_p].at[me], ici_s, ici_r, k, peer),
                            _remote(h_in[n_p], h_out[n_p].at[other], ici_s, ici_r, k, peer)))
        for p, n in enumerate(pieces):
            own.append(_remote(srcs[n].at[l], _win(n, outs[n], me), own_s, own_r, p, sib))
        if with_small:
            own.append(_remote(h_in[n_p], h_out[n_p].at[me], own_s, own_r, n_p, sib))
        return ici, fwd, fwd_in, own

    def start(h_in, h_out, sems):
        ici, _, _, own = descs(h_in, h_out, sems)
        for send, _ in ici:
            send.start()
        for cp in own:
            cp.start()

    def finish(h_in, h_out, sems):
        ici, fwd, fwd_in, own = descs(h_in, h_out, sems)
        per_chip = n_p + (1 if with_small else 0)
        for j in range(3):
            for p in range(n_p):
                ici[j * per_chip + p][1].wait_recv()
                fwd[j * n_p + p].start()
            if with_small:
                ici[j * per_chip + n_p][1].wait_recv()
        for cp in fwd_in:
            cp.wait_recv()
        for send, _ in ici:
            send.wait_send()
        for cp in fwd:
            cp.wait_send()
        for cp in own:
            cp.wait()

    arrays = [shards[n] for n in pieces] + ([small] if with_small else [])
    out_shape = [jax.ShapeDtypeStruct(LAYER_SHAPE[n], BF) for n in pieces]
    if with_small:
        out_shape.append(jax.ShapeDtypeStruct((4,) + small.shape, small.dtype))
    sems = [pltpu.SemaphoreType.DMA((3 * n_p + 3,)), pltpu.SemaphoreType.DMA((3 * n_p + 3,)),
            pltpu.SemaphoreType.DMA((3 * n_p,)), pltpu.SemaphoreType.DMA((3 * n_p,)),
            pltpu.SemaphoreType.DMA((n_p + 1,)), pltpu.SemaphoreType.DMA((n_p + 1,))]
    return Hosted(arrays, out_shape, sems, start, finish)


def both_hosted(h1, h2):
    a1, o1, s1 = len(h1.arrays), len(h1.out_shape), len(h1.sems)

    def start(h_in, h_out, sems):
        h1.start(h_in[:a1], h_out[:o1], sems[:s1])
        h2.start(h_in[a1:], h_out[o1:], sems[s1:])

    def finish(h_in, h_out, sems):
        h1.finish(h_in[:a1], h_out[:o1], sems[:s1])
        h2.finish(h_in[a1:], h_out[o1:], sems[s1:])

    return Hosted(list(h1.arrays) + list(h2.arrays), list(h1.out_shape) + list(h2.out_shape),
                  list(h1.sems) + list(h2.sems), start, finish)


def run_hosted(host, name):
    n_in, n_out = len(host.arrays), len(host.out_shape)

    def body(*refs):
        h_in, h_out, sems = refs[:n_in], refs[n_in:n_in + n_out], refs[n_in + n_out:]
        host.start(h_in, h_out, sems)
        host.finish(h_in, h_out, sems)

    return pl.pallas_call(body, name=name, in_specs=[HBM_SPEC] * n_in, out_specs=[HBM_SPEC] * n_out,
                          out_shape=list(host.out_shape), scratch_shapes=list(host.sems))(*host.arrays)


def split_halves(parts, small):
    n_p = len(PIECES)
    rh = small.shape[0] // 2

    def body(*refs):
        srcs = dict(zip(PIECES, refs[:n_p]))
        s_ref = refs[n_p]
        outs = dict(zip(PIECES, refs[n_p + 1:2 * n_p + 1]))
        so_ref = refs[2 * n_p + 1]
        send_sems, recv_sems = refs[2 * n_p + 2:]
        x, y, c = _coords()
        sib = (x, y, 1 - c)
        cps = []
        for p, n in enumerate(PIECES):
            for k in range(4):
                cps.append(_remote(_half(n, _win(n, srcs[n], k), 1 - c), outs[n].at[k], send_sems, recv_sems,
                                   4 * p + k, sib))
        cps.append(_remote(s_ref.at[pl.ds(_mult((1 - c) * rh, 8), rh), :], so_ref, send_sems, recv_sems, 4 * n_p, sib))
        for cp in cps:
            cp.start()
        for cp in cps:
            cp.wait()

    res = pl.pallas_call(
        body, name="split_halves", in_specs=[HBM_SPEC] * (n_p + 1), out_specs=[HBM_SPEC] * (n_p + 1),
        out_shape=[jax.ShapeDtypeStruct((4,) + _half_shape(n), BF) for n in PIECES]
        + [jax.ShapeDtypeStruct((rh, small.shape[1]), small.dtype)],
        scratch_shapes=[pltpu.SemaphoreType.DMA((4 * n_p + 1,)), pltpu.SemaphoreType.DMA((4 * n_p + 1,))],
    )(*[parts[n] for n in PIECES], small)
    return dict(zip(PIECES, res[:n_p])), res[n_p]


def add_halves(name, part, theirs, core):
    half = _half_shape(name)
    zero = (0,) * len(half)
    if name == 'w_in':
        pspec = pl.BlockSpec(half, lambda k, cc: (cc[0], (k + 3) % 4))
    elif name == 'pool_w':
        pspec = pl.BlockSpec(half, lambda k, cc: (cc[0], k, 0))
    elif name in ('ffn_w_gate', 'ffn_w_up'):
        pspec = pl.BlockSpec((None,) + half, lambda k, cc: (k, cc[0], 0))
    else:
        pspec = pl.BlockSpec(half, lambda k, cc: (2 * k + cc[0], 0))
    slot = pl.BlockSpec((None,) + half, lambda k, cc: (k,) + zero)

    def body(cc_ref, p_ref, t_ref, o_ref):
        del cc_ref
        o_ref[...] = (p_ref[...].astype(F32) + t_ref[...].astype(F32)).astype(o_ref.dtype)

    return pl.pallas_call(
        body, name="add_cores_" + name,
        grid_spec=pltpu.PrefetchScalarGridSpec(num_scalar_prefetch=1, grid=(4,), in_specs=[pspec, slot],
                                               out_specs=slot),
        out_shape=jax.ShapeDtypeStruct((4,) + half, BF), compiler_params=_cp("arbitrary"))(core, part, theirs)


def add_halves_small(small, theirs, core):
    rh, C = theirs.shape

    def body(cc_ref, p_ref, t_ref, o_ref):
        del cc_ref
        o_ref[...] = p_ref[...] + t_ref[...]

    blk = pl.BlockSpec((rh, C), lambda i, cc: (0, 0))
    return pl.pallas_call(
        body, name="add_cores_small",
        grid_spec=pltpu.PrefetchScalarGridSpec(
            num_scalar_prefetch=1, grid=(1,),
            in_specs=[pl.BlockSpec((rh, C), lambda i, cc: (cc[0], 0)), blk], out_specs=blk),
        out_shape=jax.ShapeDtypeStruct((rh, C), F32), compiler_params=_cp("arbitrary"))(core, small, theirs)


def exchange_halves(q, qsmall):
    n_p = len(PIECES)

    def descs(h_in, h_out, sems):
        send_sems, recv_sems = sems
        x, y, c = _coords()
        cps = []
        for j, (fx, fy) in enumerate(FLIPS):
            peer = (x ^ fx, y ^ fy, c)
            other = 2 * (x ^ fx) + (y ^ fy)
            for p in range(n_p):
                cps.append(_remote(h_in[p].at[other], h_out[p].at[j], send_sems, recv_sems, 3 * p + j, peer))
            cps.append(_remote(h_in[n_p], h_out[n_p].at[j], send_sems, recv_sems, 3 * n_p + j, peer))
        return cps

    def start(h_in, h_out, sems):
        for cp in descs(h_in, h_out, sems):
            cp.start()

    def finish(h_in, h_out, sems):
        for cp in descs(h_in, h_out, sems):
            cp.wait()

    arrays = [q[n] for n in PIECES] + [qsmall]
    out_shape = [jax.ShapeDtypeStruct((3,) + _half_shape(n), BF) for n in PIECES]
    out_shape.append(jax.ShapeDtypeStruct((3,) + qsmall.shape, qsmall.dtype))
    sems = [pltpu.SemaphoreType.DMA((3 * n_p + 3,)), pltpu.SemaphoreType.DMA((3 * n_p + 3,))]
    return Hosted(arrays, out_shape, sems, start, finish)


def sum_halves(name, q, recv3, chip):
    half = _half_shape(name)
    zero = (0,) * len(half)

    def body(me_ref, q_ref, r_ref, o_ref):
        del me_ref
        acc = q_ref[...].astype(F32)
        for j in range(3):
            acc = acc + r_ref[j].astype(F32)
        o_ref[...] = acc

    return pl.pallas_call(
        body, name="sum_chips_" + name,
        grid_spec=pltpu.PrefetchScalarGridSpec(
            num_scalar_prefetch=1, grid=(1,),
            in_specs=[pl.BlockSpec((None,) + half, lambda i, me: (me[0],) + zero),
                      pl.BlockSpec((3,) + half, lambda i, me: (0,) + zero)],
            out_specs=pl.BlockSpec(half, lambda i, me: zero)),
        out_shape=jax.ShapeDtypeStruct(half, F32), compiler_params=_cp("arbitrary"))(chip, q, recv3)


def adamw_halves(w, g0, m, v, *, name):
    shp = w.shape
    C = shp[-1]
    four = (DEPTH, 2, -1, C)
    w4, m4, v4 = [t.reshape(four) for t in (w, m, v)]
    gs = [t.reshape(-1, C) for pair in g0 for t in pair]
    Rh = gs[0].shape[0]
    tr = _row_tile(Rh, C)

    def body(w_ref, a0_ref, b0_ref, a1_ref, b1_ref, m_ref, v_ref, g_ref, d_ref, mo_ref, vo_ref):
        mine = pl.program_id(1) == lax.axis_index("c")
        g_l0 = jnp.where(mine, a0_ref[...], b0_ref[...])
        g_l1 = jnp.where(mine, a1_ref[...], b1_ref[...])
        g = jnp.where(pl.program_id(0) == 0, g_l0, g_l1)
        g_ref[...] = g
        d_ref[...], mo_ref[...], vo_ref[...] = _adamw_math(w_ref[...], g, m_ref[...], v_ref[...])

    lay = pl.BlockSpec((None, None, tr, C), lambda l, h, i: (l, h, i, 0))
    one = pl.BlockSpec((tr, C), lambda l, h, i: (i, 0))
    res = pl.pallas_call(
        body, name=name, grid=(DEPTH, 2, Rh // tr), in_specs=[lay, one, one, one, one, lay, lay],
        out_specs=[lay] * 4, out_shape=[jax.ShapeDtypeStruct(w4.shape, F32)] * 4,
        compiler_params=_cp("arbitrary", "arbitrary", "arbitrary"))(w4, *gs, m4, v4)
    return [r.reshape(shp) for r in res]


def _local_shape(name, full_shape):
    shp = list(full_shape)
    ax = BIG_SHARDED.get(name, SMALL_SHARDED.get(name))
    if ax is not None:
        shp[ax] //= 4
    return tuple(shp)


FULL_SHAPES = {
    'w_in': (DEPTH, D, IN_COLS), 'b_in': (DEPTH, IN_COLS), 'pool_w': (DEPTH, 4, POOL_GD, POOL_GD),
    'pool_scale': (DEPTH, D), 'lru_conv_w': (DEPTH, 4, D), 'lru_conv_b': (DEPTH, D),
    'lru_w_r': (DEPTH, LRU_HEADS, LRU_HD, LRU_HD), 'lru_b_r': (DEPTH, D),
    'lru_w_i': (DEPTH, LRU_HEADS, LRU_HD, LRU_HD), 'lru_b_i': (DEPTH, D), 'lru_lambda': (DEPTH, D),
    'lru_w_out': (DEPTH, D, D), 'sconv_w': (DEPTH, 3, D), 'sconv_w_out': (DEPTH, D, D), 'w_mix_out': (DEPTH, D, D),
    'xa_w_q': (DEPTH, D, D), 'xa_w_k': (DEPTH, D, D), 'xa_w_v': (DEPTH, D, D), 'xa_w_o': (DEPTH, D, D),
    'ffn_w_gate': (DEPTH, D, D_FF), 'ffn_w_up': (DEPTH, D, D_FF), 'ffn_w_down': (DEPTH, D_FF, D),
    'ln_g': (DEPTH, 3, D), 'ln_b': (DEPTH, 3, D)}


def _pack(arrs, names, width, dtype, row_mult):
    flat = jnp.concatenate([arrs[n].astype(dtype).reshape(-1) for n in names])
    pad = (-flat.shape[0]) % (width * row_mult)
    if pad:
        flat = jnp.concatenate([flat, jnp.zeros((pad,), dtype)])
    return flat.reshape(-1, width)


def _unpack(flat2d, names, shapes):
    flat = flat2d.reshape(-1)
    out, off = {}, 0
    for n in names:
        size = 1
        for s in shapes[n]:
            size *= s
        out[n] = flat[off:off + size].reshape(shapes[n])
        off += size
    return out


def _gathered_full(g4, names, sharded_axis):
    loc_shapes = {n: _local_shape(n, FULL_SHAPES[n]) for n in names}
    per = [_unpack(g4[k], names, loc_shapes) for k in range(4)]
    return {n: jnp.concatenate([per[k][n] for k in range(4)], axis=sharded_axis[n]) for n in names}


def _perm_cols(a, perm, axis):
    blocks = [lax.slice_in_dim(a, p * D, (p + 1) * D, axis=axis) for p in perm]
    return jnp.concatenate(blocks, axis=axis)


def _shards_apart(a):
    w = a.shape[1] // 4
    return jnp.stack([a[:, k * w:(k + 1) * w] for k in range(4)])


def _shards_joined(a4):
    return jnp.concatenate([a4[k] for k in range(4)], axis=1)


Z_INV = tuple(Z_PERM.index(j) for j in range(8))
SMALL_SH_NAMES = list(SMALL_SHARDED)
SMALL_ROWS = 32


def kernel(x, mem, w_in, b_in, pool_w, pool_scale, lru_conv_w, lru_conv_b, lru_w_r, lru_b_r, lru_w_i, lru_b_i, lru_lambda, lru_w_out, sconv_w, sconv_w_out, w_mix_out, xa_w_q, xa_w_k, xa_w_v, xa_w_o, ffn_w_gate, ffn_w_up, ffn_w_down, ln_g, ln_b, loss_target, m_w_in, m_b_in, m_pool_w, m_pool_scale, m_lru_conv_w, m_lru_conv_b, m_lru_w_r, m_lru_b_r, m_lru_w_i, m_lru_b_i, m_lru_lambda, m_lru_w_out, m_sconv_w, m_sconv_w_out, m_w_mix_out, m_xa_w_q, m_xa_w_k, m_xa_w_v, m_xa_w_o, m_ffn_w_gate, m_ffn_w_up, m_ffn_w_down, m_ln_g, m_ln_b, v_w_in, v_b_in, v_pool_w, v_pool_scale, v_lru_conv_w, v_lru_conv_b, v_lru_w_r, v_lru_b_r, v_lru_w_i, v_lru_b_i, v_lru_lambda, v_lru_w_out, v_sconv_w, v_sconv_w_out, v_w_mix_out, v_xa_w_q, v_xa_w_k, v_xa_w_v, v_xa_w_o, v_ffn_w_gate, v_ffn_w_up, v_ffn_w_down, v_ln_g, v_ln_b):
    loc = dict(w_in=w_in, b_in=b_in, pool_w=pool_w, pool_scale=pool_scale, lru_conv_w=lru_conv_w,
               lru_conv_b=lru_conv_b, lru_w_r=lru_w_r, lru_b_r=lru_b_r, lru_w_i=lru_w_i, lru_b_i=lru_b_i,
               lru_lambda=lru_lambda, lru_w_out=lru_w_out, sconv_w=sconv_w, sconv_w_out=sconv_w_out,
               w_mix_out=w_mix_out, xa_w_q=xa_w_q, xa_w_k=xa_w_k, xa_w_v=xa_w_v, xa_w_o=xa_w_o,
               ffn_w_gate=ffn_w_gate, ffn_w_up=ffn_w_up, ffn_w_down=ffn_w_down, ln_g=ln_g, ln_b=ln_b)
    mom = dict(w_in=m_w_in, b_in=m_b_in, pool_w=m_pool_w, pool_scale=m_pool_scale, lru_conv_w=m_lru_conv_w,
               lru_conv_b=m_lru_conv_b, lru_w_r=m_lru_w_r, lru_b_r=m_lru_b_r, lru_w_i=m_lru_w_i, lru_b_i=m_lru_b_i,
               lru_lambda=m_lru_lambda, lru_w_out=m_lru_w_out, sconv_w=m_sconv_w, sconv_w_out=m_sconv_w_out,
               w_mix_out=m_w_mix_out, xa_w_q=m_xa_w_q, xa_w_k=m_xa_w_k, xa_w_v=m_xa_w_v, xa_w_o=m_xa_w_o,
               ffn_w_gate=m_ffn_w_gate, ffn_w_up=m_ffn_w_up, ffn_w_down=m_ffn_w_down, ln_g=m_ln_g, ln_b=m_ln_b)
    var = dict(w_in=v_w_in, b_in=v_b_in, pool_w=v_pool_w, pool_scale=v_pool_scale, lru_conv_w=v_lru_conv_w,
               lru_conv_b=v_lru_conv_b, lru_w_r=v_lru_w_r, lru_b_r=v_lru_b_r, lru_w_i=v_lru_w_i, lru_b_i=v_lru_b_i,
               lru_lambda=v_lru_lambda, lru_w_out=v_lru_w_out, sconv_w=v_sconv_w, sconv_w_out=v_sconv_w_out,
               w_mix_out=v_w_mix_out, xa_w_q=v_xa_w_q, xa_w_k=v_xa_w_k, xa_w_v=v_xa_w_v, xa_w_o=v_xa_w_o,
               ffn_w_gate=v_ffn_w_gate, ffn_w_up=v_ffn_w_up, ffn_w_down=v_ffn_w_down, ln_g=v_ln_g, ln_b=v_ln_b)

    chip = 2 * lax.axis_index("x") + lax.axis_index("y")
    core = lax.axis_index("c")
    chip_arr = jnp.reshape(chip, (1,)).astype(jnp.int32)
    core_arr = jnp.reshape(core, (1,)).astype(jnp.int32)
    n_p = len(PIECES)

    shards = {n: loc[n].astype(BF) for n in PIECES}
    small = _pack(loc, SMALL_SH_NAMES, 256, F32, 8)
    first = ('w_in',)
    rest = tuple(n for n in PIECES if n not in first)
    early1 = ('w_in', 'pool_w')
    late1 = tuple(n for n in PIECES if n not in early1)
    got = run_hosted(gather_layer(0, shards, small, pieces=first), "gather_first")
    vec = _gathered_full(got[len(first)], SMALL_SH_NAMES, SMALL_SHARDED)
    W = {n: [None] * DEPTH for n in ('w_in', 'pool_w', 'ffn_w_down', 'ffn_w_gu') + SQUARES}
    W['b_in'] = [jnp.roll(b_in[l:l + 1], -2 * D, axis=1) for l in range(DEPTH)]
    for n in ('lru_conv_w', 'sconv_w', 'ln_g', 'ln_b'):
        W[n] = [vec[n][l] for l in range(DEPTH)]
    for n in ('lru_w_r', 'lru_w_i'):
        W[n] = [loc[n][l].astype(BF) for l in range(DEPTH)]
    for n in ('pool_scale', 'lru_conv_b', 'lru_b_r', 'lru_b_i', 'lru_lambda'):
        W[n] = [loc[n][l:l + 1] for l in range(DEPTH)]

    def take(l, names, arrays):
        full = dict(zip(names, arrays))
        if 'ffn_w_gate' in full:
            W['ffn_w_gu'][l] = jnp.concatenate([_shards_joined(full['ffn_w_gate']), _shards_joined(full['ffn_w_up'])],
                                               axis=1)
        for n in names:
            if n in W:
                W[n][l] = full[n]

    take(0, first, got[:len(first)])

    def after_in_proj(results):
        take(0, rest, results[:len(rest)])
        take(1, early1, results[len(rest):])

    xs, memb = x[0], mem[0].astype(BF)
    host_a = both_hosted(gather_layer(0, shards, pieces=rest), gather_layer(1, shards, pieces=early1))
    xf, xb, sv0 = _layer_fwd(0, xs, None, None, memb, W, host=host_a, host2=gather_layer(1, shards, pieces=late1),
                             after_in_proj=after_in_proj)
    take(1, late1, sv0['hosted2'])
    xf, xb, sv1 = _layer_fwd(1, xf, xb, None, memb, W)
    saves = [sv0, sv1]
    kvs = [(sv['kb'], sv['vb']) for sv in saves]

    def packed(G):
        g = dict(G)
        g['ffn_w_gate'] = _shards_apart(g['ffn_w_gu'][:, :D_FF])
        g['ffn_w_up'] = _shards_apart(g['ffn_w_gu'][:, D_FF:])
        g['pool_w'] = g['pool_w'].astype(BF)
        g['b_in'] = jnp.roll(g['b_in'], 2 * D, axis=1)
        return {n: g[n] for n in PIECES}, _pack(g, SMALL_ALL, D, F32, SMALL_ROWS)

    def reduce_start(G):
        parts, smallp = packed(G)
        theirs, theirs_small = split_halves(parts, smallp)
        q = {n: add_halves(n, parts[n], theirs[n], core_arr) for n in PIECES}
        qs = add_halves_small(smallp, theirs_small, core_arr)
        return q, qs, exchange_halves(q, qs)

    def reduce_finish(q, qs, recv):
        sums = [sum_halves(n, q[n], recv[p], chip_arr) for p, n in enumerate(PIECES)]
        sums.append(sum_chips_small(qs, recv[n_p], chip_arr))
        other = swap_cores(sums)
        return sums, other

    lf = (W['ln_b'][1][2:3], loss_target[0])
    dx, G1, loss_blk = _layer_bwd(1, None, saves[1], memb, kvs[1][0], kvs[1][1], W, loss_from=lf)
    q1, qs1, host1 = reduce_start(G1)
    started0 = []

    def last_host_fn(G):
        started0.extend(reduce_start(G))
        return started0[2]

    grad_x, G0, _ = _layer_bwd(0, dx, saves[0], memb, kvs[0][0], kvs[0][1], W, host=host1, last_host_fn=last_host_fn)
    red = [reduce_finish(started0[0], started0[1], G0['hosted_last']), reduce_finish(q1, qs1, G0['hosted'])]
    loss = lax.psum(loss_blk[0, 0], ("x", "y", "c"))

    small_shapes = {n: FULL_SHAPES[n][1:] for n in SMALL_ALL}
    per_layer = []
    for l in range(DEPTH):
        mine, theirs = red[l][0][-1], red[l][1][-1]
        whole = jnp.where(core == 0, jnp.concatenate([mine, theirs]), jnp.concatenate([theirs, mine]))
        per_layer.append(_unpack(whole, SMALL_ALL, small_shapes))
    grads = {}
    for n in SMALL_ALL:
        gn = jnp.stack([per_layer[l][n] for l in range(DEPTH)])
        if n in SMALL_SHARDED:
            size = loc[n].shape[SMALL_SHARDED[n]]
            gn = lax.dynamic_slice_in_dim(gn, chip * size, size, axis=SMALL_SHARDED[n])
        grads[n] = gn

    out_d, out_m, out_v = {}, {}, {}
    for p, n in enumerate(PIECES):
        pairs = [(red[l][0][p], red[l][1][p]) for l in range(DEPTH)]
        grads[n], out_d[n], out_m[n], out_v[n] = adamw_halves(loc[n], pairs, mom[n], var[n], name="adamw_" + n)
    for n in SMALL_ALL:
        out_d[n], out_m[n], out_v[n] = adamw(loc[n], grads[n], mom[n], var[n], name="adamw_" + n)

    return (loss, grad_x[None], *[grads[n] for n in WEIGHTS], *[out_d[n] for n in WEIGHTS],
            *[out_m[n] for n in WEIGHTS], *[out_v[n] for n in WEIGHTS])
```

```python
import functools

import jax
import jax.numpy as jnp
from jax import lax
from jax.experimental import pallas as pl
from jax.experimental.pallas import tpu as pltpu

F32 = jnp.float32
BF = jnp.bfloat16
MESH = pl.DeviceIdType.MESH

D = 1024
DEPTH = 2
N_MEM = 256
POOL_WINDOWS = (2, 4, 8, 16)
POOL_GD = 256
LRU_HEADS = 8
LRU_HD = 128
LRU_C = 8.0
X_HEADS = 4
X_HD = 256
D_FF = 2816
IN_COLS = 8 * D
ALPHA = (2 * DEPTH) ** 0.25
LN_EPS = 1e-5
ADAM_LR = 0.001
ADAM_B1 = 0.9
ADAM_B2 = 0.999
ADAM_EPS = 1e-08
ADAM_WD = 0.01
ADAM_STEP = 10

Z_PERM = (2, 3, 4, 5, 6, 7, 0, 1)
ZB_SCONV, ZB_GATE, ZB_POOL, ZB_LRU = 0, 1, 6, 7
HALO = 16
VMEM_LIMIT = 56 * 1024 * 1024

WEIGHTS = ['w_in', 'b_in', 'pool_w', 'pool_scale', 'lru_conv_w', 'lru_conv_b', 'lru_w_r', 'lru_b_r', 'lru_w_i',
           'lru_b_i', 'lru_lambda', 'lru_w_out', 'sconv_w', 'sconv_w_out', 'w_mix_out', 'xa_w_q', 'xa_w_k', 'xa_w_v',
           'xa_w_o', 'ffn_w_gate', 'ffn_w_up', 'ffn_w_down', 'ln_g', 'ln_b']
BIG_SHARDED = {'w_in': 2, 'pool_w': 2, 'lru_w_out': 1, 'sconv_w_out': 1, 'w_mix_out': 1, 'xa_w_q': 1, 'xa_w_k': 1,
               'xa_w_v': 1, 'xa_w_o': 1, 'ffn_w_gate': 2, 'ffn_w_up': 2, 'ffn_w_down': 1}
SMALL_SHARDED = {'lru_conv_w': 2, 'sconv_w': 2, 'ln_g': 2, 'ln_b': 2}
REPLICATED = ['b_in', 'pool_scale', 'lru_conv_b', 'lru_w_r', 'lru_b_r', 'lru_w_i', 'lru_b_i', 'lru_lambda']
SMALL_ALL = ['b_in', 'pool_scale', 'lru_conv_w', 'lru_conv_b', 'lru_w_r', 'lru_b_r', 'lru_w_i', 'lru_b_i',
             'lru_lambda', 'sconv_w', 'ln_g', 'ln_b']


def _cp(*sem):
    return pltpu.CompilerParams(dimension_semantics=sem, vmem_limit_bytes=VMEM_LIMIT)


def _sigmoid(x):
    return 0.5 * jnp.tanh(0.5 * x) + 0.5


class Hosted:
    def __init__(self, arrays, out_shape, sems, start, finish):
        self.arrays, self.out_shape, self.sems, self.start, self.finish = arrays, out_shape, sems, start, finish


def _with_host(host, n_in, n_out, refs):
    if host is None:
        return refs[:n_in], (), refs[n_in:n_in + n_out], (), refs[n_in + n_out:], ()
    hi, ho, hs = len(host.arrays), len(host.out_shape), len(host.sems)
    ins, h_in = refs[:n_in], refs[n_in:n_in + hi]
    outs = refs[n_in + hi:n_in + hi + n_out]
    h_out = refs[n_in + hi + n_out:n_in + hi + n_out + ho]
    rest = refs[n_in + hi + n_out + ho:]
    return ins, h_in, outs, h_out, rest[:len(rest) - hs], rest[len(rest) - hs:]


HBM_SPEC = pl.BlockSpec(memory_space=pl.ANY)


def mm_nn(a, w, bias, *, out_dtype, tm, tn, name, host=None):
    T, K = a.shape
    N = w.shape[1]
    tm, tn = min(tm, T), min(tn, N)
    nj, ni = N // tn, T // tm
    n_in = 2 if bias is None else 3

    def body(*refs):
        ins, h_in, (o_ref,), h_out, _, h_sems = _with_host(host, n_in, 1, refs)
        a_ref, w_ref = ins[:2]
        j, i = pl.program_id(0), pl.program_id(1)
        if host is not None:
            @pl.when((j == 0) & (i == 0))
            def _():
                host.start(h_in, h_out, h_sems)
        acc = jnp.dot(a_ref[...].astype(BF), w_ref[...], preferred_element_type=F32)
        if bias is not None:
            acc = acc + ins[2][...]
        o_ref[...] = acc.astype(o_ref.dtype)
        if host is not None:
            @pl.when((j == nj - 1) & (i == ni - 1))
            def _():
                host.finish(h_in, h_out, h_sems)

    in_specs = [pl.BlockSpec((tm, K), lambda j, i: (i, 0)), pl.BlockSpec((K, tn), lambda j, i: (0, j))]
    args = [a, w]
    if bias is not None:
        in_specs.append(pl.BlockSpec((1, tn), lambda j, i: (0, j)))
        args.append(bias)
    out_specs = [pl.BlockSpec((tm, tn), lambda j, i: (i, j))]
    out_shape = [jax.ShapeDtypeStruct((T, N), out_dtype)]
    scratch = []
    if host is not None:
        in_specs += [HBM_SPEC] * len(host.arrays)
        args += list(host.arrays)
        out_specs += [HBM_SPEC] * len(host.out_shape)
        out_shape += list(host.out_shape)
        scratch = list(host.sems)
    res = pl.pallas_call(
        body, name=name, grid=(nj, ni), in_specs=in_specs, out_specs=out_specs, out_shape=out_shape,
        scratch_shapes=scratch, compiler_params=_cp("arbitrary", "arbitrary"))(*args)
    return res[0] if host is None else (res[0], res[1:])


def mm_nt(a, w, res, *, out_dtype, tm, tc, name, host=None):
    T, C = a.shape
    K = w.shape[0]
    tm, tc = min(tm, T), min(tc, C)
    nc = C // tc
    ni = T // tm
    n_in = 2 if res is None else 3

    def body(*refs):
        ins, h_in, (o_ref,), h_out, (acc_ref,), h_sems = _with_host(host, n_in, 1, refs)
        a_ref, w_ref = ins[:2]
        r_ref = ins[2] if res is not None else None
        c = pl.program_id(1)
        if host is not None:
            @pl.when((pl.program_id(0) == 0) & (c == 0))
            def _():
                host.start(h_in, h_out, h_sems)

            @pl.when((pl.program_id(0) == ni - 1) & (c == nc - 1))
            def _():
                host.finish(h_in, h_out, h_sems)
        part = lax.dot_general(a_ref[...].astype(BF), w_ref[...], (((1,), (1,)), ((), ())),
                               preferred_element_type=F32)

        @pl.when(c == 0)
        def _():
            acc_ref[...] = part

        @pl.when(c > 0)
        def _():
            acc_ref[...] += part

        @pl.when(c == nc - 1)
        def _():
            out = acc_ref[...]
            if res is not None:
                out = out + ALPHA * r_ref[...]
            o_ref[...] = out.astype(o_ref.dtype)

    in_specs = [pl.BlockSpec((tm, tc), lambda i, c: (i, c)), pl.BlockSpec((K, tc), lambda i, c: (0, c))]
    args = [a, w]
    if res is not None:
        in_specs.append(pl.BlockSpec((tm, K), lambda i, c: (i, 0)))
        args.append(res)
    out_specs = [pl.BlockSpec((tm, K), lambda i, c: (i, 0))]
    out_shape = [jax.ShapeDtypeStruct((T, K), out_dtype)]
    scratch = [pltpu.VMEM((tm, K), F32)]
    if host is not None:
        in_specs += [HBM_SPEC] * len(host.arrays)
        args += list(host.arrays)
        out_specs += [HBM_SPEC] * len(host.out_shape)
        out_shape += list(host.out_shape)
        scratch += list(host.sems)
    out = pl.pallas_call(
        body, name=name, grid=(ni, nc), in_specs=in_specs, out_specs=out_specs, out_shape=out_shape,
        scratch_shapes=scratch, compiler_params=_cp("arbitrary", "arbitrary"))(*args)
    return out[0] if host is None else (out[0], out[1:])


def mm_tn(a, b, *, out_dtype, tk, tn, tt, name, colsum=False):
    T, K = a.shape
    N = b.shape[1]
    tk, tn, tt = min(tk, K), min(tn, N), min(tt, T)
    nt = T // tt

    def body(*refs):
        if colsum:
            a_ref, b_ref, o_ref, cs_ref, acc_ref = refs
        else:
            a_ref, b_ref, o_ref, acc_ref = refs
        i, t = pl.program_id(1), pl.program_id(2)
        bb = b_ref[...]
        part = lax.dot_general(a_ref[...].astype(BF), bb.astype(BF), (((0,), (0,)), ((), ())),
                               preferred_element_type=F32)

        @pl.when(t == 0)
        def _():
            acc_ref[...] = part

        @pl.when(t > 0)
        def _():
            acc_ref[...] += part

        @pl.when(t == nt - 1)
        def _():
            o_ref[...] = acc_ref[...].astype(o_ref.dtype)

        if colsum:
            s = jnp.sum(bb.astype(F32), axis=0, keepdims=True)

            @pl.when((i == 0) & (t == 0))
            def _():
                cs_ref[...] = s

            @pl.when((i == 0) & (t > 0))
            def _():
                cs_ref[...] += s

    out_specs = [pl.BlockSpec((tk, tn), lambda j, i, t: (i, j))]
    out_shape = [jax.ShapeDtypeStruct((K, N), out_dtype)]
    if colsum:
        out_specs.append(pl.BlockSpec((1, tn), lambda j, i, t: (0, j)))
        out_shape.append(jax.ShapeDtypeStruct((1, N), F32))
    res = pl.pallas_call(
        body, name=name, grid=(N // tn, K // tk, nt),
        in_specs=[pl.BlockSpec((tt, tk), lambda j, i, t: (t, i)), pl.BlockSpec((tt, tn), lambda j, i, t: (t, j))],
        out_specs=out_specs, out_shape=out_shape, scratch_shapes=[pltpu.VMEM((tk, tn), F32)],
        compiler_params=_cp("arbitrary", "arbitrary", "arbitrary"))(a, b)
    return res if colsum else res[0]


def mm_res_ln(a, w, res, g, b, *, tm, name):
    T, K = a.shape
    tm = min(tm, T)

    def body(a_ref, w_ref, r_ref, g_ref, b_ref, y_ref, yb_ref, xh_ref, rs_ref):
        pre = ALPHA * r_ref[...] + jnp.dot(a_ref[...].astype(BF), w_ref[...], preferred_element_type=F32)
        mu = jnp.mean(pre, axis=-1, keepdims=True)
        cen = pre - mu
        var = jnp.mean(cen * cen, axis=-1, keepdims=True)
        rstd = lax.rsqrt(var + LN_EPS)
        xhat = cen * rstd
        y = xhat * g_ref[...] + b_ref[...]
        y_ref[...] = y
        yb_ref[...] = y.astype(BF)
        xh_ref[...] = xhat
        rs_ref[...] = rstd

    row = lambda i: (i, 0)
    fix = lambda i: (0, 0)
    return pl.pallas_call(
        body, name=name, grid=(T // tm,),
        in_specs=[pl.BlockSpec((tm, K), row), pl.BlockSpec((K, D), fix), pl.BlockSpec((tm, D), row),
                  pl.BlockSpec((1, D), fix), pl.BlockSpec((1, D), fix)],
        out_specs=[pl.BlockSpec((tm, D), row), pl.BlockSpec((tm, D), row), pl.BlockSpec((tm, D), row),
                   pl.BlockSpec((tm, 1), row)],
        out_shape=[jax.ShapeDtypeStruct((T, D), F32), jax.ShapeDtypeStruct((T, D), BF),
                   jax.ShapeDtypeStruct((T, D), F32), jax.ShapeDtypeStruct((T, 1), F32)],
        compiler_params=_cp("arbitrary"))(a, w, res, g, b)


def ln_bwd(dy, xhat, rstd, g, *, tm, name, loss_from=None, dw_of=None):
    T = xhat.shape[0]
    tm = min(tm, T)
    nt = T // tm
    with_loss = loss_from is not None
    with_dw = dw_of is not None

    def body(*refs):
        if with_dw:
            acc_ref, refs = refs[-1], refs[:-1]
            n_main_in = 5 if with_loss else 4
            a_ref = refs[n_main_in]
            dw_ref = refs[-1]
            refs = refs[:n_main_in] + refs[n_main_in + 1:-1]
        if with_loss:
            xh_ref, rs_ref, g_ref, b_ref, t_ref, dp_ref, dpb_ref, dg_ref, db_ref, ls_ref = refs
        else:
            dy_ref, xh_ref, rs_ref, g_ref, dp_ref, dpb_ref, dg_ref, db_ref = refs
        i = pl.program_id(0)
        xhat_ = xh_ref[...]
        gg = g_ref[...]
        if with_loss:
            err = xhat_ * gg + b_ref[...] - t_ref[...]
            dyv = err * (1.0 / D)
            lpart = 0.5 * jnp.sum(jnp.sum(err * err, axis=-1, keepdims=True) * (1.0 / D))
        else:
            dyv = dy_ref[...]
        dxh = dyv * gg
        m1 = jnp.mean(dxh, axis=-1, keepdims=True)
        m2 = jnp.mean(dxh * xhat_, axis=-1, keepdims=True)
        dpre = rs_ref[...] * (dxh - m1 - xhat_ * m2)
        dp_ref[...] = dpre
        dpb = dpre.astype(BF)
        dpb_ref[...] = dpb
        dgp = jnp.sum(dyv * xhat_, axis=0, keepdims=True)
        dbp = jnp.sum(dyv, axis=0, keepdims=True)
        if with_dw:
            dwp = lax.dot_general(a_ref[...], dpb, (((0,), (0,)), ((), ())), preferred_element_type=F32)

        @pl.when(i == 0)
        def _():
            dg_ref[...] = dgp
            db_ref[...] = dbp
            if with_loss:
                ls_ref[...] = jnp.full((8, 128), lpart, F32)
            if with_dw:
                acc_ref[...] = dwp

        @pl.when(i > 0)
        def _():
            dg_ref[...] += dgp
            db_ref[...] += dbp
            if with_loss:
                ls_ref[...] += jnp.full((8, 128), lpart, F32)
            if with_dw:
                acc_ref[...] += dwp

        if with_dw:
            @pl.when(i == nt - 1)
            def _():
                dw_ref[...] = acc_ref[...].astype(BF)

    row = lambda i: (i, 0)
    fix = lambda i: (0, 0)
    if with_loss:
        in_specs = [pl.BlockSpec((tm, D), row), pl.BlockSpec((tm, 1), row), pl.BlockSpec((1, D), fix),
                    pl.BlockSpec((1, D), fix), pl.BlockSpec((tm, D), row)]
        args = [xhat, rstd, g, loss_from[0], loss_from[1]]
    else:
        in_specs = [pl.BlockSpec((tm, D), row), pl.BlockSpec((tm, D), row), pl.BlockSpec((tm, 1), row),
                    pl.BlockSpec((1, D), fix)]
        args = [dy, xhat, rstd, g]
    out_specs = [pl.BlockSpec((tm, D), row), pl.BlockSpec((tm, D), row), pl.BlockSpec((1, D), fix),
                 pl.BlockSpec((1, D), fix)]
    out_shape = [jax.ShapeDtypeStruct((T, D), F32), jax.ShapeDtypeStruct((T, D), BF),
                 jax.ShapeDtypeStruct((1, D), F32), jax.ShapeDtypeStruct((1, D), F32)]
    if with_loss:
        out_specs.append(pl.BlockSpec((8, 128), fix))
        out_shape.append(jax.ShapeDtypeStruct((8, 128), F32))
    scratch = []
    if with_dw:
        K = dw_of.shape[1]
        in_specs.append(pl.BlockSpec((tm, K), row))
        args.append(dw_of)
        out_specs.append(pl.BlockSpec((K, D), fix))
        out_shape.append(jax.ShapeDtypeStruct((K, D), BF))
        scratch.append(pltpu.VMEM((K, D), F32))
    return pl.pallas_call(body, name=name, grid=(nt,), in_specs=in_specs, out_specs=out_specs,
                          out_shape=out_shape, scratch_shapes=scratch, compiler_params=_cp("arbitrary"))(*args)


def _prev_halo(tm, blk):
    return lambda i: (jnp.maximum(i * (tm // HALO) - 1, 0), blk)


def _next_halo(tm, T, blk):
    return lambda i: (jnp.minimum((i + 1) * (tm // HALO), T // HALO - 1), blk)


def _pool_p(ext, t, g):
    e = ext[:, g * POOL_GD:(g + 1) * POOL_GD]
    s = e
    for sh in (1, 2, 4, 8)[:g + 1]:
        s = s + pltpu.roll(s, sh, axis=0)
    inv = 1.0 / jnp.minimum(t + 1, POOL_WINDOWS[g]).astype(F32)
    return s[HALO:] * inv - e[HALO:]


def pool_fwd(z, pw, *, tm, name):
    T = z.shape[0]
    tm = min(tm, T)

    def body(zm_ref, zh_ref, pw_ref, o_ref):
        i = pl.program_id(0)
        keep = jnp.where(i == 0, 0.0, 1.0).astype(F32)
        ext = jnp.concatenate([zh_ref[...].astype(F32) * keep, zm_ref[...].astype(F32)], axis=0)
        t = i * tm + lax.broadcasted_iota(jnp.int32, (tm, 1), 0)
        outs = [jnp.dot(_pool_p(ext, t, g).astype(BF), pw_ref[g], preferred_element_type=F32) for g in range(4)]
        o_ref[...] = jnp.concatenate(outs, axis=1).astype(o_ref.dtype)

    return pl.pallas_call(
        body, name=name, grid=(T // tm,),
        in_specs=[pl.BlockSpec((tm, D), lambda i: (i, ZB_POOL)), pl.BlockSpec((HALO, D), _prev_halo(tm, ZB_POOL)),
                  pl.BlockSpec((4, POOL_GD, POOL_GD), lambda i: (0, 0, 0))],
        out_specs=pl.BlockSpec((tm, D), lambda i: (i, 0)),
        out_shape=jax.ShapeDtypeStruct((T, D), BF), compiler_params=_cp("arbitrary"))(z, z, pw)


def pool_bwd(dz, dyp, yp_pre, z, pw, ps, *, tm, name):
    T = z.shape[0]
    tm = min(tm, T)
    nt = T // tm

    def body(dz_in, dy_ref, dyn_ref, yp_ref, zm_ref, zh_ref, pw_ref, ps_ref, dz_ref, dpw_ref, dps_ref):
        del dz_in
        i = pl.program_id(0)
        keep_p = jnp.where(i == 0, 0.0, 1.0).astype(F32)
        keep_n = jnp.where(i == nt - 1, 0.0, 1.0).astype(F32)
        ext = jnp.concatenate([zh_ref[...].astype(F32) * keep_p, zm_ref[...].astype(F32)], axis=0)
        t = i * tm + lax.broadcasted_iota(jnp.int32, (tm, 1), 0)
        psv = ps_ref[...]
        dy = dy_ref[...].astype(F32)
        dyp_ext = jnp.concatenate([dy, dyn_ref[...].astype(F32) * keep_n], axis=0) * psv
        t_ext = i * tm + lax.broadcasted_iota(jnp.int32, (tm + HALO, 1), 0)
        dps = jnp.sum(dy * yp_ref[...].astype(F32), axis=0, keepdims=True)
        dzs, dpws = [], []
        for g in range(4):
            sl = slice(g * POOL_GD, (g + 1) * POOL_GD)
            dyg = dyp_ext[:, sl].astype(BF)
            dp = lax.dot_general(dyg, pw_ref[g], (((1,), (1,)), ((), ())), preferred_element_type=F32)
            q = dp * (1.0 / jnp.minimum(t_ext + 1, POOL_WINDOWS[g]).astype(F32))
            s = q
            for sh in (1, 2, 4, 8)[:g + 1]:
                s = s + pltpu.roll(s, tm + HALO - sh, axis=0)
            dzs.append(s[:tm] - dp[:tm])
            p = _pool_p(ext, t, g).astype(BF)
            dpws.append(lax.dot_general(p, dyg[:tm], (((0,), (0,)), ((), ())), preferred_element_type=F32))
        dz_ref[...] = jnp.concatenate(dzs, axis=1).astype(dz_ref.dtype)

        @pl.when(i == 0)
        def _():
            for g in range(4):
                dpw_ref[g] = dpws[g]
            dps_ref[...] = dps

        @pl.when(i > 0)
        def _():
            for g in range(4):
                dpw_ref[g] += dpws[g]
            dps_ref[...] += dps

    row = lambda i: (i, 0)
    return pl.pallas_call(
        body, name=name, grid=(nt,),
        in_specs=[pl.BlockSpec(memory_space=pl.ANY),
                  pl.BlockSpec((tm, D), row), pl.BlockSpec((HALO, D), _next_halo(tm, T, 0)),
                  pl.BlockSpec((tm, D), row),
                  pl.BlockSpec((tm, D), lambda i: (i, ZB_POOL)), pl.BlockSpec((HALO, D), _prev_halo(tm, ZB_POOL)),
                  pl.BlockSpec((4, POOL_GD, POOL_GD), lambda i: (0, 0, 0)), pl.BlockSpec((1, D), lambda i: (0, 0))],
        out_specs=[pl.BlockSpec((tm, D), lambda i: (i, ZB_POOL)),
                   pl.BlockSpec((4, POOL_GD, POOL_GD), lambda i: (0, 0, 0)), pl.BlockSpec((1, D), lambda i: (0, 0))],
        out_shape=[jax.ShapeDtypeStruct(dz.shape, dz.dtype), jax.ShapeDtypeStruct((4, POOL_GD, POOL_GD), F32),
                   jax.ShapeDtypeStruct((1, D), F32)],
        input_output_aliases={0: 0}, compiler_params=_cp("arbitrary"))(dz, dyp, dyp, yp_pre, z, z, pw, ps)


def _fill_ext(ext_s, halo, main, keep):
    ext_s[0:HALO, :] = halo * keep
    ext_s[HALO:, :] = main


def _lru_gates(ext_s, tm, cw, cb, wr_ref, br, wi_ref, bi, lam):
    shifted = []
    v = cb
    for k in range(4):
        zs = ext_s[pl.ds(HALO - 3 + k, tm), :]
        shifted.append(zs)
        v = v + cw[k:k + 1, :] * zs
    vb = v.astype(BF)
    rp, ip = [], []
    for h in range(LRU_HEADS):
        sl = slice(h * LRU_HD, (h + 1) * LRU_HD)
        rp.append(jnp.dot(vb[:, sl], wr_ref[h], preferred_element_type=F32))
        ip.append(jnp.dot(vb[:, sl], wi_ref[h], preferred_element_type=F32))
    r = _sigmoid(jnp.concatenate(rp, axis=1) + br)
    ig = _sigmoid(jnp.concatenate(ip, axis=1) + bi)
    sp = jnp.maximum(-lam, 0.0) + jnp.log(1.0 + jnp.exp(-jnp.abs(lam)))
    a = jnp.exp(-LRU_C * r * sp)
    om = 1.0 - a * a
    rs = lax.rsqrt(om)
    return v, vb, r, ig, a, om, rs, sp, shifted


def lru_fwd(z, cw, cb, wr, br, wi, bi, lam, wlo, *, tm, name):
    T = z.shape[0]
    tm = min(tm, T)
    nch = tm // 8

    def body(zm_ref, zh_ref, cw_ref, cb_ref, wr_ref, br_ref, wi_ref, bi_ref, lam_ref, wlo_ref, h_ref, y_ref,
             a_s, b_s, carry, ext_s):
        i = pl.program_id(0)

        @pl.when(i == 0)
        def _():
            carry[...] = jnp.zeros_like(carry)

        keep = jnp.where(i == 0, 0.0, 1.0).astype(F32)
        _fill_ext(ext_s, zh_ref[...].astype(F32), zm_ref[...].astype(F32), keep)
        v, _, _, ig, a, om, rs, _, _ = _lru_gates(ext_s, tm, cw_ref[...], cb_ref[...], wr_ref, br_ref[...], wi_ref,
                                                  bi_ref[...], lam_ref[...])
        a_s[...] = a
        b_s[...] = jnp.where(om > 0.0, om * rs, 0.0) * (ig * v)
        row = lax.broadcasted_iota(jnp.int32, (8, D), 0)

        def step(ci, hprev):
            sl = pl.ds(pl.multiple_of(ci * 8, 8), 8)
            aa, bb = a_s[sl, :], b_s[sl, :]
            for s in (1, 2, 4):
                m = row >= s
                bb = bb + aa * jnp.where(m, pltpu.roll(bb, s, axis=0), 0.0)
                aa = aa * jnp.where(m, pltpu.roll(aa, s, axis=0), 1.0)
            h = bb + aa * hprev
            h_ref[sl, :] = h
            return jnp.broadcast_to(h[7:8, :], (8, D))

        carry[...] = lax.fori_loop(0, nch, step, carry[...])
        y_ref[...] = jnp.dot(h_ref[...].astype(BF), wlo_ref[...], preferred_element_type=F32).astype(BF)

    fix2 = lambda i: (0, 0)
    fix3 = lambda i: (0, 0, 0)
    return pl.pallas_call(
        body, name=name, grid=(T // tm,),
        in_specs=[pl.BlockSpec((tm, D), lambda i: (i, ZB_LRU)), pl.BlockSpec((HALO, D), _prev_halo(tm, ZB_LRU)),
                  pl.BlockSpec((4, D), fix2), pl.BlockSpec((1, D), fix2),
                  pl.BlockSpec((LRU_HEADS, LRU_HD, LRU_HD), fix3), pl.BlockSpec((1, D), fix2),
                  pl.BlockSpec((LRU_HEADS, LRU_HD, LRU_HD), fix3), pl.BlockSpec((1, D), fix2),
                  pl.BlockSpec((1, D), fix2), pl.BlockSpec((D, D), fix2)],
        out_specs=[pl.BlockSpec((tm, D), lambda i: (i, 0)), pl.BlockSpec((tm, D), lambda i: (i, 0))],
        out_shape=[jax.ShapeDtypeStruct((T, D), F32), jax.ShapeDtypeStruct((T, D), BF)],
        scratch_shapes=[pltpu.VMEM((tm, D), F32), pltpu.VMEM((tm, D), F32), pltpu.VMEM((8, D), F32),
                        pltpu.VMEM((tm + HALO, D), F32)],
        compiler_params=_cp("arbitrary"))(z, z, cw, cb, wr, br, wi, bi, lam, wlo)


def lru_bwd(dz, dyl, z, h, cw, cb, wr, br, wi, bi, lam, wlo, *, tm, name):
    T = z.shape[0]
    tm = min(tm, T)
    nt = T // tm
    nch = tm // 8

    def body(dz_in, dy_ref, zm_ref, zh_ref, h_ref, hh_ref, cw_ref, cb_ref, wr_ref, br_ref, wi_ref, bi_ref, lam_ref,
             wlo_ref, dz_ref, dcw_ref, dcb_ref, dwr_ref, dbr_ref, dwi_ref, dbi_ref, dlam_ref, dwlo_ref,
             c_s, g_s, dh_s, dh_carry, a_ext, dv_ext, ext_s, h_ext, wlo_acc):
        del dz_in
        i = pl.program_id(0)
        ti = nt - 1 - i

        @pl.when(i == 0)
        def _():
            dh_carry[...] = jnp.zeros_like(dh_carry)
            a_ext[tm:, :] = jnp.zeros((8, D), F32)
            dv_ext[tm:, :] = jnp.zeros((HALO, D), F32)

        keep = jnp.where(ti == 0, 0.0, 1.0).astype(F32)
        _fill_ext(ext_s, zh_ref[...].astype(F32), zm_ref[...].astype(F32), keep)
        cw_ = cw_ref[...]
        lam_ = lam_ref[...]
        v, vb, r, ig, a, om, rs, sp, shifted = _lru_gates(ext_s, tm, cw_, cb_ref[...], wr_ref, br_ref[...], wi_ref,
                                                          bi_ref[...], lam_)
        mult = jnp.where(om > 0.0, om * rs, 0.0)
        a_ext[0:tm, :] = a
        c_s[...] = a_ext[pl.ds(1, tm), :]
        g_s[...] = lax.dot_general(dy_ref[...], wlo_ref[...], (((1,), (1,)), ((), ())), preferred_element_type=F32)
        row = lax.broadcasted_iota(jnp.int32, (8, D), 0)

        def step(k, nxt):
            ci = nch - 1 - k
            sl = pl.ds(pl.multiple_of(ci * 8, 8), 8)
            cc, gg = c_s[sl, :], g_s[sl, :]
            for s in (1, 2, 4):
                m = row < 8 - s
                gg = gg + cc * jnp.where(m, pltpu.roll(gg, 8 - s, axis=0), 0.0)
                cc = cc * jnp.where(m, pltpu.roll(cc, 8 - s, axis=0), 1.0)
            dh = gg + cc * nxt
            dh_s[sl, :] = dh
            return jnp.broadcast_to(dh[0:1, :], (8, D))

        dh_carry[...] = lax.fori_loop(0, nch, step, dh_carry[...])
        a_ext[tm:, :] = a[0:8, :]
        dh = dh_s[...]
        h_ext[0:8, :] = hh_ref[...] * keep
        hv = h_ref[...]
        h_ext[8:, :] = hv
        hprev = h_ext[pl.ds(7, tm), :]
        dwlo = lax.dot_general(hv.astype(BF), dy_ref[...], (((0,), (0,)), ((), ())), preferred_element_type=F32)
        iv = ig * v
        da = dh * hprev
        dmult = dh * iv
        div = dh * mult
        dlog = da * a - dmult * (a * a) * rs
        dr = dlog * (-LRU_C * sp)
        dlam = jnp.sum(dlog * r, axis=0, keepdims=True) * (LRU_C * _sigmoid(-lam_))
        di = div * v
        dv = div * ig
        drp = dr * r * (1.0 - r)
        dip = di * ig * (1.0 - ig)
        drb, dib = drp.astype(BF), dip.astype(BF)
        dvh, dwr, dwi = [], [], []
        nt_dims = (((1,), (1,)), ((), ()))
        tn_dims = (((0,), (0,)), ((), ()))
        for hd in range(LRU_HEADS):
            sl = slice(hd * LRU_HD, (hd + 1) * LRU_HD)
            dvh.append(lax.dot_general(drb[:, sl], wr_ref[hd], nt_dims, preferred_element_type=F32)
                       + lax.dot_general(dib[:, sl], wi_ref[hd], nt_dims, preferred_element_type=F32))
            dwr.append(lax.dot_general(vb[:, sl], drb[:, sl], tn_dims, preferred_element_type=F32))
            dwi.append(lax.dot_general(vb[:, sl], dib[:, sl], tn_dims, preferred_element_type=F32))
        dv = dv + jnp.concatenate(dvh, axis=1)
        dv_ext[0:tm, :] = dv
        dzl = cw_[3:4, :] * dv
        for k in range(3):
            dzl = dzl + cw_[k:k + 1, :] * dv_ext[pl.ds(3 - k, tm), :]
        dz_ref[...] = dzl.astype(dz_ref.dtype)
        dv_ext[tm:, :] = dv[:HALO]
        dcw = jnp.concatenate([jnp.sum(dv * shifted[k], axis=0, keepdims=True) for k in range(4)], axis=0)
        dcb = jnp.sum(dv, axis=0, keepdims=True)
        dbr = jnp.sum(drp, axis=0, keepdims=True)
        dbi = jnp.sum(dip, axis=0, keepdims=True)

        @pl.when(i == 0)
        def _():
            dcw_ref[...] = dcw
            dcb_ref[...] = dcb
            dbr_ref[...] = dbr
            dbi_ref[...] = dbi
            dlam_ref[...] = dlam
            wlo_acc[...] = dwlo
            for hd in range(LRU_HEADS):
                dwr_ref[hd] = dwr[hd]
                dwi_ref[hd] = dwi[hd]

        @pl.when(i > 0)
        def _():
            dcw_ref[...] += dcw
            dcb_ref[...] += dcb
            dbr_ref[...] += dbr
            dbi_ref[...] += dbi
            dlam_ref[...] += dlam
            wlo_acc[...] += dwlo
            for hd in range(LRU_HEADS):
                dwr_ref[hd] += dwr[hd]
                dwi_ref[hd] += dwi[hd]

        @pl.when(i == nt - 1)
        def _():
            dwlo_ref[...] = wlo_acc[...].astype(BF)

    fix2 = lambda i: (0, 0)
    fix3 = lambda i: (0, 0, 0)
    rev = lambda i: (nt - 1 - i, 0)
    vec = pl.BlockSpec((1, D), fix2)
    hw = pl.BlockSpec((LRU_HEADS, LRU_HD, LRU_HD), fix3)
    return pl.pallas_call(
        body, name=name, grid=(nt,),
        in_specs=[pl.BlockSpec(memory_space=pl.ANY),
                  pl.BlockSpec((tm, D), rev),
                  pl.BlockSpec((tm, D), lambda i: (nt - 1 - i, ZB_LRU)),
                  pl.BlockSpec((HALO, D), lambda i: (jnp.maximum((nt - 1 - i) * (tm // HALO) - 1, 0), ZB_LRU)),
                  pl.BlockSpec((tm, D), rev),
                  pl.BlockSpec((8, D), lambda i: (jnp.maximum((nt - 1 - i) * (tm // 8) - 1, 0), 0)),
                  pl.BlockSpec((4, D), fix2), vec, hw, vec, hw, vec, vec, pl.BlockSpec((D, D), fix2)],
        out_specs=[pl.BlockSpec((tm, D), lambda i: (nt - 1 - i, ZB_LRU)),
                   pl.BlockSpec((4, D), fix2), vec, hw, vec, hw, vec, vec, pl.BlockSpec((D, D), fix2)],
        out_shape=[jax.ShapeDtypeStruct(dz.shape, dz.dtype), jax.ShapeDtypeStruct((4, D), F32),
                   jax.ShapeDtypeStruct((1, D), F32), jax.ShapeDtypeStruct((LRU_HEADS, LRU_HD, LRU_HD), F32),
                   jax.ShapeDtypeStruct((1, D), F32), jax.ShapeDtypeStruct((LRU_HEADS, LRU_HD, LRU_HD), F32),
                   jax.ShapeDtypeStruct((1, D), F32), jax.ShapeDtypeStruct((1, D), F32),
                   jax.ShapeDtypeStruct((D, D), BF)],
        scratch_shapes=[pltpu.VMEM((tm, D), F32), pltpu.VMEM((tm, D), F32), pltpu.VMEM((tm, D), F32),
                        pltpu.VMEM((8, D), F32), pltpu.VMEM((tm + 8, D), F32), pltpu.VMEM((tm + HALO, D), F32),
                        pltpu.VMEM((tm + HALO, D), F32), pltpu.VMEM((tm + 8, D), F32), pltpu.VMEM((D, D), F32)],
        input_output_aliases={0: 0},
        compiler_params=_cp("arbitrary"))(dz, dyl, z, z, h, h, cw, cb, wr, br, wi, bi, lam, wlo)


def _sconv_cv(u_ext, sw):
    shifted = []
    cv = None
    for k in range(3):
        us = (u_ext if k == 2 else pltpu.roll(u_ext, 2 - k, axis=0))[HALO:]
        shifted.append(us)
        term = sw[k:k + 1, :] * us
        cv = term if cv is None else cv + term
    return cv, shifted


def sconv_fwd(z, sw, wso, *, tm, name):
    T = z.shape[0]
    tm = min(tm, T)

    def body(zm_ref, zh_ref, sw_ref, wso_ref, s_ref, y_ref):
        i = pl.program_id(0)
        keep = jnp.where(i == 0, 0.0, 1.0).astype(F32)
        zm = zm_ref[...].astype(F32)
        zh = zh_ref[...].astype(F32)
        u_ext = jnp.concatenate([zh[:, D:2 * D] * zh[:, 2 * D:] * keep, zm[:, D:2 * D] * zm[:, 2 * D:]], axis=0)
        cv, _ = _sconv_cv(u_ext, sw_ref[...])
        s = (zm[:, :D] * cv).astype(BF)
        s_ref[...] = s
        y_ref[...] = jnp.dot(s, wso_ref[...], preferred_element_type=F32).astype(BF)

    return pl.pallas_call(
        body, name=name, grid=(T // tm,),
        in_specs=[pl.BlockSpec((tm, 3 * D), lambda i: (i, ZB_SCONV)),
                  pl.BlockSpec((HALO, 3 * D), _prev_halo(tm, ZB_SCONV)),
                  pl.BlockSpec((3, D), lambda i: (0, 0)), pl.BlockSpec((D, D), lambda i: (0, 0))],
        out_specs=[pl.BlockSpec((tm, D), lambda i: (i, 0)), pl.BlockSpec((tm, D), lambda i: (i, 0))],
        out_shape=[jax.ShapeDtypeStruct((T, D), BF), jax.ShapeDtypeStruct((T, D), BF)],
        compiler_params=_cp("arbitrary"))(z, z, sw, wso)


def sconv_bwd(dz, dyc, z, sw, wso, *, tm, name):
    T = z.shape[0]
    tm = min(tm, T)
    nt = T // tm

    def body(dz_in, dy_ref, dyn_ref, zm_ref, zp_ref, zn_ref, sw_ref, wso_ref, dz_ref, dsw_ref, dwso_ref, acc_ref):
        del dz_in
        i = pl.program_id(0)
        keep_p = jnp.where(i == 0, 0.0, 1.0).astype(F32)
        keep_n = jnp.where(i == nt - 1, 0.0, 1.0).astype(F32)
        sw_ = sw_ref[...]
        zm = zm_ref[...].astype(F32)
        zp = zp_ref[...].astype(F32)
        zb, zc, zh = zm[:, :D], zm[:, D:2 * D], zm[:, 2 * D:]
        u_ext = jnp.concatenate([zp[:, D:2 * D] * zp[:, 2 * D:] * keep_p, zc * zh], axis=0)
        cv, shifted = _sconv_cv(u_ext, sw_)
        dy_ext = jnp.concatenate([dy_ref[...], dyn_ref[...]], axis=0)
        ds_ext = lax.dot_general(dy_ext, wso_ref[...], (((1,), (1,)), ((), ())), preferred_element_type=F32)
        zb_ext = jnp.concatenate([zb, zn_ref[...][:, :D].astype(F32) * keep_n], axis=0)
        dcv_ext = ds_ext * zb_ext
        du = sw_[2:3, :] * dcv_ext[:tm]
        for k in range(2):
            du = du + sw_[k:k + 1, :] * pltpu.roll(dcv_ext, tm + HALO - (2 - k), axis=0)[:tm]
        dz_ref[...] = jnp.concatenate([ds_ext[:tm] * cv, du * zh, du * zc], axis=1).astype(dz_ref.dtype)
        dcv = dcv_ext[:tm]
        dsw = jnp.concatenate([jnp.sum(dcv * shifted[k], axis=0, keepdims=True) for k in range(3)], axis=0)
        dwso = lax.dot_general((zb * cv).astype(BF), dy_ref[...], (((0,), (0,)), ((), ())),
                               preferred_element_type=F32)

        @pl.when(i == 0)
        def _():
            dsw_ref[...] = dsw
            acc_ref[...] = dwso

        @pl.when(i > 0)
        def _():
            dsw_ref[...] += dsw
            acc_ref[...] += dwso

        @pl.when(i == nt - 1)
        def _():
            dwso_ref[...] = acc_ref[...].astype(BF)

    return pl.pallas_call(
        body, name=name, grid=(nt,),
        in_specs=[pl.BlockSpec(memory_space=pl.ANY),
                  pl.BlockSpec((tm, D), lambda i: (i, 0)), pl.BlockSpec((HALO, D), _next_halo(tm, T, 0)),
                  pl.BlockSpec((tm, 3 * D), lambda i: (i, ZB_SCONV)),
                  pl.BlockSpec((HALO, 3 * D), _prev_halo(tm, ZB_SCONV)),
                  pl.BlockSpec((HALO, 3 * D), _next_halo(tm, T, ZB_SCONV)),
                  pl.BlockSpec((3, D), lambda i: (0, 0)), pl.BlockSpec((D, D), lambda i: (0, 0))],
        out_specs=[pl.BlockSpec((tm, 3 * D), lambda i: (i, ZB_SCONV)), pl.BlockSpec((3, D), lambda i: (0, 0)),
                   pl.BlockSpec((D, D), lambda i: (0, 0))],
        out_shape=[jax.ShapeDtypeStruct(dz.shape, dz.dtype), jax.ShapeDtypeStruct((3, D), F32),
                   jax.ShapeDtypeStruct((D, D), BF)],
        scratch_shapes=[pltpu.VMEM((D, D), F32)],
        input_output_aliases={0: 0}, compiler_params=_cp("arbitrary"))(dz, dyc, dyc, z, z, z, sw, wso)


def merge_fwd(z, yp_pre, yl, yc, ps, *, tm, name):
    T = z.shape[0]
    tm = min(tm, T)

    def body(zg_ref, yp_ref, yl_ref, yc_ref, ps_ref, o_ref):
        gts = _sigmoid(zg_ref[...].astype(F32))
        m = (gts[:, :D] * (yp_ref[...].astype(F32) * ps_ref[...]) + gts[:, D:2 * D] * yl_ref[...].astype(F32)
             + gts[:, 2 * D:] * yc_ref[...].astype(F32))
        o_ref[...] = m.astype(o_ref.dtype)

    row = lambda i: (i, 0)
    return pl.pallas_call(
        body, name=name, grid=(T // tm,),
        in_specs=[pl.BlockSpec((tm, 3 * D), lambda i: (i, ZB_GATE)), pl.BlockSpec((tm, D), row),
                  pl.BlockSpec((tm, D), row), pl.BlockSpec((tm, D), row), pl.BlockSpec((1, D), lambda i: (0, 0))],
        out_specs=pl.BlockSpec((tm, D), row), out_shape=jax.ShapeDtypeStruct((T, D), BF),
        compiler_params=_cp("arbitrary"))(z, yp_pre, yl, yc, ps)


def merge_bwd(dm, z, yp_pre, yl, yc, ps, *, tm, name):
    T = z.shape[0]
    tm = min(tm, T)

    def body(dm_ref, zg_ref, yp_ref, yl_ref, yc_ref, ps_ref, dz_ref, dyp_ref, dyl_ref, dyc_ref):
        gts = _sigmoid(zg_ref[...].astype(F32))
        dmv = dm_ref[...].astype(F32)
        ys = (yp_ref[...].astype(F32) * ps_ref[...], yl_ref[...].astype(F32), yc_ref[...].astype(F32))
        outs = (dyp_ref, dyl_ref, dyc_ref)
        dgs = []
        for j in range(3):
            gj = gts[:, j * D:(j + 1) * D]
            outs[j][...] = (dmv * gj).astype(BF)
            dgs.append(dmv * ys[j] * gj * (1.0 - gj))
        dz_ref[...] = jnp.concatenate(dgs, axis=1).astype(dz_ref.dtype)

    row = lambda i: (i, 0)
    return pl.pallas_call(
        body, name=name, grid=(T // tm,),
        in_specs=[pl.BlockSpec((tm, D), row), pl.BlockSpec((tm, 3 * D), lambda i: (i, ZB_GATE)),
                  pl.BlockSpec((tm, D), row), pl.BlockSpec((tm, D), row), pl.BlockSpec((tm, D), row),
                  pl.BlockSpec((1, D), lambda i: (0, 0))],
        out_specs=[pl.BlockSpec((tm, 3 * D), lambda i: (i, ZB_GATE)), pl.BlockSpec((tm, D), row),
                   pl.BlockSpec((tm, D), row), pl.BlockSpec((tm, D), row)],
        out_shape=[jax.ShapeDtypeStruct((T, IN_COLS), BF), jax.ShapeDtypeStruct((T, D), BF),
                   jax.ShapeDtypeStruct((T, D), BF), jax.ShapeDtypeStruct((T, D), BF)],
        compiler_params=_cp("arbitrary"))(dm, z, yp_pre, yl, yc, ps)


def _attn_probs(qh, kh):
    s = lax.dot_general(qh, kh, (((1,), (1,)), ((), ())), preferred_element_type=F32) * (X_HD ** -0.5)
    e = jnp.exp(s - jnp.max(s, axis=-1, keepdims=True))
    return e / jnp.sum(e, axis=-1, keepdims=True)


def attn_fwd(xb, wq, kb, vb, *, tm, name):
    T = xb.shape[0]
    tm = min(tm, T)

    def body(x_ref, wq_ref, k_ref, v_ref, q_ref, o_ref):
        q = jnp.dot(x_ref[...], wq_ref[...], preferred_element_type=F32).astype(BF)
        q_ref[...] = q
        outs = []
        for h in range(X_HEADS):
            sl = slice(h * X_HD, (h + 1) * X_HD)
            p = _attn_probs(q[:, sl], k_ref[:, sl])
            outs.append(jnp.dot(p.astype(BF), v_ref[:, sl], preferred_element_type=F32))
        o_ref[...] = jnp.concatenate(outs, axis=1).astype(BF)

    row = lambda i: (i, 0)
    fix = lambda i: (0, 0)
    return pl.pallas_call(
        body, name=name, grid=(T // tm,),
        in_specs=[pl.BlockSpec((tm, D), row), pl.BlockSpec((D, D), fix), pl.BlockSpec((N_MEM, D), fix),
                  pl.BlockSpec((N_MEM, D), fix)],
        out_specs=[pl.BlockSpec((tm, D), row), pl.BlockSpec((tm, D), row)],
        out_shape=[jax.ShapeDtypeStruct((T, D), BF), jax.ShapeDtypeStruct((T, D), BF)],
        compiler_params=_cp("arbitrary"))(xb, wq, kb, vb)


def attn_bwd(dxa, wo, q, kb, vb, xb, *, tm, name):
    T = q.shape[0]
    tm = min(tm, T)
    nt = T // tm

    def body(d_ref, wo_ref, q_ref, k_ref, v_ref, x_ref, dq_ref, dk_ref, dv_ref, dwq_ref, acc_ref):
        i = pl.program_id(0)
        do = lax.dot_general(d_ref[...], wo_ref[...], (((1,), (1,)), ((), ())),
                             preferred_element_type=F32).astype(BF)
        q = q_ref[...]
        dqs, dks, dvs = [], [], []
        for h in range(X_HEADS):
            sl = slice(h * X_HD, (h + 1) * X_HD)
            kh, vh = k_ref[:, sl], v_ref[:, sl]
            p = _attn_probs(q[:, sl], kh)
            dp = lax.dot_general(do[:, sl], vh, (((1,), (1,)), ((), ())), preferred_element_type=F32)
            ds = (p * (dp - jnp.sum(dp * p, axis=-1, keepdims=True)) * (X_HD ** -0.5)).astype(BF)
            dqs.append(jnp.dot(ds, kh, preferred_element_type=F32))
            dks.append(lax.dot_general(ds, q[:, sl], (((0,), (0,)), ((), ())), preferred_element_type=F32))
            dvs.append(lax.dot_general(p.astype(BF), do[:, sl], (((0,), (0,)), ((), ())),
                                       preferred_element_type=F32))
        dqb = jnp.concatenate(dqs, axis=1).astype(BF)
        dq_ref[...] = dqb
        dk = jnp.concatenate(dks, axis=1)
        dv = jnp.concatenate(dvs, axis=1)
        dwq = lax.dot_general(x_ref[...], dqb, (((0,), (0,)), ((), ())), preferred_element_type=F32)

        @pl.when(i == 0)
        def _():
            dk_ref[...] = dk
            dv_ref[...] = dv
            acc_ref[...] = dwq

        @pl.when(i > 0)
        def _():
            dk_ref[...] += dk
            dv_ref[...] += dv
            acc_ref[...] += dwq

        @pl.when(i == nt - 1)
        def _():
            dwq_ref[...] = acc_ref[...].astype(BF)

    row = lambda i: (i, 0)
    fix = lambda i: (0, 0)
    return pl.pallas_call(
        body, name=name, grid=(nt,),
        in_specs=[pl.BlockSpec((tm, D), row), pl.BlockSpec((D, D), fix), pl.BlockSpec((tm, D), row),
                  pl.BlockSpec((N_MEM, D), fix), pl.BlockSpec((N_MEM, D), fix), pl.BlockSpec((tm, D), row)],
        out_specs=[pl.BlockSpec((tm, D), row), pl.BlockSpec((N_MEM, D), fix), pl.BlockSpec((N_MEM, D), fix),
                   pl.BlockSpec((D, D), fix)],
        out_shape=[jax.ShapeDtypeStruct((T, D), BF), jax.ShapeDtypeStruct((N_MEM, D), F32),
                   jax.ShapeDtypeStruct((N_MEM, D), F32), jax.ShapeDtypeStruct((D, D), BF)],
        scratch_shapes=[pltpu.VMEM((D, D), F32)],
        compiler_params=_cp("arbitrary"))(dxa, wo, q, kb, vb, xb)


def swiglu_fwd(gu, *, tm, name):
    T = gu.shape[0]
    tm = min(tm, T)

    def body(g_ref, u_ref, o_ref):
        g = g_ref[...].astype(F32)
        o_ref[...] = (g * _sigmoid(g) * u_ref[...].astype(F32)).astype(BF)

    return pl.pallas_call(
        body, name=name, grid=(T // tm,),
        in_specs=[pl.BlockSpec((tm, D_FF), lambda i: (i, 0)), pl.BlockSpec((tm, D_FF), lambda i: (i, 1))],
        out_specs=pl.BlockSpec((tm, D_FF), lambda i: (i, 0)), out_shape=jax.ShapeDtypeStruct((T, D_FF), BF),
        compiler_params=_cp("arbitrary"))(gu, gu)


def swiglu_bwd(dh, gu, *, tm, name):
    T = gu.shape[0]
    tm = min(tm, T)

    def body(dh_ref, g_ref, u_ref, o_ref):
        g = g_ref[...].astype(F32)
        u = u_ref[...].astype(F32)
        dhv = dh_ref[...].astype(F32)
        sg = _sigmoid(g)
        o_ref[:, :D_FF] = (dhv * u * sg * (1.0 + g * (1.0 - sg))).astype(BF)
        o_ref[:, D_FF:] = (dhv * g * sg).astype(BF)

    return pl.pallas_call(
        body, name=name, grid=(T // tm,),
        in_specs=[pl.BlockSpec((tm, D_FF), lambda i: (i, 0)), pl.BlockSpec((tm, D_FF), lambda i: (i, 0)),
                  pl.BlockSpec((tm, D_FF), lambda i: (i, 1))],
        out_specs=pl.BlockSpec((tm, 2 * D_FF), lambda i: (i, 0)),
        out_shape=jax.ShapeDtypeStruct((T, 2 * D_FF), BF), compiler_params=_cp("arbitrary"))(dh, gu, gu)


TM_MM = 1024
TM_EW = 512
TM_SEQ = 512
TT_DW = 2048


def _mem_kv(l, memb, W):
    kb = mm_nn(memb, W['xa_w_k'][l], None, out_dtype=BF, tm=N_MEM, tn=1024, name=f"l{l}_mem_k")
    vb = mm_nn(memb, W['xa_w_v'][l], None, out_dtype=BF, tm=N_MEM, tn=1024, name=f"l{l}_mem_v")
    return kb, vb


def _layer_fwd(l, x, xb, kb, vb, W, host=None, host2=None, after_in_proj=None):
    n = f"l{l}_"
    sv = {'x0': x if xb is None else xb}
    z = mm_nn(sv['x0'], W['w_in'][l], W['b_in'][l], out_dtype=BF, tm=TM_MM, tn=1024, name=n + "in_proj", host=host)
    if host is not None:
        z, sv['hosted'] = z
        if after_in_proj is not None:
            after_in_proj(sv['hosted'])
    if kb is None:
        kb, vb = _mem_kv(l, vb, W)
    sv['kb'], sv['vb'] = kb, vb
    yp = pool_fwd(z, W['pool_w'][l], tm=TM_SEQ, name=n + "pool_fwd")
    h, yl = lru_fwd(z, W['lru_conv_w'][l], W['lru_conv_b'][l], W['lru_w_r'][l], W['lru_b_r'][l], W['lru_w_i'][l],
                    W['lru_b_i'][l], W['lru_lambda'][l], W['lru_w_out'][l], tm=TM_SEQ, name=n + "lru_fwd")
    s, yc = sconv_fwd(z, W['sconv_w'][l], W['sconv_w_out'][l], tm=TM_SEQ, name=n + "sconv_fwd")
    merged = merge_fwd(z, yp, yl, yc, W['pool_scale'][l], tm=TM_EW, name=n + "merge_fwd")
    x1, x1b, xh1, rs1 = mm_res_ln(merged, W['w_mix_out'][l], x, W['ln_g'][l][0:1], W['ln_b'][l][0:1], tm=TM_EW,
                                  name=n + "mix_out_ln")
    q, o = attn_fwd(x1b, W['xa_w_q'][l], kb, vb, tm=TM_EW, name=n + "attn_fwd")
    x2, x2b, xh2, rs2 = mm_res_ln(o, W['xa_w_o'][l], x1, W['ln_g'][l][1:2], W['ln_b'][l][1:2], tm=TM_EW,
                                  name=n + "attn_out_ln")
    gu = mm_nn(x2b, W['ffn_w_gu'][l], None, out_dtype=BF, tm=TM_MM, tn=1408, name=n + "ffn_in", host=host2)
    if host2 is not None:
        gu, sv['hosted2'] = gu
    hdn = swiglu_fwd(gu, tm=TM_EW, name=n + "swiglu_fwd")
    x3, x3b, xh3, rs3 = mm_res_ln(hdn, W['ffn_w_down'][l], x2, W['ln_g'][l][2:3], W['ln_b'][l][2:3], tm=TM_EW,
                                  name=n + "ffn_out_ln")
    sv.update(z=z, yp=yp, h=h, yl=yl, s=s, yc=yc, merged=merged, x1b=x1b, xh1=xh1, rs1=rs1, q=q, o=o, x2b=x2b,
              xh2=xh2, rs2=rs2, gu=gu, hdn=hdn, xh3=xh3, rs3=rs3)
    return x3, x3b, sv


def _layer_bwd(l, dx3, sv, memb, kb, vb, W, loss_from=None, host=None, last_host_fn=None):
    n = f"l{l}_"
    G = {}
    res = ln_bwd(dx3, sv['xh3'], sv['rs3'], W['ln_g'][l][2:3], tm=TM_EW, name=n + "ln3_bwd", loss_from=loss_from,
                 dw_of=sv['hdn'])
    dp3, dp3b, dg3, db3 = res[:4]
    loss = res[4] if loss_from is not None else None
    G['ffn_w_down'] = res[-1]
    dhdn = mm_nt(dp3b, W['ffn_w_down'][l], None, out_dtype=BF, tm=TM_EW, tc=D, name=n + "ffn_down_dx")
    dgu = swiglu_bwd(dhdn, sv['gu'], tm=TM_EW, name=n + "swiglu_bwd")
    dx2 = mm_nt(dgu, W['ffn_w_gu'][l], dp3, out_dtype=F32, tm=TM_MM, tc=1408, name=n + "ffn_in_dx", host=host)
    if host is not None:
        dx2, G['hosted'] = dx2
    G['ffn_w_gu'] = mm_tn(sv['x2b'], dgu, out_dtype=BF, tk=1024, tn=1408, tt=TT_DW,name=n + "ffn_in_dw")

    dp2, dp2b, dg2, db2, G['xa_w_o'] = ln_bwd(dx2, sv['xh2'], sv['rs2'], W['ln_g'][l][1:2], tm=TM_EW,
                                              name=n + "ln2_bwd", dw_of=sv['o'])
    dq, dk, dv, G['xa_w_q'] = attn_bwd(dp2b, W['xa_w_o'][l], sv['q'], kb, vb, sv['x1b'], tm=TM_EW,
                                       name=n + "attn_bwd")
    dx1 = mm_nt(dq, W['xa_w_q'][l], dp2, out_dtype=F32, tm=TM_MM, tc=D, name=n + "attn_q_dx")
    G['xa_w_k'] = mm_tn(memb, dk, out_dtype=BF, tk=1024, tn=1024, tt=N_MEM, name=n + "attn_k_dw")
    G['xa_w_v'] = mm_tn(memb, dv, out_dtype=BF, tk=1024, tn=1024, tt=N_MEM, name=n + "attn_v_dw")

    dp1, dp1b, dg1, db1, G['w_mix_out'] = ln_bwd(dx1, sv['xh1'], sv['rs1'], W['ln_g'][l][0:1], tm=TM_EW,
                                                 name=n + "ln1_bwd", dw_of=sv['merged'])
    dmerged = mm_nt(dp1b, W['w_mix_out'][l], None, out_dtype=BF, tm=TM_MM, tc=D, name=n + "mix_out_dx")
    z = sv['z']
    dz, dyp, dyl, dyc = merge_bwd(dmerged, z, sv['yp'], sv['yl'], sv['yc'], W['pool_scale'][l], tm=TM_EW,
                                  name=n + "merge_bwd")
    dz, G['pool_w'], G['pool_scale'] = pool_bwd(dz, dyp, sv['yp'], z, W['pool_w'][l], W['pool_scale'][l],
                                                tm=TM_SEQ, name=n + "pool_bwd")
    (dz, G['lru_conv_w'], G['lru_conv_b'], G['lru_w_r'], G['lru_b_r'], G['lru_w_i'], G['lru_b_i'],
     G['lru_lambda'], G['lru_w_out']) = lru_bwd(dz, dyl, z, sv['h'], W['lru_conv_w'][l], W['lru_conv_b'][l],
                                                W['lru_w_r'][l], W['lru_b_r'][l], W['lru_w_i'][l], W['lru_b_i'][l],
                                                W['lru_lambda'][l], W['lru_w_out'][l], tm=TM_SEQ, name=n + "lru_bwd")
    dz, G['sconv_w'], G['sconv_w_out'] = sconv_bwd(dz, dyc, z, W['sconv_w'][l], W['sconv_w_out'][l], tm=TM_SEQ,
                                                   name=n + "sconv_bwd")
    G['w_in'], G['b_in'] = mm_tn(sv['x0'], dz, out_dtype=BF, tk=1024, tn=1024, tt=TT_DW,name=n + "in_proj_dw",
                                 colsum=True)
    G['ln_g'] = jnp.concatenate([dg1, dg2, dg3], axis=0)
    G['ln_b'] = jnp.concatenate([db1, db2, db3], axis=0)
    last_host = None if last_host_fn is None else last_host_fn(G)
    dx0 = mm_nt(dz, W['w_in'][l], dp1, out_dtype=F32, tm=TM_MM, tc=2048, name=n + "in_proj_dx", host=last_host)
    if last_host is not None:
        dx0, G['hosted_last'] = dx0
    return dx0, G, loss


def local_step(x, mem, target, W):
    memb = mem.astype(BF)
    saves, kvs = [], []
    xf, xb = x, None
    for l in range(DEPTH):
        kb = mm_nn(memb, W['xa_w_k'][l], None, out_dtype=BF, tm=N_MEM, tn=1024, name=f"l{l}_mem_k")
        vb = mm_nn(memb, W['xa_w_v'][l], None, out_dtype=BF, tm=N_MEM, tn=1024, name=f"l{l}_mem_v")
        xf, xb, sv = _layer_fwd(l, xf, xb, kb, vb, W)
        saves.append(sv)
        kvs.append((kb, vb))
    grads = [None] * DEPTH
    dx, loss = None, None
    for l in reversed(range(DEPTH)):
        lf = (W['ln_b'][l][2:3], target) if l == DEPTH - 1 else None
        dx, grads[l], ls = _layer_bwd(l, dx, saves[l], memb, kvs[l][0], kvs[l][1], W, loss_from=lf)
        if ls is not None:
            loss = ls
    return loss, dx, grads


def _coords():
    return lax.axis_index("x"), lax.axis_index("y"), lax.axis_index("c")


FLIPS = ((1, 0), (0, 1), (1, 1))
SQUARES = ('lru_w_out', 'sconv_w_out', 'w_mix_out', 'xa_w_q', 'xa_w_k', 'xa_w_v', 'xa_w_o')
LAYER_SHAPE = {'w_in': (D, IN_COLS), 'pool_w': (4, POOL_GD, POOL_GD), 'ffn_w_gate': (4, D, D_FF // 4),
               'ffn_w_up': (4, D, D_FF // 4), 'ffn_w_down': (D_FF, D), **{n: (D, D) for n in SQUARES}}
PIECES = ('w_in', 'pool_w') + SQUARES + ('ffn_w_gate', 'ffn_w_up', 'ffn_w_down')


def _mult(v, m):
    return v if isinstance(v, int) else pl.multiple_of(v, m)


def _win(name, ref, k):
    if name == 'w_in':
        return ref.at[:, pl.ds(_mult(((2 * k + 6) % 8) * D, D), 2 * D)]
    if name == 'pool_w':
        return ref.at[:, pl.ds(_mult(k * (POOL_GD // 4), POOL_GD // 4), POOL_GD // 4), :]
    if name in ('ffn_w_gate', 'ffn_w_up'):
        return ref.at[k]
    rows = LAYER_SHAPE[name][0] // 4
    return ref.at[pl.ds(_mult(k * rows, 16), rows), :]


def _half_shape(name):
    shard = _shard_shape(name)
    return (shard[0] // 2,) + shard[1:]


def _half(name, ref, h):
    rows = _shard_shape(name)[0] // 2
    if name == 'pool_w':
        return ref.at[pl.ds(h * rows, rows)]
    return ref.at[pl.ds(_mult(h * rows, 16), rows), :]


def gather_weights(shards, small):
    n_p = len(PIECES)

    def body(*refs):
        srcs = dict(zip(PIECES, refs[:n_p]))
        small_ref = refs[n_p]
        outs = [dict(zip(PIECES, refs[n_p + 1 + l * n_p:n_p + 1 + (l + 1) * n_p])) for l in range(DEPTH)]
        gs_ref = refs[n_p + 1 + DEPTH * n_p]
        ici_send, ici_recv, d2d_send, d2d_recv, own_send, own_recv = refs[n_p + 2 + DEPTH * n_p:]
        x, y, c = _coords()
        me = 2 * x + y
        sib = (x, y, 1 - c)

        def own_copies():
            cps = []
            li = 0
            for n in PIECES:
                for l in range(DEPTH):
                    cps.append(pltpu.make_async_remote_copy(
                        src_ref=srcs[n].at[l], dst_ref=_win(n, outs[l][n], me), send_sem=own_send.at[li],
                        recv_sem=own_recv.at[li], device_id=sib, device_id_type=MESH))
                    li += 1
            cps.append(pltpu.make_async_remote_copy(
                src_ref=small_ref, dst_ref=gs_ref.at[me], send_sem=own_send.at[li], recv_sem=own_recv.at[li],
                device_id=sib, device_id_type=MESH))
            return cps

        def run(lc):
            sends = []
            for j, (fx, fy) in enumerate(FLIPS):
                peer = (x ^ fx, y ^ fy, c)
                for p, n in enumerate(PIECES):
                    k = 3 * p + j
                    sends.append(pltpu.make_async_remote_copy(
                        src_ref=srcs[n].at[lc], dst_ref=_win(n, outs[lc][n], me), send_sem=ici_send.at[k],
                        recv_sem=ici_recv.at[k], device_id=peer, device_id_type=MESH))
                k = 3 * n_p + j
                sends.append(pltpu.make_async_remote_copy(
                    src_ref=small_ref, dst_ref=gs_ref.at[me], send_sem=ici_send.at[k], recv_sem=ici_recv.at[k],
                    device_id=peer, device_id_type=MESH))
            own = own_copies()
            for cp in sends + own:
                cp.start()
            for j, (fx, fy) in enumerate(FLIPS):
                other = 2 * (x ^ fx) + (y ^ fy)
                for p, n in enumerate(PIECES):
                    k = 3 * p + j
                    w = _win(n, outs[lc][n], other)
                    pltpu.make_async_remote_copy(src_ref=srcs[n].at[lc], dst_ref=w, send_sem=ici_send.at[k],
                                                 recv_sem=ici_recv.at[k], device_id=sib,
                                                 device_id_type=MESH).wait_recv()
                    fw = pltpu.make_async_remote_copy(src_ref=w, dst_ref=w, send_sem=d2d_send.at[k],
                                                      recv_sem=d2d_recv.at[k], device_id=sib, device_id_type=MESH)
                    fw.start()
                    sends.append(fw)
                k = 3 * n_p + j
                pltpu.make_async_remote_copy(src_ref=small_ref, dst_ref=gs_ref.at[other], send_sem=ici_send.at[k],
                                             recv_sem=ici_recv.at[k], device_id=sib, device_id_type=MESH).wait_recv()
            for j, (fx, fy) in enumerate(FLIPS):
                other = 2 * (x ^ fx) + (y ^ fy)
                for p, n in enumerate(PIECES):
                    k = 3 * p + j
                    w = _win(n, outs[1 - lc][n], other)
                    pltpu.make_async_remote_copy(src_ref=w, dst_ref=w, send_sem=d2d_send.at[k],
                                                 recv_sem=d2d_recv.at[k], device_id=sib,
                                                 device_id_type=MESH).wait_recv()
            for cp in sends:
                cp.wait_send()
            for cp in own:
                cp.wait()

        @pl.when(c == 0)
        def _():
            run(0)

        @pl.when(c == 1)
        def _():
            run(1)

    hbm = pl.BlockSpec(memory_space=pl.ANY)
    n_out = DEPTH * n_p + 1
    res = pl.pallas_call(
        body, name="gather_weights", in_specs=[hbm] * (n_p + 1), out_specs=[hbm] * n_out,
        out_shape=[jax.ShapeDtypeStruct(LAYER_SHAPE[n], BF) for _ in range(DEPTH) for n in PIECES]
        + [jax.ShapeDtypeStruct((4,) + small.shape, small.dtype)],
        scratch_shapes=[pltpu.SemaphoreType.DMA((3 * n_p + 3,)), pltpu.SemaphoreType.DMA((3 * n_p + 3,)),
                        pltpu.SemaphoreType.DMA((3 * n_p,)), pltpu.SemaphoreType.DMA((3 * n_p,)),
                        pltpu.SemaphoreType.DMA((DEPTH * n_p + 1,)), pltpu.SemaphoreType.DMA((DEPTH * n_p + 1,))],
    )(*[shards[n] for n in PIECES], small)
    full = {n: [res[l * n_p + p] for l in range(DEPTH)] for p, n in enumerate(PIECES)}
    return full, res[DEPTH * n_p]


def split_layers(p0, p1):
    n = len(p0)

    def body(*refs):
        a0, a1 = refs[:n], refs[n:2 * n]
        theirs = refs[2 * n:3 * n]
        send_sems, recv_sems = refs[3 * n:]
        x, y, c = _coords()
        sib = (x, y, 1 - c)

        def give(arrs):
            for i in range(n):
                pltpu.make_async_remote_copy(src_ref=arrs[i], dst_ref=theirs[i], send_sem=send_sems.at[i],
                                             recv_sem=recv_sems.at[i], device_id=sib, device_id_type=MESH).start()

        @pl.when(c == 0)
        def _():
            give(a1)

        @pl.when(c == 1)
        def _():
            give(a0)

        for i in range(n):
            pltpu.make_async_remote_copy(src_ref=a0[i], dst_ref=theirs[i], send_sem=send_sems.at[i],
                                         recv_sem=recv_sems.at[i], device_id=sib, device_id_type=MESH).wait()

    hbm = pl.BlockSpec(memory_space=pl.ANY)
    return pl.pallas_call(
        body, name="split_layers", in_specs=[hbm] * (2 * n), out_specs=[hbm] * n,
        out_shape=[jax.ShapeDtypeStruct(a.shape, a.dtype) for a in p0],
        scratch_shapes=[pltpu.SemaphoreType.DMA((n,)), pltpu.SemaphoreType.DMA((n,))],
    )(*p0, *p1)


def exchange_chips(q, qsmall):
    n_p = len(PIECES)

    def body(*refs):
        srcs = dict(zip(PIECES, refs[:n_p]))
        s_ref = refs[n_p]
        outs = dict(zip(PIECES, refs[n_p + 1:2 * n_p + 1]))
        so_ref = refs[2 * n_p + 1]
        send_sems, recv_sems = refs[2 * n_p + 2:]
        x, y, c = _coords()
        me = 2 * x + y
        sends = []
        for j, (fx, fy) in enumerate(FLIPS):
            peer = (x ^ fx, y ^ fy, c)
            other = 2 * (x ^ fx) + (y ^ fy)
            for p, n in enumerate(PIECES):
                k = 3 * p + j
                sends.append(pltpu.make_async_remote_copy(
                    src_ref=_win(n, srcs[n], other), dst_ref=outs[n].at[j], send_sem=send_sems.at[k],
                    recv_sem=recv_sems.at[k], device_id=peer, device_id_type=MESH))
            k = 3 * n_p + j
            sends.append(pltpu.make_async_remote_copy(
                src_ref=s_ref, dst_ref=so_ref.at[j], send_sem=send_sems.at[k], recv_sem=recv_sems.at[k],
                device_id=peer, device_id_type=MESH))
        for cp in sends:
            cp.start()
        for cp in sends:
            cp.wait()

    hbm = pl.BlockSpec(memory_space=pl.ANY)
    res = pl.pallas_call(
        body, name="exchange_chips", in_specs=[hbm] * (n_p + 1), out_specs=[hbm] * (n_p + 1),
        out_shape=[jax.ShapeDtypeStruct((3,) + _shard_shape(n), BF) for n in PIECES]
        + [jax.ShapeDtypeStruct((3,) + qsmall.shape, qsmall.dtype)],
        scratch_shapes=[pltpu.SemaphoreType.DMA((3 * n_p + 3,)), pltpu.SemaphoreType.DMA((3 * n_p + 3,))],
    )(*[q[n] for n in PIECES], qsmall)
    return dict(zip(PIECES, res[:n_p])), res[n_p]


def _shard_shape(n):
    shp = LAYER_SHAPE[n]
    if n == 'w_in':
        return (shp[0], shp[1] // 4)
    if n == 'pool_w':
        return (shp[0], shp[1] // 4, shp[2])
    if n in ('ffn_w_gate', 'ffn_w_up'):
        return shp[1:]
    return (shp[0] // 4, shp[1])


def swap_cores(s):
    n = len(s)

    def body(*refs):
        srcs, outs = refs[:n], refs[n:2 * n]
        send_sems, recv_sems = refs[2 * n:]
        x, y, c = _coords()
        cps = [pltpu.make_async_remote_copy(src_ref=srcs[i], dst_ref=outs[i], send_sem=send_sems.at[i],
                                            recv_sem=recv_sems.at[i], device_id=(x, y, 1 - c), device_id_type=MESH)
               for i in range(n)]
        for cp in cps:
            cp.start()
        for cp in cps:
            cp.wait()

    hbm = pl.BlockSpec(memory_space=pl.ANY)
    return pl.pallas_call(
        body, name="swap_cores", in_specs=[hbm] * n, out_specs=[hbm] * n,
        out_shape=[jax.ShapeDtypeStruct(a.shape, a.dtype) for a in s],
        scratch_shapes=[pltpu.SemaphoreType.DMA((n,)), pltpu.SemaphoreType.DMA((n,))],
    )(*s)


def add_cores(p0, p1, theirs, *, out_dtype, name):
    shp = p0.shape
    args = [t.reshape(-1, shp[-1]) for t in (p0, p1, theirs)]
    R, C = args[0].shape
    tr = _row_tile(R, C)

    def body(a0_ref, a1_ref, t_ref, o_ref):
        c = lax.axis_index("c")
        t = t_ref[...].astype(F32)

        @pl.when(c == 0)
        def _():
            o_ref[...] = (a0_ref[...].astype(F32) + t).astype(o_ref.dtype)

        @pl.when(c == 1)
        def _():
            o_ref[...] = (a1_ref[...].astype(F32) + t).astype(o_ref.dtype)

    spec = pl.BlockSpec((tr, C), lambda i: (i, 0))
    out = pl.pallas_call(body, name=name, grid=(R // tr,), in_specs=[spec] * 3, out_specs=spec,
                         out_shape=jax.ShapeDtypeStruct((R, C), out_dtype), compiler_params=_cp("arbitrary"))(*args)
    return out.reshape(shp)


def sum_chips(name, q, recv3, chip):
    shard = _shard_shape(name)
    zero = (0,) * len(shard)
    if name == 'w_in':
        tr = 128
        grid = (shard[0] // tr,)
        qspec = pl.BlockSpec((tr, shard[1]), lambda i, me: (i, (me[0] + 3) % 4))
        rspec = pl.BlockSpec((3, tr, shard[1]), lambda i, me: (0, i, 0))
        ospec = pl.BlockSpec((tr, shard[1]), lambda i, me: (i, 0))
    else:
        grid = (1,)
        rspec = pl.BlockSpec((3,) + shard, lambda i, me: (0,) + zero)
        ospec = pl.BlockSpec(shard, lambda i, me: zero)
        if name == 'pool_w':
            qspec = pl.BlockSpec(shard, lambda i, me: (0, me[0], 0))
        elif name in ('ffn_w_gate', 'ffn_w_up'):
            qspec = pl.BlockSpec((None,) + shard, lambda i, me: (me[0], 0, 0))
        else:
            qspec = pl.BlockSpec(shard, lambda i, me: (me[0], 0))

    def body(me_ref, q_ref, r_ref, o_ref):
        del me_ref
        acc = q_ref[...].astype(F32)
        for j in range(3):
            acc = acc + r_ref[j].astype(F32)
        o_ref[...] = acc

    return pl.pallas_call(
        body, name="sum_chips_" + name,
        grid_spec=pltpu.PrefetchScalarGridSpec(num_scalar_prefetch=1, grid=grid, in_specs=[qspec, rspec],
                                               out_specs=ospec),
        out_shape=jax.ShapeDtypeStruct(shard, F32), compiler_params=_cp("arbitrary"))(chip, q, recv3)


def sum_chips_small(q, recv3, chip):
    r, C = q.shape
    slot_of_xor = {2: 0, 1: 1, 3: 2}

    def body(me_ref, q_ref, r_ref, o_ref):
        me = me_ref[0]
        acc = None
        for k in range(4):
            kx = k ^ me
            term = q_ref[...]
            for xv, j in slot_of_xor.items():
                term = jnp.where(kx == xv, r_ref[j], term)
            acc = term if acc is None else acc + term
        o_ref[...] = acc

    return pl.pallas_call(
        body, name="sum_chips_small",
        grid_spec=pltpu.PrefetchScalarGridSpec(
            num_scalar_prefetch=1, grid=(1,),
            in_specs=[pl.BlockSpec((r, C), lambda i, me: (0, 0)), pl.BlockSpec((3, r, C), lambda i, me: (0, 0, 0))],
            out_specs=pl.BlockSpec((r, C), lambda i, me: (0, 0))),
        out_shape=jax.ShapeDtypeStruct((r, C), F32), compiler_params=_cp("arbitrary"))(chip, q, recv3)


def _row_tile(R, C):
    for cand in (1024, 512, 256, 128, 64, 32, 16):
        if R % cand == 0 and cand * C * 4 <= 2 * 1024 * 1024:
            return cand
    return R


def sum_slots(a, *, name):
    n = a.shape[0]
    shp = a.shape[1:]
    a3 = a.reshape(n, -1, shp[-1])
    R, C = a3.shape[1:]
    tr = _row_tile(R, C * n // 2)

    def body(a_ref, o_ref):
        acc = a_ref[0].astype(F32)
        for k in range(1, n):
            acc = acc + a_ref[k].astype(F32)
        o_ref[...] = acc

    out = pl.pallas_call(
        body, name=name, grid=(R // tr,), in_specs=[pl.BlockSpec((n, tr, C), lambda i: (0, i, 0))],
        out_specs=pl.BlockSpec((tr, C), lambda i: (i, 0)), out_shape=jax.ShapeDtypeStruct((R, C), F32),
        compiler_params=_cp("arbitrary"))(a3)
    return out.reshape(shp)


def _adamw_math(w, g, m, v):
    mn = ADAM_B1 * m + (1.0 - ADAM_B1) * g
    vn = ADAM_B2 * v + (1.0 - ADAM_B2) * (g * g)
    m_hat = mn / (1.0 - ADAM_B1 ** ADAM_STEP)
    v_hat = vn / (1.0 - ADAM_B2 ** ADAM_STEP)
    return -ADAM_LR * (m_hat / (jnp.sqrt(v_hat) + ADAM_EPS) + ADAM_WD * w), mn, vn


def adamw(w, g, m, v, *, name):
    shp = w.shape
    args = [t.reshape(-1, shp[-1]) for t in (w, g, m, v)]
    R, C = args[0].shape
    tr = _row_tile(R, C)

    def body(w_ref, g_ref, m_ref, v_ref, d_ref, mo_ref, vo_ref):
        d_ref[...], mo_ref[...], vo_ref[...] = _adamw_math(w_ref[...], g_ref[...], m_ref[...], v_ref[...])

    spec = pl.BlockSpec((tr, C), lambda i: (i, 0))
    res = pl.pallas_call(
        body, name=name, grid=(R // tr,), in_specs=[spec] * 4, out_specs=[spec] * 3,
        out_shape=[jax.ShapeDtypeStruct((R, C), F32)] * 3, compiler_params=_cp("arbitrary"))(*args)
    return [r.reshape(shp) for r in res]


def adamw_layers(w, g_mine, g_theirs, m, v, *, name):
    shp = w.shape
    three = (DEPTH, -1, shp[-1])
    w3, m3, v3 = [t.reshape(three) for t in (w, m, v)]
    ga, gb = [t.reshape(-1, shp[-1]) for t in (g_mine, g_theirs)]
    R, C = ga.shape
    tr = _row_tile(R, C)

    def body(w_ref, ga_ref, gb_ref, m_ref, v_ref, g_ref, d_ref, mo_ref, vo_ref):
        mine = pl.program_id(0) == lax.axis_index("c")
        g = jnp.where(mine, ga_ref[...], gb_ref[...])
        g_ref[...] = g
        d_ref[...], mo_ref[...], vo_ref[...] = _adamw_math(w_ref[...], g, m_ref[...], v_ref[...])

    lay = pl.BlockSpec((None, tr, C), lambda l, i: (l, i, 0))
    one = pl.BlockSpec((tr, C), lambda l, i: (i, 0))
    res = pl.pallas_call(
        body, name=name, grid=(DEPTH, R // tr), in_specs=[lay, one, one, lay, lay], out_specs=[lay] * 4,
        out_shape=[jax.ShapeDtypeStruct(w3.shape, F32)] * 4, compiler_params=_cp("arbitrary", "arbitrary"))(
            w3, ga, gb, m3, v3)
    return [r.reshape(shp) for r in res]


def _remote(src, dst, send_sems, recv_sems, k, peer):
    return pltpu.make_async_remote_copy(src_ref=src, dst_ref=dst, send_sem=send_sems.at[k], recv_sem=recv_sems.at[k],
                                        device_id=peer, device_id_type=MESH)


def gather_layer(l, shards, small=None, pieces=PIECES):
    n_p = len(pieces)
    with_small = small is not None

    def descs(h_in, h_out, sems):
        srcs = dict(zip(pieces, h_in[:n_p]))
        outs = dict(zip(pieces, h_out[:n_p]))
        ici_s, ici_r, d2d_s, d2d_r, own_s, own_r = sems
        x, y, c = _coords()
        me = 2 * x + y
        sib = (x, y, 1 - c)
        ici, fwd, fwd_in, own = [], [], [], []
        for j, (fx, fy) in enumerate(FLIPS):
            peer = (x ^ fx, y ^ fy, c)
            other = 2 * (x ^ fx) + (y ^ fy)
            for p, n in enumerate(pieces):
                k = 3 * p + j
                mine = _half(n, _win(n, outs[n], me), c)
                landed = _half(n, _win(n, outs[n], other), c)
                sib_half = _half(n, _win(n, outs[n], other), 1 - c)
                ici.append((_remote(_half(n, srcs[n].at[l], c), mine, ici_s, ici_r, k, peer),
                            _remote(_half(n, srcs[n].at[l], c), landed, ici_s, ici_r, k, peer)))
                fwd.append(_remote(landed, landed, d2d_s, d2d_r, k, sib))
                fwd_in.append(_remote(sib_half, sib_half, d2d_s, d2d_r, k, sib))
            if with_small:
                k = 3 * n_p + j
                ici.append((_remote(h_in[n_p], h_out[n_p].at[me], ici_s, ici_r, k, peer),
                            _remote(h_in[n_p], h_out[n_p].at[other], ici_s, ici_r, k, peer)))
        for p, n in enumerate(pieces):
            own.append(_remote(srcs[n].at[l], _win(n, outs[n], me), own_s, own_r, p, sib))
        if with_small:
            own.append(_remote(h_in[n_p], h_out[n_p].at[me], own_s, own_r, n_p, sib))
        return ici, fwd, fwd_in, own

    def start(h_in, h_out, sems):
        ici, _, _, own = descs(h_in, h_out, sems)
        for send, _ in ici:
            send.start()
        for cp in own:
            cp.start()

    def finish(h_in, h_out, sems):
        ici, fwd, fwd_in, own = descs(h_in, h_out, sems)
        per_chip = n_p + (1 if with_small else 0)
        for j in range(3):
            for p in range(n_p):
                ici[j * per_chip + p][1].wait_recv()
                fwd[j * n_p + p].start()
            if with_small:
                ici[j * per_chip + n_p][1].wait_recv()
        for cp in fwd_in:
            cp.wait_recv()
        for send, _ in ici:
            send.wait_send()
        for cp in fwd:
            cp.wait_send()
        for cp in own:
            cp.wait()

    arrays = [shards[n] for n in pieces] + ([small] if with_small else [])
    out_shape = [jax.ShapeDtypeStruct(LAYER_SHAPE[n], BF) for n in pieces]
    if with_small:
        out_shape.append(jax.ShapeDtypeStruct((4,) + small.shape, small.dtype))
    sems = [pltpu.SemaphoreType.DMA((3 * n_p + 3,)), pltpu.SemaphoreType.DMA((3 * n_p + 3,)),
            pltpu.SemaphoreType.DMA((3 * n_p,)), pltpu.SemaphoreType.DMA((3 * n_p,)),
            pltpu.SemaphoreType.DMA((n_p + 1,)), pltpu.SemaphoreType.DMA((n_p + 1,))]
    return Hosted(arrays, out_shape, sems, start, finish)


def both_hosted(h1, h2):
    a1, o1, s1 = len(h1.arrays), len(h1.out_shape), len(h1.sems)

    def start(h_in, h_out, sems):
        h1.start(h_in[:a1], h_out[:o1], sems[:s1])
        h2.start(h_in[a1:], h_out[o1:], sems[s1:])

    def finish(h_in, h_out, sems):
        h1.finish(h_in[:a1], h_out[:o1], sems[:s1])
        h2.finish(h_in[a1:], h_out[o1:], sems[s1:])

    return Hosted(list(h1.arrays) + list(h2.arrays), list(h1.out_shape) + list(h2.out_shape),
                  list(h1.sems) + list(h2.sems), start, finish)


def run_hosted(host, name):
    n_in, n_out = len(host.arrays), len(host.out_shape)

    def body(*refs):
        h_in, h_out, sems = refs[:n_in], refs[n_in:n_in + n_out], refs[n_in + n_out:]
        host.start(h_in, h_out, sems)
        host.finish(h_in, h_out, sems)

    return pl.pallas_call(body, name=name, in_specs=[HBM_SPEC] * n_in, out_specs=[HBM_SPEC] * n_out,
                          out_shape=list(host.out_shape), scratch_shapes=list(host.sems))(*host.arrays)


def split_halves(parts, small):
    n_p = len(PIECES)
    rh = small.shape[0] // 2

    def body(*refs):
        srcs = dict(zip(PIECES, refs[:n_p]))
        s_ref = refs[n_p]
        outs = dict(zip(PIECES, refs[n_p + 1:2 * n_p + 1]))
        so_ref = refs[2 * n_p + 1]
        send_sems, recv_sems = refs[2 * n_p + 2:]
        x, y, c = _coords()
        sib = (x, y, 1 - c)
        cps = []
        for p, n in enumerate(PIECES):
            for k in range(4):
                cps.append(_remote(_half(n, _win(n, srcs[n], k), 1 - c), outs[n].at[k], send_sems, recv_sems,
                                   4 * p + k, sib))
        cps.append(_remote(s_ref.at[pl.ds(_mult((1 - c) * rh, 8), rh), :], so_ref, send_sems, recv_sems, 4 * n_p, sib))
        for cp in cps:
            cp.start()
        for cp in cps:
            cp.wait()

    res = pl.pallas_call(
        body, name="split_halves", in_specs=[HBM_SPEC] * (n_p + 1), out_specs=[HBM_SPEC] * (n_p + 1),
        out_shape=[jax.ShapeDtypeStruct((4,) + _half_shape(n), BF) for n in PIECES]
        + [jax.ShapeDtypeStruct((rh, small.shape[1]), small.dtype)],
        scratch_shapes=[pltpu.SemaphoreType.DMA((4 * n_p + 1,)), pltpu.SemaphoreType.DMA((4 * n_p + 1,))],
    )(*[parts[n] for n in PIECES], small)
    return dict(zip(PIECES, res[:n_p])), res[n_p]


def add_halves(name, part, theirs, core):
    half = _half_shape(name)
    zero = (0,) * len(half)
    if name == 'w_in':
        pspec = pl.BlockSpec(half, lambda k, cc: (cc[0], (k + 3) % 4))
    elif name == 'pool_w':
        pspec = pl.BlockSpec(half, lambda k, cc: (cc[0], k, 0))
    elif name in ('ffn_w_gate', 'ffn_w_up'):
        pspec = pl.BlockSpec((None,) + half, lambda k, cc: (k, cc[0], 0))
    else:
        pspec = pl.BlockSpec(half, lambda k, cc: (2 * k + cc[0], 0))
    slot = pl.BlockSpec((None,) + half, lambda k, cc: (k,) + zero)

    def body(cc_ref, p_ref, t_ref, o_ref):
        del cc_ref
        o_ref[...] = (p_ref[...].astype(F32) + t_ref[...].astype(F32)).astype(o_ref.dtype)

    return pl.pallas_call(
        body, name="add_cores_" + name,
        grid_spec=pltpu.PrefetchScalarGridSpec(num_scalar_prefetch=1, grid=(4,), in_specs=[pspec, slot],
                                               out_specs=slot),
        out_shape=jax.ShapeDtypeStruct((4,) + half, BF), compiler_params=_cp("arbitrary"))(core, part, theirs)


def add_halves_small(small, theirs, core):
    rh, C = theirs.shape

    def body(cc_ref, p_ref, t_ref, o_ref):
        del cc_ref
        o_ref[...] = p_ref[...] + t_ref[...]

    blk = pl.BlockSpec((rh, C), lambda i, cc: (0, 0))
    return pl.pallas_call(
        body, name="add_cores_small",
        grid_spec=pltpu.PrefetchScalarGridSpec(
            num_scalar_prefetch=1, grid=(1,),
            in_specs=[pl.BlockSpec((rh, C), lambda i, cc: (cc[0], 0)), blk], out_specs=blk),
        out_shape=jax.ShapeDtypeStruct((rh, C), F32), compiler_params=_cp("arbitrary"))(core, small, theirs)


def exchange_halves(q, qsmall):
    n_p = len(PIECES)

    def descs(h_in, h_out, sems):
        send_sems, recv_sems = sems
        x, y, c = _coords()
        cps = []
        for j, (fx, fy) in enumerate(FLIPS):
            peer = (x ^ fx, y ^ fy, c)
            other = 2 * (x ^ fx) + (y ^ fy)
            for p in range(n_p):
                cps.append(_remote(h_in[p].at[other], h_out[p].at[j], send_sems, recv_sems, 3 * p + j, peer))
            cps.append(_remote(h_in[n_p], h_out[n_p].at[j], send_sems, recv_sems, 3 * n_p + j, peer))
        return cps

    def start(h_in, h_out, sems):
        for cp in descs(h_in, h_out, sems):
            cp.start()

    def finish(h_in, h_out, sems):
        for cp in descs(h_in, h_out, sems):
            cp.wait()

    arrays = [q[n] for n in PIECES] + [qsmall]
    out_shape = [jax.ShapeDtypeStruct((3,) + _half_shape(n), BF) for n in PIECES]
    out_shape.append(jax.ShapeDtypeStruct((3,) + qsmall.shape, qsmall.dtype))
    sems = [pltpu.SemaphoreType.DMA((3 * n_p + 3,)), pltpu.SemaphoreType.DMA((3 * n_p + 3,))]
    return Hosted(arrays, out_shape, sems, start, finish)


def sum_halves(name, q, recv3, chip):
    half = _half_shape(name)
    zero = (0,) * len(half)

    def body(me_ref, q_ref, r_ref, o_ref):
        del me_ref
        acc = q_ref[...].astype(F32)
        for j in range(3):
            acc = acc + r_ref[j].astype(F32)
        o_ref[...] = acc

    return pl.pallas_call(
        body, name="sum_chips_" + name,
        grid_spec=pltpu.PrefetchScalarGridSpec(
            num_scalar_prefetch=1, grid=(1,),
            in_specs=[pl.BlockSpec((None,) + half, lambda i, me: (me[0],) + zero),
                      pl.BlockSpec((3,) + half, lambda i, me: (0,) + zero)],
            out_specs=pl.BlockSpec(half, lambda i, me: zero)),
        out_shape=jax.ShapeDtypeStruct(half, F32), compiler_params=_cp("arbitrary"))(chip, q, recv3)


def adamw_halves(w, g0, m, v, *, name):
    shp = w.shape
    C = shp[-1]
    four = (DEPTH, 2, -1, C)
    w4, m4, v4 = [t.reshape(four) for t in (w, m, v)]
    gs = [t.reshape(-1, C) for pair in g0 for t in pair]
    Rh = gs[0].shape[0]
    tr = _row_tile(Rh, C)

    def body(w_ref, a0_ref, b0_ref, a1_ref, b1_ref, m_ref, v_ref, g_ref, d_ref, mo_ref, vo_ref):
        mine = pl.program_id(1) == lax.axis_index("c")
        g_l0 = jnp.where(mine, a0_ref[...], b0_ref[...])
        g_l1 = jnp.where(mine, a1_ref[...], b1_ref[...])
        g = jnp.where(pl.program_id(0) == 0, g_l0, g_l1)
        g_ref[...] = g
        d_ref[...], mo_ref[...], vo_ref[...] = _adamw_math(w_ref[...], g, m_ref[...], v_ref[...])

    lay = pl.BlockSpec((None, None, tr, C), lambda l, h, i: (l, h, i, 0))
    one = pl.BlockSpec((tr, C), lambda l, h, i: (i, 0))
    res = pl.pallas_call(
        body, name=name, grid=(DEPTH, 2, Rh // tr), in_specs=[lay, one, one, one, one, lay, lay],
        out_specs=[lay] * 4, out_shape=[jax.ShapeDtypeStruct(w4.shape, F32)] * 4,
        compiler_params=_cp("arbitrary", "arbitrary", "arbitrary"))(w4, *gs, m4, v4)
    return [r.reshape(shp) for r in res]


def _local_shape(name, full_shape):
    shp = list(full_shape)
    ax = BIG_SHARDED.get(name, SMALL_SHARDED.get(name))
    if ax is not None:
        shp[ax] //= 4
    return tuple(shp)


FULL_SHAPES = {
    'w_in': (DEPTH, D, IN_COLS), 'b_in': (DEPTH, IN_COLS), 'pool_w': (DEPTH, 4, POOL_GD, POOL_GD),
    'pool_scale': (DEPTH, D), 'lru_conv_w': (DEPTH, 4, D), 'lru_conv_b': (DEPTH, D),
    'lru_w_r': (DEPTH, LRU_HEADS, LRU_HD, LRU_HD), 'lru_b_r': (DEPTH, D),
    'lru_w_i': (DEPTH, LRU_HEADS, LRU_HD, LRU_HD), 'lru_b_i': (DEPTH, D), 'lru_lambda': (DEPTH, D),
    'lru_w_out': (DEPTH, D, D), 'sconv_w': (DEPTH, 3, D), 'sconv_w_out': (DEPTH, D, D), 'w_mix_out': (DEPTH, D, D),
    'xa_w_q': (DEPTH, D, D), 'xa_w_k': (DEPTH, D, D), 'xa_w_v': (DEPTH, D, D), 'xa_w_o': (DEPTH, D, D),
    'ffn_w_gate': (DEPTH, D, D_FF), 'ffn_w_up': (DEPTH, D, D_FF), 'ffn_w_down': (DEPTH, D_FF, D),
    'ln_g': (DEPTH, 3, D), 'ln_b': (DEPTH, 3, D)}


def _pack(arrs, names, width, dtype, row_mult):
    flat = jnp.concatenate([arrs[n].astype(dtype).reshape(-1) for n in names])
    pad = (-flat.shape[0]) % (width * row_mult)
    if pad:
        flat = jnp.concatenate([flat, jnp.zeros((pad,), dtype)])
    return flat.reshape(-1, width)


def _unpack(flat2d, names, shapes):
    flat = flat2d.reshape(-1)
    out, off = {}, 0
    for n in names:
        size = 1
        for s in shapes[n]:
            size *= s
        out[n] = flat[off:off + size].reshape(shapes[n])
        off += size
    return out


def _gathered_full(g4, names, sharded_axis):
    loc_shapes = {n: _local_shape(n, FULL_SHAPES[n]) for n in names}
    per = [_unpack(g4[k], names, loc_shapes) for k in range(4)]
    return {n: jnp.concatenate([per[k][n] for k in range(4)], axis=sharded_axis[n]) for n in names}


def _perm_cols(a, perm, axis):
    blocks = [lax.slice_in_dim(a, p * D, (p + 1) * D, axis=axis) for p in perm]
    return jnp.concatenate(blocks, axis=axis)


def _shards_apart(a):
    w = a.shape[1] // 4
    return jnp.stack([a[:, k * w:(k + 1) * w] for k in range(4)])


def _shards_joined(a4):
    return jnp.concatenate([a4[k] for k in range(4)], axis=1)


Z_INV = tuple(Z_PERM.index(j) for j in range(8))
SMALL_SH_NAMES = list(SMALL_SHARDED)
SMALL_ROWS = 32


def kernel(x, mem, w_in, b_in, pool_w, pool_scale, lru_conv_w, lru_conv_b, lru_w_r, lru_b_r, lru_w_i, lru_b_i, lru_lambda, lru_w_out, sconv_w, sconv_w_out, w_mix_out, xa_w_q, xa_w_k, xa_w_v, xa_w_o, ffn_w_gate, ffn_w_up, ffn_w_down, ln_g, ln_b, loss_target, m_w_in, m_b_in, m_pool_w, m_pool_scale, m_lru_conv_w, m_lru_conv_b, m_lru_w_r, m_lru_b_r, m_lru_w_i, m_lru_b_i, m_lru_lambda, m_lru_w_out, m_sconv_w, m_sconv_w_out, m_w_mix_out, m_xa_w_q, m_xa_w_k, m_xa_w_v, m_xa_w_o, m_ffn_w_gate, m_ffn_w_up, m_ffn_w_down, m_ln_g, m_ln_b, v_w_in, v_b_in, v_pool_w, v_pool_scale, v_lru_conv_w, v_lru_conv_b, v_lru_w_r, v_lru_b_r, v_lru_w_i, v_lru_b_i, v_lru_lambda, v_lru_w_out, v_sconv_w, v_sconv_w_out, v_w_mix_out, v_xa_w_q, v_xa_w_k, v_xa_w_v, v_xa_w_o, v_ffn_w_gate, v_ffn_w_up, v_ffn_w_down, v_ln_g, v_ln_b):
    loc = dict(w_in=w_in, b_in=b_in, pool_w=pool_w, pool_scale=pool_scale, lru_conv_w=lru_conv_w,
               lru_conv_b=lru_conv_b, lru_w_r=lru_w_r, lru_b_r=lru_b_r, lru_w_i=lru_w_i, lru_b_i=lru_b_i,
               lru_lambda=lru_lambda, lru_w_out=lru_w_out, sconv_w=sconv_w, sconv_w_out=sconv_w_out,
               w_mix_out=w_mix_out, xa_w_q=xa_w_q, xa_w_k=xa_w_k, xa_w_v=xa_w_v, xa_w_o=xa_w_o,
               ffn_w_gate=ffn_w_gate, ffn_w_up=ffn_w_up, ffn_w_down=ffn_w_down, ln_g=ln_g, ln_b=ln_b)
    mom = dict(w_in=m_w_in, b_in=m_b_in, pool_w=m_pool_w, pool_scale=m_pool_scale, lru_conv_w=m_lru_conv_w,
               lru_conv_b=m_lru_conv_b, lru_w_r=m_lru_w_r, lru_b_r=m_lru_b_r, lru_w_i=m_lru_w_i, lru_b_i=m_lru_b_i,
               lru_lambda=m_lru_lambda, lru_w_out=m_lru_w_out, sconv_w=m_sconv_w, sconv_w_out=m_sconv_w_out,
               w_mix_out=m_w_mix_out, xa_w_q=m_xa_w_q, xa_w_k=m_xa_w_k, xa_w_v=m_xa_w_v, xa_w_o=m_xa_w_o,
               ffn_w_gate=m_ffn_w_gate, ffn_w_up=m_ffn_w_up, ffn_w_down=m_ffn_w_down, ln_g=m_ln_g, ln_b=m_ln_b)
    var = dict(w_in=v_w_in, b_in=v_b_in, pool_w=v_pool_w, pool_scale=v_pool_scale, lru_conv_w=v_lru_conv_w,
               lru_conv_b=v_lru_conv_b, lru_w_r=v_lru_w_r, lru_b_r=v_lru_b_r, lru_w_i=v_lru_w_i, lru_b_i=v_lru_b_i,
               lru_lambda=v_lru_lambda, lru_w_out=v_lru_w_out, sconv_w=v_sconv_w, sconv_w_out=v_sconv_w_out,
               w_mix_out=v_w_mix_out, xa_w_q=v_xa_w_q, xa_w_k=v_xa_w_k, xa_w_v=v_xa_w_v, xa_w_o=v_xa_w_o,
               ffn_w_gate=v_ffn_w_gate, ffn_w_up=v_ffn_w_up, ffn_w_down=v_ffn_w_down, ln_g=v_ln_g, ln_b=v_ln_b)

    chip = 2 * lax.axis_index("x") + lax.axis_index("y")
    core = lax.axis_index("c")
    chip_arr = jnp.reshape(chip, (1,)).astype(jnp.int32)
    core_arr = jnp.reshape(core, (1,)).astype(jnp.int32)
    n_p = len(PIECES)

    shards = {n: loc[n].astype(BF) for n in PIECES}
    small = _pack(loc, SMALL_SH_NAMES, 256, F32, 8)
    first = ('w_in',)
    rest = tuple(n for n in PIECES if n not in first)
    early1 = ('w_in', 'pool_w')
    late1 = tuple(n for n in PIECES if n not in early1)
    got = run_hosted(gather_layer(0, shards, small, pieces=first), "gather_first")
    vec = _gathered_full(got[len(first)], SMALL_SH_NAMES, SMALL_SHARDED)
    W = {n: [None] * DEPTH for n in ('w_in', 'pool_w', 'ffn_w_down', 'ffn_w_gu') + SQUARES}
    W['b_in'] = [jnp.roll(b_in[l:l + 1], -2 * D, axis=1) for l in range(DEPTH)]
    for n in ('lru_conv_w', 'sconv_w', 'ln_g', 'ln_b'):
        W[n] = [vec[n][l] for l in range(DEPTH)]
    for n in ('lru_w_r', 'lru_w_i'):
        W[n] = [loc[n][l].astype(BF) for l in range(DEPTH)]
    for n in ('pool_scale', 'lru_conv_b', 'lru_b_r', 'lru_b_i', 'lru_lambda'):
        W[n] = [loc[n][l:l + 1] for l in range(DEPTH)]

    def take(l, names, arrays):
        full = dict(zip(names, arrays))
        if 'ffn_w_gate' in full:
            W['ffn_w_gu'][l] = jnp.concatenate([_shards_joined(full['ffn_w_gate']), _shards_joined(full['ffn_w_up'])],
                                               axis=1)
        for n in names:
            if n in W:
                W[n][l] = full[n]

    take(0, first, got[:len(first)])

    def after_in_proj(results):
        take(0, rest, results[:len(rest)])
        take(1, early1, results[len(rest):])

    xs, memb = x[0], mem[0].astype(BF)
    host_a = both_hosted(gather_layer(0, shards, pieces=rest), gather_layer(1, shards, pieces=early1))
    xf, xb, sv0 = _layer_fwd(0, xs, None, None, memb, W, host=host_a, host2=gather_layer(1, shards, pieces=late1),
                             after_in_proj=after_in_proj)
    take(1, late1, sv0['hosted2'])
    xf, xb, sv1 = _layer_fwd(1, xf, xb, None, memb, W)
    saves = [sv0, sv1]
    kvs = [(sv['kb'], sv['vb']) for sv in saves]

    def packed(G):
        g = dict(G)
        g['ffn_w_gate'] = _shards_apart(g['ffn_w_gu'][:, :D_FF])
        g['ffn_w_up'] = _shards_apart(g['ffn_w_gu'][:, D_FF:])
        g['pool_w'] = g['pool_w'].astype(BF)
        g['b_in'] = jnp.roll(g['b_in'], 2 * D, axis=1)
        return {n: g[n] for n in PIECES}, _pack(g, SMALL_ALL, D, F32, SMALL_ROWS)

    def reduce_start(G):
        parts, smallp = packed(G)
        theirs, theirs_small = split_halves(parts, smallp)
        q = {n: add_halves(n, parts[n], theirs[n], core_arr) for n in PIECES}
        qs = add_halves_small(smallp, theirs_small, core_arr)
        return q, qs, exchange_halves(q, qs)

    def reduce_finish(q, qs, recv):
        sums = [sum_halves(n, q[n], recv[p], chip_arr) for p, n in enumerate(PIECES)]
        sums.append(sum_chips_small(qs, recv[n_p], chip_arr))
        other = swap_cores(sums)
        return sums, other

    lf = (W['ln_b'][1][2:3], loss_target[0])
    dx, G1, loss_blk = _layer_bwd(1, None, saves[1], memb, kvs[1][0], kvs[1][1], W, loss_from=lf)
    q1, qs1, host1 = reduce_start(G1)
    started0 = []

    def last_host_fn(G):
        started0.extend(reduce_start(G))
        return started0[2]

    grad_x, G0, _ = _layer_bwd(0, dx, saves[0], memb, kvs[0][0], kvs[0][1], W, host=host1, last_host_fn=last_host_fn)
    red = [reduce_finish(started0[0], started0[1], G0['hosted_last']), reduce_finish(q1, qs1, G0['hosted'])]
    loss = lax.psum(loss_blk[0, 0], ("x", "y", "c"))

    small_shapes = {n: FULL_SHAPES[n][1:] for n in SMALL_ALL}
    per_layer = []
    for l in range(DEPTH):
        mine, theirs = red[l][0][-1], red[l][1][-1]
        whole = jnp.where(core == 0, jnp.concatenate([mine, theirs]), jnp.concatenate([theirs, mine]))
        per_layer.append(_unpack(whole, SMALL_ALL, small_shapes))
    grads = {}
    for n in SMALL_ALL:
        gn = jnp.stack([per_layer[l][n] for l in range(DEPTH)])
        if n in SMALL_SHARDED:
            size = loc[n].shape[SMALL_SHARDED[n]]
            gn = lax.dynamic_slice_in_dim(gn, chip * size, size, axis=SMALL_SHARDED[n])
        grads[n] = gn

    out_d, out_m, out_v = {}, {}, {}
    for p, n in enumerate(PIECES):
        pairs = [(red[l][0][p], red[l][1][p]) for l in range(DEPTH)]
        grads[n], out_d[n], out_m[n], out_v[n] = adamw_halves(loc[n], pairs, mom[n], var[n], name="adamw_" + n)
    for n in SMALL_ALL:
        out_d[n], out_m[n], out_v[n] = adamw(loc[n], grads[n], mom[n], var[n], name="adamw_" + n)

    return (loss, grad_x[None], *[grads[n] for n in WEIGHTS], *[out_d[n] for n in WEIGHTS],
            *[out_m[n] for n in WEIGHTS], *[out_v[n] for n in WEIGHTS])
```

```python
import functools

import jax
import jax.numpy as jnp
from jax import lax
from jax.experimental import pallas as pl
from jax.experimental.pallas import tpu as pltpu

F32 = jnp.float32
BF = jnp.bfloat16
MESH = pl.DeviceIdType.MESH

D = 1024
DEPTH = 2
N_MEM = 256
POOL_WINDOWS = (2, 4, 8, 16)
POOL_GD = 256
LRU_HEADS = 8
LRU_HD = 128
LRU_C = 8.0
X_HEADS = 4
X_HD = 256
D_FF = 2816
IN_COLS = 8 * D
ALPHA = (2 * DEPTH) ** 0.25
LN_EPS = 1e-5
ADAM_LR = 0.001
ADAM_B1 = 0.9
ADAM_B2 = 0.999
ADAM_EPS = 1e-08
ADAM_WD = 0.01
ADAM_STEP = 10

Z_PERM = (2, 3, 4, 5, 6, 7, 0, 1)
ZB_SCONV, ZB_GATE, ZB_POOL, ZB_LRU = 0, 1, 6, 7
HALO = 16
VMEM_LIMIT = 56 * 1024 * 1024

WEIGHTS = ['w_in', 'b_in', 'pool_w', 'pool_scale', 'lru_conv_w', 'lru_conv_b', 'lru_w_r', 'lru_b_r', 'lru_w_i',
           'lru_b_i', 'lru_lambda', 'lru_w_out', 'sconv_w', 'sconv_w_out', 'w_mix_out', 'xa_w_q', 'xa_w_k', 'xa_w_v',
           'xa_w_o', 'ffn_w_gate', 'ffn_w_up', 'ffn_w_down', 'ln_g', 'ln_b']
BIG_SHARDED = {'w_in': 2, 'pool_w': 2, 'lru_w_out': 1, 'sconv_w_out': 1, 'w_mix_out': 1, 'xa_w_q': 1, 'xa_w_k': 1,
               'xa_w_v': 1, 'xa_w_o': 1, 'ffn_w_gate': 2, 'ffn_w_up': 2, 'ffn_w_down': 1}
SMALL_SHARDED = {'lru_conv_w': 2, 'sconv_w': 2, 'ln_g': 2, 'ln_b': 2}
REPLICATED = ['b_in', 'pool_scale', 'lru_conv_b', 'lru_w_r', 'lru_b_r', 'lru_w_i', 'lru_b_i', 'lru_lambda']
SMALL_ALL = ['b_in', 'pool_scale', 'lru_conv_w', 'lru_conv_b', 'lru_w_r', 'lru_b_r', 'lru_w_i', 'lru_b_i',
             'lru_lambda', 'sconv_w', 'ln_g', 'ln_b']


def _cp(*sem):
    return pltpu.CompilerParams(dimension_semantics=sem, vmem_limit_bytes=VMEM_LIMIT)


def _sigmoid(x):
    return 0.5 * jnp.tanh(0.5 * x) + 0.5


class Hosted:
    def __init__(self, arrays, out_shape, sems, start, finish):
        self.arrays, self.out_shape, self.sems, self.start, self.finish = arrays, out_shape, sems, start, finish


def _with_host(host, n_in, n_out, refs):
    if host is None:
        return refs[:n_in], (), refs[n_in:n_in + n_out], (), refs[n_in + n_out:], ()
    hi, ho, hs = len(host.arrays), len(host.out_shape), len(host.sems)
    ins, h_in = refs[:n_in], refs[n_in:n_in + hi]
    outs = refs[n_in + hi:n_in + hi + n_out]
    h_out = refs[n_in + hi + n_out:n_in + hi + n_out + ho]
    rest = refs[n_in + hi + n_out + ho:]
    return ins, h_in, outs, h_out, rest[:len(rest) - hs], rest[len(rest) - hs:]


HBM_SPEC = pl.BlockSpec(memory_space=pl.ANY)


def mm_nn(a, w, bias, *, out_dtype, tm, tn, name, host=None):
    T, K = a.shape
    N = w.shape[1]
    tm, tn = min(tm, T), min(tn, N)
    nj, ni = N // tn, T // tm
    n_in = 2 if bias is None else 3

    def body(*refs):
        ins, h_in, (o_ref,), h_out, _, h_sems = _with_host(host, n_in, 1, refs)
        a_ref, w_ref = ins[:2]
        j, i = pl.program_id(0), pl.program_id(1)
        if host is not None:
            @pl.when((j == 0) & (i == 0))
            def _():
                host.start(h_in, h_out, h_sems)
        acc = jnp.dot(a_ref[...].astype(BF), w_ref[...], preferred_element_type=F32)
        if bias is not None:
            acc = acc + ins[2][...]
        o_ref[...] = acc.astype(o_ref.dtype)
        if host is not None:
            @pl.when((j == nj - 1) & (i == ni - 1))
            def _():
                host.finish(h_in, h_out, h_sems)

    in_specs = [pl.BlockSpec((tm, K), lambda j, i: (i, 0)), pl.BlockSpec((K, tn), lambda j, i: (0, j))]
    args = [a, w]
    if bias is not None:
        in_specs.append(pl.BlockSpec((1, tn), lambda j, i: (0, j)))
        args.append(bias)
    out_specs = [pl.BlockSpec((tm, tn), lambda j, i: (i, j))]
    out_shape = [jax.ShapeDtypeStruct((T, N), out_dtype)]
    scratch = []
    if host is not None:
        in_specs += [HBM_SPEC] * len(host.arrays)
        args += list(host.arrays)
        out_specs += [HBM_SPEC] * len(host.out_shape)
        out_shape += list(host.out_shape)
        scratch = list(host.sems)
    res = pl.pallas_call(
        body, name=name, grid=(nj, ni), in_specs=in_specs, out_specs=out_specs, out_shape=out_shape,
        scratch_shapes=scratch, compiler_params=_cp("arbitrary", "arbitrary"))(*args)
    return res[0] if host is None else (res[0], res[1:])


FF_HALF = D_FF // 2


def ffn_in_swiglu(x, wgu, *, tm, name, host=None):
    T, K = x.shape
    tm = min(tm, T)
    ni = T // tm

    def body(*refs):
        (x_ref, w_ref), h_in, (gu_ref, h_ref), h_out, _, h_sems = _with_host(host, 2, 2, refs)
        j, i = pl.program_id(0), pl.program_id(1)
        if host is not None:
            @pl.when((j == 0) & (i == 0))
            def _():
                host.start(h_in, h_out, h_sems)
        acc = jnp.dot(x_ref[...].astype(BF), w_ref[...], preferred_element_type=F32)
        gu = acc.astype(BF)
        gu_ref[...] = gu
        g = gu[:, :FF_HALF].astype(F32)
        h_ref[...] = (g * _sigmoid(g) * gu[:, FF_HALF:].astype(F32)).astype(BF)
        if host is not None:
            @pl.when((j == 1) & (i == ni - 1))
            def _():
                host.finish(h_in, h_out, h_sems)

    in_specs = [pl.BlockSpec((tm, K), lambda j, i: (i, 0)), pl.BlockSpec((K, 2 * FF_HALF), lambda j, i: (0, j))]
    args = [x, wgu]
    out_specs = [pl.BlockSpec((tm, 2 * FF_HALF), lambda j, i: (i, j)), pl.BlockSpec((tm, FF_HALF), lambda j, i: (i, j))]
    out_shape = [jax.ShapeDtypeStruct((T, 2 * D_FF), BF), jax.ShapeDtypeStruct((T, D_FF), BF)]
    scratch = []
    if host is not None:
        in_specs += [HBM_SPEC] * len(host.arrays)
        args += list(host.arrays)
        out_specs += [HBM_SPEC] * len(host.out_shape)
        out_shape += list(host.out_shape)
        scratch = list(host.sems)
    res = pl.pallas_call(
        body, name=name, grid=(2, ni), in_specs=in_specs, out_specs=out_specs, out_shape=out_shape,
        scratch_shapes=scratch, compiler_params=_cp("arbitrary", "arbitrary"))(*args)
    return (res[0], res[1]) if host is None else (res[0], res[1], res[2:])


def ffn_down_dx_swiglu(dp, wd, gu, *, tm, name):
    T = dp.shape[0]
    tm = min(tm, T)

    def body(dp_ref, w_ref, gu_ref, o_ref):
        dh = lax.dot_general(dp_ref[...], w_ref[...], (((1,), (1,)), ((), ())), preferred_element_type=F32)
        dh = dh.astype(BF).astype(F32)
        g = gu_ref[:, :FF_HALF].astype(F32)
        u = gu_ref[:, FF_HALF:].astype(F32)
        sg = _sigmoid(g)
        o_ref[:, :FF_HALF] = (dh * u * sg * (1.0 + g * (1.0 - sg))).astype(BF)
        o_ref[:, FF_HALF:] = (dh * g * sg).astype(BF)

    return pl.pallas_call(
        body, name=name, grid=(2, T // tm),
        in_specs=[pl.BlockSpec((tm, D), lambda j, i: (i, 0)), pl.BlockSpec((FF_HALF, D), lambda j, i: (j, 0)),
                  pl.BlockSpec((tm, 2 * FF_HALF), lambda j, i: (i, j))],
        out_specs=pl.BlockSpec((tm, 2 * FF_HALF), lambda j, i: (i, j)),
        out_shape=jax.ShapeDtypeStruct((T, 2 * D_FF), BF), compiler_params=_cp("arbitrary", "arbitrary"))(dp, wd, gu)


def mm_nt(a, w, res, *, out_dtype, tm, tc, name, host=None):
    T, C = a.shape
    K = w.shape[0]
    tm, tc = min(tm, T), min(tc, C)
    nc = C // tc
    ni = T // tm
    n_in = 2 if res is None else 3

    def body(*refs):
        ins, h_in, (o_ref,), h_out, (acc_ref,), h_sems = _with_host(host, n_in, 1, refs)
        a_ref, w_ref = ins[:2]
        r_ref = ins[2] if res is not None else None
        c = pl.program_id(1)
        if host is not None:
            @pl.when((pl.program_id(0) == 0) & (c == 0))
            def _():
                host.start(h_in, h_out, h_sems)

            @pl.when((pl.program_id(0) == ni - 1) & (c == nc - 1))
            def _():
                host.finish(h_in, h_out, h_sems)
        part = lax.dot_general(a_ref[...].astype(BF), w_ref[...], (((1,), (1,)), ((), ())),
                               preferred_element_type=F32)

        @pl.when(c == 0)
        def _():
            acc_ref[...] = part

        @pl.when(c > 0)
        def _():
            acc_ref[...] += part

        @pl.when(c == nc - 1)
        def _():
            out = acc_ref[...]
            if res is not None:
                out = out + ALPHA * r_ref[...]
            o_ref[...] = out.astype(o_ref.dtype)

    in_specs = [pl.BlockSpec((tm, tc), lambda i, c: (i, c)), pl.BlockSpec((K, tc), lambda i, c: (0, c))]
    args = [a, w]
    if res is not None:
        in_specs.append(pl.BlockSpec((tm, K), lambda i, c: (i, 0)))
        args.append(res)
    out_specs = [pl.BlockSpec((tm, K), lambda i, c: (i, 0))]
    out_shape = [jax.ShapeDtypeStruct((T, K), out_dtype)]
    scratch = [pltpu.VMEM((tm, K), F32)]
    if host is not None:
        in_specs += [HBM_SPEC] * len(host.arrays)
        args += list(host.arrays)
        out_specs += [HBM_SPEC] * len(host.out_shape)
        out_shape += list(host.out_shape)
        scratch += list(host.sems)
    out = pl.pallas_call(
        body, name=name, grid=(ni, nc), in_specs=in_specs, out_specs=out_specs, out_shape=out_shape,
        scratch_shapes=scratch, compiler_params=_cp("arbitrary", "arbitrary"))(*args)
    return out[0] if host is None else (out[0], out[1:])


def mm_tn(a, b, *, out_dtype, tk, tn, tt, name, colsum=False):
    T, K = a.shape
    N = b.shape[1]
    tk, tn, tt = min(tk, K), min(tn, N), min(tt, T)
    nt = T // tt

    def body(*refs):
        if colsum:
            a_ref, b_ref, o_ref, cs_ref, acc_ref = refs
        else:
            a_ref, b_ref, o_ref, acc_ref = refs
        i, t = pl.program_id(1), pl.program_id(2)
        bb = b_ref[...]
        part = lax.dot_general(a_ref[...].astype(BF), bb.astype(BF), (((0,), (0,)), ((), ())),
                               preferred_element_type=F32)

        @pl.when(t == 0)
        def _():
            acc_ref[...] = part

        @pl.when(t > 0)
        def _():
            acc_ref[...] += part

        @pl.when(t == nt - 1)
        def _():
            o_ref[...] = acc_ref[...].astype(o_ref.dtype)

        if colsum:
            s = jnp.sum(bb.astype(F32), axis=0, keepdims=True)

            @pl.when((i == 0) & (t == 0))
            def _():
                cs_ref[...] = s

            @pl.when((i == 0) & (t > 0))
            def _():
                cs_ref[...] += s

    out_specs = [pl.BlockSpec((tk, tn), lambda j, i, t: (i, j))]
    out_shape = [jax.ShapeDtypeStruct((K, N), out_dtype)]
    if colsum:
        out_specs.append(pl.BlockSpec((1, tn), lambda j, i, t: (0, j)))
        out_shape.append(jax.ShapeDtypeStruct((1, N), F32))
    res = pl.pallas_call(
        body, name=name, grid=(N // tn, K // tk, nt),
        in_specs=[pl.BlockSpec((tt, tk), lambda j, i, t: (t, i)), pl.BlockSpec((tt, tn), lambda j, i, t: (t, j))],
        out_specs=out_specs, out_shape=out_shape, scratch_shapes=[pltpu.VMEM((tk, tn), F32)],
        compiler_params=_cp("arbitrary", "arbitrary", "arbitrary"))(a, b)
    return res if colsum else res[0]


def mm_res_ln(a, w, res, g, b, *, tm, name):
    T, K = a.shape
    tm = min(tm, T)

    def body(a_ref, w_ref, r_ref, g_ref, b_ref, y_ref, yb_ref, xh_ref, rs_ref):
        pre = ALPHA * r_ref[...] + jnp.dot(a_ref[...].astype(BF), w_ref[...], preferred_element_type=F32)
        mu = jnp.mean(pre, axis=-1, keepdims=True)
        cen = pre - mu
        var = jnp.mean(cen * cen, axis=-1, keepdims=True)
        rstd = lax.rsqrt(var + LN_EPS)
        xhat = cen * rstd
        y = xhat * g_ref[...] + b_ref[...]
        y_ref[...] = y
        yb_ref[...] = y.astype(BF)
        xh_ref[...] = xhat
        rs_ref[...] = rstd

    row = lambda i: (i, 0)
    fix = lambda i: (0, 0)
    return pl.pallas_call(
        body, name=name, grid=(T // tm,),
        in_specs=[pl.BlockSpec((tm, K), row), pl.BlockSpec((K, D), fix), pl.BlockSpec((tm, D), row),
                  pl.BlockSpec((1, D), fix), pl.BlockSpec((1, D), fix)],
        out_specs=[pl.BlockSpec((tm, D), row), pl.BlockSpec((tm, D), row), pl.BlockSpec((tm, D), row),
                   pl.BlockSpec((tm, 1), row)],
        out_shape=[jax.ShapeDtypeStruct((T, D), F32), jax.ShapeDtypeStruct((T, D), BF),
                   jax.ShapeDtypeStruct((T, D), F32), jax.ShapeDtypeStruct((T, 1), F32)],
        compiler_params=_cp("arbitrary"))(a, w, res, g, b)


def ln_bwd(dy, xhat, rstd, g, *, tm, name, loss_from=None, dw_of=None):
    T = xhat.shape[0]
    tm = min(tm, T)
    nt = T // tm
    with_loss = loss_from is not None
    with_dw = dw_of is not None

    def body(*refs):
        if with_dw:
            acc_ref, refs = refs[-1], refs[:-1]
            n_main_in = 5 if with_loss else 4
            a_ref = refs[n_main_in]
            dw_ref = refs[-1]
            refs = refs[:n_main_in] + refs[n_main_in + 1:-1]
        if with_loss:
            xh_ref, rs_ref, g_ref, b_ref, t_ref, dp_ref, dpb_ref, dg_ref, db_ref, ls_ref = refs
        else:
            dy_ref, xh_ref, rs_ref, g_ref, dp_ref, dpb_ref, dg_ref, db_ref = refs
        i = pl.program_id(0)
        xhat_ = xh_ref[...]
        gg = g_ref[...]
        if with_loss:
            err = xhat_ * gg + b_ref[...] - t_ref[...]
            dyv = err * (1.0 / D)
            lpart = 0.5 * jnp.sum(jnp.sum(err * err, axis=-1, keepdims=True) * (1.0 / D))
        else:
            dyv = dy_ref[...]
        dxh = dyv * gg
        m1 = jnp.mean(dxh, axis=-1, keepdims=True)
        m2 = jnp.mean(dxh * xhat_, axis=-1, keepdims=True)
        dpre = rs_ref[...] * (dxh - m1 - xhat_ * m2)
        dp_ref[...] = dpre
        dpb = dpre.astype(BF)
        dpb_ref[...] = dpb
        dgp = jnp.sum(dyv * xhat_, axis=0, keepdims=True)
        dbp = jnp.sum(dyv, axis=0, keepdims=True)
        if with_dw:
            dwp = lax.dot_general(a_ref[...], dpb, (((0,), (0,)), ((), ())), preferred_element_type=F32)

        @pl.when(i == 0)
        def _():
            dg_ref[...] = dgp
            db_ref[...] = dbp
            if with_loss:
                ls_ref[...] = jnp.full((8, 128), lpart, F32)
            if with_dw:
                acc_ref[...] = dwp

        @pl.when(i > 0)
        def _():
            dg_ref[...] += dgp
            db_ref[...] += dbp
            if with_loss:
                ls_ref[...] += jnp.full((8, 128), lpart, F32)
            if with_dw:
                acc_ref[...] += dwp

        if with_dw:
            @pl.when(i == nt - 1)
            def _():
                dw_ref[...] = acc_ref[...].astype(BF)

    row = lambda i: (i, 0)
    fix = lambda i: (0, 0)
    if with_loss:
        in_specs = [pl.BlockSpec((tm, D), row), pl.BlockSpec((tm, 1), row), pl.BlockSpec((1, D), fix),
                    pl.BlockSpec((1, D), fix), pl.BlockSpec((tm, D), row)]
        args = [xhat, rstd, g, loss_from[0], loss_from[1]]
    else:
        in_specs = [pl.BlockSpec((tm, D), row), pl.BlockSpec((tm, D), row), pl.BlockSpec((tm, 1), row),
                    pl.BlockSpec((1, D), fix)]
        args = [dy, xhat, rstd, g]
    out_specs = [pl.BlockSpec((tm, D), row), pl.BlockSpec((tm, D), row), pl.BlockSpec((1, D), fix),
                 pl.BlockSpec((1, D), fix)]
    out_shape = [jax.ShapeDtypeStruct((T, D), F32), jax.ShapeDtypeStruct((T, D), BF),
                 jax.ShapeDtypeStruct((1, D), F32), jax.ShapeDtypeStruct((1, D), F32)]
    if with_loss:
        out_specs.append(pl.BlockSpec((8, 128), fix))
        out_shape.append(jax.ShapeDtypeStruct((8, 128), F32))
    scratch = []
    if with_dw:
        K = dw_of.shape[1]
        in_specs.append(pl.BlockSpec((tm, K), row))
        args.append(dw_of)
        out_specs.append(pl.BlockSpec((K, D), fix))
        out_shape.append(jax.ShapeDtypeStruct((K, D), BF))
        scratch.append(pltpu.VMEM((K, D), F32))
    return pl.pallas_call(body, name=name, grid=(nt,), in_specs=in_specs, out_specs=out_specs,
                          out_shape=out_shape, scratch_shapes=scratch, compiler_params=_cp("arbitrary"))(*args)


def _prev_halo(tm, blk):
    return lambda i: (jnp.maximum(i * (tm // HALO) - 1, 0), blk)


def _next_halo(tm, T, blk):
    return lambda i: (jnp.minimum((i + 1) * (tm // HALO), T // HALO - 1), blk)


def _pool_p(ext, t, g):
    e = ext[:, g * POOL_GD:(g + 1) * POOL_GD]
    s = e
    for sh in (1, 2, 4, 8)[:g + 1]:
        s = s + pltpu.roll(s, sh, axis=0)
    inv = 1.0 / jnp.minimum(t + 1, POOL_WINDOWS[g]).astype(F32)
    return s[HALO:] * inv - e[HALO:]


def pool_fwd(z, pw, *, tm, name):
    T = z.shape[0]
    tm = min(tm, T)

    def body(zm_ref, zh_ref, pw_ref, o_ref):
        i = pl.program_id(0)
        keep = jnp.where(i == 0, 0.0, 1.0).astype(F32)
        ext = jnp.concatenate([zh_ref[...].astype(F32) * keep, zm_ref[...].astype(F32)], axis=0)
        t = i * tm + lax.broadcasted_iota(jnp.int32, (tm, 1), 0)
        outs = [jnp.dot(_pool_p(ext, t, g).astype(BF), pw_ref[g], preferred_element_type=F32) for g in range(4)]
        o_ref[...] = jnp.concatenate(outs, axis=1).astype(o_ref.dtype)

    return pl.pallas_call(
        body, name=name, grid=(T // tm,),
        in_specs=[pl.BlockSpec((tm, D), lambda i: (i, ZB_POOL)), pl.BlockSpec((HALO, D), _prev_halo(tm, ZB_POOL)),
                  pl.BlockSpec((4, POOL_GD, POOL_GD), lambda i: (0, 0, 0))],
        out_specs=pl.BlockSpec((tm, D), lambda i: (i, 0)),
        out_shape=jax.ShapeDtypeStruct((T, D), BF), compiler_params=_cp("arbitrary"))(z, z, pw)


def pool_bwd(dz, dyp, yp_pre, z, pw, ps, *, tm, name):
    T = z.shape[0]
    tm = min(tm, T)
    nt = T // tm

    def body(dz_in, dy_ref, dyn_ref, yp_ref, zm_ref, zh_ref, pw_ref, ps_ref, dz_ref, dpw_ref, dps_ref):
        del dz_in
        i = pl.program_id(0)
        keep_p = jnp.where(i == 0, 0.0, 1.0).astype(F32)
        keep_n = jnp.where(i == nt - 1, 0.0, 1.0).astype(F32)
        ext = jnp.concatenate([zh_ref[...].astype(F32) * keep_p, zm_ref[...].astype(F32)], axis=0)
        t = i * tm + lax.broadcasted_iota(jnp.int32, (tm, 1), 0)
        psv = ps_ref[...]
        dy = dy_ref[...].astype(F32)
        dyp_ext = jnp.concatenate([dy, dyn_ref[...].astype(F32) * keep_n], axis=0) * psv
        t_ext = i * tm + lax.broadcasted_iota(jnp.int32, (tm + HALO, 1), 0)
        dps = jnp.sum(dy * yp_ref[...].astype(F32), axis=0, keepdims=True)
        dzs, dpws = [], []
        for g in range(4):
            sl = slice(g * POOL_GD, (g + 1) * POOL_GD)
            dyg = dyp_ext[:, sl].astype(BF)
            dp = lax.dot_general(dyg, pw_ref[g], (((1,), (1,)), ((), ())), preferred_element_type=F32)
            q = dp * (1.0 / jnp.minimum(t_ext + 1, POOL_WINDOWS[g]).astype(F32))
            s = q
            for sh in (1, 2, 4, 8)[:g + 1]:
                s = s + pltpu.roll(s, tm + HALO - sh, axis=0)
            dzs.append(s[:tm] - dp[:tm])
            p = _pool_p(ext, t, g).astype(BF)
            dpws.append(lax.dot_general(p, dyg[:tm], (((0,), (0,)), ((), ())), preferred_element_type=F32))
        dz_ref[...] = jnp.concatenate(dzs, axis=1).astype(dz_ref.dtype)

        @pl.when(i == 0)
        def _():
            for g in range(4):
                dpw_ref[g] = dpws[g]
            dps_ref[...] = dps

        @pl.when(i > 0)
        def _():
            for g in range(4):
                dpw_ref[g] += dpws[g]
            dps_ref[...] += dps

    row = lambda i: (i, 0)
    return pl.pallas_call(
        body, name=name, grid=(nt,),
        in_specs=[pl.BlockSpec(memory_space=pl.ANY),
                  pl.BlockSpec((tm, D), row), pl.BlockSpec((HALO, D), _next_halo(tm, T, 0)),
                  pl.BlockSpec((tm, D), row),
                  pl.BlockSpec((tm, D), lambda i: (i, ZB_POOL)), pl.BlockSpec((HALO, D), _prev_halo(tm, ZB_POOL)),
                  pl.BlockSpec((4, POOL_GD, POOL_GD), lambda i: (0, 0, 0)), pl.BlockSpec((1, D), lambda i: (0, 0))],
        out_specs=[pl.BlockSpec((tm, D), lambda i: (i, ZB_POOL)),
                   pl.BlockSpec((4, POOL_GD, POOL_GD), lambda i: (0, 0, 0)), pl.BlockSpec((1, D), lambda i: (0, 0))],
        out_shape=[jax.ShapeDtypeStruct(dz.shape, dz.dtype), jax.ShapeDtypeStruct((4, POOL_GD, POOL_GD), F32),
                   jax.ShapeDtypeStruct((1, D), F32)],
        input_output_aliases={0: 0}, compiler_params=_cp("arbitrary"))(dz, dyp, dyp, yp_pre, z, z, pw, ps)


def _fill_ext(ext_s, halo, main, keep):
    ext_s[0:HALO, :] = halo * keep
    ext_s[HALO:, :] = main


def _lru_gates(ext_s, tm, cw, cb, wr_ref, br, wi_ref, bi, lam):
    shifted = []
    v = cb
    for k in range(4):
        zs = ext_s[pl.ds(HALO - 3 + k, tm), :]
        shifted.append(zs)
        v = v + cw[k:k + 1, :] * zs
    vb = v.astype(BF)
    rp, ip = [], []
    for h in range(LRU_HEADS):
        sl = slice(h * LRU_HD, (h + 1) * LRU_HD)
        rp.append(jnp.dot(vb[:, sl], wr_ref[h], preferred_element_type=F32))
        ip.append(jnp.dot(vb[:, sl], wi_ref[h], preferred_element_type=F32))
    r = _sigmoid(jnp.concatenate(rp, axis=1) + br)
    ig = _sigmoid(jnp.concatenate(ip, axis=1) + bi)
    sp = jnp.maximum(-lam, 0.0) + jnp.log(1.0 + jnp.exp(-jnp.abs(lam)))
    a = jnp.exp(-LRU_C * r * sp)
    om = 1.0 - a * a
    rs = lax.rsqrt(om)
    return v, vb, r, ig, a, om, rs, sp, shifted


def lru_fwd(z, cw, cb, wr, br, wi, bi, lam, wlo, *, tm, name):
    T = z.shape[0]
    tm = min(tm, T)
    nch = tm // 8

    def body(zm_ref, zh_ref, cw_ref, cb_ref, wr_ref, br_ref, wi_ref, bi_ref, lam_ref, wlo_ref, h_ref, y_ref,
             a_s, b_s, carry, ext_s):
        i = pl.program_id(0)

        @pl.when(i == 0)
        def _():
            carry[...] = jnp.zeros_like(carry)

        keep = jnp.where(i == 0, 0.0, 1.0).astype(F32)
        _fill_ext(ext_s, zh_ref[...].astype(F32), zm_ref[...].astype(F32), keep)
        v, _, _, ig, a, om, rs, _, _ = _lru_gates(ext_s, tm, cw_ref[...], cb_ref[...], wr_ref, br_ref[...], wi_ref,
                                                  bi_ref[...], lam_ref[...])
        a_s[...] = a
        b_s[...] = jnp.where(om > 0.0, om * rs, 0.0) * (ig * v)
        row = lax.broadcasted_iota(jnp.int32, (8, D), 0)

        def step(ci, hprev):
            sl = pl.ds(pl.multiple_of(ci * 8, 8), 8)
            aa, bb = a_s[sl, :], b_s[sl, :]
            for s in (1, 2, 4):
                m = row >= s
                bb = bb + aa * jnp.where(m, pltpu.roll(bb, s, axis=0), 0.0)
                aa = aa * jnp.where(m, pltpu.roll(aa, s, axis=0), 1.0)
            h = bb + aa * hprev
            h_ref[sl, :] = h
            return jnp.broadcast_to(h[7:8, :], (8, D))

        carry[...] = lax.fori_loop(0, nch, step, carry[...])
        y_ref[...] = jnp.dot(h_ref[...].astype(BF), wlo_ref[...], preferred_element_type=F32).astype(BF)

    fix2 = lambda i: (0, 0)
    fix3 = lambda i: (0, 0, 0)
    return pl.pallas_call(
        body, name=name, grid=(T // tm,),
        in_specs=[pl.BlockSpec((tm, D), lambda i: (i, ZB_LRU)), pl.BlockSpec((HALO, D), _prev_halo(tm, ZB_LRU)),
                  pl.BlockSpec((4, D), fix2), pl.BlockSpec((1, D), fix2),
                  pl.BlockSpec((LRU_HEADS, LRU_HD, LRU_HD), fix3), pl.BlockSpec((1, D), fix2),
                  pl.BlockSpec((LRU_HEADS, LRU_HD, LRU_HD), fix3), pl.BlockSpec((1, D), fix2),
                  pl.BlockSpec((1, D), fix2), pl.BlockSpec((D, D), fix2)],
        out_specs=[pl.BlockSpec((tm, D), lambda i: (i, 0)), pl.BlockSpec((tm, D), lambda i: (i, 0))],
        out_shape=[jax.ShapeDtypeStruct((T, D), F32), jax.ShapeDtypeStruct((T, D), BF)],
        scratch_shapes=[pltpu.VMEM((tm, D), F32), pltpu.VMEM((tm, D), F32), pltpu.VMEM((8, D), F32),
                        pltpu.VMEM((tm + HALO, D), F32)],
        compiler_params=_cp("arbitrary"))(z, z, cw, cb, wr, br, wi, bi, lam, wlo)


def lru_bwd(dz, dyl, z, h, cw, cb, wr, br, wi, bi, lam, wlo, *, tm, name):
    T = z.shape[0]
    tm = min(tm, T)
    nt = T // tm
    nch = tm // 8

    def body(dz_in, dy_ref, zm_ref, zh_ref, h_ref, hh_ref, cw_ref, cb_ref, wr_ref, br_ref, wi_ref, bi_ref, lam_ref,
             wlo_ref, dz_ref, dcw_ref, dcb_ref, dwr_ref, dbr_ref, dwi_ref, dbi_ref, dlam_ref, dwlo_ref,
             c_s, g_s, dh_s, dh_carry, a_ext, dv_ext, ext_s, h_ext, wlo_acc):
        del dz_in
        i = pl.program_id(0)
        ti = nt - 1 - i

        @pl.when(i == 0)
        def _():
            dh_carry[...] = jnp.zeros_like(dh_carry)
            a_ext[tm:, :] = jnp.zeros((8, D), F32)
            dv_ext[tm:, :] = jnp.zeros((HALO, D), F32)

        keep = jnp.where(ti == 0, 0.0, 1.0).astype(F32)
        _fill_ext(ext_s, zh_ref[...].astype(F32), zm_ref[...].astype(F32), keep)
        cw_ = cw_ref[...]
        lam_ = lam_ref[...]
        v, vb, r, ig, a, om, rs, sp, shifted = _lru_gates(ext_s, tm, cw_, cb_ref[...], wr_ref, br_ref[...], wi_ref,
                                                          bi_ref[...], lam_)
        mult = jnp.where(om > 0.0, om * rs, 0.0)
        a_ext[0:tm, :] = a
        c_s[...] = a_ext[pl.ds(1, tm), :]
        g_s[...] = lax.dot_general(dy_ref[...], wlo_ref[...], (((1,), (1,)), ((), ())), preferred_element_type=F32)
        row = lax.broadcasted_iota(jnp.int32, (8, D), 0)

        def step(k, nxt):
            ci = nch - 1 - k
            sl = pl.ds(pl.multiple_of(ci * 8, 8), 8)
            cc, gg = c_s[sl, :], g_s[sl, :]
            for s in (1, 2, 4):
                m = row < 8 - s
                gg = gg + cc * jnp.where(m, pltpu.roll(gg, 8 - s, axis=0), 0.0)
                cc = cc * jnp.where(m, pltpu.roll(cc, 8 - s, axis=0), 1.0)
            dh = gg + cc * nxt
            dh_s[sl, :] = dh
            return jnp.broadcast_to(dh[0:1, :], (8, D))

        dh_carry[...] = lax.fori_loop(0, nch, step, dh_carry[...])
        a_ext[tm:, :] = a[0:8, :]
        dh = dh_s[...]
        h_ext[0:8, :] = hh_ref[...] * keep
        hv = h_ref[...]
        h_ext[8:, :] = hv
        hprev = h_ext[pl.ds(7, tm), :]
        dwlo = lax.dot_general(hv.astype(BF), dy_ref[...], (((0,), (0,)), ((), ())), preferred_element_type=F32)
        iv = ig * v
        da = dh * hprev
        dmult = dh * iv
        div = dh * mult
        dlog = da * a - dmult * (a * a) * rs
        dr = dlog * (-LRU_C * sp)
        dlam = jnp.sum(dlog * r, axis=0, keepdims=True) * (LRU_C * _sigmoid(-lam_))
        di = div * v
        dv = div * ig
        drp = dr * r * (1.0 - r)
        dip = di * ig * (1.0 - ig)
        drb, dib = drp.astype(BF), dip.astype(BF)
        dvh, dwr, dwi = [], [], []
        nt_dims = (((1,), (1,)), ((), ()))
        tn_dims = (((0,), (0,)), ((), ()))
        for hd in range(LRU_HEADS):
            sl = slice(hd * LRU_HD, (hd + 1) * LRU_HD)
            dvh.append(lax.dot_general(drb[:, sl], wr_ref[hd], nt_dims, preferred_element_type=F32)
                       + lax.dot_general(dib[:, sl], wi_ref[hd], nt_dims, preferred_element_type=F32))
            dwr.append(lax.dot_general(vb[:, sl], drb[:, sl], tn_dims, preferred_element_type=F32))
            dwi.append(lax.dot_general(vb[:, sl], dib[:, sl], tn_dims, preferred_element_type=F32))
        dv = dv + jnp.concatenate(dvh, axis=1)
        dv_ext[0:tm, :] = dv
        dzl = cw_[3:4, :] * dv
        for k in range(3):
            dzl = dzl + cw_[k:k + 1, :] * dv_ext[pl.ds(3 - k, tm), :]
        dz_ref[...] = dzl.astype(dz_ref.dtype)
        dv_ext[tm:, :] = dv[:HALO]
        dcw = jnp.concatenate([jnp.sum(dv * shifted[k], axis=0, keepdims=True) for k in range(4)], axis=0)
        dcb = jnp.sum(dv, axis=0, keepdims=True)
        dbr = jnp.sum(drp, axis=0, keepdims=True)
        dbi = jnp.sum(dip, axis=0, keepdims=True)

        @pl.when(i == 0)
        def _():
            dcw_ref[...] = dcw
            dcb_ref[...] = dcb
            dbr_ref[...] = dbr
            dbi_ref[...] = dbi
            dlam_ref[...] = dlam
            wlo_acc[...] = dwlo
            for hd in range(LRU_HEADS):
                dwr_ref[hd] = dwr[hd]
                dwi_ref[hd] = dwi[hd]

        @pl.when(i > 0)
        def _():
            dcw_ref[...] += dcw
            dcb_ref[...] += dcb
            dbr_ref[...] += dbr
            dbi_ref[...] += dbi
            dlam_ref[...] += dlam
            wlo_acc[...] += dwlo
            for hd in range(LRU_HEADS):
                dwr_ref[hd] += dwr[hd]
                dwi_ref[hd] += dwi[hd]

        @pl.when(i == nt - 1)
        def _():
            dwlo_ref[...] = wlo_acc[...].astype(BF)

    fix2 = lambda i: (0, 0)
    fix3 = lambda i: (0, 0, 0)
    rev = lambda i: (nt - 1 - i, 0)
    vec = pl.BlockSpec((1, D), fix2)
    hw = pl.BlockSpec((LRU_HEADS, LRU_HD, LRU_HD), fix3)
    return pl.pallas_call(
        body, name=name, grid=(nt,),
        in_specs=[pl.BlockSpec(memory_space=pl.ANY),
                  pl.BlockSpec((tm, D), rev),
                  pl.BlockSpec((tm, D), lambda i: (nt - 1 - i, ZB_LRU)),
                  pl.BlockSpec((HALO, D), lambda i: (jnp.maximum((nt - 1 - i) * (tm // HALO) - 1, 0), ZB_LRU)),
                  pl.BlockSpec((tm, D), rev),
                  pl.BlockSpec((8, D), lambda i: (jnp.maximum((nt - 1 - i) * (tm // 8) - 1, 0), 0)),
                  pl.BlockSpec((4, D), fix2), vec, hw, vec, hw, vec, vec, pl.BlockSpec((D, D), fix2)],
        out_specs=[pl.BlockSpec((tm, D), lambda i: (nt - 1 - i, ZB_LRU)),
                   pl.BlockSpec((4, D), fix2), vec, hw, vec, hw, vec, vec, pl.BlockSpec((D, D), fix2)],
        out_shape=[jax.ShapeDtypeStruct(dz.shape, dz.dtype), jax.ShapeDtypeStruct((4, D), F32),
                   jax.ShapeDtypeStruct((1, D), F32), jax.ShapeDtypeStruct((LRU_HEADS, LRU_HD, LRU_HD), F32),
                   jax.ShapeDtypeStruct((1, D), F32), jax.ShapeDtypeStruct((LRU_HEADS, LRU_HD, LRU_HD), F32),
                   jax.ShapeDtypeStruct((1, D), F32), jax.ShapeDtypeStruct((1, D), F32),
                   jax.ShapeDtypeStruct((D, D), BF)],
        scratch_shapes=[pltpu.VMEM((tm, D), F32), pltpu.VMEM((tm, D), F32), pltpu.VMEM((tm, D), F32),
                        pltpu.VMEM((8, D), F32), pltpu.VMEM((tm + 8, D), F32), pltpu.VMEM((tm + HALO, D), F32),
                        pltpu.VMEM((tm + HALO, D), F32), pltpu.VMEM((tm + 8, D), F32), pltpu.VMEM((D, D), F32)],
        input_output_aliases={0: 0},
        compiler_params=_cp("arbitrary"))(dz, dyl, z, z, h, h, cw, cb, wr, br, wi, bi, lam, wlo)


def _sconv_cv(u_ext, sw):
    shifted = []
    cv = None
    for k in range(3):
        us = (u_ext if k == 2 else pltpu.roll(u_ext, 2 - k, axis=0))[HALO:]
        shifted.append(us)
        term = sw[k:k + 1, :] * us
        cv = term if cv is None else cv + term
    return cv, shifted


def sconv_fwd(z, sw, wso, *, tm, name):
    T = z.shape[0]
    tm = min(tm, T)

    def body(zm_ref, zh_ref, sw_ref, wso_ref, s_ref, y_ref):
        i = pl.program_id(0)
        keep = jnp.where(i == 0, 0.0, 1.0).astype(F32)
        zm = zm_ref[...].astype(F32)
        zh = zh_ref[...].astype(F32)
        u_ext = jnp.concatenate([zh[:, D:2 * D] * zh[:, 2 * D:] * keep, zm[:, D:2 * D] * zm[:, 2 * D:]], axis=0)
        cv, _ = _sconv_cv(u_ext, sw_ref[...])
        s = (zm[:, :D] * cv).astype(BF)
        s_ref[...] = s
        y_ref[...] = jnp.dot(s, wso_ref[...], preferred_element_type=F32).astype(BF)

    return pl.pallas_call(
        body, name=name, grid=(T // tm,),
        in_specs=[pl.BlockSpec((tm, 3 * D), lambda i: (i, ZB_SCONV)),
                  pl.BlockSpec((HALO, 3 * D), _prev_halo(tm, ZB_SCONV)),
                  pl.BlockSpec((3, D), lambda i: (0, 0)), pl.BlockSpec((D, D), lambda i: (0, 0))],
        out_specs=[pl.BlockSpec((tm, D), lambda i: (i, 0)), pl.BlockSpec((tm, D), lambda i: (i, 0))],
        out_shape=[jax.ShapeDtypeStruct((T, D), BF), jax.ShapeDtypeStruct((T, D), BF)],
        compiler_params=_cp("arbitrary"))(z, z, sw, wso)


def sconv_bwd(dz, dyc, z, sw, wso, *, tm, name):
    T = z.shape[0]
    tm = min(tm, T)
    nt = T // tm

    def body(dz_in, dy_ref, dyn_ref, zm_ref, zp_ref, zn_ref, sw_ref, wso_ref, dz_ref, dsw_ref, dwso_ref, acc_ref):
        del dz_in
        i = pl.program_id(0)
        keep_p = jnp.where(i == 0, 0.0, 1.0).astype(F32)
        keep_n = jnp.where(i == nt - 1, 0.0, 1.0).astype(F32)
        sw_ = sw_ref[...]
        zm = zm_ref[...].astype(F32)
        zp = zp_ref[...].astype(F32)
        zb, zc, zh = zm[:, :D], zm[:, D:2 * D], zm[:, 2 * D:]
        u_ext = jnp.concatenate([zp[:, D:2 * D] * zp[:, 2 * D:] * keep_p, zc * zh], axis=0)
        cv, shifted = _sconv_cv(u_ext, sw_)
        dy_ext = jnp.concatenate([dy_ref[...], dyn_ref[...]], axis=0)
        ds_ext = lax.dot_general(dy_ext, wso_ref[...], (((1,), (1,)), ((), ())), preferred_element_type=F32)
        zb_ext = jnp.concatenate([zb, zn_ref[...][:, :D].astype(F32) * keep_n], axis=0)
        dcv_ext = ds_ext * zb_ext
        du = sw_[2:3, :] * dcv_ext[:tm]
        for k in range(2):
            du = du + sw_[k:k + 1, :] * pltpu.roll(dcv_ext, tm + HALO - (2 - k), axis=0)[:tm]
        dz_ref[...] = jnp.concatenate([ds_ext[:tm] * cv, du * zh, du * zc], axis=1).astype(dz_ref.dtype)
        dcv = dcv_ext[:tm]
        dsw = jnp.concatenate([jnp.sum(dcv * shifted[k], axis=0, keepdims=True) for k in range(3)], axis=0)
        dwso = lax.dot_general((zb * cv).astype(BF), dy_ref[...], (((0,), (0,)), ((), ())),
                               preferred_element_type=F32)

        @pl.when(i == 0)
        def _():
            dsw_ref[...] = dsw
            acc_ref[...] = dwso

        @pl.when(i > 0)
        def _():
            dsw_ref[...] += dsw
            acc_ref[...] += dwso

        @pl.when(i == nt - 1)
        def _():
            dwso_ref[...] = acc_ref[...].astype(BF)

    return pl.pallas_call(
        body, name=name, grid=(nt,),
        in_specs=[pl.BlockSpec(memory_space=pl.ANY),
                  pl.BlockSpec((tm, D), lambda i: (i, 0)), pl.BlockSpec((HALO, D), _next_halo(tm, T, 0)),
                  pl.BlockSpec((tm, 3 * D), lambda i: (i, ZB_SCONV)),
                  pl.BlockSpec((HALO, 3 * D), _prev_halo(tm, ZB_SCONV)),
                  pl.BlockSpec((HALO, 3 * D), _next_halo(tm, T, ZB_SCONV)),
                  pl.BlockSpec((3, D), lambda i: (0, 0)), pl.BlockSpec((D, D), lambda i: (0, 0))],
        out_specs=[pl.BlockSpec((tm, 3 * D), lambda i: (i, ZB_SCONV)), pl.BlockSpec((3, D), lambda i: (0, 0)),
                   pl.BlockSpec((D, D), lambda i: (0, 0))],
        out_shape=[jax.ShapeDtypeStruct(dz.shape, dz.dtype), jax.ShapeDtypeStruct((3, D), F32),
                   jax.ShapeDtypeStruct((D, D), BF)],
        scratch_shapes=[pltpu.VMEM((D, D), F32)],
        input_output_aliases={0: 0}, compiler_params=_cp("arbitrary"))(dz, dyc, dyc, z, z, z, sw, wso)


def merge_fwd(z, yp_pre, yl, yc, ps, *, tm, name):
    T = z.shape[0]
    tm = min(tm, T)

    def body(zg_ref, yp_ref, yl_ref, yc_ref, ps_ref, o_ref):
        gts = _sigmoid(zg_ref[...].astype(F32))
        m = (gts[:, :D] * (yp_ref[...].astype(F32) * ps_ref[...]) + gts[:, D:2 * D] * yl_ref[...].astype(F32)
             + gts[:, 2 * D:] * yc_ref[...].astype(F32))
        o_ref[...] = m.astype(o_ref.dtype)

    row = lambda i: (i, 0)
    return pl.pallas_call(
        body, name=name, grid=(T // tm,),
        in_specs=[pl.BlockSpec((tm, 3 * D), lambda i: (i, ZB_GATE)), pl.BlockSpec((tm, D), row),
                  pl.BlockSpec((tm, D), row), pl.BlockSpec((tm, D), row), pl.BlockSpec((1, D), lambda i: (0, 0))],
        out_specs=pl.BlockSpec((tm, D), row), out_shape=jax.ShapeDtypeStruct((T, D), BF),
        compiler_params=_cp("arbitrary"))(z, yp_pre, yl, yc, ps)


def merge_bwd(dm, z, yp_pre, yl, yc, ps, *, tm, name):
    T = z.shape[0]
    tm = min(tm, T)

    def body(dm_ref, zg_ref, yp_ref, yl_ref, yc_ref, ps_ref, dz_ref, dyp_ref, dyl_ref, dyc_ref):
        gts = _sigmoid(zg_ref[...].astype(F32))
        dmv = dm_ref[...].astype(F32)
        ys = (yp_ref[...].astype(F32) * ps_ref[...], yl_ref[...].astype(F32), yc_ref[...].astype(F32))
        outs = (dyp_ref, dyl_ref, dyc_ref)
        dgs = []
        for j in range(3):
            gj = gts[:, j * D:(j + 1) * D]
            outs[j][...] = (dmv * gj).astype(BF)
            dgs.append(dmv * ys[j] * gj * (1.0 - gj))
        dz_ref[...] = jnp.concatenate(dgs, axis=1).astype(dz_ref.dtype)

    row = lambda i: (i, 0)
    return pl.pallas_call(
        body, name=name, grid=(T // tm,),
        in_specs=[pl.BlockSpec((tm, D), row), pl.BlockSpec((tm, 3 * D), lambda i: (i, ZB_GATE)),
                  pl.BlockSpec((tm, D), row), pl.BlockSpec((tm, D), row), pl.BlockSpec((tm, D), row),
                  pl.BlockSpec((1, D), lambda i: (0, 0))],
        out_specs=[pl.BlockSpec((tm, 3 * D), lambda i: (i, ZB_GATE)), pl.BlockSpec((tm, D), row),
                   pl.BlockSpec((tm, D), row), pl.BlockSpec((tm, D), row)],
        out_shape=[jax.ShapeDtypeStruct((T, IN_COLS), BF), jax.ShapeDtypeStruct((T, D), BF),
                   jax.ShapeDtypeStruct((T, D), BF), jax.ShapeDtypeStruct((T, D), BF)],
        compiler_params=_cp("arbitrary"))(dm, z, yp_pre, yl, yc, ps)


def _attn_probs(qh, kh):
    s = lax.dot_general(qh, kh, (((1,), (1,)), ((), ())), preferred_element_type=F32) * (X_HD ** -0.5)
    e = jnp.exp(s - jnp.max(s, axis=-1, keepdims=True))
    return e / jnp.sum(e, axis=-1, keepdims=True)


def attn_fwd(xb, wq, kb, vb, *, tm, name):
    T = xb.shape[0]
    tm = min(tm, T)

    def body(x_ref, wq_ref, k_ref, v_ref, q_ref, o_ref):
        q = jnp.dot(x_ref[...], wq_ref[...], preferred_element_type=F32).astype(BF)
        q_ref[...] = q
        outs = []
        for h in range(X_HEADS):
            sl = slice(h * X_HD, (h + 1) * X_HD)
            p = _attn_probs(q[:, sl], k_ref[:, sl])
            outs.append(jnp.dot(p.astype(BF), v_ref[:, sl], preferred_element_type=F32))
        o_ref[...] = jnp.concatenate(outs, axis=1).astype(BF)

    row = lambda i: (i, 0)
    fix = lambda i: (0, 0)
    return pl.pallas_call(
        body, name=name, grid=(T // tm,),
        in_specs=[pl.BlockSpec((tm, D), row), pl.BlockSpec((D, D), fix), pl.BlockSpec((N_MEM, D), fix),
                  pl.BlockSpec((N_MEM, D), fix)],
        out_specs=[pl.BlockSpec((tm, D), row), pl.BlockSpec((tm, D), row)],
        out_shape=[jax.ShapeDtypeStruct((T, D), BF), jax.ShapeDtypeStruct((T, D), BF)],
        compiler_params=_cp("arbitrary"))(xb, wq, kb, vb)


def attn_bwd(dxa, wo, q, kb, vb, xb, *, tm, name):
    T = q.shape[0]
    tm = min(tm, T)
    nt = T // tm

    def body(d_ref, wo_ref, q_ref, k_ref, v_ref, x_ref, dq_ref, dk_ref, dv_ref, dwq_ref, acc_ref):
        i = pl.program_id(0)
        do = lax.dot_general(d_ref[...], wo_ref[...], (((1,), (1,)), ((), ())),
                             preferred_element_type=F32).astype(BF)
        q = q_ref[...]
        dqs, dks, dvs = [], [], []
        for h in range(X_HEADS):
            sl = slice(h * X_HD, (h + 1) * X_HD)
            kh, vh = k_ref[:, sl], v_ref[:, sl]
            p = _attn_probs(q[:, sl], kh)
            dp = lax.dot_general(do[:, sl], vh, (((1,), (1,)), ((), ())), preferred_element_type=F32)
            ds = (p * (dp - jnp.sum(dp * p, axis=-1, keepdims=True)) * (X_HD ** -0.5)).astype(BF)
            dqs.append(jnp.dot(ds, kh, preferred_element_type=F32))
            dks.append(lax.dot_general(ds, q[:, sl], (((0,), (0,)), ((), ())), preferred_element_type=F32))
            dvs.append(lax.dot_general(p.astype(BF), do[:, sl], (((0,), (0,)), ((), ())),
                                       preferred_element_type=F32))
        dqb = jnp.concatenate(dqs, axis=1).astype(BF)
        dq_ref[...] = dqb
        dk = jnp.concatenate(dks, axis=1)
        dv = jnp.concatenate(dvs, axis=1)
        dwq = lax.dot_general(x_ref[...], dqb, (((0,), (0,)), ((), ())), preferred_element_type=F32)

        @pl.when(i == 0)
        def _():
            dk_ref[...] = dk
            dv_ref[...] = dv
            acc_ref[...] = dwq

        @pl.when(i > 0)
        def _():
            dk_ref[...] += dk
            dv_ref[...] += dv
            acc_ref[...] += dwq

        @pl.when(i == nt - 1)
        def _():
            dwq_ref[...] = acc_ref[...].astype(BF)

    row = lambda i: (i, 0)
    fix = lambda i: (0, 0)
    return pl.pallas_call(
        body, name=name, grid=(nt,),
        in_specs=[pl.BlockSpec((tm, D), row), pl.BlockSpec((D, D), fix), pl.BlockSpec((tm, D), row),
                  pl.BlockSpec((N_MEM, D), fix), pl.BlockSpec((N_MEM, D), fix), pl.BlockSpec((tm, D), row)],
        out_specs=[pl.BlockSpec((tm, D), row), pl.BlockSpec((N_MEM, D), fix), pl.BlockSpec((N_MEM, D), fix),
                   pl.BlockSpec((D, D), fix)],
        out_shape=[jax.ShapeDtypeStruct((T, D), BF), jax.ShapeDtypeStruct((N_MEM, D), F32),
                   jax.ShapeDtypeStruct((N_MEM, D), F32), jax.ShapeDtypeStruct((D, D), BF)],
        scratch_shapes=[pltpu.VMEM((D, D), F32)],
        compiler_params=_cp("arbitrary"))(dxa, wo, q, kb, vb, xb)


def swiglu_fwd(gu, *, tm, name):
    T = gu.shape[0]
    tm = min(tm, T)

    def body(g_ref, u_ref, o_ref):
        g = g_ref[...].astype(F32)
        o_ref[...] = (g * _sigmoid(g) * u_ref[...].astype(F32)).astype(BF)

    return pl.pallas_call(
        body, name=name, grid=(T // tm,),
        in_specs=[pl.BlockSpec((tm, D_FF), lambda i: (i, 0)), pl.BlockSpec((tm, D_FF), lambda i: (i, 1))],
        out_specs=pl.BlockSpec((tm, D_FF), lambda i: (i, 0)), out_shape=jax.ShapeDtypeStruct((T, D_FF), BF),
        compiler_params=_cp("arbitrary"))(gu, gu)


def swiglu_bwd(dh, gu, *, tm, name):
    T = gu.shape[0]
    tm = min(tm, T)

    def body(dh_ref, g_ref, u_ref, o_ref):
        g = g_ref[...].astype(F32)
        u = u_ref[...].astype(F32)
        dhv = dh_ref[...].astype(F32)
        sg = _sigmoid(g)
        o_ref[:, :D_FF] = (dhv * u * sg * (1.0 + g * (1.0 - sg))).astype(BF)
        o_ref[:, D_FF:] = (dhv * g * sg).astype(BF)

    return pl.pallas_call(
        body, name=name, grid=(T // tm,),
        in_specs=[pl.BlockSpec((tm, D_FF), lambda i: (i, 0)), pl.BlockSpec((tm, D_FF), lambda i: (i, 0)),
                  pl.BlockSpec((tm, D_FF), lambda i: (i, 1))],
        out_specs=pl.BlockSpec((tm, 2 * D_FF), lambda i: (i, 0)),
        out_shape=jax.ShapeDtypeStruct((T, 2 * D_FF), BF), compiler_params=_cp("arbitrary"))(dh, gu, gu)


TM_MM = 1024
TM_EW = 512
TM_SEQ = 512
TT_DW = 2048


def _mem_kv(l, memb, W):
    kb = mm_nn(memb, W['xa_w_k'][l], None, out_dtype=BF, tm=N_MEM, tn=1024, name=f"l{l}_mem_k")
    vb = mm_nn(memb, W['xa_w_v'][l], None, out_dtype=BF, tm=N_MEM, tn=1024, name=f"l{l}_mem_v")
    return kb, vb


def _layer_fwd(l, x, xb, kb, vb, W, host=None, host2=None, after_in_proj=None):
    n = f"l{l}_"
    sv = {'x0': x if xb is None else xb}
    z = mm_nn(sv['x0'], W['w_in'][l], W['b_in'][l], out_dtype=BF, tm=TM_MM, tn=1024, name=n + "in_proj", host=host)
    if host is not None:
        z, sv['hosted'] = z
        if after_in_proj is not None:
            after_in_proj(sv['hosted'])
    if kb is None:
        kb, vb = _mem_kv(l, vb, W)
    sv['kb'], sv['vb'] = kb, vb
    yp = pool_fwd(z, W['pool_w'][l], tm=TM_SEQ, name=n + "pool_fwd")
    h, yl = lru_fwd(z, W['lru_conv_w'][l], W['lru_conv_b'][l], W['lru_w_r'][l], W['lru_b_r'][l], W['lru_w_i'][l],
                    W['lru_b_i'][l], W['lru_lambda'][l], W['lru_w_out'][l], tm=TM_SEQ, name=n + "lru_fwd")
    s, yc = sconv_fwd(z, W['sconv_w'][l], W['sconv_w_out'][l], tm=TM_SEQ, name=n + "sconv_fwd")
    merged = merge_fwd(z, yp, yl, yc, W['pool_scale'][l], tm=TM_EW, name=n + "merge_fwd")
    x1, x1b, xh1, rs1 = mm_res_ln(merged, W['w_mix_out'][l], x, W['ln_g'][l][0:1], W['ln_b'][l][0:1], tm=TM_EW,
                                  name=n + "mix_out_ln")
    q, o = attn_fwd(x1b, W['xa_w_q'][l], kb, vb, tm=TM_EW, name=n + "attn_fwd")
    x2, x2b, xh2, rs2 = mm_res_ln(o, W['xa_w_o'][l], x1, W['ln_g'][l][1:2], W['ln_b'][l][1:2], tm=TM_EW,
                                  name=n + "attn_out_ln")
    res = ffn_in_swiglu(x2b, W['ffn_w_gu'][l], tm=TM_MM, name=n + "ffn_in", host=host2)
    gu, hdn = res[:2]
    if host2 is not None:
        sv['hosted2'] = res[2]
    x3, x3b, xh3, rs3 = mm_res_ln(hdn, W['ffn_w_down'][l], x2, W['ln_g'][l][2:3], W['ln_b'][l][2:3], tm=TM_EW,
                                  name=n + "ffn_out_ln")
    sv.update(z=z, yp=yp, h=h, yl=yl, s=s, yc=yc, merged=merged, x1b=x1b, xh1=xh1, rs1=rs1, q=q, o=o, x2b=x2b,
              xh2=xh2, rs2=rs2, gu=gu, hdn=hdn, xh3=xh3, rs3=rs3)
    return x3, x3b, sv


def _layer_bwd(l, dx3, sv, memb, kb, vb, W, loss_from=None, host=None, last_host_fn=None):
    n = f"l{l}_"
    G = {}
    res = ln_bwd(dx3, sv['xh3'], sv['rs3'], W['ln_g'][l][2:3], tm=TM_EW, name=n + "ln3_bwd", loss_from=loss_from,
                 dw_of=sv['hdn'])
    dp3, dp3b, dg3, db3 = res[:4]
    loss = res[4] if loss_from is not None else None
    G['ffn_w_down'] = res[-1]
    dgu = ffn_down_dx_swiglu(dp3b, W['ffn_w_down'][l], sv['gu'], tm=TM_EW, name=n + "ffn_down_dx")
    dx2 = mm_nt(dgu, W['ffn_w_gu'][l], dp3, out_dtype=F32, tm=TM_MM, tc=1408, name=n + "ffn_in_dx", host=host)
    if host is not None:
        dx2, G['hosted'] = dx2
    G['ffn_w_gu'] = mm_tn(sv['x2b'], dgu, out_dtype=BF, tk=1024, tn=1408, tt=TT_DW,name=n + "ffn_in_dw")

    dp2, dp2b, dg2, db2, G['xa_w_o'] = ln_bwd(dx2, sv['xh2'], sv['rs2'], W['ln_g'][l][1:2], tm=TM_EW,
                                              name=n + "ln2_bwd", dw_of=sv['o'])
    dq, dk, dv, G['xa_w_q'] = attn_bwd(dp2b, W['xa_w_o'][l], sv['q'], kb, vb, sv['x1b'], tm=TM_EW,
                                       name=n + "attn_bwd")
    dx1 = mm_nt(dq, W['xa_w_q'][l], dp2, out_dtype=F32, tm=TM_MM, tc=D, name=n + "attn_q_dx")
    G['xa_w_k'] = mm_tn(memb, dk, out_dtype=BF, tk=1024, tn=1024, tt=N_MEM, name=n + "attn_k_dw")
    G['xa_w_v'] = mm_tn(memb, dv, out_dtype=BF, tk=1024, tn=1024, tt=N_MEM, name=n + "attn_v_dw")

    dp1, dp1b, dg1, db1, G['w_mix_out'] = ln_bwd(dx1, sv['xh1'], sv['rs1'], W['ln_g'][l][0:1], tm=TM_EW,
                                                 name=n + "ln1_bwd", dw_of=sv['merged'])
    dmerged = mm_nt(dp1b, W['w_mix_out'][l], None, out_dtype=BF, tm=TM_MM, tc=D, name=n + "mix_out_dx")
    z = sv['z']
    dz, dyp, dyl, dyc = merge_bwd(dmerged, z, sv['yp'], sv['yl'], sv['yc'], W['pool_scale'][l], tm=TM_EW,
                                  name=n + "merge_bwd")
    dz, G['pool_w'], G['pool_scale'] = pool_bwd(dz, dyp, sv['yp'], z, W['pool_w'][l], W['pool_scale'][l],
                                                tm=TM_SEQ, name=n + "pool_bwd")
    (dz, G['lru_conv_w'], G['lru_conv_b'], G['lru_w_r'], G['lru_b_r'], G['lru_w_i'], G['lru_b_i'],
     G['lru_lambda'], G['lru_w_out']) = lru_bwd(dz, dyl, z, sv['h'], W['lru_conv_w'][l], W['lru_conv_b'][l],
                                                W['lru_w_r'][l], W['lru_b_r'][l], W['lru_w_i'][l], W['lru_b_i'][l],
                                                W['lru_lambda'][l], W['lru_w_out'][l], tm=TM_SEQ, name=n + "lru_bwd")
    dz, G['sconv_w'], G['sconv_w_out'] = sconv_bwd(dz, dyc, z, W['sconv_w'][l], W['sconv_w_out'][l], tm=TM_SEQ,
                                                   name=n + "sconv_bwd")
    G['w_in'], G['b_in'] = mm_tn(sv['x0'], dz, out_dtype=BF, tk=1024, tn=1024, tt=TT_DW,name=n + "in_proj_dw",
                                 colsum=True)
    G['ln_g'] = jnp.concatenate([dg1, dg2, dg3], axis=0)
    G['ln_b'] = jnp.concatenate([db1, db2, db3], axis=0)
    last_host = None if last_host_fn is None else last_host_fn(G)
    dx0 = mm_nt(dz, W['w_in'][l], dp1, out_dtype=F32, tm=TM_MM, tc=2048, name=n + "in_proj_dx", host=last_host)
    if last_host is not None:
        dx0, G['hosted_last'] = dx0
    return dx0, G, loss


def local_step(x, mem, target, W):
    memb = mem.astype(BF)
    saves, kvs = [], []
    xf, xb = x, None
    for l in range(DEPTH):
        kb = mm_nn(memb, W['xa_w_k'][l], None, out_dtype=BF, tm=N_MEM, tn=1024, name=f"l{l}_mem_k")
        vb = mm_nn(memb, W['xa_w_v'][l], None, out_dtype=BF, tm=N_MEM, tn=1024, name=f"l{l}_mem_v")
        xf, xb, sv = _layer_fwd(l, xf, xb, kb, vb, W)
        saves.append(sv)
        kvs.append((kb, vb))
    grads = [None] * DEPTH
    dx, loss = None, None
    for l in reversed(range(DEPTH)):
        lf = (W['ln_b'][l][2:3], target) if l == DEPTH - 1 else None
        dx, grads[l], ls = _layer_bwd(l, dx, saves[l], memb, kvs[l][0], kvs[l][1], W, loss_from=lf)
        if ls is not None:
            loss = ls
    return loss, dx, grads


def _coords():
    return lax.axis_index("x"), lax.axis_index("y"), lax.axis_index("c")


FLIPS = ((1, 0), (0, 1), (1, 1))
SQUARES = ('lru_w_out', 'sconv_w_out', 'w_mix_out', 'xa_w_q', 'xa_w_k', 'xa_w_v', 'xa_w_o')
LAYER_SHAPE = {'w_in': (D, IN_COLS), 'pool_w': (4, POOL_GD, POOL_GD), 'ffn_w_gate': (4, D, D_FF // 4),
               'ffn_w_up': (4, D, D_FF // 4), 'ffn_w_down': (D_FF, D), **{n: (D, D) for n in SQUARES}}
PIECES = ('w_in', 'pool_w') + SQUARES + ('ffn_w_gate', 'ffn_w_up', 'ffn_w_down')


def _mult(v, m):
    return v if isinstance(v, int) else pl.multiple_of(v, m)


def _win(name, ref, k):
    if name == 'w_in':
        return ref.at[:, pl.ds(_mult(((2 * k + 6) % 8) * D, D), 2 * D)]
    if name == 'pool_w':
        return ref.at[:, pl.ds(_mult(k * (POOL_GD // 4), POOL_GD // 4), POOL_GD // 4), :]
    if name in ('ffn_w_gate', 'ffn_w_up'):
        return ref.at[k]
    rows = LAYER_SHAPE[name][0] // 4
    return ref.at[pl.ds(_mult(k * rows, 16), rows), :]


def _half_shape(name):
    shard = _shard_shape(name)
    return (shard[0] // 2,) + shard[1:]


def _half(name, ref, h):
    rows = _shard_shape(name)[0] // 2
    if name == 'pool_w':
        return ref.at[pl.ds(h * rows, rows)]
    return ref.at[pl.ds(_mult(h * rows, 16), rows), :]


def gather_weights(shards, small):
    n_p = len(PIECES)

    def body(*refs):
        srcs = dict(zip(PIECES, refs[:n_p]))
        small_ref = refs[n_p]
        outs = [dict(zip(PIECES, refs[n_p + 1 + l * n_p:n_p + 1 + (l + 1) * n_p])) for l in range(DEPTH)]
        gs_ref = refs[n_p + 1 + DEPTH * n_p]
        ici_send, ici_recv, d2d_send, d2d_recv, own_send, own_recv = refs[n_p + 2 + DEPTH * n_p:]
        x, y, c = _coords()
        me = 2 * x + y
        sib = (x, y, 1 - c)

        def own_copies():
            cps = []
            li = 0
            for n in PIECES:
                for l in range(DEPTH):
                    cps.append(pltpu.make_async_remote_copy(
                        src_ref=srcs[n].at[l], dst_ref=_win(n, outs[l][n], me), send_sem=own_send.at[li],
                        recv_sem=own_recv.at[li], device_id=sib, device_id_type=MESH))
                    li += 1
            cps.append(pltpu.make_async_remote_copy(
                src_ref=small_ref, dst_ref=gs_ref.at[me], send_sem=own_send.at[li], recv_sem=own_recv.at[li],
                device_id=sib, device_id_type=MESH))
            return cps

        def run(lc):
            sends = []
            for j, (fx, fy) in enumerate(FLIPS):
                peer = (x ^ fx, y ^ fy, c)
                for p, n in enumerate(PIECES):
                    k = 3 * p + j
                    sends.append(pltpu.make_async_remote_copy(
                        src_ref=srcs[n].at[lc], dst_ref=_win(n, outs[lc][n], me), send_sem=ici_send.at[k],
                        recv_sem=ici_recv.at[k], device_id=peer, device_id_type=MESH))
                k = 3 * n_p + j
                sends.append(pltpu.make_async_remote_copy(
                    src_ref=small_ref, dst_ref=gs_ref.at[me], send_sem=ici_send.at[k], recv_sem=ici_recv.at[k],
                    device_id=peer, device_id_type=MESH))
            own = own_copies()
            for cp in sends + own:
                cp.start()
            for j, (fx, fy) in enumerate(FLIPS):
                other = 2 * (x ^ fx) + (y ^ fy)
                for p, n in enumerate(PIECES):
                    k = 3 * p + j
                    w = _win(n, outs[lc][n], other)
                    pltpu.make_async_remote_copy(src_ref=srcs[n].at[lc], dst_ref=w, send_sem=ici_send.at[k],
                                                 recv_sem=ici_recv.at[k], device_id=sib,
                                                 device_id_type=MESH).wait_recv()
                    fw = pltpu.make_async_remote_copy(src_ref=w, dst_ref=w, send_sem=d2d_send.at[k],
                                                      recv_sem=d2d_recv.at[k], device_id=sib, device_id_type=MESH)
                    fw.start()
                    sends.append(fw)
                k = 3 * n_p + j
                pltpu.make_async_remote_copy(src_ref=small_ref, dst_ref=gs_ref.at[other], send_sem=ici_send.at[k],
                                             recv_sem=ici_recv.at[k], device_id=sib, device_id_type=MESH).wait_recv()
            for j, (fx, fy) in enumerate(FLIPS):
                other = 2 * (x ^ fx) + (y ^ fy)
                for p, n in enumerate(PIECES):
                    k = 3 * p + j
                    w = _win(n, outs[1 - lc][n], other)
                    pltpu.make_async_remote_copy(src_ref=w, dst_ref=w, send_sem=d2d_send.at[k],
                                                 recv_sem=d2d_recv.at[k], device_id=sib,
                                                 device_id_type=MESH).wait_recv()
            for cp in sends:
                cp.wait_send()
            for cp in own:
                cp.wait()

        @pl.when(c == 0)
        def _():
            run(0)

        @pl.when(c == 1)
        def _():
            run(1)

    hbm = pl.BlockSpec(memory_space=pl.ANY)
    n_out = DEPTH * n_p + 1
    res = pl.pallas_call(
        body, name="gather_weights", in_specs=[hbm] * (n_p + 1), out_specs=[hbm] * n_out,
        out_shape=[jax.ShapeDtypeStruct(LAYER_SHAPE[n], BF) for _ in range(DEPTH) for n in PIECES]
        + [jax.ShapeDtypeStruct((4,) + small.shape, small.dtype)],
        scratch_shapes=[pltpu.SemaphoreType.DMA((3 * n_p + 3,)), pltpu.SemaphoreType.DMA((3 * n_p + 3,)),
                        pltpu.SemaphoreType.DMA((3 * n_p,)), pltpu.SemaphoreType.DMA((3 * n_p,)),
                        pltpu.SemaphoreType.DMA((DEPTH * n_p + 1,)), pltpu.SemaphoreType.DMA((DEPTH * n_p + 1,))],
    )(*[shards[n] for n in PIECES], small)
    full = {n: [res[l * n_p + p] for l in range(DEPTH)] for p, n in enumerate(PIECES)}
    return full, res[DEPTH * n_p]


def split_layers(p0, p1):
    n = len(p0)

    def body(*refs):
        a0, a1 = refs[:n], refs[n:2 * n]
        theirs = refs[2 * n:3 * n]
        send_sems, recv_sems = refs[3 * n:]
        x, y, c = _coords()
        sib = (x, y, 1 - c)

        def give(arrs):
            for i in range(n):
                pltpu.make_async_remote_copy(src_ref=arrs[i], dst_ref=theirs[i], send_sem=send_sems.at[i],
                                             recv_sem=recv_sems.at[i], device_id=sib, device_id_type=MESH).start()

        @pl.when(c == 0)
        def _():
            give(a1)

        @pl.when(c == 1)
        def _():
            give(a0)

        for i in range(n):
            pltpu.make_async_remote_copy(src_ref=a0[i], dst_ref=theirs[i], send_sem=send_sems.at[i],
                                         recv_sem=recv_sems.at[i], device_id=sib, device_id_type=MESH).wait()

    hbm = pl.BlockSpec(memory_space=pl.ANY)
    return pl.pallas_call(
        body, name="split_layers", in_specs=[hbm] * (2 * n), out_specs=[hbm] * n,
        out_shape=[jax.ShapeDtypeStruct(a.shape, a.dtype) for a in p0],
        scratch_shapes=[pltpu.SemaphoreType.DMA((n,)), pltpu.SemaphoreType.DMA((n,))],
    )(*p0, *p1)


def exchange_chips(q, qsmall):
    n_p = len(PIECES)

    def body(*refs):
        srcs = dict(zip(PIECES, refs[:n_p]))
        s_ref = refs[n_p]
        outs = dict(zip(PIECES, refs[n_p + 1:2 * n_p + 1]))
        so_ref = refs[2 * n_p + 1]
        send_sems, recv_sems = refs[2 * n_p + 2:]
        x, y, c = _coords()
        me = 2 * x + y
        sends = []
        for j, (fx, fy) in enumerate(FLIPS):
            peer = (x ^ fx, y ^ fy, c)
            other = 2 * (x ^ fx) + (y ^ fy)
            for p, n in enumerate(PIECES):
                k = 3 * p + j
                sends.append(pltpu.make_async_remote_copy(
                    src_ref=_win(n, srcs[n], other), dst_ref=outs[n].at[j], send_sem=send_sems.at[k],
                    recv_sem=recv_sems.at[k], device_id=peer, device_id_type=MESH))
            k = 3 * n_p + j
            sends.append(pltpu.make_async_remote_copy(
                src_ref=s_ref, dst_ref=so_ref.at[j], send_sem=send_sems.at[k], recv_sem=recv_sems.at[k],
                device_id=peer, device_id_type=MESH))
        for cp in sends:
            cp.start()
        for cp in sends:
            cp.wait()

    hbm = pl.BlockSpec(memory_space=pl.ANY)
    res = pl.pallas_call(
        body, name="exchange_chips", in_specs=[hbm] * (n_p + 1), out_specs=[hbm] * (n_p + 1),
        out_shape=[jax.ShapeDtypeStruct((3,) + _shard_shape(n), BF) for n in PIECES]
        + [jax.ShapeDtypeStruct((3,) + qsmall.shape, qsmall.dtype)],
        scratch_shapes=[pltpu.SemaphoreType.DMA((3 * n_p + 3,)), pltpu.SemaphoreType.DMA((3 * n_p + 3,))],
    )(*[q[n] for n in PIECES], qsmall)
    return dict(zip(PIECES, res[:n_p])), res[n_p]


def _shard_shape(n):
    shp = LAYER_SHAPE[n]
    if n == 'w_in':
        return (shp[0], shp[1] // 4)
    if n == 'pool_w':
        return (shp[0], shp[1] // 4, shp[2])
    if n in ('ffn_w_gate', 'ffn_w_up'):
        return shp[1:]
    return (shp[0] // 4, shp[1])


def swap_cores(s):
    n = len(s)

    def body(*refs):
        srcs, outs = refs[:n], refs[n:2 * n]
        send_sems, recv_sems = refs[2 * n:]
        x, y, c = _coords()
        cps = [pltpu.make_async_remote_copy(src_ref=srcs[i], dst_ref=outs[i], send_sem=send_sems.at[i],
                                            recv_sem=recv_sems.at[i], device_id=(x, y, 1 - c), device_id_type=MESH)
               for i in range(n)]
        for cp in cps:
            cp.start()
        for cp in cps:
            cp.wait()

    hbm = pl.BlockSpec(memory_space=pl.ANY)
    return pl.pallas_call(
        body, name="swap_cores", in_specs=[hbm] * n, out_specs=[hbm] * n,
        out_shape=[jax.ShapeDtypeStruct(a.shape, a.dtype) for a in s],
        scratch_shapes=[pltpu.SemaphoreType.DMA((n,)), pltpu.SemaphoreType.DMA((n,))],
    )(*s)


def add_cores(p0, p1, theirs, *, out_dtype, name):
    shp = p0.shape
    args = [t.reshape(-1, shp[-1]) for t in (p0, p1, theirs)]
    R, C = args[0].shape
    tr = _row_tile(R, C)

    def body(a0_ref, a1_ref, t_ref, o_ref):
        c = lax.axis_index("c")
        t = t_ref[...].astype(F32)

        @pl.when(c == 0)
        def _():
            o_ref[...] = (a0_ref[...].astype(F32) + t).astype(o_ref.dtype)

        @pl.when(c == 1)
        def _():
            o_ref[...] = (a1_ref[...].astype(F32) + t).astype(o_ref.dtype)

    spec = pl.BlockSpec((tr, C), lambda i: (i, 0))
    out = pl.pallas_call(body, name=name, grid=(R // tr,), in_specs=[spec] * 3, out_specs=spec,
                         out_shape=jax.ShapeDtypeStruct((R, C), out_dtype), compiler_params=_cp("arbitrary"))(*args)
    return out.reshape(shp)


def sum_chips(name, q, recv3, chip):
    shard = _shard_shape(name)
    zero = (0,) * len(shard)
    if name == 'w_in':
        tr = 128
        grid = (shard[0] // tr,)
        qspec = pl.BlockSpec((tr, shard[1]), lambda i, me: (i, (me[0] + 3) % 4))
        rspec = pl.BlockSpec((3, tr, shard[1]), lambda i, me: (0, i, 0))
        ospec = pl.BlockSpec((tr, shard[1]), lambda i, me: (i, 0))
    else:
        grid = (1,)
        rspec = pl.BlockSpec((3,) + shard, lambda i, me: (0,) + zero)
        ospec = pl.BlockSpec(shard, lambda i, me: zero)
        if name == 'pool_w':
            qspec = pl.BlockSpec(shard, lambda i, me: (0, me[0], 0))
        elif name in ('ffn_w_gate', 'ffn_w_up'):
            qspec = pl.BlockSpec((None,) + shard, lambda i, me: (me[0], 0, 0))
        else:
            qspec = pl.BlockSpec(shard, lambda i, me: (me[0], 0))

    def body(me_ref, q_ref, r_ref, o_ref):
        del me_ref
        acc = q_ref[...].astype(F32)
        for j in range(3):
            acc = acc + r_ref[j].astype(F32)
        o_ref[...] = acc

    return pl.pallas_call(
        body, name="sum_chips_" + name,
        grid_spec=pltpu.PrefetchScalarGridSpec(num_scalar_prefetch=1, grid=grid, in_specs=[qspec, rspec],
                                               out_specs=ospec),
        out_shape=jax.ShapeDtypeStruct(shard, F32), compiler_params=_cp("arbitrary"))(chip, q, recv3)


def sum_chips_small(q, recv3, chip):
    r, C = q.shape
    slot_of_xor = {2: 0, 1: 1, 3: 2}

    def body(me_ref, q_ref, r_ref, o_ref):
        me = me_ref[0]
        acc = None
        for k in range(4):
            kx = k ^ me
            term = q_ref[...]
            for xv, j in slot_of_xor.items():
                term = jnp.where(kx == xv, r_ref[j], term)
            acc = term if acc is None else acc + term
        o_ref[...] = acc

    return pl.pallas_call(
        body, name="sum_chips_small",
        grid_spec=pltpu.PrefetchScalarGridSpec(
            num_scalar_prefetch=1, grid=(1,),
            in_specs=[pl.BlockSpec((r, C), lambda i, me: (0, 0)), pl.BlockSpec((3, r, C), lambda i, me: (0, 0, 0))],
            out_specs=pl.BlockSpec((r, C), lambda i, me: (0, 0))),
        out_shape=jax.ShapeDtypeStruct((r, C), F32), compiler_params=_cp("arbitrary"))(chip, q, recv3)


def _row_tile(R, C):
    for cand in (1024, 512, 256, 128, 64, 32, 16):
        if R % cand == 0 and cand * C * 4 <= 2 * 1024 * 1024:
            return cand
    return R


def sum_slots(a, *, name):
    n = a.shape[0]
    shp = a.shape[1:]
    a3 = a.reshape(n, -1, shp[-1])
    R, C = a3.shape[1:]
    tr = _row_tile(R, C * n // 2)

    def body(a_ref, o_ref):
        acc = a_ref[0].astype(F32)
        for k in range(1, n):
            acc = acc + a_ref[k].astype(F32)
        o_ref[...] = acc

    out = pl.pallas_call(
        body, name=name, grid=(R // tr,), in_specs=[pl.BlockSpec((n, tr, C), lambda i: (0, i, 0))],
        out_specs=pl.BlockSpec((tr, C), lambda i: (i, 0)), out_shape=jax.ShapeDtypeStruct((R, C), F32),
        compiler_params=_cp("arbitrary"))(a3)
    return out.reshape(shp)


def _adamw_math(w, g, m, v):
    mn = ADAM_B1 * m + (1.0 - ADAM_B1) * g
    vn = ADAM_B2 * v + (1.0 - ADAM_B2) * (g * g)
    m_hat = mn / (1.0 - ADAM_B1 ** ADAM_STEP)
    v_hat = vn / (1.0 - ADAM_B2 ** ADAM_STEP)
    return -ADAM_LR * (m_hat / (jnp.sqrt(v_hat) + ADAM_EPS) + ADAM_WD * w), mn, vn


def adamw(w, g, m, v, *, name):
    shp = w.shape
    args = [t.reshape(-1, shp[-1]) for t in (w, g, m, v)]
    R, C = args[0].shape
    tr = _row_tile(R, C)

    def body(w_ref, g_ref, m_ref, v_ref, d_ref, mo_ref, vo_ref):
        d_ref[...], mo_ref[...], vo_ref[...] = _adamw_math(w_ref[...], g_ref[...], m_ref[...], v_ref[...])

    spec = pl.BlockSpec((tr, C), lambda i: (i, 0))
    res = pl.pallas_call(
        body, name=name, grid=(R // tr,), in_specs=[spec] * 4, out_specs=[spec] * 3,
        out_shape=[jax.ShapeDtypeStruct((R, C), F32)] * 3, compiler_params=_cp("arbitrary"))(*args)
    return [r.reshape(shp) for r in res]


def adamw_layers(w, g_mine, g_theirs, m, v, *, name):
    shp = w.shape
    three = (DEPTH, -1, shp[-1])
    w3, m3, v3 = [t.reshape(three) for t in (w, m, v)]
    ga, gb = [t.reshape(-1, shp[-1]) for t in (g_mine, g_theirs)]
    R, C = ga.shape
    tr = _row_tile(R, C)

    def body(w_ref, ga_ref, gb_ref, m_ref, v_ref, g_ref, d_ref, mo_ref, vo_ref):
        mine = pl.program_id(0) == lax.axis_index("c")
        g = jnp.where(mine, ga_ref[...], gb_ref[...])
        g_ref[...] = g
        d_ref[...], mo_ref[...], vo_ref[...] = _adamw_math(w_ref[...], g, m_ref[...], v_ref[...])

    lay = pl.BlockSpec((None, tr, C), lambda l, i: (l, i, 0))
    one = pl.BlockSpec((tr, C), lambda l, i: (i, 0))
    res = pl.pallas_call(
        body, name=name, grid=(DEPTH, R // tr), in_specs=[lay, one, one, lay, lay], out_specs=[lay] * 4,
        out_shape=[jax.ShapeDtypeStruct(w3.shape, F32)] * 4, compiler_params=_cp("arbitrary", "arbitrary"))(
            w3, ga, gb, m3, v3)
    return [r.reshape(shp) for r in res]


def _remote(src, dst, send_sems, recv_sems, k, peer):
    return pltpu.make_async_remote_copy(src_ref=src, dst_ref=dst, send_sem=send_sems.at[k], recv_sem=recv_sems.at[k],
                                        device_id=peer, device_id_type=MESH)


def gather_layer(l, shards, small=None, pieces=PIECES):
    n_p = len(pieces)
    with_small = small is not None

    def descs(h_in, h_out, sems):
        srcs = dict(zip(pieces, h_in[:n_p]))
        outs = dict(zip(pieces, h_out[:n_p]))
        ici_s, ici_r, d2d_s, d2d_r, own_s, own_r = sems
        x, y, c = _coords()
        me = 2 * x + y
        sib = (x, y, 1 - c)
        ici, fwd, fwd_in, own = [], [], [], []
        for j, (fx, fy) in enumerate(FLIPS):
            peer = (x ^ fx, y ^ fy, c)
            other = 2 * (x ^ fx) + (y ^ fy)
            for p, n in enumerate(pieces):
                k = 3 * p + j
                mine = _half(n, _win(n, outs[n], me), c)
                landed = _half(n, _win(n, outs[n], other), c)
                sib_half = _half(n, _win(n, outs[n], other), 1 - c)
                ici.append((_remote(_half(n, srcs[n].at[l], c), mine, ici_s, ici_r, k, peer),
                            _remote(_half(n, srcs[n].at[l], c), landed, ici_s, ici_r, k, peer)))
                fwd.append(_remote(landed, landed, d2d_s, d2d_r, k, sib))
                fwd_in.append(_remote(sib_half, sib_half, d2d_s, d2d_r, k, sib))
            if with_small:
                k = 3 * n_p + j
                ici.append((_remote(h_in[n_p], h_out[n_p].at[me], ici_s, ici_r, k, peer),
                            _remote(h_in[n_p], h_out[n_p].at[other], ici_s, ici_r, k, peer)))
        for p, n in enumerate(pieces):
            own.append(_remote(srcs[n].at[l], _win(n, outs[n], me), own_s, own_r, p, sib))
        if with_small:
            own.append(_remote(h_in[n_p], h_out[n_p].at[me], own_s, own_r, n_p, sib))
        return ici, fwd, fwd_in, own

    def start(h_in, h_out, sems):
        ici, _, _, own = descs(h_in, h_out, sems)
        for send, _ in ici:
            send.start()
        for cp in own:
            cp.start()

    def finish(h_in, h_out, sems):
        ici, fwd, fwd_in, own = descs(h_in, h_out, sems)
        per_chip = n_p + (1 if with_small else 0)
        for j in range(3):
            for p in range(n_p):
                ici[j * per_chip + p][1].wait_recv()
                fwd[j * n_p + p].start()
            if with_small:
                ici[j * per_chip + n_p][1].wait_recv()
        for cp in fwd_in:
            cp.wait_recv()
        for send, _ in ici:
            send.wait_send()
        for cp in fwd:
            cp.wait_send()
        for cp in own:
            cp.wait()

    arrays = [shards[n] for n in pieces] + ([small] if with_small else [])
    out_shape = [jax.ShapeDtypeStruct(LAYER_SHAPE[n], BF) for n in pieces]
    if with_small:
        out_shape.append(jax.ShapeDtypeStruct((4,) + small.shape, small.dtype))
    sems = [pltpu.SemaphoreType.DMA((3 * n_p + 3,)), pltpu.SemaphoreType.DMA((3 * n_p + 3,)),
            pltpu.SemaphoreType.DMA((3 * n_p,)), pltpu.SemaphoreType.DMA((3 * n_p,)),
            pltpu.SemaphoreType.DMA((n_p + 1,)), pltpu.SemaphoreType.DMA((n_p + 1,))]
    return Hosted(arrays, out_shape, sems, start, finish)


def both_hosted(h1, h2):
    a1, o1, s1 = len(h1.arrays), len(h1.out_shape), len(h1.sems)

    def start(h_in, h_out, sems):
        h1.start(h_in[:a1], h_out[:o1], sems[:s1])
        h2.start(h_in[a1:], h_out[o1:], sems[s1:])

    def finish(h_in, h_out, sems):
        h1.finish(h_in[:a1], h_out[:o1], sems[:s1])
        h2.finish(h_in[a1:], h_out[o1:], sems[s1:])

    return Hosted(list(h1.arrays) + list(h2.arrays), list(h1.out_shape) + list(h2.out_shape),
                  list(h1.sems) + list(h2.sems), start, finish)


def run_hosted(host, name):
    n_in, n_out = len(host.arrays), len(host.out_shape)

    def body(*refs):
        h_in, h_out, sems = refs[:n_in], refs[n_in:n_in + n_out], refs[n_in + n_out:]
        host.start(h_in, h_out, sems)
        host.finish(h_in, h_out, sems)

    return pl.pallas_call(body, name=name, in_specs=[HBM_SPEC] * n_in, out_specs=[HBM_SPEC] * n_out,
                          out_shape=list(host.out_shape), scratch_shapes=list(host.sems))(*host.arrays)


def split_halves(parts, small):
    n_p = len(PIECES)
    rh = small.shape[0] // 2

    def body(*refs):
        srcs = dict(zip(PIECES, refs[:n_p]))
        s_ref = refs[n_p]
        outs = dict(zip(PIECES, refs[n_p + 1:2 * n_p + 1]))
        so_ref = refs[2 * n_p + 1]
        send_sems, recv_sems = refs[2 * n_p + 2:]
        x, y, c = _coords()
        sib = (x, y, 1 - c)
        cps = []
        for p, n in enumerate(PIECES):
            for k in range(4):
                cps.append(_remote(_half(n, _win(n, srcs[n], k), 1 - c), outs[n].at[k], send_sems, recv_sems,
                                   4 * p + k, sib))
        cps.append(_remote(s_ref.at[pl.ds(_mult((1 - c) * rh, 8), rh), :], so_ref, send_sems, recv_sems, 4 * n_p, sib))
        for cp in cps:
            cp.start()
        for cp in cps:
            cp.wait()

    res = pl.pallas_call(
        body, name="split_halves", in_specs=[HBM_SPEC] * (n_p + 1), out_specs=[HBM_SPEC] * (n_p + 1),
        out_shape=[jax.ShapeDtypeStruct((4,) + _half_shape(n), BF) for n in PIECES]
        + [jax.ShapeDtypeStruct((rh, small.shape[1]), small.dtype)],
        scratch_shapes=[pltpu.SemaphoreType.DMA((4 * n_p + 1,)), pltpu.SemaphoreType.DMA((4 * n_p + 1,))],
    )(*[parts[n] for n in PIECES], small)
    return dict(zip(PIECES, res[:n_p])), res[n_p]


SAME_SHAPE = (('w_in',), ('pool_w',), SQUARES, ('ffn_w_gate', 'ffn_w_up'), ('ffn_w_down',))


def add_halves(names, parts, theirs, core):
    name = names[0]
    half = _half_shape(name)
    zero = (0,) * len(half)
    if name == 'w_in':
        pspec = pl.BlockSpec(half, lambda k, cc: (cc[0], (k + 3) % 4))
    elif name == 'pool_w':
        pspec = pl.BlockSpec(half, lambda k, cc: (cc[0], k, 0))
    elif name in ('ffn_w_gate', 'ffn_w_up'):
        pspec = pl.BlockSpec((None,) + half, lambda k, cc: (k, cc[0], 0))
    else:
        pspec = pl.BlockSpec(half, lambda k, cc: (2 * k + cc[0], 0))
    slot = pl.BlockSpec((None,) + half, lambda k, cc: (k,) + zero)
    m = len(names)

    def body(cc_ref, *refs):
        del cc_ref
        for i in range(m):
            refs[2 * m + i][...] = (refs[i][...].astype(F32) + refs[m + i][...].astype(F32)).astype(BF)

    res = pl.pallas_call(
        body, name="add_cores_" + name,
        grid_spec=pltpu.PrefetchScalarGridSpec(num_scalar_prefetch=1, grid=(4,), in_specs=[pspec] * m + [slot] * m,
                                               out_specs=[slot] * m),
        out_shape=[jax.ShapeDtypeStruct((4,) + half, BF)] * m, compiler_params=_cp("arbitrary"))(
            core, *[parts[n] for n in names], *[theirs[n] for n in names])
    return dict(zip(names, res))


def add_halves_small(small, theirs, core):
    rh, C = theirs.shape

    def body(cc_ref, p_ref, t_ref, o_ref):
        del cc_ref
        o_ref[...] = p_ref[...] + t_ref[...]

    blk = pl.BlockSpec((rh, C), lambda i, cc: (0, 0))
    return pl.pallas_call(
        body, name="add_cores_small",
        grid_spec=pltpu.PrefetchScalarGridSpec(
            num_scalar_prefetch=1, grid=(1,),
            in_specs=[pl.BlockSpec((rh, C), lambda i, cc: (cc[0], 0)), blk], out_specs=blk),
        out_shape=jax.ShapeDtypeStruct((rh, C), F32), compiler_params=_cp("arbitrary"))(core, small, theirs)


def exchange_halves(q, qsmall):
    n_p = len(PIECES)

    def descs(h_in, h_out, sems):
        send_sems, recv_sems = sems
        x, y, c = _coords()
        cps = []
        for j, (fx, fy) in enumerate(FLIPS):
            peer = (x ^ fx, y ^ fy, c)
            other = 2 * (x ^ fx) + (y ^ fy)
            for p in range(n_p):
                cps.append(_remote(h_in[p].at[other], h_out[p].at[j], send_sems, recv_sems, 3 * p + j, peer))
            cps.append(_remote(h_in[n_p], h_out[n_p].at[j], send_sems, recv_sems, 3 * n_p + j, peer))
        return cps

    def start(h_in, h_out, sems):
        for cp in descs(h_in, h_out, sems):
            cp.start()

    def finish(h_in, h_out, sems):
        for cp in descs(h_in, h_out, sems):
            cp.wait()

    arrays = [q[n] for n in PIECES] + [qsmall]
    out_shape = [jax.ShapeDtypeStruct((3,) + _half_shape(n), BF) for n in PIECES]
    out_shape.append(jax.ShapeDtypeStruct((3,) + qsmall.shape, qsmall.dtype))
    sems = [pltpu.SemaphoreType.DMA((3 * n_p + 3,)), pltpu.SemaphoreType.DMA((3 * n_p + 3,))]
    return Hosted(arrays, out_shape, sems, start, finish)


def sum_halves(names, q, recv3, chip):
    half = _half_shape(names[0])
    zero = (0,) * len(half)
    m = len(names)

    def body(me_ref, *refs):
        del me_ref
        for i in range(m):
            acc = refs[i][...].astype(F32)
            for j in range(3):
                acc = acc + refs[m + i][j].astype(F32)
            refs[2 * m + i][...] = acc

    res = pl.pallas_call(
        body, name="sum_chips_" + names[0],
        grid_spec=pltpu.PrefetchScalarGridSpec(
            num_scalar_prefetch=1, grid=(1,),
            in_specs=[pl.BlockSpec((None,) + half, lambda i, me: (me[0],) + zero)] * m
            + [pl.BlockSpec((3,) + half, lambda i, me: (0,) + zero)] * m,
            out_specs=[pl.BlockSpec(half, lambda i, me: zero)] * m),
        out_shape=[jax.ShapeDtypeStruct(half, F32)] * m, compiler_params=_cp("arbitrary"))(
            chip, *[q[n] for n in names], *[recv3[n] for n in names])
    return dict(zip(names, res))


def adamw_halves(w, g0, m, v, *, name):
    shp = w.shape
    C = shp[-1]
    four = (DEPTH, 2, -1, C)
    w4, m4, v4 = [t.reshape(four) for t in (w, m, v)]
    gs = [t.reshape(-1, C) for pair in g0 for t in pair]
    Rh = gs[0].shape[0]
    tr = _row_tile(Rh, C)

    def body(w_ref, a0_ref, b0_ref, a1_ref, b1_ref, m_ref, v_ref, g_ref, d_ref, mo_ref, vo_ref):
        mine = pl.program_id(1) == lax.axis_index("c")
        g_l0 = jnp.where(mine, a0_ref[...], b0_ref[...])
        g_l1 = jnp.where(mine, a1_ref[...], b1_ref[...])
        g = jnp.where(pl.program_id(0) == 0, g_l0, g_l1)
        g_ref[...] = g
        d_ref[...], mo_ref[...], vo_ref[...] = _adamw_math(w_ref[...], g, m_ref[...], v_ref[...])

    lay = pl.BlockSpec((None, None, tr, C), lambda l, h, i: (l, h, i, 0))
    one = pl.BlockSpec((tr, C), lambda l, h, i: (i, 0))
    res = pl.pallas_call(
        body, name=name, grid=(DEPTH, 2, Rh // tr), in_specs=[lay, one, one, one, one, lay, lay],
        out_specs=[lay] * 4, out_shape=[jax.ShapeDtypeStruct(w4.shape, F32)] * 4,
        compiler_params=_cp("arbitrary", "arbitrary", "arbitrary"))(w4, *gs, m4, v4)
    return [r.reshape(shp) for r in res]


def _local_shape(name, full_shape):
    shp = list(full_shape)
    ax = BIG_SHARDED.get(name, SMALL_SHARDED.get(name))
    if ax is not None:
        shp[ax] //= 4
    return tuple(shp)


FULL_SHAPES = {
    'w_in': (DEPTH, D, IN_COLS), 'b_in': (DEPTH, IN_COLS), 'pool_w': (DEPTH, 4, POOL_GD, POOL_GD),
    'pool_scale': (DEPTH, D), 'lru_conv_w': (DEPTH, 4, D), 'lru_conv_b': (DEPTH, D),
    'lru_w_r': (DEPTH, LRU_HEADS, LRU_HD, LRU_HD), 'lru_b_r': (DEPTH, D),
    'lru_w_i': (DEPTH, LRU_HEADS, LRU_HD, LRU_HD), 'lru_b_i': (DEPTH, D), 'lru_lambda': (DEPTH, D),
    'lru_w_out': (DEPTH, D, D), 'sconv_w': (DEPTH, 3, D), 'sconv_w_out': (DEPTH, D, D), 'w_mix_out': (DEPTH, D, D),
    'xa_w_q': (DEPTH, D, D), 'xa_w_k': (DEPTH, D, D), 'xa_w_v': (DEPTH, D, D), 'xa_w_o': (DEPTH, D, D),
    'ffn_w_gate': (DEPTH, D, D_FF), 'ffn_w_up': (DEPTH, D, D_FF), 'ffn_w_down': (DEPTH, D_FF, D),
    'ln_g': (DEPTH, 3, D), 'ln_b': (DEPTH, 3, D)}


def _pack(arrs, names, width, dtype, row_mult):
    flat = jnp.concatenate([arrs[n].astype(dtype).reshape(-1) for n in names])
    pad = (-flat.shape[0]) % (width * row_mult)
    if pad:
        flat = jnp.concatenate([flat, jnp.zeros((pad,), dtype)])
    return flat.reshape(-1, width)


def _unpack(flat2d, names, shapes):
    flat = flat2d.reshape(-1)
    out, off = {}, 0
    for n in names:
        size = 1
        for s in shapes[n]:
            size *= s
        out[n] = flat[off:off + size].reshape(shapes[n])
        off += size
    return out


def _gathered_full(g4, names, sharded_axis):
    loc_shapes = {n: _local_shape(n, FULL_SHAPES[n]) for n in names}
    per = [_unpack(g4[k], names, loc_shapes) for k in range(4)]
    return {n: jnp.concatenate([per[k][n] for k in range(4)], axis=sharded_axis[n]) for n in names}


def _perm_cols(a, perm, axis):
    blocks = [lax.slice_in_dim(a, p * D, (p + 1) * D, axis=axis) for p in perm]
    return jnp.concatenate(blocks, axis=axis)


def _gu_joined(g4, u4):
    return jnp.concatenate([g4[0], g4[1], u4[0], u4[1], g4[2], g4[3], u4[2], u4[3]], axis=1)


def _gu_apart(a):
    w = a.shape[1] // 8
    cols = [a[:, i * w:(i + 1) * w] for i in range(8)]
    return jnp.stack([cols[0], cols[1], cols[4], cols[5]]), jnp.stack([cols[2], cols[3], cols[6], cols[7]])


Z_INV = tuple(Z_PERM.index(j) for j in range(8))
SMALL_SH_NAMES = list(SMALL_SHARDED)
SMALL_ROWS = 32


def kernel(x, mem, w_in, b_in, pool_w, pool_scale, lru_conv_w, lru_conv_b, lru_w_r, lru_b_r, lru_w_i, lru_b_i, lru_lambda, lru_w_out, sconv_w, sconv_w_out, w_mix_out, xa_w_q, xa_w_k, xa_w_v, xa_w_o, ffn_w_gate, ffn_w_up, ffn_w_down, ln_g, ln_b, loss_target, m_w_in, m_b_in, m_pool_w, m_pool_scale, m_lru_conv_w, m_lru_conv_b, m_lru_w_r, m_lru_b_r, m_lru_w_i, m_lru_b_i, m_lru_lambda, m_lru_w_out, m_sconv_w, m_sconv_w_out, m_w_mix_out, m_xa_w_q, m_xa_w_k, m_xa_w_v, m_xa_w_o, m_ffn_w_gate, m_ffn_w_up, m_ffn_w_down, m_ln_g, m_ln_b, v_w_in, v_b_in, v_pool_w, v_pool_scale, v_lru_conv_w, v_lru_conv_b, v_lru_w_r, v_lru_b_r, v_lru_w_i, v_lru_b_i, v_lru_lambda, v_lru_w_out, v_sconv_w, v_sconv_w_out, v_w_mix_out, v_xa_w_q, v_xa_w_k, v_xa_w_v, v_xa_w_o, v_ffn_w_gate, v_ffn_w_up, v_ffn_w_down, v_ln_g, v_ln_b):
    loc = dict(w_in=w_in, b_in=b_in, pool_w=pool_w, pool_scale=pool_scale, lru_conv_w=lru_conv_w,
               lru_conv_b=lru_conv_b, lru_w_r=lru_w_r, lru_b_r=lru_b_r, lru_w_i=lru_w_i, lru_b_i=lru_b_i,
               lru_lambda=lru_lambda, lru_w_out=lru_w_out, sconv_w=sconv_w, sconv_w_out=sconv_w_out,
               w_mix_out=w_mix_out, xa_w_q=xa_w_q, xa_w_k=xa_w_k, xa_w_v=xa_w_v, xa_w_o=xa_w_o,
               ffn_w_gate=ffn_w_gate, ffn_w_up=ffn_w_up, ffn_w_down=ffn_w_down, ln_g=ln_g, ln_b=ln_b)
    mom = dict(w_in=m_w_in, b_in=m_b_in, pool_w=m_pool_w, pool_scale=m_pool_scale, lru_conv_w=m_lru_conv_w,
               lru_conv_b=m_lru_conv_b, lru_w_r=m_lru_w_r, lru_b_r=m_lru_b_r, lru_w_i=m_lru_w_i, lru_b_i=m_lru_b_i,
               lru_lambda=m_lru_lambda, lru_w_out=m_lru_w_out, sconv_w=m_sconv_w, sconv_w_out=m_sconv_w_out,
               w_mix_out=m_w_mix_out, xa_w_q=m_xa_w_q, xa_w_k=m_xa_w_k, xa_w_v=m_xa_w_v, xa_w_o=m_xa_w_o,
               ffn_w_gate=m_ffn_w_gate, ffn_w_up=m_ffn_w_up, ffn_w_down=m_ffn_w_down, ln_g=m_ln_g, ln_b=m_ln_b)
    var = dict(w_in=v_w_in, b_in=v_b_in, pool_w=v_pool_w, pool_scale=v_pool_scale, lru_conv_w=v_lru_conv_w,
               lru_conv_b=v_lru_conv_b, lru_w_r=v_lru_w_r, lru_b_r=v_lru_b_r, lru_w_i=v_lru_w_i, lru_b_i=v_lru_b_i,
               lru_lambda=v_lru_lambda, lru_w_out=v_lru_w_out, sconv_w=v_sconv_w, sconv_w_out=v_sconv_w_out,
               w_mix_out=v_w_mix_out, xa_w_q=v_xa_w_q, xa_w_k=v_xa_w_k, xa_w_v=v_xa_w_v, xa_w_o=v_xa_w_o,
               ffn_w_gate=v_ffn_w_gate, ffn_w_up=v_ffn_w_up, ffn_w_down=v_ffn_w_down, ln_g=v_ln_g, ln_b=v_ln_b)

    chip = 2 * lax.axis_index("x") + lax.axis_index("y")
    core = lax.axis_index("c")
    chip_arr = jnp.reshape(chip, (1,)).astype(jnp.int32)
    core_arr = jnp.reshape(core, (1,)).astype(jnp.int32)
    n_p = len(PIECES)

    shards = {n: loc[n].astype(BF) for n in PIECES}
    small = _pack(loc, SMALL_SH_NAMES, 256, F32, 8)
    first = ('w_in',)
    rest = tuple(n for n in PIECES if n not in first)
    early1 = ('w_in', 'pool_w')
    late1 = tuple(n for n in PIECES if n not in early1)
    got = run_hosted(gather_layer(0, shards, small, pieces=first), "gather_first")
    vec = _gathered_full(got[len(first)], SMALL_SH_NAMES, SMALL_SHARDED)
    W = {n: [None] * DEPTH for n in ('w_in', 'pool_w', 'ffn_w_down', 'ffn_w_gu') + SQUARES}
    W['b_in'] = [jnp.roll(b_in[l:l + 1], -2 * D, axis=1) for l in range(DEPTH)]
    for n in ('lru_conv_w', 'sconv_w', 'ln_g', 'ln_b'):
        W[n] = [vec[n][l] for l in range(DEPTH)]
    for n in ('lru_w_r', 'lru_w_i'):
        W[n] = [loc[n][l].astype(BF) for l in range(DEPTH)]
    for n in ('pool_scale', 'lru_conv_b', 'lru_b_r', 'lru_b_i', 'lru_lambda'):
        W[n] = [loc[n][l:l + 1] for l in range(DEPTH)]

    def take(l, names, arrays):
        full = dict(zip(names, arrays))
        if 'ffn_w_gate' in full:
            W['ffn_w_gu'][l] = _gu_joined(full['ffn_w_gate'], full['ffn_w_up'])
        for n in names:
            if n in W:
                W[n][l] = full[n]

    take(0, first, got[:len(first)])

    def after_in_proj(results):
        take(0, rest, results[:len(rest)])
        take(1, early1, results[len(rest):])

    xs, memb = x[0], mem[0].astype(BF)
    host_a = both_hosted(gather_layer(0, shards, pieces=rest), gather_layer(1, shards, pieces=early1))
    xf, xb, sv0 = _layer_fwd(0, xs, None, None, memb, W, host=host_a, host2=gather_layer(1, shards, pieces=late1),
                             after_in_proj=after_in_proj)
    take(1, late1, sv0['hosted2'])
    xf, xb, sv1 = _layer_fwd(1, xf, xb, None, memb, W)
    saves = [sv0, sv1]
    kvs = [(sv['kb'], sv['vb']) for sv in saves]

    def packed(G):
        g = dict(G)
        g['ffn_w_gate'], g['ffn_w_up'] = _gu_apart(g['ffn_w_gu'])
        g['pool_w'] = g['pool_w'].astype(BF)
        g['b_in'] = jnp.roll(g['b_in'], 2 * D, axis=1)
        return {n: g[n] for n in PIECES}, _pack(g, SMALL_ALL, D, F32, SMALL_ROWS)

    def reduce_start(G):
        parts, smallp = packed(G)
        theirs, theirs_small = split_halves(parts, smallp)
        q = {}
        for group in SAME_SHAPE:
            q.update(add_halves(group, parts, theirs, core_arr))
        qs = add_halves_small(smallp, theirs_small, core_arr)
        return q, qs, exchange_halves(q, qs)

    def reduce_finish(q, qs, recv):
        recv3 = dict(zip(PIECES, recv[:n_p]))
        summed = {}
        for group in SAME_SHAPE:
            summed.update(sum_halves(group, q, recv3, chip_arr))
        sums = [summed[n] for n in PIECES]
        sums.append(sum_chips_small(qs, recv[n_p], chip_arr))
        other = swap_cores(sums)
        return sums, other

    lf = (W['ln_b'][1][2:3], loss_target[0])
    dx, G1, loss_blk = _layer_bwd(1, None, saves[1], memb, kvs[1][0], kvs[1][1], W, loss_from=lf)
    q1, qs1, host1 = reduce_start(G1)
    started0 = []

    def last_host_fn(G):
        started0.extend(reduce_start(G))
        return started0[2]

    grad_x, G0, _ = _layer_bwd(0, dx, saves[0], memb, kvs[0][0], kvs[0][1], W, host=host1, last_host_fn=last_host_fn)
    red = [reduce_finish(started0[0], started0[1], G0['hosted_last']), reduce_finish(q1, qs1, G0['hosted'])]
    loss = lax.psum(loss_blk[0, 0], ("x", "y", "c"))

    small_shapes = {n: FULL_SHAPES[n][1:] for n in SMALL_ALL}
    per_layer = []
    for l in range(DEPTH):
        mine, theirs = red[l][0][-1], red[l][1][-1]
        whole = jnp.where(core == 0, jnp.concatenate([mine, theirs]), jnp.concatenate([theirs, mine]))
        per_layer.append(_unpack(whole, SMALL_ALL, small_shapes))
    grads = {}
    for n in SMALL_ALL:
        gn = jnp.stack([per_layer[l][n] for l in range(DEPTH)])
        if n in SMALL_SHARDED:
            size = loc[n].shape[SMALL_SHARDED[n]]
            gn = lax.dynamic_slice_in_dim(gn, chip * size, size, axis=SMALL_SHARDED[n])
        grads[n] = gn

    out_d, out_m, out_v = {}, {}, {}
    for p, n in enumerate(PIECES):
        pairs = [(red[l][0][p], red[l][1][p]) for l in range(DEPTH)]
        grads[n], out_d[n], out_m[n], out_v[n] = adamw_halves(loc[n], pairs, mom[n], var[n], name="adamw_" + n)
    for n in SMALL_ALL:
        out_d[n], out_m[n], out_v[n] = adamw(loc[n], grads[n], mom[n], var[n], name="adamw_" + n)

    return (loss, grad_x[None], *[grads[n] for n in WEIGHTS], *[out_d[n] for n in WEIGHTS],
            *[out_m[n] for n in WEIGHTS], *[out_v[n] for n in WEIGHTS])
```

```python
import functools

import jax
import jax.numpy as jnp
from jax import lax
from jax.experimental import pallas as pl
from jax.experimental.pallas import tpu as pltpu

F32 = jnp.float32
BF = jnp.bfloat16
MESH = pl.DeviceIdType.MESH

D = 1024
DEPTH = 2
N_MEM = 256
POOL_WINDOWS = (2, 4, 8, 16)
POOL_GD = 256
LRU_HEADS = 8
LRU_HD = 128
LRU_C = 8.0
X_HEADS = 4
X_HD = 256
D_FF = 2816
IN_COLS = 8 * D
ALPHA = (2 * DEPTH) ** 0.25
LN_EPS = 1e-5
ADAM_LR = 0.001
ADAM_B1 = 0.9
ADAM_B2 = 0.999
ADAM_EPS = 1e-08
ADAM_WD = 0.01
ADAM_STEP = 10

Z_PERM = (2, 3, 4, 5, 6, 7, 0, 1)
ZB_SCONV, ZB_GATE, ZB_POOL, ZB_LRU = 0, 1, 6, 7
HALO = 16
VMEM_LIMIT = 56 * 1024 * 1024

WEIGHTS = ['w_in', 'b_in', 'pool_w', 'pool_scale', 'lru_conv_w', 'lru_conv_b', 'lru_w_r', 'lru_b_r', 'lru_w_i',
           'lru_b_i', 'lru_lambda', 'lru_w_out', 'sconv_w', 'sconv_w_out', 'w_mix_out', 'xa_w_q', 'xa_w_k', 'xa_w_v',
           'xa_w_o', 'ffn_w_gate', 'ffn_w_up', 'ffn_w_down', 'ln_g', 'ln_b']
BIG_SHARDED = {'w_in': 2, 'pool_w': 2, 'lru_w_out': 1, 'sconv_w_out': 1, 'w_mix_out': 1, 'xa_w_q': 1, 'xa_w_k': 1,
               'xa_w_v': 1, 'xa_w_o': 1, 'ffn_w_gate': 2, 'ffn_w_up': 2, 'ffn_w_down': 1}
SMALL_SHARDED = {'lru_conv_w': 2, 'sconv_w': 2, 'ln_g': 2, 'ln_b': 2}
REPLICATED = ['b_in', 'pool_scale', 'lru_conv_b', 'lru_w_r', 'lru_b_r', 'lru_w_i', 'lru_b_i', 'lru_lambda']
SMALL_ALL = ['b_in', 'pool_scale', 'lru_conv_w', 'lru_conv_b', 'lru_w_r', 'lru_b_r', 'lru_w_i', 'lru_b_i',
             'lru_lambda', 'sconv_w', 'ln_g', 'ln_b']


def _cp(*sem):
    return pltpu.CompilerParams(dimension_semantics=sem, vmem_limit_bytes=VMEM_LIMIT)


def _sigmoid(x):
    return 0.5 * jnp.tanh(0.5 * x) + 0.5


class Hosted:
    def __init__(self, arrays, out_shape, sems, start, finish):
        self.arrays, self.out_shape, self.sems, self.start, self.finish = arrays, out_shape, sems, start, finish


def _with_host(host, n_in, n_out, refs):
    if host is None:
        return refs[:n_in], (), refs[n_in:n_in + n_out], (), refs[n_in + n_out:], ()
    hi, ho, hs = len(host.arrays), len(host.out_shape), len(host.sems)
    ins, h_in = refs[:n_in], refs[n_in:n_in + hi]
    outs = refs[n_in + hi:n_in + hi + n_out]
    h_out = refs[n_in + hi + n_out:n_in + hi + n_out + ho]
    rest = refs[n_in + hi + n_out + ho:]
    return ins, h_in, outs, h_out, rest[:len(rest) - hs], rest[len(rest) - hs:]


HBM_SPEC = pl.BlockSpec(memory_space=pl.ANY)


def mm_nn(a, w, bias, *, out_dtype, tm, tn, name, host=None):
    T, K = a.shape
    N = w.shape[1]
    tm, tn = min(tm, T), min(tn, N)
    nj, ni = N // tn, T // tm
    n_in = 2 if bias is None else 3

    def body(*refs):
        ins, h_in, (o_ref,), h_out, _, h_sems = _with_host(host, n_in, 1, refs)
        a_ref, w_ref = ins[:2]
        j, i = pl.program_id(0), pl.program_id(1)
        if host is not None:
            @pl.when((j == 0) & (i == 0))
            def _():
                host.start(h_in, h_out, h_sems)
        acc = jnp.dot(a_ref[...].astype(BF), w_ref[...], preferred_element_type=F32)
        if bias is not None:
            acc = acc + ins[2][...]
        o_ref[...] = acc.astype(o_ref.dtype)
        if host is not None:
            @pl.when((j == nj - 1) & (i == ni - 1))
            def _():
                host.finish(h_in, h_out, h_sems)

    in_specs = [pl.BlockSpec((tm, K), lambda j, i: (i, 0)), pl.BlockSpec((K, tn), lambda j, i: (0, j))]
    args = [a, w]
    if bias is not None:
        in_specs.append(pl.BlockSpec((1, tn), lambda j, i: (0, j)))
        args.append(bias)
    out_specs = [pl.BlockSpec((tm, tn), lambda j, i: (i, j))]
    out_shape = [jax.ShapeDtypeStruct((T, N), out_dtype)]
    scratch = []
    if host is not None:
        in_specs += [HBM_SPEC] * len(host.arrays)
        args += list(host.arrays)
        out_specs += [HBM_SPEC] * len(host.out_shape)
        out_shape += list(host.out_shape)
        scratch = list(host.sems)
    res = pl.pallas_call(
        body, name=name, grid=(nj, ni), in_specs=in_specs, out_specs=out_specs, out_shape=out_shape,
        scratch_shapes=scratch, compiler_params=_cp("arbitrary", "arbitrary"))(*args)
    return res[0] if host is None else (res[0], res[1:])


FF_HALF = D_FF // 2


def ffn_in_swiglu(x, wgu, *, tm, name, host=None):
    T, K = x.shape
    tm = min(tm, T)
    ni = T // tm

    def body(*refs):
        (x_ref, w_ref), h_in, (gu_ref, h_ref), h_out, _, h_sems = _with_host(host, 2, 2, refs)
        j, i = pl.program_id(0), pl.program_id(1)
        if host is not None:
            @pl.when((j == 0) & (i == 0))
            def _():
                host.start(h_in, h_out, h_sems)
        acc = jnp.dot(x_ref[...].astype(BF), w_ref[...], preferred_element_type=F32)
        gu = acc.astype(BF)
        gu_ref[...] = gu
        g = gu[:, :FF_HALF].astype(F32)
        h_ref[...] = (g * _sigmoid(g) * gu[:, FF_HALF:].astype(F32)).astype(BF)
        if host is not None:
            @pl.when((j == 1) & (i == ni - 1))
            def _():
                host.finish(h_in, h_out, h_sems)

    in_specs = [pl.BlockSpec((tm, K), lambda j, i: (i, 0)), pl.BlockSpec((K, 2 * FF_HALF), lambda j, i: (0, j))]
    args = [x, wgu]
    out_specs = [pl.BlockSpec((tm, 2 * FF_HALF), lambda j, i: (i, j)), pl.BlockSpec((tm, FF_HALF), lambda j, i: (i, j))]
    out_shape = [jax.ShapeDtypeStruct((T, 2 * D_FF), BF), jax.ShapeDtypeStruct((T, D_FF), BF)]
    scratch = []
    if host is not None:
        in_specs += [HBM_SPEC] * len(host.arrays)
        args += list(host.arrays)
        out_specs += [HBM_SPEC] * len(host.out_shape)
        out_shape += list(host.out_shape)
        scratch = list(host.sems)
    res = pl.pallas_call(
        body, name=name, grid=(2, ni), in_specs=in_specs, out_specs=out_specs, out_shape=out_shape,
        scratch_shapes=scratch, compiler_params=_cp("arbitrary", "arbitrary"))(*args)
    return (res[0], res[1]) if host is None else (res[0], res[1], res[2:])


def ffn_down_dx_swiglu(dp, wd, gu, *, tm, name):
    T = dp.shape[0]
    tm = min(tm, T)

    def body(dp_ref, w_ref, gu_ref, o_ref):
        dh = lax.dot_general(dp_ref[...], w_ref[...], (((1,), (1,)), ((), ())), preferred_element_type=F32)
        g = gu_ref[:, :FF_HALF].astype(F32)
        u = gu_ref[:, FF_HALF:].astype(F32)
        sg = _sigmoid(g)
        t = dh * sg
        s = g * sg
        o_ref[:, :FF_HALF] = (t * u * (1.0 + g - s)).astype(BF)
        o_ref[:, FF_HALF:] = (t * g).astype(BF)

    return pl.pallas_call(
        body, name=name, grid=(2, T // tm),
        in_specs=[pl.BlockSpec((tm, D), lambda j, i: (i, 0)), pl.BlockSpec((FF_HALF, D), lambda j, i: (j, 0)),
                  pl.BlockSpec((tm, 2 * FF_HALF), lambda j, i: (i, j))],
        out_specs=pl.BlockSpec((tm, 2 * FF_HALF), lambda j, i: (i, j)),
        out_shape=jax.ShapeDtypeStruct((T, 2 * D_FF), BF), compiler_params=_cp("arbitrary", "arbitrary"))(dp, wd, gu)


def mm_nt(a, w, res, *, out_dtype, tm, tc, name, host=None):
    T, C = a.shape
    K = w.shape[0]
    tm, tc = min(tm, T), min(tc, C)
    nc = C // tc
    ni = T // tm
    n_in = 2 if res is None else 3

    def body(*refs):
        ins, h_in, (o_ref,), h_out, (acc_ref,), h_sems = _with_host(host, n_in, 1, refs)
        a_ref, w_ref = ins[:2]
        r_ref = ins[2] if res is not None else None
        c = pl.program_id(1)
        if host is not None:
            @pl.when((pl.program_id(0) == 0) & (c == 0))
            def _():
                host.start(h_in, h_out, h_sems)

            @pl.when((pl.program_id(0) == ni - 1) & (c == nc - 1))
            def _():
                host.finish(h_in, h_out, h_sems)
        part = lax.dot_general(a_ref[...].astype(BF), w_ref[...], (((1,), (1,)), ((), ())),
                               preferred_element_type=F32)

        @pl.when(c == 0)
        def _():
            acc_ref[...] = part

        @pl.when(c > 0)
        def _():
            acc_ref[...] += part

        @pl.when(c == nc - 1)
        def _():
            out = acc_ref[...]
            if res is not None:
                out = out + ALPHA * r_ref[...]
            o_ref[...] = out.astype(o_ref.dtype)

    in_specs = [pl.BlockSpec((tm, tc), lambda i, c: (i, c)), pl.BlockSpec((K, tc), lambda i, c: (0, c))]
    args = [a, w]
    if res is not None:
        in_specs.append(pl.BlockSpec((tm, K), lambda i, c: (i, 0)))
        args.append(res)
    out_specs = [pl.BlockSpec((tm, K), lambda i, c: (i, 0))]
    out_shape = [jax.ShapeDtypeStruct((T, K), out_dtype)]
    scratch = [pltpu.VMEM((tm, K), F32)]
    if host is not None:
        in_specs += [HBM_SPEC] * len(host.arrays)
        args += list(host.arrays)
        out_specs += [HBM_SPEC] * len(host.out_shape)
        out_shape += list(host.out_shape)
        scratch += list(host.sems)
    out = pl.pallas_call(
        body, name=name, grid=(ni, nc), in_specs=in_specs, out_specs=out_specs, out_shape=out_shape,
        scratch_shapes=scratch, compiler_params=_cp("arbitrary", "arbitrary"))(*args)
    return out[0] if host is None else (out[0], out[1:])


def mm_tn(a, b, *, out_dtype, tk, tn, tt, name, colsum=False):
    T, K = a.shape
    N = b.shape[1]
    tk, tn, tt = min(tk, K), min(tn, N), min(tt, T)
    nt = T // tt

    def body(*refs):
        if colsum:
            a_ref, b_ref, o_ref, cs_ref, acc_ref = refs
        else:
            a_ref, b_ref, o_ref, acc_ref = refs
        i, t = pl.program_id(1), pl.program_id(2)
        bb = b_ref[...]
        part = lax.dot_general(a_ref[...].astype(BF), bb.astype(BF), (((0,), (0,)), ((), ())),
                               preferred_element_type=F32)

        @pl.when(t == 0)
        def _():
            acc_ref[...] = part

        @pl.when(t > 0)
        def _():
            acc_ref[...] += part

        @pl.when(t == nt - 1)
        def _():
            o_ref[...] = acc_ref[...].astype(o_ref.dtype)

        if colsum:
            s = jnp.sum(bb.astype(F32), axis=0, keepdims=True)

            @pl.when((i == 0) & (t == 0))
            def _():
                cs_ref[...] = s

            @pl.when((i == 0) & (t > 0))
            def _():
                cs_ref[...] += s

    out_specs = [pl.BlockSpec((tk, tn), lambda j, i, t: (i, j))]
    out_shape = [jax.ShapeDtypeStruct((K, N), out_dtype)]
    if colsum:
        out_specs.append(pl.BlockSpec((1, tn), lambda j, i, t: (0, j)))
        out_shape.append(jax.ShapeDtypeStruct((1, N), F32))
    res = pl.pallas_call(
        body, name=name, grid=(N // tn, K // tk, nt),
        in_specs=[pl.BlockSpec((tt, tk), lambda j, i, t: (t, i)), pl.BlockSpec((tt, tn), lambda j, i, t: (t, j))],
        out_specs=out_specs, out_shape=out_shape, scratch_shapes=[pltpu.VMEM((tk, tn), F32)],
        compiler_params=_cp("arbitrary", "arbitrary", "arbitrary"))(a, b)
    return res if colsum else res[0]


def mm_res_ln(a, w, res, g, b, *, tm, name):
    T, K = a.shape
    tm = min(tm, T)

    def body(a_ref, w_ref, r_ref, g_ref, b_ref, y_ref, yb_ref, xh_ref, rs_ref):
        pre = ALPHA * r_ref[...] + jnp.dot(a_ref[...].astype(BF), w_ref[...], preferred_element_type=F32)
        mu = jnp.mean(pre, axis=-1, keepdims=True)
        cen = pre - mu
        var = jnp.mean(cen * cen, axis=-1, keepdims=True)
        rstd = lax.rsqrt(var + LN_EPS)
        xhat = cen * rstd
        y = xhat * g_ref[...] + b_ref[...]
        y_ref[...] = y
        yb_ref[...] = y.astype(BF)
        xh_ref[...] = xhat
        rs_ref[...] = rstd

    row = lambda i: (i, 0)
    fix = lambda i: (0, 0)
    return pl.pallas_call(
        body, name=name, grid=(T // tm,),
        in_specs=[pl.BlockSpec((tm, K), row), pl.BlockSpec((K, D), fix), pl.BlockSpec((tm, D), row),
                  pl.BlockSpec((1, D), fix), pl.BlockSpec((1, D), fix)],
        out_specs=[pl.BlockSpec((tm, D), row), pl.BlockSpec((tm, D), row), pl.BlockSpec((tm, D), row),
                   pl.BlockSpec((tm, 1), row)],
        out_shape=[jax.ShapeDtypeStruct((T, D), F32), jax.ShapeDtypeStruct((T, D), BF),
                   jax.ShapeDtypeStruct((T, D), F32), jax.ShapeDtypeStruct((T, 1), F32)],
        compiler_params=_cp("arbitrary"))(a, w, res, g, b)


def ln_bwd(dy, xhat, rstd, g, *, tm, name, loss_from=None, dw_of=None):
    T = xhat.shape[0]
    tm = min(tm, T)
    nt = T // tm
    with_loss = loss_from is not None
    with_dw = dw_of is not None

    def body(*refs):
        if with_dw:
            acc_ref, refs = refs[-1], refs[:-1]
            n_main_in = 5 if with_loss else 4
            a_ref = refs[n_main_in]
            dw_ref = refs[-1]
            refs = refs[:n_main_in] + refs[n_main_in + 1:-1]
        if with_loss:
            xh_ref, rs_ref, g_ref, b_ref, t_ref, dp_ref, dpb_ref, dg_ref, db_ref, ls_ref = refs
        else:
            dy_ref, xh_ref, rs_ref, g_ref, dp_ref, dpb_ref, dg_ref, db_ref = refs
        i = pl.program_id(0)
        xhat_ = xh_ref[...]
        gg = g_ref[...]
        if with_loss:
            err = xhat_ * gg + b_ref[...] - t_ref[...]
            dyv = err * (1.0 / D)
            lpart = 0.5 * jnp.sum(jnp.sum(err * err, axis=-1, keepdims=True) * (1.0 / D))
        else:
            dyv = dy_ref[...]
        dxh = dyv * gg
        m1 = jnp.mean(dxh, axis=-1, keepdims=True)
        m2 = jnp.mean(dxh * xhat_, axis=-1, keepdims=True)
        dpre = rs_ref[...] * (dxh - m1 - xhat_ * m2)
        dp_ref[...] = dpre
        dpb = dpre.astype(BF)
        dpb_ref[...] = dpb
        dgp = jnp.sum(dyv * xhat_, axis=0, keepdims=True)
        dbp = jnp.sum(dyv, axis=0, keepdims=True)
        if with_dw:
            dwp = lax.dot_general(a_ref[...], dpb, (((0,), (0,)), ((), ())), preferred_element_type=F32)

        @pl.when(i == 0)
        def _():
            dg_ref[...] = dgp
            db_ref[...] = dbp
            if with_loss:
                ls_ref[...] = jnp.full((8, 128), lpart, F32)
            if with_dw:
                acc_ref[...] = dwp

        @pl.when(i > 0)
        def _():
            dg_ref[...] += dgp
            db_ref[...] += dbp
            if with_loss:
                ls_ref[...] += jnp.full((8, 128), lpart, F32)
            if with_dw:
                acc_ref[...] += dwp

        if with_dw:
            @pl.when(i == nt - 1)
            def _():
                dw_ref[...] = acc_ref[...].astype(BF)

    row = lambda i: (i, 0)
    fix = lambda i: (0, 0)
    if with_loss:
        in_specs = [pl.BlockSpec((tm, D), row), pl.BlockSpec((tm, 1), row), pl.BlockSpec((1, D), fix),
                    pl.BlockSpec((1, D), fix), pl.BlockSpec((tm, D), row)]
        args = [xhat, rstd, g, loss_from[0], loss_from[1]]
    else:
        in_specs = [pl.BlockSpec((tm, D), row), pl.BlockSpec((tm, D), row), pl.BlockSpec((tm, 1), row),
                    pl.BlockSpec((1, D), fix)]
        args = [dy, xhat, rstd, g]
    out_specs = [pl.BlockSpec((tm, D), row), pl.BlockSpec((tm, D), row), pl.BlockSpec((1, D), fix),
                 pl.BlockSpec((1, D), fix)]
    out_shape = [jax.ShapeDtypeStruct((T, D), F32), jax.ShapeDtypeStruct((T, D), BF),
                 jax.ShapeDtypeStruct((1, D), F32), jax.ShapeDtypeStruct((1, D), F32)]
    if with_loss:
        out_specs.append(pl.BlockSpec((8, 128), fix))
        out_shape.append(jax.ShapeDtypeStruct((8, 128), F32))
    scratch = []
    if with_dw:
        K = dw_of.shape[1]
        in_specs.append(pl.BlockSpec((tm, K), row))
        args.append(dw_of)
        out_specs.append(pl.BlockSpec((K, D), fix))
        out_shape.append(jax.ShapeDtypeStruct((K, D), BF))
        scratch.append(pltpu.VMEM((K, D), F32))
    return pl.pallas_call(body, name=name, grid=(nt,), in_specs=in_specs, out_specs=out_specs,
                          out_shape=out_shape, scratch_shapes=scratch, compiler_params=_cp("arbitrary"))(*args)


def _accumulate(ref, value):
    i = pl.program_id(0)

    @pl.when(i == 0)
    def _():
        ref[...] = value

    @pl.when(i > 0)
    def _():
        ref[...] += value


def _prev_halo(tm, blk):
    return lambda i: (jnp.maximum(i * (tm // HALO) - 1, 0), blk)


def _next_halo(tm, T, blk):
    return lambda i: (jnp.minimum((i + 1) * (tm // HALO), T // HALO - 1), blk)


def _pool_p(ext, t, g):
    e = ext[:, g * POOL_GD:(g + 1) * POOL_GD]
    s = e
    for sh in (1, 2, 4, 8)[:g + 1]:
        s = s + pltpu.roll(s, sh, axis=0)
    inv = 1.0 / jnp.minimum(t + 1, POOL_WINDOWS[g]).astype(F32)
    return s[HALO:] * inv - e[HALO:]


def pool_fwd(z, pw, *, tm, name):
    T = z.shape[0]
    tm = min(tm, T)

    def body(zm_ref, zh_ref, pw_ref, o_ref):
        i = pl.program_id(0)
        keep = jnp.where(i == 0, 0.0, 1.0).astype(F32)
        ext = jnp.concatenate([zh_ref[...].astype(F32) * keep, zm_ref[...].astype(F32)], axis=0)
        t = i * tm + lax.broadcasted_iota(jnp.int32, (tm, 1), 0)
        outs = [jnp.dot(_pool_p(ext, t, g).astype(BF), pw_ref[g], preferred_element_type=F32) for g in range(4)]
        o_ref[...] = jnp.concatenate(outs, axis=1).astype(o_ref.dtype)

    return pl.pallas_call(
        body, name=name, grid=(T // tm,),
        in_specs=[pl.BlockSpec((tm, D), lambda i: (i, ZB_POOL)), pl.BlockSpec((HALO, D), _prev_halo(tm, ZB_POOL)),
                  pl.BlockSpec((4, POOL_GD, POOL_GD), lambda i: (0, 0, 0))],
        out_specs=pl.BlockSpec((tm, D), lambda i: (i, 0)),
        out_shape=jax.ShapeDtypeStruct((T, D), BF), compiler_params=_cp("arbitrary"))(z, z, pw)


def pool_bwd(dz, dyp, yp_pre, z, pw, ps, *, tm, name):
    T = z.shape[0]
    tm = min(tm, T)
    nt = T // tm

    def body(dz_in, dy_ref, dyn_ref, yp_ref, zm_ref, zh_ref, pw_ref, ps_ref, dz_ref, dpw_ref, dps_ref, cs_ref):
        del dz_in
        i = pl.program_id(0)
        keep_p = jnp.where(i == 0, 0.0, 1.0).astype(F32)
        keep_n = jnp.where(i == nt - 1, 0.0, 1.0).astype(F32)
        ext = jnp.concatenate([zh_ref[...].astype(F32) * keep_p, zm_ref[...].astype(F32)], axis=0)
        t = i * tm + lax.broadcasted_iota(jnp.int32, (tm, 1), 0)
        psv = ps_ref[...]
        dy = dy_ref[...].astype(F32)
        dyp_ext = jnp.concatenate([dy, dyn_ref[...].astype(F32) * keep_n], axis=0) * psv
        t_ext = i * tm + lax.broadcasted_iota(jnp.int32, (tm + HALO, 1), 0)
        dps = jnp.sum(dy * yp_ref[...].astype(F32), axis=0, keepdims=True)
        dzs, dpws = [], []
        for g in range(4):
            sl = slice(g * POOL_GD, (g + 1) * POOL_GD)
            dyg = dyp_ext[:, sl].astype(BF)
            dp = lax.dot_general(dyg, pw_ref[g], (((1,), (1,)), ((), ())), preferred_element_type=F32)
            q = dp * (1.0 / jnp.minimum(t_ext + 1, POOL_WINDOWS[g]).astype(F32))
            s = q
            for sh in (1, 2, 4, 8)[:g + 1]:
                s = s + pltpu.roll(s, tm + HALO - sh, axis=0)
            dzs.append(s[:tm] - dp[:tm])
            p = _pool_p(ext, t, g).astype(BF)
            dpws.append(lax.dot_general(p, dyg[:tm], (((0,), (0,)), ((), ())), preferred_element_type=F32))
        dz_ref[...] = jnp.concatenate(dzs, axis=1).astype(dz_ref.dtype)
        _accumulate(cs_ref, jnp.concatenate([jnp.sum(d, axis=0, keepdims=True) for d in dzs], axis=1))

        @pl.when(i == 0)
        def _():
            for g in range(4):
                dpw_ref[g] = dpws[g]
            dps_ref[...] = dps

        @pl.when(i > 0)
        def _():
            for g in range(4):
                dpw_ref[g] += dpws[g]
            dps_ref[...] += dps

    row = lambda i: (i, 0)
    return pl.pallas_call(
        body, name=name, grid=(nt,),
        in_specs=[pl.BlockSpec(memory_space=pl.ANY),
                  pl.BlockSpec((tm, D), row), pl.BlockSpec((HALO, D), _next_halo(tm, T, 0)),
                  pl.BlockSpec((tm, D), row),
                  pl.BlockSpec((tm, D), lambda i: (i, ZB_POOL)), pl.BlockSpec((HALO, D), _prev_halo(tm, ZB_POOL)),
                  pl.BlockSpec((4, POOL_GD, POOL_GD), lambda i: (0, 0, 0)), pl.BlockSpec((1, D), lambda i: (0, 0))],
        out_specs=[pl.BlockSpec((tm, D), lambda i: (i, ZB_POOL)),
                   pl.BlockSpec((4, POOL_GD, POOL_GD), lambda i: (0, 0, 0)), pl.BlockSpec((1, D), lambda i: (0, 0)),
                   pl.BlockSpec((1, D), lambda i: (0, 0))],
        out_shape=[jax.ShapeDtypeStruct(dz.shape, dz.dtype), jax.ShapeDtypeStruct((4, POOL_GD, POOL_GD), F32),
                   jax.ShapeDtypeStruct((1, D), F32), jax.ShapeDtypeStruct((1, D), F32)],
        input_output_aliases={0: 0}, compiler_params=_cp("arbitrary"))(dz, dyp, dyp, yp_pre, z, z, pw, ps)


def _fill_ext(ext_s, halo, main, keep):
    ext_s[0:HALO, :] = halo * keep
    ext_s[HALO:, :] = main


def _lru_gates(ext_s, tm, cw, cb, wr_ref, br, wi_ref, bi, lam):
    shifted = []
    v = cb
    for k in range(4):
        zs = ext_s[pl.ds(HALO - 3 + k, tm), :]
        shifted.append(zs)
        v = v + cw[k:k + 1, :] * zs
    vb = v.astype(BF)
    rp, ip = [], []
    for h in range(LRU_HEADS):
        sl = slice(h * LRU_HD, (h + 1) * LRU_HD)
        rp.append(jnp.dot(vb[:, sl], wr_ref[h], preferred_element_type=F32))
        ip.append(jnp.dot(vb[:, sl], wi_ref[h], preferred_element_type=F32))
    r = _sigmoid(jnp.concatenate(rp, axis=1) + br)
    ig = _sigmoid(jnp.concatenate(ip, axis=1) + bi)
    sp = jnp.maximum(-lam, 0.0) + jnp.log(1.0 + jnp.exp(-jnp.abs(lam)))
    a = jnp.exp(-LRU_C * r * sp)
    om = 1.0 - a * a
    rs = lax.rsqrt(om)
    return v, vb, r, ig, a, om, rs, sp, shifted


def lru_fwd(z, cw, cb, wr, br, wi, bi, lam, wlo, *, tm, name):
    T = z.shape[0]
    tm = min(tm, T)
    nch = tm // 8

    def body(zm_ref, zh_ref, cw_ref, cb_ref, wr_ref, br_ref, wi_ref, bi_ref, lam_ref, wlo_ref, h_ref, y_ref,
             a_s, b_s, carry, ext_s):
        i = pl.program_id(0)

        @pl.when(i == 0)
        def _():
            carry[...] = jnp.zeros_like(carry)

        keep = jnp.where(i == 0, 0.0, 1.0).astype(F32)
        _fill_ext(ext_s, zh_ref[...].astype(F32), zm_ref[...].astype(F32), keep)
        v, _, _, ig, a, om, rs, _, _ = _lru_gates(ext_s, tm, cw_ref[...], cb_ref[...], wr_ref, br_ref[...], wi_ref,
                                                  bi_ref[...], lam_ref[...])
        a_s[...] = a
        b_s[...] = jnp.where(om > 0.0, om * rs, 0.0) * (ig * v)
        row = lax.broadcasted_iota(jnp.int32, (8, D), 0)

        def step(ci, hprev):
            sl = pl.ds(pl.multiple_of(ci * 8, 8), 8)
            aa, bb = a_s[sl, :], b_s[sl, :]
            for s in (1, 2, 4):
                m = row >= s
                bb = bb + aa * jnp.where(m, pltpu.roll(bb, s, axis=0), 0.0)
                aa = aa * jnp.where(m, pltpu.roll(aa, s, axis=0), 1.0)
            h = bb + aa * hprev
            h_ref[sl, :] = h
            return jnp.broadcast_to(h[7:8, :], (8, D))

        carry[...] = lax.fori_loop(0, nch, step, carry[...])
        y_ref[...] = jnp.dot(h_ref[...].astype(BF), wlo_ref[...], preferred_element_type=F32).astype(BF)

    fix2 = lambda i: (0, 0)
    fix3 = lambda i: (0, 0, 0)
    return pl.pallas_call(
        body, name=name, grid=(T // tm,),
        in_specs=[pl.BlockSpec((tm, D), lambda i: (i, ZB_LRU)), pl.BlockSpec((HALO, D), _prev_halo(tm, ZB_LRU)),
                  pl.BlockSpec((4, D), fix2), pl.BlockSpec((1, D), fix2),
                  pl.BlockSpec((LRU_HEADS, LRU_HD, LRU_HD), fix3), pl.BlockSpec((1, D), fix2),
                  pl.BlockSpec((LRU_HEADS, LRU_HD, LRU_HD), fix3), pl.BlockSpec((1, D), fix2),
                  pl.BlockSpec((1, D), fix2), pl.BlockSpec((D, D), fix2)],
        out_specs=[pl.BlockSpec((tm, D), lambda i: (i, 0)), pl.BlockSpec((tm, D), lambda i: (i, 0))],
        out_shape=[jax.ShapeDtypeStruct((T, D), F32), jax.ShapeDtypeStruct((T, D), BF)],
        scratch_shapes=[pltpu.VMEM((tm, D), F32), pltpu.VMEM((tm, D), F32), pltpu.VMEM((8, D), F32),
                        pltpu.VMEM((tm + HALO, D), F32)],
        compiler_params=_cp("arbitrary"))(z, z, cw, cb, wr, br, wi, bi, lam, wlo)


def lru_bwd(dz, dyl, z, h, cw, cb, wr, br, wi, bi, lam, wlo, *, tm, name):
    T = z.shape[0]
    tm = min(tm, T)
    nt = T // tm
    nch = tm // 8

    def body(dz_in, dy_ref, zm_ref, zh_ref, h_ref, hh_ref, cw_ref, cb_ref, wr_ref, br_ref, wi_ref, bi_ref, lam_ref,
             wlo_ref, dz_ref, dcw_ref, dcb_ref, dwr_ref, dbr_ref, dwi_ref, dbi_ref, dlam_ref, dwlo_ref, cs_ref,
             c_s, g_s, dh_s, dh_carry, a_ext, dv_ext, ext_s, h_ext, wlo_acc):
        del dz_in
        i = pl.program_id(0)
        ti = nt - 1 - i

        @pl.when(i == 0)
        def _():
            dh_carry[...] = jnp.zeros_like(dh_carry)
            a_ext[tm:, :] = jnp.zeros((8, D), F32)
            dv_ext[tm:, :] = jnp.zeros((HALO, D), F32)

        keep = jnp.where(ti == 0, 0.0, 1.0).astype(F32)
        _fill_ext(ext_s, zh_ref[...].astype(F32), zm_ref[...].astype(F32), keep)
        cw_ = cw_ref[...]
        lam_ = lam_ref[...]
        v, vb, r, ig, a, om, rs, sp, shifted = _lru_gates(ext_s, tm, cw_, cb_ref[...], wr_ref, br_ref[...], wi_ref,
                                                          bi_ref[...], lam_)
        mult = jnp.where(om > 0.0, om * rs, 0.0)
        a_ext[0:tm, :] = a
        c_s[...] = a_ext[pl.ds(1, tm), :]
        g_s[...] = lax.dot_general(dy_ref[...], wlo_ref[...], (((1,), (1,)), ((), ())), preferred_element_type=F32)
        row = lax.broadcasted_iota(jnp.int32, (8, D), 0)

        def step(k, nxt):
            ci = nch - 1 - k
            sl = pl.ds(pl.multiple_of(ci * 8, 8), 8)
            cc, gg = c_s[sl, :], g_s[sl, :]
            for s in (1, 2, 4):
                m = row < 8 - s
                gg = gg + cc * jnp.where(m, pltpu.roll(gg, 8 - s, axis=0), 0.0)
                cc = cc * jnp.where(m, pltpu.roll(cc, 8 - s, axis=0), 1.0)
            dh = gg + cc * nxt
            dh_s[sl, :] = dh
            return jnp.broadcast_to(dh[0:1, :], (8, D))

        dh_carry[...] = lax.fori_loop(0, nch, step, dh_carry[...])
        a_ext[tm:, :] = a[0:8, :]
        dh = dh_s[...]
        h_ext[0:8, :] = hh_ref[...] * keep
        hv = h_ref[...]
        h_ext[8:, :] = hv
        hprev = h_ext[pl.ds(7, tm), :]
        dwlo = lax.dot_general(hv.astype(BF), dy_ref[...], (((0,), (0,)), ((), ())), preferred_element_type=F32)
        iv = ig * v
        da = dh * hprev
        dmult = dh * iv
        div = dh * mult
        dlog = da * a - dmult * (a * a) * rs
        dr = dlog * (-LRU_C * sp)
        dlam = jnp.sum(dlog * r, axis=0, keepdims=True) * (LRU_C * _sigmoid(-lam_))
        di = div * v
        dv = div * ig
        drp = dr * r * (1.0 - r)
        dip = di * ig * (1.0 - ig)
        drb, dib = drp.astype(BF), dip.astype(BF)
        dvh, dwr, dwi = [], [], []
        nt_dims = (((1,), (1,)), ((), ()))
        tn_dims = (((0,), (0,)), ((), ()))
        for hd in range(LRU_HEADS):
            sl = slice(hd * LRU_HD, (hd + 1) * LRU_HD)
            dvh.append(lax.dot_general(drb[:, sl], wr_ref[hd], nt_dims, preferred_element_type=F32)
                       + lax.dot_general(dib[:, sl], wi_ref[hd], nt_dims, preferred_element_type=F32))
            dwr.append(lax.dot_general(vb[:, sl], drb[:, sl], tn_dims, preferred_element_type=F32))
            dwi.append(lax.dot_general(vb[:, sl], dib[:, sl], tn_dims, preferred_element_type=F32))
        dv = dv + jnp.concatenate(dvh, axis=1)
        dv_ext[0:tm, :] = dv
        dzl = cw_[3:4, :] * dv
        for k in range(3):
            dzl = dzl + cw_[k:k + 1, :] * dv_ext[pl.ds(3 - k, tm), :]
        dz_ref[...] = dzl.astype(dz_ref.dtype)
        _accumulate(cs_ref, jnp.sum(dzl, axis=0, keepdims=True))
        dv_ext[tm:, :] = dv[:HALO]
        dcw = jnp.concatenate([jnp.sum(dv * shifted[k], axis=0, keepdims=True) for k in range(4)], axis=0)
        dcb = jnp.sum(dv, axis=0, keepdims=True)
        dbr = jnp.sum(drp, axis=0, keepdims=True)
        dbi = jnp.sum(dip, axis=0, keepdims=True)

        @pl.when(i == 0)
        def _():
            dcw_ref[...] = dcw
            dcb_ref[...] = dcb
            dbr_ref[...] = dbr
            dbi_ref[...] = dbi
            dlam_ref[...] = dlam
            wlo_acc[...] = dwlo
            for hd in range(LRU_HEADS):
                dwr_ref[hd] = dwr[hd]
                dwi_ref[hd] = dwi[hd]

        @pl.when(i > 0)
        def _():
            dcw_ref[...] += dcw
            dcb_ref[...] += dcb
            dbr_ref[...] += dbr
            dbi_ref[...] += dbi
            dlam_ref[...] += dlam
            wlo_acc[...] += dwlo
            for hd in range(LRU_HEADS):
                dwr_ref[hd] += dwr[hd]
                dwi_ref[hd] += dwi[hd]

        @pl.when(i == nt - 1)
        def _():
            dwlo_ref[...] = wlo_acc[...].astype(BF)

    fix2 = lambda i: (0, 0)
    fix3 = lambda i: (0, 0, 0)
    rev = lambda i: (nt - 1 - i, 0)
    vec = pl.BlockSpec((1, D), fix2)
    hw = pl.BlockSpec((LRU_HEADS, LRU_HD, LRU_HD), fix3)
    return pl.pallas_call(
        body, name=name, grid=(nt,),
        in_specs=[pl.BlockSpec(memory_space=pl.ANY),
                  pl.BlockSpec((tm, D), rev),
                  pl.BlockSpec((tm, D), lambda i: (nt - 1 - i, ZB_LRU)),
                  pl.BlockSpec((HALO, D), lambda i: (jnp.maximum((nt - 1 - i) * (tm // HALO) - 1, 0), ZB_LRU)),
                  pl.BlockSpec((tm, D), rev),
                  pl.BlockSpec((8, D), lambda i: (jnp.maximum((nt - 1 - i) * (tm // 8) - 1, 0), 0)),
                  pl.BlockSpec((4, D), fix2), vec, hw, vec, hw, vec, vec, pl.BlockSpec((D, D), fix2)],
        out_specs=[pl.BlockSpec((tm, D), lambda i: (nt - 1 - i, ZB_LRU)),
                   pl.BlockSpec((4, D), fix2), vec, hw, vec, hw, vec, vec, pl.BlockSpec((D, D), fix2), vec],
        out_shape=[jax.ShapeDtypeStruct(dz.shape, dz.dtype), jax.ShapeDtypeStruct((4, D), F32),
                   jax.ShapeDtypeStruct((1, D), F32), jax.ShapeDtypeStruct((LRU_HEADS, LRU_HD, LRU_HD), F32),
                   jax.ShapeDtypeStruct((1, D), F32), jax.ShapeDtypeStruct((LRU_HEADS, LRU_HD, LRU_HD), F32),
                   jax.ShapeDtypeStruct((1, D), F32), jax.ShapeDtypeStruct((1, D), F32),
                   jax.ShapeDtypeStruct((D, D), BF), jax.ShapeDtypeStruct((1, D), F32)],
        scratch_shapes=[pltpu.VMEM((tm, D), F32), pltpu.VMEM((tm, D), F32), pltpu.VMEM((tm, D), F32),
                        pltpu.VMEM((8, D), F32), pltpu.VMEM((tm + 8, D), F32), pltpu.VMEM((tm + HALO, D), F32),
                        pltpu.VMEM((tm + HALO, D), F32), pltpu.VMEM((tm + 8, D), F32), pltpu.VMEM((D, D), F32)],
        input_output_aliases={0: 0},
        compiler_params=_cp("arbitrary"))(dz, dyl, z, z, h, h, cw, cb, wr, br, wi, bi, lam, wlo)


def _sconv_cv(u_ext, sw):
    shifted = []
    cv = None
    for k in range(3):
        us = (u_ext if k == 2 else pltpu.roll(u_ext, 2 - k, axis=0))[HALO:]
        shifted.append(us)
        term = sw[k:k + 1, :] * us
        cv = term if cv is None else cv + term
    return cv, shifted


def sconv_fwd(z, sw, wso, *, tm, name):
    T = z.shape[0]
    tm = min(tm, T)

    def body(zm_ref, zh_ref, sw_ref, wso_ref, s_ref, y_ref):
        i = pl.program_id(0)
        keep = jnp.where(i == 0, 0.0, 1.0).astype(F32)
        zm = zm_ref[...].astype(F32)
        zh = zh_ref[...].astype(F32)
        u_ext = jnp.concatenate([zh[:, D:2 * D] * zh[:, 2 * D:] * keep, zm[:, D:2 * D] * zm[:, 2 * D:]], axis=0)
        cv, _ = _sconv_cv(u_ext, sw_ref[...])
        s = (zm[:, :D] * cv).astype(BF)
        s_ref[...] = s
        y_ref[...] = jnp.dot(s, wso_ref[...], preferred_element_type=F32).astype(BF)

    return pl.pallas_call(
        body, name=name, grid=(T // tm,),
        in_specs=[pl.BlockSpec((tm, 3 * D), lambda i: (i, ZB_SCONV)),
                  pl.BlockSpec((HALO, 3 * D), _prev_halo(tm, ZB_SCONV)),
                  pl.BlockSpec((3, D), lambda i: (0, 0)), pl.BlockSpec((D, D), lambda i: (0, 0))],
        out_specs=[pl.BlockSpec((tm, D), lambda i: (i, 0)), pl.BlockSpec((tm, D), lambda i: (i, 0))],
        out_shape=[jax.ShapeDtypeStruct((T, D), BF), jax.ShapeDtypeStruct((T, D), BF)],
        compiler_params=_cp("arbitrary"))(z, z, sw, wso)


def sconv_bwd(dz, dyc, z, sw, wso, *, tm, name):
    T = z.shape[0]
    tm = min(tm, T)
    nt = T // tm

    def body(dz_in, dy_ref, dyn_ref, zm_ref, zp_ref, zn_ref, sw_ref, wso_ref, dz_ref, dsw_ref, dwso_ref, cs_ref,
             acc_ref):
        del dz_in
        i = pl.program_id(0)
        keep_p = jnp.where(i == 0, 0.0, 1.0).astype(F32)
        keep_n = jnp.where(i == nt - 1, 0.0, 1.0).astype(F32)
        sw_ = sw_ref[...]
        zm = zm_ref[...].astype(F32)
        zp = zp_ref[...].astype(F32)
        zb, zc, zh = zm[:, :D], zm[:, D:2 * D], zm[:, 2 * D:]
        u_ext = jnp.concatenate([zp[:, D:2 * D] * zp[:, 2 * D:] * keep_p, zc * zh], axis=0)
        cv, shifted = _sconv_cv(u_ext, sw_)
        dy_ext = jnp.concatenate([dy_ref[...], dyn_ref[...]], axis=0)
        ds_ext = lax.dot_general(dy_ext, wso_ref[...], (((1,), (1,)), ((), ())), preferred_element_type=F32)
        zb_ext = jnp.concatenate([zb, zn_ref[...][:, :D].astype(F32) * keep_n], axis=0)
        dcv_ext = ds_ext * zb_ext
        du = sw_[2:3, :] * dcv_ext[:tm]
        for k in range(2):
            du = du + sw_[k:k + 1, :] * pltpu.roll(dcv_ext, tm + HALO - (2 - k), axis=0)[:tm]
        dzs = [ds_ext[:tm] * cv, du * zh, du * zc]
        dz_ref[...] = jnp.concatenate(dzs, axis=1).astype(dz_ref.dtype)
        _accumulate(cs_ref, jnp.concatenate([jnp.sum(d, axis=0, keepdims=True) for d in dzs], axis=1))
        dcv = dcv_ext[:tm]
        dsw = jnp.concatenate([jnp.sum(dcv * shifted[k], axis=0, keepdims=True) for k in range(3)], axis=0)
        dwso = lax.dot_general((zb * cv).astype(BF), dy_ref[...], (((0,), (0,)), ((), ())),
                               preferred_element_type=F32)

        @pl.when(i == 0)
        def _():
            dsw_ref[...] = dsw
            acc_ref[...] = dwso

        @pl.when(i > 0)
        def _():
            dsw_ref[...] += dsw
            acc_ref[...] += dwso

        @pl.when(i == nt - 1)
        def _():
            dwso_ref[...] = acc_ref[...].astype(BF)

    return pl.pallas_call(
        body, name=name, grid=(nt,),
        in_specs=[pl.BlockSpec(memory_space=pl.ANY),
                  pl.BlockSpec((tm, D), lambda i: (i, 0)), pl.BlockSpec((HALO, D), _next_halo(tm, T, 0)),
                  pl.BlockSpec((tm, 3 * D), lambda i: (i, ZB_SCONV)),
                  pl.BlockSpec((HALO, 3 * D), _prev_halo(tm, ZB_SCONV)),
                  pl.BlockSpec((HALO, 3 * D), _next_halo(tm, T, ZB_SCONV)),
                  pl.BlockSpec((3, D), lambda i: (0, 0)), pl.BlockSpec((D, D), lambda i: (0, 0))],
        out_specs=[pl.BlockSpec((tm, 3 * D), lambda i: (i, ZB_SCONV)), pl.BlockSpec((3, D), lambda i: (0, 0)),
                   pl.BlockSpec((D, D), lambda i: (0, 0)), pl.BlockSpec((1, 3 * D), lambda i: (0, 0))],
        out_shape=[jax.ShapeDtypeStruct(dz.shape, dz.dtype), jax.ShapeDtypeStruct((3, D), F32),
                   jax.ShapeDtypeStruct((D, D), BF), jax.ShapeDtypeStruct((1, 3 * D), F32)],
        scratch_shapes=[pltpu.VMEM((D, D), F32)],
        input_output_aliases={0: 0}, compiler_params=_cp("arbitrary"))(dz, dyc, dyc, z, z, z, sw, wso)


def merge_fwd(z, yp_pre, yl, yc, ps, *, tm, name):
    T = z.shape[0]
    tm = min(tm, T)

    def body(zg_ref, yp_ref, yl_ref, yc_ref, ps_ref, o_ref):
        gts = _sigmoid(zg_ref[...].astype(F32))
        m = (gts[:, :D] * (yp_ref[...].astype(F32) * ps_ref[...]) + gts[:, D:2 * D] * yl_ref[...].astype(F32)
             + gts[:, 2 * D:] * yc_ref[...].astype(F32))
        o_ref[...] = m.astype(o_ref.dtype)

    row = lambda i: (i, 0)
    return pl.pallas_call(
        body, name=name, grid=(T // tm,),
        in_specs=[pl.BlockSpec((tm, 3 * D), lambda i: (i, ZB_GATE)), pl.BlockSpec((tm, D), row),
                  pl.BlockSpec((tm, D), row), pl.BlockSpec((tm, D), row), pl.BlockSpec((1, D), lambda i: (0, 0))],
        out_specs=pl.BlockSpec((tm, D), row), out_shape=jax.ShapeDtypeStruct((T, D), BF),
        compiler_params=_cp("arbitrary"))(z, yp_pre, yl, yc, ps)


def merge_bwd(dm, z, yp_pre, yl, yc, ps, *, tm, name):
    T = z.shape[0]
    tm = min(tm, T)

    def body(dm_ref, zg_ref, yp_ref, yl_ref, yc_ref, ps_ref, dz_ref, dyp_ref, dyl_ref, dyc_ref, cs_ref):
        gts = _sigmoid(zg_ref[...].astype(F32))
        dmv = dm_ref[...].astype(F32)
        ys = (yp_ref[...].astype(F32) * ps_ref[...], yl_ref[...].astype(F32), yc_ref[...].astype(F32))
        outs = (dyp_ref, dyl_ref, dyc_ref)
        dgs = []
        for j in range(3):
            gj = gts[:, j * D:(j + 1) * D]
            outs[j][...] = (dmv * gj).astype(BF)
            dgs.append(dmv * ys[j] * gj * (1.0 - gj))
        dz_ref[...] = jnp.concatenate(dgs, axis=1).astype(dz_ref.dtype)
        _accumulate(cs_ref, jnp.concatenate([jnp.sum(d, axis=0, keepdims=True) for d in dgs], axis=1))

    row = lambda i: (i, 0)
    return pl.pallas_call(
        body, name=name, grid=(T // tm,),
        in_specs=[pl.BlockSpec((tm, D), row), pl.BlockSpec((tm, 3 * D), lambda i: (i, ZB_GATE)),
                  pl.BlockSpec((tm, D), row), pl.BlockSpec((tm, D), row), pl.BlockSpec((tm, D), row),
                  pl.BlockSpec((1, D), lambda i: (0, 0))],
        out_specs=[pl.BlockSpec((tm, 3 * D), lambda i: (i, ZB_GATE)), pl.BlockSpec((tm, D), row),
                   pl.BlockSpec((tm, D), row), pl.BlockSpec((tm, D), row), pl.BlockSpec((1, 3 * D), lambda i: (0, 0))],
        out_shape=[jax.ShapeDtypeStruct((T, IN_COLS), BF), jax.ShapeDtypeStruct((T, D), BF),
                   jax.ShapeDtypeStruct((T, D), BF), jax.ShapeDtypeStruct((T, D), BF),
                   jax.ShapeDtypeStruct((1, 3 * D), F32)],
        compiler_params=_cp("arbitrary"))(dm, z, yp_pre, yl, yc, ps)


def _attn_probs(qh, kh):
    s = lax.dot_general(qh, kh, (((1,), (1,)), ((), ())), preferred_element_type=F32) * (X_HD ** -0.5)
    e = jnp.exp(s - jnp.max(s, axis=-1, keepdims=True))
    return e / jnp.sum(e, axis=-1, keepdims=True)


def attn_fwd(xb, wq, kb, vb, *, tm, name):
    T = xb.shape[0]
    tm = min(tm, T)

    def body(x_ref, wq_ref, k_ref, v_ref, q_ref, o_ref):
        q = jnp.dot(x_ref[...], wq_ref[...], preferred_element_type=F32).astype(BF)
        q_ref[...] = q
        outs = []
        for h in range(X_HEADS):
            sl = slice(h * X_HD, (h + 1) * X_HD)
            p = _attn_probs(q[:, sl], k_ref[:, sl])
            outs.append(jnp.dot(p.astype(BF), v_ref[:, sl], preferred_element_type=F32))
        o_ref[...] = jnp.concatenate(outs, axis=1).astype(BF)

    row = lambda i: (i, 0)
    fix = lambda i: (0, 0)
    return pl.pallas_call(
        body, name=name, grid=(T // tm,),
        in_specs=[pl.BlockSpec((tm, D), row), pl.BlockSpec((D, D), fix), pl.BlockSpec((N_MEM, D), fix),
                  pl.BlockSpec((N_MEM, D), fix)],
        out_specs=[pl.BlockSpec((tm, D), row), pl.BlockSpec((tm, D), row)],
        out_shape=[jax.ShapeDtypeStruct((T, D), BF), jax.ShapeDtypeStruct((T, D), BF)],
        compiler_params=_cp("arbitrary"))(xb, wq, kb, vb)


def attn_bwd(dxa, wo, q, kb, vb, xb, *, tm, name):
    T = q.shape[0]
    tm = min(tm, T)
    nt = T // tm

    def body(d_ref, wo_ref, q_ref, k_ref, v_ref, x_ref, dq_ref, dk_ref, dv_ref, dwq_ref, acc_ref):
        i = pl.program_id(0)
        do = lax.dot_general(d_ref[...], wo_ref[...], (((1,), (1,)), ((), ())),
                             preferred_element_type=F32).astype(BF)
        q = q_ref[...]
        dqs, dks, dvs = [], [], []
        for h in range(X_HEADS):
            sl = slice(h * X_HD, (h + 1) * X_HD)
            kh, vh = k_ref[:, sl], v_ref[:, sl]
            p = _attn_probs(q[:, sl], kh)
            dp = lax.dot_general(do[:, sl], vh, (((1,), (1,)), ((), ())), preferred_element_type=F32)
            ds = (p * (dp - jnp.sum(dp * p, axis=-1, keepdims=True)) * (X_HD ** -0.5)).astype(BF)
            dqs.append(jnp.dot(ds, kh, preferred_element_type=F32))
            dks.append(lax.dot_general(ds, q[:, sl], (((0,), (0,)), ((), ())), preferred_element_type=F32))
            dvs.append(lax.dot_general(p.astype(BF), do[:, sl], (((0,), (0,)), ((), ())),
                                       preferred_element_type=F32))
        dqb = jnp.concatenate(dqs, axis=1).astype(BF)
        dq_ref[...] = dqb
        dk = jnp.concatenate(dks, axis=1)
        dv = jnp.concatenate(dvs, axis=1)
        dwq = lax.dot_general(x_ref[...], dqb, (((0,), (0,)), ((), ())), preferred_element_type=F32)

        @pl.when(i == 0)
        def _():
            dk_ref[...] = dk
            dv_ref[...] = dv
            acc_ref[...] = dwq

        @pl.when(i > 0)
        def _():
            dk_ref[...] += dk
            dv_ref[...] += dv
            acc_ref[...] += dwq

        @pl.when(i == nt - 1)
        def _():
            dwq_ref[...] = acc_ref[...].astype(BF)

    row = lambda i: (i, 0)
    fix = lambda i: (0, 0)
    return pl.pallas_call(
        body, name=name, grid=(nt,),
        in_specs=[pl.BlockSpec((tm, D), row), pl.BlockSpec((D, D), fix), pl.BlockSpec((tm, D), row),
                  pl.BlockSpec((N_MEM, D), fix), pl.BlockSpec((N_MEM, D), fix), pl.BlockSpec((tm, D), row)],
        out_specs=[pl.BlockSpec((tm, D), row), pl.BlockSpec((N_MEM, D), fix), pl.BlockSpec((N_MEM, D), fix),
                   pl.BlockSpec((D, D), fix)],
        out_shape=[jax.ShapeDtypeStruct((T, D), BF), jax.ShapeDtypeStruct((N_MEM, D), F32),
                   jax.ShapeDtypeStruct((N_MEM, D), F32), jax.ShapeDtypeStruct((D, D), BF)],
        scratch_shapes=[pltpu.VMEM((D, D), F32)],
        compiler_params=_cp("arbitrary"))(dxa, wo, q, kb, vb, xb)


def swiglu_fwd(gu, *, tm, name):
    T = gu.shape[0]
    tm = min(tm, T)

    def body(g_ref, u_ref, o_ref):
        g = g_ref[...].astype(F32)
        o_ref[...] = (g * _sigmoid(g) * u_ref[...].astype(F32)).astype(BF)

    return pl.pallas_call(
        body, name=name, grid=(T // tm,),
        in_specs=[pl.BlockSpec((tm, D_FF), lambda i: (i, 0)), pl.BlockSpec((tm, D_FF), lambda i: (i, 1))],
        out_specs=pl.BlockSpec((tm, D_FF), lambda i: (i, 0)), out_shape=jax.ShapeDtypeStruct((T, D_FF), BF),
        compiler_params=_cp("arbitrary"))(gu, gu)


def swiglu_bwd(dh, gu, *, tm, name):
    T = gu.shape[0]
    tm = min(tm, T)

    def body(dh_ref, g_ref, u_ref, o_ref):
        g = g_ref[...].astype(F32)
        u = u_ref[...].astype(F32)
        dhv = dh_ref[...].astype(F32)
        sg = _sigmoid(g)
        o_ref[:, :D_FF] = (dhv * u * sg * (1.0 + g * (1.0 - sg))).astype(BF)
        o_ref[:, D_FF:] = (dhv * g * sg).astype(BF)

    return pl.pallas_call(
        body, name=name, grid=(T // tm,),
        in_specs=[pl.BlockSpec((tm, D_FF), lambda i: (i, 0)), pl.BlockSpec((tm, D_FF), lambda i: (i, 0)),
                  pl.BlockSpec((tm, D_FF), lambda i: (i, 1))],
        out_specs=pl.BlockSpec((tm, 2 * D_FF), lambda i: (i, 0)),
        out_shape=jax.ShapeDtypeStruct((T, 2 * D_FF), BF), compiler_params=_cp("arbitrary"))(dh, gu, gu)


TM_MM = 1024
TM_EW = 512
TM_SEQ = 512
TT_DW = 2048


def _mem_kv(l, memb, W):
    kb = mm_nn(memb, W['xa_w_k'][l], None, out_dtype=BF, tm=N_MEM, tn=1024, name=f"l{l}_mem_k")
    vb = mm_nn(memb, W['xa_w_v'][l], None, out_dtype=BF, tm=N_MEM, tn=1024, name=f"l{l}_mem_v")
    return kb, vb


def _layer_fwd(l, x, xb, kb, vb, W, host=None, host2=None, after_in_proj=None):
    n = f"l{l}_"
    sv = {'x0': x if xb is None else xb}
    z = mm_nn(sv['x0'], W['w_in'][l], W['b_in'][l], out_dtype=BF, tm=2 * TM_MM, tn=1024, name=n + "in_proj",
              host=host)
    if host is not None:
        z, sv['hosted'] = z
        if after_in_proj is not None:
            after_in_proj(sv['hosted'])
    if kb is None:
        kb, vb = _mem_kv(l, vb, W)
    sv['kb'], sv['vb'] = kb, vb
    yp = pool_fwd(z, W['pool_w'][l], tm=TM_SEQ, name=n + "pool_fwd")
    h, yl = lru_fwd(z, W['lru_conv_w'][l], W['lru_conv_b'][l], W['lru_w_r'][l], W['lru_b_r'][l], W['lru_w_i'][l],
                    W['lru_b_i'][l], W['lru_lambda'][l], W['lru_w_out'][l], tm=TM_SEQ, name=n + "lru_fwd")
    s, yc = sconv_fwd(z, W['sconv_w'][l], W['sconv_w_out'][l], tm=TM_SEQ, name=n + "sconv_fwd")
    merged = merge_fwd(z, yp, yl, yc, W['pool_scale'][l], tm=TM_EW, name=n + "merge_fwd")
    x1, x1b, xh1, rs1 = mm_res_ln(merged, W['w_mix_out'][l], x, W['ln_g'][l][0:1], W['ln_b'][l][0:1], tm=TM_EW,
                                  name=n + "mix_out_ln")
    q, o = attn_fwd(x1b, W['xa_w_q'][l], kb, vb, tm=TM_EW, name=n + "attn_fwd")
    x2, x2b, xh2, rs2 = mm_res_ln(o, W['xa_w_o'][l], x1, W['ln_g'][l][1:2], W['ln_b'][l][1:2], tm=TM_EW,
                                  name=n + "attn_out_ln")
    res = ffn_in_swiglu(x2b, W['ffn_w_gu'][l], tm=TM_MM, name=n + "ffn_in", host=host2)
    gu, hdn = res[:2]
    if host2 is not None:
        sv['hosted2'] = res[2]
    x3, x3b, xh3, rs3 = mm_res_ln(hdn, W['ffn_w_down'][l], x2, W['ln_g'][l][2:3], W['ln_b'][l][2:3], tm=TM_EW,
                                  name=n + "ffn_out_ln")
    sv.update(z=z, yp=yp, h=h, yl=yl, s=s, yc=yc, merged=merged, x1b=x1b, xh1=xh1, rs1=rs1, q=q, o=o, x2b=x2b,
              xh2=xh2, rs2=rs2, gu=gu, hdn=hdn, xh3=xh3, rs3=rs3)
    return x3, x3b, sv


def _layer_bwd(l, dx3, sv, memb, kb, vb, W, loss_from=None, host=None, last_host_fn=None):
    n = f"l{l}_"
    G = {}
    res = ln_bwd(dx3, sv['xh3'], sv['rs3'], W['ln_g'][l][2:3], tm=TM_EW, name=n + "ln3_bwd", loss_from=loss_from,
                 dw_of=sv['hdn'])
    dp3, dp3b, dg3, db3 = res[:4]
    loss = res[4] if loss_from is not None else None
    G['ffn_w_down'] = res[-1]
    dgu = ffn_down_dx_swiglu(dp3b, W['ffn_w_down'][l], sv['gu'], tm=TM_EW, name=n + "ffn_down_dx")
    dx2 = mm_nt(dgu, W['ffn_w_gu'][l], dp3, out_dtype=F32, tm=TM_MM, tc=1408, name=n + "ffn_in_dx", host=host)
    if host is not None:
        dx2, G['hosted'] = dx2
    G['ffn_w_gu'] = mm_tn(sv['x2b'], dgu, out_dtype=BF, tk=1024, tn=1408, tt=TT_DW,name=n + "ffn_in_dw")

    dp2, dp2b, dg2, db2, G['xa_w_o'] = ln_bwd(dx2, sv['xh2'], sv['rs2'], W['ln_g'][l][1:2], tm=TM_EW,
                                              name=n + "ln2_bwd", dw_of=sv['o'])
    dq, dk, dv, G['xa_w_q'] = attn_bwd(dp2b, W['xa_w_o'][l], sv['q'], kb, vb, sv['x1b'], tm=TM_EW,
                                       name=n + "attn_bwd")
    dx1 = mm_nt(dq, W['xa_w_q'][l], dp2, out_dtype=F32, tm=TM_MM, tc=D, name=n + "attn_q_dx")
    G['xa_w_k'] = mm_tn(memb, dk, out_dtype=BF, tk=1024, tn=1024, tt=N_MEM, name=n + "attn_k_dw")
    G['xa_w_v'] = mm_tn(memb, dv, out_dtype=BF, tk=1024, tn=1024, tt=N_MEM, name=n + "attn_v_dw")

    dp1, dp1b, dg1, db1, G['w_mix_out'] = ln_bwd(dx1, sv['xh1'], sv['rs1'], W['ln_g'][l][0:1], tm=TM_EW,
                                                 name=n + "ln1_bwd", dw_of=sv['merged'])
    dmerged = mm_nt(dp1b, W['w_mix_out'][l], None, out_dtype=BF, tm=TM_MM, tc=D, name=n + "mix_out_dx")
    z = sv['z']
    dz, dyp, dyl, dyc, cs_gate = merge_bwd(dmerged, z, sv['yp'], sv['yl'], sv['yc'], W['pool_scale'][l], tm=TM_EW,
                                           name=n + "merge_bwd")
    dz, G['pool_w'], G['pool_scale'], cs_pool = pool_bwd(dz, dyp, sv['yp'], z, W['pool_w'][l], W['pool_scale'][l],
                                                         tm=TM_SEQ, name=n + "pool_bwd")
    (dz, G['lru_conv_w'], G['lru_conv_b'], G['lru_w_r'], G['lru_b_r'], G['lru_w_i'], G['lru_b_i'], G['lru_lambda'],
     G['lru_w_out'], cs_lru) = lru_bwd(dz, dyl, z, sv['h'], W['lru_conv_w'][l], W['lru_conv_b'][l], W['lru_w_r'][l],
                                       W['lru_b_r'][l], W['lru_w_i'][l], W['lru_b_i'][l], W['lru_lambda'][l],
                                       W['lru_w_out'][l], tm=TM_SEQ, name=n + "lru_bwd")
    dz, G['sconv_w'], G['sconv_w_out'], cs_conv = sconv_bwd(dz, dyc, z, W['sconv_w'][l], W['sconv_w_out'][l],
                                                            tm=TM_SEQ, name=n + "sconv_bwd")
    G['b_in'] = jnp.concatenate([cs_conv, cs_gate, cs_pool, cs_lru], axis=1)
    G['w_in'] = mm_tn(sv['x0'], dz, out_dtype=BF, tk=1024, tn=1024, tt=TT_DW, name=n + "in_proj_dw")
    G['ln_g'] = jnp.concatenate([dg1, dg2, dg3], axis=0)
    G['ln_b'] = jnp.concatenate([db1, db2, db3], axis=0)
    last_host = None if last_host_fn is None else last_host_fn(G)
    dx0 = mm_nt(dz, W['w_in'][l], dp1, out_dtype=F32, tm=TM_MM, tc=2048, name=n + "in_proj_dx", host=last_host)
    if last_host is not None:
        dx0, G['hosted_last'] = dx0
    return dx0, G, loss


def local_step(x, mem, target, W):
    memb = mem.astype(BF)
    saves, kvs = [], []
    xf, xb = x, None
    for l in range(DEPTH):
        kb = mm_nn(memb, W['xa_w_k'][l], None, out_dtype=BF, tm=N_MEM, tn=1024, name=f"l{l}_mem_k")
        vb = mm_nn(memb, W['xa_w_v'][l], None, out_dtype=BF, tm=N_MEM, tn=1024, name=f"l{l}_mem_v")
        xf, xb, sv = _layer_fwd(l, xf, xb, kb, vb, W)
        saves.append(sv)
        kvs.append((kb, vb))
    grads = [None] * DEPTH
    dx, loss = None, None
    for l in reversed(range(DEPTH)):
        lf = (W['ln_b'][l][2:3], target) if l == DEPTH - 1 else None
        dx, grads[l], ls = _layer_bwd(l, dx, saves[l], memb, kvs[l][0], kvs[l][1], W, loss_from=lf)
        if ls is not None:
            loss = ls
    return loss, dx, grads


def _coords():
    return lax.axis_index("x"), lax.axis_index("y"), lax.axis_index("c")


FLIPS = ((1, 0), (0, 1), (1, 1))
SQUARES = ('lru_w_out', 'sconv_w_out', 'w_mix_out', 'xa_w_q', 'xa_w_k', 'xa_w_v', 'xa_w_o')
LAYER_SHAPE = {'w_in': (D, IN_COLS), 'pool_w': (4, POOL_GD, POOL_GD), 'ffn_w_gate': (4, D, D_FF // 4),
               'ffn_w_up': (4, D, D_FF // 4), 'ffn_w_down': (D_FF, D), **{n: (D, D) for n in SQUARES}}
PIECES = ('w_in', 'pool_w') + SQUARES + ('ffn_w_gate', 'ffn_w_up', 'ffn_w_down')


def _mult(v, m):
    return v if isinstance(v, int) else pl.multiple_of(v, m)


def _win(name, ref, k):
    if name == 'w_in':
        return ref.at[:, pl.ds(_mult(((2 * k + 6) % 8) * D, D), 2 * D)]
    if name == 'pool_w':
        return ref.at[:, pl.ds(_mult(k * (POOL_GD // 4), POOL_GD // 4), POOL_GD // 4), :]
    if name in ('ffn_w_gate', 'ffn_w_up'):
        return ref.at[k]
    rows = LAYER_SHAPE[name][0] // 4
    return ref.at[pl.ds(_mult(k * rows, 16), rows), :]


def _half_shape(name):
    shard = _shard_shape(name)
    return (shard[0] // 2,) + shard[1:]


def _half(name, ref, h):
    rows = _shard_shape(name)[0] // 2
    if name == 'pool_w':
        return ref.at[pl.ds(h * rows, rows)]
    return ref.at[pl.ds(_mult(h * rows, 16), rows), :]


def gather_weights(shards, small):
    n_p = len(PIECES)

    def body(*refs):
        srcs = dict(zip(PIECES, refs[:n_p]))
        small_ref = refs[n_p]
        outs = [dict(zip(PIECES, refs[n_p + 1 + l * n_p:n_p + 1 + (l + 1) * n_p])) for l in range(DEPTH)]
        gs_ref = refs[n_p + 1 + DEPTH * n_p]
        ici_send, ici_recv, d2d_send, d2d_recv, own_send, own_recv = refs[n_p + 2 + DEPTH * n_p:]
        x, y, c = _coords()
        me = 2 * x + y
        sib = (x, y, 1 - c)

        def own_copies():
            cps = []
            li = 0
            for n in PIECES:
                for l in range(DEPTH):
                    cps.append(pltpu.make_async_remote_copy(
                        src_ref=srcs[n].at[l], dst_ref=_win(n, outs[l][n], me), send_sem=own_send.at[li],
                        recv_sem=own_recv.at[li], device_id=sib, device_id_type=MESH))
                    li += 1
            cps.append(pltpu.make_async_remote_copy(
                src_ref=small_ref, dst_ref=gs_ref.at[me], send_sem=own_send.at[li], recv_sem=own_recv.at[li],
                device_id=sib, device_id_type=MESH))
            return cps

        def run(lc):
            sends = []
            for j, (fx, fy) in enumerate(FLIPS):
                peer = (x ^ fx, y ^ fy, c)
                for p, n in enumerate(PIECES):
                    k = 3 * p + j
                    sends.append(pltpu.make_async_remote_copy(
                        src_ref=srcs[n].at[lc], dst_ref=_win(n, outs[lc][n], me), send_sem=ici_send.at[k],
                        recv_sem=ici_recv.at[k], device_id=peer, device_id_type=MESH))
                k = 3 * n_p + j
                sends.append(pltpu.make_async_remote_copy(
                    src_ref=small_ref, dst_ref=gs_ref.at[me], send_sem=ici_send.at[k], recv_sem=ici_recv.at[k],
                    device_id=peer, device_id_type=MESH))
            own = own_copies()
            for cp in sends + own:
                cp.start()
            for j, (fx, fy) in enumerate(FLIPS):
                other = 2 * (x ^ fx) + (y ^ fy)
                for p, n in enumerate(PIECES):
                    k = 3 * p + j
                    w = _win(n, outs[lc][n], other)
                    pltpu.make_async_remote_copy(src_ref=srcs[n].at[lc], dst_ref=w, send_sem=ici_send.at[k],
                                                 recv_sem=ici_recv.at[k], device_id=sib,
                                                 device_id_type=MESH).wait_recv()
                    fw = pltpu.make_async_remote_copy(src_ref=w, dst_ref=w, send_sem=d2d_send.at[k],
                                                      recv_sem=d2d_recv.at[k], device_id=sib, device_id_type=MESH)
                    fw.start()
                    sends.append(fw)
                k = 3 * n_p + j
                pltpu.make_async_remote_copy(src_ref=small_ref, dst_ref=gs_ref.at[other], send_sem=ici_send.at[k],
                                             recv_sem=ici_recv.at[k], device_id=sib, device_id_type=MESH).wait_recv()
            for j, (fx, fy) in enumerate(FLIPS):
                other = 2 * (x ^ fx) + (y ^ fy)
                for p, n in enumerate(PIECES):
                    k = 3 * p + j
                    w = _win(n, outs[1 - lc][n], other)
                    pltpu.make_async_remote_copy(src_ref=w, dst_ref=w, send_sem=d2d_send.at[k],
                                                 recv_sem=d2d_recv.at[k], device_id=sib,
                                                 device_id_type=MESH).wait_recv()
            for cp in sends:
                cp.wait_send()
            for cp in own:
                cp.wait()

        @pl.when(c == 0)
        def _():
            run(0)

        @pl.when(c == 1)
        def _():
            run(1)

    hbm = pl.BlockSpec(memory_space=pl.ANY)
    n_out = DEPTH * n_p + 1
    res = pl.pallas_call(
        body, name="gather_weights", in_specs=[hbm] * (n_p + 1), out_specs=[hbm] * n_out,
        out_shape=[jax.ShapeDtypeStruct(LAYER_SHAPE[n], BF) for _ in range(DEPTH) for n in PIECES]
        + [jax.ShapeDtypeStruct((4,) + small.shape, small.dtype)],
        scratch_shapes=[pltpu.SemaphoreType.DMA((3 * n_p + 3,)), pltpu.SemaphoreType.DMA((3 * n_p + 3,)),
                        pltpu.SemaphoreType.DMA((3 * n_p,)), pltpu.SemaphoreType.DMA((3 * n_p,)),
                        pltpu.SemaphoreType.DMA((DEPTH * n_p + 1,)), pltpu.SemaphoreType.DMA((DEPTH * n_p + 1,))],
    )(*[shards[n] for n in PIECES], small)
    full = {n: [res[l * n_p + p] for l in range(DEPTH)] for p, n in enumerate(PIECES)}
    return full, res[DEPTH * n_p]


def split_layers(p0, p1):
    n = len(p0)

    def body(*refs):
        a0, a1 = refs[:n], refs[n:2 * n]
        theirs = refs[2 * n:3 * n]
        send_sems, recv_sems = refs[3 * n:]
        x, y, c = _coords()
        sib = (x, y, 1 - c)

        def give(arrs):
            for i in range(n):
                pltpu.make_async_remote_copy(src_ref=arrs[i], dst_ref=theirs[i], send_sem=send_sems.at[i],
                                             recv_sem=recv_sems.at[i], device_id=sib, device_id_type=MESH).start()

        @pl.when(c == 0)
        def _():
            give(a1)

        @pl.when(c == 1)
        def _():
            give(a0)

        for i in range(n):
            pltpu.make_async_remote_copy(src_ref=a0[i], dst_ref=theirs[i], send_sem=send_sems.at[i],
                                         recv_sem=recv_sems.at[i], device_id=sib, device_id_type=MESH).wait()

    hbm = pl.BlockSpec(memory_space=pl.ANY)
    return pl.pallas_call(
        body, name="split_layers", in_specs=[hbm] * (2 * n), out_specs=[hbm] * n,
        out_shape=[jax.ShapeDtypeStruct(a.shape, a.dtype) for a in p0],
        scratch_shapes=[pltpu.SemaphoreType.DMA((n,)), pltpu.SemaphoreType.DMA((n,))],
    )(*p0, *p1)


def exchange_chips(q, qsmall):
    n_p = len(PIECES)

    def body(*refs):
        srcs = dict(zip(PIECES, refs[:n_p]))
        s_ref = refs[n_p]
        outs = dict(zip(PIECES, refs[n_p + 1:2 * n_p + 1]))
        so_ref = refs[2 * n_p + 1]
        send_sems, recv_sems = refs[2 * n_p + 2:]
        x, y, c = _coords()
        me = 2 * x + y
        sends = []
        for j, (fx, fy) in enumerate(FLIPS):
            peer = (x ^ fx, y ^ fy, c)
            other = 2 * (x ^ fx) + (y ^ fy)
            for p, n in enumerate(PIECES):
                k = 3 * p + j
                sends.append(pltpu.make_async_remote_copy(
                    src_ref=_win(n, srcs[n], other), dst_ref=outs[n].at[j], send_sem=send_sems.at[k],
                    recv_sem=recv_sems.at[k], device_id=peer, device_id_type=MESH))
            k = 3 * n_p + j
            sends.append(pltpu.make_async_remote_copy(
                src_ref=s_ref, dst_ref=so_ref.at[j], send_sem=send_sems.at[k], recv_sem=recv_sems.at[k],
                device_id=peer, device_id_type=MESH))
        for cp in sends:
            cp.start()
        for cp in sends:
            cp.wait()

    hbm = pl.BlockSpec(memory_space=pl.ANY)
    res = pl.pallas_call(
        body, name="exchange_chips", in_specs=[hbm] * (n_p + 1), out_specs=[hbm] * (n_p + 1),
        out_shape=[jax.ShapeDtypeStruct((3,) + _shard_shape(n), BF) for n in PIECES]
        + [jax.ShapeDtypeStruct((3,) + qsmall.shape, qsmall.dtype)],
        scratch_shapes=[pltpu.SemaphoreType.DMA((3 * n_p + 3,)), pltpu.SemaphoreType.DMA((3 * n_p + 3,))],
    )(*[q[n] for n in PIECES], qsmall)
    return dict(zip(PIECES, res[:n_p])), res[n_p]


def _shard_shape(n):
    shp = LAYER_SHAPE[n]
    if n == 'w_in':
        return (shp[0], shp[1] // 4)
    if n == 'pool_w':
        return (shp[0], shp[1] // 4, shp[2])
    if n in ('ffn_w_gate', 'ffn_w_up'):
        return shp[1:]
    return (shp[0] // 4, shp[1])


def swap_cores(s):
    n = len(s)

    def body(*refs):
        srcs, outs = refs[:n], refs[n:2 * n]
        send_sems, recv_sems = refs[2 * n:]
        x, y, c = _coords()
        cps = [pltpu.make_async_remote_copy(src_ref=srcs[i], dst_ref=outs[i], send_sem=send_sems.at[i],
                                            recv_sem=recv_sems.at[i], device_id=(x, y, 1 - c), device_id_type=MESH)
               for i in range(n)]
        for cp in cps:
            cp.start()
        for cp in cps:
            cp.wait()

    hbm = pl.BlockSpec(memory_space=pl.ANY)
    return pl.pallas_call(
        body, name="swap_cores", in_specs=[hbm] * n, out_specs=[hbm] * n,
        out_shape=[jax.ShapeDtypeStruct(a.shape, a.dtype) for a in s],
        scratch_shapes=[pltpu.SemaphoreType.DMA((n,)), pltpu.SemaphoreType.DMA((n,))],
    )(*s)


def add_cores(p0, p1, theirs, *, out_dtype, name):
    shp = p0.shape
    args = [t.reshape(-1, shp[-1]) for t in (p0, p1, theirs)]
    R, C = args[0].shape
    tr = _row_tile(R, C)

    def body(a0_ref, a1_ref, t_ref, o_ref):
        c = lax.axis_index("c")
        t = t_ref[...].astype(F32)

        @pl.when(c == 0)
        def _():
            o_ref[...] = (a0_ref[...].astype(F32) + t).astype(o_ref.dtype)

        @pl.when(c == 1)
        def _():
            o_ref[...] = (a1_ref[...].astype(F32) + t).astype(o_ref.dtype)

    spec = pl.BlockSpec((tr, C), lambda i: (i, 0))
    out = pl.pallas_call(body, name=name, grid=(R // tr,), in_specs=[spec] * 3, out_specs=spec,
                         out_shape=jax.ShapeDtypeStruct((R, C), out_dtype), compiler_params=_cp("arbitrary"))(*args)
    return out.reshape(shp)


def sum_chips(name, q, recv3, chip):
    shard = _shard_shape(name)
    zero = (0,) * len(shard)
    if name == 'w_in':
        tr = 128
        grid = (shard[0] // tr,)
        qspec = pl.BlockSpec((tr, shard[1]), lambda i, me: (i, (me[0] + 3) % 4))
        rspec = pl.BlockSpec((3, tr, shard[1]), lambda i, me: (0, i, 0))
        ospec = pl.BlockSpec((tr, shard[1]), lambda i, me: (i, 0))
    else:
        grid = (1,)
        rspec = pl.BlockSpec((3,) + shard, lambda i, me: (0,) + zero)
        ospec = pl.BlockSpec(shard, lambda i, me: zero)
        if name == 'pool_w':
            qspec = pl.BlockSpec(shard, lambda i, me: (0, me[0], 0))
        elif name in ('ffn_w_gate', 'ffn_w_up'):
            qspec = pl.BlockSpec((None,) + shard, lambda i, me: (me[0], 0, 0))
        else:
            qspec = pl.BlockSpec(shard, lambda i, me: (me[0], 0))

    def body(me_ref, q_ref, r_ref, o_ref):
        del me_ref
        acc = q_ref[...].astype(F32)
        for j in range(3):
            acc = acc + r_ref[j].astype(F32)
        o_ref[...] = acc

    return pl.pallas_call(
        body, name="sum_chips_" + name,
        grid_spec=pltpu.PrefetchScalarGridSpec(num_scalar_prefetch=1, grid=grid, in_specs=[qspec, rspec],
                                               out_specs=ospec),
        out_shape=jax.ShapeDtypeStruct(shard, F32), compiler_params=_cp("arbitrary"))(chip, q, recv3)


def sum_chips_small(q, recv3, chip):
    r, C = q.shape
    slot_of_xor = {2: 0, 1: 1, 3: 2}

    def body(me_ref, q_ref, r_ref, o_ref):
        me = me_ref[0]
        acc = None
        for k in range(4):
            kx = k ^ me
            term = q_ref[...]
            for xv, j in slot_of_xor.items():
                term = jnp.where(kx == xv, r_ref[j], term)
            acc = term if acc is None else acc + term
        o_ref[...] = acc

    return pl.pallas_call(
        body, name="sum_chips_small",
        grid_spec=pltpu.PrefetchScalarGridSpec(
            num_scalar_prefetch=1, grid=(1,),
            in_specs=[pl.BlockSpec((r, C), lambda i, me: (0, 0)), pl.BlockSpec((3, r, C), lambda i, me: (0, 0, 0))],
            out_specs=pl.BlockSpec((r, C), lambda i, me: (0, 0))),
        out_shape=jax.ShapeDtypeStruct((r, C), F32), compiler_params=_cp("arbitrary"))(chip, q, recv3)


def _row_tile(R, C):
    for cand in (1024, 512, 256, 128, 64, 32, 16):
        if R % cand == 0 and cand * C * 4 <= 2 * 1024 * 1024:
            return cand
    return R


def sum_slots(a, *, name):
    n = a.shape[0]
    shp = a.shape[1:]
    a3 = a.reshape(n, -1, shp[-1])
    R, C = a3.shape[1:]
    tr = _row_tile(R, C * n // 2)

    def body(a_ref, o_ref):
        acc = a_ref[0].astype(F32)
        for k in range(1, n):
            acc = acc + a_ref[k].astype(F32)
        o_ref[...] = acc

    out = pl.pallas_call(
        body, name=name, grid=(R // tr,), in_specs=[pl.BlockSpec((n, tr, C), lambda i: (0, i, 0))],
        out_specs=pl.BlockSpec((tr, C), lambda i: (i, 0)), out_shape=jax.ShapeDtypeStruct((R, C), F32),
        compiler_params=_cp("arbitrary"))(a3)
    return out.reshape(shp)


def _adamw_math(w, g, m, v):
    mn = ADAM_B1 * m + (1.0 - ADAM_B1) * g
    vn = ADAM_B2 * v + (1.0 - ADAM_B2) * (g * g)
    m_hat = mn / (1.0 - ADAM_B1 ** ADAM_STEP)
    v_hat = vn / (1.0 - ADAM_B2 ** ADAM_STEP)
    return -ADAM_LR * (m_hat / (jnp.sqrt(v_hat) + ADAM_EPS) + ADAM_WD * w), mn, vn


def adamw(w, g, m, v, *, name):
    shp = w.shape
    args = [t.reshape(-1, shp[-1]) for t in (w, g, m, v)]
    R, C = args[0].shape
    tr = _row_tile(R, C)

    def body(w_ref, g_ref, m_ref, v_ref, d_ref, mo_ref, vo_ref):
        d_ref[...], mo_ref[...], vo_ref[...] = _adamw_math(w_ref[...], g_ref[...], m_ref[...], v_ref[...])

    spec = pl.BlockSpec((tr, C), lambda i: (i, 0))
    res = pl.pallas_call(
        body, name=name, grid=(R // tr,), in_specs=[spec] * 4, out_specs=[spec] * 3,
        out_shape=[jax.ShapeDtypeStruct((R, C), F32)] * 3, compiler_params=_cp("arbitrary"))(*args)
    return [r.reshape(shp) for r in res]


def adamw_layers(w, g_mine, g_theirs, m, v, *, name):
    shp = w.shape
    three = (DEPTH, -1, shp[-1])
    w3, m3, v3 = [t.reshape(three) for t in (w, m, v)]
    ga, gb = [t.reshape(-1, shp[-1]) for t in (g_mine, g_theirs)]
    R, C = ga.shape
    tr = _row_tile(R, C)

    def body(w_ref, ga_ref, gb_ref, m_ref, v_ref, g_ref, d_ref, mo_ref, vo_ref):
        mine = pl.program_id(0) == lax.axis_index("c")
        g = jnp.where(mine, ga_ref[...], gb_ref[...])
        g_ref[...] = g
        d_ref[...], mo_ref[...], vo_ref[...] = _adamw_math(w_ref[...], g, m_ref[...], v_ref[...])

    lay = pl.BlockSpec((None, tr, C), lambda l, i: (l, i, 0))
    one = pl.BlockSpec((tr, C), lambda l, i: (i, 0))
    res = pl.pallas_call(
        body, name=name, grid=(DEPTH, R // tr), in_specs=[lay, one, one, lay, lay], out_specs=[lay] * 4,
        out_shape=[jax.ShapeDtypeStruct(w3.shape, F32)] * 4, compiler_params=_cp("arbitrary", "arbitrary"))(
            w3, ga, gb, m3, v3)
    return [r.reshape(shp) for r in res]


def _remote(src, dst, send_sems, recv_sems, k, peer):
    return pltpu.make_async_remote_copy(src_ref=src, dst_ref=dst, send_sem=send_sems.at[k], recv_sem=recv_sems.at[k],
                                        device_id=peer, device_id_type=MESH)


def gather_layer(l, shards, small=None, pieces=PIECES):
    n_p = len(pieces)
    with_small = small is not None

    def descs(h_in, h_out, sems):
        srcs = dict(zip(pieces, h_in[:n_p]))
        outs = dict(zip(pieces, h_out[:n_p]))
        ici_s, ici_r, d2d_s, d2d_r, own_s, own_r = sems
        x, y, c = _coords()
        me = 2 * x + y
        sib = (x, y, 1 - c)
        ici, fwd, fwd_in, own = [], [], [], []
        for j, (fx, fy) in enumerate(FLIPS):
            peer = (x ^ fx, y ^ fy, c)
            other = 2 * (x ^ fx) + (y ^ fy)
            for p, n in enumerate(pieces):
                k = 3 * p + j
                mine = _half(n, _win(n, outs[n], me), c)
                landed = _half(n, _win(n, outs[n], other), c)
                sib_half = _half(n, _win(n, outs[n], other), 1 - c)
                ici.append((_remote(_half(n, srcs[n].at[l], c), mine, ici_s, ici_r, k, peer),
                            _remote(_half(n, srcs[n].at[l], c), landed, ici_s, ici_r, k, peer)))
                fwd.append(_remote(landed, landed, d2d_s, d2d_r, k, sib))
                fwd_in.append(_remote(sib_half, sib_half, d2d_s, d2d_r, k, sib))
            if with_small:
                k = 3 * n_p + j
                ici.append((_remote(h_in[n_p], h_out[n_p].at[me], ici_s, ici_r, k, peer),
                            _remote(h_in[n_p], h_out[n_p].at[other], ici_s, ici_r, k, peer)))
        for p, n in enumerate(pieces):
            own.append(_remote(srcs[n].at[l], _win(n, outs[n], me), own_s, own_r, p, sib))
        if with_small:
            own.append(_remote(h_in[n_p], h_out[n_p].at[me], own_s, own_r, n_p, sib))
        return ici, fwd, fwd_in, own

    def start(h_in, h_out, sems):
        ici, _, _, own = descs(h_in, h_out, sems)
        for send, _ in ici:
            send.start()
        for cp in own:
            cp.start()

    def finish(h_in, h_out, sems):
        ici, fwd, fwd_in, own = descs(h_in, h_out, sems)
        per_chip = n_p + (1 if with_small else 0)
        for j in range(3):
            for p in range(n_p):
                ici[j * per_chip + p][1].wait_recv()
                fwd[j * n_p + p].start()
            if with_small:
                ici[j * per_chip + n_p][1].wait_recv()
        for cp in fwd_in:
            cp.wait_recv()
        for send, _ in ici:
            send.wait_send()
        for cp in fwd:
            cp.wait_send()
        for cp in own:
            cp.wait()

    arrays = [shards[n] for n in pieces] + ([small] if with_small else [])
    out_shape = [jax.ShapeDtypeStruct(LAYER_SHAPE[n], BF) for n in pieces]
    if with_small:
        out_shape.append(jax.ShapeDtypeStruct((4,) + small.shape, small.dtype))
    sems = [pltpu.SemaphoreType.DMA((3 * n_p + 3,)), pltpu.SemaphoreType.DMA((3 * n_p + 3,)),
            pltpu.SemaphoreType.DMA((3 * n_p,)), pltpu.SemaphoreType.DMA((3 * n_p,)),
            pltpu.SemaphoreType.DMA((n_p + 1,)), pltpu.SemaphoreType.DMA((n_p + 1,))]
    return Hosted(arrays, out_shape, sems, start, finish)


def both_hosted(h1, h2):
    a1, o1, s1 = len(h1.arrays), len(h1.out_shape), len(h1.sems)

    def start(h_in, h_out, sems):
        h1.start(h_in[:a1], h_out[:o1], sems[:s1])
        h2.start(h_in[a1:], h_out[o1:], sems[s1:])

    def finish(h_in, h_out, sems):
        h1.finish(h_in[:a1], h_out[:o1], sems[:s1])
        h2.finish(h_in[a1:], h_out[o1:], sems[s1:])

    return Hosted(list(h1.arrays) + list(h2.arrays), list(h1.out_shape) + list(h2.out_shape),
                  list(h1.sems) + list(h2.sems), start, finish)


def run_hosted(host, name):
    n_in, n_out = len(host.arrays), len(host.out_shape)

    def body(*refs):
        h_in, h_out, sems = refs[:n_in], refs[n_in:n_in + n_out], refs[n_in + n_out:]
        host.start(h_in, h_out, sems)
        host.finish(h_in, h_out, sems)

    return pl.pallas_call(body, name=name, in_specs=[HBM_SPEC] * n_in, out_specs=[HBM_SPEC] * n_out,
                          out_shape=list(host.out_shape), scratch_shapes=list(host.sems))(*host.arrays)


def split_halves(parts, small):
    n_p = len(PIECES)
    rh = small.shape[0] // 2

    def body(*refs):
        srcs = dict(zip(PIECES, refs[:n_p]))
        s_ref = refs[n_p]
        outs = dict(zip(PIECES, refs[n_p + 1:2 * n_p + 1]))
        so_ref = refs[2 * n_p + 1]
        send_sems, recv_sems = refs[2 * n_p + 2:]
        x, y, c = _coords()
        sib = (x, y, 1 - c)
        cps = []
        for p, n in enumerate(PIECES):
            for k in range(4):
                cps.append(_remote(_half(n, _win(n, srcs[n], k), 1 - c), outs[n].at[k], send_sems, recv_sems,
                                   4 * p + k, sib))
        cps.append(_remote(s_ref.at[pl.ds(_mult((1 - c) * rh, 8), rh), :], so_ref, send_sems, recv_sems, 4 * n_p, sib))
        for cp in cps:
            cp.start()
        for cp in cps:
            cp.wait()

    res = pl.pallas_call(
        body, name="split_halves", in_specs=[HBM_SPEC] * (n_p + 1), out_specs=[HBM_SPEC] * (n_p + 1),
        out_shape=[jax.ShapeDtypeStruct((4,) + _half_shape(n), BF) for n in PIECES]
        + [jax.ShapeDtypeStruct((rh, small.shape[1]), small.dtype)],
        scratch_shapes=[pltpu.SemaphoreType.DMA((4 * n_p + 1,)), pltpu.SemaphoreType.DMA((4 * n_p + 1,))],
    )(*[parts[n] for n in PIECES], small)
    return dict(zip(PIECES, res[:n_p])), res[n_p]


SAME_SHAPE = (('w_in',), ('pool_w',), SQUARES, ('ffn_w_gate', 'ffn_w_up'), ('ffn_w_down',))


def add_halves(names, parts, theirs, core):
    name = names[0]
    half = _half_shape(name)
    zero = (0,) * len(half)
    if name == 'w_in':
        pspec = pl.BlockSpec(half, lambda k, cc: (cc[0], (k + 3) % 4))
    elif name == 'pool_w':
        pspec = pl.BlockSpec(half, lambda k, cc: (cc[0], k, 0))
    elif name in ('ffn_w_gate', 'ffn_w_up'):
        pspec = pl.BlockSpec((None,) + half, lambda k, cc: (k, cc[0], 0))
    else:
        pspec = pl.BlockSpec(half, lambda k, cc: (2 * k + cc[0], 0))
    slot = pl.BlockSpec((None,) + half, lambda k, cc: (k,) + zero)
    m = len(names)

    def body(cc_ref, *refs):
        del cc_ref
        for i in range(m):
            refs[2 * m + i][...] = (refs[i][...].astype(F32) + refs[m + i][...].astype(F32)).astype(BF)

    res = pl.pallas_call(
        body, name="add_cores_" + name,
        grid_spec=pltpu.PrefetchScalarGridSpec(num_scalar_prefetch=1, grid=(4,), in_specs=[pspec] * m + [slot] * m,
                                               out_specs=[slot] * m),
        out_shape=[jax.ShapeDtypeStruct((4,) + half, BF)] * m, compiler_params=_cp("arbitrary"))(
            core, *[parts[n] for n in names], *[theirs[n] for n in names])
    return dict(zip(names, res))


def add_halves_small(small, theirs, core):
    rh, C = theirs.shape

    def body(cc_ref, p_ref, t_ref, o_ref):
        del cc_ref
        o_ref[...] = p_ref[...] + t_ref[...]

    blk = pl.BlockSpec((rh, C), lambda i, cc: (0, 0))
    return pl.pallas_call(
        body, name="add_cores_small",
        grid_spec=pltpu.PrefetchScalarGridSpec(
            num_scalar_prefetch=1, grid=(1,),
            in_specs=[pl.BlockSpec((rh, C), lambda i, cc: (cc[0], 0)), blk], out_specs=blk),
        out_shape=jax.ShapeDtypeStruct((rh, C), F32), compiler_params=_cp("arbitrary"))(core, small, theirs)


def exchange_halves(q, qsmall):
    n_p = len(PIECES)

    def descs(h_in, h_out, sems):
        send_sems, recv_sems = sems
        x, y, c = _coords()
        cps = []
        for j, (fx, fy) in enumerate(FLIPS):
            peer = (x ^ fx, y ^ fy, c)
            other = 2 * (x ^ fx) + (y ^ fy)
            for p in range(n_p):
                cps.append(_remote(h_in[p].at[other], h_out[p].at[j], send_sems, recv_sems, 3 * p + j, peer))
            cps.append(_remote(h_in[n_p], h_out[n_p].at[j], send_sems, recv_sems, 3 * n_p + j, peer))
        return cps

    def start(h_in, h_out, sems):
        for cp in descs(h_in, h_out, sems):
            cp.start()

    def finish(h_in, h_out, sems):
        for cp in descs(h_in, h_out, sems):
            cp.wait()

    arrays = [q[n] for n in PIECES] + [qsmall]
    out_shape = [jax.ShapeDtypeStruct((3,) + _half_shape(n), BF) for n in PIECES]
    out_shape.append(jax.ShapeDtypeStruct((3,) + qsmall.shape, qsmall.dtype))
    sems = [pltpu.SemaphoreType.DMA((3 * n_p + 3,)), pltpu.SemaphoreType.DMA((3 * n_p + 3,))]
    return Hosted(arrays, out_shape, sems, start, finish)


def sum_halves(names, q, recv3, chip):
    half = _half_shape(names[0])
    zero = (0,) * len(half)
    m = len(names)

    def body(me_ref, *refs):
        del me_ref
        for i in range(m):
            acc = refs[i][...].astype(F32)
            for j in range(3):
                acc = acc + refs[m + i][j].astype(F32)
            refs[2 * m + i][...] = acc

    res = pl.pallas_call(
        body, name="sum_chips_" + names[0],
        grid_spec=pltpu.PrefetchScalarGridSpec(
            num_scalar_prefetch=1, grid=(1,),
            in_specs=[pl.BlockSpec((None,) + half, lambda i, me: (me[0],) + zero)] * m
            + [pl.BlockSpec((3,) + half, lambda i, me: (0,) + zero)] * m,
            out_specs=[pl.BlockSpec(half, lambda i, me: zero)] * m),
        out_shape=[jax.ShapeDtypeStruct(half, F32)] * m, compiler_params=_cp("arbitrary"))(
            chip, *[q[n] for n in names], *[recv3[n] for n in names])
    return dict(zip(names, res))


def adamw_halves(w, g0, m, v, *, name):
    shp = w.shape
    C = shp[-1]
    four = (DEPTH, 2, -1, C)
    w4, m4, v4 = [t.reshape(four) for t in (w, m, v)]
    gs = [t.reshape(-1, C) for pair in g0 for t in pair]
    Rh = gs[0].shape[0]
    tr = _row_tile(Rh, C)

    def body(w_ref, a0_ref, b0_ref, a1_ref, b1_ref, m_ref, v_ref, g_ref, d_ref, mo_ref, vo_ref):
        mine = pl.program_id(1) == lax.axis_index("c")
        g_l0 = jnp.where(mine, a0_ref[...], b0_ref[...])
        g_l1 = jnp.where(mine, a1_ref[...], b1_ref[...])
        g = jnp.where(pl.program_id(0) == 0, g_l0, g_l1)
        g_ref[...] = g
        d_ref[...], mo_ref[...], vo_ref[...] = _adamw_math(w_ref[...], g, m_ref[...], v_ref[...])

    lay = pl.BlockSpec((None, None, tr, C), lambda l, h, i: (l, h, i, 0))
    one = pl.BlockSpec((tr, C), lambda l, h, i: (i, 0))
    res = pl.pallas_call(
        body, name=name, grid=(DEPTH, 2, Rh // tr), in_specs=[lay, one, one, one, one, lay, lay],
        out_specs=[lay] * 4, out_shape=[jax.ShapeDtypeStruct(w4.shape, F32)] * 4,
        compiler_params=_cp("arbitrary", "arbitrary", "arbitrary"))(w4, *gs, m4, v4)
    return [r.reshape(shp) for r in res]


def _local_shape(name, full_shape):
    shp = list(full_shape)
    ax = BIG_SHARDED.get(name, SMALL_SHARDED.get(name))
    if ax is not None:
        shp[ax] //= 4
    return tuple(shp)


FULL_SHAPES = {
    'w_in': (DEPTH, D, IN_COLS), 'b_in': (DEPTH, IN_COLS), 'pool_w': (DEPTH, 4, POOL_GD, POOL_GD),
    'pool_scale': (DEPTH, D), 'lru_conv_w': (DEPTH, 4, D), 'lru_conv_b': (DEPTH, D),
    'lru_w_r': (DEPTH, LRU_HEADS, LRU_HD, LRU_HD), 'lru_b_r': (DEPTH, D),
    'lru_w_i': (DEPTH, LRU_HEADS, LRU_HD, LRU_HD), 'lru_b_i': (DEPTH, D), 'lru_lambda': (DEPTH, D),
    'lru_w_out': (DEPTH, D, D), 'sconv_w': (DEPTH, 3, D), 'sconv_w_out': (DEPTH, D, D), 'w_mix_out': (DEPTH, D, D),
    'xa_w_q': (DEPTH, D, D), 'xa_w_k': (DEPTH, D, D), 'xa_w_v': (DEPTH, D, D), 'xa_w_o': (DEPTH, D, D),
    'ffn_w_gate': (DEPTH, D, D_FF), 'ffn_w_up': (DEPTH, D, D_FF), 'ffn_w_down': (DEPTH, D_FF, D),
    'ln_g': (DEPTH, 3, D), 'ln_b': (DEPTH, 3, D)}


def _pack(arrs, names, width, dtype, row_mult):
    flat = jnp.concatenate([arrs[n].astype(dtype).reshape(-1) for n in names])
    pad = (-flat.shape[0]) % (width * row_mult)
    if pad:
        flat = jnp.concatenate([flat, jnp.zeros((pad,), dtype)])
    return flat.reshape(-1, width)


def _unpack(flat2d, names, shapes):
    flat = flat2d.reshape(-1)
    out, off = {}, 0
    for n in names:
        size = 1
        for s in shapes[n]:
            size *= s
        out[n] = flat[off:off + size].reshape(shapes[n])
        off += size
    return out


def _gathered_full(g4, names, sharded_axis):
    loc_shapes = {n: _local_shape(n, FULL_SHAPES[n]) for n in names}
    per = [_unpack(g4[k], names, loc_shapes) for k in range(4)]
    return {n: jnp.concatenate([per[k][n] for k in range(4)], axis=sharded_axis[n]) for n in names}


def _perm_cols(a, perm, axis):
    blocks = [lax.slice_in_dim(a, p * D, (p + 1) * D, axis=axis) for p in perm]
    return jnp.concatenate(blocks, axis=axis)


def _gu_joined(g4, u4):
    return jnp.concatenate([g4[0], g4[1], u4[0], u4[1], g4[2], g4[3], u4[2], u4[3]], axis=1)


def _gu_apart(a):
    w = a.shape[1] // 8
    cols = [a[:, i * w:(i + 1) * w] for i in range(8)]
    return jnp.stack([cols[0], cols[1], cols[4], cols[5]]), jnp.stack([cols[2], cols[3], cols[6], cols[7]])


Z_INV = tuple(Z_PERM.index(j) for j in range(8))
SMALL_SH_NAMES = list(SMALL_SHARDED)
SMALL_ROWS = 32


def kernel(x, mem, w_in, b_in, pool_w, pool_scale, lru_conv_w, lru_conv_b, lru_w_r, lru_b_r, lru_w_i, lru_b_i, lru_lambda, lru_w_out, sconv_w, sconv_w_out, w_mix_out, xa_w_q, xa_w_k, xa_w_v, xa_w_o, ffn_w_gate, ffn_w_up, ffn_w_down, ln_g, ln_b, loss_target, m_w_in, m_b_in, m_pool_w, m_pool_scale, m_lru_conv_w, m_lru_conv_b, m_lru_w_r, m_lru_b_r, m_lru_w_i, m_lru_b_i, m_lru_lambda, m_lru_w_out, m_sconv_w, m_sconv_w_out, m_w_mix_out, m_xa_w_q, m_xa_w_k, m_xa_w_v, m_xa_w_o, m_ffn_w_gate, m_ffn_w_up, m_ffn_w_down, m_ln_g, m_ln_b, v_w_in, v_b_in, v_pool_w, v_pool_scale, v_lru_conv_w, v_lru_conv_b, v_lru_w_r, v_lru_b_r, v_lru_w_i, v_lru_b_i, v_lru_lambda, v_lru_w_out, v_sconv_w, v_sconv_w_out, v_w_mix_out, v_xa_w_q, v_xa_w_k, v_xa_w_v, v_xa_w_o, v_ffn_w_gate, v_ffn_w_up, v_ffn_w_down, v_ln_g, v_ln_b):
    loc = dict(w_in=w_in, b_in=b_in, pool_w=pool_w, pool_scale=pool_scale, lru_conv_w=lru_conv_w,
               lru_conv_b=lru_conv_b, lru_w_r=lru_w_r, lru_b_r=lru_b_r, lru_w_i=lru_w_i, lru_b_i=lru_b_i,
               lru_lambda=lru_lambda, lru_w_out=lru_w_out, sconv_w=sconv_w, sconv_w_out=sconv_w_out,
               w_mix_out=w_mix_out, xa_w_q=xa_w_q, xa_w_k=xa_w_k, xa_w_v=xa_w_v, xa_w_o=xa_w_o,
               ffn_w_gate=ffn_w_gate, ffn_w_up=ffn_w_up, ffn_w_down=ffn_w_down, ln_g=ln_g, ln_b=ln_b)
    mom = dict(w_in=m_w_in, b_in=m_b_in, pool_w=m_pool_w, pool_scale=m_pool_scale, lru_conv_w=m_lru_conv_w,
               lru_conv_b=m_lru_conv_b, lru_w_r=m_lru_w_r, lru_b_r=m_lru_b_r, lru_w_i=m_lru_w_i, lru_b_i=m_lru_b_i,
               lru_lambda=m_lru_lambda, lru_w_out=m_lru_w_out, sconv_w=m_sconv_w, sconv_w_out=m_sconv_w_out,
               w_mix_out=m_w_mix_out, xa_w_q=m_xa_w_q, xa_w_k=m_xa_w_k, xa_w_v=m_xa_w_v, xa_w_o=m_xa_w_o,
               ffn_w_gate=m_ffn_w_gate, ffn_w_up=m_ffn_w_up, ffn_w_down=m_ffn_w_down, ln_g=m_ln_g, ln_b=m_ln_b)
    var = dict(w_in=v_w_in, b_in=v_b_in, pool_w=v_pool_w, pool_scale=v_pool_scale, lru_conv_w=v_lru_conv_w,
               lru_conv_b=v_lru_conv_b, lru_w_r=v_lru_w_r, lru_b_r=v_lru_b_r, lru_w_i=v_lru_w_i, lru_b_i=v_lru_b_i,
               lru_lambda=v_lru_lambda, lru_w_out=v_lru_w_out, sconv_w=v_sconv_w, sconv_w_out=v_sconv_w_out,
               w_mix_out=v_w_mix_out, xa_w_q=v_xa_w_q, xa_w_k=v_xa_w_k, xa_w_v=v_xa_w_v, xa_w_o=v_xa_w_o,
               ffn_w_gate=v_ffn_w_gate, ffn_w_up=v_ffn_w_up, ffn_w_down=v_ffn_w_down, ln_g=v_ln_g, ln_b=v_ln_b)

    chip = 2 * lax.axis_index("x") + lax.axis_index("y")
    core = lax.axis_index("c")
    chip_arr = jnp.reshape(chip, (1,)).astype(jnp.int32)
    core_arr = jnp.reshape(core, (1,)).astype(jnp.int32)
    n_p = len(PIECES)

    shards = {n: loc[n].astype(BF) for n in PIECES}
    small = _pack(loc, SMALL_SH_NAMES, 256, F32, 8)
    first = ('w_in',)
    rest = tuple(n for n in PIECES if n not in first)
    early1 = ('w_in', 'pool_w')
    late1 = tuple(n for n in PIECES if n not in early1)
    got = run_hosted(gather_layer(0, shards, small, pieces=first), "gather_first")
    vec = _gathered_full(got[len(first)], SMALL_SH_NAMES, SMALL_SHARDED)
    W = {n: [None] * DEPTH for n in ('w_in', 'pool_w', 'ffn_w_down', 'ffn_w_gu') + SQUARES}
    W['b_in'] = [jnp.roll(b_in[l:l + 1], -2 * D, axis=1) for l in range(DEPTH)]
    for n in ('lru_conv_w', 'sconv_w', 'ln_g', 'ln_b'):
        W[n] = [vec[n][l] for l in range(DEPTH)]
    for n in ('lru_w_r', 'lru_w_i'):
        W[n] = [loc[n][l].astype(BF) for l in range(DEPTH)]
    for n in ('pool_scale', 'lru_conv_b', 'lru_b_r', 'lru_b_i', 'lru_lambda'):
        W[n] = [loc[n][l:l + 1] for l in range(DEPTH)]

    def take(l, names, arrays):
        full = dict(zip(names, arrays))
        if 'ffn_w_gate' in full:
            W['ffn_w_gu'][l] = _gu_joined(full['ffn_w_gate'], full['ffn_w_up'])
        for n in names:
            if n in W:
                W[n][l] = full[n]

    take(0, first, got[:len(first)])

    def after_in_proj(results):
        take(0, rest, results[:len(rest)])
        take(1, early1, results[len(rest):])

    xs, memb = x[0], mem[0].astype(BF)
    host_a = both_hosted(gather_layer(0, shards, pieces=rest), gather_layer(1, shards, pieces=early1))
    xf, xb, sv0 = _layer_fwd(0, xs, None, None, memb, W, host=host_a, host2=gather_layer(1, shards, pieces=late1),
                             after_in_proj=after_in_proj)
    take(1, late1, sv0['hosted2'])
    xf, xb, sv1 = _layer_fwd(1, xf, xb, None, memb, W)
    saves = [sv0, sv1]
    kvs = [(sv['kb'], sv['vb']) for sv in saves]

    def packed(G):
        g = dict(G)
        g['ffn_w_gate'], g['ffn_w_up'] = _gu_apart(g['ffn_w_gu'])
        g['pool_w'] = g['pool_w'].astype(BF)
        g['b_in'] = jnp.roll(g['b_in'], 2 * D, axis=1)
        return {n: g[n] for n in PIECES}, _pack(g, SMALL_ALL, D, F32, SMALL_ROWS)

    def reduce_start(G):
        parts, smallp = packed(G)
        theirs, theirs_small = split_halves(parts, smallp)
        q = {}
        for group in SAME_SHAPE:
            q.update(add_halves(group, parts, theirs, core_arr))
        qs = add_halves_small(smallp, theirs_small, core_arr)
        return q, qs, exchange_halves(q, qs)

    def reduce_finish(q, qs, recv):
        recv3 = dict(zip(PIECES, recv[:n_p]))
        summed = {}
        for group in SAME_SHAPE:
            summed.update(sum_halves(group, q, recv3, chip_arr))
        sums = [summed[n] for n in PIECES]
        sums.append(sum_chips_small(qs, recv[n_p], chip_arr))
        other = swap_cores(sums)
        return sums, other

    lf = (W['ln_b'][1][2:3], loss_target[0])
    dx, G1, loss_blk = _layer_bwd(1, None, saves[1], memb, kvs[1][0], kvs[1][1], W, loss_from=lf)
    q1, qs1, host1 = reduce_start(G1)
    started0 = []

    def last_host_fn(G):
        started0.extend(reduce_start(G))
        return started0[2]

    grad_x, G0, _ = _layer_bwd(0, dx, saves[0], memb, kvs[0][0], kvs[0][1], W, host=host1, last_host_fn=last_host_fn)
    red = [reduce_finish(started0[0], started0[1], G0['hosted_last']), reduce_finish(q1, qs1, G0['hosted'])]
    loss = lax.psum(loss_blk[0, 0], ("x", "y", "c"))

    small_shapes = {n: FULL_SHAPES[n][1:] for n in SMALL_ALL}
    per_layer = []
    for l in range(DEPTH):
        mine, theirs = red[l][0][-1], red[l][1][-1]
        whole = jnp.where(core == 0, jnp.concatenate([mine, theirs]), jnp.concatenate([theirs, mine]))
        per_layer.append(_unpack(whole, SMALL_ALL, small_shapes))
    grads = {}
    for n in SMALL_ALL:
        gn = jnp.stack([per_layer[l][n] for l in range(DEPTH)])
        if n in SMALL_SHARDED:
            size = loc[n].shape[SMALL_SHARDED[n]]
            gn = lax.dynamic_slice_in_dim(gn, chip * size, size, axis=SMALL_SHARDED[n])
        grads[n] = gn

    out_d, out_m, out_v = {}, {}, {}
    for p, n in enumerate(PIECES):
        pairs = [(red[l][0][p], red[l][1][p]) for l in range(DEPTH)]
        grads[n], out_d[n], out_m[n], out_v[n] = adamw_halves(loc[n], pairs, mom[n], var[n], name="adamw_" + n)
    for n in SMALL_ALL:
        out_d[n], out_m[n], out_v[n] = adamw(loc[n], grads[n], mom[n], var[n], name="adamw_" + n)

    return (loss, grad_x[None], *[grads[n] for n in WEIGHTS], *[out_d[n] for n in WEIGHTS],
            *[out_m[n] for n in WEIGHTS], *[out_v[n] for n in WEIGHTS])
```

```python
import functools

import jax
import jax.numpy as jnp
from jax import lax
from jax.experimental import pallas as pl
from jax.experimental.pallas import tpu as pltpu

F32 = jnp.float32
BF = jnp.bfloat16
MESH = pl.DeviceIdType.MESH

D = 1024
DEPTH = 2
N_MEM = 256
POOL_WINDOWS = (2, 4, 8, 16)
POOL_GD = 256
LRU_HEADS = 8
LRU_HD = 128
LRU_C = 8.0
X_HEADS = 4
X_HD = 256
D_FF = 2816
IN_COLS = 8 * D
ALPHA = (2 * DEPTH) ** 0.25
LN_EPS = 1e-5
ADAM_LR = 0.001
ADAM_B1 = 0.9
ADAM_B2 = 0.999
ADAM_EPS = 1e-08
ADAM_WD = 0.01
ADAM_STEP = 10

Z_PERM = (2, 3, 4, 5, 6, 7, 0, 1)
ZB_SCONV, ZB_GATE, ZB_POOL, ZB_LRU = 0, 1, 6, 7
HALO = 16
VMEM_LIMIT = 56 * 1024 * 1024

WEIGHTS = ['w_in', 'b_in', 'pool_w', 'pool_scale', 'lru_conv_w', 'lru_conv_b', 'lru_w_r', 'lru_b_r', 'lru_w_i',
           'lru_b_i', 'lru_lambda', 'lru_w_out', 'sconv_w', 'sconv_w_out', 'w_mix_out', 'xa_w_q', 'xa_w_k', 'xa_w_v',
           'xa_w_o', 'ffn_w_gate', 'ffn_w_up', 'ffn_w_down', 'ln_g', 'ln_b']
BIG_SHARDED = {'w_in': 2, 'pool_w': 2, 'lru_w_out': 1, 'sconv_w_out': 1, 'w_mix_out': 1, 'xa_w_q': 1, 'xa_w_k': 1,
               'xa_w_v': 1, 'xa_w_o': 1, 'ffn_w_gate': 2, 'ffn_w_up': 2, 'ffn_w_down': 1}
SMALL_SHARDED = {'lru_conv_w': 2, 'sconv_w': 2, 'ln_g': 2, 'ln_b': 2}
REPLICATED = ['b_in', 'pool_scale', 'lru_conv_b', 'lru_w_r', 'lru_b_r', 'lru_w_i', 'lru_b_i', 'lru_lambda']
SMALL_ALL = ['b_in', 'pool_scale', 'lru_conv_w', 'lru_conv_b', 'lru_w_r', 'lru_b_r', 'lru_w_i', 'lru_b_i',
             'lru_lambda', 'sconv_w', 'ln_g', 'ln_b']


def _cp(*sem):
    return pltpu.CompilerParams(dimension_semantics=sem, vmem_limit_bytes=VMEM_LIMIT)


def _sigmoid(x):
    return 0.5 * jnp.tanh(0.5 * x) + 0.5


class Hosted:
    def __init__(self, arrays, out_shape, sems, start, finish):
        self.arrays, self.out_shape, self.sems, self.start, self.finish = arrays, out_shape, sems, start, finish


def _with_host(host, n_in, n_out, refs):
    if host is None:
        return refs[:n_in], (), refs[n_in:n_in + n_out], (), refs[n_in + n_out:], ()
    hi, ho, hs = len(host.arrays), len(host.out_shape), len(host.sems)
    ins, h_in = refs[:n_in], refs[n_in:n_in + hi]
    outs = refs[n_in + hi:n_in + hi + n_out]
    h_out = refs[n_in + hi + n_out:n_in + hi + n_out + ho]
    rest = refs[n_in + hi + n_out + ho:]
    return ins, h_in, outs, h_out, rest[:len(rest) - hs], rest[len(rest) - hs:]


HBM_SPEC = pl.BlockSpec(memory_space=pl.ANY)


def mm_nn(a, w, bias, *, out_dtype, tm, tn, name, host=None):
    T, K = a.shape
    N = w.shape[1]
    tm, tn = min(tm, T), min(tn, N)
    nj, ni = N // tn, T // tm
    n_in = 2 if bias is None else 3

    def body(*refs):
        ins, h_in, (o_ref,), h_out, _, h_sems = _with_host(host, n_in, 1, refs)
        a_ref, w_ref = ins[:2]
        j, i = pl.program_id(0), pl.program_id(1)
        if host is not None:
            @pl.when((j == 0) & (i == 0))
            def _():
                host.start(h_in, h_out, h_sems)
        acc = jnp.dot(a_ref[...].astype(BF), w_ref[...], preferred_element_type=F32)
        if bias is not None:
            acc = acc + ins[2][...]
        o_ref[...] = acc.astype(o_ref.dtype)
        if host is not None:
            @pl.when((j == nj - 1) & (i == ni - 1))
            def _():
                host.finish(h_in, h_out, h_sems)

    in_specs = [pl.BlockSpec((tm, K), lambda j, i: (i, 0)), pl.BlockSpec((K, tn), lambda j, i: (0, j))]
    args = [a, w]
    if bias is not None:
        in_specs.append(pl.BlockSpec((1, tn), lambda j, i: (0, j)))
        args.append(bias)
    out_specs = [pl.BlockSpec((tm, tn), lambda j, i: (i, j))]
    out_shape = [jax.ShapeDtypeStruct((T, N), out_dtype)]
    scratch = []
    if host is not None:
        in_specs += [HBM_SPEC] * len(host.arrays)
        args += list(host.arrays)
        out_specs += [HBM_SPEC] * len(host.out_shape)
        out_shape += list(host.out_shape)
        scratch = list(host.sems)
    res = pl.pallas_call(
        body, name=name, grid=(nj, ni), in_specs=in_specs, out_specs=out_specs, out_shape=out_shape,
        scratch_shapes=scratch, compiler_params=_cp("arbitrary", "arbitrary"))(*args)
    return res[0] if host is None else (res[0], res[1:])


FF_HALF = D_FF // 2


def ffn_in_swiglu(x, wgu, *, tm, name, host=None):
    T, K = x.shape
    tm = min(tm, T)
    ni = T // tm

    def body(*refs):
        (x_ref, w_ref), h_in, (gu_ref, h_ref), h_out, _, h_sems = _with_host(host, 2, 2, refs)
        j, i = pl.program_id(0), pl.program_id(1)
        if host is not None:
            @pl.when((j == 0) & (i == 0))
            def _():
                host.start(h_in, h_out, h_sems)
        acc = jnp.dot(x_ref[...].astype(BF), w_ref[...], preferred_element_type=F32)
        gu = acc.astype(BF)
        gu_ref[...] = gu
        g = gu[:, :FF_HALF].astype(F32)
        h_ref[...] = (g * _sigmoid(g) * gu[:, FF_HALF:].astype(F32)).astype(BF)
        if host is not None:
            @pl.when((j == 1) & (i == ni - 1))
            def _():
                host.finish(h_in, h_out, h_sems)

    in_specs = [pl.BlockSpec((tm, K), lambda j, i: (i, 0)), pl.BlockSpec((K, 2 * FF_HALF), lambda j, i: (0, j))]
    args = [x, wgu]
    out_specs = [pl.BlockSpec((tm, 2 * FF_HALF), lambda j, i: (i, j)), pl.BlockSpec((tm, FF_HALF), lambda j, i: (i, j))]
    out_shape = [jax.ShapeDtypeStruct((T, 2 * D_FF), BF), jax.ShapeDtypeStruct((T, D_FF), BF)]
    scratch = []
    if host is not None:
        in_specs += [HBM_SPEC] * len(host.arrays)
        args += list(host.arrays)
        out_specs += [HBM_SPEC] * len(host.out_shape)
        out_shape += list(host.out_shape)
        scratch = list(host.sems)
    res = pl.pallas_call(
        body, name=name, grid=(2, ni), in_specs=in_specs, out_specs=out_specs, out_shape=out_shape,
        scratch_shapes=scratch, compiler_params=_cp("arbitrary", "arbitrary"))(*args)
    return (res[0], res[1]) if host is None else (res[0], res[1], res[2:])


def ffn_down_dx_swiglu(dp, wd, gu, *, tm, name):
    T = dp.shape[0]
    tm = min(tm, T)

    def body(dp_ref, w_ref, gu_ref, o_ref):
        dh = lax.dot_general(dp_ref[...], w_ref[...], (((1,), (1,)), ((), ())), preferred_element_type=F32)
        g = gu_ref[:, :FF_HALF].astype(F32)
        u = gu_ref[:, FF_HALF:].astype(F32)
        sg = _sigmoid(g)
        t = dh * sg
        s = g * sg
        o_ref[:, :FF_HALF] = (t * u * (1.0 + g - s)).astype(BF)
        o_ref[:, FF_HALF:] = (t * g).astype(BF)

    return pl.pallas_call(
        body, name=name, grid=(2, T // tm),
        in_specs=[pl.BlockSpec((tm, D), lambda j, i: (i, 0)), pl.BlockSpec((FF_HALF, D), lambda j, i: (j, 0)),
                  pl.BlockSpec((tm, 2 * FF_HALF), lambda j, i: (i, j))],
        out_specs=pl.BlockSpec((tm, 2 * FF_HALF), lambda j, i: (i, j)),
        out_shape=jax.ShapeDtypeStruct((T, 2 * D_FF), BF), compiler_params=_cp("arbitrary", "arbitrary"))(dp, wd, gu)


def mm_nt(a, w, res, *, out_dtype, tm, tc, name, host=None):
    T, C = a.shape
    K = w.shape[0]
    tm, tc = min(tm, T), min(tc, C)
    nc = C // tc
    ni = T // tm
    n_in = 2 if res is None else 3

    def body(*refs):
        ins, h_in, (o_ref,), h_out, (acc_ref,), h_sems = _with_host(host, n_in, 1, refs)
        a_ref, w_ref = ins[:2]
        r_ref = ins[2] if res is not None else None
        c = pl.program_id(1)
        if host is not None:
            @pl.when((pl.program_id(0) == 0) & (c == 0))
            def _():
                host.start(h_in, h_out, h_sems)

            @pl.when((pl.program_id(0) == ni - 1) & (c == nc - 1))
            def _():
                host.finish(h_in, h_out, h_sems)
        part = lax.dot_general(a_ref[...].astype(BF), w_ref[...], (((1,), (1,)), ((), ())),
                               preferred_element_type=F32)

        @pl.when(c == 0)
        def _():
            acc_ref[...] = part

        @pl.when(c > 0)
        def _():
            acc_ref[...] += part

        @pl.when(c == nc - 1)
        def _():
            out = acc_ref[...]
            if res is not None:
                out = out + ALPHA * r_ref[...]
            o_ref[...] = out.astype(o_ref.dtype)

    in_specs = [pl.BlockSpec((tm, tc), lambda i, c: (i, c)), pl.BlockSpec((K, tc), lambda i, c: (0, c))]
    args = [a, w]
    if res is not None:
        in_specs.append(pl.BlockSpec((tm, K), lambda i, c: (i, 0)))
        args.append(res)
    out_specs = [pl.BlockSpec((tm, K), lambda i, c: (i, 0))]
    out_shape = [jax.ShapeDtypeStruct((T, K), out_dtype)]
    scratch = [pltpu.VMEM((tm, K), F32)]
    if host is not None:
        in_specs += [HBM_SPEC] * len(host.arrays)
        args += list(host.arrays)
        out_specs += [HBM_SPEC] * len(host.out_shape)
        out_shape += list(host.out_shape)
        scratch += list(host.sems)
    out = pl.pallas_call(
        body, name=name, grid=(ni, nc), in_specs=in_specs, out_specs=out_specs, out_shape=out_shape,
        scratch_shapes=scratch, compiler_params=_cp("arbitrary", "arbitrary"))(*args)
    return out[0] if host is None else (out[0], out[1:])


def mm_tn(a, b, *, out_dtype, tk, tn, tt, name, colsum=False):
    T, K = a.shape
    N = b.shape[1]
    tk, tn, tt = min(tk, K), min(tn, N), min(tt, T)
    nt = T // tt

    def body(*refs):
        if colsum:
            a_ref, b_ref, o_ref, cs_ref, acc_ref = refs
        else:
            a_ref, b_ref, o_ref, acc_ref = refs
        i, t = pl.program_id(1), pl.program_id(2)
        bb = b_ref[...]
        part = lax.dot_general(a_ref[...].astype(BF), bb.astype(BF), (((0,), (0,)), ((), ())),
                               preferred_element_type=F32)

        @pl.when(t == 0)
        def _():
            acc_ref[...] = part

        @pl.when(t > 0)
        def _():
            acc_ref[...] += part

        @pl.when(t == nt - 1)
        def _():
            o_ref[...] = acc_ref[...].astype(o_ref.dtype)

        if colsum:
            s = jnp.dot(jnp.ones((8, tt), BF), bb.astype(BF), preferred_element_type=F32)[0:1, :]

            @pl.when((i == 0) & (t == 0))
            def _():
                cs_ref[...] = s

            @pl.when((i == 0) & (t > 0))
            def _():
                cs_ref[...] += s

    out_specs = [pl.BlockSpec((tk, tn), lambda j, i, t: (i, j))]
    out_shape = [jax.ShapeDtypeStruct((K, N), out_dtype)]
    if colsum:
        out_specs.append(pl.BlockSpec((1, tn), lambda j, i, t: (0, j)))
        out_shape.append(jax.ShapeDtypeStruct((1, N), F32))
    res = pl.pallas_call(
        body, name=name, grid=(N // tn, K // tk, nt),
        in_specs=[pl.BlockSpec((tt, tk), lambda j, i, t: (t, i)), pl.BlockSpec((tt, tn), lambda j, i, t: (t, j))],
        out_specs=out_specs, out_shape=out_shape, scratch_shapes=[pltpu.VMEM((tk, tn), F32)],
        compiler_params=_cp("arbitrary", "arbitrary", "arbitrary"))(a, b)
    return res if colsum else res[0]


def mm_res_ln(a, w, res, g, b, *, tm, name):
    T, K = a.shape
    tm = min(tm, T)

    def body(a_ref, w_ref, r_ref, g_ref, b_ref, y_ref, yb_ref, xh_ref, rs_ref):
        pre = ALPHA * r_ref[...] + jnp.dot(a_ref[...].astype(BF), w_ref[...], preferred_element_type=F32)
        mu = jnp.mean(pre, axis=-1, keepdims=True)
        cen = pre - mu
        var = jnp.mean(cen * cen, axis=-1, keepdims=True)
        rstd = lax.rsqrt(var + LN_EPS)
        xhat = cen * rstd
        y = xhat * g_ref[...] + b_ref[...]
        y_ref[...] = y
        yb_ref[...] = y.astype(BF)
        xh_ref[...] = xhat
        rs_ref[...] = rstd

    row = lambda i: (i, 0)
    fix = lambda i: (0, 0)
    return pl.pallas_call(
        body, name=name, grid=(T // tm,),
        in_specs=[pl.BlockSpec((tm, K), row), pl.BlockSpec((K, D), fix), pl.BlockSpec((tm, D), row),
                  pl.BlockSpec((1, D), fix), pl.BlockSpec((1, D), fix)],
        out_specs=[pl.BlockSpec((tm, D), row), pl.BlockSpec((tm, D), row), pl.BlockSpec((tm, D), row),
                   pl.BlockSpec((tm, 1), row)],
        out_shape=[jax.ShapeDtypeStruct((T, D), F32), jax.ShapeDtypeStruct((T, D), BF),
                   jax.ShapeDtypeStruct((T, D), F32), jax.ShapeDtypeStruct((T, 1), F32)],
        compiler_params=_cp("arbitrary"))(a, w, res, g, b)


def ln_bwd(dy, xhat, rstd, g, *, tm, name, loss_from=None, dw_of=None):
    T = xhat.shape[0]
    tm = min(tm, T)
    nt = T // tm
    with_loss = loss_from is not None
    with_dw = dw_of is not None

    def body(*refs):
        if with_dw:
            acc_ref, refs = refs[-1], refs[:-1]
            n_main_in = 5 if with_loss else 4
            a_ref = refs[n_main_in]
            dw_ref = refs[-1]
            refs = refs[:n_main_in] + refs[n_main_in + 1:-1]
        if with_loss:
            xh_ref, rs_ref, g_ref, b_ref, t_ref, dp_ref, dpb_ref, dg_ref, db_ref, ls_ref = refs
        else:
            dy_ref, xh_ref, rs_ref, g_ref, dp_ref, dpb_ref, dg_ref, db_ref = refs
        i = pl.program_id(0)
        xhat_ = xh_ref[...]
        gg = g_ref[...]
        if with_loss:
            err = xhat_ * gg + b_ref[...] - t_ref[...]
            dyv = err * (1.0 / D)
            lpart = 0.5 * jnp.sum(jnp.sum(err * err, axis=-1, keepdims=True) * (1.0 / D))
        else:
            dyv = dy_ref[...]
        dxh = dyv * gg
        m1 = jnp.mean(dxh, axis=-1, keepdims=True)
        m2 = jnp.mean(dxh * xhat_, axis=-1, keepdims=True)
        dpre = rs_ref[...] * (dxh - m1 - xhat_ * m2)
        dp_ref[...] = dpre
        dpb = dpre.astype(BF)
        dpb_ref[...] = dpb
        dgp = jnp.sum(dyv * xhat_, axis=0, keepdims=True)
        dbp = jnp.sum(dyv, axis=0, keepdims=True)
        if with_dw:
            dwp = lax.dot_general(a_ref[...], dpb, (((0,), (0,)), ((), ())), preferred_element_type=F32)

        @pl.when(i == 0)
        def _():
            dg_ref[...] = dgp
            db_ref[...] = dbp
            if with_loss:
                ls_ref[...] = jnp.full((8, 128), lpart, F32)
            if with_dw:
                acc_ref[...] = dwp

        @pl.when(i > 0)
        def _():
            dg_ref[...] += dgp
            db_ref[...] += dbp
            if with_loss:
                ls_ref[...] += jnp.full((8, 128), lpart, F32)
            if with_dw:
                acc_ref[...] += dwp

        if with_dw:
            @pl.when(i == nt - 1)
            def _():
                dw_ref[...] = acc_ref[...].astype(BF)

    row = lambda i: (i, 0)
    fix = lambda i: (0, 0)
    if with_loss:
        in_specs = [pl.BlockSpec((tm, D), row), pl.BlockSpec((tm, 1), row), pl.BlockSpec((1, D), fix),
                    pl.BlockSpec((1, D), fix), pl.BlockSpec((tm, D), row)]
        args = [xhat, rstd, g, loss_from[0], loss_from[1]]
    else:
        in_specs = [pl.BlockSpec((tm, D), row), pl.BlockSpec((tm, D), row), pl.BlockSpec((tm, 1), row),
                    pl.BlockSpec((1, D), fix)]
        args = [dy, xhat, rstd, g]
    out_specs = [pl.BlockSpec((tm, D), row), pl.BlockSpec((tm, D), row), pl.BlockSpec((1, D), fix),
                 pl.BlockSpec((1, D), fix)]
    out_shape = [jax.ShapeDtypeStruct((T, D), F32), jax.ShapeDtypeStruct((T, D), BF),
                 jax.ShapeDtypeStruct((1, D), F32), jax.ShapeDtypeStruct((1, D), F32)]
    if with_loss:
        out_specs.append(pl.BlockSpec((8, 128), fix))
        out_shape.append(jax.ShapeDtypeStruct((8, 128), F32))
    scratch = []
    if with_dw:
        K = dw_of.shape[1]
        in_specs.append(pl.BlockSpec((tm, K), row))
        args.append(dw_of)
        out_specs.append(pl.BlockSpec((K, D), fix))
        out_shape.append(jax.ShapeDtypeStruct((K, D), BF))
        scratch.append(pltpu.VMEM((K, D), F32))
    return pl.pallas_call(body, name=name, grid=(nt,), in_specs=in_specs, out_specs=out_specs,
                          out_shape=out_shape, scratch_shapes=scratch, compiler_params=_cp("arbitrary"))(*args)


def _accumulate(ref, value):
    i = pl.program_id(0)

    @pl.when(i == 0)
    def _():
        ref[...] = value

    @pl.when(i > 0)
    def _():
        ref[...] += value


def _prev_halo(tm, blk):
    return lambda i: (jnp.maximum(i * (tm // HALO) - 1, 0), blk)


def _next_halo(tm, T, blk):
    return lambda i: (jnp.minimum((i + 1) * (tm // HALO), T // HALO - 1), blk)


def _pool_p(ext, t, g):
    e = ext[:, g * POOL_GD:(g + 1) * POOL_GD]
    s = e
    for sh in (1, 2, 4, 8)[:g + 1]:
        s = s + pltpu.roll(s, sh, axis=0)
    inv = 1.0 / jnp.minimum(t + 1, POOL_WINDOWS[g]).astype(F32)
    return s[HALO:] * inv - e[HALO:]


def pool_fwd(z, pw, *, tm, name):
    T = z.shape[0]
    tm = min(tm, T)

    def body(zm_ref, zh_ref, pw_ref, o_ref):
        i = pl.program_id(0)
        keep = jnp.where(i == 0, 0.0, 1.0).astype(F32)
        ext = jnp.concatenate([zh_ref[...].astype(F32) * keep, zm_ref[...].astype(F32)], axis=0)
        t = i * tm + lax.broadcasted_iota(jnp.int32, (tm, 1), 0)
        outs = [jnp.dot(_pool_p(ext, t, g).astype(BF), pw_ref[g], preferred_element_type=F32) for g in range(4)]
        o_ref[...] = jnp.concatenate(outs, axis=1).astype(o_ref.dtype)

    return pl.pallas_call(
        body, name=name, grid=(T // tm,),
        in_specs=[pl.BlockSpec((tm, D), lambda i: (i, ZB_POOL)), pl.BlockSpec((HALO, D), _prev_halo(tm, ZB_POOL)),
                  pl.BlockSpec((4, POOL_GD, POOL_GD), lambda i: (0, 0, 0))],
        out_specs=pl.BlockSpec((tm, D), lambda i: (i, 0)),
        out_shape=jax.ShapeDtypeStruct((T, D), BF), compiler_params=_cp("arbitrary"))(z, z, pw)


def pool_bwd(dz, dyp, yp_pre, z, pw, ps, *, tm, name):
    T = z.shape[0]
    tm = min(tm, T)
    nt = T // tm

    def body(dz_in, dy_ref, dyn_ref, yp_ref, zm_ref, zh_ref, pw_ref, ps_ref, dz_ref, dpw_ref, dps_ref):
        del dz_in
        i = pl.program_id(0)
        keep_p = jnp.where(i == 0, 0.0, 1.0).astype(F32)
        keep_n = jnp.where(i == nt - 1, 0.0, 1.0).astype(F32)
        ext = jnp.concatenate([zh_ref[...].astype(F32) * keep_p, zm_ref[...].astype(F32)], axis=0)
        t = i * tm + lax.broadcasted_iota(jnp.int32, (tm, 1), 0)
        psv = ps_ref[...]
        dy = dy_ref[...].astype(F32)
        dyp_ext = jnp.concatenate([dy, dyn_ref[...].astype(F32) * keep_n], axis=0) * psv
        t_ext = i * tm + lax.broadcasted_iota(jnp.int32, (tm + HALO, 1), 0)
        dps = jnp.sum(dy * yp_ref[...].astype(F32), axis=0, keepdims=True)
        dzs, dpws = [], []
        for g in range(4):
            sl = slice(g * POOL_GD, (g + 1) * POOL_GD)
            dyg = dyp_ext[:, sl].astype(BF)
            dp = lax.dot_general(dyg, pw_ref[g], (((1,), (1,)), ((), ())), preferred_element_type=F32)
            q = dp * (1.0 / jnp.minimum(t_ext + 1, POOL_WINDOWS[g]).astype(F32))
            s = q
            for sh in (1, 2, 4, 8)[:g + 1]:
                s = s + pltpu.roll(s, tm + HALO - sh, axis=0)
            dzs.append(s[:tm] - dp[:tm])
            p = _pool_p(ext, t, g).astype(BF)
            dpws.append(lax.dot_general(p, dyg[:tm], (((0,), (0,)), ((), ())), preferred_element_type=F32))
        dz_ref[...] = jnp.concatenate(dzs, axis=1).astype(dz_ref.dtype)

        @pl.when(i == 0)
        def _():
            for g in range(4):
                dpw_ref[g] = dpws[g]
            dps_ref[...] = dps

        @pl.when(i > 0)
        def _():
            for g in range(4):
                dpw_ref[g] += dpws[g]
            dps_ref[...] += dps

    row = lambda i: (i, 0)
    return pl.pallas_call(
        body, name=name, grid=(nt,),
        in_specs=[pl.BlockSpec(memory_space=pl.ANY),
                  pl.BlockSpec((tm, D), row), pl.BlockSpec((HALO, D), _next_halo(tm, T, 0)),
                  pl.BlockSpec((tm, D), row),
                  pl.BlockSpec((tm, D), lambda i: (i, ZB_POOL)), pl.BlockSpec((HALO, D), _prev_halo(tm, ZB_POOL)),
                  pl.BlockSpec((4, POOL_GD, POOL_GD), lambda i: (0, 0, 0)), pl.BlockSpec((1, D), lambda i: (0, 0))],
        out_specs=[pl.BlockSpec((tm, D), lambda i: (i, ZB_POOL)),
                   pl.BlockSpec((4, POOL_GD, POOL_GD), lambda i: (0, 0, 0)), pl.BlockSpec((1, D), lambda i: (0, 0))],
        out_shape=[jax.ShapeDtypeStruct(dz.shape, dz.dtype), jax.ShapeDtypeStruct((4, POOL_GD, POOL_GD), F32),
                   jax.ShapeDtypeStruct((1, D), F32)],
        input_output_aliases={0: 0}, compiler_params=_cp("arbitrary"))(dz, dyp, dyp, yp_pre, z, z, pw, ps)


def _fill_ext(ext_s, halo, main, keep):
    ext_s[0:HALO, :] = halo * keep
    ext_s[HALO:, :] = main


def _lru_gates(ext_s, tm, cw, cb, wr_ref, br, wi_ref, bi, lam):
    shifted = []
    v = cb
    for k in range(4):
        zs = ext_s[pl.ds(HALO - 3 + k, tm), :]
        shifted.append(zs)
        v = v + cw[k:k + 1, :] * zs
    vb = v.astype(BF)
    rp, ip = [], []
    for h in range(LRU_HEADS):
        sl = slice(h * LRU_HD, (h + 1) * LRU_HD)
        rp.append(jnp.dot(vb[:, sl], wr_ref[h], preferred_element_type=F32))
        ip.append(jnp.dot(vb[:, sl], wi_ref[h], preferred_element_type=F32))
    r = _sigmoid(jnp.concatenate(rp, axis=1) + br)
    ig = _sigmoid(jnp.concatenate(ip, axis=1) + bi)
    sp = jnp.maximum(-lam, 0.0) + jnp.log(1.0 + jnp.exp(-jnp.abs(lam)))
    a = jnp.exp(-LRU_C * r * sp)
    om = 1.0 - a * a
    rs = lax.rsqrt(om)
    return v, vb, r, ig, a, om, rs, sp, shifted


def lru_fwd(z, cw, cb, wr, br, wi, bi, lam, wlo, *, tm, name):
    T = z.shape[0]
    tm = min(tm, T)
    nch = tm // 8

    def body(zm_ref, zh_ref, cw_ref, cb_ref, wr_ref, br_ref, wi_ref, bi_ref, lam_ref, wlo_ref, h_ref, y_ref,
             a_s, b_s, carry, ext_s):
        i = pl.program_id(0)

        @pl.when(i == 0)
        def _():
            carry[...] = jnp.zeros_like(carry)

        keep = jnp.where(i == 0, 0.0, 1.0).astype(F32)
        _fill_ext(ext_s, zh_ref[...].astype(F32), zm_ref[...].astype(F32), keep)
        v, _, _, ig, a, om, rs, _, _ = _lru_gates(ext_s, tm, cw_ref[...], cb_ref[...], wr_ref, br_ref[...], wi_ref,
                                                  bi_ref[...], lam_ref[...])
        a_s[...] = a
        b_s[...] = jnp.where(om > 0.0, om * rs, 0.0) * (ig * v)
        row = lax.broadcasted_iota(jnp.int32, (8, D), 0)

        def step(ci, hprev):
            sl = pl.ds(pl.multiple_of(ci * 8, 8), 8)
            aa, bb = a_s[sl, :], b_s[sl, :]
            for s in (1, 2, 4):
                m = row >= s
                bb = bb + aa * jnp.where(m, pltpu.roll(bb, s, axis=0), 0.0)
                aa = aa * jnp.where(m, pltpu.roll(aa, s, axis=0), 1.0)
            h = bb + aa * hprev
            h_ref[sl, :] = h
            return jnp.broadcast_to(h[7:8, :], (8, D))

        carry[...] = lax.fori_loop(0, nch, step, carry[...])
        y_ref[...] = jnp.dot(h_ref[...].astype(BF), wlo_ref[...], preferred_element_type=F32).astype(BF)

    fix2 = lambda i: (0, 0)
    fix3 = lambda i: (0, 0, 0)
    return pl.pallas_call(
        body, name=name, grid=(T // tm,),
        in_specs=[pl.BlockSpec((tm, D), lambda i: (i, ZB_LRU)), pl.BlockSpec((HALO, D), _prev_halo(tm, ZB_LRU)),
                  pl.BlockSpec((4, D), fix2), pl.BlockSpec((1, D), fix2),
                  pl.BlockSpec((LRU_HEADS, LRU_HD, LRU_HD), fix3), pl.BlockSpec((1, D), fix2),
                  pl.BlockSpec((LRU_HEADS, LRU_HD, LRU_HD), fix3), pl.BlockSpec((1, D), fix2),
                  pl.BlockSpec((1, D), fix2), pl.BlockSpec((D, D), fix2)],
        out_specs=[pl.BlockSpec((tm, D), lambda i: (i, 0)), pl.BlockSpec((tm, D), lambda i: (i, 0))],
        out_shape=[jax.ShapeDtypeStruct((T, D), F32), jax.ShapeDtypeStruct((T, D), BF)],
        scratch_shapes=[pltpu.VMEM((tm, D), F32), pltpu.VMEM((tm, D), F32), pltpu.VMEM((8, D), F32),
                        pltpu.VMEM((tm + HALO, D), F32)],
        compiler_params=_cp("arbitrary"))(z, z, cw, cb, wr, br, wi, bi, lam, wlo)


def lru_bwd(dz, dyl, z, h, cw, cb, wr, br, wi, bi, lam, wlo, *, tm, name):
    T = z.shape[0]
    tm = min(tm, T)
    nt = T // tm
    nch = tm // 8

    def body(dz_in, dy_ref, zm_ref, zh_ref, h_ref, hh_ref, cw_ref, cb_ref, wr_ref, br_ref, wi_ref, bi_ref, lam_ref,
             wlo_ref, dz_ref, dcw_ref, dcb_ref, dwr_ref, dbr_ref, dwi_ref, dbi_ref, dlam_ref, dwlo_ref,
             c_s, g_s, dh_s, dh_carry, a_ext, dv_ext, ext_s, h_ext, wlo_acc):
        del dz_in
        i = pl.program_id(0)
        ti = nt - 1 - i

        @pl.when(i == 0)
        def _():
            dh_carry[...] = jnp.zeros_like(dh_carry)
            a_ext[tm:, :] = jnp.zeros((8, D), F32)
            dv_ext[tm:, :] = jnp.zeros((HALO, D), F32)

        keep = jnp.where(ti == 0, 0.0, 1.0).astype(F32)
        _fill_ext(ext_s, zh_ref[...].astype(F32), zm_ref[...].astype(F32), keep)
        cw_ = cw_ref[...]
        lam_ = lam_ref[...]
        v, vb, r, ig, a, om, rs, sp, shifted = _lru_gates(ext_s, tm, cw_, cb_ref[...], wr_ref, br_ref[...], wi_ref,
                                                          bi_ref[...], lam_)
        mult = jnp.where(om > 0.0, om * rs, 0.0)
        a_ext[0:tm, :] = a
        c_s[...] = a_ext[pl.ds(1, tm), :]
        g_s[...] = lax.dot_general(dy_ref[...], wlo_ref[...], (((1,), (1,)), ((), ())), preferred_element_type=F32)
        row = lax.broadcasted_iota(jnp.int32, (8, D), 0)

        def step(k, nxt):
            ci = nch - 1 - k
            sl = pl.ds(pl.multiple_of(ci * 8, 8), 8)
            cc, gg = c_s[sl, :], g_s[sl, :]
            for s in (1, 2, 4):
                m = row < 8 - s
                gg = gg + cc * jnp.where(m, pltpu.roll(gg, 8 - s, axis=0), 0.0)
                cc = cc * jnp.where(m, pltpu.roll(cc, 8 - s, axis=0), 1.0)
            dh = gg + cc * nxt
            dh_s[sl, :] = dh
            return jnp.broadcast_to(dh[0:1, :], (8, D))

        dh_carry[...] = lax.fori_loop(0, nch, step, dh_carry[...])
        a_ext[tm:, :] = a[0:8, :]
        dh = dh_s[...]
        h_ext[0:8, :] = hh_ref[...] * keep
        hv = h_ref[...]
        h_ext[8:, :] = hv
        hprev = h_ext[pl.ds(7, tm), :]
        dwlo = lax.dot_general(hv.astype(BF), dy_ref[...], (((0,), (0,)), ((), ())), preferred_element_type=F32)
        iv = ig * v
        da = dh * hprev
        dmult = dh * iv
        div = dh * mult
        dlog = da * a - dmult * (a * a) * rs
        dr = dlog * (-LRU_C * sp)
        dlam = jnp.sum(dlog * r, axis=0, keepdims=True) * (LRU_C * _sigmoid(-lam_))
        di = div * v
        dv = div * ig
        drp = dr * r * (1.0 - r)
        dip = di * ig * (1.0 - ig)
        drb, dib = drp.astype(BF), dip.astype(BF)
        dvh, dwr, dwi = [], [], []
        nt_dims = (((1,), (1,)), ((), ()))
        tn_dims = (((0,), (0,)), ((), ()))
        for hd in range(LRU_HEADS):
            sl = slice(hd * LRU_HD, (hd + 1) * LRU_HD)
            dvh.append(lax.dot_general(drb[:, sl], wr_ref[hd], nt_dims, preferred_element_type=F32)
                       + lax.dot_general(dib[:, sl], wi_ref[hd], nt_dims, preferred_element_type=F32))
            dwr.append(lax.dot_general(vb[:, sl], drb[:, sl], tn_dims, preferred_element_type=F32))
            dwi.append(lax.dot_general(vb[:, sl], dib[:, sl], tn_dims, preferred_element_type=F32))
        dv = dv + jnp.concatenate(dvh, axis=1)
        dv_ext[0:tm, :] = dv
        dzl = cw_[3:4, :] * dv
        for k in range(3):
            dzl = dzl + cw_[k:k + 1, :] * dv_ext[pl.ds(3 - k, tm), :]
        dz_ref[...] = dzl.astype(dz_ref.dtype)
        dv_ext[tm:, :] = dv[:HALO]
        dcw = jnp.concatenate([jnp.sum(dv * shifted[k], axis=0, keepdims=True) for k in range(4)], axis=0)
        dcb = jnp.sum(dv, axis=0, keepdims=True)
        dbr = jnp.sum(drp, axis=0, keepdims=True)
        dbi = jnp.sum(dip, axis=0, keepdims=True)

        @pl.when(i == 0)
        def _():
            dcw_ref[...] = dcw
            dcb_ref[...] = dcb
            dbr_ref[...] = dbr
            dbi_ref[...] = dbi
            dlam_ref[...] = dlam
            wlo_acc[...] = dwlo
            for hd in range(LRU_HEADS):
                dwr_ref[hd] = dwr[hd]
                dwi_ref[hd] = dwi[hd]

        @pl.when(i > 0)
        def _():
            dcw_ref[...] += dcw
            dcb_ref[...] += dcb
            dbr_ref[...] += dbr
            dbi_ref[...] += dbi
            dlam_ref[...] += dlam
            wlo_acc[...] += dwlo
            for hd in range(LRU_HEADS):
                dwr_ref[hd] += dwr[hd]
                dwi_ref[hd] += dwi[hd]

        @pl.when(i == nt - 1)
        def _():
            dwlo_ref[...] = wlo_acc[...].astype(BF)

    fix2 = lambda i: (0, 0)
    fix3 = lambda i: (0, 0, 0)
    rev = lambda i: (nt - 1 - i, 0)
    vec = pl.BlockSpec((1, D), fix2)
    hw = pl.BlockSpec((LRU_HEADS, LRU_HD, LRU_HD), fix3)
    return pl.pallas_call(
        body, name=name, grid=(nt,),
        in_specs=[pl.BlockSpec(memory_space=pl.ANY),
                  pl.BlockSpec((tm, D), rev),
                  pl.BlockSpec((tm, D), lambda i: (nt - 1 - i, ZB_LRU)),
                  pl.BlockSpec((HALO, D), lambda i: (jnp.maximum((nt - 1 - i) * (tm // HALO) - 1, 0), ZB_LRU)),
                  pl.BlockSpec((tm, D), rev),
                  pl.BlockSpec((8, D), lambda i: (jnp.maximum((nt - 1 - i) * (tm // 8) - 1, 0), 0)),
                  pl.BlockSpec((4, D), fix2), vec, hw, vec, hw, vec, vec, pl.BlockSpec((D, D), fix2)],
        out_specs=[pl.BlockSpec((tm, D), lambda i: (nt - 1 - i, ZB_LRU)),
                   pl.BlockSpec((4, D), fix2), vec, hw, vec, hw, vec, vec, pl.BlockSpec((D, D), fix2)],
        out_shape=[jax.ShapeDtypeStruct(dz.shape, dz.dtype), jax.ShapeDtypeStruct((4, D), F32),
                   jax.ShapeDtypeStruct((1, D), F32), jax.ShapeDtypeStruct((LRU_HEADS, LRU_HD, LRU_HD), F32),
                   jax.ShapeDtypeStruct((1, D), F32), jax.ShapeDtypeStruct((LRU_HEADS, LRU_HD, LRU_HD), F32),
                   jax.ShapeDtypeStruct((1, D), F32), jax.ShapeDtypeStruct((1, D), F32),
                   jax.ShapeDtypeStruct((D, D), BF)],
        scratch_shapes=[pltpu.VMEM((tm, D), F32), pltpu.VMEM((tm, D), F32), pltpu.VMEM((tm, D), F32),
                        pltpu.VMEM((8, D), F32), pltpu.VMEM((tm + 8, D), F32), pltpu.VMEM((tm + HALO, D), F32),
                        pltpu.VMEM((tm + HALO, D), F32), pltpu.VMEM((tm + 8, D), F32), pltpu.VMEM((D, D), F32)],
        input_output_aliases={0: 0},
        compiler_params=_cp("arbitrary"))(dz, dyl, z, z, h, h, cw, cb, wr, br, wi, bi, lam, wlo)


def _sconv_cv(u_ext, sw):
    shifted = []
    cv = None
    for k in range(3):
        us = (u_ext if k == 2 else pltpu.roll(u_ext, 2 - k, axis=0))[HALO:]
        shifted.append(us)
        term = sw[k:k + 1, :] * us
        cv = term if cv is None else cv + term
    return cv, shifted


def sconv_fwd(z, sw, wso, *, tm, name):
    T = z.shape[0]
    tm = min(tm, T)

    def body(zm_ref, zh_ref, sw_ref, wso_ref, s_ref, y_ref):
        i = pl.program_id(0)
        keep = jnp.where(i == 0, 0.0, 1.0).astype(F32)
        zm = zm_ref[...].astype(F32)
        zh = zh_ref[...].astype(F32)
        u_ext = jnp.concatenate([zh[:, D:2 * D] * zh[:, 2 * D:] * keep, zm[:, D:2 * D] * zm[:, 2 * D:]], axis=0)
        cv, _ = _sconv_cv(u_ext, sw_ref[...])
        s = (zm[:, :D] * cv).astype(BF)
        s_ref[...] = s
        y_ref[...] = jnp.dot(s, wso_ref[...], preferred_element_type=F32).astype(BF)

    return pl.pallas_call(
        body, name=name, grid=(T // tm,),
        in_specs=[pl.BlockSpec((tm, 3 * D), lambda i: (i, ZB_SCONV)),
                  pl.BlockSpec((HALO, 3 * D), _prev_halo(tm, ZB_SCONV)),
                  pl.BlockSpec((3, D), lambda i: (0, 0)), pl.BlockSpec((D, D), lambda i: (0, 0))],
        out_specs=[pl.BlockSpec((tm, D), lambda i: (i, 0)), pl.BlockSpec((tm, D), lambda i: (i, 0))],
        out_shape=[jax.ShapeDtypeStruct((T, D), BF), jax.ShapeDtypeStruct((T, D), BF)],
        compiler_params=_cp("arbitrary"))(z, z, sw, wso)


def sconv_bwd(dz, dyc, z, sw, wso, *, tm, name):
    T = z.shape[0]
    tm = min(tm, T)
    nt = T // tm

    def body(dz_in, dy_ref, dyn_ref, zm_ref, zp_ref, zn_ref, sw_ref, wso_ref, dz_ref, dsw_ref, dwso_ref, acc_ref):
        del dz_in
        i = pl.program_id(0)
        keep_p = jnp.where(i == 0, 0.0, 1.0).astype(F32)
        keep_n = jnp.where(i == nt - 1, 0.0, 1.0).astype(F32)
        sw_ = sw_ref[...]
        zm = zm_ref[...].astype(F32)
        zp = zp_ref[...].astype(F32)
        zb, zc, zh = zm[:, :D], zm[:, D:2 * D], zm[:, 2 * D:]
        u_ext = jnp.concatenate([zp[:, D:2 * D] * zp[:, 2 * D:] * keep_p, zc * zh], axis=0)
        cv, shifted = _sconv_cv(u_ext, sw_)
        dy_ext = jnp.concatenate([dy_ref[...], dyn_ref[...]], axis=0)
        ds_ext = lax.dot_general(dy_ext, wso_ref[...], (((1,), (1,)), ((), ())), preferred_element_type=F32)
        zb_ext = jnp.concatenate([zb, zn_ref[...][:, :D].astype(F32) * keep_n], axis=0)
        dcv_ext = ds_ext * zb_ext
        du = sw_[2:3, :] * dcv_ext[:tm]
        for k in range(2):
            du = du + sw_[k:k + 1, :] * pltpu.roll(dcv_ext, tm + HALO - (2 - k), axis=0)[:tm]
        dz_ref[...] = jnp.concatenate([ds_ext[:tm] * cv, du * zh, du * zc], axis=1).astype(dz_ref.dtype)
        dcv = dcv_ext[:tm]
        dsw = jnp.concatenate([jnp.sum(dcv * shifted[k], axis=0, keepdims=True) for k in range(3)], axis=0)
        dwso = lax.dot_general((zb * cv).astype(BF), dy_ref[...], (((0,), (0,)), ((), ())),
                               preferred_element_type=F32)

        @pl.when(i == 0)
        def _():
            dsw_ref[...] = dsw
            acc_ref[...] = dwso

        @pl.when(i > 0)
        def _():
            dsw_ref[...] += dsw
            acc_ref[...] += dwso

        @pl.when(i == nt - 1)
        def _():
            dwso_ref[...] = acc_ref[...].astype(BF)

    return pl.pallas_call(
        body, name=name, grid=(nt,),
        in_specs=[pl.BlockSpec(memory_space=pl.ANY),
                  pl.BlockSpec((tm, D), lambda i: (i, 0)), pl.BlockSpec((HALO, D), _next_halo(tm, T, 0)),
                  pl.BlockSpec((tm, 3 * D), lambda i: (i, ZB_SCONV)),
                  pl.BlockSpec((HALO, 3 * D), _prev_halo(tm, ZB_SCONV)),
                  pl.BlockSpec((HALO, 3 * D), _next_halo(tm, T, ZB_SCONV)),
                  pl.BlockSpec((3, D), lambda i: (0, 0)), pl.BlockSpec((D, D), lambda i: (0, 0))],
        out_specs=[pl.BlockSpec((tm, 3 * D), lambda i: (i, ZB_SCONV)), pl.BlockSpec((3, D), lambda i: (0, 0)),
                   pl.BlockSpec((D, D), lambda i: (0, 0))],
        out_shape=[jax.ShapeDtypeStruct(dz.shape, dz.dtype), jax.ShapeDtypeStruct((3, D), F32),
                   jax.ShapeDtypeStruct((D, D), BF)],
        scratch_shapes=[pltpu.VMEM((D, D), F32)],
        input_output_aliases={0: 0}, compiler_params=_cp("arbitrary"))(dz, dyc, dyc, z, z, z, sw, wso)


def merge_fwd(z, yp_pre, yl, yc, ps, *, tm, name):
    T = z.shape[0]
    tm = min(tm, T)

    def body(zg_ref, yp_ref, yl_ref, yc_ref, ps_ref, o_ref):
        gts = _sigmoid(zg_ref[...].astype(F32))
        m = (gts[:, :D] * (yp_ref[...].astype(F32) * ps_ref[...]) + gts[:, D:2 * D] * yl_ref[...].astype(F32)
             + gts[:, 2 * D:] * yc_ref[...].astype(F32))
        o_ref[...] = m.astype(o_ref.dtype)

    row = lambda i: (i, 0)
    return pl.pallas_call(
        body, name=name, grid=(T // tm,),
        in_specs=[pl.BlockSpec((tm, 3 * D), lambda i: (i, ZB_GATE)), pl.BlockSpec((tm, D), row),
                  pl.BlockSpec((tm, D), row), pl.BlockSpec((tm, D), row), pl.BlockSpec((1, D), lambda i: (0, 0))],
        out_specs=pl.BlockSpec((tm, D), row), out_shape=jax.ShapeDtypeStruct((T, D), BF),
        compiler_params=_cp("arbitrary"))(z, yp_pre, yl, yc, ps)


def merge_bwd(dm, z, yp_pre, yl, yc, ps, *, tm, name):
    T = z.shape[0]
    tm = min(tm, T)

    def body(dm_ref, zg_ref, yp_ref, yl_ref, yc_ref, ps_ref, dz_ref, dyp_ref, dyl_ref, dyc_ref):
        gts = _sigmoid(zg_ref[...].astype(F32))
        dmv = dm_ref[...].astype(F32)
        ys = (yp_ref[...].astype(F32) * ps_ref[...], yl_ref[...].astype(F32), yc_ref[...].astype(F32))
        outs = (dyp_ref, dyl_ref, dyc_ref)
        dgs = []
        for j in range(3):
            gj = gts[:, j * D:(j + 1) * D]
            outs[j][...] = (dmv * gj).astype(BF)
            dgs.append(dmv * ys[j] * gj * (1.0 - gj))
        dz_ref[...] = jnp.concatenate(dgs, axis=1).astype(dz_ref.dtype)

    row = lambda i: (i, 0)
    return pl.pallas_call(
        body, name=name, grid=(T // tm,),
        in_specs=[pl.BlockSpec((tm, D), row), pl.BlockSpec((tm, 3 * D), lambda i: (i, ZB_GATE)),
                  pl.BlockSpec((tm, D), row), pl.BlockSpec((tm, D), row), pl.BlockSpec((tm, D), row),
                  pl.BlockSpec((1, D), lambda i: (0, 0))],
        out_specs=[pl.BlockSpec((tm, 3 * D), lambda i: (i, ZB_GATE)), pl.BlockSpec((tm, D), row),
                   pl.BlockSpec((tm, D), row), pl.BlockSpec((tm, D), row)],
        out_shape=[jax.ShapeDtypeStruct((T, IN_COLS), BF), jax.ShapeDtypeStruct((T, D), BF),
                   jax.ShapeDtypeStruct((T, D), BF), jax.ShapeDtypeStruct((T, D), BF)],
        compiler_params=_cp("arbitrary"))(dm, z, yp_pre, yl, yc, ps)


def _attn_probs(qh, kh):
    s = lax.dot_general(qh, kh, (((1,), (1,)), ((), ())), preferred_element_type=F32) * (X_HD ** -0.5)
    e = jnp.exp(s - jnp.max(s, axis=-1, keepdims=True))
    return e / jnp.sum(e, axis=-1, keepdims=True)


def attn_fwd(xb, wq, kb, vb, *, tm, name):
    T = xb.shape[0]
    tm = min(tm, T)

    def body(x_ref, wq_ref, k_ref, v_ref, q_ref, o_ref):
        q = jnp.dot(x_ref[...], wq_ref[...], preferred_element_type=F32).astype(BF)
        q_ref[...] = q
        outs = []
        for h in range(X_HEADS):
            sl = slice(h * X_HD, (h + 1) * X_HD)
            p = _attn_probs(q[:, sl], k_ref[:, sl])
            outs.append(jnp.dot(p.astype(BF), v_ref[:, sl], preferred_element_type=F32))
        o_ref[...] = jnp.concatenate(outs, axis=1).astype(BF)

    row = lambda i: (i, 0)
    fix = lambda i: (0, 0)
    return pl.pallas_call(
        body, name=name, grid=(T // tm,),
        in_specs=[pl.BlockSpec((tm, D), row), pl.BlockSpec((D, D), fix), pl.BlockSpec((N_MEM, D), fix),
                  pl.BlockSpec((N_MEM, D), fix)],
        out_specs=[pl.BlockSpec((tm, D), row), pl.BlockSpec((tm, D), row)],
        out_shape=[jax.ShapeDtypeStruct((T, D), BF), jax.ShapeDtypeStruct((T, D), BF)],
        compiler_params=_cp("arbitrary"))(xb, wq, kb, vb)


def attn_bwd(dxa, wo, q, kb, vb, xb, *, tm, name):
    T = q.shape[0]
    tm = min(tm, T)
    nt = T // tm

    def body(d_ref, wo_ref, q_ref, k_ref, v_ref, x_ref, dq_ref, dk_ref, dv_ref, dwq_ref, acc_ref):
        i = pl.program_id(0)
        do = lax.dot_general(d_ref[...], wo_ref[...], (((1,), (1,)), ((), ())),
                             preferred_element_type=F32).astype(BF)
        q = q_ref[...]
        dqs, dks, dvs = [], [], []
        for h in range(X_HEADS):
            sl = slice(h * X_HD, (h + 1) * X_HD)
            kh, vh = k_ref[:, sl], v_ref[:, sl]
            p = _attn_probs(q[:, sl], kh)
            dp = lax.dot_general(do[:, sl], vh, (((1,), (1,)), ((), ())), preferred_element_type=F32)
            ds = (p * (dp - jnp.sum(dp * p, axis=-1, keepdims=True)) * (X_HD ** -0.5)).astype(BF)
            dqs.append(jnp.dot(ds, kh, preferred_element_type=F32))
            dks.append(lax.dot_general(ds, q[:, sl], (((0,), (0,)), ((), ())), preferred_element_type=F32))
            dvs.append(lax.dot_general(p.astype(BF), do[:, sl], (((0,), (0,)), ((), ())),
                                       preferred_element_type=F32))
        dqb = jnp.concatenate(dqs, axis=1).astype(BF)
        dq_ref[...] = dqb
        dk = jnp.concatenate(dks, axis=1)
        dv = jnp.concatenate(dvs, axis=1)
        dwq = lax.dot_general(x_ref[...], dqb, (((0,), (0,)), ((), ())), preferred_element_type=F32)

        @pl.when(i == 0)
        def _():
            dk_ref[...] = dk
            dv_ref[...] = dv
            acc_ref[...] = dwq

        @pl.when(i > 0)
        def _():
            dk_ref[...] += dk
            dv_ref[...] += dv
            acc_ref[...] += dwq

        @pl.when(i == nt - 1)
        def _():
            dwq_ref[...] = acc_ref[...].astype(BF)

    row = lambda i: (i, 0)
    fix = lambda i: (0, 0)
    return pl.pallas_call(
        body, name=name, grid=(nt,),
        in_specs=[pl.BlockSpec((tm, D), row), pl.BlockSpec((D, D), fix), pl.BlockSpec((tm, D), row),
                  pl.BlockSpec((N_MEM, D), fix), pl.BlockSpec((N_MEM, D), fix), pl.BlockSpec((tm, D), row)],
        out_specs=[pl.BlockSpec((tm, D), row), pl.BlockSpec((N_MEM, D), fix), pl.BlockSpec((N_MEM, D), fix),
                   pl.BlockSpec((D, D), fix)],
        out_shape=[jax.ShapeDtypeStruct((T, D), BF), jax.ShapeDtypeStruct((N_MEM, D), F32),
                   jax.ShapeDtypeStruct((N_MEM, D), F32), jax.ShapeDtypeStruct((D, D), BF)],
        scratch_shapes=[pltpu.VMEM((D, D), F32)],
        compiler_params=_cp("arbitrary"))(dxa, wo, q, kb, vb, xb)


def swiglu_fwd(gu, *, tm, name):
    T = gu.shape[0]
    tm = min(tm, T)

    def body(g_ref, u_ref, o_ref):
        g = g_ref[...].astype(F32)
        o_ref[...] = (g * _sigmoid(g) * u_ref[...].astype(F32)).astype(BF)

    return pl.pallas_call(
        body, name=name, grid=(T // tm,),
        in_specs=[pl.BlockSpec((tm, D_FF), lambda i: (i, 0)), pl.BlockSpec((tm, D_FF), lambda i: (i, 1))],
        out_specs=pl.BlockSpec((tm, D_FF), lambda i: (i, 0)), out_shape=jax.ShapeDtypeStruct((T, D_FF), BF),
        compiler_params=_cp("arbitrary"))(gu, gu)


def swiglu_bwd(dh, gu, *, tm, name):
    T = gu.shape[0]
    tm = min(tm, T)

    def body(dh_ref, g_ref, u_ref, o_ref):
        g = g_ref[...].astype(F32)
        u = u_ref[...].astype(F32)
        dhv = dh_ref[...].astype(F32)
        sg = _sigmoid(g)
        o_ref[:, :D_FF] = (dhv * u * sg * (1.0 + g * (1.0 - sg))).astype(BF)
        o_ref[:, D_FF:] = (dhv * g * sg).astype(BF)

    return pl.pallas_call(
        body, name=name, grid=(T // tm,),
        in_specs=[pl.BlockSpec((tm, D_FF), lambda i: (i, 0)), pl.BlockSpec((tm, D_FF), lambda i: (i, 0)),
                  pl.BlockSpec((tm, D_FF), lambda i: (i, 1))],
        out_specs=pl.BlockSpec((tm, 2 * D_FF), lambda i: (i, 0)),
        out_shape=jax.ShapeDtypeStruct((T, 2 * D_FF), BF), compiler_params=_cp("arbitrary"))(dh, gu, gu)


TM_MM = 1024
TM_EW = 512
TM_SEQ = 512
TT_DW = 2048


def _mem_kv(l, memb, W):
    kb = mm_nn(memb, W['xa_w_k'][l], None, out_dtype=BF, tm=N_MEM, tn=1024, name=f"l{l}_mem_k")
    vb = mm_nn(memb, W['xa_w_v'][l], None, out_dtype=BF, tm=N_MEM, tn=1024, name=f"l{l}_mem_v")
    return kb, vb


def _layer_fwd(l, x, xb, kb, vb, W, host=None, host2=None, after_in_proj=None):
    n = f"l{l}_"
    sv = {'x0': x if xb is None else xb}
    z = mm_nn(sv['x0'], W['w_in'][l], W['b_in'][l], out_dtype=BF, tm=2 * TM_MM, tn=1024, name=n + "in_proj",
              host=host)
    if host is not None:
        z, sv['hosted'] = z
        if after_in_proj is not None:
            after_in_proj(sv['hosted'])
    if kb is None:
        kb, vb = _mem_kv(l, vb, W)
    sv['kb'], sv['vb'] = kb, vb
    yp = pool_fwd(z, W['pool_w'][l], tm=TM_SEQ, name=n + "pool_fwd")
    h, yl = lru_fwd(z, W['lru_conv_w'][l], W['lru_conv_b'][l], W['lru_w_r'][l], W['lru_b_r'][l], W['lru_w_i'][l],
                    W['lru_b_i'][l], W['lru_lambda'][l], W['lru_w_out'][l], tm=TM_SEQ, name=n + "lru_fwd")
    s, yc = sconv_fwd(z, W['sconv_w'][l], W['sconv_w_out'][l], tm=TM_SEQ, name=n + "sconv_fwd")
    merged = merge_fwd(z, yp, yl, yc, W['pool_scale'][l], tm=TM_EW, name=n + "merge_fwd")
    x1, x1b, xh1, rs1 = mm_res_ln(merged, W['w_mix_out'][l], x, W['ln_g'][l][0:1], W['ln_b'][l][0:1], tm=TM_EW,
                                  name=n + "mix_out_ln")
    q, o = attn_fwd(x1b, W['xa_w_q'][l], kb, vb, tm=TM_EW, name=n + "attn_fwd")
    x2, x2b, xh2, rs2 = mm_res_ln(o, W['xa_w_o'][l], x1, W['ln_g'][l][1:2], W['ln_b'][l][1:2], tm=TM_EW,
                                  name=n + "attn_out_ln")
    res = ffn_in_swiglu(x2b, W['ffn_w_gu'][l], tm=TM_MM, name=n + "ffn_in", host=host2)
    gu, hdn = res[:2]
    if host2 is not None:
        sv['hosted2'] = res[2]
    x3, x3b, xh3, rs3 = mm_res_ln(hdn, W['ffn_w_down'][l], x2, W['ln_g'][l][2:3], W['ln_b'][l][2:3], tm=TM_EW,
                                  name=n + "ffn_out_ln")
    sv.update(z=z, yp=yp, h=h, yl=yl, s=s, yc=yc, merged=merged, x1b=x1b, xh1=xh1, rs1=rs1, q=q, o=o, x2b=x2b,
              xh2=xh2, rs2=rs2, gu=gu, hdn=hdn, xh3=xh3, rs3=rs3)
    return x3, x3b, sv


def _layer_bwd(l, dx3, sv, memb, kb, vb, W, loss_from=None, host=None, last_host_fn=None):
    n = f"l{l}_"
    G = {}
    res = ln_bwd(dx3, sv['xh3'], sv['rs3'], W['ln_g'][l][2:3], tm=TM_EW, name=n + "ln3_bwd", loss_from=loss_from,
                 dw_of=sv['hdn'])
    dp3, dp3b, dg3, db3 = res[:4]
    loss = res[4] if loss_from is not None else None
    G['ffn_w_down'] = res[-1]
    dgu = ffn_down_dx_swiglu(dp3b, W['ffn_w_down'][l], sv['gu'], tm=TM_EW, name=n + "ffn_down_dx")
    dx2 = mm_nt(dgu, W['ffn_w_gu'][l], dp3, out_dtype=F32, tm=TM_MM, tc=1408, name=n + "ffn_in_dx", host=host)
    if host is not None:
        dx2, G['hosted'] = dx2
    G['ffn_w_gu'] = mm_tn(sv['x2b'], dgu, out_dtype=BF, tk=1024, tn=1408, tt=TT_DW,name=n + "ffn_in_dw")

    dp2, dp2b, dg2, db2, G['xa_w_o'] = ln_bwd(dx2, sv['xh2'], sv['rs2'], W['ln_g'][l][1:2], tm=TM_EW,
                                              name=n + "ln2_bwd", dw_of=sv['o'])
    dq, dk, dv, G['xa_w_q'] = attn_bwd(dp2b, W['xa_w_o'][l], sv['q'], kb, vb, sv['x1b'], tm=TM_EW,
                                       name=n + "attn_bwd")
    dx1 = mm_nt(dq, W['xa_w_q'][l], dp2, out_dtype=F32, tm=TM_MM, tc=D, name=n + "attn_q_dx")
    G['xa_w_k'] = mm_tn(memb, dk, out_dtype=BF, tk=1024, tn=1024, tt=N_MEM, name=n + "attn_k_dw")
    G['xa_w_v'] = mm_tn(memb, dv, out_dtype=BF, tk=1024, tn=1024, tt=N_MEM, name=n + "attn_v_dw")

    dp1, dp1b, dg1, db1, G['w_mix_out'] = ln_bwd(dx1, sv['xh1'], sv['rs1'], W['ln_g'][l][0:1], tm=TM_EW,
                                                 name=n + "ln1_bwd", dw_of=sv['merged'])
    dmerged = mm_nt(dp1b, W['w_mix_out'][l], None, out_dtype=BF, tm=TM_MM, tc=D, name=n + "mix_out_dx")
    z = sv['z']
    dz, dyp, dyl, dyc = merge_bwd(dmerged, z, sv['yp'], sv['yl'], sv['yc'], W['pool_scale'][l], tm=TM_EW,
                                  name=n + "merge_bwd")
    dz, G['pool_w'], G['pool_scale'] = pool_bwd(dz, dyp, sv['yp'], z, W['pool_w'][l], W['pool_scale'][l],
                                                tm=TM_SEQ, name=n + "pool_bwd")
    (dz, G['lru_conv_w'], G['lru_conv_b'], G['lru_w_r'], G['lru_b_r'], G['lru_w_i'], G['lru_b_i'], G['lru_lambda'],
     G['lru_w_out']) = lru_bwd(dz, dyl, z, sv['h'], W['lru_conv_w'][l], W['lru_conv_b'][l], W['lru_w_r'][l],
                               W['lru_b_r'][l], W['lru_w_i'][l], W['lru_b_i'][l], W['lru_lambda'][l],
                               W['lru_w_out'][l], tm=TM_SEQ, name=n + "lru_bwd")
    dz, G['sconv_w'], G['sconv_w_out'] = sconv_bwd(dz, dyc, z, W['sconv_w'][l], W['sconv_w_out'][l], tm=TM_SEQ,
                                                   name=n + "sconv_bwd")
    G['w_in'], G['b_in'] = mm_tn(sv['x0'], dz, out_dtype=BF, tk=1024, tn=1024, tt=TT_DW, name=n + "in_proj_dw",
                                 colsum=True)
    G['ln_g'] = jnp.concatenate([dg1, dg2, dg3], axis=0)
    G['ln_b'] = jnp.concatenate([db1, db2, db3], axis=0)
    last_host = None if last_host_fn is None else last_host_fn(G)
    dx0 = mm_nt(dz, W['w_in'][l], dp1, out_dtype=F32, tm=TM_MM, tc=2048, name=n + "in_proj_dx", host=last_host)
    if last_host is not None:
        dx0, G['hosted_last'] = dx0
    return dx0, G, loss


def local_step(x, mem, target, W):
    memb = mem.astype(BF)
    saves, kvs = [], []
    xf, xb = x, None
    for l in range(DEPTH):
        kb = mm_nn(memb, W['xa_w_k'][l], None, out_dtype=BF, tm=N_MEM, tn=1024, name=f"l{l}_mem_k")
        vb = mm_nn(memb, W['xa_w_v'][l], None, out_dtype=BF, tm=N_MEM, tn=1024, name=f"l{l}_mem_v")
        xf, xb, sv = _layer_fwd(l, xf, xb, kb, vb, W)
        saves.append(sv)
        kvs.append((kb, vb))
    grads = [None] * DEPTH
    dx, loss = None, None
    for l in reversed(range(DEPTH)):
        lf = (W['ln_b'][l][2:3], target) if l == DEPTH - 1 else None
        dx, grads[l], ls = _layer_bwd(l, dx, saves[l], memb, kvs[l][0], kvs[l][1], W, loss_from=lf)
        if ls is not None:
            loss = ls
    return loss, dx, grads


def _coords():
    return lax.axis_index("x"), lax.axis_index("y"), lax.axis_index("c")


FLIPS = ((1, 0), (0, 1), (1, 1))
SQUARES = ('lru_w_out', 'sconv_w_out', 'w_mix_out', 'xa_w_q', 'xa_w_k', 'xa_w_v', 'xa_w_o')
LAYER_SHAPE = {'w_in': (D, IN_COLS), 'pool_w': (4, POOL_GD, POOL_GD), 'ffn_w_gate': (4, D, D_FF // 4),
               'ffn_w_up': (4, D, D_FF // 4), 'ffn_w_down': (D_FF, D), **{n: (D, D) for n in SQUARES}}
PIECES = ('w_in', 'pool_w') + SQUARES + ('ffn_w_gate', 'ffn_w_up', 'ffn_w_down')


def _mult(v, m):
    return v if isinstance(v, int) else pl.multiple_of(v, m)


def _win(name, ref, k):
    if name == 'w_in':
        return ref.at[:, pl.ds(_mult(((2 * k + 6) % 8) * D, D), 2 * D)]
    if name == 'pool_w':
        return ref.at[:, pl.ds(_mult(k * (POOL_GD // 4), POOL_GD // 4), POOL_GD // 4), :]
    if name in ('ffn_w_gate', 'ffn_w_up'):
        return ref.at[k]
    rows = LAYER_SHAPE[name][0] // 4
    return ref.at[pl.ds(_mult(k * rows, 16), rows), :]


def _half_shape(name):
    shard = _shard_shape(name)
    return (shard[0] // 2,) + shard[1:]


def _half(name, ref, h):
    rows = _shard_shape(name)[0] // 2
    if name == 'pool_w':
        return ref.at[pl.ds(h * rows, rows)]
    return ref.at[pl.ds(_mult(h * rows, 16), rows), :]


def gather_weights(shards, small):
    n_p = len(PIECES)

    def body(*refs):
        srcs = dict(zip(PIECES, refs[:n_p]))
        small_ref = refs[n_p]
        outs = [dict(zip(PIECES, refs[n_p + 1 + l * n_p:n_p + 1 + (l + 1) * n_p])) for l in range(DEPTH)]
        gs_ref = refs[n_p + 1 + DEPTH * n_p]
        ici_send, ici_recv, d2d_send, d2d_recv, own_send, own_recv = refs[n_p + 2 + DEPTH * n_p:]
        x, y, c = _coords()
        me = 2 * x + y
        sib = (x, y, 1 - c)

        def own_copies():
            cps = []
            li = 0
            for n in PIECES:
                for l in range(DEPTH):
                    cps.append(pltpu.make_async_remote_copy(
                        src_ref=srcs[n].at[l], dst_ref=_win(n, outs[l][n], me), send_sem=own_send.at[li],
                        recv_sem=own_recv.at[li], device_id=sib, device_id_type=MESH))
                    li += 1
            cps.append(pltpu.make_async_remote_copy(
                src_ref=small_ref, dst_ref=gs_ref.at[me], send_sem=own_send.at[li], recv_sem=own_recv.at[li],
                device_id=sib, device_id_type=MESH))
            return cps

        def run(lc):
            sends = []
            for j, (fx, fy) in enumerate(FLIPS):
                peer = (x ^ fx, y ^ fy, c)
                for p, n in enumerate(PIECES):
                    k = 3 * p + j
                    sends.append(pltpu.make_async_remote_copy(
                        src_ref=srcs[n].at[lc], dst_ref=_win(n, outs[lc][n], me), send_sem=ici_send.at[k],
                        recv_sem=ici_recv.at[k], device_id=peer, device_id_type=MESH))
                k = 3 * n_p + j
                sends.append(pltpu.make_async_remote_copy(
                    src_ref=small_ref, dst_ref=gs_ref.at[me], send_sem=ici_send.at[k], recv_sem=ici_recv.at[k],
                    device_id=peer, device_id_type=MESH))
            own = own_copies()
            for cp in sends + own:
                cp.start()
            for j, (fx, fy) in enumerate(FLIPS):
                other = 2 * (x ^ fx) + (y ^ fy)
                for p, n in enumerate(PIECES):
                    k = 3 * p + j
                    w = _win(n, outs[lc][n], other)
                    pltpu.make_async_remote_copy(src_ref=srcs[n].at[lc], dst_ref=w, send_sem=ici_send.at[k],
                                                 recv_sem=ici_recv.at[k], device_id=sib,
                                                 device_id_type=MESH).wait_recv()
                    fw = pltpu.make_async_remote_copy(src_ref=w, dst_ref=w, send_sem=d2d_send.at[k],
                                                      recv_sem=d2d_recv.at[k], device_id=sib, device_id_type=MESH)
                    fw.start()
                    sends.append(fw)
                k = 3 * n_p + j
                pltpu.make_async_remote_copy(src_ref=small_ref, dst_ref=gs_ref.at[other], send_sem=ici_send.at[k],
                                             recv_sem=ici_recv.at[k], device_id=sib, device_id_type=MESH).wait_recv()
            for j, (fx, fy) in enumerate(FLIPS):
                other = 2 * (x ^ fx) + (y ^ fy)
                for p, n in enumerate(PIECES):
                    k = 3 * p + j
                    w = _win(n, outs[1 - lc][n], other)
                    pltpu.make_async_remote_copy(src_ref=w, dst_ref=w, send_sem=d2d_send.at[k],
                                                 recv_sem=d2d_recv.at[k], device_id=sib,
                                                 device_id_type=MESH).wait_recv()
            for cp in sends:
                cp.wait_send()
            for cp in own:
                cp.wait()

        @pl.when(c == 0)
        def _():
            run(0)

        @pl.when(c == 1)
        def _():
            run(1)

    hbm = pl.BlockSpec(memory_space=pl.ANY)
    n_out = DEPTH * n_p + 1
    res = pl.pallas_call(
        body, name="gather_weights", in_specs=[hbm] * (n_p + 1), out_specs=[hbm] * n_out,
        out_shape=[jax.ShapeDtypeStruct(LAYER_SHAPE[n], BF) for _ in range(DEPTH) for n in PIECES]
        + [jax.ShapeDtypeStruct((4,) + small.shape, small.dtype)],
        scratch_shapes=[pltpu.SemaphoreType.DMA((3 * n_p + 3,)), pltpu.SemaphoreType.DMA((3 * n_p + 3,)),
                        pltpu.SemaphoreType.DMA((3 * n_p,)), pltpu.SemaphoreType.DMA((3 * n_p,)),
                        pltpu.SemaphoreType.DMA((DEPTH * n_p + 1,)), pltpu.SemaphoreType.DMA((DEPTH * n_p + 1,))],
    )(*[shards[n] for n in PIECES], small)
    full = {n: [res[l * n_p + p] for l in range(DEPTH)] for p, n in enumerate(PIECES)}
    return full, res[DEPTH * n_p]


def split_layers(p0, p1):
    n = len(p0)

    def body(*refs):
        a0, a1 = refs[:n], refs[n:2 * n]
        theirs = refs[2 * n:3 * n]
        send_sems, recv_sems = refs[3 * n:]
        x, y, c = _coords()
        sib = (x, y, 1 - c)

        def give(arrs):
            for i in range(n):
                pltpu.make_async_remote_copy(src_ref=arrs[i], dst_ref=theirs[i], send_sem=send_sems.at[i],
                                             recv_sem=recv_sems.at[i], device_id=sib, device_id_type=MESH).start()

        @pl.when(c == 0)
        def _():
            give(a1)

        @pl.when(c == 1)
        def _():
            give(a0)

        for i in range(n):
            pltpu.make_async_remote_copy(src_ref=a0[i], dst_ref=theirs[i], send_sem=send_sems.at[i],
                                         recv_sem=recv_sems.at[i], device_id=sib, device_id_type=MESH).wait()

    hbm = pl.BlockSpec(memory_space=pl.ANY)
    return pl.pallas_call(
        body, name="split_layers", in_specs=[hbm] * (2 * n), out_specs=[hbm] * n,
        out_shape=[jax.ShapeDtypeStruct(a.shape, a.dtype) for a in p0],
        scratch_shapes=[pltpu.SemaphoreType.DMA((n,)), pltpu.SemaphoreType.DMA((n,))],
    )(*p0, *p1)


def exchange_chips(q, qsmall):
    n_p = len(PIECES)

    def body(*refs):
        srcs = dict(zip(PIECES, refs[:n_p]))
        s_ref = refs[n_p]
        outs = dict(zip(PIECES, refs[n_p + 1:2 * n_p + 1]))
        so_ref = refs[2 * n_p + 1]
        send_sems, recv_sems = refs[2 * n_p + 2:]
        x, y, c = _coords()
        me = 2 * x + y
        sends = []
        for j, (fx, fy) in enumerate(FLIPS):
            peer = (x ^ fx, y ^ fy, c)
            other = 2 * (x ^ fx) + (y ^ fy)
            for p, n in enumerate(PIECES):
                k = 3 * p + j
                sends.append(pltpu.make_async_remote_copy(
                    src_ref=_win(n, srcs[n], other), dst_ref=outs[n].at[j], send_sem=send_sems.at[k],
                    recv_sem=recv_sems.at[k], device_id=peer, device_id_type=MESH))
            k = 3 * n_p + j
            sends.append(pltpu.make_async_remote_copy(
                src_ref=s_ref, dst_ref=so_ref.at[j], send_sem=send_sems.at[k], recv_sem=recv_sems.at[k],
                device_id=peer, device_id_type=MESH))
        for cp in sends:
            cp.start()
        for cp in sends:
            cp.wait()

    hbm = pl.BlockSpec(memory_space=pl.ANY)
    res = pl.pallas_call(
        body, name="exchange_chips", in_specs=[hbm] * (n_p + 1), out_specs=[hbm] * (n_p + 1),
        out_shape=[jax.ShapeDtypeStruct((3,) + _shard_shape(n), BF) for n in PIECES]
        + [jax.ShapeDtypeStruct((3,) + qsmall.shape, qsmall.dtype)],
        scratch_shapes=[pltpu.SemaphoreType.DMA((3 * n_p + 3,)), pltpu.SemaphoreType.DMA((3 * n_p + 3,))],
    )(*[q[n] for n in PIECES], qsmall)
    return dict(zip(PIECES, res[:n_p])), res[n_p]


def _shard_shape(n):
    shp = LAYER_SHAPE[n]
    if n == 'w_in':
        return (shp[0], shp[1] // 4)
    if n == 'pool_w':
        return (shp[0], shp[1] // 4, shp[2])
    if n in ('ffn_w_gate', 'ffn_w_up'):
        return shp[1:]
    return (shp[0] // 4, shp[1])


def swap_cores(s):
    n = len(s)

    def body(*refs):
        srcs, outs = refs[:n], refs[n:2 * n]
        send_sems, recv_sems = refs[2 * n:]
        x, y, c = _coords()
        cps = [pltpu.make_async_remote_copy(src_ref=srcs[i], dst_ref=outs[i], send_sem=send_sems.at[i],
                                            recv_sem=recv_sems.at[i], device_id=(x, y, 1 - c), device_id_type=MESH)
               for i in range(n)]
        for cp in cps:
            cp.start()
        for cp in cps:
            cp.wait()

    hbm = pl.BlockSpec(memory_space=pl.ANY)
    return pl.pallas_call(
        body, name="swap_cores", in_specs=[hbm] * n, out_specs=[hbm] * n,
        out_shape=[jax.ShapeDtypeStruct(a.shape, a.dtype) for a in s],
        scratch_shapes=[pltpu.SemaphoreType.DMA((n,)), pltpu.SemaphoreType.DMA((n,))],
    )(*s)


def add_cores(p0, p1, theirs, *, out_dtype, name):
    shp = p0.shape
    args = [t.reshape(-1, shp[-1]) for t in (p0, p1, theirs)]
    R, C = args[0].shape
    tr = _row_tile(R, C)

    def body(a0_ref, a1_ref, t_ref, o_ref):
        c = lax.axis_index("c")
        t = t_ref[...].astype(F32)

        @pl.when(c == 0)
        def _():
            o_ref[...] = (a0_ref[...].astype(F32) + t).astype(o_ref.dtype)

        @pl.when(c == 1)
        def _():
            o_ref[...] = (a1_ref[...].astype(F32) + t).astype(o_ref.dtype)

    spec = pl.BlockSpec((tr, C), lambda i: (i, 0))
    out = pl.pallas_call(body, name=name, grid=(R // tr,), in_specs=[spec] * 3, out_specs=spec,
                         out_shape=jax.ShapeDtypeStruct((R, C), out_dtype), compiler_params=_cp("arbitrary"))(*args)
    return out.reshape(shp)


def sum_chips(name, q, recv3, chip):
    shard = _shard_shape(name)
    zero = (0,) * len(shard)
    if name == 'w_in':
        tr = 128
        grid = (shard[0] // tr,)
        qspec = pl.BlockSpec((tr, shard[1]), lambda i, me: (i, (me[0] + 3) % 4))
        rspec = pl.BlockSpec((3, tr, shard[1]), lambda i, me: (0, i, 0))
        ospec = pl.BlockSpec((tr, shard[1]), lambda i, me: (i, 0))
    else:
        grid = (1,)
        rspec = pl.BlockSpec((3,) + shard, lambda i, me: (0,) + zero)
        ospec = pl.BlockSpec(shard, lambda i, me: zero)
        if name == 'pool_w':
            qspec = pl.BlockSpec(shard, lambda i, me: (0, me[0], 0))
        elif name in ('ffn_w_gate', 'ffn_w_up'):
            qspec = pl.BlockSpec((None,) + shard, lambda i, me: (me[0], 0, 0))
        else:
            qspec = pl.BlockSpec(shard, lambda i, me: (me[0], 0))

    def body(me_ref, q_ref, r_ref, o_ref):
        del me_ref
        acc = q_ref[...].astype(F32)
        for j in range(3):
            acc = acc + r_ref[j].astype(F32)
        o_ref[...] = acc

    return pl.pallas_call(
        body, name="sum_chips_" + name,
        grid_spec=pltpu.PrefetchScalarGridSpec(num_scalar_prefetch=1, grid=grid, in_specs=[qspec, rspec],
                                               out_specs=ospec),
        out_shape=jax.ShapeDtypeStruct(shard, F32), compiler_params=_cp("arbitrary"))(chip, q, recv3)


def sum_chips_small(q, recv3, chip):
    r, C = q.shape
    slot_of_xor = {2: 0, 1: 1, 3: 2}

    def body(me_ref, q_ref, r_ref, o_ref):
        me = me_ref[0]
        acc = None
        for k in range(4):
            kx = k ^ me
            term = q_ref[...]
            for xv, j in slot_of_xor.items():
                term = jnp.where(kx == xv, r_ref[j], term)
            acc = term if acc is None else acc + term
        o_ref[...] = acc

    return pl.pallas_call(
        body, name="sum_chips_small",
        grid_spec=pltpu.PrefetchScalarGridSpec(
            num_scalar_prefetch=1, grid=(1,),
            in_specs=[pl.BlockSpec((r, C), lambda i, me: (0, 0)), pl.BlockSpec((3, r, C), lambda i, me: (0, 0, 0))],
            out_specs=pl.BlockSpec((r, C), lambda i, me: (0, 0))),
        out_shape=jax.ShapeDtypeStruct((r, C), F32), compiler_params=_cp("arbitrary"))(chip, q, recv3)


def _row_tile(R, C):
    for cand in (1024, 512, 256, 128, 64, 32, 16):
        if R % cand == 0 and cand * C * 4 <= 2 * 1024 * 1024:
            return cand
    return R


def sum_slots(a, *, name):
    n = a.shape[0]
    shp = a.shape[1:]
    a3 = a.reshape(n, -1, shp[-1])
    R, C = a3.shape[1:]
    tr = _row_tile(R, C * n // 2)

    def body(a_ref, o_ref):
        acc = a_ref[0].astype(F32)
        for k in range(1, n):
            acc = acc + a_ref[k].astype(F32)
        o_ref[...] = acc

    out = pl.pallas_call(
        body, name=name, grid=(R // tr,), in_specs=[pl.BlockSpec((n, tr, C), lambda i: (0, i, 0))],
        out_specs=pl.BlockSpec((tr, C), lambda i: (i, 0)), out_shape=jax.ShapeDtypeStruct((R, C), F32),
        compiler_params=_cp("arbitrary"))(a3)
    return out.reshape(shp)


def _adamw_math(w, g, m, v):
    mn = ADAM_B1 * m + (1.0 - ADAM_B1) * g
    vn = ADAM_B2 * v + (1.0 - ADAM_B2) * (g * g)
    m_hat = mn / (1.0 - ADAM_B1 ** ADAM_STEP)
    v_hat = vn / (1.0 - ADAM_B2 ** ADAM_STEP)
    return -ADAM_LR * (m_hat / (jnp.sqrt(v_hat) + ADAM_EPS) + ADAM_WD * w), mn, vn


def adamw(w, g, m, v, *, name):
    shp = w.shape
    args = [t.reshape(-1, shp[-1]) for t in (w, g, m, v)]
    R, C = args[0].shape
    tr = _row_tile(R, C)

    def body(w_ref, g_ref, m_ref, v_ref, d_ref, mo_ref, vo_ref):
        d_ref[...], mo_ref[...], vo_ref[...] = _adamw_math(w_ref[...], g_ref[...], m_ref[...], v_ref[...])

    spec = pl.BlockSpec((tr, C), lambda i: (i, 0))
    res = pl.pallas_call(
        body, name=name, grid=(R // tr,), in_specs=[spec] * 4, out_specs=[spec] * 3,
        out_shape=[jax.ShapeDtypeStruct((R, C), F32)] * 3, compiler_params=_cp("arbitrary"))(*args)
    return [r.reshape(shp) for r in res]


def adamw_layers(w, g_mine, g_theirs, m, v, *, name):
    shp = w.shape
    three = (DEPTH, -1, shp[-1])
    w3, m3, v3 = [t.reshape(three) for t in (w, m, v)]
    ga, gb = [t.reshape(-1, shp[-1]) for t in (g_mine, g_theirs)]
    R, C = ga.shape
    tr = _row_tile(R, C)

    def body(w_ref, ga_ref, gb_ref, m_ref, v_ref, g_ref, d_ref, mo_ref, vo_ref):
        mine = pl.program_id(0) == lax.axis_index("c")
        g = jnp.where(mine, ga_ref[...], gb_ref[...])
        g_ref[...] = g
        d_ref[...], mo_ref[...], vo_ref[...] = _adamw_math(w_ref[...], g, m_ref[...], v_ref[...])

    lay = pl.BlockSpec((None, tr, C), lambda l, i: (l, i, 0))
    one = pl.BlockSpec((tr, C), lambda l, i: (i, 0))
    res = pl.pallas_call(
        body, name=name, grid=(DEPTH, R // tr), in_specs=[lay, one, one, lay, lay], out_specs=[lay] * 4,
        out_shape=[jax.ShapeDtypeStruct(w3.shape, F32)] * 4, compiler_params=_cp("arbitrary", "arbitrary"))(
            w3, ga, gb, m3, v3)
    return [r.reshape(shp) for r in res]


def _remote(src, dst, send_sems, recv_sems, k, peer):
    return pltpu.make_async_remote_copy(src_ref=src, dst_ref=dst, send_sem=send_sems.at[k], recv_sem=recv_sems.at[k],
                                        device_id=peer, device_id_type=MESH)


def gather_layer(l, shards, small=None, pieces=PIECES):
    n_p = len(pieces)
    with_small = small is not None

    def descs(h_in, h_out, sems):
        srcs = dict(zip(pieces, h_in[:n_p]))
        outs = dict(zip(pieces, h_out[:n_p]))
        ici_s, ici_r, d2d_s, d2d_r, own_s, own_r = sems
        x, y, c = _coords()
        me = 2 * x + y
        sib = (x, y, 1 - c)
        ici, fwd, fwd_in, own = [], [], [], []
        for j, (fx, fy) in enumerate(FLIPS):
            peer = (x ^ fx, y ^ fy, c)
            other = 2 * (x ^ fx) + (y ^ fy)
            for p, n in enumerate(pieces):
                k = 3 * p + j
                mine = _half(n, _win(n, outs[n], me), c)
                landed = _half(n, _win(n, outs[n], other), c)
                sib_half = _half(n, _win(n, outs[n], other), 1 - c)
                ici.append((_remote(_half(n, srcs[n].at[l], c), mine, ici_s, ici_r, k, peer),
                            _remote(_half(n, srcs[n].at[l], c), landed, ici_s, ici_r, k, peer)))
                fwd.append(_remote(landed, landed, d2d_s, d2d_r, k, sib))
                fwd_in.append(_remote(sib_half, sib_half, d2d_s, d2d_r, k, sib))
            if with_small:
                k = 3 * n_p + j
                ici.append((_remote(h_in[n_p], h_out[n_p].at[me], ici_s, ici_r, k, peer),
                            _remote(h_in[n_p], h_out[n_p].at[other], ici_s, ici_r, k, peer)))
        for p, n in enumerate(pieces):
            own.append(_remote(srcs[n].at[l], _win(n, outs[n], me), own_s, own_r, p, sib))
        if with_small:
            own.append(_remote(h_in[n_p], h_out[n_p].at[me], own_s, own_r, n_p, sib))
        return ici, fwd, fwd_in, own

    def start(h_in, h_out, sems):
        ici, _, _, own = descs(h_in, h_out, sems)
        for send, _ in ici:
            send.start()
        for cp in own:
            cp.start()

    def finish(h_in, h_out, sems):
        ici, fwd, fwd_in, own = descs(h_in, h_out, sems)
        per_chip = n_p + (1 if with_small else 0)
        for j in range(3):
            for p in range(n_p):
                ici[j * per_chip + p][1].wait_recv()
                fwd[j * n_p + p].start()
            if with_small:
                ici[j * per_chip + n_p][1].wait_recv()
        for cp in fwd_in:
            cp.wait_recv()
        for send, _ in ici:
            send.wait_send()
        for cp in fwd:
            cp.wait_send()
        for cp in own:
            cp.wait()

    arrays = [shards[n] for n in pieces] + ([small] if with_small else [])
    out_shape = [jax.ShapeDtypeStruct(LAYER_SHAPE[n], BF) for n in pieces]
    if with_small:
        out_shape.append(jax.ShapeDtypeStruct((4,) + small.shape, small.dtype))
    sems = [pltpu.SemaphoreType.DMA((3 * n_p + 3,)), pltpu.SemaphoreType.DMA((3 * n_p + 3,)),
            pltpu.SemaphoreType.DMA((3 * n_p,)), pltpu.SemaphoreType.DMA((3 * n_p,)),
            pltpu.SemaphoreType.DMA((n_p + 1,)), pltpu.SemaphoreType.DMA((n_p + 1,))]
    return Hosted(arrays, out_shape, sems, start, finish)


def both_hosted(h1, h2):
    a1, o1, s1 = len(h1.arrays), len(h1.out_shape), len(h1.sems)

    def start(h_in, h_out, sems):
        h1.start(h_in[:a1], h_out[:o1], sems[:s1])
        h2.start(h_in[a1:], h_out[o1:], sems[s1:])

    def finish(h_in, h_out, sems):
        h1.finish(h_in[:a1], h_out[:o1], sems[:s1])
        h2.finish(h_in[a1:], h_out[o1:], sems[s1:])

    return Hosted(list(h1.arrays) + list(h2.arrays), list(h1.out_shape) + list(h2.out_shape),
                  list(h1.sems) + list(h2.sems), start, finish)


def run_hosted(host, name):
    n_in, n_out = len(host.arrays), len(host.out_shape)

    def body(*refs):
        h_in, h_out, sems = refs[:n_in], refs[n_in:n_in + n_out], refs[n_in + n_out:]
        host.start(h_in, h_out, sems)
        host.finish(h_in, h_out, sems)

    return pl.pallas_call(body, name=name, in_specs=[HBM_SPEC] * n_in, out_specs=[HBM_SPEC] * n_out,
                          out_shape=list(host.out_shape), scratch_shapes=list(host.sems))(*host.arrays)


def split_halves(parts, small):
    n_p = len(PIECES)
    rh = small.shape[0] // 2

    def body(*refs):
        srcs = dict(zip(PIECES, refs[:n_p]))
        s_ref = refs[n_p]
        outs = dict(zip(PIECES, refs[n_p + 1:2 * n_p + 1]))
        so_ref = refs[2 * n_p + 1]
        send_sems, recv_sems = refs[2 * n_p + 2:]
        x, y, c = _coords()
        sib = (x, y, 1 - c)
        cps = []
        for p, n in enumerate(PIECES):
            for k in range(4):
                cps.append(_remote(_half(n, _win(n, srcs[n], k), 1 - c), outs[n].at[k], send_sems, recv_sems,
                                   4 * p + k, sib))
        cps.append(_remote(s_ref.at[pl.ds(_mult((1 - c) * rh, 8), rh), :], so_ref, send_sems, recv_sems, 4 * n_p, sib))
        for cp in cps:
            cp.start()
        for cp in cps:
            cp.wait()

    res = pl.pallas_call(
        body, name="split_halves", in_specs=[HBM_SPEC] * (n_p + 1), out_specs=[HBM_SPEC] * (n_p + 1),
        out_shape=[jax.ShapeDtypeStruct((4,) + _half_shape(n), BF) for n in PIECES]
        + [jax.ShapeDtypeStruct((rh, small.shape[1]), small.dtype)],
        scratch_shapes=[pltpu.SemaphoreType.DMA((4 * n_p + 1,)), pltpu.SemaphoreType.DMA((4 * n_p + 1,))],
    )(*[parts[n] for n in PIECES], small)
    return dict(zip(PIECES, res[:n_p])), res[n_p]


SAME_SHAPE = (('w_in',), ('pool_w',), SQUARES, ('ffn_w_gate', 'ffn_w_up'), ('ffn_w_down',))


def add_halves(names, parts, theirs, core):
    name = names[0]
    half = _half_shape(name)
    zero = (0,) * len(half)
    if name == 'w_in':
        pspec = pl.BlockSpec(half, lambda k, cc: (cc[0], (k + 3) % 4))
    elif name == 'pool_w':
        pspec = pl.BlockSpec(half, lambda k, cc: (cc[0], k, 0))
    elif name in ('ffn_w_gate', 'ffn_w_up'):
        pspec = pl.BlockSpec((None,) + half, lambda k, cc: (k, cc[0], 0))
    else:
        pspec = pl.BlockSpec(half, lambda k, cc: (2 * k + cc[0], 0))
    slot = pl.BlockSpec((None,) + half, lambda k, cc: (k,) + zero)
    m = len(names)

    def body(cc_ref, *refs):
        del cc_ref
        for i in range(m):
            refs[2 * m + i][...] = (refs[i][...].astype(F32) + refs[m + i][...].astype(F32)).astype(BF)

    res = pl.pallas_call(
        body, name="add_cores_" + name,
        grid_spec=pltpu.PrefetchScalarGridSpec(num_scalar_prefetch=1, grid=(4,), in_specs=[pspec] * m + [slot] * m,
                                               out_specs=[slot] * m),
        out_shape=[jax.ShapeDtypeStruct((4,) + half, BF)] * m, compiler_params=_cp("arbitrary"))(
            core, *[parts[n] for n in names], *[theirs[n] for n in names])
    return dict(zip(names, res))


def add_halves_small(small, theirs, core):
    rh, C = theirs.shape

    def body(cc_ref, p_ref, t_ref, o_ref):
        del cc_ref
        o_ref[...] = p_ref[...] + t_ref[...]

    blk = pl.BlockSpec((rh, C), lambda i, cc: (0, 0))
    return pl.pallas_call(
        body, name="add_cores_small",
        grid_spec=pltpu.PrefetchScalarGridSpec(
            num_scalar_prefetch=1, grid=(1,),
            in_specs=[pl.BlockSpec((rh, C), lambda i, cc: (cc[0], 0)), blk], out_specs=blk),
        out_shape=jax.ShapeDtypeStruct((rh, C), F32), compiler_params=_cp("arbitrary"))(core, small, theirs)


def exchange_halves(q, qsmall):
    n_p = len(PIECES)

    def descs(h_in, h_out, sems):
        send_sems, recv_sems = sems
        x, y, c = _coords()
        cps = []
        for j, (fx, fy) in enumerate(FLIPS):
            peer = (x ^ fx, y ^ fy, c)
            other = 2 * (x ^ fx) + (y ^ fy)
            for p in range(n_p):
                cps.append(_remote(h_in[p].at[other], h_out[p].at[j], send_sems, recv_sems, 3 * p + j, peer))
            cps.append(_remote(h_in[n_p], h_out[n_p].at[j], send_sems, recv_sems, 3 * n_p + j, peer))
        return cps

    def start(h_in, h_out, sems):
        for cp in descs(h_in, h_out, sems):
            cp.start()

    def finish(h_in, h_out, sems):
        for cp in descs(h_in, h_out, sems):
            cp.wait()

    arrays = [q[n] for n in PIECES] + [qsmall]
    out_shape = [jax.ShapeDtypeStruct((3,) + _half_shape(n), BF) for n in PIECES]
    out_shape.append(jax.ShapeDtypeStruct((3,) + qsmall.shape, qsmall.dtype))
    sems = [pltpu.SemaphoreType.DMA((3 * n_p + 3,)), pltpu.SemaphoreType.DMA((3 * n_p + 3,))]
    return Hosted(arrays, out_shape, sems, start, finish)


def sum_halves(names, q, recv3, chip):
    half = _half_shape(names[0])
    zero = (0,) * len(half)
    m = len(names)

    def body(me_ref, *refs):
        del me_ref
        for i in range(m):
            acc = refs[i][...].astype(F32)
            for j in range(3):
                acc = acc + refs[m + i][j].astype(F32)
            refs[2 * m + i][...] = acc

    res = pl.pallas_call(
        body, name="sum_chips_" + names[0],
        grid_spec=pltpu.PrefetchScalarGridSpec(
            num_scalar_prefetch=1, grid=(1,),
            in_specs=[pl.BlockSpec((None,) + half, lambda i, me: (me[0],) + zero)] * m
            + [pl.BlockSpec((3,) + half, lambda i, me: (0,) + zero)] * m,
            out_specs=[pl.BlockSpec(half, lambda i, me: zero)] * m),
        out_shape=[jax.ShapeDtypeStruct(half, F32)] * m, compiler_params=_cp("arbitrary"))(
            chip, *[q[n] for n in names], *[recv3[n] for n in names])
    return dict(zip(names, res))


def adamw_halves(w, g0, m, v, *, name):
    shp = w.shape
    C = shp[-1]
    four = (DEPTH, 2, -1, C)
    w4, m4, v4 = [t.reshape(four) for t in (w, m, v)]
    gs = [t.reshape(-1, C) for pair in g0 for t in pair]
    Rh = gs[0].shape[0]
    tr = _row_tile(Rh, C)

    def body(w_ref, a0_ref, b0_ref, a1_ref, b1_ref, m_ref, v_ref, g_ref, d_ref, mo_ref, vo_ref):
        mine = pl.program_id(1) == lax.axis_index("c")
        g_l0 = jnp.where(mine, a0_ref[...], b0_ref[...])
        g_l1 = jnp.where(mine, a1_ref[...], b1_ref[...])
        g = jnp.where(pl.program_id(0) == 0, g_l0, g_l1)
        g_ref[...] = g
        d_ref[...], mo_ref[...], vo_ref[...] = _adamw_math(w_ref[...], g, m_ref[...], v_ref[...])

    lay = pl.BlockSpec((None, None, tr, C), lambda l, h, i: (l, h, i, 0))
    one = pl.BlockSpec((tr, C), lambda l, h, i: (i, 0))
    res = pl.pallas_call(
        body, name=name, grid=(DEPTH, 2, Rh // tr), in_specs=[lay, one, one, one, one, lay, lay],
        out_specs=[lay] * 4, out_shape=[jax.ShapeDtypeStruct(w4.shape, F32)] * 4,
        compiler_params=_cp("arbitrary", "arbitrary", "arbitrary"))(w4, *gs, m4, v4)
    return [r.reshape(shp) for r in res]


def _local_shape(name, full_shape):
    shp = list(full_shape)
    ax = BIG_SHARDED.get(name, SMALL_SHARDED.get(name))
    if ax is not None:
        shp[ax] //= 4
    return tuple(shp)


FULL_SHAPES = {
    'w_in': (DEPTH, D, IN_COLS), 'b_in': (DEPTH, IN_COLS), 'pool_w': (DEPTH, 4, POOL_GD, POOL_GD),
    'pool_scale': (DEPTH, D), 'lru_conv_w': (DEPTH, 4, D), 'lru_conv_b': (DEPTH, D),
    'lru_w_r': (DEPTH, LRU_HEADS, LRU_HD, LRU_HD), 'lru_b_r': (DEPTH, D),
    'lru_w_i': (DEPTH, LRU_HEADS, LRU_HD, LRU_HD), 'lru_b_i': (DEPTH, D), 'lru_lambda': (DEPTH, D),
    'lru_w_out': (DEPTH, D, D), 'sconv_w': (DEPTH, 3, D), 'sconv_w_out': (DEPTH, D, D), 'w_mix_out': (DEPTH, D, D),
    'xa_w_q': (DEPTH, D, D), 'xa_w_k': (DEPTH, D, D), 'xa_w_v': (DEPTH, D, D), 'xa_w_o': (DEPTH, D, D),
    'ffn_w_gate': (DEPTH, D, D_FF), 'ffn_w_up': (DEPTH, D, D_FF), 'ffn_w_down': (DEPTH, D_FF, D),
    'ln_g': (DEPTH, 3, D), 'ln_b': (DEPTH, 3, D)}


def _pack(arrs, names, width, dtype, row_mult):
    flat = jnp.concatenate([arrs[n].astype(dtype).reshape(-1) for n in names])
    pad = (-flat.shape[0]) % (width * row_mult)
    if pad:
        flat = jnp.concatenate([flat, jnp.zeros((pad,), dtype)])
    return flat.reshape(-1, width)


def _unpack(flat2d, names, shapes):
    flat = flat2d.reshape(-1)
    out, off = {}, 0
    for n in names:
        size = 1
        for s in shapes[n]:
            size *= s
        out[n] = flat[off:off + size].reshape(shapes[n])
        off += size
    return out


def _gathered_full(g4, names, sharded_axis):
    loc_shapes = {n: _local_shape(n, FULL_SHAPES[n]) for n in names}
    per = [_unpack(g4[k], names, loc_shapes) for k in range(4)]
    return {n: jnp.concatenate([per[k][n] for k in range(4)], axis=sharded_axis[n]) for n in names}


def _perm_cols(a, perm, axis):
    blocks = [lax.slice_in_dim(a, p * D, (p + 1) * D, axis=axis) for p in perm]
    return jnp.concatenate(blocks, axis=axis)


def _gu_joined(g4, u4):
    return jnp.concatenate([g4[0], g4[1], u4[0], u4[1], g4[2], g4[3], u4[2], u4[3]], axis=1)


def _gu_apart(a):
    w = a.shape[1] // 8
    cols = [a[:, i * w:(i + 1) * w] for i in range(8)]
    return jnp.stack([cols[0], cols[1], cols[4], cols[5]]), jnp.stack([cols[2], cols[3], cols[6], cols[7]])


Z_INV = tuple(Z_PERM.index(j) for j in range(8))
SMALL_SH_NAMES = list(SMALL_SHARDED)
SMALL_ROWS = 32


def kernel(x, mem, w_in, b_in, pool_w, pool_scale, lru_conv_w, lru_conv_b, lru_w_r, lru_b_r, lru_w_i, lru_b_i, lru_lambda, lru_w_out, sconv_w, sconv_w_out, w_mix_out, xa_w_q, xa_w_k, xa_w_v, xa_w_o, ffn_w_gate, ffn_w_up, ffn_w_down, ln_g, ln_b, loss_target, m_w_in, m_b_in, m_pool_w, m_pool_scale, m_lru_conv_w, m_lru_conv_b, m_lru_w_r, m_lru_b_r, m_lru_w_i, m_lru_b_i, m_lru_lambda, m_lru_w_out, m_sconv_w, m_sconv_w_out, m_w_mix_out, m_xa_w_q, m_xa_w_k, m_xa_w_v, m_xa_w_o, m_ffn_w_gate, m_ffn_w_up, m_ffn_w_down, m_ln_g, m_ln_b, v_w_in, v_b_in, v_pool_w, v_pool_scale, v_lru_conv_w, v_lru_conv_b, v_lru_w_r, v_lru_b_r, v_lru_w_i, v_lru_b_i, v_lru_lambda, v_lru_w_out, v_sconv_w, v_sconv_w_out, v_w_mix_out, v_xa_w_q, v_xa_w_k, v_xa_w_v, v_xa_w_o, v_ffn_w_gate, v_ffn_w_up, v_ffn_w_down, v_ln_g, v_ln_b):
    loc = dict(w_in=w_in, b_in=b_in, pool_w=pool_w, pool_scale=pool_scale, lru_conv_w=lru_conv_w,
               lru_conv_b=lru_conv_b, lru_w_r=lru_w_r, lru_b_r=lru_b_r, lru_w_i=lru_w_i, lru_b_i=lru_b_i,
               lru_lambda=lru_lambda, lru_w_out=lru_w_out, sconv_w=sconv_w, sconv_w_out=sconv_w_out,
               w_mix_out=w_mix_out, xa_w_q=xa_w_q, xa_w_k=xa_w_k, xa_w_v=xa_w_v, xa_w_o=xa_w_o,
               ffn_w_gate=ffn_w_gate, ffn_w_up=ffn_w_up, ffn_w_down=ffn_w_down, ln_g=ln_g, ln_b=ln_b)
    mom = dict(w_in=m_w_in, b_in=m_b_in, pool_w=m_pool_w, pool_scale=m_pool_scale, lru_conv_w=m_lru_conv_w,
               lru_conv_b=m_lru_conv_b, lru_w_r=m_lru_w_r, lru_b_r=m_lru_b_r, lru_w_i=m_lru_w_i, lru_b_i=m_lru_b_i,
               lru_lambda=m_lru_lambda, lru_w_out=m_lru_w_out, sconv_w=m_sconv_w, sconv_w_out=m_sconv_w_out,
               w_mix_out=m_w_mix_out, xa_w_q=m_xa_w_q, xa_w_k=m_xa_w_k, xa_w_v=m_xa_w_v, xa_w_o=m_xa_w_o,
               ffn_w_gate=m_ffn_w_gate, ffn_w_up=m_ffn_w_up, ffn_w_down=m_ffn_w_down, ln_g=m_ln_g, ln_b=m_ln_b)
    var = dict(w_in=v_w_in, b_in=v_b_in, pool_w=v_pool_w, pool_scale=v_pool_scale, lru_conv_w=v_lru_conv_w,
               lru_conv_b=v_lru_conv_b, lru_w_r=v_lru_w_r, lru_b_r=v_lru_b_r, lru_w_i=v_lru_w_i, lru_b_i=v_lru_b_i,
               lru_lambda=v_lru_lambda, lru_w_out=v_lru_w_out, sconv_w=v_sconv_w, sconv_w_out=v_sconv_w_out,
               w_mix_out=v_w_mix_out, xa_w_q=v_xa_w_q, xa_w_k=v_xa_w_k, xa_w_v=v_xa_w_v, xa_w_o=v_xa_w_o,
               ffn_w_gate=v_ffn_w_gate, ffn_w_up=v_ffn_w_up, ffn_w_down=v_ffn_w_down, ln_g=v_ln_g, ln_b=v_ln_b)

    chip = 2 * lax.axis_index("x") + lax.axis_index("y")
    core = lax.axis_index("c")
    chip_arr = jnp.reshape(chip, (1,)).astype(jnp.int32)
    core_arr = jnp.reshape(core, (1,)).astype(jnp.int32)
    n_p = len(PIECES)

    shards = {n: loc[n].astype(BF) for n in PIECES}
    small = _pack(loc, SMALL_SH_NAMES, 256, F32, 8)
    first = ('w_in',)
    rest = tuple(n for n in PIECES if n not in first)
    early1 = ('w_in', 'pool_w')
    late1 = tuple(n for n in PIECES if n not in early1)
    got = run_hosted(gather_layer(0, shards, small, pieces=first), "gather_first")
    vec = _gathered_full(got[len(first)], SMALL_SH_NAMES, SMALL_SHARDED)
    W = {n: [None] * DEPTH for n in ('w_in', 'pool_w', 'ffn_w_down', 'ffn_w_gu') + SQUARES}
    W['b_in'] = [jnp.roll(b_in[l:l + 1], -2 * D, axis=1) for l in range(DEPTH)]
    for n in ('lru_conv_w', 'sconv_w', 'ln_g', 'ln_b'):
        W[n] = [vec[n][l] for l in range(DEPTH)]
    for n in ('lru_w_r', 'lru_w_i'):
        W[n] = [loc[n][l].astype(BF) for l in range(DEPTH)]
    for n in ('pool_scale', 'lru_conv_b', 'lru_b_r', 'lru_b_i', 'lru_lambda'):
        W[n] = [loc[n][l:l + 1] for l in range(DEPTH)]

    def take(l, names, arrays):
        full = dict(zip(names, arrays))
        if 'ffn_w_gate' in full:
            W['ffn_w_gu'][l] = _gu_joined(full['ffn_w_gate'], full['ffn_w_up'])
        for n in names:
            if n in W:
                W[n][l] = full[n]

    take(0, first, got[:len(first)])

    def after_in_proj(results):
        take(0, rest, results[:len(rest)])
        take(1, early1, results[len(rest):])

    xs, memb = x[0], mem[0].astype(BF)
    host_a = both_hosted(gather_layer(0, shards, pieces=rest), gather_layer(1, shards, pieces=early1))
    xf, xb, sv0 = _layer_fwd(0, xs, None, None, memb, W, host=host_a, host2=gather_layer(1, shards, pieces=late1),
                             after_in_proj=after_in_proj)
    take(1, late1, sv0['hosted2'])
    xf, xb, sv1 = _layer_fwd(1, xf, xb, None, memb, W)
    saves = [sv0, sv1]
    kvs = [(sv['kb'], sv['vb']) for sv in saves]

    def packed(G):
        g = dict(G)
        g['ffn_w_gate'], g['ffn_w_up'] = _gu_apart(g['ffn_w_gu'])
        g['pool_w'] = g['pool_w'].astype(BF)
        g['b_in'] = jnp.roll(g['b_in'], 2 * D, axis=1)
        return {n: g[n] for n in PIECES}, _pack(g, SMALL_ALL, D, F32, SMALL_ROWS)

    def reduce_start(G):
        parts, smallp = packed(G)
        theirs, theirs_small = split_halves(parts, smallp)
        q = {}
        for group in SAME_SHAPE:
            q.update(add_halves(group, parts, theirs, core_arr))
        qs = add_halves_small(smallp, theirs_small, core_arr)
        return q, qs, exchange_halves(q, qs)

    def reduce_finish(q, qs, recv):
        recv3 = dict(zip(PIECES, recv[:n_p]))
        summed = {}
        for group in SAME_SHAPE:
            summed.update(sum_halves(group, q, recv3, chip_arr))
        sums = [summed[n] for n in PIECES]
        sums.append(sum_chips_small(qs, recv[n_p], chip_arr))
        other = swap_cores(sums)
        return sums, other

    lf = (W['ln_b'][1][2:3], loss_target[0])
    dx, G1, loss_blk = _layer_bwd(1, None, saves[1], memb, kvs[1][0], kvs[1][1], W, loss_from=lf)
    q1, qs1, host1 = reduce_start(G1)
    started0 = []

    def last_host_fn(G):
        started0.extend(reduce_start(G))
        return started0[2]

    grad_x, G0, _ = _layer_bwd(0, dx, saves[0], memb, kvs[0][0], kvs[0][1], W, host=host1, last_host_fn=last_host_fn)
    red = [reduce_finish(started0[0], started0[1], G0['hosted_last']), reduce_finish(q1, qs1, G0['hosted'])]
    loss = lax.psum(loss_blk[0, 0], ("x", "y", "c"))

    small_shapes = {n: FULL_SHAPES[n][1:] for n in SMALL_ALL}
    per_layer = []
    for l in range(DEPTH):
        mine, theirs = red[l][0][-1], red[l][1][-1]
        whole = jnp.where(core == 0, jnp.concatenate([mine, theirs]), jnp.concatenate([theirs, mine]))
        per_layer.append(_unpack(whole, SMALL_ALL, small_shapes))
    grads = {}
    for n in SMALL_ALL:
        gn = jnp.stack([per_layer[l][n] for l in range(DEPTH)])
        if n in SMALL_SHARDED:
            size = loc[n].shape[SMALL_SHARDED[n]]
            gn = lax.dynamic_slice_in_dim(gn, chip * size, size, axis=SMALL_SHARDED[n])
        grads[n] = gn

    out_d, out_m, out_v = {}, {}, {}
    for p, n in enumerate(PIECES):
        pairs = [(red[l][0][p], red[l][1][p]) for l in range(DEPTH)]
        grads[n], out_d[n], out_m[n], out_v[n] = adamw_halves(loc[n], pairs, mom[n], var[n], name="adamw_" + n)
    for n in SMALL_ALL:
        out_d[n], out_m[n], out_v[n] = adamw(loc[n], grads[n], mom[n], var[n], name="adamw_" + n)

    return (loss, grad_x[None], *[grads[n] for n in WEIGHTS], *[out_d[n] for n in WEIGHTS],
            *[out_m[n] for n in WEIGHTS], *[out_v[n] for n in WEIGHTS])
```

```python
import jax
import jax.numpy as jnp
from jax import lax
from jax.experimental import pallas as pl
from jax.experimental.pallas import tpu as pltpu

F32 = jnp.float32
BF = jnp.bfloat16
MESH = pl.DeviceIdType.MESH

D = 1024
DEPTH = 2
N_MEM = 256
POOL_WINDOWS = (2, 4, 8, 16)
POOL_GD = 256
LRU_HEADS = 8
LRU_HD = 128
LRU_C = 8.0
X_HEADS = 4
X_HD = 256
D_FF = 2816
IN_COLS = 8 * D
ALPHA = (2 * DEPTH) ** 0.25
LN_EPS = 1e-5
ADAM_LR = 0.001
ADAM_B1 = 0.9
ADAM_B2 = 0.999
ADAM_EPS = 1e-08
ADAM_WD = 0.01
ADAM_STEP = 10

Z_PERM = (2, 3, 4, 5, 6, 7, 0, 1)
ZB_SCONV, ZB_GATE, ZB_POOL, ZB_LRU = 0, 1, 6, 7
HALO = 16
VMEM_LIMIT = 56 * 1024 * 1024

WEIGHTS = ['w_in', 'b_in', 'pool_w', 'pool_scale', 'lru_conv_w', 'lru_conv_b', 'lru_w_r', 'lru_b_r', 'lru_w_i',
           'lru_b_i', 'lru_lambda', 'lru_w_out', 'sconv_w', 'sconv_w_out', 'w_mix_out', 'xa_w_q', 'xa_w_k', 'xa_w_v',
           'xa_w_o', 'ffn_w_gate', 'ffn_w_up', 'ffn_w_down', 'ln_g', 'ln_b']
BIG_SHARDED = {'w_in': 2, 'pool_w': 2, 'lru_w_out': 1, 'sconv_w_out': 1, 'w_mix_out': 1, 'xa_w_q': 1, 'xa_w_k': 1,
               'xa_w_v': 1, 'xa_w_o': 1, 'ffn_w_gate': 2, 'ffn_w_up': 2, 'ffn_w_down': 1}
SMALL_SHARDED = {'lru_conv_w': 2, 'sconv_w': 2, 'ln_g': 2, 'ln_b': 2}
SMALL_ALL = ['b_in', 'pool_scale', 'lru_conv_w', 'lru_conv_b', 'lru_w_r', 'lru_b_r', 'lru_w_i', 'lru_b_i',
             'lru_lambda', 'sconv_w', 'ln_g', 'ln_b']


def _cp(*sem):
    return pltpu.CompilerParams(dimension_semantics=sem, vmem_limit_bytes=VMEM_LIMIT)


def _sigmoid(x):
    return 0.5 * jnp.tanh(0.5 * x) + 0.5


class Hosted:
    def __init__(self, arrays, out_shape, sems, start, finish):
        self.arrays, self.out_shape, self.sems, self.start, self.finish = arrays, out_shape, sems, start, finish


def _with_host(host, n_in, n_out, refs):
    if host is None:
        return refs[:n_in], (), refs[n_in:n_in + n_out], (), refs[n_in + n_out:], ()
    hi, ho, hs = len(host.arrays), len(host.out_shape), len(host.sems)
    ins, h_in = refs[:n_in], refs[n_in:n_in + hi]
    outs = refs[n_in + hi:n_in + hi + n_out]
    h_out = refs[n_in + hi + n_out:n_in + hi + n_out + ho]
    rest = refs[n_in + hi + n_out + ho:]
    return ins, h_in, outs, h_out, rest[:len(rest) - hs], rest[len(rest) - hs:]


HBM_SPEC = pl.BlockSpec(memory_space=pl.ANY)


def mm_nn(a, w, bias, *, out_dtype, tm, tn, name, host=None):
    T, K = a.shape
    N = w.shape[1]
    tm, tn = min(tm, T), min(tn, N)
    nj, ni = N // tn, T // tm
    n_in = 2 if bias is None else 3

    def body(*refs):
        ins, h_in, (o_ref,), h_out, _, h_sems = _with_host(host, n_in, 1, refs)
        a_ref, w_ref = ins[:2]
        j, i = pl.program_id(0), pl.program_id(1)
        if host is not None:
            @pl.when((j == 0) & (i == 0))
            def _():
                host.start(h_in, h_out, h_sems)
        acc = jnp.dot(a_ref[...].astype(BF), w_ref[...], preferred_element_type=F32)
        if bias is not None:
            acc = acc + ins[2][...]
        o_ref[...] = acc.astype(o_ref.dtype)
        if host is not None:
            @pl.when((j == nj - 1) & (i == ni - 1))
            def _():
                host.finish(h_in, h_out, h_sems)

    in_specs = [pl.BlockSpec((tm, K), lambda j, i: (i, 0)), pl.BlockSpec((K, tn), lambda j, i: (0, j))]
    args = [a, w]
    if bias is not None:
        in_specs.append(pl.BlockSpec((1, tn), lambda j, i: (0, j)))
        args.append(bias)
    out_specs = [pl.BlockSpec((tm, tn), lambda j, i: (i, j))]
    out_shape = [jax.ShapeDtypeStruct((T, N), out_dtype)]
    scratch = []
    if host is not None:
        in_specs += [HBM_SPEC] * len(host.arrays)
        args += list(host.arrays)
        out_specs += [HBM_SPEC] * len(host.out_shape)
        out_shape += list(host.out_shape)
        scratch = list(host.sems)
    res = pl.pallas_call(
        body, name=name, grid=(nj, ni), in_specs=in_specs, out_specs=out_specs, out_shape=out_shape,
        scratch_shapes=scratch, compiler_params=_cp("arbitrary", "arbitrary"))(*args)
    return res[0] if host is None else (res[0], res[1:])


FF_HALF = D_FF // 2


def ffn_in_swiglu(x, wgu, *, tm, name, host=None):
    T, K = x.shape
    tm = min(tm, T)
    ni = T // tm

    def body(*refs):
        (x_ref, w_ref), h_in, (gu_ref, h_ref), h_out, _, h_sems = _with_host(host, 2, 2, refs)
        j, i = pl.program_id(0), pl.program_id(1)
        if host is not None:
            @pl.when((j == 0) & (i == 0))
            def _():
                host.start(h_in, h_out, h_sems)
        acc = jnp.dot(x_ref[...].astype(BF), w_ref[...], preferred_element_type=F32)
        gu = acc.astype(BF)
        gu_ref[...] = gu
        g = gu[:, :FF_HALF].astype(F32)
        h_ref[...] = (g * _sigmoid(g) * gu[:, FF_HALF:].astype(F32)).astype(BF)
        if host is not None:
            @pl.when((j == 1) & (i == ni - 1))
            def _():
                host.finish(h_in, h_out, h_sems)

    in_specs = [pl.BlockSpec((tm, K), lambda j, i: (i, 0)), pl.BlockSpec((K, 2 * FF_HALF), lambda j, i: (0, j))]
    args = [x, wgu]
    out_specs = [pl.BlockSpec((tm, 2 * FF_HALF), lambda j, i: (i, j)), pl.BlockSpec((tm, FF_HALF), lambda j, i: (i, j))]
    out_shape = [jax.ShapeDtypeStruct((T, 2 * D_FF), BF), jax.ShapeDtypeStruct((T, D_FF), BF)]
    scratch = []
    if host is not None:
        in_specs += [HBM_SPEC] * len(host.arrays)
        args += list(host.arrays)
        out_specs += [HBM_SPEC] * len(host.out_shape)
        out_shape += list(host.out_shape)
        scratch = list(host.sems)
    res = pl.pallas_call(
        body, name=name, grid=(2, ni), in_specs=in_specs, out_specs=out_specs, out_shape=out_shape,
        scratch_shapes=scratch, compiler_params=_cp("arbitrary", "arbitrary"))(*args)
    return (res[0], res[1]) if host is None else (res[0], res[1], res[2:])


def ffn_down_dx_swiglu(dp, wd, gu, *, tm, name):
    T = dp.shape[0]
    tm = min(tm, T)

    def body(dp_ref, w_ref, gu_ref, o_ref):
        dh = lax.dot_general(dp_ref[...], w_ref[...], (((1,), (1,)), ((), ())), preferred_element_type=F32)
        g = gu_ref[:, :FF_HALF].astype(F32)
        u = gu_ref[:, FF_HALF:].astype(F32)
        sg = _sigmoid(g)
        t = dh * sg
        s = g * sg
        o_ref[:, :FF_HALF] = (t * u * (1.0 + g - s)).astype(BF)
        o_ref[:, FF_HALF:] = (t * g).astype(BF)

    return pl.pallas_call(
        body, name=name, grid=(2, T // tm),
        in_specs=[pl.BlockSpec((tm, D), lambda j, i: (i, 0)), pl.BlockSpec((FF_HALF, D), lambda j, i: (j, 0)),
                  pl.BlockSpec((tm, 2 * FF_HALF), lambda j, i: (i, j))],
        out_specs=pl.BlockSpec((tm, 2 * FF_HALF), lambda j, i: (i, j)),
        out_shape=jax.ShapeDtypeStruct((T, 2 * D_FF), BF), compiler_params=_cp("arbitrary", "arbitrary"))(dp, wd, gu)


def mm_nt(a, w, res, *, out_dtype, tm, tc, name, host=None):
    T, C = a.shape
    K = w.shape[0]
    tm, tc = min(tm, T), min(tc, C)
    nc = C // tc
    ni = T // tm
    n_in = 2 if res is None else 3

    def body(*refs):
        ins, h_in, (o_ref,), h_out, (acc_ref,), h_sems = _with_host(host, n_in, 1, refs)
        a_ref, w_ref = ins[:2]
        r_ref = ins[2] if res is not None else None
        c = pl.program_id(1)
        if host is not None:
            @pl.when((pl.program_id(0) == 0) & (c == 0))
            def _():
                host.start(h_in, h_out, h_sems)

            @pl.when((pl.program_id(0) == ni - 1) & (c == nc - 1))
            def _():
                host.finish(h_in, h_out, h_sems)
        part = lax.dot_general(a_ref[...].astype(BF), w_ref[...], (((1,), (1,)), ((), ())),
                               preferred_element_type=F32)

        @pl.when(c == 0)
        def _():
            acc_ref[...] = part

        @pl.when(c > 0)
        def _():
            acc_ref[...] += part

        @pl.when(c == nc - 1)
        def _():
            out = acc_ref[...]
            if res is not None:
                out = out + ALPHA * r_ref[...]
            o_ref[...] = out.astype(o_ref.dtype)

    in_specs = [pl.BlockSpec((tm, tc), lambda i, c: (i, c)), pl.BlockSpec((K, tc), lambda i, c: (0, c))]
    args = [a, w]
    if res is not None:
        in_specs.append(pl.BlockSpec((tm, K), lambda i, c: (i, 0)))
        args.append(res)
    out_specs = [pl.BlockSpec((tm, K), lambda i, c: (i, 0))]
    out_shape = [jax.ShapeDtypeStruct((T, K), out_dtype)]
    scratch = [pltpu.VMEM((tm, K), F32)]
    if host is not None:
        in_specs += [HBM_SPEC] * len(host.arrays)
        args += list(host.arrays)
        out_specs += [HBM_SPEC] * len(host.out_shape)
        out_shape += list(host.out_shape)
        scratch += list(host.sems)
    out = pl.pallas_call(
        body, name=name, grid=(ni, nc), in_specs=in_specs, out_specs=out_specs, out_shape=out_shape,
        scratch_shapes=scratch, compiler_params=_cp("arbitrary", "arbitrary"))(*args)
    return out[0] if host is None else (out[0], out[1:])


def mm_tn(a, b, *, out_dtype, tk, tn, tt, name, colsum=False):
    T, K = a.shape
    N = b.shape[1]
    tk, tn, tt = min(tk, K), min(tn, N), min(tt, T)
    nt = T // tt

    def body(*refs):
        if colsum:
            a_ref, b_ref, o_ref, cs_ref, acc_ref = refs
        else:
            a_ref, b_ref, o_ref, acc_ref = refs
        i, t = pl.program_id(1), pl.program_id(2)
        bb = b_ref[...]
        part = lax.dot_general(a_ref[...].astype(BF), bb.astype(BF), (((0,), (0,)), ((), ())),
                               preferred_element_type=F32)

        @pl.when(t == 0)
        def _():
            acc_ref[...] = part

        @pl.when(t > 0)
        def _():
            acc_ref[...] += part

        @pl.when(t == nt - 1)
        def _():
            o_ref[...] = acc_ref[...].astype(o_ref.dtype)

        if colsum:
            s = jnp.sum(bb.astype(F32), axis=0, keepdims=True)

            @pl.when((i == 0) & (t == 0))
            def _():
                cs_ref[...] = s

            @pl.when((i == 0) & (t > 0))
            def _():
                cs_ref[...] += s

    out_specs = [pl.BlockSpec((tk, tn), lambda j, i, t: (i, j))]
    out_shape = [jax.ShapeDtypeStruct((K, N), out_dtype)]
    if colsum:
        out_specs.append(pl.BlockSpec((1, tn), lambda j, i, t: (0, j)))
        out_shape.append(jax.ShapeDtypeStruct((1, N), F32))
    res = pl.pallas_call(
        body, name=name, grid=(N // tn, K // tk, nt),
        in_specs=[pl.BlockSpec((tt, tk), lambda j, i, t: (t, i)), pl.BlockSpec((tt, tn), lambda j, i, t: (t, j))],
        out_specs=out_specs, out_shape=out_shape, scratch_shapes=[pltpu.VMEM((tk, tn), F32)],
        compiler_params=_cp("arbitrary", "arbitrary", "arbitrary"))(a, b)
    return res if colsum else res[0]


def mm_res_ln(a, w, res, g, b, *, tm, name):
    T, K = a.shape
    tm = min(tm, T)

    def body(a_ref, w_ref, r_ref, g_ref, b_ref, y_ref, yb_ref, xh_ref, rs_ref):
        pre = ALPHA * r_ref[...] + jnp.dot(a_ref[...].astype(BF), w_ref[...], preferred_element_type=F32)
        mu = jnp.mean(pre, axis=-1, keepdims=True)
        cen = pre - mu
        var = jnp.mean(cen * cen, axis=-1, keepdims=True)
        rstd = lax.rsqrt(var + LN_EPS)
        xhat = cen * rstd
        y = xhat * g_ref[...] + b_ref[...]
        y_ref[...] = y
        yb_ref[...] = y.astype(BF)
        xh_ref[...] = xhat
        rs_ref[...] = rstd

    row = lambda i: (i, 0)
    fix = lambda i: (0, 0)
    return pl.pallas_call(
        body, name=name, grid=(T // tm,),
        in_specs=[pl.BlockSpec((tm, K), row), pl.BlockSpec((K, D), fix), pl.BlockSpec((tm, D), row),
                  pl.BlockSpec((1, D), fix), pl.BlockSpec((1, D), fix)],
        out_specs=[pl.BlockSpec((tm, D), row), pl.BlockSpec((tm, D), row), pl.BlockSpec((tm, D), row),
                   pl.BlockSpec((tm, 1), row)],
        out_shape=[jax.ShapeDtypeStruct((T, D), F32), jax.ShapeDtypeStruct((T, D), BF),
                   jax.ShapeDtypeStruct((T, D), F32), jax.ShapeDtypeStruct((T, 1), F32)],
        compiler_params=_cp("arbitrary"))(a, w, res, g, b)


def ln_bwd(dy, xhat, rstd, g, *, tm, name, loss_from=None, dw_of=None):
    T = xhat.shape[0]
    tm = min(tm, T)
    nt = T // tm
    with_loss = loss_from is not None
    with_dw = dw_of is not None

    def body(*refs):
        if with_dw:
            acc_ref, refs = refs[-1], refs[:-1]
            n_main_in = 5 if with_loss else 4
            a_ref = refs[n_main_in]
            dw_ref = refs[-1]
            refs = refs[:n_main_in] + refs[n_main_in + 1:-1]
        if with_loss:
            xh_ref, rs_ref, g_ref, b_ref, t_ref, dp_ref, dpb_ref, dg_ref, db_ref, ls_ref = refs
        else:
            dy_ref, xh_ref, rs_ref, g_ref, dp_ref, dpb_ref, dg_ref, db_ref = refs
        i = pl.program_id(0)
        xhat_ = xh_ref[...]
        gg = g_ref[...]
        if with_loss:
            err = xhat_ * gg + b_ref[...] - t_ref[...]
            dyv = err * (1.0 / D)
            lpart = 0.5 * jnp.sum(jnp.sum(err * err, axis=-1, keepdims=True) * (1.0 / D))
        else:
            dyv = dy_ref[...]
        dxh = dyv * gg
        m1 = jnp.mean(dxh, axis=-1, keepdims=True)
        m2 = jnp.mean(dxh * xhat_, axis=-1, keepdims=True)
        dpre = rs_ref[...] * (dxh - m1 - xhat_ * m2)
        dp_ref[...] = dpre
        dpb = dpre.astype(BF)
        dpb_ref[...] = dpb
        dgp = jnp.sum(dyv * xhat_, axis=0, keepdims=True)
        dbp = jnp.sum(dyv, axis=0, keepdims=True)
        if with_dw:
            dwp = lax.dot_general(a_ref[...], dpb, (((0,), (0,)), ((), ())), preferred_element_type=F32)

        @pl.when(i == 0)
        def _():
            dg_ref[...] = dgp
            db_ref[...] = dbp
            if with_loss:
                ls_ref[...] = jnp.full((8, 128), lpart, F32)
            if with_dw:
                acc_ref[...] = dwp

        @pl.when(i > 0)
        def _():
            dg_ref[...] += dgp
            db_ref[...] += dbp
            if with_loss:
                ls_ref[...] += jnp.full((8, 128), lpart, F32)
            if with_dw:
                acc_ref[...] += dwp

        if with_dw:
            @pl.when(i == nt - 1)
            def _():
                dw_ref[...] = acc_ref[...].astype(BF)

    row = lambda i: (i, 0)
    fix = lambda i: (0, 0)
    if with_loss:
        in_specs = [pl.BlockSpec((tm, D), row), pl.BlockSpec((tm, 1), row), pl.BlockSpec((1, D), fix),
                    pl.BlockSpec((1, D), fix), pl.BlockSpec((tm, D), row)]
        args = [xhat, rstd, g, loss_from[0], loss_from[1]]
    else:
        in_specs = [pl.BlockSpec((tm, D), row), pl.BlockSpec((tm, D), row), pl.BlockSpec((tm, 1), row),
                    pl.BlockSpec((1, D), fix)]
        args = [dy, xhat, rstd, g]
    out_specs = [pl.BlockSpec((tm, D), row), pl.BlockSpec((tm, D), row), pl.BlockSpec((1, D), fix),
                 pl.BlockSpec((1, D), fix)]
    out_shape = [jax.ShapeDtypeStruct((T, D), F32), jax.ShapeDtypeStruct((T, D), BF),
                 jax.ShapeDtypeStruct((1, D), F32), jax.ShapeDtypeStruct((1, D), F32)]
    if with_loss:
        out_specs.append(pl.BlockSpec((8, 128), fix))
        out_shape.append(jax.ShapeDtypeStruct((8, 128), F32))
    scratch = []
    if with_dw:
        K = dw_of.shape[1]
        in_specs.append(pl.BlockSpec((tm, K), row))
        args.append(dw_of)
        out_specs.append(pl.BlockSpec((K, D), fix))
        out_shape.append(jax.ShapeDtypeStruct((K, D), BF))
        scratch.append(pltpu.VMEM((K, D), F32))
    return pl.pallas_call(body, name=name, grid=(nt,), in_specs=in_specs, out_specs=out_specs,
                          out_shape=out_shape, scratch_shapes=scratch, compiler_params=_cp("arbitrary"))(*args)


def _prev_halo(tm, blk):
    return lambda i: (jnp.maximum(i * (tm // HALO) - 1, 0), blk)


def _next_halo(tm, T, blk):
    return lambda i: (jnp.minimum((i + 1) * (tm // HALO), T // HALO - 1), blk)


def _pool_p(ext, t, g):
    e = ext[:, g * POOL_GD:(g + 1) * POOL_GD]
    s = e
    for sh in (1, 2, 4, 8)[:g + 1]:
        s = s + pltpu.roll(s, sh, axis=0)
    inv = 1.0 / jnp.minimum(t + 1, POOL_WINDOWS[g]).astype(F32)
    return s[HALO:] * inv - e[HALO:]


def pool_fwd(z, pw, *, tm, name):
    T = z.shape[0]
    tm = min(tm, T)

    def body(zm_ref, zh_ref, pw_ref, o_ref):
        i = pl.program_id(0)
        keep = jnp.where(i == 0, 0.0, 1.0).astype(F32)
        ext = jnp.concatenate([zh_ref[...].astype(F32) * keep, zm_ref[...].astype(F32)], axis=0)
        t = i * tm + lax.broadcasted_iota(jnp.int32, (tm, 1), 0)
        outs = [jnp.dot(_pool_p(ext, t, g).astype(BF), pw_ref[g], preferred_element_type=F32) for g in range(4)]
        o_ref[...] = jnp.concatenate(outs, axis=1).astype(o_ref.dtype)

    return pl.pallas_call(
        body, name=name, grid=(T // tm,),
        in_specs=[pl.BlockSpec((tm, D), lambda i: (i, ZB_POOL)), pl.BlockSpec((HALO, D), _prev_halo(tm, ZB_POOL)),
                  pl.BlockSpec((4, POOL_GD, POOL_GD), lambda i: (0, 0, 0))],
        out_specs=pl.BlockSpec((tm, D), lambda i: (i, 0)),
        out_shape=jax.ShapeDtypeStruct((T, D), BF), compiler_params=_cp("arbitrary"))(z, z, pw)


def pool_bwd(dz, dyp, yp_pre, z, pw, ps, *, tm, name):
    T = z.shape[0]
    tm = min(tm, T)
    nt = T // tm

    def body(dz_in, dy_ref, dyn_ref, yp_ref, zm_ref, zh_ref, pw_ref, ps_ref, dz_ref, dpw_ref, dps_ref):
        del dz_in
        i = pl.program_id(0)
        keep_p = jnp.where(i == 0, 0.0, 1.0).astype(F32)
        keep_n = jnp.where(i == nt - 1, 0.0, 1.0).astype(F32)
        ext = jnp.concatenate([zh_ref[...].astype(F32) * keep_p, zm_ref[...].astype(F32)], axis=0)
        t = i * tm + lax.broadcasted_iota(jnp.int32, (tm, 1), 0)
        psv = ps_ref[...]
        dy = dy_ref[...].astype(F32)
        dyp_ext = jnp.concatenate([dy, dyn_ref[...].astype(F32) * keep_n], axis=0) * psv
        t_ext = i * tm + lax.broadcasted_iota(jnp.int32, (tm + HALO, 1), 0)
        dps = jnp.sum(dy * yp_ref[...].astype(F32), axis=0, keepdims=True)
        dzs, dpws = [], []
        for g in range(4):
            sl = slice(g * POOL_GD, (g + 1) * POOL_GD)
            dyg = dyp_ext[:, sl].astype(BF)
            dp = lax.dot_general(dyg, pw_ref[g], (((1,), (1,)), ((), ())), preferred_element_type=F32)
            q = dp * (1.0 / jnp.minimum(t_ext + 1, POOL_WINDOWS[g]).astype(F32))
            s = q
            for sh in (1, 2, 4, 8)[:g + 1]:
                s = s + pltpu.roll(s, tm + HALO - sh, axis=0)
            dzs.append(s[:tm] - dp[:tm])
            p = _pool_p(ext, t, g).astype(BF)
            dpws.append(lax.dot_general(p, dyg[:tm], (((0,), (0,)), ((), ())), preferred_element_type=F32))
        dz_ref[...] = jnp.concatenate(dzs, axis=1).astype(dz_ref.dtype)

        @pl.when(i == 0)
        def _():
            for g in range(4):
                dpw_ref[g] = dpws[g]
            dps_ref[...] = dps

        @pl.when(i > 0)
        def _():
            for g in range(4):
                dpw_ref[g] += dpws[g]
            dps_ref[...] += dps

    row = lambda i: (i, 0)
    return pl.pallas_call(
        body, name=name, grid=(nt,),
        in_specs=[pl.BlockSpec(memory_space=pl.ANY),
                  pl.BlockSpec((tm, D), row), pl.BlockSpec((HALO, D), _next_halo(tm, T, 0)),
                  pl.BlockSpec((tm, D), row),
                  pl.BlockSpec((tm, D), lambda i: (i, ZB_POOL)), pl.BlockSpec((HALO, D), _prev_halo(tm, ZB_POOL)),
                  pl.BlockSpec((4, POOL_GD, POOL_GD), lambda i: (0, 0, 0)), pl.BlockSpec((1, D), lambda i: (0, 0))],
        out_specs=[pl.BlockSpec((tm, D), lambda i: (i, ZB_POOL)),
                   pl.BlockSpec((4, POOL_GD, POOL_GD), lambda i: (0, 0, 0)), pl.BlockSpec((1, D), lambda i: (0, 0))],
        out_shape=[jax.ShapeDtypeStruct(dz.shape, dz.dtype), jax.ShapeDtypeStruct((4, POOL_GD, POOL_GD), F32),
                   jax.ShapeDtypeStruct((1, D), F32)],
        input_output_aliases={0: 0}, compiler_params=_cp("arbitrary"))(dz, dyp, dyp, yp_pre, z, z, pw, ps)


def _fill_ext(ext_s, halo, main, keep):
    ext_s[0:HALO, :] = halo * keep
    ext_s[HALO:, :] = main


def _lru_gates(ext_s, tm, cw, cb, wr_ref, br, wi_ref, bi, lam):
    shifted = []
    v = cb
    for k in range(4):
        zs = ext_s[pl.ds(HALO - 3 + k, tm), :]
        shifted.append(zs)
        v = v + cw[k:k + 1, :] * zs
    vb = v.astype(BF)
    rp, ip = [], []
    for h in range(LRU_HEADS):
        sl = slice(h * LRU_HD, (h + 1) * LRU_HD)
        rp.append(jnp.dot(vb[:, sl], wr_ref[h], preferred_element_type=F32))
        ip.append(jnp.dot(vb[:, sl], wi_ref[h], preferred_element_type=F32))
    r = _sigmoid(jnp.concatenate(rp, axis=1) + br)
    ig = _sigmoid(jnp.concatenate(ip, axis=1) + bi)
    sp = jnp.maximum(-lam, 0.0) + jnp.log(1.0 + jnp.exp(-jnp.abs(lam)))
    a = jnp.exp(-LRU_C * r * sp)
    om = 1.0 - a * a
    rs = lax.rsqrt(om)
    return v, vb, r, ig, a, om, rs, sp, shifted


def lru_fwd(z, cw, cb, wr, br, wi, bi, lam, wlo, *, tm, name):
    T = z.shape[0]
    tm = min(tm, T)
    nch = tm // 8

    def body(zm_ref, zh_ref, cw_ref, cb_ref, wr_ref, br_ref, wi_ref, bi_ref, lam_ref, wlo_ref, h_ref, y_ref,
             a_s, b_s, carry, ext_s):
        i = pl.program_id(0)

        @pl.when(i == 0)
        def _():
            carry[...] = jnp.zeros_like(carry)

        keep = jnp.where(i == 0, 0.0, 1.0).astype(F32)
        _fill_ext(ext_s, zh_ref[...].astype(F32), zm_ref[...].astype(F32), keep)
        v, _, _, ig, a, om, rs, _, _ = _lru_gates(ext_s, tm, cw_ref[...], cb_ref[...], wr_ref, br_ref[...], wi_ref,
                                                  bi_ref[...], lam_ref[...])
        a_s[...] = a
        b_s[...] = jnp.where(om > 0.0, om * rs, 0.0) * (ig * v)
        row = lax.broadcasted_iota(jnp.int32, (8, D), 0)

        def step(ci, hprev):
            sl = pl.ds(pl.multiple_of(ci * 8, 8), 8)
            aa, bb = a_s[sl, :], b_s[sl, :]
            for s in (1, 2, 4):
                m = row >= s
                bb = bb + aa * jnp.where(m, pltpu.roll(bb, s, axis=0), 0.0)
                aa = aa * jnp.where(m, pltpu.roll(aa, s, axis=0), 1.0)
            h = bb + aa * hprev
            h_ref[sl, :] = h
            return jnp.broadcast_to(h[7:8, :], (8, D))

        carry[...] = lax.fori_loop(0, nch, step, carry[...])
        y_ref[...] = jnp.dot(h_ref[...].astype(BF), wlo_ref[...], preferred_element_type=F32).astype(BF)

    fix2 = lambda i: (0, 0)
    fix3 = lambda i: (0, 0, 0)
    return pl.pallas_call(
        body, name=name, grid=(T // tm,),
        in_specs=[pl.BlockSpec((tm, D), lambda i: (i, ZB_LRU)), pl.BlockSpec((HALO, D), _prev_halo(tm, ZB_LRU)),
                  pl.BlockSpec((4, D), fix2), pl.BlockSpec((1, D), fix2),
                  pl.BlockSpec((LRU_HEADS, LRU_HD, LRU_HD), fix3), pl.BlockSpec((1, D), fix2),
                  pl.BlockSpec((LRU_HEADS, LRU_HD, LRU_HD), fix3), pl.BlockSpec((1, D), fix2),
                  pl.BlockSpec((1, D), fix2), pl.BlockSpec((D, D), fix2)],
        out_specs=[pl.BlockSpec((tm, D), lambda i: (i, 0)), pl.BlockSpec((tm, D), lambda i: (i, 0))],
        out_shape=[jax.ShapeDtypeStruct((T, D), F32), jax.ShapeDtypeStruct((T, D), BF)],
        scratch_shapes=[pltpu.VMEM((tm, D), F32), pltpu.VMEM((tm, D), F32), pltpu.VMEM((8, D), F32),
                        pltpu.VMEM((tm + HALO, D), F32)],
        compiler_params=_cp("arbitrary"))(z, z, cw, cb, wr, br, wi, bi, lam, wlo)


def lru_bwd(dz, dyl, z, h, cw, cb, wr, br, wi, bi, lam, wlo, *, tm, name):
    T = z.shape[0]
    tm = min(tm, T)
    nt = T // tm
    nch = tm // 8

    def body(dz_in, dy_ref, zm_ref, zh_ref, h_ref, hh_ref, cw_ref, cb_ref, wr_ref, br_ref, wi_ref, bi_ref, lam_ref,
             wlo_ref, dz_ref, dcw_ref, dcb_ref, dwr_ref, dbr_ref, dwi_ref, dbi_ref, dlam_ref, dwlo_ref,
             c_s, g_s, dh_s, dh_carry, a_ext, dv_ext, ext_s, h_ext, wlo_acc):
        del dz_in
        i = pl.program_id(0)
        ti = nt - 1 - i

        @pl.when(i == 0)
        def _():
            dh_carry[...] = jnp.zeros_like(dh_carry)
            a_ext[tm:, :] = jnp.zeros((8, D), F32)
            dv_ext[tm:, :] = jnp.zeros((HALO, D), F32)

        keep = jnp.where(ti == 0, 0.0, 1.0).astype(F32)
        _fill_ext(ext_s, zh_ref[...].astype(F32), zm_ref[...].astype(F32), keep)
        cw_ = cw_ref[...]
        lam_ = lam_ref[...]
        v, vb, r, ig, a, om, rs, sp, shifted = _lru_gates(ext_s, tm, cw_, cb_ref[...], wr_ref, br_ref[...], wi_ref,
                                                          bi_ref[...], lam_)
        mult = jnp.where(om > 0.0, om * rs, 0.0)
        a_ext[0:tm, :] = a
        c_s[...] = a_ext[pl.ds(1, tm), :]
        g_s[...] = lax.dot_general(dy_ref[...], wlo_ref[...], (((1,), (1,)), ((), ())), preferred_element_type=F32)
        row = lax.broadcasted_iota(jnp.int32, (8, D), 0)

        def step(k, nxt):
            ci = nch - 1 - k
            sl = pl.ds(pl.multiple_of(ci * 8, 8), 8)
            cc, gg = c_s[sl, :], g_s[sl, :]
            for s in (1, 2, 4):
                m = row < 8 - s
                gg = gg + cc * jnp.where(m, pltpu.roll(gg, 8 - s, axis=0), 0.0)
                cc = cc * jnp.where(m, pltpu.roll(cc, 8 - s, axis=0), 1.0)
            dh = gg + cc * nxt
            dh_s[sl, :] = dh
            return jnp.broadcast_to(dh[0:1, :], (8, D))

        dh_carry[...] = lax.fori_loop(0, nch, step, dh_carry[...])
        a_ext[tm:, :] = a[0:8, :]
        dh = dh_s[...]
        h_ext[0:8, :] = hh_ref[...] * keep
        hv = h_ref[...]
        h_ext[8:, :] = hv
        hprev = h_ext[pl.ds(7, tm), :]
        dwlo = lax.dot_general(hv.astype(BF), dy_ref[...], (((0,), (0,)), ((), ())), preferred_element_type=F32)
        iv = ig * v
        da = dh * hprev
        dmult = dh * iv
        div = dh * mult
        dlog = da * a - dmult * (a * a) * rs
        dr = dlog * (-LRU_C * sp)
        dlam = jnp.sum(dlog * r, axis=0, keepdims=True) * (LRU_C * _sigmoid(-lam_))
        di = div * v
        dv = div * ig
        drp = dr * r * (1.0 - r)
        dip = di * ig * (1.0 - ig)
        drb, dib = drp.astype(BF), dip.astype(BF)
        dvh, dwr, dwi = [], [], []
        nt_dims = (((1,), (1,)), ((), ()))
        tn_dims = (((0,), (0,)), ((), ()))
        for hd in range(LRU_HEADS):
            sl = slice(hd * LRU_HD, (hd + 1) * LRU_HD)
            dvh.append(lax.dot_general(drb[:, sl], wr_ref[hd], nt_dims, preferred_element_type=F32)
                       + lax.dot_general(dib[:, sl], wi_ref[hd], nt_dims, preferred_element_type=F32))
            dwr.append(lax.dot_general(vb[:, sl], drb[:, sl], tn_dims, preferred_element_type=F32))
            dwi.append(lax.dot_general(vb[:, sl], dib[:, sl], tn_dims, preferred_element_type=F32))
        dv = dv + jnp.concatenate(dvh, axis=1)
        dv_ext[0:tm, :] = dv
        dzl = cw_[3:4, :] * dv
        for k in range(3):
            dzl = dzl + cw_[k:k + 1, :] * dv_ext[pl.ds(3 - k, tm), :]
        dz_ref[...] = dzl.astype(dz_ref.dtype)
        dv_ext[tm:, :] = dv[:HALO]
        dcw = jnp.concatenate([jnp.sum(dv * shifted[k], axis=0, keepdims=True) for k in range(4)], axis=0)
        dcb = jnp.sum(dv, axis=0, keepdims=True)
        dbr = jnp.sum(drp, axis=0, keepdims=True)
        dbi = jnp.sum(dip, axis=0, keepdims=True)

        @pl.when(i == 0)
        def _():
            dcw_ref[...] = dcw
            dcb_ref[...] = dcb
            dbr_ref[...] = dbr
            dbi_ref[...] = dbi
            dlam_ref[...] = dlam
            wlo_acc[...] = dwlo
            for hd in range(LRU_HEADS):
                dwr_ref[hd] = dwr[hd]
                dwi_ref[hd] = dwi[hd]

        @pl.when(i > 0)
        def _():
            dcw_ref[...] += dcw
            dcb_ref[...] += dcb
            dbr_ref[...] += dbr
            dbi_ref[...] += dbi
            dlam_ref[...] += dlam
            wlo_acc[...] += dwlo
            for hd in range(LRU_HEADS):
                dwr_ref[hd] += dwr[hd]
                dwi_ref[hd] += dwi[hd]

        @pl.when(i == nt - 1)
        def _():
            dwlo_ref[...] = wlo_acc[...].astype(BF)

    fix2 = lambda i: (0, 0)
    fix3 = lambda i: (0, 0, 0)
    rev = lambda i: (nt - 1 - i, 0)
    vec = pl.BlockSpec((1, D), fix2)
    hw = pl.BlockSpec((LRU_HEADS, LRU_HD, LRU_HD), fix3)
    return pl.pallas_call(
        body, name=name, grid=(nt,),
        in_specs=[pl.BlockSpec(memory_space=pl.ANY),
                  pl.BlockSpec((tm, D), rev),
                  pl.BlockSpec((tm, D), lambda i: (nt - 1 - i, ZB_LRU)),
                  pl.BlockSpec((HALO, D), lambda i: (jnp.maximum((nt - 1 - i) * (tm // HALO) - 1, 0), ZB_LRU)),
                  pl.BlockSpec((tm, D), rev),
                  pl.BlockSpec((8, D), lambda i: (jnp.maximum((nt - 1 - i) * (tm // 8) - 1, 0), 0)),
                  pl.BlockSpec((4, D), fix2), vec, hw, vec, hw, vec, vec, pl.BlockSpec((D, D), fix2)],
        out_specs=[pl.BlockSpec((tm, D), lambda i: (nt - 1 - i, ZB_LRU)),
                   pl.BlockSpec((4, D), fix2), vec, hw, vec, hw, vec, vec, pl.BlockSpec((D, D), fix2)],
        out_shape=[jax.ShapeDtypeStruct(dz.shape, dz.dtype), jax.ShapeDtypeStruct((4, D), F32),
                   jax.ShapeDtypeStruct((1, D), F32), jax.ShapeDtypeStruct((LRU_HEADS, LRU_HD, LRU_HD), F32),
                   jax.ShapeDtypeStruct((1, D), F32), jax.ShapeDtypeStruct((LRU_HEADS, LRU_HD, LRU_HD), F32),
                   jax.ShapeDtypeStruct((1, D), F32), jax.ShapeDtypeStruct((1, D), F32),
                   jax.ShapeDtypeStruct((D, D), BF)],
        scratch_shapes=[pltpu.VMEM((tm, D), F32), pltpu.VMEM((tm, D), F32), pltpu.VMEM((tm, D), F32),
                        pltpu.VMEM((8, D), F32), pltpu.VMEM((tm + 8, D), F32), pltpu.VMEM((tm + HALO, D), F32),
                        pltpu.VMEM((tm + HALO, D), F32), pltpu.VMEM((tm + 8, D), F32), pltpu.VMEM((D, D), F32)],
        input_output_aliases={0: 0},
        compiler_params=_cp("arbitrary"))(dz, dyl, z, z, h, h, cw, cb, wr, br, wi, bi, lam, wlo)


def _sconv_cv(u_ext, sw):
    shifted = []
    cv = None
    for k in range(3):
        us = (u_ext if k == 2 else pltpu.roll(u_ext, 2 - k, axis=0))[HALO:]
        shifted.append(us)
        term = sw[k:k + 1, :] * us
        cv = term if cv is None else cv + term
    return cv, shifted


def sconv_fwd(z, sw, wso, *, tm, name):
    T = z.shape[0]
    tm = min(tm, T)

    def body(zm_ref, zh_ref, sw_ref, wso_ref, y_ref):
        i = pl.program_id(0)
        keep = jnp.where(i == 0, 0.0, 1.0).astype(F32)
        zm = zm_ref[...].astype(F32)
        zh = zh_ref[...].astype(F32)
        u_ext = jnp.concatenate([zh[:, D:2 * D] * zh[:, 2 * D:] * keep, zm[:, D:2 * D] * zm[:, 2 * D:]], axis=0)
        cv, _ = _sconv_cv(u_ext, sw_ref[...])
        s = (zm[:, :D] * cv).astype(BF)
        y_ref[...] = jnp.dot(s, wso_ref[...], preferred_element_type=F32).astype(BF)

    return pl.pallas_call(
        body, name=name, grid=(T // tm,),
        in_specs=[pl.BlockSpec((tm, 3 * D), lambda i: (i, ZB_SCONV)),
                  pl.BlockSpec((HALO, 3 * D), _prev_halo(tm, ZB_SCONV)),
                  pl.BlockSpec((3, D), lambda i: (0, 0)), pl.BlockSpec((D, D), lambda i: (0, 0))],
        out_specs=pl.BlockSpec((tm, D), lambda i: (i, 0)),
        out_shape=jax.ShapeDtypeStruct((T, D), BF),
        compiler_params=_cp("arbitrary"))(z, z, sw, wso)


def sconv_bwd(dz, dyc, z, sw, wso, *, tm, name):
    T = z.shape[0]
    tm = min(tm, T)
    nt = T // tm

    def body(dz_in, dy_ref, dyn_ref, zm_ref, zp_ref, zn_ref, sw_ref, wso_ref, dz_ref, dsw_ref, dwso_ref, acc_ref):
        del dz_in
        i = pl.program_id(0)
        keep_p = jnp.where(i == 0, 0.0, 1.0).astype(F32)
        keep_n = jnp.where(i == nt - 1, 0.0, 1.0).astype(F32)
        sw_ = sw_ref[...]
        zm = zm_ref[...].astype(F32)
        zp = zp_ref[...].astype(F32)
        zb, zc, zh = zm[:, :D], zm[:, D:2 * D], zm[:, 2 * D:]
        u_ext = jnp.concatenate([zp[:, D:2 * D] * zp[:, 2 * D:] * keep_p, zc * zh], axis=0)
        cv, shifted = _sconv_cv(u_ext, sw_)
        dy_ext = jnp.concatenate([dy_ref[...], dyn_ref[...]], axis=0)
        ds_ext = lax.dot_general(dy_ext, wso_ref[...], (((1,), (1,)), ((), ())), preferred_element_type=F32)
        zb_ext = jnp.concatenate([zb, zn_ref[...][:, :D].astype(F32) * keep_n], axis=0)
        dcv_ext = ds_ext * zb_ext
        du = sw_[2:3, :] * dcv_ext[:tm]
        for k in range(2):
            du = du + sw_[k:k + 1, :] * pltpu.roll(dcv_ext, tm + HALO - (2 - k), axis=0)[:tm]
        dz_ref[...] = jnp.concatenate([ds_ext[:tm] * cv, du * zh, du * zc], axis=1).astype(dz_ref.dtype)
        dcv = dcv_ext[:tm]
        dsw = jnp.concatenate([jnp.sum(dcv * shifted[k], axis=0, keepdims=True) for k in range(3)], axis=0)
        dwso = lax.dot_general((zb * cv).astype(BF), dy_ref[...], (((0,), (0,)), ((), ())),
                               preferred_element_type=F32)

        @pl.when(i == 0)
        def _():
            dsw_ref[...] = dsw
            acc_ref[...] = dwso

        @pl.when(i > 0)
        def _():
            dsw_ref[...] += dsw
            acc_ref[...] += dwso

        @pl.when(i == nt - 1)
        def _():
            dwso_ref[...] = acc_ref[...].astype(BF)

    return pl.pallas_call(
        body, name=name, grid=(nt,),
        in_specs=[pl.BlockSpec(memory_space=pl.ANY),
                  pl.BlockSpec((tm, D), lambda i: (i, 0)), pl.BlockSpec((HALO, D), _next_halo(tm, T, 0)),
                  pl.BlockSpec((tm, 3 * D), lambda i: (i, ZB_SCONV)),
                  pl.BlockSpec((HALO, 3 * D), _prev_halo(tm, ZB_SCONV)),
                  pl.BlockSpec((HALO, 3 * D), _next_halo(tm, T, ZB_SCONV)),
                  pl.BlockSpec((3, D), lambda i: (0, 0)), pl.BlockSpec((D, D), lambda i: (0, 0))],
        out_specs=[pl.BlockSpec((tm, 3 * D), lambda i: (i, ZB_SCONV)), pl.BlockSpec((3, D), lambda i: (0, 0)),
                   pl.BlockSpec((D, D), lambda i: (0, 0))],
        out_shape=[jax.ShapeDtypeStruct(dz.shape, dz.dtype), jax.ShapeDtypeStruct((3, D), F32),
                   jax.ShapeDtypeStruct((D, D), BF)],
        scratch_shapes=[pltpu.VMEM((D, D), F32)],
        input_output_aliases={0: 0}, compiler_params=_cp("arbitrary"))(dz, dyc, dyc, z, z, z, sw, wso)


def merge_fwd(z, yp_pre, yl, yc, ps, *, tm, name):
    T = z.shape[0]
    tm = min(tm, T)

    def body(zg_ref, yp_ref, yl_ref, yc_ref, ps_ref, o_ref):
        gts = _sigmoid(zg_ref[...].astype(F32))
        m = (gts[:, :D] * (yp_ref[...].astype(F32) * ps_ref[...]) + gts[:, D:2 * D] * yl_ref[...].astype(F32)
             + gts[:, 2 * D:] * yc_ref[...].astype(F32))
        o_ref[...] = m.astype(o_ref.dtype)

    row = lambda i: (i, 0)
    return pl.pallas_call(
        body, name=name, grid=(T // tm,),
        in_specs=[pl.BlockSpec((tm, 3 * D), lambda i: (i, ZB_GATE)), pl.BlockSpec((tm, D), row),
                  pl.BlockSpec((tm, D), row), pl.BlockSpec((tm, D), row), pl.BlockSpec((1, D), lambda i: (0, 0))],
        out_specs=pl.BlockSpec((tm, D), row), out_shape=jax.ShapeDtypeStruct((T, D), BF),
        compiler_params=_cp("arbitrary"))(z, yp_pre, yl, yc, ps)


def merge_bwd(dm, z, yp_pre, yl, yc, ps, *, tm, name):
    T = z.shape[0]
    tm = min(tm, T)

    def body(dm_ref, zg_ref, yp_ref, yl_ref, yc_ref, ps_ref, dz_ref, dyp_ref, dyl_ref, dyc_ref):
        gts = _sigmoid(zg_ref[...].astype(F32))
        dmv = dm_ref[...].astype(F32)
        ys = (yp_ref[...].astype(F32) * ps_ref[...], yl_ref[...].astype(F32), yc_ref[...].astype(F32))
        outs = (dyp_ref, dyl_ref, dyc_ref)
        dgs = []
        for j in range(3):
            gj = gts[:, j * D:(j + 1) * D]
            outs[j][...] = (dmv * gj).astype(BF)
            dgs.append(dmv * ys[j] * gj * (1.0 - gj))
        dz_ref[...] = jnp.concatenate(dgs, axis=1).astype(dz_ref.dtype)

    row = lambda i: (i, 0)
    return pl.pallas_call(
        body, name=name, grid=(T // tm,),
        in_specs=[pl.BlockSpec((tm, D), row), pl.BlockSpec((tm, 3 * D), lambda i: (i, ZB_GATE)),
                  pl.BlockSpec((tm, D), row), pl.BlockSpec((tm, D), row), pl.BlockSpec((tm, D), row),
                  pl.BlockSpec((1, D), lambda i: (0, 0))],
        out_specs=[pl.BlockSpec((tm, 3 * D), lambda i: (i, ZB_GATE)), pl.BlockSpec((tm, D), row),
                   pl.BlockSpec((tm, D), row), pl.BlockSpec((tm, D), row)],
        out_shape=[jax.ShapeDtypeStruct((T, IN_COLS), BF), jax.ShapeDtypeStruct((T, D), BF),
                   jax.ShapeDtypeStruct((T, D), BF), jax.ShapeDtypeStruct((T, D), BF)],
        compiler_params=_cp("arbitrary"))(dm, z, yp_pre, yl, yc, ps)


def _attn_probs(qh, kh):
    s = lax.dot_general(qh, kh, (((1,), (1,)), ((), ())), preferred_element_type=F32) * (X_HD ** -0.5)
    e = jnp.exp(s - jnp.max(s, axis=-1, keepdims=True))
    return e / jnp.sum(e, axis=-1, keepdims=True)


def attn_fwd(xb, wq, kb, vb, *, tm, name):
    T = xb.shape[0]
    tm = min(tm, T)

    def body(x_ref, wq_ref, k_ref, v_ref, q_ref, o_ref):
        q = jnp.dot(x_ref[...], wq_ref[...], preferred_element_type=F32).astype(BF)
        q_ref[...] = q
        outs = []
        for h in range(X_HEADS):
            sl = slice(h * X_HD, (h + 1) * X_HD)
            p = _attn_probs(q[:, sl], k_ref[:, sl])
            outs.append(jnp.dot(p.astype(BF), v_ref[:, sl], preferred_element_type=F32))
        o_ref[...] = jnp.concatenate(outs, axis=1).astype(BF)

    row = lambda i: (i, 0)
    fix = lambda i: (0, 0)
    return pl.pallas_call(
        body, name=name, grid=(T // tm,),
        in_specs=[pl.BlockSpec((tm, D), row), pl.BlockSpec((D, D), fix), pl.BlockSpec((N_MEM, D), fix),
                  pl.BlockSpec((N_MEM, D), fix)],
        out_specs=[pl.BlockSpec((tm, D), row), pl.BlockSpec((tm, D), row)],
        out_shape=[jax.ShapeDtypeStruct((T, D), BF), jax.ShapeDtypeStruct((T, D), BF)],
        compiler_params=_cp("arbitrary"))(xb, wq, kb, vb)


def attn_bwd(dxa, wo, q, kb, vb, xb, *, tm, name):
    T = q.shape[0]
    tm = min(tm, T)
    nt = T // tm

    def body(d_ref, wo_ref, q_ref, k_ref, v_ref, x_ref, dq_ref, dk_ref, dv_ref, dwq_ref, acc_ref):
        i = pl.program_id(0)
        do = lax.dot_general(d_ref[...], wo_ref[...], (((1,), (1,)), ((), ())),
                             preferred_element_type=F32).astype(BF)
        q = q_ref[...]
        dqs, dks, dvs = [], [], []
        for h in range(X_HEADS):
            sl = slice(h * X_HD, (h + 1) * X_HD)
            kh, vh = k_ref[:, sl], v_ref[:, sl]
            p = _attn_probs(q[:, sl], kh)
            dp = lax.dot_general(do[:, sl], vh, (((1,), (1,)), ((), ())), preferred_element_type=F32)
            ds = (p * (dp - jnp.sum(dp * p, axis=-1, keepdims=True)) * (X_HD ** -0.5)).astype(BF)
            dqs.append(jnp.dot(ds, kh, preferred_element_type=F32))
            dks.append(lax.dot_general(ds, q[:, sl], (((0,), (0,)), ((), ())), preferred_element_type=F32))
            dvs.append(lax.dot_general(p.astype(BF), do[:, sl], (((0,), (0,)), ((), ())),
                                       preferred_element_type=F32))
        dqb = jnp.concatenate(dqs, axis=1).astype(BF)
        dq_ref[...] = dqb
        dk = jnp.concatenate(dks, axis=1)
        dv = jnp.concatenate(dvs, axis=1)
        dwq = lax.dot_general(x_ref[...], dqb, (((0,), (0,)), ((), ())), preferred_element_type=F32)

        @pl.when(i == 0)
        def _():
            dk_ref[...] = dk
            dv_ref[...] = dv
            acc_ref[...] = dwq

        @pl.when(i > 0)
        def _():
            dk_ref[...] += dk
            dv_ref[...] += dv
            acc_ref[...] += dwq

        @pl.when(i == nt - 1)
        def _():
            dwq_ref[...] = acc_ref[...].astype(BF)

    row = lambda i: (i, 0)
    fix = lambda i: (0, 0)
    return pl.pallas_call(
        body, name=name, grid=(nt,),
        in_specs=[pl.BlockSpec((tm, D), row), pl.BlockSpec((D, D), fix), pl.BlockSpec((tm, D), row),
                  pl.BlockSpec((N_MEM, D), fix), pl.BlockSpec((N_MEM, D), fix), pl.BlockSpec((tm, D), row)],
        out_specs=[pl.BlockSpec((tm, D), row), pl.BlockSpec((N_MEM, D), fix), pl.BlockSpec((N_MEM, D), fix),
                   pl.BlockSpec((D, D), fix)],
        out_shape=[jax.ShapeDtypeStruct((T, D), BF), jax.ShapeDtypeStruct((N_MEM, D), F32),
                   jax.ShapeDtypeStruct((N_MEM, D), F32), jax.ShapeDtypeStruct((D, D), BF)],
        scratch_shapes=[pltpu.VMEM((D, D), F32)],
        compiler_params=_cp("arbitrary"))(dxa, wo, q, kb, vb, xb)


TM_MM = 1024
TM_EW = 512
TM_SEQ = 512
TT_DW = 2048


def _mem_kv(l, memb, W):
    kb = mm_nn(memb, W['xa_w_k'][l], None, out_dtype=BF, tm=N_MEM, tn=1024, name=f"l{l}_mem_k")
    vb = mm_nn(memb, W['xa_w_v'][l], None, out_dtype=BF, tm=N_MEM, tn=1024, name=f"l{l}_mem_v")
    return kb, vb


def _layer_fwd(l, x, xb, kb, vb, W, host=None, host2=None, after_in_proj=None):
    n = f"l{l}_"
    sv = {'x0': x if xb is None else xb}
    z = mm_nn(sv['x0'], W['w_in'][l], W['b_in'][l], out_dtype=BF, tm=2 * TM_MM, tn=1024, name=n + "in_proj",
              host=host)
    if host is not None:
        z, sv['hosted'] = z
        if after_in_proj is not None:
            after_in_proj(sv['hosted'])
    if kb is None:
        kb, vb = _mem_kv(l, vb, W)
    sv['kb'], sv['vb'] = kb, vb
    yp = pool_fwd(z, W['pool_w'][l], tm=TM_SEQ, name=n + "pool_fwd")
    h, yl = lru_fwd(z, W['lru_conv_w'][l], W['lru_conv_b'][l], W['lru_w_r'][l], W['lru_b_r'][l], W['lru_w_i'][l],
                    W['lru_b_i'][l], W['lru_lambda'][l], W['lru_w_out'][l], tm=TM_SEQ, name=n + "lru_fwd")
    yc = sconv_fwd(z, W['sconv_w'][l], W['sconv_w_out'][l], tm=TM_SEQ, name=n + "sconv_fwd")
    merged = merge_fwd(z, yp, yl, yc, W['pool_scale'][l], tm=TM_EW, name=n + "merge_fwd")
    x1, x1b, xh1, rs1 = mm_res_ln(merged, W['w_mix_out'][l], x, W['ln_g'][l][0:1], W['ln_b'][l][0:1], tm=TM_MM,
                                  name=n + "mix_out_ln")
    q, o = attn_fwd(x1b, W['xa_w_q'][l], kb, vb, tm=TM_EW, name=n + "attn_fwd")
    x2, x2b, xh2, rs2 = mm_res_ln(o, W['xa_w_o'][l], x1, W['ln_g'][l][1:2], W['ln_b'][l][1:2], tm=TM_MM,
                                  name=n + "attn_out_ln")
    res = ffn_in_swiglu(x2b, W['ffn_w_gu'][l], tm=TM_MM, name=n + "ffn_in", host=host2)
    gu, hdn = res[:2]
    if host2 is not None:
        sv['hosted2'] = res[2]
    x3, x3b, xh3, rs3 = mm_res_ln(hdn, W['ffn_w_down'][l], x2, W['ln_g'][l][2:3], W['ln_b'][l][2:3], tm=TM_EW,
                                  name=n + "ffn_out_ln")
    sv.update(z=z, yp=yp, h=h, yl=yl, yc=yc, merged=merged, x1b=x1b, xh1=xh1, rs1=rs1, q=q, o=o, x2b=x2b,
              xh2=xh2, rs2=rs2, gu=gu, hdn=hdn, xh3=xh3, rs3=rs3)
    return x3, x3b, sv


def _layer_bwd(l, dx3, sv, memb, kb, vb, W, loss_from=None, host=None, last_host_fn=None):
    n = f"l{l}_"
    G = {}
    res = ln_bwd(dx3, sv['xh3'], sv['rs3'], W['ln_g'][l][2:3], tm=TM_EW, name=n + "ln3_bwd", loss_from=loss_from,
                 dw_of=sv['hdn'])
    dp3, dp3b, dg3, db3 = res[:4]
    loss = res[4] if loss_from is not None else None
    G['ffn_w_down'] = res[-1]
    dgu = ffn_down_dx_swiglu(dp3b, W['ffn_w_down'][l], sv['gu'], tm=TM_EW, name=n + "ffn_down_dx")
    dx2 = mm_nt(dgu, W['ffn_w_gu'][l], dp3, out_dtype=F32, tm=TM_MM, tc=1408, name=n + "ffn_in_dx", host=host)
    if host is not None:
        dx2, G['hosted'] = dx2
    G['ffn_w_gu'] = mm_tn(sv['x2b'], dgu, out_dtype=BF, tk=1024, tn=1408, tt=TT_DW,name=n + "ffn_in_dw")

    dp2, dp2b, dg2, db2, G['xa_w_o'] = ln_bwd(dx2, sv['xh2'], sv['rs2'], W['ln_g'][l][1:2], tm=TM_EW,
                                              name=n + "ln2_bwd", dw_of=sv['o'])
    dq, dk, dv, G['xa_w_q'] = attn_bwd(dp2b, W['xa_w_o'][l], sv['q'], kb, vb, sv['x1b'], tm=TM_EW,
                                       name=n + "attn_bwd")
    dx1 = mm_nt(dq, W['xa_w_q'][l], dp2, out_dtype=F32, tm=TM_MM, tc=D, name=n + "attn_q_dx")
    G['xa_w_k'] = mm_tn(memb, dk, out_dtype=BF, tk=1024, tn=1024, tt=N_MEM, name=n + "attn_k_dw")
    G['xa_w_v'] = mm_tn(memb, dv, out_dtype=BF, tk=1024, tn=1024, tt=N_MEM, name=n + "attn_v_dw")

    dp1, dp1b, dg1, db1, G['w_mix_out'] = ln_bwd(dx1, sv['xh1'], sv['rs1'], W['ln_g'][l][0:1], tm=TM_EW,
                                                 name=n + "ln1_bwd", dw_of=sv['merged'])
    dmerged = mm_nt(dp1b, W['w_mix_out'][l], None, out_dtype=BF, tm=TM_MM, tc=D, name=n + "mix_out_dx")
    z = sv['z']
    dz, dyp, dyl, dyc = merge_bwd(dmerged, z, sv['yp'], sv['yl'], sv['yc'], W['pool_scale'][l], tm=TM_EW,
                                  name=n + "merge_bwd")
    dz, G['pool_w'], G['pool_scale'] = pool_bwd(dz, dyp, sv['yp'], z, W['pool_w'][l], W['pool_scale'][l],
                                                tm=TM_SEQ, name=n + "pool_bwd")
    (dz, G['lru_conv_w'], G['lru_conv_b'], G['lru_w_r'], G['lru_b_r'], G['lru_w_i'], G['lru_b_i'], G['lru_lambda'],
     G['lru_w_out']) = lru_bwd(dz, dyl, z, sv['h'], W['lru_conv_w'][l], W['lru_conv_b'][l], W['lru_w_r'][l],
                               W['lru_b_r'][l], W['lru_w_i'][l], W['lru_b_i'][l], W['lru_lambda'][l],
                               W['lru_w_out'][l], tm=TM_SEQ, name=n + "lru_bwd")
    dz, G['sconv_w'], G['sconv_w_out'] = sconv_bwd(dz, dyc, z, W['sconv_w'][l], W['sconv_w_out'][l], tm=TM_SEQ,
                                                   name=n + "sconv_bwd")
    G['w_in'], G['b_in'] = mm_tn(sv['x0'], dz, out_dtype=BF, tk=1024, tn=1024, tt=TT_DW, name=n + "in_proj_dw",
                                 colsum=True)
    G['ln_g'] = jnp.concatenate([dg1, dg2, dg3], axis=0)
    G['ln_b'] = jnp.concatenate([db1, db2, db3], axis=0)
    last_host = None if last_host_fn is None else last_host_fn(G)
    dx0 = mm_nt(dz, W['w_in'][l], dp1, out_dtype=F32, tm=TM_MM, tc=2048, name=n + "in_proj_dx", host=last_host)
    if last_host is not None:
        dx0, G['hosted_last'] = dx0
    return dx0, G, loss


def local_step(x, mem, target, W):
    memb = mem.astype(BF)
    saves, kvs = [], []
    xf, xb = x, None
    for l in range(DEPTH):
        kb = mm_nn(memb, W['xa_w_k'][l], None, out_dtype=BF, tm=N_MEM, tn=1024, name=f"l{l}_mem_k")
        vb = mm_nn(memb, W['xa_w_v'][l], None, out_dtype=BF, tm=N_MEM, tn=1024, name=f"l{l}_mem_v")
        xf, xb, sv = _layer_fwd(l, xf, xb, kb, vb, W)
        saves.append(sv)
        kvs.append((kb, vb))
    grads = [None] * DEPTH
    dx, loss = None, None
    for l in reversed(range(DEPTH)):
        lf = (W['ln_b'][l][2:3], target) if l == DEPTH - 1 else None
        dx, grads[l], ls = _layer_bwd(l, dx, saves[l], memb, kvs[l][0], kvs[l][1], W, loss_from=lf)
        if ls is not None:
            loss = ls
    return loss, dx, grads


def _coords():
    return lax.axis_index("x"), lax.axis_index("y"), lax.axis_index("c")


FLIPS = ((1, 0), (0, 1), (1, 1))
SQUARES = ('lru_w_out', 'sconv_w_out', 'w_mix_out', 'xa_w_q', 'xa_w_k', 'xa_w_v', 'xa_w_o')
LAYER_SHAPE = {'w_in': (D, IN_COLS), 'pool_w': (4, POOL_GD, POOL_GD), 'ffn_w_gate': (4, D, D_FF // 4),
               'ffn_w_up': (4, D, D_FF // 4), 'ffn_w_down': (D_FF, D), **{n: (D, D) for n in SQUARES}}
PIECES = ('w_in', 'pool_w') + SQUARES + ('ffn_w_gate', 'ffn_w_up', 'ffn_w_down')


def _mult(v, m):
    return v if isinstance(v, int) else pl.multiple_of(v, m)


def _win(name, ref, k):
    if name == 'w_in':
        return ref.at[:, pl.ds(_mult(((2 * k + 6) % 8) * D, D), 2 * D)]
    if name == 'pool_w':
        return ref.at[:, pl.ds(_mult(k * (POOL_GD // 4), POOL_GD // 4), POOL_GD // 4), :]
    if name in ('ffn_w_gate', 'ffn_w_up'):
        return ref.at[k]
    rows = LAYER_SHAPE[name][0] // 4
    return ref.at[pl.ds(_mult(k * rows, 16), rows), :]


def _half_shape(name):
    shard = _shard_shape(name)
    return (shard[0] // 2,) + shard[1:]


def _half(name, ref, h):
    rows = _shard_shape(name)[0] // 2
    if name == 'pool_w':
        return ref.at[pl.ds(h * rows, rows)]
    return ref.at[pl.ds(_mult(h * rows, 16), rows), :]


def _shard_shape(n):
    shp = LAYER_SHAPE[n]
    if n == 'w_in':
        return (shp[0], shp[1] // 4)
    if n == 'pool_w':
        return (shp[0], shp[1] // 4, shp[2])
    if n in ('ffn_w_gate', 'ffn_w_up'):
        return shp[1:]
    return (shp[0] // 4, shp[1])


def swap_cores(s):
    n = len(s)

    def body(*refs):
        srcs, outs = refs[:n], refs[n:2 * n]
        send_sems, recv_sems = refs[2 * n:]
        x, y, c = _coords()
        cps = [pltpu.make_async_remote_copy(src_ref=srcs[i], dst_ref=outs[i], send_sem=send_sems.at[i],
                                            recv_sem=recv_sems.at[i], device_id=(x, y, 1 - c), device_id_type=MESH)
               for i in range(n)]
        for cp in cps:
            cp.start()
        for cp in cps:
            cp.wait()

    hbm = pl.BlockSpec(memory_space=pl.ANY)
    return pl.pallas_call(
        body, name="swap_cores", in_specs=[hbm] * n, out_specs=[hbm] * n,
        out_shape=[jax.ShapeDtypeStruct(a.shape, a.dtype) for a in s],
        scratch_shapes=[pltpu.SemaphoreType.DMA((n,)), pltpu.SemaphoreType.DMA((n,))],
    )(*s)


def sum_chips_small(q, recv3, chip):
    r, C = q.shape
    slot_of_xor = {2: 0, 1: 1, 3: 2}

    def body(me_ref, q_ref, r_ref, o_ref):
        me = me_ref[0]
        acc = None
        for k in range(4):
            kx = k ^ me
            term = q_ref[...]
            for xv, j in slot_of_xor.items():
                term = jnp.where(kx == xv, r_ref[j], term)
            acc = term if acc is None else acc + term
        o_ref[...] = acc

    return pl.pallas_call(
        body, name="sum_chips_small",
        grid_spec=pltpu.PrefetchScalarGridSpec(
            num_scalar_prefetch=1, grid=(1,),
            in_specs=[pl.BlockSpec((r, C), lambda i, me: (0, 0)), pl.BlockSpec((3, r, C), lambda i, me: (0, 0, 0))],
            out_specs=pl.BlockSpec((r, C), lambda i, me: (0, 0))),
        out_shape=jax.ShapeDtypeStruct((r, C), F32), compiler_params=_cp("arbitrary"))(chip, q, recv3)


def _row_tile(R, C):
    for cand in (1024, 512, 256, 128, 64, 32, 16):
        if R % cand == 0 and cand * C * 4 <= 2 * 1024 * 1024:
            return cand
    return R


def _adamw_math(w, g, m, v):
    mn = ADAM_B1 * m + (1.0 - ADAM_B1) * g
    vn = ADAM_B2 * v + (1.0 - ADAM_B2) * (g * g)
    m_hat = mn / (1.0 - ADAM_B1 ** ADAM_STEP)
    v_hat = vn / (1.0 - ADAM_B2 ** ADAM_STEP)
    return -ADAM_LR * (m_hat / (jnp.sqrt(v_hat) + ADAM_EPS) + ADAM_WD * w), mn, vn


def adamw(w, g, m, v, *, name):
    shp = w.shape
    args = [t.reshape(-1, shp[-1]) for t in (w, g, m, v)]
    R, C = args[0].shape
    tr = _row_tile(R, C)

    def body(w_ref, g_ref, m_ref, v_ref, d_ref, mo_ref, vo_ref):
        d_ref[...], mo_ref[...], vo_ref[...] = _adamw_math(w_ref[...], g_ref[...], m_ref[...], v_ref[...])

    spec = pl.BlockSpec((tr, C), lambda i: (i, 0))
    res = pl.pallas_call(
        body, name=name, grid=(R // tr,), in_specs=[spec] * 4, out_specs=[spec] * 3,
        out_shape=[jax.ShapeDtypeStruct((R, C), F32)] * 3, compiler_params=_cp("arbitrary"))(*args)
    return [r.reshape(shp) for r in res]


def _remote(src, dst, send_sems, recv_sems, k, peer):
    return pltpu.make_async_remote_copy(src_ref=src, dst_ref=dst, send_sem=send_sems.at[k], recv_sem=recv_sems.at[k],
                                        device_id=peer, device_id_type=MESH)


def gather_layer(l, shards, small=None, pieces=PIECES):
    n_p = len(pieces)
    with_small = small is not None

    def descs(h_in, h_out, sems):
        srcs = dict(zip(pieces, h_in[:n_p]))
        outs = dict(zip(pieces, h_out[:n_p]))
        ici_s, ici_r, d2d_s, d2d_r, own_s, own_r = sems
        x, y, c = _coords()
        me = 2 * x + y
        sib = (x, y, 1 - c)
        ici, fwd, fwd_in, own = [], [], [], []
        for j, (fx, fy) in enumerate(FLIPS):
            peer = (x ^ fx, y ^ fy, c)
            other = 2 * (x ^ fx) + (y ^ fy)
            for p, n in enumerate(pieces):
                k = 3 * p + j
                mine = _half(n, _win(n, outs[n], me), c)
                landed = _half(n, _win(n, outs[n], other), c)
                sib_half = _half(n, _win(n, outs[n], other), 1 - c)
                ici.append((_remote(_half(n, srcs[n].at[l], c), mine, ici_s, ici_r, k, peer),
                            _remote(_half(n, srcs[n].at[l], c), landed, ici_s, ici_r, k, peer)))
                fwd.append(_remote(landed, landed, d2d_s, d2d_r, k, sib))
                fwd_in.append(_remote(sib_half, sib_half, d2d_s, d2d_r, k, sib))
            if with_small:
                k = 3 * n_p + j
                ici.append((_remote(h_in[n_p], h_out[n_p].at[me], ici_s, ici_r, k, peer),
                            _remote(h_in[n_p], h_out[n_p].at[other], ici_s, ici_r, k, peer)))
        for p, n in enumerate(pieces):
            own.append(_remote(srcs[n].at[l], _win(n, outs[n], me), own_s, own_r, p, sib))
        if with_small:
            own.append(_remote(h_in[n_p], h_out[n_p].at[me], own_s, own_r, n_p, sib))
        return ici, fwd, fwd_in, own

    def start(h_in, h_out, sems):
        ici, _, _, own = descs(h_in, h_out, sems)
        for send, _ in ici:
            send.start()
        for cp in own:
            cp.start()

    def finish(h_in, h_out, sems):
        ici, fwd, fwd_in, own = descs(h_in, h_out, sems)
        per_chip = n_p + (1 if with_small else 0)
        for j in range(3):
            for p in range(n_p):
                ici[j * per_chip + p][1].wait_recv()
                fwd[j * n_p + p].start()
            if with_small:
                ici[j * per_chip + n_p][1].wait_recv()
        for cp in fwd_in:
            cp.wait_recv()
        for send, _ in ici:
            send.wait_send()
        for cp in fwd:
            cp.wait_send()
        for cp in own:
            cp.wait()

    arrays = [shards[n] for n in pieces] + ([small] if with_small else [])
    out_shape = [jax.ShapeDtypeStruct(LAYER_SHAPE[n], BF) for n in pieces]
    if with_small:
        out_shape.append(jax.ShapeDtypeStruct((4,) + small.shape, small.dtype))
    sems = [pltpu.SemaphoreType.DMA((3 * n_p + 3,)), pltpu.SemaphoreType.DMA((3 * n_p + 3,)),
            pltpu.SemaphoreType.DMA((3 * n_p,)), pltpu.SemaphoreType.DMA((3 * n_p,)),
            pltpu.SemaphoreType.DMA((n_p + 1,)), pltpu.SemaphoreType.DMA((n_p + 1,))]
    return Hosted(arrays, out_shape, sems, start, finish)


def both_hosted(h1, h2):
    a1, o1, s1 = len(h1.arrays), len(h1.out_shape), len(h1.sems)

    def start(h_in, h_out, sems):
        h1.start(h_in[:a1], h_out[:o1], sems[:s1])
        h2.start(h_in[a1:], h_out[o1:], sems[s1:])

    def finish(h_in, h_out, sems):
        h1.finish(h_in[:a1], h_out[:o1], sems[:s1])
        h2.finish(h_in[a1:], h_out[o1:], sems[s1:])

    return Hosted(list(h1.arrays) + list(h2.arrays), list(h1.out_shape) + list(h2.out_shape),
                  list(h1.sems) + list(h2.sems), start, finish)


def run_hosted(host, name):
    n_in, n_out = len(host.arrays), len(host.out_shape)

    def body(*refs):
        h_in, h_out, sems = refs[:n_in], refs[n_in:n_in + n_out], refs[n_in + n_out:]
        host.start(h_in, h_out, sems)
        host.finish(h_in, h_out, sems)

    return pl.pallas_call(body, name=name, in_specs=[HBM_SPEC] * n_in, out_specs=[HBM_SPEC] * n_out,
                          out_shape=list(host.out_shape), scratch_shapes=list(host.sems))(*host.arrays)


def split_halves(parts, small):
    n_p = len(PIECES)
    rh = small.shape[0] // 2

    def body(*refs):
        srcs = dict(zip(PIECES, refs[:n_p]))
        s_ref = refs[n_p]
        outs = dict(zip(PIECES, refs[n_p + 1:2 * n_p + 1]))
        so_ref = refs[2 * n_p + 1]
        send_sems, recv_sems = refs[2 * n_p + 2:]
        x, y, c = _coords()
        sib = (x, y, 1 - c)
        cps = []
        for p, n in enumerate(PIECES):
            for k in range(4):
                cps.append(_remote(_half(n, _win(n, srcs[n], k), 1 - c), outs[n].at[k], send_sems, recv_sems,
                                   4 * p + k, sib))
        cps.append(_remote(s_ref.at[pl.ds(_mult((1 - c) * rh, 8), rh), :], so_ref, send_sems, recv_sems, 4 * n_p, sib))
        for cp in cps:
            cp.start()
        for cp in cps:
            cp.wait()

    res = pl.pallas_call(
        body, name="split_halves", in_specs=[HBM_SPEC] * (n_p + 1), out_specs=[HBM_SPEC] * (n_p + 1),
        out_shape=[jax.ShapeDtypeStruct((4,) + _half_shape(n), BF) for n in PIECES]
        + [jax.ShapeDtypeStruct((rh, small.shape[1]), small.dtype)],
        scratch_shapes=[pltpu.SemaphoreType.DMA((4 * n_p + 1,)), pltpu.SemaphoreType.DMA((4 * n_p + 1,))],
    )(*[parts[n] for n in PIECES], small)
    return dict(zip(PIECES, res[:n_p])), res[n_p]


SAME_SHAPE = (('w_in',), ('pool_w',), SQUARES, ('ffn_w_gate', 'ffn_w_up'), ('ffn_w_down',))


def add_halves(names, parts, theirs, core):
    name = names[0]
    half = _half_shape(name)
    zero = (0,) * len(half)
    if name == 'w_in':
        pspec = pl.BlockSpec(half, lambda k, cc: (cc[0], (k + 3) % 4))
    elif name == 'pool_w':
        pspec = pl.BlockSpec(half, lambda k, cc: (cc[0], k, 0))
    elif name in ('ffn_w_gate', 'ffn_w_up'):
        pspec = pl.BlockSpec((None,) + half, lambda k, cc: (k, cc[0], 0))
    else:
        pspec = pl.BlockSpec(half, lambda k, cc: (2 * k + cc[0], 0))
    slot = pl.BlockSpec((None,) + half, lambda k, cc: (k,) + zero)
    m = len(names)

    def body(cc_ref, *refs):
        del cc_ref
        for i in range(m):
            refs[2 * m + i][...] = (refs[i][...].astype(F32) + refs[m + i][...].astype(F32)).astype(BF)

    res = pl.pallas_call(
        body, name="add_cores_" + name,
        grid_spec=pltpu.PrefetchScalarGridSpec(num_scalar_prefetch=1, grid=(4,), in_specs=[pspec] * m + [slot] * m,
                                               out_specs=[slot] * m),
        out_shape=[jax.ShapeDtypeStruct((4,) + half, BF)] * m, compiler_params=_cp("arbitrary"))(
            core, *[parts[n] for n in names], *[theirs[n] for n in names])
    return dict(zip(names, res))


def add_halves_small(small, theirs, core):
    rh, C = theirs.shape

    def body(cc_ref, p_ref, t_ref, o_ref):
        del cc_ref
        o_ref[...] = p_ref[...] + t_ref[...]

    blk = pl.BlockSpec((rh, C), lambda i, cc: (0, 0))
    return pl.pallas_call(
        body, name="add_cores_small",
        grid_spec=pltpu.PrefetchScalarGridSpec(
            num_scalar_prefetch=1, grid=(1,),
            in_specs=[pl.BlockSpec((rh, C), lambda i, cc: (cc[0], 0)), blk], out_specs=blk),
        out_shape=jax.ShapeDtypeStruct((rh, C), F32), compiler_params=_cp("arbitrary"))(core, small, theirs)


def exchange_halves(q, qsmall):
    n_p = len(PIECES)

    def descs(h_in, h_out, sems):
        send_sems, recv_sems = sems
        x, y, c = _coords()
        cps = []
        for j, (fx, fy) in enumerate(FLIPS):
            peer = (x ^ fx, y ^ fy, c)
            other = 2 * (x ^ fx) + (y ^ fy)
            for p in range(n_p):
                cps.append(_remote(h_in[p].at[other], h_out[p].at[j], send_sems, recv_sems, 3 * p + j, peer))
            cps.append(_remote(h_in[n_p], h_out[n_p].at[j], send_sems, recv_sems, 3 * n_p + j, peer))
        return cps

    def start(h_in, h_out, sems):
        for cp in descs(h_in, h_out, sems):
            cp.start()

    def finish(h_in, h_out, sems):
        for cp in descs(h_in, h_out, sems):
            cp.wait()

    arrays = [q[n] for n in PIECES] + [qsmall]
    out_shape = [jax.ShapeDtypeStruct((3,) + _half_shape(n), BF) for n in PIECES]
    out_shape.append(jax.ShapeDtypeStruct((3,) + qsmall.shape, qsmall.dtype))
    sems = [pltpu.SemaphoreType.DMA((3 * n_p + 3,)), pltpu.SemaphoreType.DMA((3 * n_p + 3,))]
    return Hosted(arrays, out_shape, sems, start, finish)


def sum_halves(names, q, recv3, chip):
    half = _half_shape(names[0])
    zero = (0,) * len(half)
    m = len(names)

    def body(me_ref, *refs):
        del me_ref
        for i in range(m):
            acc = refs[i][...].astype(F32)
            for j in range(3):
                acc = acc + refs[m + i][j].astype(F32)
            refs[2 * m + i][...] = acc

    res = pl.pallas_call(
        body, name="sum_chips_" + names[0],
        grid_spec=pltpu.PrefetchScalarGridSpec(
            num_scalar_prefetch=1, grid=(1,),
            in_specs=[pl.BlockSpec((None,) + half, lambda i, me: (me[0],) + zero)] * m
            + [pl.BlockSpec((3,) + half, lambda i, me: (0,) + zero)] * m,
            out_specs=[pl.BlockSpec(half, lambda i, me: zero)] * m),
        out_shape=[jax.ShapeDtypeStruct(half, F32)] * m, compiler_params=_cp("arbitrary"))(
            chip, *[q[n] for n in names], *[recv3[n] for n in names])
    return dict(zip(names, res))


def adamw_halves(w, g0, m, v, *, name):
    shp = w.shape
    C = shp[-1]
    four = (DEPTH, 2, -1, C)
    w4, m4, v4 = [t.reshape(four) for t in (w, m, v)]
    gs = [t.reshape(-1, C) for pair in g0 for t in pair]
    Rh = gs[0].shape[0]
    tr = _row_tile(Rh, C)

    def body(w_ref, a0_ref, b0_ref, a1_ref, b1_ref, m_ref, v_ref, g_ref, d_ref, mo_ref, vo_ref):
        mine = pl.program_id(1) == lax.axis_index("c")
        g_l0 = jnp.where(mine, a0_ref[...], b0_ref[...])
        g_l1 = jnp.where(mine, a1_ref[...], b1_ref[...])
        g = jnp.where(pl.program_id(0) == 0, g_l0, g_l1)
        g_ref[...] = g
        d_ref[...], mo_ref[...], vo_ref[...] = _adamw_math(w_ref[...], g, m_ref[...], v_ref[...])

    lay = pl.BlockSpec((None, None, tr, C), lambda l, h, i: (l, h, i, 0))
    one = pl.BlockSpec((tr, C), lambda l, h, i: (i, 0))
    res = pl.pallas_call(
        body, name=name, grid=(DEPTH, 2, Rh // tr), in_specs=[lay, one, one, one, one, lay, lay],
        out_specs=[lay] * 4, out_shape=[jax.ShapeDtypeStruct(w4.shape, F32)] * 4,
        compiler_params=_cp("arbitrary", "arbitrary", "arbitrary"))(w4, *gs, m4, v4)
    return [r.reshape(shp) for r in res]


def _local_shape(name, full_shape):
    shp = list(full_shape)
    ax = BIG_SHARDED.get(name, SMALL_SHARDED.get(name))
    if ax is not None:
        shp[ax] //= 4
    return tuple(shp)


FULL_SHAPES = {
    'w_in': (DEPTH, D, IN_COLS), 'b_in': (DEPTH, IN_COLS), 'pool_w': (DEPTH, 4, POOL_GD, POOL_GD),
    'pool_scale': (DEPTH, D), 'lru_conv_w': (DEPTH, 4, D), 'lru_conv_b': (DEPTH, D),
    'lru_w_r': (DEPTH, LRU_HEADS, LRU_HD, LRU_HD), 'lru_b_r': (DEPTH, D),
    'lru_w_i': (DEPTH, LRU_HEADS, LRU_HD, LRU_HD), 'lru_b_i': (DEPTH, D), 'lru_lambda': (DEPTH, D),
    'lru_w_out': (DEPTH, D, D), 'sconv_w': (DEPTH, 3, D), 'sconv_w_out': (DEPTH, D, D), 'w_mix_out': (DEPTH, D, D),
    'xa_w_q': (DEPTH, D, D), 'xa_w_k': (DEPTH, D, D), 'xa_w_v': (DEPTH, D, D), 'xa_w_o': (DEPTH, D, D),
    'ffn_w_gate': (DEPTH, D, D_FF), 'ffn_w_up': (DEPTH, D, D_FF), 'ffn_w_down': (DEPTH, D_FF, D),
    'ln_g': (DEPTH, 3, D), 'ln_b': (DEPTH, 3, D)}


def _pack(arrs, names, width, dtype, row_mult):
    flat = jnp.concatenate([arrs[n].astype(dtype).reshape(-1) for n in names])
    pad = (-flat.shape[0]) % (width * row_mult)
    if pad:
        flat = jnp.concatenate([flat, jnp.zeros((pad,), dtype)])
    return flat.reshape(-1, width)


def _unpack(flat2d, names, shapes):
    flat = flat2d.reshape(-1)
    out, off = {}, 0
    for n in names:
        size = 1
        for s in shapes[n]:
            size *= s
        out[n] = flat[off:off + size].reshape(shapes[n])
        off += size
    return out


def _gathered_full(g4, names, sharded_axis):
    loc_shapes = {n: _local_shape(n, FULL_SHAPES[n]) for n in names}
    per = [_unpack(g4[k], names, loc_shapes) for k in range(4)]
    return {n: jnp.concatenate([per[k][n] for k in range(4)], axis=sharded_axis[n]) for n in names}


def _gu_joined(g4, u4):
    return jnp.concatenate([g4[0], g4[1], u4[0], u4[1], g4[2], g4[3], u4[2], u4[3]], axis=1)


def _gu_apart(a):
    w = a.shape[1] // 8
    cols = [a[:, i * w:(i + 1) * w] for i in range(8)]
    return jnp.stack([cols[0], cols[1], cols[4], cols[5]]), jnp.stack([cols[2], cols[3], cols[6], cols[7]])


SMALL_SH_NAMES = list(SMALL_SHARDED)
SMALL_ROWS = 32


def kernel(x, mem, w_in, b_in, pool_w, pool_scale, lru_conv_w, lru_conv_b, lru_w_r, lru_b_r, lru_w_i, lru_b_i, lru_lambda, lru_w_out, sconv_w, sconv_w_out, w_mix_out, xa_w_q, xa_w_k, xa_w_v, xa_w_o, ffn_w_gate, ffn_w_up, ffn_w_down, ln_g, ln_b, loss_target, m_w_in, m_b_in, m_pool_w, m_pool_scale, m_lru_conv_w, m_lru_conv_b, m_lru_w_r, m_lru_b_r, m_lru_w_i, m_lru_b_i, m_lru_lambda, m_lru_w_out, m_sconv_w, m_sconv_w_out, m_w_mix_out, m_xa_w_q, m_xa_w_k, m_xa_w_v, m_xa_w_o, m_ffn_w_gate, m_ffn_w_up, m_ffn_w_down, m_ln_g, m_ln_b, v_w_in, v_b_in, v_pool_w, v_pool_scale, v_lru_conv_w, v_lru_conv_b, v_lru_w_r, v_lru_b_r, v_lru_w_i, v_lru_b_i, v_lru_lambda, v_lru_w_out, v_sconv_w, v_sconv_w_out, v_w_mix_out, v_xa_w_q, v_xa_w_k, v_xa_w_v, v_xa_w_o, v_ffn_w_gate, v_ffn_w_up, v_ffn_w_down, v_ln_g, v_ln_b):
    loc = dict(w_in=w_in, b_in=b_in, pool_w=pool_w, pool_scale=pool_scale, lru_conv_w=lru_conv_w,
               lru_conv_b=lru_conv_b, lru_w_r=lru_w_r, lru_b_r=lru_b_r, lru_w_i=lru_w_i, lru_b_i=lru_b_i,
               lru_lambda=lru_lambda, lru_w_out=lru_w_out, sconv_w=sconv_w, sconv_w_out=sconv_w_out,
               w_mix_out=w_mix_out, xa_w_q=xa_w_q, xa_w_k=xa_w_k, xa_w_v=xa_w_v, xa_w_o=xa_w_o,
               ffn_w_gate=ffn_w_gate, ffn_w_up=ffn_w_up, ffn_w_down=ffn_w_down, ln_g=ln_g, ln_b=ln_b)
    mom = dict(w_in=m_w_in, b_in=m_b_in, pool_w=m_pool_w, pool_scale=m_pool_scale, lru_conv_w=m_lru_conv_w,
               lru_conv_b=m_lru_conv_b, lru_w_r=m_lru_w_r, lru_b_r=m_lru_b_r, lru_w_i=m_lru_w_i, lru_b_i=m_lru_b_i,
               lru_lambda=m_lru_lambda, lru_w_out=m_lru_w_out, sconv_w=m_sconv_w, sconv_w_out=m_sconv_w_out,
               w_mix_out=m_w_mix_out, xa_w_q=m_xa_w_q, xa_w_k=m_xa_w_k, xa_w_v=m_xa_w_v, xa_w_o=m_xa_w_o,
               ffn_w_gate=m_ffn_w_gate, ffn_w_up=m_ffn_w_up, ffn_w_down=m_ffn_w_down, ln_g=m_ln_g, ln_b=m_ln_b)
    var = dict(w_in=v_w_in, b_in=v_b_in, pool_w=v_pool_w, pool_scale=v_pool_scale, lru_conv_w=v_lru_conv_w,
               lru_conv_b=v_lru_conv_b, lru_w_r=v_lru_w_r, lru_b_r=v_lru_b_r, lru_w_i=v_lru_w_i, lru_b_i=v_lru_b_i,
               lru_lambda=v_lru_lambda, lru_w_out=v_lru_w_out, sconv_w=v_sconv_w, sconv_w_out=v_sconv_w_out,
               w_mix_out=v_w_mix_out, xa_w_q=v_xa_w_q, xa_w_k=v_xa_w_k, xa_w_v=v_xa_w_v, xa_w_o=v_xa_w_o,
               ffn_w_gate=v_ffn_w_gate, ffn_w_up=v_ffn_w_up, ffn_w_down=v_ffn_w_down, ln_g=v_ln_g, ln_b=v_ln_b)

    chip = 2 * lax.axis_index("x") + lax.axis_index("y")
    core = lax.axis_index("c")
    chip_arr = jnp.reshape(chip, (1,)).astype(jnp.int32)
    core_arr = jnp.reshape(core, (1,)).astype(jnp.int32)
    n_p = len(PIECES)

    shards = {n: loc[n].astype(BF) for n in PIECES}
    small = _pack(loc, SMALL_SH_NAMES, 256, F32, 8)
    first = ('w_in',)
    rest = tuple(n for n in PIECES if n not in first)
    early1 = ('w_in', 'pool_w')
    late1 = tuple(n for n in PIECES if n not in early1)
    got = run_hosted(gather_layer(0, shards, small, pieces=first), "gather_first")
    vec = _gathered_full(got[len(first)], SMALL_SH_NAMES, SMALL_SHARDED)
    W = {n: [None] * DEPTH for n in ('w_in', 'pool_w', 'ffn_w_down', 'ffn_w_gu') + SQUARES}
    W['b_in'] = [jnp.roll(b_in[l:l + 1], -2 * D, axis=1) for l in range(DEPTH)]
    for n in ('lru_conv_w', 'sconv_w', 'ln_g', 'ln_b'):
        W[n] = [vec[n][l] for l in range(DEPTH)]
    for n in ('lru_w_r', 'lru_w_i'):
        W[n] = [loc[n][l].astype(BF) for l in range(DEPTH)]
    for n in ('pool_scale', 'lru_conv_b', 'lru_b_r', 'lru_b_i', 'lru_lambda'):
        W[n] = [loc[n][l:l + 1] for l in range(DEPTH)]

    def take(l, names, arrays):
        full = dict(zip(names, arrays))
        if 'ffn_w_gate' in full:
            W['ffn_w_gu'][l] = _gu_joined(full['ffn_w_gate'], full['ffn_w_up'])
        for n in names:
            if n in W:
                W[n][l] = full[n]

    take(0, first, got[:len(first)])

    def after_in_proj(results):
        take(0, rest, results[:len(rest)])
        take(1, early1, results[len(rest):])

    xs, memb = x[0], mem[0].astype(BF)
    host_a = both_hosted(gather_layer(0, shards, pieces=rest), gather_layer(1, shards, pieces=early1))
    xf, xb, sv0 = _layer_fwd(0, xs, None, None, memb, W, host=host_a, host2=gather_layer(1, shards, pieces=late1),
                             after_in_proj=after_in_proj)
    take(1, late1, sv0['hosted2'])
    xf, xb, sv1 = _layer_fwd(1, xf, xb, None, memb, W)
    saves = [sv0, sv1]
    kvs = [(sv['kb'], sv['vb']) for sv in saves]

    def packed(G):
        g = dict(G)
        g['ffn_w_gate'], g['ffn_w_up'] = _gu_apart(g['ffn_w_gu'])
        g['pool_w'] = g['pool_w'].astype(BF)
        g['b_in'] = jnp.roll(g['b_in'], 2 * D, axis=1)
        return {n: g[n] for n in PIECES}, _pack(g, SMALL_ALL, D, F32, SMALL_ROWS)

    def reduce_start(G):
        parts, smallp = packed(G)
        theirs, theirs_small = split_halves(parts, smallp)
        q = {}
        for group in SAME_SHAPE:
            q.update(add_halves(group, parts, theirs, core_arr))
        qs = add_halves_small(smallp, theirs_small, core_arr)
        return q, qs, exchange_halves(q, qs)

    def reduce_finish(q, qs, recv):
        recv3 = dict(zip(PIECES, recv[:n_p]))
        summed = {}
        for group in SAME_SHAPE:
            summed.update(sum_halves(group, q, recv3, chip_arr))
        sums = [summed[n] for n in PIECES]
        sums.append(sum_chips_small(qs, recv[n_p], chip_arr))
        other = swap_cores(sums)
        return sums, other

    lf = (W['ln_b'][1][2:3], loss_target[0])
    dx, G1, loss_blk = _layer_bwd(1, None, saves[1], memb, kvs[1][0], kvs[1][1], W, loss_from=lf)
    q1, qs1, host1 = reduce_start(G1)
    started0 = []

    def last_host_fn(G):
        started0.extend(reduce_start(G))
        return started0[2]

    grad_x, G0, _ = _layer_bwd(0, dx, saves[0], memb, kvs[0][0], kvs[0][1], W, host=host1, last_host_fn=last_host_fn)
    red = [reduce_finish(started0[0], started0[1], G0['hosted_last']), reduce_finish(q1, qs1, G0['hosted'])]
    loss = lax.psum(loss_blk[0, 0], ("x", "y", "c"))

    small_shapes = {n: FULL_SHAPES[n][1:] for n in SMALL_ALL}
    per_layer = []
    for l in range(DEPTH):
        mine, theirs = red[l][0][-1], red[l][1][-1]
        whole = jnp.where(core == 0, jnp.concatenate([mine, theirs]), jnp.concatenate([theirs, mine]))
        per_layer.append(_unpack(whole, SMALL_ALL, small_shapes))
    grads = {}
    for n in SMALL_ALL:
        gn = jnp.stack([per_layer[l][n] for l in range(DEPTH)])
        if n in SMALL_SHARDED:
            size = loc[n].shape[SMALL_SHARDED[n]]
            gn = lax.dynamic_slice_in_dim(gn, chip * size, size, axis=SMALL_SHARDED[n])
        grads[n] = gn

    out_d, out_m, out_v = {}, {}, {}
    for p, n in enumerate(PIECES):
        pairs = [(red[l][0][p], red[l][1][p]) for l in range(DEPTH)]
        grads[n], out_d[n], out_m[n], out_v[n] = adamw_halves(loc[n], pairs, mom[n], var[n], name="adamw_" + n)
    for n in SMALL_ALL:
        out_d[n], out_m[n], out_v[n] = adamw(loc[n], grads[n], mom[n], var[n], name="adamw_" + n)

    return (loss, grad_x[None], *[grads[n] for n in WEIGHTS], *[out_d[n] for n in WEIGHTS],
            *[out_m[n] for n in WEIGHTS], *[out_v[n] for n in WEIGHTS])
```

```python
import jax
import jax.numpy as jnp
from jax import lax
from jax.experimental import pallas as pl
from jax.experimental.pallas import tpu as pltpu

F32 = jnp.float32
BF = jnp.bfloat16
MESH = pl.DeviceIdType.MESH

D = 1024
DEPTH = 2
N_MEM = 256
POOL_WINDOWS = (2, 4, 8, 16)
POOL_GD = 256
LRU_HEADS = 8
LRU_HD = 128
LRU_C = 8.0
X_HEADS = 4
X_HD = 256
D_FF = 2816
IN_COLS = 8 * D
ALPHA = (2 * DEPTH) ** 0.25
LN_EPS = 1e-5
ADAM_LR = 0.001
ADAM_B1 = 0.9
ADAM_B2 = 0.999
ADAM_EPS = 1e-08
ADAM_WD = 0.01
ADAM_STEP = 10

Z_PERM = (2, 3, 4, 5, 6, 7, 0, 1)
ZB_SCONV, ZB_GATE, ZB_POOL, ZB_LRU = 0, 1, 6, 7
HALO = 16
VMEM_LIMIT = 56 * 1024 * 1024

WEIGHTS = ['w_in', 'b_in', 'pool_w', 'pool_scale', 'lru_conv_w', 'lru_conv_b', 'lru_w_r', 'lru_b_r', 'lru_w_i',
           'lru_b_i', 'lru_lambda', 'lru_w_out', 'sconv_w', 'sconv_w_out', 'w_mix_out', 'xa_w_q', 'xa_w_k', 'xa_w_v',
           'xa_w_o', 'ffn_w_gate', 'ffn_w_up', 'ffn_w_down', 'ln_g', 'ln_b']
BIG_SHARDED = {'w_in': 2, 'pool_w': 2, 'lru_w_out': 1, 'sconv_w_out': 1, 'w_mix_out': 1, 'xa_w_q': 1, 'xa_w_k': 1,
               'xa_w_v': 1, 'xa_w_o': 1, 'ffn_w_gate': 2, 'ffn_w_up': 2, 'ffn_w_down': 1}
SMALL_SHARDED = {'lru_conv_w': 2, 'sconv_w': 2, 'ln_g': 2, 'ln_b': 2}
SMALL_ALL = ['b_in', 'pool_scale', 'lru_conv_w', 'lru_conv_b', 'lru_w_r', 'lru_b_r', 'lru_w_i', 'lru_b_i',
             'lru_lambda', 'sconv_w', 'ln_g', 'ln_b']


def _cp(*sem):
    return pltpu.CompilerParams(dimension_semantics=sem, vmem_limit_bytes=VMEM_LIMIT)


def _sigmoid(x):
    return 0.5 * jnp.tanh(0.5 * x) + 0.5


class Hosted:
    def __init__(self, arrays, out_shape, sems, start, finish):
        self.arrays, self.out_shape, self.sems, self.start, self.finish = arrays, out_shape, sems, start, finish


def _with_host(host, n_in, n_out, refs):
    if host is None:
        return refs[:n_in], (), refs[n_in:n_in + n_out], (), refs[n_in + n_out:], ()
    hi, ho, hs = len(host.arrays), len(host.out_shape), len(host.sems)
    ins, h_in = refs[:n_in], refs[n_in:n_in + hi]
    outs = refs[n_in + hi:n_in + hi + n_out]
    h_out = refs[n_in + hi + n_out:n_in + hi + n_out + ho]
    rest = refs[n_in + hi + n_out + ho:]
    return ins, h_in, outs, h_out, rest[:len(rest) - hs], rest[len(rest) - hs:]


HBM_SPEC = pl.BlockSpec(memory_space=pl.ANY)


def mm_nn(a, w, bias, *, out_dtype, tm, tn, name, host=None):
    T, K = a.shape
    N = w.shape[1]
    tm, tn = min(tm, T), min(tn, N)
    nj, ni = N // tn, T // tm
    n_in = 2 if bias is None else 3

    def body(*refs):
        ins, h_in, (o_ref,), h_out, _, h_sems = _with_host(host, n_in, 1, refs)
        a_ref, w_ref = ins[:2]
        j, i = pl.program_id(0), pl.program_id(1)
        if host is not None:
            @pl.when((j == 0) & (i == 0))
            def _():
                host.start(h_in, h_out, h_sems)
        acc = jnp.dot(a_ref[...].astype(BF), w_ref[...], preferred_element_type=F32)
        if bias is not None:
            acc = acc + ins[2][...]
        o_ref[...] = acc.astype(o_ref.dtype)
        if host is not None:
            @pl.when((j == nj - 1) & (i == ni - 1))
            def _():
                host.finish(h_in, h_out, h_sems)

    in_specs = [pl.BlockSpec((tm, K), lambda j, i: (i, 0)), pl.BlockSpec((K, tn), lambda j, i: (0, j))]
    args = [a, w]
    if bias is not None:
        in_specs.append(pl.BlockSpec((1, tn), lambda j, i: (0, j)))
        args.append(bias)
    out_specs = [pl.BlockSpec((tm, tn), lambda j, i: (i, j))]
    out_shape = [jax.ShapeDtypeStruct((T, N), out_dtype)]
    scratch = []
    if host is not None:
        in_specs += [HBM_SPEC] * len(host.arrays)
        args += list(host.arrays)
        out_specs += [HBM_SPEC] * len(host.out_shape)
        out_shape += list(host.out_shape)
        scratch = list(host.sems)
    res = pl.pallas_call(
        body, name=name, grid=(nj, ni), in_specs=in_specs, out_specs=out_specs, out_shape=out_shape,
        scratch_shapes=scratch, compiler_params=_cp("arbitrary", "arbitrary"))(*args)
    return res[0] if host is None else (res[0], res[1:])


FF_HALF = D_FF // 2


def ffn_in_swiglu(x, wgu, *, tm, name, host=None):
    T, K = x.shape
    tm = min(tm, T)
    ni = T // tm

    def body(*refs):
        (x_ref, w_ref), h_in, (gu_ref, h_ref), h_out, _, h_sems = _with_host(host, 2, 2, refs)
        j, i = pl.program_id(0), pl.program_id(1)
        if host is not None:
            @pl.when((j == 0) & (i == 0))
            def _():
                host.start(h_in, h_out, h_sems)
        acc = jnp.dot(x_ref[...].astype(BF), w_ref[...], preferred_element_type=F32)
        gu = acc.astype(BF)
        gu_ref[...] = gu
        g = gu[:, :FF_HALF].astype(F32)
        h_ref[...] = (g * _sigmoid(g) * gu[:, FF_HALF:].astype(F32)).astype(BF)
        if host is not None:
            @pl.when((j == 1) & (i == ni - 1))
            def _():
                host.finish(h_in, h_out, h_sems)

    in_specs = [pl.BlockSpec((tm, K), lambda j, i: (i, 0)), pl.BlockSpec((K, 2 * FF_HALF), lambda j, i: (0, j))]
    args = [x, wgu]
    out_specs = [pl.BlockSpec((tm, 2 * FF_HALF), lambda j, i: (i, j)), pl.BlockSpec((tm, FF_HALF), lambda j, i: (i, j))]
    out_shape = [jax.ShapeDtypeStruct((T, 2 * D_FF), BF), jax.ShapeDtypeStruct((T, D_FF), BF)]
    scratch = []
    if host is not None:
        in_specs += [HBM_SPEC] * len(host.arrays)
        args += list(host.arrays)
        out_specs += [HBM_SPEC] * len(host.out_shape)
        out_shape += list(host.out_shape)
        scratch = list(host.sems)
    res = pl.pallas_call(
        body, name=name, grid=(2, ni), in_specs=in_specs, out_specs=out_specs, out_shape=out_shape,
        scratch_shapes=scratch, compiler_params=_cp("arbitrary", "arbitrary"))(*args)
    return (res[0], res[1]) if host is None else (res[0], res[1], res[2:])


def ffn_down_dx_swiglu(dp, wd, gu, *, tm, name, host=None):
    T = dp.shape[0]
    tm = min(tm, T)
    ni = T // tm

    def body(*refs):
        (dp_ref, w_ref, gu_ref), h_in, (o_ref,), h_out, _, h_sems = _with_host(host, 3, 1, refs)
        if host is not None:
            j, i = pl.program_id(0), pl.program_id(1)

            @pl.when((j == 0) & (i == 0))
            def _():
                host.start(h_in, h_out, h_sems)

            @pl.when((j == 1) & (i == ni - 1))
            def _():
                host.finish(h_in, h_out, h_sems)
        dh = lax.dot_general(dp_ref[...], w_ref[...], (((1,), (1,)), ((), ())), preferred_element_type=F32)
        g = gu_ref[:, :FF_HALF].astype(F32)
        u = gu_ref[:, FF_HALF:].astype(F32)
        sg = _sigmoid(g)
        t = dh * sg
        s = g * sg
        o_ref[:, :FF_HALF] = (t * u * (1.0 + g - s)).astype(BF)
        o_ref[:, FF_HALF:] = (t * g).astype(BF)

    in_specs = [pl.BlockSpec((tm, D), lambda j, i: (i, 0)), pl.BlockSpec((FF_HALF, D), lambda j, i: (j, 0)),
                pl.BlockSpec((tm, 2 * FF_HALF), lambda j, i: (i, j))]
    args = [dp, wd, gu]
    out_specs = [pl.BlockSpec((tm, 2 * FF_HALF), lambda j, i: (i, j))]
    out_shape = [jax.ShapeDtypeStruct((T, 2 * D_FF), BF)]
    scratch = []
    if host is not None:
        in_specs += [HBM_SPEC] * len(host.arrays)
        args += list(host.arrays)
        out_specs += [HBM_SPEC] * len(host.out_shape)
        out_shape += list(host.out_shape)
        scratch = list(host.sems)
    res = pl.pallas_call(
        body, name=name, grid=(2, ni), in_specs=in_specs, out_specs=out_specs, out_shape=out_shape,
        scratch_shapes=scratch, compiler_params=_cp("arbitrary", "arbitrary"))(*args)
    return res[0] if host is None else (res[0], res[1:])


def mm_nt(a, w, res, *, out_dtype, tm, tc, name, host=None):
    T, C = a.shape
    K = w.shape[0]
    tm, tc = min(tm, T), min(tc, C)
    nc = C // tc
    ni = T // tm
    n_in = 2 if res is None else 3

    def body(*refs):
        ins, h_in, (o_ref,), h_out, (acc_ref,), h_sems = _with_host(host, n_in, 1, refs)
        a_ref, w_ref = ins[:2]
        r_ref = ins[2] if res is not None else None
        c = pl.program_id(1)
        if host is not None:
            @pl.when((pl.program_id(0) == 0) & (c == 0))
            def _():
                host.start(h_in, h_out, h_sems)

            @pl.when((pl.program_id(0) == ni - 1) & (c == nc - 1))
            def _():
                host.finish(h_in, h_out, h_sems)
        part = lax.dot_general(a_ref[...].astype(BF), w_ref[...], (((1,), (1,)), ((), ())),
                               preferred_element_type=F32)

        @pl.when(c == 0)
        def _():
            acc_ref[...] = part

        @pl.when(c > 0)
        def _():
            acc_ref[...] += part

        @pl.when(c == nc - 1)
        def _():
            out = acc_ref[...]
            if res is not None:
                out = out + ALPHA * r_ref[...]
            o_ref[...] = out.astype(o_ref.dtype)

    in_specs = [pl.BlockSpec((tm, tc), lambda i, c: (i, c)), pl.BlockSpec((K, tc), lambda i, c: (0, c))]
    args = [a, w]
    if res is not None:
        in_specs.append(pl.BlockSpec((tm, K), lambda i, c: (i, 0)))
        args.append(res)
    out_specs = [pl.BlockSpec((tm, K), lambda i, c: (i, 0))]
    out_shape = [jax.ShapeDtypeStruct((T, K), out_dtype)]
    scratch = [pltpu.VMEM((tm, K), F32)]
    if host is not None:
        in_specs += [HBM_SPEC] * len(host.arrays)
        args += list(host.arrays)
        out_specs += [HBM_SPEC] * len(host.out_shape)
        out_shape += list(host.out_shape)
        scratch += list(host.sems)
    out = pl.pallas_call(
        body, name=name, grid=(ni, nc), in_specs=in_specs, out_specs=out_specs, out_shape=out_shape,
        scratch_shapes=scratch, compiler_params=_cp("arbitrary", "arbitrary"))(*args)
    return out[0] if host is None else (out[0], out[1:])


def mm_tn(a, b, *, out_dtype, tk, tn, tt, name, colsum=False):
    T, K = a.shape
    N = b.shape[1]
    tk, tn, tt = min(tk, K), min(tn, N), min(tt, T)
    nt = T // tt

    def body(*refs):
        if colsum:
            a_ref, b_ref, o_ref, cs_ref, acc_ref = refs
        else:
            a_ref, b_ref, o_ref, acc_ref = refs
        i, t = pl.program_id(1), pl.program_id(2)
        bb = b_ref[...]
        part = lax.dot_general(a_ref[...].astype(BF), bb.astype(BF), (((0,), (0,)), ((), ())),
                               preferred_element_type=F32)

        @pl.when(t == 0)
        def _():
            acc_ref[...] = part

        @pl.when(t > 0)
        def _():
            acc_ref[...] += part

        @pl.when(t == nt - 1)
        def _():
            o_ref[...] = acc_ref[...].astype(o_ref.dtype)

        if colsum:
            s = jnp.sum(bb.astype(F32), axis=0, keepdims=True)

            @pl.when((i == 0) & (t == 0))
            def _():
                cs_ref[...] = s

            @pl.when((i == 0) & (t > 0))
            def _():
                cs_ref[...] += s

    out_specs = [pl.BlockSpec((tk, tn), lambda j, i, t: (i, j))]
    out_shape = [jax.ShapeDtypeStruct((K, N), out_dtype)]
    if colsum:
        out_specs.append(pl.BlockSpec((1, tn), lambda j, i, t: (0, j)))
        out_shape.append(jax.ShapeDtypeStruct((1, N), F32))
    res = pl.pallas_call(
        body, name=name, grid=(N // tn, K // tk, nt),
        in_specs=[pl.BlockSpec((tt, tk), lambda j, i, t: (t, i)), pl.BlockSpec((tt, tn), lambda j, i, t: (t, j))],
        out_specs=out_specs, out_shape=out_shape, scratch_shapes=[pltpu.VMEM((tk, tn), F32)],
        compiler_params=_cp("arbitrary", "arbitrary", "arbitrary"))(a, b)
    return res if colsum else res[0]


def mm_res_ln(a, w, res, g, b, *, tm, name):
    T, K = a.shape
    tm = min(tm, T)

    def body(a_ref, w_ref, r_ref, g_ref, b_ref, y_ref, yb_ref, xh_ref, rs_ref):
        pre = ALPHA * r_ref[...] + jnp.dot(a_ref[...].astype(BF), w_ref[...], preferred_element_type=F32)
        mu = jnp.mean(pre, axis=-1, keepdims=True)
        cen = pre - mu
        var = jnp.mean(cen * cen, axis=-1, keepdims=True)
        rstd = lax.rsqrt(var + LN_EPS)
        xhat = cen * rstd
        y = xhat * g_ref[...] + b_ref[...]
        y_ref[...] = y
        yb_ref[...] = y.astype(BF)
        xh_ref[...] = xhat
        rs_ref[...] = rstd

    row = lambda i: (i, 0)
    fix = lambda i: (0, 0)
    return pl.pallas_call(
        body, name=name, grid=(T // tm,),
        in_specs=[pl.BlockSpec((tm, K), row), pl.BlockSpec((K, D), fix), pl.BlockSpec((tm, D), row),
                  pl.BlockSpec((1, D), fix), pl.BlockSpec((1, D), fix)],
        out_specs=[pl.BlockSpec((tm, D), row), pl.BlockSpec((tm, D), row), pl.BlockSpec((tm, D), row),
                   pl.BlockSpec((tm, 1), row)],
        out_shape=[jax.ShapeDtypeStruct((T, D), F32), jax.ShapeDtypeStruct((T, D), BF),
                   jax.ShapeDtypeStruct((T, D), F32), jax.ShapeDtypeStruct((T, 1), F32)],
        compiler_params=_cp("arbitrary"))(a, w, res, g, b)


def ln_bwd(dy, xhat, rstd, g, *, tm, name, loss_from=None, dw_of=None):
    T = xhat.shape[0]
    tm = min(tm, T)
    nt = T // tm
    with_loss = loss_from is not None
    with_dw = dw_of is not None

    def body(*refs):
        if with_dw:
            acc_ref, refs = refs[-1], refs[:-1]
            n_main_in = 5 if with_loss else 4
            a_ref = refs[n_main_in]
            dw_ref = refs[-1]
            refs = refs[:n_main_in] + refs[n_main_in + 1:-1]
        if with_loss:
            xh_ref, rs_ref, g_ref, b_ref, t_ref, dp_ref, dpb_ref, dg_ref, db_ref, ls_ref = refs
        else:
            dy_ref, xh_ref, rs_ref, g_ref, dp_ref, dpb_ref, dg_ref, db_ref = refs
        i = pl.program_id(0)
        xhat_ = xh_ref[...]
        gg = g_ref[...]
        if with_loss:
            err = xhat_ * gg + b_ref[...] - t_ref[...]
            dyv = err * (1.0 / D)
            lpart = 0.5 * jnp.sum(jnp.sum(err * err, axis=-1, keepdims=True) * (1.0 / D))
        else:
            dyv = dy_ref[...]
        dxh = dyv * gg
        m1 = jnp.mean(dxh, axis=-1, keepdims=True)
        m2 = jnp.mean(dxh * xhat_, axis=-1, keepdims=True)
        dpre = rs_ref[...] * (dxh - m1 - xhat_ * m2)
        dp_ref[...] = dpre
        dpb = dpre.astype(BF)
        dpb_ref[...] = dpb
        dgp = jnp.sum(dyv * xhat_, axis=0, keepdims=True)
        dbp = jnp.sum(dyv, axis=0, keepdims=True)
        if with_dw:
            dwp = lax.dot_general(a_ref[...], dpb, (((0,), (0,)), ((), ())), preferred_element_type=F32)

        @pl.when(i == 0)
        def _():
            dg_ref[...] = dgp
            db_ref[...] = dbp
            if with_loss:
                ls_ref[...] = jnp.full((8, 128), lpart, F32)
            if with_dw:
                acc_ref[...] = dwp

        @pl.when(i > 0)
        def _():
            dg_ref[...] += dgp
            db_ref[...] += dbp
            if with_loss:
                ls_ref[...] += jnp.full((8, 128), lpart, F32)
            if with_dw:
                acc_ref[...] += dwp

        if with_dw:
            @pl.when(i == nt - 1)
            def _():
                dw_ref[...] = acc_ref[...].astype(BF)

    row = lambda i: (i, 0)
    fix = lambda i: (0, 0)
    if with_loss:
        in_specs = [pl.BlockSpec((tm, D), row), pl.BlockSpec((tm, 1), row), pl.BlockSpec((1, D), fix),
                    pl.BlockSpec((1, D), fix), pl.BlockSpec((tm, D), row)]
        args = [xhat, rstd, g, loss_from[0], loss_from[1]]
    else:
        in_specs = [pl.BlockSpec((tm, D), row), pl.BlockSpec((tm, D), row), pl.BlockSpec((tm, 1), row),
                    pl.BlockSpec((1, D), fix)]
        args = [dy, xhat, rstd, g]
    out_specs = [pl.BlockSpec((tm, D), row), pl.BlockSpec((tm, D), row), pl.BlockSpec((1, D), fix),
                 pl.BlockSpec((1, D), fix)]
    out_shape = [jax.ShapeDtypeStruct((T, D), F32), jax.ShapeDtypeStruct((T, D), BF),
                 jax.ShapeDtypeStruct((1, D), F32), jax.ShapeDtypeStruct((1, D), F32)]
    if with_loss:
        out_specs.append(pl.BlockSpec((8, 128), fix))
        out_shape.append(jax.ShapeDtypeStruct((8, 128), F32))
    scratch = []
    if with_dw:
        K = dw_of.shape[1]
        in_specs.append(pl.BlockSpec((tm, K), row))
        args.append(dw_of)
        out_specs.append(pl.BlockSpec((K, D), fix))
        out_shape.append(jax.ShapeDtypeStruct((K, D), BF))
        scratch.append(pltpu.VMEM((K, D), F32))
    return pl.pallas_call(body, name=name, grid=(nt,), in_specs=in_specs, out_specs=out_specs,
                          out_shape=out_shape, scratch_shapes=scratch, compiler_params=_cp("arbitrary"))(*args)


def _prev_halo(tm, blk):
    return lambda i: (jnp.maximum(i * (tm // HALO) - 1, 0), blk)


def _next_halo(tm, T, blk):
    return lambda i: (jnp.minimum((i + 1) * (tm // HALO), T // HALO - 1), blk)


def _pool_p(ext, t, g):
    e = ext[:, g * POOL_GD:(g + 1) * POOL_GD]
    s = e
    for sh in (1, 2, 4, 8)[:g + 1]:
        s = s + pltpu.roll(s, sh, axis=0)
    inv = 1.0 / jnp.minimum(t + 1, POOL_WINDOWS[g]).astype(F32)
    return s[HALO:] * inv - e[HALO:]


def pool_fwd(z, pw, *, tm, name):
    T = z.shape[0]
    tm = min(tm, T)

    def body(zm_ref, zh_ref, pw_ref, o_ref):
        i = pl.program_id(0)
        keep = jnp.where(i == 0, 0.0, 1.0).astype(F32)
        ext = jnp.concatenate([zh_ref[...].astype(F32) * keep, zm_ref[...].astype(F32)], axis=0)
        t = i * tm + lax.broadcasted_iota(jnp.int32, (tm, 1), 0)
        outs = [jnp.dot(_pool_p(ext, t, g).astype(BF), pw_ref[g], preferred_element_type=F32) for g in range(4)]
        o_ref[...] = jnp.concatenate(outs, axis=1).astype(o_ref.dtype)

    return pl.pallas_call(
        body, name=name, grid=(T // tm,),
        in_specs=[pl.BlockSpec((tm, D), lambda i: (i, ZB_POOL)), pl.BlockSpec((HALO, D), _prev_halo(tm, ZB_POOL)),
                  pl.BlockSpec((4, POOL_GD, POOL_GD), lambda i: (0, 0, 0))],
        out_specs=pl.BlockSpec((tm, D), lambda i: (i, 0)),
        out_shape=jax.ShapeDtypeStruct((T, D), BF), compiler_params=_cp("arbitrary"))(z, z, pw)


def pool_bwd(dz, dyp, yp_pre, z, pw, ps, *, tm, name):
    T = z.shape[0]
    tm = min(tm, T)
    nt = T // tm

    def body(dz_in, dy_ref, dyn_ref, yp_ref, zm_ref, zh_ref, pw_ref, ps_ref, dz_ref, dpw_ref, dps_ref):
        del dz_in
        i = pl.program_id(0)
        keep_p = jnp.where(i == 0, 0.0, 1.0).astype(F32)
        keep_n = jnp.where(i == nt - 1, 0.0, 1.0).astype(F32)
        ext = jnp.concatenate([zh_ref[...].astype(F32) * keep_p, zm_ref[...].astype(F32)], axis=0)
        t = i * tm + lax.broadcasted_iota(jnp.int32, (tm, 1), 0)
        psv = ps_ref[...]
        dy = dy_ref[...].astype(F32)
        dyp_ext = jnp.concatenate([dy, dyn_ref[...].astype(F32) * keep_n], axis=0) * psv
        t_ext = i * tm + lax.broadcasted_iota(jnp.int32, (tm + HALO, 1), 0)
        dps = jnp.sum(dy * yp_ref[...].astype(F32), axis=0, keepdims=True)
        dzs, dpws = [], []
        for g in range(4):
            sl = slice(g * POOL_GD, (g + 1) * POOL_GD)
            dyg = dyp_ext[:, sl].astype(BF)
            dp = lax.dot_general(dyg, pw_ref[g], (((1,), (1,)), ((), ())), preferred_element_type=F32)
            q = dp * (1.0 / jnp.minimum(t_ext + 1, POOL_WINDOWS[g]).astype(F32))
            s = q
            for sh in (1, 2, 4, 8)[:g + 1]:
                s = s + pltpu.roll(s, tm + HALO - sh, axis=0)
            dzs.append(s[:tm] - dp[:tm])
            p = _pool_p(ext, t, g).astype(BF)
            dpws.append(lax.dot_general(p, dyg[:tm], (((0,), (0,)), ((), ())), preferred_element_type=F32))
        dz_ref[...] = jnp.concatenate(dzs, axis=1).astype(dz_ref.dtype)

        @pl.when(i == 0)
        def _():
            for g in range(4):
                dpw_ref[g] = dpws[g]
            dps_ref[...] = dps

        @pl.when(i > 0)
        def _():
            for g in range(4):
                dpw_ref[g] += dpws[g]
            dps_ref[...] += dps

    row = lambda i: (i, 0)
    return pl.pallas_call(
        body, name=name, grid=(nt,),
        in_specs=[pl.BlockSpec(memory_space=pl.ANY),
                  pl.BlockSpec((tm, D), row), pl.BlockSpec((HALO, D), _next_halo(tm, T, 0)),
                  pl.BlockSpec((tm, D), row),
                  pl.BlockSpec((tm, D), lambda i: (i, ZB_POOL)), pl.BlockSpec((HALO, D), _prev_halo(tm, ZB_POOL)),
                  pl.BlockSpec((4, POOL_GD, POOL_GD), lambda i: (0, 0, 0)), pl.BlockSpec((1, D), lambda i: (0, 0))],
        out_specs=[pl.BlockSpec((tm, D), lambda i: (i, ZB_POOL)),
                   pl.BlockSpec((4, POOL_GD, POOL_GD), lambda i: (0, 0, 0)), pl.BlockSpec((1, D), lambda i: (0, 0))],
        out_shape=[jax.ShapeDtypeStruct(dz.shape, dz.dtype), jax.ShapeDtypeStruct((4, POOL_GD, POOL_GD), F32),
                   jax.ShapeDtypeStruct((1, D), F32)],
        input_output_aliases={0: 0}, compiler_params=_cp("arbitrary"))(dz, dyp, dyp, yp_pre, z, z, pw, ps)


def _fill_ext(ext_s, halo, main, keep):
    ext_s[0:HALO, :] = halo * keep
    ext_s[HALO:, :] = main


def _lru_gates(ext_s, tm, cw, cb, wr_ref, br, wi_ref, bi, lam):
    shifted = []
    v = cb
    for k in range(4):
        zs = ext_s[pl.ds(HALO - 3 + k, tm), :]
        shifted.append(zs)
        v = v + cw[k:k + 1, :] * zs
    vb = v.astype(BF)
    rp, ip = [], []
    for h in range(LRU_HEADS):
        sl = slice(h * LRU_HD, (h + 1) * LRU_HD)
        rp.append(jnp.dot(vb[:, sl], wr_ref[h], preferred_element_type=F32))
        ip.append(jnp.dot(vb[:, sl], wi_ref[h], preferred_element_type=F32))
    r = _sigmoid(jnp.concatenate(rp, axis=1) + br)
    ig = _sigmoid(jnp.concatenate(ip, axis=1) + bi)
    sp = jnp.maximum(-lam, 0.0) + jnp.log(1.0 + jnp.exp(-jnp.abs(lam)))
    a = jnp.exp(-LRU_C * r * sp)
    om = 1.0 - a * a
    rs = lax.rsqrt(om)
    return v, vb, r, ig, a, om, rs, sp, shifted


def lru_fwd(z, cw, cb, wr, br, wi, bi, lam, wlo, *, tm, name):
    T = z.shape[0]
    tm = min(tm, T)
    nch = tm // 8

    def body(zm_ref, zh_ref, cw_ref, cb_ref, wr_ref, br_ref, wi_ref, bi_ref, lam_ref, wlo_ref, h_ref, y_ref,
             a_s, b_s, carry, ext_s):
        i = pl.program_id(0)

        @pl.when(i == 0)
        def _():
            carry[...] = jnp.zeros_like(carry)

        keep = jnp.where(i == 0, 0.0, 1.0).astype(F32)
        _fill_ext(ext_s, zh_ref[...].astype(F32), zm_ref[...].astype(F32), keep)
        v, _, _, ig, a, om, rs, _, _ = _lru_gates(ext_s, tm, cw_ref[...], cb_ref[...], wr_ref, br_ref[...], wi_ref,
                                                  bi_ref[...], lam_ref[...])
        a_s[...] = a
        b_s[...] = jnp.where(om > 0.0, om * rs, 0.0) * (ig * v)
        row = lax.broadcasted_iota(jnp.int32, (8, D), 0)

        def step(ci, hprev):
            sl = pl.ds(pl.multiple_of(ci * 8, 8), 8)
            aa, bb = a_s[sl, :], b_s[sl, :]
            for s in (1, 2, 4):
                m = row >= s
                bb = bb + aa * jnp.where(m, pltpu.roll(bb, s, axis=0), 0.0)
                aa = aa * jnp.where(m, pltpu.roll(aa, s, axis=0), 1.0)
            h = bb + aa * hprev
            h_ref[sl, :] = h
            return jnp.broadcast_to(h[7:8, :], (8, D))

        carry[...] = lax.fori_loop(0, nch, step, carry[...])
        y_ref[...] = jnp.dot(h_ref[...].astype(BF), wlo_ref[...], preferred_element_type=F32).astype(BF)

    fix2 = lambda i: (0, 0)
    fix3 = lambda i: (0, 0, 0)
    return pl.pallas_call(
        body, name=name, grid=(T // tm,),
        in_specs=[pl.BlockSpec((tm, D), lambda i: (i, ZB_LRU)), pl.BlockSpec((HALO, D), _prev_halo(tm, ZB_LRU)),
                  pl.BlockSpec((4, D), fix2), pl.BlockSpec((1, D), fix2),
                  pl.BlockSpec((LRU_HEADS, LRU_HD, LRU_HD), fix3), pl.BlockSpec((1, D), fix2),
                  pl.BlockSpec((LRU_HEADS, LRU_HD, LRU_HD), fix3), pl.BlockSpec((1, D), fix2),
                  pl.BlockSpec((1, D), fix2), pl.BlockSpec((D, D), fix2)],
        out_specs=[pl.BlockSpec((tm, D), lambda i: (i, 0)), pl.BlockSpec((tm, D), lambda i: (i, 0))],
        out_shape=[jax.ShapeDtypeStruct((T, D), F32), jax.ShapeDtypeStruct((T, D), BF)],
        scratch_shapes=[pltpu.VMEM((tm, D), F32), pltpu.VMEM((tm, D), F32), pltpu.VMEM((8, D), F32),
                        pltpu.VMEM((tm + HALO, D), F32)],
        compiler_params=_cp("arbitrary"))(z, z, cw, cb, wr, br, wi, bi, lam, wlo)


def lru_bwd(dz, dyl, z, h, cw, cb, wr, br, wi, bi, lam, wlo, *, tm, name):
    T = z.shape[0]
    tm = min(tm, T)
    nt = T // tm
    nch = tm // 8

    def body(dz_in, dy_ref, zm_ref, zh_ref, h_ref, hh_ref, cw_ref, cb_ref, wr_ref, br_ref, wi_ref, bi_ref, lam_ref,
             wlo_ref, dz_ref, dcw_ref, dcb_ref, dwr_ref, dbr_ref, dwi_ref, dbi_ref, dlam_ref, dwlo_ref,
             c_s, g_s, dh_s, dh_carry, a_ext, dv_ext, ext_s, h_ext, wlo_acc):
        del dz_in
        i = pl.program_id(0)
        ti = nt - 1 - i

        @pl.when(i == 0)
        def _():
            dh_carry[...] = jnp.zeros_like(dh_carry)
            a_ext[tm:, :] = jnp.zeros((8, D), F32)
            dv_ext[tm:, :] = jnp.zeros((HALO, D), F32)

        keep = jnp.where(ti == 0, 0.0, 1.0).astype(F32)
        _fill_ext(ext_s, zh_ref[...].astype(F32), zm_ref[...].astype(F32), keep)
        cw_ = cw_ref[...]
        lam_ = lam_ref[...]
        v, vb, r, ig, a, om, rs, sp, shifted = _lru_gates(ext_s, tm, cw_, cb_ref[...], wr_ref, br_ref[...], wi_ref,
                                                          bi_ref[...], lam_)
        mult = jnp.where(om > 0.0, om * rs, 0.0)
        a_ext[0:tm, :] = a
        c_s[...] = a_ext[pl.ds(1, tm), :]
        g_s[...] = lax.dot_general(dy_ref[...], wlo_ref[...], (((1,), (1,)), ((), ())), preferred_element_type=F32)
        row = lax.broadcasted_iota(jnp.int32, (8, D), 0)

        def step(k, nxt):
            ci = nch - 1 - k
            sl = pl.ds(pl.multiple_of(ci * 8, 8), 8)
            cc, gg = c_s[sl, :], g_s[sl, :]
            for s in (1, 2, 4):
                m = row < 8 - s
                gg = gg + cc * jnp.where(m, pltpu.roll(gg, 8 - s, axis=0), 0.0)
                cc = cc * jnp.where(m, pltpu.roll(cc, 8 - s, axis=0), 1.0)
            dh = gg + cc * nxt
            dh_s[sl, :] = dh
            return jnp.broadcast_to(dh[0:1, :], (8, D))

        dh_carry[...] = lax.fori_loop(0, nch, step, dh_carry[...])
        a_ext[tm:, :] = a[0:8, :]
        dh = dh_s[...]
        h_ext[0:8, :] = hh_ref[...] * keep
        hv = h_ref[...]
        h_ext[8:, :] = hv
        hprev = h_ext[pl.ds(7, tm), :]
        dwlo = lax.dot_general(hv.astype(BF), dy_ref[...], (((0,), (0,)), ((), ())), preferred_element_type=F32)
        iv = ig * v
        da = dh * hprev
        dmult = dh * iv
        div = dh * mult
        dlog = da * a - dmult * (a * a) * rs
        dr = dlog * (-LRU_C * sp)
        dlam = jnp.sum(dlog * r, axis=0, keepdims=True) * (LRU_C * _sigmoid(-lam_))
        di = div * v
        dv = div * ig
        drp = dr * r * (1.0 - r)
        dip = di * ig * (1.0 - ig)
        drb, dib = drp.astype(BF), dip.astype(BF)
        dvh, dwr, dwi = [], [], []
        nt_dims = (((1,), (1,)), ((), ()))
        tn_dims = (((0,), (0,)), ((), ()))
        for hd in range(LRU_HEADS):
            sl = slice(hd * LRU_HD, (hd + 1) * LRU_HD)
            dvh.append(lax.dot_general(drb[:, sl], wr_ref[hd], nt_dims, preferred_element_type=F32)
                       + lax.dot_general(dib[:, sl], wi_ref[hd], nt_dims, preferred_element_type=F32))
            dwr.append(lax.dot_general(vb[:, sl], drb[:, sl], tn_dims, preferred_element_type=F32))
            dwi.append(lax.dot_general(vb[:, sl], dib[:, sl], tn_dims, preferred_element_type=F32))
        dv = dv + jnp.concatenate(dvh, axis=1)
        dv_ext[0:tm, :] = dv
        dzl = cw_[3:4, :] * dv
        for k in range(3):
            dzl = dzl + cw_[k:k + 1, :] * dv_ext[pl.ds(3 - k, tm), :]
        dz_ref[...] = dzl.astype(dz_ref.dtype)
        dv_ext[tm:, :] = dv[:HALO]
        dcw = jnp.concatenate([jnp.sum(dv * shifted[k], axis=0, keepdims=True) for k in range(4)], axis=0)
        dcb = jnp.sum(dv, axis=0, keepdims=True)
        dbr = jnp.sum(drp, axis=0, keepdims=True)
        dbi = jnp.sum(dip, axis=0, keepdims=True)

        @pl.when(i == 0)
        def _():
            dcw_ref[...] = dcw
            dcb_ref[...] = dcb
            dbr_ref[...] = dbr
            dbi_ref[...] = dbi
            dlam_ref[...] = dlam
            wlo_acc[...] = dwlo
            for hd in range(LRU_HEADS):
                dwr_ref[hd] = dwr[hd]
                dwi_ref[hd] = dwi[hd]

        @pl.when(i > 0)
        def _():
            dcw_ref[...] += dcw
            dcb_ref[...] += dcb
            dbr_ref[...] += dbr
            dbi_ref[...] += dbi
            dlam_ref[...] += dlam
            wlo_acc[...] += dwlo
            for hd in range(LRU_HEADS):
                dwr_ref[hd] += dwr[hd]
                dwi_ref[hd] += dwi[hd]

        @pl.when(i == nt - 1)
        def _():
            dwlo_ref[...] = wlo_acc[...].astype(BF)

    fix2 = lambda i: (0, 0)
    fix3 = lambda i: (0, 0, 0)
    rev = lambda i: (nt - 1 - i, 0)
    vec = pl.BlockSpec((1, D), fix2)
    hw = pl.BlockSpec((LRU_HEADS, LRU_HD, LRU_HD), fix3)
    return pl.pallas_call(
        body, name=name, grid=(nt,),
        in_specs=[pl.BlockSpec(memory_space=pl.ANY),
                  pl.BlockSpec((tm, D), rev),
                  pl.BlockSpec((tm, D), lambda i: (nt - 1 - i, ZB_LRU)),
                  pl.BlockSpec((HALO, D), lambda i: (jnp.maximum((nt - 1 - i) * (tm // HALO) - 1, 0), ZB_LRU)),
                  pl.BlockSpec((tm, D), rev),
                  pl.BlockSpec((8, D), lambda i: (jnp.maximum((nt - 1 - i) * (tm // 8) - 1, 0), 0)),
                  pl.BlockSpec((4, D), fix2), vec, hw, vec, hw, vec, vec, pl.BlockSpec((D, D), fix2)],
        out_specs=[pl.BlockSpec((tm, D), lambda i: (nt - 1 - i, ZB_LRU)),
                   pl.BlockSpec((4, D), fix2), vec, hw, vec, hw, vec, vec, pl.BlockSpec((D, D), fix2)],
        out_shape=[jax.ShapeDtypeStruct(dz.shape, dz.dtype), jax.ShapeDtypeStruct((4, D), F32),
                   jax.ShapeDtypeStruct((1, D), F32), jax.ShapeDtypeStruct((LRU_HEADS, LRU_HD, LRU_HD), F32),
                   jax.ShapeDtypeStruct((1, D), F32), jax.ShapeDtypeStruct((LRU_HEADS, LRU_HD, LRU_HD), F32),
                   jax.ShapeDtypeStruct((1, D), F32), jax.ShapeDtypeStruct((1, D), F32),
                   jax.ShapeDtypeStruct((D, D), BF)],
        scratch_shapes=[pltpu.VMEM((tm, D), F32), pltpu.VMEM((tm, D), F32), pltpu.VMEM((tm, D), F32),
                        pltpu.VMEM((8, D), F32), pltpu.VMEM((tm + 8, D), F32), pltpu.VMEM((tm + HALO, D), F32),
                        pltpu.VMEM((tm + HALO, D), F32), pltpu.VMEM((tm + 8, D), F32), pltpu.VMEM((D, D), F32)],
        input_output_aliases={0: 0},
        compiler_params=_cp("arbitrary"))(dz, dyl, z, z, h, h, cw, cb, wr, br, wi, bi, lam, wlo)


def _sconv_cv(u_ext, sw):
    shifted = []
    cv = None
    for k in range(3):
        us = (u_ext if k == 2 else pltpu.roll(u_ext, 2 - k, axis=0))[HALO:]
        shifted.append(us)
        term = sw[k:k + 1, :] * us
        cv = term if cv is None else cv + term
    return cv, shifted


def sconv_fwd(z, sw, wso, *, tm, name):
    T = z.shape[0]
    tm = min(tm, T)

    def body(zm_ref, zh_ref, sw_ref, wso_ref, y_ref):
        i = pl.program_id(0)
        keep = jnp.where(i == 0, 0.0, 1.0).astype(F32)
        zm = zm_ref[...].astype(F32)
        zh = zh_ref[...].astype(F32)
        u_ext = jnp.concatenate([zh[:, D:2 * D] * zh[:, 2 * D:] * keep, zm[:, D:2 * D] * zm[:, 2 * D:]], axis=0)
        cv, _ = _sconv_cv(u_ext, sw_ref[...])
        s = (zm[:, :D] * cv).astype(BF)
        y_ref[...] = jnp.dot(s, wso_ref[...], preferred_element_type=F32).astype(BF)

    return pl.pallas_call(
        body, name=name, grid=(T // tm,),
        in_specs=[pl.BlockSpec((tm, 3 * D), lambda i: (i, ZB_SCONV)),
                  pl.BlockSpec((HALO, 3 * D), _prev_halo(tm, ZB_SCONV)),
                  pl.BlockSpec((3, D), lambda i: (0, 0)), pl.BlockSpec((D, D), lambda i: (0, 0))],
        out_specs=pl.BlockSpec((tm, D), lambda i: (i, 0)),
        out_shape=jax.ShapeDtypeStruct((T, D), BF),
        compiler_params=_cp("arbitrary"))(z, z, sw, wso)


def sconv_bwd(dz, dyc, z, sw, wso, *, tm, name):
    T = z.shape[0]
    tm = min(tm, T)
    nt = T // tm

    def body(dz_in, dy_ref, dyn_ref, zm_ref, zp_ref, zn_ref, sw_ref, wso_ref, dz_ref, dsw_ref, dwso_ref, acc_ref):
        del dz_in
        i = pl.program_id(0)
        keep_p = jnp.where(i == 0, 0.0, 1.0).astype(F32)
        keep_n = jnp.where(i == nt - 1, 0.0, 1.0).astype(F32)
        sw_ = sw_ref[...]
        zm = zm_ref[...].astype(F32)
        zp = zp_ref[...].astype(F32)
        zb, zc, zh = zm[:, :D], zm[:, D:2 * D], zm[:, 2 * D:]
        u_ext = jnp.concatenate([zp[:, D:2 * D] * zp[:, 2 * D:] * keep_p, zc * zh], axis=0)
        cv, shifted = _sconv_cv(u_ext, sw_)
        dy_ext = jnp.concatenate([dy_ref[...], dyn_ref[...]], axis=0)
        ds_ext = lax.dot_general(dy_ext, wso_ref[...], (((1,), (1,)), ((), ())), preferred_element_type=F32)
        zb_ext = jnp.concatenate([zb, zn_ref[...][:, :D].astype(F32) * keep_n], axis=0)
        dcv_ext = ds_ext * zb_ext
        du = sw_[2:3, :] * dcv_ext[:tm]
        for k in range(2):
            du = du + sw_[k:k + 1, :] * pltpu.roll(dcv_ext, tm + HALO - (2 - k), axis=0)[:tm]
        dz_ref[...] = jnp.concatenate([ds_ext[:tm] * cv, du * zh, du * zc], axis=1).astype(dz_ref.dtype)
        dcv = dcv_ext[:tm]
        dsw = jnp.concatenate([jnp.sum(dcv * shifted[k], axis=0, keepdims=True) for k in range(3)], axis=0)
        dwso = lax.dot_general((zb * cv).astype(BF), dy_ref[...], (((0,), (0,)), ((), ())),
                               preferred_element_type=F32)

        @pl.when(i == 0)
        def _():
            dsw_ref[...] = dsw
            acc_ref[...] = dwso

        @pl.when(i > 0)
        def _():
            dsw_ref[...] += dsw
            acc_ref[...] += dwso

        @pl.when(i == nt - 1)
        def _():
            dwso_ref[...] = acc_ref[...].astype(BF)

    return pl.pallas_call(
        body, name=name, grid=(nt,),
        in_specs=[pl.BlockSpec(memory_space=pl.ANY),
                  pl.BlockSpec((tm, D), lambda i: (i, 0)), pl.BlockSpec((HALO, D), _next_halo(tm, T, 0)),
                  pl.BlockSpec((tm, 3 * D), lambda i: (i, ZB_SCONV)),
                  pl.BlockSpec((HALO, 3 * D), _prev_halo(tm, ZB_SCONV)),
                  pl.BlockSpec((HALO, 3 * D), _next_halo(tm, T, ZB_SCONV)),
                  pl.BlockSpec((3, D), lambda i: (0, 0)), pl.BlockSpec((D, D), lambda i: (0, 0))],
        out_specs=[pl.BlockSpec((tm, 3 * D), lambda i: (i, ZB_SCONV)), pl.BlockSpec((3, D), lambda i: (0, 0)),
                   pl.BlockSpec((D, D), lambda i: (0, 0))],
        out_shape=[jax.ShapeDtypeStruct(dz.shape, dz.dtype), jax.ShapeDtypeStruct((3, D), F32),
                   jax.ShapeDtypeStruct((D, D), BF)],
        scratch_shapes=[pltpu.VMEM((D, D), F32)],
        input_output_aliases={0: 0}, compiler_params=_cp("arbitrary"))(dz, dyc, dyc, z, z, z, sw, wso)


def merge_fwd(z, yp_pre, yl, yc, ps, *, tm, name):
    T = z.shape[0]
    tm = min(tm, T)

    def body(zg_ref, yp_ref, yl_ref, yc_ref, ps_ref, o_ref):
        gts = _sigmoid(zg_ref[...].astype(F32))
        m = (gts[:, :D] * (yp_ref[...].astype(F32) * ps_ref[...]) + gts[:, D:2 * D] * yl_ref[...].astype(F32)
             + gts[:, 2 * D:] * yc_ref[...].astype(F32))
        o_ref[...] = m.astype(o_ref.dtype)

    row = lambda i: (i, 0)
    return pl.pallas_call(
        body, name=name, grid=(T // tm,),
        in_specs=[pl.BlockSpec((tm, 3 * D), lambda i: (i, ZB_GATE)), pl.BlockSpec((tm, D), row),
                  pl.BlockSpec((tm, D), row), pl.BlockSpec((tm, D), row), pl.BlockSpec((1, D), lambda i: (0, 0))],
        out_specs=pl.BlockSpec((tm, D), row), out_shape=jax.ShapeDtypeStruct((T, D), BF),
        compiler_params=_cp("arbitrary"))(z, yp_pre, yl, yc, ps)


def merge_bwd(dm, z, yp_pre, yl, yc, ps, *, tm, name):
    T = z.shape[0]
    tm = min(tm, T)

    def body(dm_ref, zg_ref, yp_ref, yl_ref, yc_ref, ps_ref, dz_ref, dyp_ref, dyl_ref, dyc_ref):
        gts = _sigmoid(zg_ref[...].astype(F32))
        dmv = dm_ref[...].astype(F32)
        ys = (yp_ref[...].astype(F32) * ps_ref[...], yl_ref[...].astype(F32), yc_ref[...].astype(F32))
        outs = (dyp_ref, dyl_ref, dyc_ref)
        dgs = []
        for j in range(3):
            gj = gts[:, j * D:(j + 1) * D]
            outs[j][...] = (dmv * gj).astype(BF)
            dgs.append(dmv * ys[j] * gj * (1.0 - gj))
        dz_ref[...] = jnp.concatenate(dgs, axis=1).astype(dz_ref.dtype)

    row = lambda i: (i, 0)
    return pl.pallas_call(
        body, name=name, grid=(T // tm,),
        in_specs=[pl.BlockSpec((tm, D), row), pl.BlockSpec((tm, 3 * D), lambda i: (i, ZB_GATE)),
                  pl.BlockSpec((tm, D), row), pl.BlockSpec((tm, D), row), pl.BlockSpec((tm, D), row),
                  pl.BlockSpec((1, D), lambda i: (0, 0))],
        out_specs=[pl.BlockSpec((tm, 3 * D), lambda i: (i, ZB_GATE)), pl.BlockSpec((tm, D), row),
                   pl.BlockSpec((tm, D), row), pl.BlockSpec((tm, D), row)],
        out_shape=[jax.ShapeDtypeStruct((T, IN_COLS), BF), jax.ShapeDtypeStruct((T, D), BF),
                   jax.ShapeDtypeStruct((T, D), BF), jax.ShapeDtypeStruct((T, D), BF)],
        compiler_params=_cp("arbitrary"))(dm, z, yp_pre, yl, yc, ps)


def _attn_probs(qh, kh):
    s = lax.dot_general(qh, kh, (((1,), (1,)), ((), ())), preferred_element_type=F32) * (X_HD ** -0.5)
    e = jnp.exp(s - jnp.max(s, axis=-1, keepdims=True))
    return e / jnp.sum(e, axis=-1, keepdims=True)


def attn_fwd(xb, wq, kb, vb, *, tm, name):
    T = xb.shape[0]
    tm = min(tm, T)

    def body(x_ref, wq_ref, k_ref, v_ref, q_ref, o_ref):
        q = jnp.dot(x_ref[...], wq_ref[...], preferred_element_type=F32).astype(BF)
        q_ref[...] = q
        outs = []
        for h in range(X_HEADS):
            sl = slice(h * X_HD, (h + 1) * X_HD)
            p = _attn_probs(q[:, sl], k_ref[:, sl])
            outs.append(jnp.dot(p.astype(BF), v_ref[:, sl], preferred_element_type=F32))
        o_ref[...] = jnp.concatenate(outs, axis=1).astype(BF)

    row = lambda i: (i, 0)
    fix = lambda i: (0, 0)
    return pl.pallas_call(
        body, name=name, grid=(T // tm,),
        in_specs=[pl.BlockSpec((tm, D), row), pl.BlockSpec((D, D), fix), pl.BlockSpec((N_MEM, D), fix),
                  pl.BlockSpec((N_MEM, D), fix)],
        out_specs=[pl.BlockSpec((tm, D), row), pl.BlockSpec((tm, D), row)],
        out_shape=[jax.ShapeDtypeStruct((T, D), BF), jax.ShapeDtypeStruct((T, D), BF)],
        compiler_params=_cp("arbitrary"))(xb, wq, kb, vb)


def attn_bwd(dxa, wo, q, kb, vb, xb, *, tm, name):
    T = q.shape[0]
    tm = min(tm, T)
    nt = T // tm

    def body(d_ref, wo_ref, q_ref, k_ref, v_ref, x_ref, dq_ref, dk_ref, dv_ref, dwq_ref, acc_ref):
        i = pl.program_id(0)
        do = lax.dot_general(d_ref[...], wo_ref[...], (((1,), (1,)), ((), ())),
                             preferred_element_type=F32).astype(BF)
        q = q_ref[...]
        dqs, dks, dvs = [], [], []
        for h in range(X_HEADS):
            sl = slice(h * X_HD, (h + 1) * X_HD)
            kh, vh = k_ref[:, sl], v_ref[:, sl]
            p = _attn_probs(q[:, sl], kh)
            dp = lax.dot_general(do[:, sl], vh, (((1,), (1,)), ((), ())), preferred_element_type=F32)
            ds = (p * (dp - jnp.sum(dp * p, axis=-1, keepdims=True)) * (X_HD ** -0.5)).astype(BF)
            dqs.append(jnp.dot(ds, kh, preferred_element_type=F32))
            dks.append(lax.dot_general(ds, q[:, sl], (((0,), (0,)), ((), ())), preferred_element_type=F32))
            dvs.append(lax.dot_general(p.astype(BF), do[:, sl], (((0,), (0,)), ((), ())),
                                       preferred_element_type=F32))
        dqb = jnp.concatenate(dqs, axis=1).astype(BF)
        dq_ref[...] = dqb
        dk = jnp.concatenate(dks, axis=1)
        dv = jnp.concatenate(dvs, axis=1)
        dwq = lax.dot_general(x_ref[...], dqb, (((0,), (0,)), ((), ())), preferred_element_type=F32)

        @pl.when(i == 0)
        def _():
            dk_ref[...] = dk
            dv_ref[...] = dv
            acc_ref[...] = dwq

        @pl.when(i > 0)
        def _():
            dk_ref[...] += dk
            dv_ref[...] += dv
            acc_ref[...] += dwq

        @pl.when(i == nt - 1)
        def _():
            dwq_ref[...] = acc_ref[...].astype(BF)

    row = lambda i: (i, 0)
    fix = lambda i: (0, 0)
    return pl.pallas_call(
        body, name=name, grid=(nt,),
        in_specs=[pl.BlockSpec((tm, D), row), pl.BlockSpec((D, D), fix), pl.BlockSpec((tm, D), row),
                  pl.BlockSpec((N_MEM, D), fix), pl.BlockSpec((N_MEM, D), fix), pl.BlockSpec((tm, D), row)],
        out_specs=[pl.BlockSpec((tm, D), row), pl.BlockSpec((N_MEM, D), fix), pl.BlockSpec((N_MEM, D), fix),
                   pl.BlockSpec((D, D), fix)],
        out_shape=[jax.ShapeDtypeStruct((T, D), BF), jax.ShapeDtypeStruct((N_MEM, D), F32),
                   jax.ShapeDtypeStruct((N_MEM, D), F32), jax.ShapeDtypeStruct((D, D), BF)],
        scratch_shapes=[pltpu.VMEM((D, D), F32)],
        compiler_params=_cp("arbitrary"))(dxa, wo, q, kb, vb, xb)


TM_MM = 1024
TM_EW = 512
TM_SEQ = 512
TT_DW = 2048


def _mem_kv(l, memb, W):
    kb = mm_nn(memb, W['xa_w_k'][l], None, out_dtype=BF, tm=N_MEM, tn=1024, name=f"l{l}_mem_k")
    vb = mm_nn(memb, W['xa_w_v'][l], None, out_dtype=BF, tm=N_MEM, tn=1024, name=f"l{l}_mem_v")
    return kb, vb


def _layer_fwd(l, x, xb, kb, vb, W, host=None, host2=None, after_in_proj=None):
    n = f"l{l}_"
    sv = {'x0': x if xb is None else xb}
    z = mm_nn(sv['x0'], W['w_in'][l], W['b_in'][l], out_dtype=BF, tm=2 * TM_MM, tn=1024, name=n + "in_proj",
              host=host)
    if host is not None:
        z, sv['hosted'] = z
        if after_in_proj is not None:
            after_in_proj(sv['hosted'])
    if kb is None:
        kb, vb = _mem_kv(l, vb, W)
    sv['kb'], sv['vb'] = kb, vb
    yp = pool_fwd(z, W['pool_w'][l], tm=TM_SEQ, name=n + "pool_fwd")
    h, yl = lru_fwd(z, W['lru_conv_w'][l], W['lru_conv_b'][l], W['lru_w_r'][l], W['lru_b_r'][l], W['lru_w_i'][l],
                    W['lru_b_i'][l], W['lru_lambda'][l], W['lru_w_out'][l], tm=TM_SEQ, name=n + "lru_fwd")
    yc = sconv_fwd(z, W['sconv_w'][l], W['sconv_w_out'][l], tm=TM_SEQ, name=n + "sconv_fwd")
    merged = merge_fwd(z, yp, yl, yc, W['pool_scale'][l], tm=TM_EW, name=n + "merge_fwd")
    x1, x1b, xh1, rs1 = mm_res_ln(merged, W['w_mix_out'][l], x, W['ln_g'][l][0:1], W['ln_b'][l][0:1], tm=TM_MM,
                                  name=n + "mix_out_ln")
    q, o = attn_fwd(x1b, W['xa_w_q'][l], kb, vb, tm=TM_EW, name=n + "attn_fwd")
    x2, x2b, xh2, rs2 = mm_res_ln(o, W['xa_w_o'][l], x1, W['ln_g'][l][1:2], W['ln_b'][l][1:2], tm=TM_MM,
                                  name=n + "attn_out_ln")
    res = ffn_in_swiglu(x2b, W['ffn_w_gu'][l], tm=TM_MM, name=n + "ffn_in", host=host2)
    gu, hdn = res[:2]
    if host2 is not None:
        sv['hosted2'] = res[2]
    x3, x3b, xh3, rs3 = mm_res_ln(hdn, W['ffn_w_down'][l], x2, W['ln_g'][l][2:3], W['ln_b'][l][2:3], tm=TM_EW,
                                  name=n + "ffn_out_ln")
    sv.update(z=z, yp=yp, h=h, yl=yl, yc=yc, merged=merged, x1b=x1b, xh1=xh1, rs1=rs1, q=q, o=o, x2b=x2b,
              xh2=xh2, rs2=rs2, gu=gu, hdn=hdn, xh3=xh3, rs3=rs3)
    return x3, x3b, sv


def _layer_bwd(l, dx3, sv, memb, kb, vb, W, loss_from=None, chain=None, last_host_fn=None):
    n = f"l{l}_"
    G = {}
    res = ln_bwd(dx3, sv['xh3'], sv['rs3'], W['ln_g'][l][2:3], tm=TM_EW, name=n + "ln3_bwd", loss_from=loss_from,
                 dw_of=sv['hdn'])
    dp3, dp3b, dg3, db3 = res[:4]
    loss = res[4] if loss_from is not None else None
    G['ffn_w_down'] = res[-1]
    host_b = host_c = None
    dgu = ffn_down_dx_swiglu(dp3b, W['ffn_w_down'][l], sv['gu'], tm=TM_EW, name=n + "ffn_down_dx",
                             host=None if chain is None else chain[0])
    if chain is not None:
        dgu, got = dgu
        host_b = chain[1](got)
    dx2 = mm_nt(dgu, W['ffn_w_gu'][l], dp3, out_dtype=F32, tm=TM_MM, tc=1408, name=n + "ffn_in_dx", host=host_b)
    if chain is not None:
        dx2, got = dx2
        host_c = chain[2](got)
    G['ffn_w_gu'] = mm_tn(sv['x2b'], dgu, out_dtype=BF, tk=1024, tn=1408, tt=TT_DW,name=n + "ffn_in_dw")

    dp2, dp2b, dg2, db2, G['xa_w_o'] = ln_bwd(dx2, sv['xh2'], sv['rs2'], W['ln_g'][l][1:2], tm=TM_EW,
                                              name=n + "ln2_bwd", dw_of=sv['o'])
    dq, dk, dv, G['xa_w_q'] = attn_bwd(dp2b, W['xa_w_o'][l], sv['q'], kb, vb, sv['x1b'], tm=TM_EW,
                                       name=n + "attn_bwd")
    dx1 = mm_nt(dq, W['xa_w_q'][l], dp2, out_dtype=F32, tm=TM_MM, tc=D, name=n + "attn_q_dx", host=host_c)
    if chain is not None:
        dx1, got = dx1
        chain[3](got)
    G['xa_w_k'] = mm_tn(memb, dk, out_dtype=BF, tk=1024, tn=1024, tt=N_MEM, name=n + "attn_k_dw")
    G['xa_w_v'] = mm_tn(memb, dv, out_dtype=BF, tk=1024, tn=1024, tt=N_MEM, name=n + "attn_v_dw")

    dp1, dp1b, dg1, db1, G['w_mix_out'] = ln_bwd(dx1, sv['xh1'], sv['rs1'], W['ln_g'][l][0:1], tm=TM_EW,
                                                 name=n + "ln1_bwd", dw_of=sv['merged'])
    dmerged = mm_nt(dp1b, W['w_mix_out'][l], None, out_dtype=BF, tm=TM_MM, tc=D, name=n + "mix_out_dx")
    z = sv['z']
    dz, dyp, dyl, dyc = merge_bwd(dmerged, z, sv['yp'], sv['yl'], sv['yc'], W['pool_scale'][l], tm=TM_EW,
                                  name=n + "merge_bwd")
    dz, G['pool_w'], G['pool_scale'] = pool_bwd(dz, dyp, sv['yp'], z, W['pool_w'][l], W['pool_scale'][l],
                                                tm=TM_SEQ, name=n + "pool_bwd")
    (dz, G['lru_conv_w'], G['lru_conv_b'], G['lru_w_r'], G['lru_b_r'], G['lru_w_i'], G['lru_b_i'], G['lru_lambda'],
     G['lru_w_out']) = lru_bwd(dz, dyl, z, sv['h'], W['lru_conv_w'][l], W['lru_conv_b'][l], W['lru_w_r'][l],
                               W['lru_b_r'][l], W['lru_w_i'][l], W['lru_b_i'][l], W['lru_lambda'][l],
                               W['lru_w_out'][l], tm=TM_SEQ, name=n + "lru_bwd")
    dz, G['sconv_w'], G['sconv_w_out'] = sconv_bwd(dz, dyc, z, W['sconv_w'][l], W['sconv_w_out'][l], tm=TM_SEQ,
                                                   name=n + "sconv_bwd")
    G['w_in'], G['b_in'] = mm_tn(sv['x0'], dz, out_dtype=BF, tk=1024, tn=1024, tt=TT_DW, name=n + "in_proj_dw",
                                 colsum=True)
    G['ln_g'] = jnp.concatenate([dg1, dg2, dg3], axis=0)
    G['ln_b'] = jnp.concatenate([db1, db2, db3], axis=0)
    last_host = None if last_host_fn is None else last_host_fn(G)
    dx0 = mm_nt(dz, W['w_in'][l], dp1, out_dtype=F32, tm=TM_MM, tc=2048, name=n + "in_proj_dx", host=last_host)
    if last_host is not None:
        dx0, G['hosted_last'] = dx0
    return dx0, G, loss


def local_step(x, mem, target, W):
    memb = mem.astype(BF)
    saves, kvs = [], []
    xf, xb = x, None
    for l in range(DEPTH):
        kb = mm_nn(memb, W['xa_w_k'][l], None, out_dtype=BF, tm=N_MEM, tn=1024, name=f"l{l}_mem_k")
        vb = mm_nn(memb, W['xa_w_v'][l], None, out_dtype=BF, tm=N_MEM, tn=1024, name=f"l{l}_mem_v")
        xf, xb, sv = _layer_fwd(l, xf, xb, kb, vb, W)
        saves.append(sv)
        kvs.append((kb, vb))
    grads = [None] * DEPTH
    dx, loss = None, None
    for l in reversed(range(DEPTH)):
        lf = (W['ln_b'][l][2:3], target) if l == DEPTH - 1 else None
        dx, grads[l], ls = _layer_bwd(l, dx, saves[l], memb, kvs[l][0], kvs[l][1], W, loss_from=lf)
        if ls is not None:
            loss = ls
    return loss, dx, grads


def _coords():
    return lax.axis_index("x"), lax.axis_index("y"), lax.axis_index("c")


FLIPS = ((1, 0), (0, 1), (1, 1))
SQUARES = ('lru_w_out', 'sconv_w_out', 'w_mix_out', 'xa_w_q', 'xa_w_k', 'xa_w_v', 'xa_w_o')
LAYER_SHAPE = {'w_in': (D, IN_COLS), 'pool_w': (4, POOL_GD, POOL_GD), 'ffn_w_gate': (4, D, D_FF // 4),
               'ffn_w_up': (4, D, D_FF // 4), 'ffn_w_down': (D_FF, D), **{n: (D, D) for n in SQUARES}}
PIECES = ('w_in', 'pool_w') + SQUARES + ('ffn_w_gate', 'ffn_w_up', 'ffn_w_down')


def _mult(v, m):
    return v if isinstance(v, int) else pl.multiple_of(v, m)


def _win(name, ref, k):
    if name == 'w_in':
        return ref.at[:, pl.ds(_mult(((2 * k + 6) % 8) * D, D), 2 * D)]
    if name == 'pool_w':
        return ref.at[:, pl.ds(_mult(k * (POOL_GD // 4), POOL_GD // 4), POOL_GD // 4), :]
    if name in ('ffn_w_gate', 'ffn_w_up'):
        return ref.at[k]
    rows = LAYER_SHAPE[name][0] // 4
    return ref.at[pl.ds(_mult(k * rows, 16), rows), :]


def _half_shape(name):
    shard = _shard_shape(name)
    return (shard[0] // 2,) + shard[1:]


def _half(name, ref, h):
    rows = _shard_shape(name)[0] // 2
    if name == 'pool_w':
        return ref.at[pl.ds(h * rows, rows)]
    return ref.at[pl.ds(_mult(h * rows, 16), rows), :]


def _shard_shape(n):
    shp = LAYER_SHAPE[n]
    if n == 'w_in':
        return (shp[0], shp[1] // 4)
    if n == 'pool_w':
        return (shp[0], shp[1] // 4, shp[2])
    if n in ('ffn_w_gate', 'ffn_w_up'):
        return shp[1:]
    return (shp[0] // 4, shp[1])


def sum_chips_small(q, recv3, chip):
    r, C = q.shape
    slot_of_xor = {2: 0, 1: 1, 3: 2}

    def body(me_ref, q_ref, r_ref, o_ref):
        me = me_ref[0]
        acc = None
        for k in range(4):
            kx = k ^ me
            term = q_ref[...]
            for xv, j in slot_of_xor.items():
                term = jnp.where(kx == xv, r_ref[j], term)
            acc = term if acc is None else acc + term
        o_ref[...] = acc

    return pl.pallas_call(
        body, name="sum_chips_small",
        grid_spec=pltpu.PrefetchScalarGridSpec(
            num_scalar_prefetch=1, grid=(1,),
            in_specs=[pl.BlockSpec((r, C), lambda i, me: (0, 0)), pl.BlockSpec((3, r, C), lambda i, me: (0, 0, 0))],
            out_specs=pl.BlockSpec((r, C), lambda i, me: (0, 0))),
        out_shape=jax.ShapeDtypeStruct((r, C), F32), compiler_params=_cp("arbitrary"))(chip, q, recv3)


def _row_tile(R, C):
    for cand in (1024, 512, 256, 128, 64, 32, 16):
        if R % cand == 0 and cand * C * 4 <= 2 * 1024 * 1024:
            return cand
    return R


def _adamw_math(w, g, m, v):
    mn = ADAM_B1 * m + (1.0 - ADAM_B1) * g
    vn = ADAM_B2 * v + (1.0 - ADAM_B2) * (g * g)
    m_hat = mn / (1.0 - ADAM_B1 ** ADAM_STEP)
    v_hat = vn / (1.0 - ADAM_B2 ** ADAM_STEP)
    return -ADAM_LR * (m_hat / (jnp.sqrt(v_hat) + ADAM_EPS) + ADAM_WD * w), mn, vn


def adamw(w, g, m, v, *, name):
    shp = w.shape
    args = [t.reshape(-1, shp[-1]) for t in (w, g, m, v)]
    R, C = args[0].shape
    tr = _row_tile(R, C)

    def body(w_ref, g_ref, m_ref, v_ref, d_ref, mo_ref, vo_ref):
        d_ref[...], mo_ref[...], vo_ref[...] = _adamw_math(w_ref[...], g_ref[...], m_ref[...], v_ref[...])

    spec = pl.BlockSpec((tr, C), lambda i: (i, 0))
    res = pl.pallas_call(
        body, name=name, grid=(R // tr,), in_specs=[spec] * 4, out_specs=[spec] * 3,
        out_shape=[jax.ShapeDtypeStruct((R, C), F32)] * 3, compiler_params=_cp("arbitrary"))(*args)
    return [r.reshape(shp) for r in res]


def _remote(src, dst, send_sems, recv_sems, k, peer):
    return pltpu.make_async_remote_copy(src_ref=src, dst_ref=dst, send_sem=send_sems.at[k], recv_sem=recv_sems.at[k],
                                        device_id=peer, device_id_type=MESH)


def gather_layer(l, shards, small=None, pieces=PIECES):
    n_p = len(pieces)
    with_small = small is not None

    def descs(h_in, h_out, sems):
        srcs = dict(zip(pieces, h_in[:n_p]))
        outs = dict(zip(pieces, h_out[:n_p]))
        ici_s, ici_r, d2d_s, d2d_r, own_s, own_r = sems
        x, y, c = _coords()
        me = 2 * x + y
        sib = (x, y, 1 - c)
        ici, fwd, fwd_in, own = [], [], [], []
        for j, (fx, fy) in enumerate(FLIPS):
            peer = (x ^ fx, y ^ fy, c)
            other = 2 * (x ^ fx) + (y ^ fy)
            for p, n in enumerate(pieces):
                k = 3 * p + j
                mine = _half(n, _win(n, outs[n], me), c)
                landed = _half(n, _win(n, outs[n], other), c)
                sib_half = _half(n, _win(n, outs[n], other), 1 - c)
                ici.append((_remote(_half(n, srcs[n].at[l], c), mine, ici_s, ici_r, k, peer),
                            _remote(_half(n, srcs[n].at[l], c), landed, ici_s, ici_r, k, peer)))
                fwd.append(_remote(landed, landed, d2d_s, d2d_r, k, sib))
                fwd_in.append(_remote(sib_half, sib_half, d2d_s, d2d_r, k, sib))
            if with_small:
                k = 3 * n_p + j
                ici.append((_remote(h_in[n_p], h_out[n_p].at[me], ici_s, ici_r, k, peer),
                            _remote(h_in[n_p], h_out[n_p].at[other], ici_s, ici_r, k, peer)))
        for p, n in enumerate(pieces):
            own.append(_remote(srcs[n].at[l], _win(n, outs[n], me), own_s, own_r, p, sib))
        if with_small:
            own.append(_remote(h_in[n_p], h_out[n_p].at[me], own_s, own_r, n_p, sib))
        return ici, fwd, fwd_in, own

    def start(h_in, h_out, sems):
        ici, _, _, own = descs(h_in, h_out, sems)
        for send, _ in ici:
            send.start()
        for cp in own:
            cp.start()

    def finish(h_in, h_out, sems):
        ici, fwd, fwd_in, own = descs(h_in, h_out, sems)
        per_chip = n_p + (1 if with_small else 0)
        for j in range(3):
            for p in range(n_p):
                ici[j * per_chip + p][1].wait_recv()
                fwd[j * n_p + p].start()
            if with_small:
                ici[j * per_chip + n_p][1].wait_recv()
        for cp in fwd_in:
            cp.wait_recv()
        for send, _ in ici:
            send.wait_send()
        for cp in fwd:
            cp.wait_send()
        for cp in own:
            cp.wait()

    arrays = [shards[n] for n in pieces] + ([small] if with_small else [])
    out_shape = [jax.ShapeDtypeStruct(LAYER_SHAPE[n], BF) for n in pieces]
    if with_small:
        out_shape.append(jax.ShapeDtypeStruct((4,) + small.shape, small.dtype))
    sems = [pltpu.SemaphoreType.DMA((3 * n_p + 3,)), pltpu.SemaphoreType.DMA((3 * n_p + 3,)),
            pltpu.SemaphoreType.DMA((3 * n_p,)), pltpu.SemaphoreType.DMA((3 * n_p,)),
            pltpu.SemaphoreType.DMA((n_p + 1,)), pltpu.SemaphoreType.DMA((n_p + 1,))]
    return Hosted(arrays, out_shape, sems, start, finish)


def both_hosted(h1, h2):
    a1, o1, s1 = len(h1.arrays), len(h1.out_shape), len(h1.sems)

    def start(h_in, h_out, sems):
        h1.start(h_in[:a1], h_out[:o1], sems[:s1])
        h2.start(h_in[a1:], h_out[o1:], sems[s1:])

    def finish(h_in, h_out, sems):
        h1.finish(h_in[:a1], h_out[:o1], sems[:s1])
        h2.finish(h_in[a1:], h_out[o1:], sems[s1:])

    return Hosted(list(h1.arrays) + list(h2.arrays), list(h1.out_shape) + list(h2.out_shape),
                  list(h1.sems) + list(h2.sems), start, finish)


def run_hosted(host, name):
    n_in, n_out = len(host.arrays), len(host.out_shape)

    def body(*refs):
        h_in, h_out, sems = refs[:n_in], refs[n_in:n_in + n_out], refs[n_in + n_out:]
        host.start(h_in, h_out, sems)
        host.finish(h_in, h_out, sems)

    return pl.pallas_call(body, name=name, in_specs=[HBM_SPEC] * n_in, out_specs=[HBM_SPEC] * n_out,
                          out_shape=list(host.out_shape), scratch_shapes=list(host.sems))(*host.arrays)


def split_halves_hosted(parts, small):
    n_p = len(PIECES)
    rh = small.shape[0] // 2

    def descs(h_in, h_out, sems):
        send_sems, recv_sems = sems
        x, y, c = _coords()
        sib = (x, y, 1 - c)
        cps = []
        for p, n in enumerate(PIECES):
            for k in range(4):
                cps.append(_remote(_half(n, _win(n, h_in[p], k), 1 - c), h_out[p].at[k], send_sems, recv_sems,
                                   4 * p + k, sib))
        cps.append(_remote(h_in[n_p].at[pl.ds(_mult((1 - c) * rh, 8), rh), :], h_out[n_p], send_sems, recv_sems,
                           4 * n_p, sib))
        return cps

    out_shape = [jax.ShapeDtypeStruct((4,) + _half_shape(n), BF) for n in PIECES]
    out_shape.append(jax.ShapeDtypeStruct((rh, small.shape[1]), small.dtype))
    sems = [pltpu.SemaphoreType.DMA((4 * n_p + 1,)), pltpu.SemaphoreType.DMA((4 * n_p + 1,))]
    return _all_at_once([parts[n] for n in PIECES] + [small], out_shape, sems, descs)


def _all_at_once(arrays, out_shape, sems, descs):
    def start(h_in, h_out, s):
        for cp in descs(h_in, h_out, s):
            cp.start()

    def finish(h_in, h_out, s):
        for cp in descs(h_in, h_out, s):
            cp.wait()

    return Hosted(arrays, out_shape, sems, start, finish)


def swap_cores_hosted(s):
    n = len(s)

    def descs(h_in, h_out, sems):
        x, y, c = _coords()
        return [_remote(h_in[i], h_out[i], sems[0], sems[1], i, (x, y, 1 - c)) for i in range(n)]

    return _all_at_once(list(s), [jax.ShapeDtypeStruct(a.shape, a.dtype) for a in s],
                        [pltpu.SemaphoreType.DMA((n,)), pltpu.SemaphoreType.DMA((n,))], descs)


SAME_SHAPE = (('w_in',), ('pool_w',), SQUARES, ('ffn_w_gate', 'ffn_w_up'), ('ffn_w_down',))


def add_halves(names, parts, theirs, core):
    name = names[0]
    half = _half_shape(name)
    zero = (0,) * len(half)
    if name == 'w_in':
        pspec = pl.BlockSpec(half, lambda k, cc: (cc[0], (k + 3) % 4))
    elif name == 'pool_w':
        pspec = pl.BlockSpec(half, lambda k, cc: (cc[0], k, 0))
    elif name in ('ffn_w_gate', 'ffn_w_up'):
        pspec = pl.BlockSpec((None,) + half, lambda k, cc: (k, cc[0], 0))
    else:
        pspec = pl.BlockSpec(half, lambda k, cc: (2 * k + cc[0], 0))
    slot = pl.BlockSpec((None,) + half, lambda k, cc: (k,) + zero)
    m = len(names)

    def body(cc_ref, *refs):
        del cc_ref
        for i in range(m):
            refs[2 * m + i][...] = (refs[i][...].astype(F32) + refs[m + i][...].astype(F32)).astype(BF)

    res = pl.pallas_call(
        body, name="add_cores_" + name,
        grid_spec=pltpu.PrefetchScalarGridSpec(num_scalar_prefetch=1, grid=(4,), in_specs=[pspec] * m + [slot] * m,
                                               out_specs=[slot] * m),
        out_shape=[jax.ShapeDtypeStruct((4,) + half, BF)] * m, compiler_params=_cp("arbitrary"))(
            core, *[parts[n] for n in names], *[theirs[n] for n in names])
    return dict(zip(names, res))


def add_halves_small(small, theirs, core):
    rh, C = theirs.shape

    def body(cc_ref, p_ref, t_ref, o_ref):
        del cc_ref
        o_ref[...] = p_ref[...] + t_ref[...]

    blk = pl.BlockSpec((rh, C), lambda i, cc: (0, 0))
    return pl.pallas_call(
        body, name="add_cores_small",
        grid_spec=pltpu.PrefetchScalarGridSpec(
            num_scalar_prefetch=1, grid=(1,),
            in_specs=[pl.BlockSpec((rh, C), lambda i, cc: (cc[0], 0)), blk], out_specs=blk),
        out_shape=jax.ShapeDtypeStruct((rh, C), F32), compiler_params=_cp("arbitrary"))(core, small, theirs)


def exchange_halves(q, qsmall):
    n_p = len(PIECES)

    def descs(h_in, h_out, sems):
        send_sems, recv_sems = sems
        x, y, c = _coords()
        cps = []
        for j, (fx, fy) in enumerate(FLIPS):
            peer = (x ^ fx, y ^ fy, c)
            other = 2 * (x ^ fx) + (y ^ fy)
            for p in range(n_p):
                cps.append(_remote(h_in[p].at[other], h_out[p].at[j], send_sems, recv_sems, 3 * p + j, peer))
            cps.append(_remote(h_in[n_p], h_out[n_p].at[j], send_sems, recv_sems, 3 * n_p + j, peer))
        return cps

    def start(h_in, h_out, sems):
        for cp in descs(h_in, h_out, sems):
            cp.start()

    def finish(h_in, h_out, sems):
        for cp in descs(h_in, h_out, sems):
            cp.wait()

    arrays = [q[n] for n in PIECES] + [qsmall]
    out_shape = [jax.ShapeDtypeStruct((3,) + _half_shape(n), BF) for n in PIECES]
    out_shape.append(jax.ShapeDtypeStruct((3,) + qsmall.shape, qsmall.dtype))
    sems = [pltpu.SemaphoreType.DMA((3 * n_p + 3,)), pltpu.SemaphoreType.DMA((3 * n_p + 3,))]
    return Hosted(arrays, out_shape, sems, start, finish)


def sum_halves(names, q, recv3, chip):
    half = _half_shape(names[0])
    zero = (0,) * len(half)
    m = len(names)

    def body(me_ref, *refs):
        del me_ref
        for i in range(m):
            acc = refs[i][...].astype(F32)
            for j in range(3):
                acc = acc + refs[m + i][j].astype(F32)
            refs[2 * m + i][...] = acc

    res = pl.pallas_call(
        body, name="sum_chips_" + names[0],
        grid_spec=pltpu.PrefetchScalarGridSpec(
            num_scalar_prefetch=1, grid=(1,),
            in_specs=[pl.BlockSpec((None,) + half, lambda i, me: (me[0],) + zero)] * m
            + [pl.BlockSpec((3,) + half, lambda i, me: (0,) + zero)] * m,
            out_specs=[pl.BlockSpec(half, lambda i, me: zero)] * m),
        out_shape=[jax.ShapeDtypeStruct(half, F32)] * m, compiler_params=_cp("arbitrary"))(
            chip, *[q[n] for n in names], *[recv3[n] for n in names])
    return dict(zip(names, res))


def adamw_halves(w, g0, m, v, *, name):
    shp = w.shape
    C = shp[-1]
    four = (DEPTH, 2, -1, C)
    w4, m4, v4 = [t.reshape(four) for t in (w, m, v)]
    gs = [t.reshape(-1, C) for pair in g0 for t in pair]
    Rh = gs[0].shape[0]
    tr = _row_tile(Rh, C)

    def body(w_ref, a0_ref, b0_ref, a1_ref, b1_ref, m_ref, v_ref, g_ref, d_ref, mo_ref, vo_ref):
        mine = pl.program_id(1) == lax.axis_index("c")
        g_l0 = jnp.where(mine, a0_ref[...], b0_ref[...])
        g_l1 = jnp.where(mine, a1_ref[...], b1_ref[...])
        g = jnp.where(pl.program_id(0) == 0, g_l0, g_l1)
        g_ref[...] = g
        d_ref[...], mo_ref[...], vo_ref[...] = _adamw_math(w_ref[...], g, m_ref[...], v_ref[...])

    lay = pl.BlockSpec((None, None, tr, C), lambda l, h, i: (l, h, i, 0))
    one = pl.BlockSpec((tr, C), lambda l, h, i: (i, 0))
    res = pl.pallas_call(
        body, name=name, grid=(DEPTH, 2, Rh // tr), in_specs=[lay, one, one, one, one, lay, lay],
        out_specs=[lay] * 4, out_shape=[jax.ShapeDtypeStruct(w4.shape, F32)] * 4,
        compiler_params=_cp("arbitrary", "arbitrary", "arbitrary"))(w4, *gs, m4, v4)
    return [r.reshape(shp) for r in res]


def _local_shape(name, full_shape):
    shp = list(full_shape)
    ax = BIG_SHARDED.get(name, SMALL_SHARDED.get(name))
    if ax is not None:
        shp[ax] //= 4
    return tuple(shp)


FULL_SHAPES = {
    'w_in': (DEPTH, D, IN_COLS), 'b_in': (DEPTH, IN_COLS), 'pool_w': (DEPTH, 4, POOL_GD, POOL_GD),
    'pool_scale': (DEPTH, D), 'lru_conv_w': (DEPTH, 4, D), 'lru_conv_b': (DEPTH, D),
    'lru_w_r': (DEPTH, LRU_HEADS, LRU_HD, LRU_HD), 'lru_b_r': (DEPTH, D),
    'lru_w_i': (DEPTH, LRU_HEADS, LRU_HD, LRU_HD), 'lru_b_i': (DEPTH, D), 'lru_lambda': (DEPTH, D),
    'lru_w_out': (DEPTH, D, D), 'sconv_w': (DEPTH, 3, D), 'sconv_w_out': (DEPTH, D, D), 'w_mix_out': (DEPTH, D, D),
    'xa_w_q': (DEPTH, D, D), 'xa_w_k': (DEPTH, D, D), 'xa_w_v': (DEPTH, D, D), 'xa_w_o': (DEPTH, D, D),
    'ffn_w_gate': (DEPTH, D, D_FF), 'ffn_w_up': (DEPTH, D, D_FF), 'ffn_w_down': (DEPTH, D_FF, D),
    'ln_g': (DEPTH, 3, D), 'ln_b': (DEPTH, 3, D)}


def _pack(arrs, names, width, dtype, row_mult):
    flat = jnp.concatenate([arrs[n].astype(dtype).reshape(-1) for n in names])
    pad = (-flat.shape[0]) % (width * row_mult)
    if pad:
        flat = jnp.concatenate([flat, jnp.zeros((pad,), dtype)])
    return flat.reshape(-1, width)


def _unpack(flat2d, names, shapes):
    flat = flat2d.reshape(-1)
    out, off = {}, 0
    for n in names:
        size = 1
        for s in shapes[n]:
            size *= s
        out[n] = flat[off:off + size].reshape(shapes[n])
        off += size
    return out


def _gathered_full(g4, names, sharded_axis):
    loc_shapes = {n: _local_shape(n, FULL_SHAPES[n]) for n in names}
    per = [_unpack(g4[k], names, loc_shapes) for k in range(4)]
    return {n: jnp.concatenate([per[k][n] for k in range(4)], axis=sharded_axis[n]) for n in names}


def _gu_joined(g4, u4):
    return jnp.concatenate([g4[0], g4[1], u4[0], u4[1], g4[2], g4[3], u4[2], u4[3]], axis=1)


def _gu_apart(a):
    w = a.shape[1] // 8
    cols = [a[:, i * w:(i + 1) * w] for i in range(8)]
    return jnp.stack([cols[0], cols[1], cols[4], cols[5]]), jnp.stack([cols[2], cols[3], cols[6], cols[7]])


SMALL_SH_NAMES = list(SMALL_SHARDED)
SMALL_ROWS = 32


def kernel(x, mem, w_in, b_in, pool_w, pool_scale, lru_conv_w, lru_conv_b, lru_w_r, lru_b_r, lru_w_i, lru_b_i, lru_lambda, lru_w_out, sconv_w, sconv_w_out, w_mix_out, xa_w_q, xa_w_k, xa_w_v, xa_w_o, ffn_w_gate, ffn_w_up, ffn_w_down, ln_g, ln_b, loss_target, m_w_in, m_b_in, m_pool_w, m_pool_scale, m_lru_conv_w, m_lru_conv_b, m_lru_w_r, m_lru_b_r, m_lru_w_i, m_lru_b_i, m_lru_lambda, m_lru_w_out, m_sconv_w, m_sconv_w_out, m_w_mix_out, m_xa_w_q, m_xa_w_k, m_xa_w_v, m_xa_w_o, m_ffn_w_gate, m_ffn_w_up, m_ffn_w_down, m_ln_g, m_ln_b, v_w_in, v_b_in, v_pool_w, v_pool_scale, v_lru_conv_w, v_lru_conv_b, v_lru_w_r, v_lru_b_r, v_lru_w_i, v_lru_b_i, v_lru_lambda, v_lru_w_out, v_sconv_w, v_sconv_w_out, v_w_mix_out, v_xa_w_q, v_xa_w_k, v_xa_w_v, v_xa_w_o, v_ffn_w_gate, v_ffn_w_up, v_ffn_w_down, v_ln_g, v_ln_b):
    loc = dict(w_in=w_in, b_in=b_in, pool_w=pool_w, pool_scale=pool_scale, lru_conv_w=lru_conv_w,
               lru_conv_b=lru_conv_b, lru_w_r=lru_w_r, lru_b_r=lru_b_r, lru_w_i=lru_w_i, lru_b_i=lru_b_i,
               lru_lambda=lru_lambda, lru_w_out=lru_w_out, sconv_w=sconv_w, sconv_w_out=sconv_w_out,
               w_mix_out=w_mix_out, xa_w_q=xa_w_q, xa_w_k=xa_w_k, xa_w_v=xa_w_v, xa_w_o=xa_w_o,
               ffn_w_gate=ffn_w_gate, ffn_w_up=ffn_w_up, ffn_w_down=ffn_w_down, ln_g=ln_g, ln_b=ln_b)
    mom = dict(w_in=m_w_in, b_in=m_b_in, pool_w=m_pool_w, pool_scale=m_pool_scale, lru_conv_w=m_lru_conv_w,
               lru_conv_b=m_lru_conv_b, lru_w_r=m_lru_w_r, lru_b_r=m_lru_b_r, lru_w_i=m_lru_w_i, lru_b_i=m_lru_b_i,
               lru_lambda=m_lru_lambda, lru_w_out=m_lru_w_out, sconv_w=m_sconv_w, sconv_w_out=m_sconv_w_out,
               w_mix_out=m_w_mix_out, xa_w_q=m_xa_w_q, xa_w_k=m_xa_w_k, xa_w_v=m_xa_w_v, xa_w_o=m_xa_w_o,
               ffn_w_gate=m_ffn_w_gate, ffn_w_up=m_ffn_w_up, ffn_w_down=m_ffn_w_down, ln_g=m_ln_g, ln_b=m_ln_b)
    var = dict(w_in=v_w_in, b_in=v_b_in, pool_w=v_pool_w, pool_scale=v_pool_scale, lru_conv_w=v_lru_conv_w,
               lru_conv_b=v_lru_conv_b, lru_w_r=v_lru_w_r, lru_b_r=v_lru_b_r, lru_w_i=v_lru_w_i, lru_b_i=v_lru_b_i,
               lru_lambda=v_lru_lambda, lru_w_out=v_lru_w_out, sconv_w=v_sconv_w, sconv_w_out=v_sconv_w_out,
               w_mix_out=v_w_mix_out, xa_w_q=v_xa_w_q, xa_w_k=v_xa_w_k, xa_w_v=v_xa_w_v, xa_w_o=v_xa_w_o,
               ffn_w_gate=v_ffn_w_gate, ffn_w_up=v_ffn_w_up, ffn_w_down=v_ffn_w_down, ln_g=v_ln_g, ln_b=v_ln_b)

    chip = 2 * lax.axis_index("x") + lax.axis_index("y")
    core = lax.axis_index("c")
    chip_arr = jnp.reshape(chip, (1,)).astype(jnp.int32)
    core_arr = jnp.reshape(core, (1,)).astype(jnp.int32)
    n_p = len(PIECES)

    shards = {n: loc[n].astype(BF) for n in PIECES}
    small = _pack(loc, SMALL_SH_NAMES, 256, F32, 8)
    first = ('w_in',)
    rest = tuple(n for n in PIECES if n not in first)
    early1 = ('w_in', 'pool_w')
    late1 = tuple(n for n in PIECES if n not in early1)
    got = run_hosted(gather_layer(0, shards, small, pieces=first), "gather_first")
    vec = _gathered_full(got[len(first)], SMALL_SH_NAMES, SMALL_SHARDED)
    W = {n: [None] * DEPTH for n in ('w_in', 'pool_w', 'ffn_w_down', 'ffn_w_gu') + SQUARES}
    W['b_in'] = [jnp.roll(b_in[l:l + 1], -2 * D, axis=1) for l in range(DEPTH)]
    for n in ('lru_conv_w', 'sconv_w', 'ln_g', 'ln_b'):
        W[n] = [vec[n][l] for l in range(DEPTH)]
    for n in ('lru_w_r', 'lru_w_i'):
        W[n] = [loc[n][l].astype(BF) for l in range(DEPTH)]
    for n in ('pool_scale', 'lru_conv_b', 'lru_b_r', 'lru_b_i', 'lru_lambda'):
        W[n] = [loc[n][l:l + 1] for l in range(DEPTH)]

    def take(l, names, arrays):
        full = dict(zip(names, arrays))
        if 'ffn_w_gate' in full:
            W['ffn_w_gu'][l] = _gu_joined(full['ffn_w_gate'], full['ffn_w_up'])
        for n in names:
            if n in W:
                W[n][l] = full[n]

    take(0, first, got[:len(first)])

    def after_in_proj(results):
        take(0, rest, results[:len(rest)])
        take(1, early1, results[len(rest):])

    xs, memb = x[0], mem[0].astype(BF)
    host_a = both_hosted(gather_layer(0, shards, pieces=rest), gather_layer(1, shards, pieces=early1))
    xf, xb, sv0 = _layer_fwd(0, xs, None, None, memb, W, host=host_a, host2=gather_layer(1, shards, pieces=late1),
                             after_in_proj=after_in_proj)
    take(1, late1, sv0['hosted2'])
    xf, xb, sv1 = _layer_fwd(1, xf, xb, None, memb, W)
    saves = [sv0, sv1]
    kvs = [(sv['kb'], sv['vb']) for sv in saves]

    def packed(G):
        g = dict(G)
        g['ffn_w_gate'], g['ffn_w_up'] = _gu_apart(g['ffn_w_gu'])
        g['pool_w'] = g['pool_w'].astype(BF)
        g['b_in'] = jnp.roll(g['b_in'], 2 * D, axis=1)
        return {n: g[n] for n in PIECES}, _pack(g, SMALL_ALL, D, F32, SMALL_ROWS)

    def add_pairs(parts, smallp, got):
        theirs = dict(zip(PIECES, got[:n_p]))
        q = {}
        for group in SAME_SHAPE:
            q.update(add_halves(group, parts, theirs, core_arr))
        qs = add_halves_small(smallp, got[n_p], core_arr)
        return q, qs, exchange_halves(q, qs)

    def sum_chips(q, qs, recv):
        recv3 = dict(zip(PIECES, recv[:n_p]))
        summed = {}
        for group in SAME_SHAPE:
            summed.update(sum_halves(group, q, recv3, chip_arr))
        return [summed[n] for n in PIECES] + [sum_chips_small(qs, recv[n_p], chip_arr)]

    lf = (W['ln_b'][1][2:3], loss_target[0])
    dx, G1, loss_blk = _layer_bwd(1, None, saves[1], memb, kvs[1][0], kvs[1][1], W, loss_from=lf)

    parts1, smallp1 = packed(G1)
    st1, st0 = {}, {}

    def after_down(got):
        st1['q'], st1['qs'], host = add_pairs(parts1, smallp1, got)
        return host

    def after_in(got):
        st1['sums'] = sum_chips(st1['q'], st1['qs'], got)
        return swap_cores_hosted(st1['sums'])

    def after_q(got):
        st1['other'] = list(got)

    def last_host_fn(G):
        parts0, smallp0 = packed(G)
        got = run_hosted(split_halves_hosted(parts0, smallp0), "split_halves")
        st0['q'], st0['qs'], host = add_pairs(parts0, smallp0, got)
        return host

    grad_x, G0, _ = _layer_bwd(0, dx, saves[0], memb, kvs[0][0], kvs[0][1], W,
                               chain=(split_halves_hosted(parts1, smallp1), after_down, after_in, after_q),
                               last_host_fn=last_host_fn)
    sums0 = sum_chips(st0['q'], st0['qs'], G0['hosted_last'])
    red = [(sums0, run_hosted(swap_cores_hosted(sums0), "swap_cores")), (st1['sums'], st1['other'])]
    loss = lax.psum(loss_blk[0, 0], ("x", "y", "c"))

    small_shapes = {n: FULL_SHAPES[n][1:] for n in SMALL_ALL}
    per_layer = []
    for l in range(DEPTH):
        mine, theirs = red[l][0][-1], red[l][1][-1]
        whole = jnp.where(core == 0, jnp.concatenate([mine, theirs]), jnp.concatenate([theirs, mine]))
        per_layer.append(_unpack(whole, SMALL_ALL, small_shapes))
    grads = {}
    for n in SMALL_ALL:
        gn = jnp.stack([per_layer[l][n] for l in range(DEPTH)])
        if n in SMALL_SHARDED:
            size = loc[n].shape[SMALL_SHARDED[n]]
            gn = lax.dynamic_slice_in_dim(gn, chip * size, size, axis=SMALL_SHARDED[n])
        grads[n] = gn

    out_d, out_m, out_v = {}, {}, {}
    for p, n in enumerate(PIECES):
        pairs = [(red[l][0][p], red[l][1][p]) for l in range(DEPTH)]
        grads[n], out_d[n], out_m[n], out_v[n] = adamw_halves(loc[n], pairs, mom[n], var[n], name="adamw_" + n)
    for n in SMALL_ALL:
        out_d[n], out_m[n], out_v[n] = adamw(loc[n], grads[n], mom[n], var[n], name="adamw_" + n)

    return (loss, grad_x[None], *[grads[n] for n in WEIGHTS], *[out_d[n] for n in WEIGHTS],
            *[out_m[n] for n in WEIGHTS], *[out_v[n] for n in WEIGHTS])
```

```python
import jax
import jax.numpy as jnp
from jax import lax
from jax.experimental import pallas as pl
from jax.experimental.pallas import tpu as pltpu

F32 = jnp.float32
BF = jnp.bfloat16
MESH = pl.DeviceIdType.MESH

D = 1024
DEPTH = 2
N_MEM = 256
POOL_WINDOWS = (2, 4, 8, 16)
POOL_GD = 256
LRU_HEADS = 8
LRU_HD = 128
LRU_C = 8.0
X_HEADS = 4
X_HD = 256
D_FF = 2816
IN_COLS = 8 * D
ALPHA = (2 * DEPTH) ** 0.25
LN_EPS = 1e-5
ADAM_LR = 0.001
ADAM_B1 = 0.9
ADAM_B2 = 0.999
ADAM_EPS = 1e-08
ADAM_WD = 0.01
ADAM_STEP = 10

Z_PERM = (2, 3, 4, 5, 6, 7, 0, 1)
ZB_SCONV, ZB_GATE, ZB_POOL, ZB_LRU = 0, 1, 6, 7
HALO = 16
VMEM_LIMIT = 56 * 1024 * 1024

WEIGHTS = ['w_in', 'b_in', 'pool_w', 'pool_scale', 'lru_conv_w', 'lru_conv_b', 'lru_w_r', 'lru_b_r', 'lru_w_i',
           'lru_b_i', 'lru_lambda', 'lru_w_out', 'sconv_w', 'sconv_w_out', 'w_mix_out', 'xa_w_q', 'xa_w_k', 'xa_w_v',
           'xa_w_o', 'ffn_w_gate', 'ffn_w_up', 'ffn_w_down', 'ln_g', 'ln_b']
BIG_SHARDED = {'w_in': 2, 'pool_w': 2, 'lru_w_out': 1, 'sconv_w_out': 1, 'w_mix_out': 1, 'xa_w_q': 1, 'xa_w_k': 1,
               'xa_w_v': 1, 'xa_w_o': 1, 'ffn_w_gate': 2, 'ffn_w_up': 2, 'ffn_w_down': 1}
SMALL_SHARDED = {'lru_conv_w': 2, 'sconv_w': 2, 'ln_g': 2, 'ln_b': 2}
SMALL_ALL = ['b_in', 'pool_scale', 'lru_conv_w', 'lru_conv_b', 'lru_w_r', 'lru_b_r', 'lru_w_i', 'lru_b_i',
             'lru_lambda', 'sconv_w', 'ln_g', 'ln_b']


def _cp(*sem):
    return pltpu.CompilerParams(dimension_semantics=sem, vmem_limit_bytes=VMEM_LIMIT)


def _sigmoid(x):
    return 0.5 * jnp.tanh(0.5 * x) + 0.5


class Hosted:
    def __init__(self, arrays, out_shape, sems, start, finish):
        self.arrays, self.out_shape, self.sems, self.start, self.finish = arrays, out_shape, sems, start, finish


def _with_host(host, n_in, n_out, refs):
    if host is None:
        return refs[:n_in], (), refs[n_in:n_in + n_out], (), refs[n_in + n_out:], ()
    hi, ho, hs = len(host.arrays), len(host.out_shape), len(host.sems)
    ins, h_in = refs[:n_in], refs[n_in:n_in + hi]
    outs = refs[n_in + hi:n_in + hi + n_out]
    h_out = refs[n_in + hi + n_out:n_in + hi + n_out + ho]
    rest = refs[n_in + hi + n_out + ho:]
    return ins, h_in, outs, h_out, rest[:len(rest) - hs], rest[len(rest) - hs:]


HBM_SPEC = pl.BlockSpec(memory_space=pl.ANY)


def mm_nn(a, w, bias, *, out_dtype, tm, tn, name, host=None):
    T, K = a.shape
    N = w.shape[1]
    tm, tn = min(tm, T), min(tn, N)
    nj, ni = N // tn, T // tm
    n_in = 2 if bias is None else 3

    def body(*refs):
        ins, h_in, (o_ref,), h_out, _, h_sems = _with_host(host, n_in, 1, refs)
        a_ref, w_ref = ins[:2]
        j, i = pl.program_id(0), pl.program_id(1)
        if host is not None:
            @pl.when((j == 0) & (i == 0))
            def _():
                host.start(h_in, h_out, h_sems)
        acc = jnp.dot(a_ref[...].astype(BF), w_ref[...], preferred_element_type=F32)
        if bias is not None:
            acc = acc + ins[2][...]
        o_ref[...] = acc.astype(o_ref.dtype)
        if host is not None:
            @pl.when((j == nj - 1) & (i == ni - 1))
            def _():
                host.finish(h_in, h_out, h_sems)

    in_specs = [pl.BlockSpec((tm, K), lambda j, i: (i, 0)), pl.BlockSpec((K, tn), lambda j, i: (0, j))]
    args = [a, w]
    if bias is not None:
        in_specs.append(pl.BlockSpec((1, tn), lambda j, i: (0, j)))
        args.append(bias)
    out_specs = [pl.BlockSpec((tm, tn), lambda j, i: (i, j))]
    out_shape = [jax.ShapeDtypeStruct((T, N), out_dtype)]
    scratch = []
    if host is not None:
        in_specs += [HBM_SPEC] * len(host.arrays)
        args += list(host.arrays)
        out_specs += [HBM_SPEC] * len(host.out_shape)
        out_shape += list(host.out_shape)
        scratch = list(host.sems)
    res = pl.pallas_call(
        body, name=name, grid=(nj, ni), in_specs=in_specs, out_specs=out_specs, out_shape=out_shape,
        scratch_shapes=scratch, compiler_params=_cp("arbitrary", "arbitrary"))(*args)
    return res[0] if host is None else (res[0], res[1:])


FF_HALF = D_FF // 2


def ffn_in_swiglu(x, wgu, *, tm, name, host=None):
    T, K = x.shape
    tm = min(tm, T)
    ni = T // tm

    def body(*refs):
        (x_ref, w_ref), h_in, (gu_ref, h_ref), h_out, _, h_sems = _with_host(host, 2, 2, refs)
        j, i = pl.program_id(0), pl.program_id(1)
        if host is not None:
            @pl.when((j == 0) & (i == 0))
            def _():
                host.start(h_in, h_out, h_sems)
        acc = jnp.dot(x_ref[...].astype(BF), w_ref[...], preferred_element_type=F32)
        gu = acc.astype(BF)
        gu_ref[...] = gu
        g = gu[:, :FF_HALF].astype(F32)
        h_ref[...] = (g * _sigmoid(g) * gu[:, FF_HALF:].astype(F32)).astype(BF)
        if host is not None:
            @pl.when((j == 1) & (i == ni - 1))
            def _():
                host.finish(h_in, h_out, h_sems)

    in_specs = [pl.BlockSpec((tm, K), lambda j, i: (i, 0)), pl.BlockSpec((K, 2 * FF_HALF), lambda j, i: (0, j))]
    args = [x, wgu]
    out_specs = [pl.BlockSpec((tm, 2 * FF_HALF), lambda j, i: (i, j)), pl.BlockSpec((tm, FF_HALF), lambda j, i: (i, j))]
    out_shape = [jax.ShapeDtypeStruct((T, 2 * D_FF), BF), jax.ShapeDtypeStruct((T, D_FF), BF)]
    scratch = []
    if host is not None:
        in_specs += [HBM_SPEC] * len(host.arrays)
        args += list(host.arrays)
        out_specs += [HBM_SPEC] * len(host.out_shape)
        out_shape += list(host.out_shape)
        scratch = list(host.sems)
    res = pl.pallas_call(
        body, name=name, grid=(2, ni), in_specs=in_specs, out_specs=out_specs, out_shape=out_shape,
        scratch_shapes=scratch, compiler_params=_cp("arbitrary", "arbitrary"))(*args)
    return (res[0], res[1]) if host is None else (res[0], res[1], res[2:])


def ffn_down_dx_swiglu(dp, wd, gu, *, tm, name, host=None):
    T = dp.shape[0]
    tm = min(tm, T)
    ni = T // tm

    def body(*refs):
        (dp_ref, w_ref, gu_ref), h_in, (o_ref,), h_out, _, h_sems = _with_host(host, 3, 1, refs)
        if host is not None:
            j, i = pl.program_id(0), pl.program_id(1)

            @pl.when((j == 0) & (i == 0))
            def _():
                host.start(h_in, h_out, h_sems)

            @pl.when((j == 1) & (i == ni - 1))
            def _():
                host.finish(h_in, h_out, h_sems)
        dh = lax.dot_general(dp_ref[...], w_ref[...], (((1,), (1,)), ((), ())), preferred_element_type=F32)
        g = gu_ref[:, :FF_HALF].astype(F32)
        u = gu_ref[:, FF_HALF:].astype(F32)
        sg = _sigmoid(g)
        t = dh * sg
        s = g * sg
        o_ref[:, :FF_HALF] = (t * u * (1.0 + g - s)).astype(BF)
        o_ref[:, FF_HALF:] = (t * g).astype(BF)

    in_specs = [pl.BlockSpec((tm, D), lambda j, i: (i, 0)), pl.BlockSpec((FF_HALF, D), lambda j, i: (j, 0)),
                pl.BlockSpec((tm, 2 * FF_HALF), lambda j, i: (i, j))]
    args = [dp, wd, gu]
    out_specs = [pl.BlockSpec((tm, 2 * FF_HALF), lambda j, i: (i, j))]
    out_shape = [jax.ShapeDtypeStruct((T, 2 * D_FF), BF)]
    scratch = []
    if host is not None:
        in_specs += [HBM_SPEC] * len(host.arrays)
        args += list(host.arrays)
        out_specs += [HBM_SPEC] * len(host.out_shape)
        out_shape += list(host.out_shape)
        scratch = list(host.sems)
    res = pl.pallas_call(
        body, name=name, grid=(2, ni), in_specs=in_specs, out_specs=out_specs, out_shape=out_shape,
        scratch_shapes=scratch, compiler_params=_cp("arbitrary", "arbitrary"))(*args)
    return res[0] if host is None else (res[0], res[1:])


def mm_nt(a, w, res, *, out_dtype, tm, tc, name, host=None):
    T, C = a.shape
    K = w.shape[0]
    tm, tc = min(tm, T), min(tc, C)
    nc = C // tc
    ni = T // tm
    n_in = 2 if res is None else 3

    def body(*refs):
        ins, h_in, (o_ref,), h_out, (acc_ref,), h_sems = _with_host(host, n_in, 1, refs)
        a_ref, w_ref = ins[:2]
        r_ref = ins[2] if res is not None else None
        c = pl.program_id(1)
        if host is not None:
            @pl.when((pl.program_id(0) == 0) & (c == 0))
            def _():
                host.start(h_in, h_out, h_sems)

            @pl.when((pl.program_id(0) == ni - 1) & (c == nc - 1))
            def _():
                host.finish(h_in, h_out, h_sems)
        part = lax.dot_general(a_ref[...].astype(BF), w_ref[...], (((1,), (1,)), ((), ())),
                               preferred_element_type=F32)

        @pl.when(c == 0)
        def _():
            acc_ref[...] = part

        @pl.when(c > 0)
        def _():
            acc_ref[...] += part

        @pl.when(c == nc - 1)
        def _():
            out = acc_ref[...]
            if res is not None:
                out = out + ALPHA * r_ref[...]
            o_ref[...] = out.astype(o_ref.dtype)

    in_specs = [pl.BlockSpec((tm, tc), lambda i, c: (i, c)), pl.BlockSpec((K, tc), lambda i, c: (0, c))]
    args = [a, w]
    if res is not None:
        in_specs.append(pl.BlockSpec((tm, K), lambda i, c: (i, 0)))
        args.append(res)
    out_specs = [pl.BlockSpec((tm, K), lambda i, c: (i, 0))]
    out_shape = [jax.ShapeDtypeStruct((T, K), out_dtype)]
    scratch = [pltpu.VMEM((tm, K), F32)]
    if host is not None:
        in_specs += [HBM_SPEC] * len(host.arrays)
        args += list(host.arrays)
        out_specs += [HBM_SPEC] * len(host.out_shape)
        out_shape += list(host.out_shape)
        scratch += list(host.sems)
    out = pl.pallas_call(
        body, name=name, grid=(ni, nc), in_specs=in_specs, out_specs=out_specs, out_shape=out_shape,
        scratch_shapes=scratch, compiler_params=_cp("arbitrary", "arbitrary"))(*args)
    return out[0] if host is None else (out[0], out[1:])


def mm_tn(a, b, *, out_dtype, tk, tn, tt, name, colsum=False):
    T, K = a.shape
    N = b.shape[1]
    tk, tn, tt = min(tk, K), min(tn, N), min(tt, T)
    nt = T // tt

    def body(*refs):
        if colsum:
            a_ref, b_ref, o_ref, cs_ref, acc_ref = refs
        else:
            a_ref, b_ref, o_ref, acc_ref = refs
        i, t = pl.program_id(1), pl.program_id(2)
        bb = b_ref[...]
        part = lax.dot_general(a_ref[...].astype(BF), bb.astype(BF), (((0,), (0,)), ((), ())),
                               preferred_element_type=F32)

        @pl.when(t == 0)
        def _():
            acc_ref[...] = part

        @pl.when(t > 0)
        def _():
            acc_ref[...] += part

        @pl.when(t == nt - 1)
        def _():
            o_ref[...] = acc_ref[...].astype(o_ref.dtype)

        if colsum:
            s = jnp.sum(bb.astype(F32), axis=0, keepdims=True)

            @pl.when((i == 0) & (t == 0))
            def _():
                cs_ref[...] = s

            @pl.when((i == 0) & (t > 0))
            def _():
                cs_ref[...] += s

    out_specs = [pl.BlockSpec((tk, tn), lambda j, i, t: (i, j))]
    out_shape = [jax.ShapeDtypeStruct((K, N), out_dtype)]
    if colsum:
        out_specs.append(pl.BlockSpec((1, tn), lambda j, i, t: (0, j)))
        out_shape.append(jax.ShapeDtypeStruct((1, N), F32))
    res = pl.pallas_call(
        body, name=name, grid=(N // tn, K // tk, nt),
        in_specs=[pl.BlockSpec((tt, tk), lambda j, i, t: (t, i)), pl.BlockSpec((tt, tn), lambda j, i, t: (t, j))],
        out_specs=out_specs, out_shape=out_shape, scratch_shapes=[pltpu.VMEM((tk, tn), F32)],
        compiler_params=_cp("arbitrary", "arbitrary", "arbitrary"))(a, b)
    return res if colsum else res[0]


def mm_res_ln(a, w, res, g, b, *, tm, name):
    T, K = a.shape
    tm = min(tm, T)

    def body(a_ref, w_ref, r_ref, g_ref, b_ref, y_ref, yb_ref, xh_ref, rs_ref):
        pre = ALPHA * r_ref[...] + jnp.dot(a_ref[...].astype(BF), w_ref[...], preferred_element_type=F32)
        mu = jnp.mean(pre, axis=-1, keepdims=True)
        cen = pre - mu
        var = jnp.mean(cen * cen, axis=-1, keepdims=True)
        rstd = lax.rsqrt(var + LN_EPS)
        xhat = cen * rstd
        y = xhat * g_ref[...] + b_ref[...]
        y_ref[...] = y
        yb_ref[...] = y.astype(BF)
        xh_ref[...] = xhat
        rs_ref[...] = rstd

    row = lambda i: (i, 0)
    fix = lambda i: (0, 0)
    return pl.pallas_call(
        body, name=name, grid=(T // tm,),
        in_specs=[pl.BlockSpec((tm, K), row), pl.BlockSpec((K, D), fix), pl.BlockSpec((tm, D), row),
                  pl.BlockSpec((1, D), fix), pl.BlockSpec((1, D), fix)],
        out_specs=[pl.BlockSpec((tm, D), row), pl.BlockSpec((tm, D), row), pl.BlockSpec((tm, D), row),
                   pl.BlockSpec((tm, 1), row)],
        out_shape=[jax.ShapeDtypeStruct((T, D), F32), jax.ShapeDtypeStruct((T, D), BF),
                   jax.ShapeDtypeStruct((T, D), F32), jax.ShapeDtypeStruct((T, 1), F32)],
        compiler_params=_cp("arbitrary"))(a, w, res, g, b)


def ln_bwd(dy, xhat, rstd, g, *, tm, name, loss_from=None, dw_of=None):
    T = xhat.shape[0]
    tm = min(tm, T)
    nt = T // tm
    with_loss = loss_from is not None
    with_dw = dw_of is not None

    def body(*refs):
        if with_dw:
            acc_ref, refs = refs[-1], refs[:-1]
            n_main_in = 5 if with_loss else 4
            a_ref = refs[n_main_in]
            dw_ref = refs[-1]
            refs = refs[:n_main_in] + refs[n_main_in + 1:-1]
        if with_loss:
            xh_ref, rs_ref, g_ref, b_ref, t_ref, dp_ref, dpb_ref, dg_ref, db_ref, ls_ref = refs
        else:
            dy_ref, xh_ref, rs_ref, g_ref, dp_ref, dpb_ref, dg_ref, db_ref = refs
        i = pl.program_id(0)
        xhat_ = xh_ref[...]
        gg = g_ref[...]
        if with_loss:
            err = xhat_ * gg + b_ref[...] - t_ref[...]
            dyv = err * (1.0 / D)
            lpart = 0.5 * jnp.sum(jnp.sum(err * err, axis=-1, keepdims=True) * (1.0 / D))
        else:
            dyv = dy_ref[...]
        dxh = dyv * gg
        m1 = jnp.mean(dxh, axis=-1, keepdims=True)
        m2 = jnp.mean(dxh * xhat_, axis=-1, keepdims=True)
        dpre = rs_ref[...] * (dxh - m1 - xhat_ * m2)
        dp_ref[...] = dpre
        dpb = dpre.astype(BF)
        dpb_ref[...] = dpb
        dgp = jnp.sum(dyv * xhat_, axis=0, keepdims=True)
        dbp = jnp.sum(dyv, axis=0, keepdims=True)
        if with_dw:
            dwp = lax.dot_general(a_ref[...], dpb, (((0,), (0,)), ((), ())), preferred_element_type=F32)

        @pl.when(i == 0)
        def _():
            dg_ref[...] = dgp
            db_ref[...] = dbp
            if with_loss:
                ls_ref[...] = jnp.full((8, 128), lpart, F32)
            if with_dw:
                acc_ref[...] = dwp

        @pl.when(i > 0)
        def _():
            dg_ref[...] += dgp
            db_ref[...] += dbp
            if with_loss:
                ls_ref[...] += jnp.full((8, 128), lpart, F32)
            if with_dw:
                acc_ref[...] += dwp

        if with_dw:
            @pl.when(i == nt - 1)
            def _():
                dw_ref[...] = acc_ref[...].astype(BF)

    row = lambda i: (i, 0)
    fix = lambda i: (0, 0)
    if with_loss:
        in_specs = [pl.BlockSpec((tm, D), row), pl.BlockSpec((tm, 1), row), pl.BlockSpec((1, D), fix),
                    pl.BlockSpec((1, D), fix), pl.BlockSpec((tm, D), row)]
        args = [xhat, rstd, g, loss_from[0], loss_from[1]]
    else:
        in_specs = [pl.BlockSpec((tm, D), row), pl.BlockSpec((tm, D), row), pl.BlockSpec((tm, 1), row),
                    pl.BlockSpec((1, D), fix)]
        args = [dy, xhat, rstd, g]
    out_specs = [pl.BlockSpec((tm, D), row), pl.BlockSpec((tm, D), row), pl.BlockSpec((1, D), fix),
                 pl.BlockSpec((1, D), fix)]
    out_shape = [jax.ShapeDtypeStruct((T, D), F32), jax.ShapeDtypeStruct((T, D), BF),
                 jax.ShapeDtypeStruct((1, D), F32), jax.ShapeDtypeStruct((1, D), F32)]
    if with_loss:
        out_specs.append(pl.BlockSpec((8, 128), fix))
        out_shape.append(jax.ShapeDtypeStruct((8, 128), F32))
    scratch = []
    if with_dw:
        K = dw_of.shape[1]
        in_specs.append(pl.BlockSpec((tm, K), row))
        args.append(dw_of)
        out_specs.append(pl.BlockSpec((K, D), fix))
        out_shape.append(jax.ShapeDtypeStruct((K, D), BF))
        scratch.append(pltpu.VMEM((K, D), F32))
    return pl.pallas_call(body, name=name, grid=(nt,), in_specs=in_specs, out_specs=out_specs,
                          out_shape=out_shape, scratch_shapes=scratch, compiler_params=_cp("arbitrary"))(*args)


def _prev_halo(tm, blk):
    return lambda i: (jnp.maximum(i * (tm // HALO) - 1, 0), blk)


def _next_halo(tm, T, blk):
    return lambda i: (jnp.minimum((i + 1) * (tm // HALO), T // HALO - 1), blk)


def _pool_p(ext, t, g):
    e = ext[:, g * POOL_GD:(g + 1) * POOL_GD]
    s = e
    for sh in (1, 2, 4, 8)[:g + 1]:
        s = s + pltpu.roll(s, sh, axis=0)
    inv = 1.0 / jnp.minimum(t + 1, POOL_WINDOWS[g]).astype(F32)
    return s[HALO:] * inv - e[HALO:]


def pool_fwd(z, pw, *, tm, name):
    T = z.shape[0]
    tm = min(tm, T)

    def body(zm_ref, zh_ref, pw_ref, o_ref):
        i = pl.program_id(0)
        keep = jnp.where(i == 0, 0.0, 1.0).astype(F32)
        ext = jnp.concatenate([zh_ref[...].astype(F32) * keep, zm_ref[...].astype(F32)], axis=0)
        t = i * tm + lax.broadcasted_iota(jnp.int32, (tm, 1), 0)
        outs = [jnp.dot(_pool_p(ext, t, g).astype(BF), pw_ref[g], preferred_element_type=F32) for g in range(4)]
        o_ref[...] = jnp.concatenate(outs, axis=1).astype(o_ref.dtype)

    return pl.pallas_call(
        body, name=name, grid=(T // tm,),
        in_specs=[pl.BlockSpec((tm, D), lambda i: (i, ZB_POOL)), pl.BlockSpec((HALO, D), _prev_halo(tm, ZB_POOL)),
                  pl.BlockSpec((4, POOL_GD, POOL_GD), lambda i: (0, 0, 0))],
        out_specs=pl.BlockSpec((tm, D), lambda i: (i, 0)),
        out_shape=jax.ShapeDtypeStruct((T, D), BF), compiler_params=_cp("arbitrary"))(z, z, pw)


def pool_bwd(dz, dyp, yp_pre, z, pw, ps, *, tm, name):
    T = z.shape[0]
    tm = min(tm, T)
    nt = T // tm

    def body(dz_in, dy_ref, dyn_ref, yp_ref, zm_ref, zh_ref, pw_ref, ps_ref, dz_ref, dpw_ref, dps_ref):
        del dz_in
        i = pl.program_id(0)
        keep_p = jnp.where(i == 0, 0.0, 1.0).astype(F32)
        keep_n = jnp.where(i == nt - 1, 0.0, 1.0).astype(F32)
        ext = jnp.concatenate([zh_ref[...].astype(F32) * keep_p, zm_ref[...].astype(F32)], axis=0)
        t = i * tm + lax.broadcasted_iota(jnp.int32, (tm, 1), 0)
        psv = ps_ref[...]
        dy = dy_ref[...].astype(F32)
        dyp_ext = jnp.concatenate([dy, dyn_ref[...].astype(F32) * keep_n], axis=0) * psv
        t_ext = i * tm + lax.broadcasted_iota(jnp.int32, (tm + HALO, 1), 0)
        dps = jnp.sum(dy * yp_ref[...].astype(F32), axis=0, keepdims=True)
        dzs, dpws = [], []
        for g in range(4):
            sl = slice(g * POOL_GD, (g + 1) * POOL_GD)
            dyg = dyp_ext[:, sl].astype(BF)
            dp = lax.dot_general(dyg, pw_ref[g], (((1,), (1,)), ((), ())), preferred_element_type=F32)
            q = dp * (1.0 / jnp.minimum(t_ext + 1, POOL_WINDOWS[g]).astype(F32))
            s = q
            for sh in (1, 2, 4, 8)[:g + 1]:
                s = s + pltpu.roll(s, tm + HALO - sh, axis=0)
            dzs.append(s[:tm] - dp[:tm])
            p = _pool_p(ext, t, g).astype(BF)
            dpws.append(lax.dot_general(p, dyg[:tm], (((0,), (0,)), ((), ())), preferred_element_type=F32))
        dz_ref[...] = jnp.concatenate(dzs, axis=1).astype(dz_ref.dtype)

        @pl.when(i == 0)
        def _():
            for g in range(4):
                dpw_ref[g] = dpws[g]
            dps_ref[...] = dps

        @pl.when(i > 0)
        def _():
            for g in range(4):
                dpw_ref[g] += dpws[g]
            dps_ref[...] += dps

    row = lambda i: (i, 0)
    return pl.pallas_call(
        body, name=name, grid=(nt,),
        in_specs=[pl.BlockSpec(memory_space=pl.ANY),
                  pl.BlockSpec((tm, D), row), pl.BlockSpec((HALO, D), _next_halo(tm, T, 0)),
                  pl.BlockSpec((tm, D), row),
                  pl.BlockSpec((tm, D), lambda i: (i, ZB_POOL)), pl.BlockSpec((HALO, D), _prev_halo(tm, ZB_POOL)),
                  pl.BlockSpec((4, POOL_GD, POOL_GD), lambda i: (0, 0, 0)), pl.BlockSpec((1, D), lambda i: (0, 0))],
        out_specs=[pl.BlockSpec((tm, D), lambda i: (i, ZB_POOL)),
                   pl.BlockSpec((4, POOL_GD, POOL_GD), lambda i: (0, 0, 0)), pl.BlockSpec((1, D), lambda i: (0, 0))],
        out_shape=[jax.ShapeDtypeStruct(dz.shape, dz.dtype), jax.ShapeDtypeStruct((4, POOL_GD, POOL_GD), F32),
                   jax.ShapeDtypeStruct((1, D), F32)],
        input_output_aliases={0: 0}, compiler_params=_cp("arbitrary"))(dz, dyp, dyp, yp_pre, z, z, pw, ps)


def _fill_ext(ext_s, halo, main, keep):
    ext_s[0:HALO, :] = halo * keep
    ext_s[HALO:, :] = main


def _lru_gates(ext_s, tm, cw, cb, wr_ref, br, wi_ref, bi, lam):
    shifted = []
    v = cb
    for k in range(4):
        zs = ext_s[pl.ds(HALO - 3 + k, tm), :]
        shifted.append(zs)
        v = v + cw[k:k + 1, :] * zs
    vb = v.astype(BF)
    rp, ip = [], []
    for h in range(LRU_HEADS):
        sl = slice(h * LRU_HD, (h + 1) * LRU_HD)
        rp.append(jnp.dot(vb[:, sl], wr_ref[h], preferred_element_type=F32))
        ip.append(jnp.dot(vb[:, sl], wi_ref[h], preferred_element_type=F32))
    r = _sigmoid(jnp.concatenate(rp, axis=1) + br)
    ig = _sigmoid(jnp.concatenate(ip, axis=1) + bi)
    sp = jnp.maximum(-lam, 0.0) + jnp.log(1.0 + jnp.exp(-jnp.abs(lam)))
    a = jnp.exp(-LRU_C * r * sp)
    om = 1.0 - a * a
    rs = lax.rsqrt(om)
    return v, vb, r, ig, a, om, rs, sp, shifted


def lru_fwd(z, cw, cb, wr, br, wi, bi, lam, wlo, *, tm, name):
    T = z.shape[0]
    tm = min(tm, T)
    nch = tm // 8

    def body(zm_ref, zh_ref, cw_ref, cb_ref, wr_ref, br_ref, wi_ref, bi_ref, lam_ref, wlo_ref, h_ref, y_ref,
             a_s, b_s, carry, ext_s):
        i = pl.program_id(0)

        @pl.when(i == 0)
        def _():
            carry[...] = jnp.zeros_like(carry)

        keep = jnp.where(i == 0, 0.0, 1.0).astype(F32)
        _fill_ext(ext_s, zh_ref[...].astype(F32), zm_ref[...].astype(F32), keep)
        v, _, _, ig, a, om, rs, _, _ = _lru_gates(ext_s, tm, cw_ref[...], cb_ref[...], wr_ref, br_ref[...], wi_ref,
                                                  bi_ref[...], lam_ref[...])
        a_s[...] = a
        b_s[...] = jnp.where(om > 0.0, om * rs, 0.0) * (ig * v)
        row = lax.broadcasted_iota(jnp.int32, (8, D), 0)

        def step(ci, hprev):
            sl = pl.ds(pl.multiple_of(ci * 8, 8), 8)
            aa, bb = a_s[sl, :], b_s[sl, :]
            for s in (1, 2, 4):
                m = row >= s
                bb = bb + aa * jnp.where(m, pltpu.roll(bb, s, axis=0), 0.0)
                aa = aa * jnp.where(m, pltpu.roll(aa, s, axis=0), 1.0)
            h = bb + aa * hprev
            h_ref[sl, :] = h
            return jnp.broadcast_to(h[7:8, :], (8, D))

        carry[...] = lax.fori_loop(0, nch, step, carry[...])
        y_ref[...] = jnp.dot(h_ref[...].astype(BF), wlo_ref[...], preferred_element_type=F32).astype(BF)

    fix2 = lambda i: (0, 0)
    fix3 = lambda i: (0, 0, 0)
    return pl.pallas_call(
        body, name=name, grid=(T // tm,),
        in_specs=[pl.BlockSpec((tm, D), lambda i: (i, ZB_LRU)), pl.BlockSpec((HALO, D), _prev_halo(tm, ZB_LRU)),
                  pl.BlockSpec((4, D), fix2), pl.BlockSpec((1, D), fix2),
                  pl.BlockSpec((LRU_HEADS, LRU_HD, LRU_HD), fix3), pl.BlockSpec((1, D), fix2),
                  pl.BlockSpec((LRU_HEADS, LRU_HD, LRU_HD), fix3), pl.BlockSpec((1, D), fix2),
                  pl.BlockSpec((1, D), fix2), pl.BlockSpec((D, D), fix2)],
        out_specs=[pl.BlockSpec((tm, D), lambda i: (i, 0)), pl.BlockSpec((tm, D), lambda i: (i, 0))],
        out_shape=[jax.ShapeDtypeStruct((T, D), F32), jax.ShapeDtypeStruct((T, D), BF)],
        scratch_shapes=[pltpu.VMEM((tm, D), F32), pltpu.VMEM((tm, D), F32), pltpu.VMEM((8, D), F32),
                        pltpu.VMEM((tm + HALO, D), F32)],
        compiler_params=_cp("arbitrary"))(z, z, cw, cb, wr, br, wi, bi, lam, wlo)


def lru_bwd(dz, dyl, z, h, cw, cb, wr, br, wi, bi, lam, wlo, *, tm, name):
    T = z.shape[0]
    tm = min(tm, T)
    nt = T // tm
    nch = tm // 8

    def body(dz_in, dy_ref, zm_ref, zh_ref, h_ref, hh_ref, cw_ref, cb_ref, wr_ref, br_ref, wi_ref, bi_ref, lam_ref,
             wlo_ref, dz_ref, dcw_ref, dcb_ref, dwr_ref, dbr_ref, dwi_ref, dbi_ref, dlam_ref, dwlo_ref,
             c_s, g_s, dh_s, dh_carry, a_ext, dv_ext, ext_s, h_ext, wlo_acc):
        del dz_in
        i = pl.program_id(0)
        ti = nt - 1 - i

        @pl.when(i == 0)
        def _():
            dh_carry[...] = jnp.zeros_like(dh_carry)
            a_ext[tm:, :] = jnp.zeros((8, D), F32)
            dv_ext[tm:, :] = jnp.zeros((HALO, D), F32)

        keep = jnp.where(ti == 0, 0.0, 1.0).astype(F32)
        _fill_ext(ext_s, zh_ref[...].astype(F32), zm_ref[...].astype(F32), keep)
        cw_ = cw_ref[...]
        lam_ = lam_ref[...]
        v, vb, r, ig, a, om, rs, sp, shifted = _lru_gates(ext_s, tm, cw_, cb_ref[...], wr_ref, br_ref[...], wi_ref,
                                                          bi_ref[...], lam_)
        mult = jnp.where(om > 0.0, om * rs, 0.0)
        a_ext[0:tm, :] = a
        c_s[...] = a_ext[pl.ds(1, tm), :]
        g_s[...] = lax.dot_general(dy_ref[...], wlo_ref[...], (((1,), (1,)), ((), ())), preferred_element_type=F32)
        row = lax.broadcasted_iota(jnp.int32, (8, D), 0)

        def step(k, nxt):
            ci = nch - 1 - k
            sl = pl.ds(pl.multiple_of(ci * 8, 8), 8)
            cc, gg = c_s[sl, :], g_s[sl, :]
            for s in (1, 2, 4):
                m = row < 8 - s
                gg = gg + cc * jnp.where(m, pltpu.roll(gg, 8 - s, axis=0), 0.0)
                cc = cc * jnp.where(m, pltpu.roll(cc, 8 - s, axis=0), 1.0)
            dh = gg + cc * nxt
            dh_s[sl, :] = dh
            return jnp.broadcast_to(dh[0:1, :], (8, D))

        dh_carry[...] = lax.fori_loop(0, nch, step, dh_carry[...])
        a_ext[tm:, :] = a[0:8, :]
        dh = dh_s[...]
        h_ext[0:8, :] = hh_ref[...] * keep
        hv = h_ref[...]
        h_ext[8:, :] = hv
        hprev = h_ext[pl.ds(7, tm), :]
        dwlo = lax.dot_general(hv.astype(BF), dy_ref[...], (((0,), (0,)), ((), ())), preferred_element_type=F32)
        iv = ig * v
        da = dh * hprev
        dmult = dh * iv
        div = dh * mult
        dlog = da * a - dmult * (a * a) * rs
        dr = dlog * (-LRU_C * sp)
        dlam = jnp.sum(dlog * r, axis=0, keepdims=True) * (LRU_C * _sigmoid(-lam_))
        di = div * v
        dv = div * ig
        drp = dr * r * (1.0 - r)
        dip = di * ig * (1.0 - ig)
        drb, dib = drp.astype(BF), dip.astype(BF)
        dvh, dwr, dwi = [], [], []
        nt_dims = (((1,), (1,)), ((), ()))
        tn_dims = (((0,), (0,)), ((), ()))
        for hd in range(LRU_HEADS):
            sl = slice(hd * LRU_HD, (hd + 1) * LRU_HD)
            dvh.append(lax.dot_general(drb[:, sl], wr_ref[hd], nt_dims, preferred_element_type=F32)
                       + lax.dot_general(dib[:, sl], wi_ref[hd], nt_dims, preferred_element_type=F32))
            dwr.append(lax.dot_general(vb[:, sl], drb[:, sl], tn_dims, preferred_element_type=F32))
            dwi.append(lax.dot_general(vb[:, sl], dib[:, sl], tn_dims, preferred_element_type=F32))
        dv = dv + jnp.concatenate(dvh, axis=1)
        dv_ext[0:tm, :] = dv
        dzl = cw_[3:4, :] * dv
        for k in range(3):
            dzl = dzl + cw_[k:k + 1, :] * dv_ext[pl.ds(3 - k, tm), :]
        dz_ref[...] = dzl.astype(dz_ref.dtype)
        dv_ext[tm:, :] = dv[:HALO]
        dcw = jnp.concatenate([jnp.sum(dv * shifted[k], axis=0, keepdims=True) for k in range(4)], axis=0)
        dcb = jnp.sum(dv, axis=0, keepdims=True)
        dbr = jnp.sum(drp, axis=0, keepdims=True)
        dbi = jnp.sum(dip, axis=0, keepdims=True)

        @pl.when(i == 0)
        def _():
            dcw_ref[...] = dcw
            dcb_ref[...] = dcb
            dbr_ref[...] = dbr
            dbi_ref[...] = dbi
            dlam_ref[...] = dlam
            wlo_acc[...] = dwlo
            for hd in range(LRU_HEADS):
                dwr_ref[hd] = dwr[hd]
                dwi_ref[hd] = dwi[hd]

        @pl.when(i > 0)
        def _():
            dcw_ref[...] += dcw
            dcb_ref[...] += dcb
            dbr_ref[...] += dbr
            dbi_ref[...] += dbi
            dlam_ref[...] += dlam
            wlo_acc[...] += dwlo
            for hd in range(LRU_HEADS):
                dwr_ref[hd] += dwr[hd]
                dwi_ref[hd] += dwi[hd]

        @pl.when(i == nt - 1)
        def _():
            dwlo_ref[...] = wlo_acc[...].astype(BF)

    fix2 = lambda i: (0, 0)
    fix3 = lambda i: (0, 0, 0)
    rev = lambda i: (nt - 1 - i, 0)
    vec = pl.BlockSpec((1, D), fix2)
    hw = pl.BlockSpec((LRU_HEADS, LRU_HD, LRU_HD), fix3)
    return pl.pallas_call(
        body, name=name, grid=(nt,),
        in_specs=[pl.BlockSpec(memory_space=pl.ANY),
                  pl.BlockSpec((tm, D), rev),
                  pl.BlockSpec((tm, D), lambda i: (nt - 1 - i, ZB_LRU)),
                  pl.BlockSpec((HALO, D), lambda i: (jnp.maximum((nt - 1 - i) * (tm // HALO) - 1, 0), ZB_LRU)),
                  pl.BlockSpec((tm, D), rev),
                  pl.BlockSpec((8, D), lambda i: (jnp.maximum((nt - 1 - i) * (tm // 8) - 1, 0), 0)),
                  pl.BlockSpec((4, D), fix2), vec, hw, vec, hw, vec, vec, pl.BlockSpec((D, D), fix2)],
        out_specs=[pl.BlockSpec((tm, D), lambda i: (nt - 1 - i, ZB_LRU)),
                   pl.BlockSpec((4, D), fix2), vec, hw, vec, hw, vec, vec, pl.BlockSpec((D, D), fix2)],
        out_shape=[jax.ShapeDtypeStruct(dz.shape, dz.dtype), jax.ShapeDtypeStruct((4, D), F32),
                   jax.ShapeDtypeStruct((1, D), F32), jax.ShapeDtypeStruct((LRU_HEADS, LRU_HD, LRU_HD), F32),
                   jax.ShapeDtypeStruct((1, D), F32), jax.ShapeDtypeStruct((LRU_HEADS, LRU_HD, LRU_HD), F32),
                   jax.ShapeDtypeStruct((1, D), F32), jax.ShapeDtypeStruct((1, D), F32),
                   jax.ShapeDtypeStruct((D, D), BF)],
        scratch_shapes=[pltpu.VMEM((tm, D), F32), pltpu.VMEM((tm, D), F32), pltpu.VMEM((tm, D), F32),
                        pltpu.VMEM((8, D), F32), pltpu.VMEM((tm + 8, D), F32), pltpu.VMEM((tm + HALO, D), F32),
                        pltpu.VMEM((tm + HALO, D), F32), pltpu.VMEM((tm + 8, D), F32), pltpu.VMEM((D, D), F32)],
        input_output_aliases={0: 0},
        compiler_params=_cp("arbitrary"))(dz, dyl, z, z, h, h, cw, cb, wr, br, wi, bi, lam, wlo)


def _sconv_cv(u_ext, sw):
    shifted = []
    cv = None
    for k in range(3):
        us = (u_ext if k == 2 else pltpu.roll(u_ext, 2 - k, axis=0))[HALO:]
        shifted.append(us)
        term = sw[k:k + 1, :] * us
        cv = term if cv is None else cv + term
    return cv, shifted


def sconv_fwd(z, sw, wso, *, tm, name):
    T = z.shape[0]
    tm = min(tm, T)

    def body(zm_ref, zh_ref, sw_ref, wso_ref, y_ref):
        i = pl.program_id(0)
        keep = jnp.where(i == 0, 0.0, 1.0).astype(F32)
        zm = zm_ref[...].astype(F32)
        zh = zh_ref[...].astype(F32)
        u_ext = jnp.concatenate([zh[:, D:2 * D] * zh[:, 2 * D:] * keep, zm[:, D:2 * D] * zm[:, 2 * D:]], axis=0)
        cv, _ = _sconv_cv(u_ext, sw_ref[...])
        s = (zm[:, :D] * cv).astype(BF)
        y_ref[...] = jnp.dot(s, wso_ref[...], preferred_element_type=F32).astype(BF)

    return pl.pallas_call(
        body, name=name, grid=(T // tm,),
        in_specs=[pl.BlockSpec((tm, 3 * D), lambda i: (i, ZB_SCONV)),
                  pl.BlockSpec((HALO, 3 * D), _prev_halo(tm, ZB_SCONV)),
                  pl.BlockSpec((3, D), lambda i: (0, 0)), pl.BlockSpec((D, D), lambda i: (0, 0))],
        out_specs=pl.BlockSpec((tm, D), lambda i: (i, 0)),
        out_shape=jax.ShapeDtypeStruct((T, D), BF),
        compiler_params=_cp("arbitrary"))(z, z, sw, wso)


def sconv_bwd(dz, dyc, z, sw, wso, *, tm, name):
    T = z.shape[0]
    tm = min(tm, T)
    nt = T // tm

    def body(dz_in, dy_ref, dyn_ref, zm_ref, zp_ref, zn_ref, sw_ref, wso_ref, dz_ref, dsw_ref, dwso_ref, acc_ref):
        del dz_in
        i = pl.program_id(0)
        keep_p = jnp.where(i == 0, 0.0, 1.0).astype(F32)
        keep_n = jnp.where(i == nt - 1, 0.0, 1.0).astype(F32)
        sw_ = sw_ref[...]
        zm = zm_ref[...].astype(F32)
        zp = zp_ref[...].astype(F32)
        zb, zc, zh = zm[:, :D], zm[:, D:2 * D], zm[:, 2 * D:]
        u_ext = jnp.concatenate([zp[:, D:2 * D] * zp[:, 2 * D:] * keep_p, zc * zh], axis=0)
        cv, shifted = _sconv_cv(u_ext, sw_)
        dy_ext = jnp.concatenate([dy_ref[...], dyn_ref[...]], axis=0)
        ds_ext = lax.dot_general(dy_ext, wso_ref[...], (((1,), (1,)), ((), ())), preferred_element_type=F32)
        zb_ext = jnp.concatenate([zb, zn_ref[...][:, :D].astype(F32) * keep_n], axis=0)
        dcv_ext = ds_ext * zb_ext
        du = sw_[2:3, :] * dcv_ext[:tm]
        for k in range(2):
            du = du + sw_[k:k + 1, :] * pltpu.roll(dcv_ext, tm + HALO - (2 - k), axis=0)[:tm]
        dz_ref[...] = jnp.concatenate([ds_ext[:tm] * cv, du * zh, du * zc], axis=1).astype(dz_ref.dtype)
        dcv = dcv_ext[:tm]
        dsw = jnp.concatenate([jnp.sum(dcv * shifted[k], axis=0, keepdims=True) for k in range(3)], axis=0)
        dwso = lax.dot_general((zb * cv).astype(BF), dy_ref[...], (((0,), (0,)), ((), ())),
                               preferred_element_type=F32)

        @pl.when(i == 0)
        def _():
            dsw_ref[...] = dsw
            acc_ref[...] = dwso

        @pl.when(i > 0)
        def _():
            dsw_ref[...] += dsw
            acc_ref[...] += dwso

        @pl.when(i == nt - 1)
        def _():
            dwso_ref[...] = acc_ref[...].astype(BF)

    return pl.pallas_call(
        body, name=name, grid=(nt,),
        in_specs=[pl.BlockSpec(memory_space=pl.ANY),
                  pl.BlockSpec((tm, D), lambda i: (i, 0)), pl.BlockSpec((HALO, D), _next_halo(tm, T, 0)),
                  pl.BlockSpec((tm, 3 * D), lambda i: (i, ZB_SCONV)),
                  pl.BlockSpec((HALO, 3 * D), _prev_halo(tm, ZB_SCONV)),
                  pl.BlockSpec((HALO, 3 * D), _next_halo(tm, T, ZB_SCONV)),
                  pl.BlockSpec((3, D), lambda i: (0, 0)), pl.BlockSpec((D, D), lambda i: (0, 0))],
        out_specs=[pl.BlockSpec((tm, 3 * D), lambda i: (i, ZB_SCONV)), pl.BlockSpec((3, D), lambda i: (0, 0)),
                   pl.BlockSpec((D, D), lambda i: (0, 0))],
        out_shape=[jax.ShapeDtypeStruct(dz.shape, dz.dtype), jax.ShapeDtypeStruct((3, D), F32),
                   jax.ShapeDtypeStruct((D, D), BF)],
        scratch_shapes=[pltpu.VMEM((D, D), F32)],
        input_output_aliases={0: 0}, compiler_params=_cp("arbitrary"))(dz, dyc, dyc, z, z, z, sw, wso)


def merge_fwd(z, yp_pre, yl, yc, ps, *, tm, name):
    T = z.shape[0]
    tm = min(tm, T)

    def body(zg_ref, yp_ref, yl_ref, yc_ref, ps_ref, o_ref):
        gts = _sigmoid(zg_ref[...].astype(F32))
        m = (gts[:, :D] * (yp_ref[...].astype(F32) * ps_ref[...]) + gts[:, D:2 * D] * yl_ref[...].astype(F32)
             + gts[:, 2 * D:] * yc_ref[...].astype(F32))
        o_ref[...] = m.astype(o_ref.dtype)

    row = lambda i: (i, 0)
    return pl.pallas_call(
        body, name=name, grid=(T // tm,),
        in_specs=[pl.BlockSpec((tm, 3 * D), lambda i: (i, ZB_GATE)), pl.BlockSpec((tm, D), row),
                  pl.BlockSpec((tm, D), row), pl.BlockSpec((tm, D), row), pl.BlockSpec((1, D), lambda i: (0, 0))],
        out_specs=pl.BlockSpec((tm, D), row), out_shape=jax.ShapeDtypeStruct((T, D), BF),
        compiler_params=_cp("arbitrary"))(z, yp_pre, yl, yc, ps)


def merge_bwd(dm, z, yp_pre, yl, yc, ps, *, tm, name):
    T = z.shape[0]
    tm = min(tm, T)

    def body(dm_ref, zg_ref, yp_ref, yl_ref, yc_ref, ps_ref, dz_ref, dyp_ref, dyl_ref, dyc_ref):
        gts = _sigmoid(zg_ref[...].astype(F32))
        dmv = dm_ref[...].astype(F32)
        ys = (yp_ref[...].astype(F32) * ps_ref[...], yl_ref[...].astype(F32), yc_ref[...].astype(F32))
        outs = (dyp_ref, dyl_ref, dyc_ref)
        dgs = []
        for j in range(3):
            gj = gts[:, j * D:(j + 1) * D]
            outs[j][...] = (dmv * gj).astype(BF)
            dgs.append(dmv * ys[j] * gj * (1.0 - gj))
        dz_ref[...] = jnp.concatenate(dgs, axis=1).astype(dz_ref.dtype)

    row = lambda i: (i, 0)
    return pl.pallas_call(
        body, name=name, grid=(T // tm,),
        in_specs=[pl.BlockSpec((tm, D), row), pl.BlockSpec((tm, 3 * D), lambda i: (i, ZB_GATE)),
                  pl.BlockSpec((tm, D), row), pl.BlockSpec((tm, D), row), pl.BlockSpec((tm, D), row),
                  pl.BlockSpec((1, D), lambda i: (0, 0))],
        out_specs=[pl.BlockSpec((tm, 3 * D), lambda i: (i, ZB_GATE)), pl.BlockSpec((tm, D), row),
                   pl.BlockSpec((tm, D), row), pl.BlockSpec((tm, D), row)],
        out_shape=[jax.ShapeDtypeStruct((T, IN_COLS), BF), jax.ShapeDtypeStruct((T, D), BF),
                   jax.ShapeDtypeStruct((T, D), BF), jax.ShapeDtypeStruct((T, D), BF)],
        compiler_params=_cp("arbitrary"))(dm, z, yp_pre, yl, yc, ps)


def _attn_probs(qh, kh):
    s = lax.dot_general(qh, kh, (((1,), (1,)), ((), ())), preferred_element_type=F32) * (X_HD ** -0.5)
    e = jnp.exp(s - jnp.max(s, axis=-1, keepdims=True))
    return e / jnp.sum(e, axis=-1, keepdims=True)


def attn_fwd(xb, wq, kb, vb, *, tm, name):
    T = xb.shape[0]
    tm = min(tm, T)

    def body(x_ref, wq_ref, k_ref, v_ref, q_ref, o_ref):
        q = jnp.dot(x_ref[...], wq_ref[...], preferred_element_type=F32).astype(BF)
        q_ref[...] = q
        outs = []
        for h in range(X_HEADS):
            sl = slice(h * X_HD, (h + 1) * X_HD)
            p = _attn_probs(q[:, sl], k_ref[:, sl])
            outs.append(jnp.dot(p.astype(BF), v_ref[:, sl], preferred_element_type=F32))
        o_ref[...] = jnp.concatenate(outs, axis=1).astype(BF)

    row = lambda i: (i, 0)
    fix = lambda i: (0, 0)
    return pl.pallas_call(
        body, name=name, grid=(T // tm,),
        in_specs=[pl.BlockSpec((tm, D), row), pl.BlockSpec((D, D), fix), pl.BlockSpec((N_MEM, D), fix),
                  pl.BlockSpec((N_MEM, D), fix)],
        out_specs=[pl.BlockSpec((tm, D), row), pl.BlockSpec((tm, D), row)],
        out_shape=[jax.ShapeDtypeStruct((T, D), BF), jax.ShapeDtypeStruct((T, D), BF)],
        compiler_params=_cp("arbitrary"))(xb, wq, kb, vb)


def attn_bwd(dxa, wo, q, kb, vb, xb, *, tm, name):
    T = q.shape[0]
    tm = min(tm, T)
    nt = T // tm

    def body(d_ref, wo_ref, q_ref, k_ref, v_ref, x_ref, dq_ref, dk_ref, dv_ref, dwq_ref, acc_ref):
        i = pl.program_id(0)
        do = lax.dot_general(d_ref[...], wo_ref[...], (((1,), (1,)), ((), ())),
                             preferred_element_type=F32).astype(BF)
        q = q_ref[...]
        dqs, dks, dvs = [], [], []
        for h in range(X_HEADS):
            sl = slice(h * X_HD, (h + 1) * X_HD)
            kh, vh = k_ref[:, sl], v_ref[:, sl]
            p = _attn_probs(q[:, sl], kh)
            dp = lax.dot_general(do[:, sl], vh, (((1,), (1,)), ((), ())), preferred_element_type=F32)
            ds = (p * (dp - jnp.sum(dp * p, axis=-1, keepdims=True)) * (X_HD ** -0.5)).astype(BF)
            dqs.append(jnp.dot(ds, kh, preferred_element_type=F32))
            dks.append(lax.dot_general(ds, q[:, sl], (((0,), (0,)), ((), ())), preferred_element_type=F32))
            dvs.append(lax.dot_general(p.astype(BF), do[:, sl], (((0,), (0,)), ((), ())),
                                       preferred_element_type=F32))
        dqb = jnp.concatenate(dqs, axis=1).astype(BF)
        dq_ref[...] = dqb
        dk = jnp.concatenate(dks, axis=1)
        dv = jnp.concatenate(dvs, axis=1)
        dwq = lax.dot_general(x_ref[...], dqb, (((0,), (0,)), ((), ())), preferred_element_type=F32)

        @pl.when(i == 0)
        def _():
            dk_ref[...] = dk
            dv_ref[...] = dv
            acc_ref[...] = dwq

        @pl.when(i > 0)
        def _():
            dk_ref[...] += dk
            dv_ref[...] += dv
            acc_ref[...] += dwq

        @pl.when(i == nt - 1)
        def _():
            dwq_ref[...] = acc_ref[...].astype(BF)

    row = lambda i: (i, 0)
    fix = lambda i: (0, 0)
    return pl.pallas_call(
        body, name=name, grid=(nt,),
        in_specs=[pl.BlockSpec((tm, D), row), pl.BlockSpec((D, D), fix), pl.BlockSpec((tm, D), row),
                  pl.BlockSpec((N_MEM, D), fix), pl.BlockSpec((N_MEM, D), fix), pl.BlockSpec((tm, D), row)],
        out_specs=[pl.BlockSpec((tm, D), row), pl.BlockSpec((N_MEM, D), fix), pl.BlockSpec((N_MEM, D), fix),
                   pl.BlockSpec((D, D), fix)],
        out_shape=[jax.ShapeDtypeStruct((T, D), BF), jax.ShapeDtypeStruct((N_MEM, D), F32),
                   jax.ShapeDtypeStruct((N_MEM, D), F32), jax.ShapeDtypeStruct((D, D), BF)],
        scratch_shapes=[pltpu.VMEM((D, D), F32)],
        compiler_params=_cp("arbitrary"))(dxa, wo, q, kb, vb, xb)


TM_MM = 1024
TM_EW = 512
TM_SEQ = 512
TT_DW = 2048


def _mem_kv(l, memb, W):
    kb = mm_nn(memb, W['xa_w_k'][l], None, out_dtype=BF, tm=N_MEM, tn=1024, name=f"l{l}_mem_k")
    vb = mm_nn(memb, W['xa_w_v'][l], None, out_dtype=BF, tm=N_MEM, tn=1024, name=f"l{l}_mem_v")
    return kb, vb


def _layer_fwd(l, x, xb, kb, vb, W, host=None, host2=None, after_in_proj=None):
    n = f"l{l}_"
    sv = {'x0': x if xb is None else xb}
    z = mm_nn(sv['x0'], W['w_in'][l], W['b_in'][l], out_dtype=BF, tm=2 * TM_MM, tn=1024, name=n + "in_proj",
              host=host)
    if host is not None:
        z, sv['hosted'] = z
        if after_in_proj is not None:
            after_in_proj(sv['hosted'])
    if kb is None:
        kb, vb = _mem_kv(l, vb, W)
    sv['kb'], sv['vb'] = kb, vb
    yp = pool_fwd(z, W['pool_w'][l], tm=TM_SEQ, name=n + "pool_fwd")
    h, yl = lru_fwd(z, W['lru_conv_w'][l], W['lru_conv_b'][l], W['lru_w_r'][l], W['lru_b_r'][l], W['lru_w_i'][l],
                    W['lru_b_i'][l], W['lru_lambda'][l], W['lru_w_out'][l], tm=TM_SEQ, name=n + "lru_fwd")
    yc = sconv_fwd(z, W['sconv_w'][l], W['sconv_w_out'][l], tm=TM_SEQ, name=n + "sconv_fwd")
    merged = merge_fwd(z, yp, yl, yc, W['pool_scale'][l], tm=TM_EW, name=n + "merge_fwd")
    x1, x1b, xh1, rs1 = mm_res_ln(merged, W['w_mix_out'][l], x, W['ln_g'][l][0:1], W['ln_b'][l][0:1], tm=TM_MM,
                                  name=n + "mix_out_ln")
    q, o = attn_fwd(x1b, W['xa_w_q'][l], kb, vb, tm=TM_EW, name=n + "attn_fwd")
    x2, x2b, xh2, rs2 = mm_res_ln(o, W['xa_w_o'][l], x1, W['ln_g'][l][1:2], W['ln_b'][l][1:2], tm=TM_MM,
                                  name=n + "attn_out_ln")
    res = ffn_in_swiglu(x2b, W['ffn_w_gu'][l], tm=TM_MM, name=n + "ffn_in", host=host2)
    gu, hdn = res[:2]
    if host2 is not None:
        sv['hosted2'] = res[2]
    x3, x3b, xh3, rs3 = mm_res_ln(hdn, W['ffn_w_down'][l], x2, W['ln_g'][l][2:3], W['ln_b'][l][2:3], tm=TM_EW,
                                  name=n + "ffn_out_ln")
    sv.update(z=z, yp=yp, h=h, yl=yl, yc=yc, merged=merged, x1b=x1b, xh1=xh1, rs1=rs1, q=q, o=o, x2b=x2b,
              xh2=xh2, rs2=rs2, gu=gu, hdn=hdn, xh3=xh3, rs3=rs3)
    return x3, x3b, sv


def _layer_bwd(l, dx3, sv, memb, kb, vb, W, loss_from=None, chain=None, last_host_fn=None):
    n = f"l{l}_"
    G = {}
    res = ln_bwd(dx3, sv['xh3'], sv['rs3'], W['ln_g'][l][2:3], tm=TM_EW, name=n + "ln3_bwd", loss_from=loss_from,
                 dw_of=sv['hdn'])
    dp3, dp3b, dg3, db3 = res[:4]
    loss = res[4] if loss_from is not None else None
    G['ffn_w_down'] = res[-1]
    host_b = host_c = None
    dgu = ffn_down_dx_swiglu(dp3b, W['ffn_w_down'][l], sv['gu'], tm=TM_EW, name=n + "ffn_down_dx",
                             host=None if chain is None else chain[0])
    if chain is not None:
        dgu, got = dgu
        host_b = chain[1](got)
    dx2 = mm_nt(dgu, W['ffn_w_gu'][l], dp3, out_dtype=F32, tm=TM_MM, tc=1408, name=n + "ffn_in_dx", host=host_b)
    if chain is not None:
        dx2, got = dx2
        host_c = chain[2](got)
    G['ffn_w_gu'] = mm_tn(sv['x2b'], dgu, out_dtype=BF, tk=1024, tn=1408, tt=TT_DW,name=n + "ffn_in_dw")

    dp2, dp2b, dg2, db2, G['xa_w_o'] = ln_bwd(dx2, sv['xh2'], sv['rs2'], W['ln_g'][l][1:2], tm=TM_MM,
                                              name=n + "ln2_bwd", dw_of=sv['o'])
    dq, dk, dv, G['xa_w_q'] = attn_bwd(dp2b, W['xa_w_o'][l], sv['q'], kb, vb, sv['x1b'], tm=TM_EW,
                                       name=n + "attn_bwd")
    dx1 = mm_nt(dq, W['xa_w_q'][l], dp2, out_dtype=F32, tm=TM_MM, tc=D, name=n + "attn_q_dx", host=host_c)
    if chain is not None:
        dx1, got = dx1
        chain[3](got)
    G['xa_w_k'] = mm_tn(memb, dk, out_dtype=BF, tk=1024, tn=1024, tt=N_MEM, name=n + "attn_k_dw")
    G['xa_w_v'] = mm_tn(memb, dv, out_dtype=BF, tk=1024, tn=1024, tt=N_MEM, name=n + "attn_v_dw")

    dp1, dp1b, dg1, db1, G['w_mix_out'] = ln_bwd(dx1, sv['xh1'], sv['rs1'], W['ln_g'][l][0:1], tm=TM_MM,
                                                 name=n + "ln1_bwd", dw_of=sv['merged'])
    dmerged = mm_nt(dp1b, W['w_mix_out'][l], None, out_dtype=BF, tm=TM_MM, tc=D, name=n + "mix_out_dx")
    z = sv['z']
    dz, dyp, dyl, dyc = merge_bwd(dmerged, z, sv['yp'], sv['yl'], sv['yc'], W['pool_scale'][l], tm=TM_EW,
                                  name=n + "merge_bwd")
    dz, G['pool_w'], G['pool_scale'] = pool_bwd(dz, dyp, sv['yp'], z, W['pool_w'][l], W['pool_scale'][l],
                                                tm=TM_SEQ, name=n + "pool_bwd")
    (dz, G['lru_conv_w'], G['lru_conv_b'], G['lru_w_r'], G['lru_b_r'], G['lru_w_i'], G['lru_b_i'], G['lru_lambda'],
     G['lru_w_out']) = lru_bwd(dz, dyl, z, sv['h'], W['lru_conv_w'][l], W['lru_conv_b'][l], W['lru_w_r'][l],
                               W['lru_b_r'][l], W['lru_w_i'][l], W['lru_b_i'][l], W['lru_lambda'][l],
                               W['lru_w_out'][l], tm=TM_SEQ, name=n + "lru_bwd")
    dz, G['sconv_w'], G['sconv_w_out'] = sconv_bwd(dz, dyc, z, W['sconv_w'][l], W['sconv_w_out'][l], tm=TM_SEQ,
                                                   name=n + "sconv_bwd")
    G['w_in'], G['b_in'] = mm_tn(sv['x0'], dz, out_dtype=BF, tk=1024, tn=1024, tt=TT_DW, name=n + "in_proj_dw",
                                 colsum=True)
    G['ln_g'] = jnp.concatenate([dg1, dg2, dg3], axis=0)
    G['ln_b'] = jnp.concatenate([db1, db2, db3], axis=0)
    last_host = None if last_host_fn is None else last_host_fn(G)
    dx0 = mm_nt(dz, W['w_in'][l], dp1, out_dtype=F32, tm=TM_MM, tc=2048, name=n + "in_proj_dx", host=last_host)
    if last_host is not None:
        dx0, G['hosted_last'] = dx0
    return dx0, G, loss


def local_step(x, mem, target, W):
    memb = mem.astype(BF)
    saves, kvs = [], []
    xf, xb = x, None
    for l in range(DEPTH):
        kb = mm_nn(memb, W['xa_w_k'][l], None, out_dtype=BF, tm=N_MEM, tn=1024, name=f"l{l}_mem_k")
        vb = mm_nn(memb, W['xa_w_v'][l], None, out_dtype=BF, tm=N_MEM, tn=1024, name=f"l{l}_mem_v")
        xf, xb, sv = _layer_fwd(l, xf, xb, kb, vb, W)
        saves.append(sv)
        kvs.append((kb, vb))
    grads = [None] * DEPTH
    dx, loss = None, None
    for l in reversed(range(DEPTH)):
        lf = (W['ln_b'][l][2:3], target) if l == DEPTH - 1 else None
        dx, grads[l], ls = _layer_bwd(l, dx, saves[l], memb, kvs[l][0], kvs[l][1], W, loss_from=lf)
        if ls is not None:
            loss = ls
    return loss, dx, grads


def _coords():
    return lax.axis_index("x"), lax.axis_index("y"), lax.axis_index("c")


FLIPS = ((1, 0), (0, 1), (1, 1))
SQUARES = ('lru_w_out', 'sconv_w_out', 'w_mix_out', 'xa_w_q', 'xa_w_k', 'xa_w_v', 'xa_w_o')
LAYER_SHAPE = {'w_in': (D, IN_COLS), 'pool_w': (4, POOL_GD, POOL_GD), 'ffn_w_gate': (4, D, D_FF // 4),
               'ffn_w_up': (4, D, D_FF // 4), 'ffn_w_down': (D_FF, D), **{n: (D, D) for n in SQUARES}}
PIECES = ('w_in', 'pool_w') + SQUARES + ('ffn_w_gate', 'ffn_w_up', 'ffn_w_down')


def _mult(v, m):
    return v if isinstance(v, int) else pl.multiple_of(v, m)


def _win(name, ref, k):
    if name == 'w_in':
        return ref.at[:, pl.ds(_mult(((2 * k + 6) % 8) * D, D), 2 * D)]
    if name == 'pool_w':
        return ref.at[:, pl.ds(_mult(k * (POOL_GD // 4), POOL_GD // 4), POOL_GD // 4), :]
    if name in ('ffn_w_gate', 'ffn_w_up'):
        return ref.at[k]
    rows = LAYER_SHAPE[name][0] // 4
    return ref.at[pl.ds(_mult(k * rows, 16), rows), :]


def _half_shape(name):
    shard = _shard_shape(name)
    return (shard[0] // 2,) + shard[1:]


def _half(name, ref, h):
    rows = _shard_shape(name)[0] // 2
    if name == 'pool_w':
        return ref.at[pl.ds(h * rows, rows)]
    return ref.at[pl.ds(_mult(h * rows, 16), rows), :]


def _shard_shape(n):
    shp = LAYER_SHAPE[n]
    if n == 'w_in':
        return (shp[0], shp[1] // 4)
    if n == 'pool_w':
        return (shp[0], shp[1] // 4, shp[2])
    if n in ('ffn_w_gate', 'ffn_w_up'):
        return shp[1:]
    return (shp[0] // 4, shp[1])


def sum_chips_small(q, recv3, chip):
    r, C = q.shape
    slot_of_xor = {2: 0, 1: 1, 3: 2}

    def body(me_ref, q_ref, r_ref, o_ref):
        me = me_ref[0]
        acc = None
        for k in range(4):
            kx = k ^ me
            term = q_ref[...]
            for xv, j in slot_of_xor.items():
                term = jnp.where(kx == xv, r_ref[j], term)
            acc = term if acc is None else acc + term
        o_ref[...] = acc

    return pl.pallas_call(
        body, name="sum_chips_small",
        grid_spec=pltpu.PrefetchScalarGridSpec(
            num_scalar_prefetch=1, grid=(1,),
            in_specs=[pl.BlockSpec((r, C), lambda i, me: (0, 0)), pl.BlockSpec((3, r, C), lambda i, me: (0, 0, 0))],
            out_specs=pl.BlockSpec((r, C), lambda i, me: (0, 0))),
        out_shape=jax.ShapeDtypeStruct((r, C), F32), compiler_params=_cp("arbitrary"))(chip, q, recv3)


def _row_tile(R, C):
    for cand in (1024, 512, 256, 128, 64, 32, 16):
        if R % cand == 0 and cand * C * 4 <= 2 * 1024 * 1024:
            return cand
    return R


def _adamw_math(w, g, m, v):
    mn = ADAM_B1 * m + (1.0 - ADAM_B1) * g
    vn = ADAM_B2 * v + (1.0 - ADAM_B2) * (g * g)
    m_hat = mn / (1.0 - ADAM_B1 ** ADAM_STEP)
    v_hat = vn / (1.0 - ADAM_B2 ** ADAM_STEP)
    return -ADAM_LR * (m_hat / (jnp.sqrt(v_hat) + ADAM_EPS) + ADAM_WD * w), mn, vn


def adamw(w, g, m, v, *, name):
    shp = w.shape
    args = [t.reshape(-1, shp[-1]) for t in (w, g, m, v)]
    R, C = args[0].shape
    tr = _row_tile(R, C)

    def body(w_ref, g_ref, m_ref, v_ref, d_ref, mo_ref, vo_ref):
        d_ref[...], mo_ref[...], vo_ref[...] = _adamw_math(w_ref[...], g_ref[...], m_ref[...], v_ref[...])

    spec = pl.BlockSpec((tr, C), lambda i: (i, 0))
    res = pl.pallas_call(
        body, name=name, grid=(R // tr,), in_specs=[spec] * 4, out_specs=[spec] * 3,
        out_shape=[jax.ShapeDtypeStruct((R, C), F32)] * 3, compiler_params=_cp("arbitrary"))(*args)
    return [r.reshape(shp) for r in res]


def _remote(src, dst, send_sems, recv_sems, k, peer):
    return pltpu.make_async_remote_copy(src_ref=src, dst_ref=dst, send_sem=send_sems.at[k], recv_sem=recv_sems.at[k],
                                        device_id=peer, device_id_type=MESH)


def gather_layer(l, shards, small=None, pieces=PIECES):
    n_p = len(pieces)
    with_small = small is not None

    def descs(h_in, h_out, sems):
        srcs = dict(zip(pieces, h_in[:n_p]))
        outs = dict(zip(pieces, h_out[:n_p]))
        ici_s, ici_r, d2d_s, d2d_r, own_s, own_r = sems
        x, y, c = _coords()
        me = 2 * x + y
        sib = (x, y, 1 - c)
        ici, fwd, fwd_in, own = [], [], [], []
        for j, (fx, fy) in enumerate(FLIPS):
            peer = (x ^ fx, y ^ fy, c)
            other = 2 * (x ^ fx) + (y ^ fy)
            for p, n in enumerate(pieces):
                k = 3 * p + j
                mine = _half(n, _win(n, outs[n], me), c)
                landed = _half(n, _win(n, outs[n], other), c)
                sib_half = _half(n, _win(n, outs[n], other), 1 - c)
                ici.append((_remote(_half(n, srcs[n].at[l], c), mine, ici_s, ici_r, k, peer),
                            _remote(_half(n, srcs[n].at[l], c), landed, ici_s, ici_r, k, peer)))
                fwd.append(_remote(landed, landed, d2d_s, d2d_r, k, sib))
                fwd_in.append(_remote(sib_half, sib_half, d2d_s, d2d_r, k, sib))
            if with_small:
                k = 3 * n_p + j
                ici.append((_remote(h_in[n_p], h_out[n_p].at[me], ici_s, ici_r, k, peer),
                            _remote(h_in[n_p], h_out[n_p].at[other], ici_s, ici_r, k, peer)))
        for p, n in enumerate(pieces):
            own.append(_remote(srcs[n].at[l], _win(n, outs[n], me), own_s, own_r, p, sib))
        if with_small:
            own.append(_remote(h_in[n_p], h_out[n_p].at[me], own_s, own_r, n_p, sib))
        return ici, fwd, fwd_in, own

    def start(h_in, h_out, sems):
        ici, _, _, own = descs(h_in, h_out, sems)
        for send, _ in ici:
            send.start()
        for cp in own:
            cp.start()

    def finish(h_in, h_out, sems):
        ici, fwd, fwd_in, own = descs(h_in, h_out, sems)
        per_chip = n_p + (1 if with_small else 0)
        for j in range(3):
            for p in range(n_p):
                ici[j * per_chip + p][1].wait_recv()
                fwd[j * n_p + p].start()
            if with_small:
                ici[j * per_chip + n_p][1].wait_recv()
        for cp in fwd_in:
            cp.wait_recv()
        for send, _ in ici:
            send.wait_send()
        for cp in fwd:
            cp.wait_send()
        for cp in own:
            cp.wait()

    arrays = [shards[n] for n in pieces] + ([small] if with_small else [])
    out_shape = [jax.ShapeDtypeStruct(LAYER_SHAPE[n], BF) for n in pieces]
    if with_small:
        out_shape.append(jax.ShapeDtypeStruct((4,) + small.shape, small.dtype))
    sems = [pltpu.SemaphoreType.DMA((3 * n_p + 3,)), pltpu.SemaphoreType.DMA((3 * n_p + 3,)),
            pltpu.SemaphoreType.DMA((3 * n_p,)), pltpu.SemaphoreType.DMA((3 * n_p,)),
            pltpu.SemaphoreType.DMA((n_p + 1,)), pltpu.SemaphoreType.DMA((n_p + 1,))]
    return Hosted(arrays, out_shape, sems, start, finish)


def both_hosted(h1, h2):
    a1, o1, s1 = len(h1.arrays), len(h1.out_shape), len(h1.sems)

    def start(h_in, h_out, sems):
        h1.start(h_in[:a1], h_out[:o1], sems[:s1])
        h2.start(h_in[a1:], h_out[o1:], sems[s1:])

    def finish(h_in, h_out, sems):
        h1.finish(h_in[:a1], h_out[:o1], sems[:s1])
        h2.finish(h_in[a1:], h_out[o1:], sems[s1:])

    return Hosted(list(h1.arrays) + list(h2.arrays), list(h1.out_shape) + list(h2.out_shape),
                  list(h1.sems) + list(h2.sems), start, finish)


def run_hosted(host, name):
    n_in, n_out = len(host.arrays), len(host.out_shape)

    def body(*refs):
        h_in, h_out, sems = refs[:n_in], refs[n_in:n_in + n_out], refs[n_in + n_out:]
        host.start(h_in, h_out, sems)
        host.finish(h_in, h_out, sems)

    return pl.pallas_call(body, name=name, in_specs=[HBM_SPEC] * n_in, out_specs=[HBM_SPEC] * n_out,
                          out_shape=list(host.out_shape), scratch_shapes=list(host.sems))(*host.arrays)


def split_halves_hosted(parts, small):
    n_p = len(PIECES)
    rh = small.shape[0] // 2

    def descs(h_in, h_out, sems):
        send_sems, recv_sems = sems
        x, y, c = _coords()
        sib = (x, y, 1 - c)
        cps = []
        for p, n in enumerate(PIECES):
            for k in range(4):
                cps.append(_remote(_half(n, _win(n, h_in[p], k), 1 - c), h_out[p].at[k], send_sems, recv_sems,
                                   4 * p + k, sib))
        cps.append(_remote(h_in[n_p].at[pl.ds(_mult((1 - c) * rh, 8), rh), :], h_out[n_p], send_sems, recv_sems,
                           4 * n_p, sib))
        return cps

    out_shape = [jax.ShapeDtypeStruct((4,) + _half_shape(n), BF) for n in PIECES]
    out_shape.append(jax.ShapeDtypeStruct((rh, small.shape[1]), small.dtype))
    sems = [pltpu.SemaphoreType.DMA((4 * n_p + 1,)), pltpu.SemaphoreType.DMA((4 * n_p + 1,))]
    return _all_at_once([parts[n] for n in PIECES] + [small], out_shape, sems, descs)


def _all_at_once(arrays, out_shape, sems, descs):
    def start(h_in, h_out, s):
        for cp in descs(h_in, h_out, s):
            cp.start()

    def finish(h_in, h_out, s):
        for cp in descs(h_in, h_out, s):
            cp.wait()

    return Hosted(arrays, out_shape, sems, start, finish)


def swap_cores_hosted(s):
    n = len(s)

    def descs(h_in, h_out, sems):
        x, y, c = _coords()
        return [_remote(h_in[i], h_out[i], sems[0], sems[1], i, (x, y, 1 - c)) for i in range(n)]

    return _all_at_once(list(s), [jax.ShapeDtypeStruct(a.shape, a.dtype) for a in s],
                        [pltpu.SemaphoreType.DMA((n,)), pltpu.SemaphoreType.DMA((n,))], descs)


SAME_SHAPE = (('w_in',), ('pool_w',), SQUARES, ('ffn_w_gate', 'ffn_w_up'), ('ffn_w_down',))


def add_halves(names, parts, theirs, core):
    name = names[0]
    half = _half_shape(name)
    zero = (0,) * len(half)
    if name == 'w_in':
        pspec = pl.BlockSpec(half, lambda k, cc: (cc[0], (k + 3) % 4))
    elif name == 'pool_w':
        pspec = pl.BlockSpec(half, lambda k, cc: (cc[0], k, 0))
    elif name in ('ffn_w_gate', 'ffn_w_up'):
        pspec = pl.BlockSpec((None,) + half, lambda k, cc: (k, cc[0], 0))
    else:
        pspec = pl.BlockSpec(half, lambda k, cc: (2 * k + cc[0], 0))
    slot = pl.BlockSpec((None,) + half, lambda k, cc: (k,) + zero)
    m = len(names)

    def body(cc_ref, *refs):
        del cc_ref
        for i in range(m):
            refs[2 * m + i][...] = (refs[i][...].astype(F32) + refs[m + i][...].astype(F32)).astype(BF)

    res = pl.pallas_call(
        body, name="add_cores_" + name,
        grid_spec=pltpu.PrefetchScalarGridSpec(num_scalar_prefetch=1, grid=(4,), in_specs=[pspec] * m + [slot] * m,
                                               out_specs=[slot] * m),
        out_shape=[jax.ShapeDtypeStruct((4,) + half, BF)] * m, compiler_params=_cp("arbitrary"))(
            core, *[parts[n] for n in names], *[theirs[n] for n in names])
    return dict(zip(names, res))


def add_halves_small(small, theirs, core):
    rh, C = theirs.shape

    def body(cc_ref, p_ref, t_ref, o_ref):
        del cc_ref
        o_ref[...] = p_ref[...] + t_ref[...]

    blk = pl.BlockSpec((rh, C), lambda i, cc: (0, 0))
    return pl.pallas_call(
        body, name="add_cores_small",
        grid_spec=pltpu.PrefetchScalarGridSpec(
            num_scalar_prefetch=1, grid=(1,),
            in_specs=[pl.BlockSpec((rh, C), lambda i, cc: (cc[0], 0)), blk], out_specs=blk),
        out_shape=jax.ShapeDtypeStruct((rh, C), F32), compiler_params=_cp("arbitrary"))(core, small, theirs)


def exchange_halves(q, qsmall):
    n_p = len(PIECES)

    def descs(h_in, h_out, sems):
        send_sems, recv_sems = sems
        x, y, c = _coords()
        cps = []
        for j, (fx, fy) in enumerate(FLIPS):
            peer = (x ^ fx, y ^ fy, c)
            other = 2 * (x ^ fx) + (y ^ fy)
            for p in range(n_p):
                cps.append(_remote(h_in[p].at[other], h_out[p].at[j], send_sems, recv_sems, 3 * p + j, peer))
            cps.append(_remote(h_in[n_p], h_out[n_p].at[j], send_sems, recv_sems, 3 * n_p + j, peer))
        return cps

    def start(h_in, h_out, sems):
        for cp in descs(h_in, h_out, sems):
            cp.start()

    def finish(h_in, h_out, sems):
        for cp in descs(h_in, h_out, sems):
            cp.wait()

    arrays = [q[n] for n in PIECES] + [qsmall]
    out_shape = [jax.ShapeDtypeStruct((3,) + _half_shape(n), BF) for n in PIECES]
    out_shape.append(jax.ShapeDtypeStruct((3,) + qsmall.shape, qsmall.dtype))
    sems = [pltpu.SemaphoreType.DMA((3 * n_p + 3,)), pltpu.SemaphoreType.DMA((3 * n_p + 3,))]
    return Hosted(arrays, out_shape, sems, start, finish)


def sum_halves(names, q, recv3, chip):
    half = _half_shape(names[0])
    zero = (0,) * len(half)
    m = len(names)

    def body(me_ref, *refs):
        del me_ref
        for i in range(m):
            acc = refs[i][...].astype(F32)
            for j in range(3):
                acc = acc + refs[m + i][j].astype(F32)
            refs[2 * m + i][...] = acc

    res = pl.pallas_call(
        body, name="sum_chips_" + names[0],
        grid_spec=pltpu.PrefetchScalarGridSpec(
            num_scalar_prefetch=1, grid=(1,),
            in_specs=[pl.BlockSpec((None,) + half, lambda i, me: (me[0],) + zero)] * m
            + [pl.BlockSpec((3,) + half, lambda i, me: (0,) + zero)] * m,
            out_specs=[pl.BlockSpec(half, lambda i, me: zero)] * m),
        out_shape=[jax.ShapeDtypeStruct(half, F32)] * m, compiler_params=_cp("arbitrary"))(
            chip, *[q[n] for n in names], *[recv3[n] for n in names])
    return dict(zip(names, res))


def adamw_halves(w, g0, m, v, *, name):
    shp = w.shape
    C = shp[-1]
    four = (DEPTH, 2, -1, C)
    w4, m4, v4 = [t.reshape(four) for t in (w, m, v)]
    gs = [t.reshape(-1, C) for pair in g0 for t in pair]
    Rh = gs[0].shape[0]
    tr = _row_tile(Rh, C)

    def body(w_ref, a0_ref, b0_ref, a1_ref, b1_ref, m_ref, v_ref, g_ref, d_ref, mo_ref, vo_ref):
        mine = pl.program_id(1) == lax.axis_index("c")
        g_l0 = jnp.where(mine, a0_ref[...], b0_ref[...])
        g_l1 = jnp.where(mine, a1_ref[...], b1_ref[...])
        g = jnp.where(pl.program_id(0) == 0, g_l0, g_l1)
        g_ref[...] = g
        d_ref[...], mo_ref[...], vo_ref[...] = _adamw_math(w_ref[...], g, m_ref[...], v_ref[...])

    lay = pl.BlockSpec((None, None, tr, C), lambda l, h, i: (l, h, i, 0))
    one = pl.BlockSpec((tr, C), lambda l, h, i: (i, 0))
    res = pl.pallas_call(
        body, name=name, grid=(DEPTH, 2, Rh // tr), in_specs=[lay, one, one, one, one, lay, lay],
        out_specs=[lay] * 4, out_shape=[jax.ShapeDtypeStruct(w4.shape, F32)] * 4,
        compiler_params=_cp("arbitrary", "arbitrary", "arbitrary"))(w4, *gs, m4, v4)
    return [r.reshape(shp) for r in res]


def _local_shape(name, full_shape):
    shp = list(full_shape)
    ax = BIG_SHARDED.get(name, SMALL_SHARDED.get(name))
    if ax is not None:
        shp[ax] //= 4
    return tuple(shp)


FULL_SHAPES = {
    'w_in': (DEPTH, D, IN_COLS), 'b_in': (DEPTH, IN_COLS), 'pool_w': (DEPTH, 4, POOL_GD, POOL_GD),
    'pool_scale': (DEPTH, D), 'lru_conv_w': (DEPTH, 4, D), 'lru_conv_b': (DEPTH, D),
    'lru_w_r': (DEPTH, LRU_HEADS, LRU_HD, LRU_HD), 'lru_b_r': (DEPTH, D),
    'lru_w_i': (DEPTH, LRU_HEADS, LRU_HD, LRU_HD), 'lru_b_i': (DEPTH, D), 'lru_lambda': (DEPTH, D),
    'lru_w_out': (DEPTH, D, D), 'sconv_w': (DEPTH, 3, D), 'sconv_w_out': (DEPTH, D, D), 'w_mix_out': (DEPTH, D, D),
    'xa_w_q': (DEPTH, D, D), 'xa_w_k': (DEPTH, D, D), 'xa_w_v': (DEPTH, D, D), 'xa_w_o': (DEPTH, D, D),
    'ffn_w_gate': (DEPTH, D, D_FF), 'ffn_w_up': (DEPTH, D, D_FF), 'ffn_w_down': (DEPTH, D_FF, D),
    'ln_g': (DEPTH, 3, D), 'ln_b': (DEPTH, 3, D)}


def _pack(arrs, names, width, dtype, row_mult):
    flat = jnp.concatenate([arrs[n].astype(dtype).reshape(-1) for n in names])
    pad = (-flat.shape[0]) % (width * row_mult)
    if pad:
        flat = jnp.concatenate([flat, jnp.zeros((pad,), dtype)])
    return flat.reshape(-1, width)


def _unpack(flat2d, names, shapes):
    flat = flat2d.reshape(-1)
    out, off = {}, 0
    for n in names:
        size = 1
        for s in shapes[n]:
            size *= s
        out[n] = flat[off:off + size].reshape(shapes[n])
        off += size
    return out


def _gathered_full(g4, names, sharded_axis):
    loc_shapes = {n: _local_shape(n, FULL_SHAPES[n]) for n in names}
    per = [_unpack(g4[k], names, loc_shapes) for k in range(4)]
    return {n: jnp.concatenate([per[k][n] for k in range(4)], axis=sharded_axis[n]) for n in names}


def _gu_joined(g4, u4):
    return jnp.concatenate([g4[0], g4[1], u4[0], u4[1], g4[2], g4[3], u4[2], u4[3]], axis=1)


def _gu_apart(a):
    w = a.shape[1] // 8
    cols = [a[:, i * w:(i + 1) * w] for i in range(8)]
    return jnp.stack([cols[0], cols[1], cols[4], cols[5]]), jnp.stack([cols[2], cols[3], cols[6], cols[7]])


SMALL_SH_NAMES = list(SMALL_SHARDED)
SMALL_ROWS = 32


def kernel(x, mem, w_in, b_in, pool_w, pool_scale, lru_conv_w, lru_conv_b, lru_w_r, lru_b_r, lru_w_i, lru_b_i, lru_lambda, lru_w_out, sconv_w, sconv_w_out, w_mix_out, xa_w_q, xa_w_k, xa_w_v, xa_w_o, ffn_w_gate, ffn_w_up, ffn_w_down, ln_g, ln_b, loss_target, m_w_in, m_b_in, m_pool_w, m_pool_scale, m_lru_conv_w, m_lru_conv_b, m_lru_w_r, m_lru_b_r, m_lru_w_i, m_lru_b_i, m_lru_lambda, m_lru_w_out, m_sconv_w, m_sconv_w_out, m_w_mix_out, m_xa_w_q, m_xa_w_k, m_xa_w_v, m_xa_w_o, m_ffn_w_gate, m_ffn_w_up, m_ffn_w_down, m_ln_g, m_ln_b, v_w_in, v_b_in, v_pool_w, v_pool_scale, v_lru_conv_w, v_lru_conv_b, v_lru_w_r, v_lru_b_r, v_lru_w_i, v_lru_b_i, v_lru_lambda, v_lru_w_out, v_sconv_w, v_sconv_w_out, v_w_mix_out, v_xa_w_q, v_xa_w_k, v_xa_w_v, v_xa_w_o, v_ffn_w_gate, v_ffn_w_up, v_ffn_w_down, v_ln_g, v_ln_b):
    loc = dict(w_in=w_in, b_in=b_in, pool_w=pool_w, pool_scale=pool_scale, lru_conv_w=lru_conv_w,
               lru_conv_b=lru_conv_b, lru_w_r=lru_w_r, lru_b_r=lru_b_r, lru_w_i=lru_w_i, lru_b_i=lru_b_i,
               lru_lambda=lru_lambda, lru_w_out=lru_w_out, sconv_w=sconv_w, sconv_w_out=sconv_w_out,
               w_mix_out=w_mix_out, xa_w_q=xa_w_q, xa_w_k=xa_w_k, xa_w_v=xa_w_v, xa_w_o=xa_w_o,
               ffn_w_gate=ffn_w_gate, ffn_w_up=ffn_w_up, ffn_w_down=ffn_w_down, ln_g=ln_g, ln_b=ln_b)
    mom = dict(w_in=m_w_in, b_in=m_b_in, pool_w=m_pool_w, pool_scale=m_pool_scale, lru_conv_w=m_lru_conv_w,
               lru_conv_b=m_lru_conv_b, lru_w_r=m_lru_w_r, lru_b_r=m_lru_b_r, lru_w_i=m_lru_w_i, lru_b_i=m_lru_b_i,
               lru_lambda=m_lru_lambda, lru_w_out=m_lru_w_out, sconv_w=m_sconv_w, sconv_w_out=m_sconv_w_out,
               w_mix_out=m_w_mix_out, xa_w_q=m_xa_w_q, xa_w_k=m_xa_w_k, xa_w_v=m_xa_w_v, xa_w_o=m_xa_w_o,
               ffn_w_gate=m_ffn_w_gate, ffn_w_up=m_ffn_w_up, ffn_w_down=m_ffn_w_down, ln_g=m_ln_g, ln_b=m_ln_b)
    var = dict(w_in=v_w_in, b_in=v_b_in, pool_w=v_pool_w, pool_scale=v_pool_scale, lru_conv_w=v_lru_conv_w,
               lru_conv_b=v_lru_conv_b, lru_w_r=v_lru_w_r, lru_b_r=v_lru_b_r, lru_w_i=v_lru_w_i, lru_b_i=v_lru_b_i,
               lru_lambda=v_lru_lambda, lru_w_out=v_lru_w_out, sconv_w=v_sconv_w, sconv_w_out=v_sconv_w_out,
               w_mix_out=v_w_mix_out, xa_w_q=v_xa_w_q, xa_w_k=v_xa_w_k, xa_w_v=v_xa_w_v, xa_w_o=v_xa_w_o,
               ffn_w_gate=v_ffn_w_gate, ffn_w_up=v_ffn_w_up, ffn_w_down=v_ffn_w_down, ln_g=v_ln_g, ln_b=v_ln_b)

    chip = 2 * lax.axis_index("x") + lax.axis_index("y")
    core = lax.axis_index("c")
    chip_arr = jnp.reshape(chip, (1,)).astype(jnp.int32)
    core_arr = jnp.reshape(core, (1,)).astype(jnp.int32)
    n_p = len(PIECES)

    shards = {n: loc[n].astype(BF) for n in PIECES}
    small = _pack(loc, SMALL_SH_NAMES, 256, F32, 8)
    first = ('w_in',)
    rest = tuple(n for n in PIECES if n not in first)
    early1 = ('w_in', 'pool_w')
    late1 = tuple(n for n in PIECES if n not in early1)
    got = run_hosted(gather_layer(0, shards, small, pieces=first), "gather_first")
    vec = _gathered_full(got[len(first)], SMALL_SH_NAMES, SMALL_SHARDED)
    W = {n: [None] * DEPTH for n in ('w_in', 'pool_w', 'ffn_w_down', 'ffn_w_gu') + SQUARES}
    W['b_in'] = [jnp.roll(b_in[l:l + 1], -2 * D, axis=1) for l in range(DEPTH)]
    for n in ('lru_conv_w', 'sconv_w', 'ln_g', 'ln_b'):
        W[n] = [vec[n][l] for l in range(DEPTH)]
    for n in ('lru_w_r', 'lru_w_i'):
        W[n] = [loc[n][l].astype(BF) for l in range(DEPTH)]
    for n in ('pool_scale', 'lru_conv_b', 'lru_b_r', 'lru_b_i', 'lru_lambda'):
        W[n] = [loc[n][l:l + 1] for l in range(DEPTH)]

    def take(l, names, arrays):
        full = dict(zip(names, arrays))
        if 'ffn_w_gate' in full:
            W['ffn_w_gu'][l] = _gu_joined(full['ffn_w_gate'], full['ffn_w_up'])
        for n in names:
            if n in W:
                W[n][l] = full[n]

    take(0, first, got[:len(first)])

    def after_in_proj(results):
        take(0, rest, results[:len(rest)])
        take(1, early1, results[len(rest):])

    xs, memb = x[0], mem[0].astype(BF)
    host_a = both_hosted(gather_layer(0, shards, pieces=rest), gather_layer(1, shards, pieces=early1))
    xf, xb, sv0 = _layer_fwd(0, xs, None, None, memb, W, host=host_a, host2=gather_layer(1, shards, pieces=late1),
                             after_in_proj=after_in_proj)
    take(1, late1, sv0['hosted2'])
    xf, xb, sv1 = _layer_fwd(1, xf, xb, None, memb, W)
    saves = [sv0, sv1]
    kvs = [(sv['kb'], sv['vb']) for sv in saves]

    def packed(G):
        g = dict(G)
        g['ffn_w_gate'], g['ffn_w_up'] = _gu_apart(g['ffn_w_gu'])
        g['pool_w'] = g['pool_w'].astype(BF)
        g['b_in'] = jnp.roll(g['b_in'], 2 * D, axis=1)
        return {n: g[n] for n in PIECES}, _pack(g, SMALL_ALL, D, F32, SMALL_ROWS)

    def add_pairs(parts, smallp, got):
        theirs = dict(zip(PIECES, got[:n_p]))
        q = {}
        for group in SAME_SHAPE:
            q.update(add_halves(group, parts, theirs, core_arr))
        qs = add_halves_small(smallp, got[n_p], core_arr)
        return q, qs, exchange_halves(q, qs)

    def sum_chips(q, qs, recv):
        recv3 = dict(zip(PIECES, recv[:n_p]))
        summed = {}
        for group in SAME_SHAPE:
            summed.update(sum_halves(group, q, recv3, chip_arr))
        return [summed[n] for n in PIECES] + [sum_chips_small(qs, recv[n_p], chip_arr)]

    lf = (W['ln_b'][1][2:3], loss_target[0])
    dx, G1, loss_blk = _layer_bwd(1, None, saves[1], memb, kvs[1][0], kvs[1][1], W, loss_from=lf)

    parts1, smallp1 = packed(G1)
    st1, st0 = {}, {}

    def after_down(got):
        st1['q'], st1['qs'], host = add_pairs(parts1, smallp1, got)
        return host

    def after_in(got):
        st1['sums'] = sum_chips(st1['q'], st1['qs'], got)
        return swap_cores_hosted(st1['sums'])

    def after_q(got):
        st1['other'] = list(got)

    def last_host_fn(G):
        parts0, smallp0 = packed(G)
        got = run_hosted(split_halves_hosted(parts0, smallp0), "split_halves")
        st0['q'], st0['qs'], host = add_pairs(parts0, smallp0, got)
        return host

    grad_x, G0, _ = _layer_bwd(0, dx, saves[0], memb, kvs[0][0], kvs[0][1], W,
                               chain=(split_halves_hosted(parts1, smallp1), after_down, after_in, after_q),
                               last_host_fn=last_host_fn)
    sums0 = sum_chips(st0['q'], st0['qs'], G0['hosted_last'])
    red = [(sums0, run_hosted(swap_cores_hosted(sums0), "swap_cores")), (st1['sums'], st1['other'])]
    loss = lax.psum(loss_blk[0, 0], ("x", "y", "c"))

    small_shapes = {n: FULL_SHAPES[n][1:] for n in SMALL_ALL}
    per_layer = []
    for l in range(DEPTH):
        mine, theirs = red[l][0][-1], red[l][1][-1]
        whole = jnp.where(core == 0, jnp.concatenate([mine, theirs]), jnp.concatenate([theirs, mine]))
        per_layer.append(_unpack(whole, SMALL_ALL, small_shapes))
    grads = {}
    for n in SMALL_ALL:
        gn = jnp.stack([per_layer[l][n] for l in range(DEPTH)])
        if n in SMALL_SHARDED:
            size = loc[n].shape[SMALL_SHARDED[n]]
            gn = lax.dynamic_slice_in_dim(gn, chip * size, size, axis=SMALL_SHARDED[n])
        grads[n] = gn

    out_d, out_m, out_v = {}, {}, {}
    for p, n in enumerate(PIECES):
        pairs = [(red[l][0][p], red[l][1][p]) for l in range(DEPTH)]
        grads[n], out_d[n], out_m[n], out_v[n] = adamw_halves(loc[n], pairs, mom[n], var[n], name="adamw_" + n)
    for n in SMALL_ALL:
        out_d[n], out_m[n], out_v[n] = adamw(loc[n], grads[n], mom[n], var[n], name="adamw_" + n)

    return (loss, grad_x[None], *[grads[n] for n in WEIGHTS], *[out_d[n] for n in WEIGHTS],
            *[out_m[n] for n in WEIGHTS], *[out_v[n] for n in WEIGHTS])
```

```python
import jax
import jax.numpy as jnp
from jax import lax
from jax.experimental import pallas as pl
from jax.experimental.pallas import tpu as pltpu

F32 = jnp.float32
BF = jnp.bfloat16
MESH = pl.DeviceIdType.MESH

D = 1024
DEPTH = 2
N_MEM = 256
POOL_WINDOWS = (2, 4, 8, 16)
POOL_GD = 256
LRU_HEADS = 8
LRU_HD = 128
LRU_C = 8.0
X_HEADS = 4
X_HD = 256
D_FF = 2816
IN_COLS = 8 * D
ALPHA = (2 * DEPTH) ** 0.25
LN_EPS = 1e-5
ADAM_LR = 0.001
ADAM_B1 = 0.9
ADAM_B2 = 0.999
ADAM_EPS = 1e-08
ADAM_WD = 0.01
ADAM_STEP = 10

Z_PERM = (2, 3, 4, 5, 6, 7, 0, 1)
ZB_SCONV, ZB_GATE, ZB_POOL, ZB_LRU = 0, 1, 6, 7
HALO = 16
VMEM_LIMIT = 56 * 1024 * 1024

WEIGHTS = ['w_in', 'b_in', 'pool_w', 'pool_scale', 'lru_conv_w', 'lru_conv_b', 'lru_w_r', 'lru_b_r', 'lru_w_i',
           'lru_b_i', 'lru_lambda', 'lru_w_out', 'sconv_w', 'sconv_w_out', 'w_mix_out', 'xa_w_q', 'xa_w_k', 'xa_w_v',
           'xa_w_o', 'ffn_w_gate', 'ffn_w_up', 'ffn_w_down', 'ln_g', 'ln_b']
BIG_SHARDED = {'w_in': 2, 'pool_w': 2, 'lru_w_out': 1, 'sconv_w_out': 1, 'w_mix_out': 1, 'xa_w_q': 1, 'xa_w_k': 1,
               'xa_w_v': 1, 'xa_w_o': 1, 'ffn_w_gate': 2, 'ffn_w_up': 2, 'ffn_w_down': 1}
SMALL_SHARDED = {'lru_conv_w': 2, 'sconv_w': 2, 'ln_g': 2, 'ln_b': 2}
SMALL_ALL = ['b_in', 'pool_scale', 'lru_conv_w', 'lru_conv_b', 'lru_w_r', 'lru_b_r', 'lru_w_i', 'lru_b_i',
             'lru_lambda', 'sconv_w', 'ln_g', 'ln_b']


def _cp(*sem):
    return pltpu.CompilerParams(dimension_semantics=sem, vmem_limit_bytes=VMEM_LIMIT)


def _sigmoid(x):
    return 0.5 * jnp.tanh(0.5 * x) + 0.5


class Hosted:
    def __init__(self, arrays, out_shape, sems, start, finish):
        self.arrays, self.out_shape, self.sems, self.start, self.finish = arrays, out_shape, sems, start, finish


def _with_host(host, n_in, n_out, refs):
    if host is None:
        return refs[:n_in], (), refs[n_in:n_in + n_out], (), refs[n_in + n_out:], ()
    hi, ho, hs = len(host.arrays), len(host.out_shape), len(host.sems)
    ins, h_in = refs[:n_in], refs[n_in:n_in + hi]
    outs = refs[n_in + hi:n_in + hi + n_out]
    h_out = refs[n_in + hi + n_out:n_in + hi + n_out + ho]
    rest = refs[n_in + hi + n_out + ho:]
    return ins, h_in, outs, h_out, rest[:len(rest) - hs], rest[len(rest) - hs:]


HBM_SPEC = pl.BlockSpec(memory_space=pl.ANY)


def mm_nn(a, w, bias, *, out_dtype, tm, tn, name, host=None):
    T, K = a.shape
    N = w.shape[1]
    tm, tn = min(tm, T), min(tn, N)
    nj, ni = N // tn, T // tm
    n_in = 2 if bias is None else 3

    def body(*refs):
        ins, h_in, (o_ref,), h_out, _, h_sems = _with_host(host, n_in, 1, refs)
        a_ref, w_ref = ins[:2]
        j, i = pl.program_id(0), pl.program_id(1)
        if host is not None:
            @pl.when((j == 0) & (i == 0))
            def _():
                host.start(h_in, h_out, h_sems)
        acc = jnp.dot(a_ref[...].astype(BF), w_ref[...], preferred_element_type=F32)
        if bias is not None:
            acc = acc + ins[2][...]
        o_ref[...] = acc.astype(o_ref.dtype)
        if host is not None:
            @pl.when((j == nj - 1) & (i == ni - 1))
            def _():
                host.finish(h_in, h_out, h_sems)

    in_specs = [pl.BlockSpec((tm, K), lambda j, i: (i, 0)), pl.BlockSpec((K, tn), lambda j, i: (0, j))]
    args = [a, w]
    if bias is not None:
        in_specs.append(pl.BlockSpec((1, tn), lambda j, i: (0, j)))
        args.append(bias)
    out_specs = [pl.BlockSpec((tm, tn), lambda j, i: (i, j))]
    out_shape = [jax.ShapeDtypeStruct((T, N), out_dtype)]
    scratch = []
    if host is not None:
        in_specs += [HBM_SPEC] * len(host.arrays)
        args += list(host.arrays)
        out_specs += [HBM_SPEC] * len(host.out_shape)
        out_shape += list(host.out_shape)
        scratch = list(host.sems)
    res = pl.pallas_call(
        body, name=name, grid=(nj, ni), in_specs=in_specs, out_specs=out_specs, out_shape=out_shape,
        scratch_shapes=scratch, compiler_params=_cp("arbitrary", "arbitrary"))(*args)
    return res[0] if host is None else (res[0], res[1:])


FF_HALF = D_FF // 2


def ffn_in_swiglu(x, wgu, *, tm, name, host=None):
    T, K = x.shape
    tm = min(tm, T)
    ni = T // tm

    def body(*refs):
        (x_ref, w_ref), h_in, (gu_ref, h_ref), h_out, _, h_sems = _with_host(host, 2, 2, refs)
        j, i = pl.program_id(0), pl.program_id(1)
        if host is not None:
            @pl.when((j == 0) & (i == 0))
            def _():
                host.start(h_in, h_out, h_sems)
        acc = jnp.dot(x_ref[...].astype(BF), w_ref[...], preferred_element_type=F32)
        gu = acc.astype(BF)
        gu_ref[...] = gu
        g = gu[:, :FF_HALF].astype(F32)
        h_ref[...] = (g * _sigmoid(g) * gu[:, FF_HALF:].astype(F32)).astype(BF)
        if host is not None:
            @pl.when((j == 1) & (i == ni - 1))
            def _():
                host.finish(h_in, h_out, h_sems)

    in_specs = [pl.BlockSpec((tm, K), lambda j, i: (i, 0)), pl.BlockSpec((K, 2 * FF_HALF), lambda j, i: (0, j))]
    args = [x, wgu]
    out_specs = [pl.BlockSpec((tm, 2 * FF_HALF), lambda j, i: (i, j)), pl.BlockSpec((tm, FF_HALF), lambda j, i: (i, j))]
    out_shape = [jax.ShapeDtypeStruct((T, 2 * D_FF), BF), jax.ShapeDtypeStruct((T, D_FF), BF)]
    scratch = []
    if host is not None:
        in_specs += [HBM_SPEC] * len(host.arrays)
        args += list(host.arrays)
        out_specs += [HBM_SPEC] * len(host.out_shape)
        out_shape += list(host.out_shape)
        scratch = list(host.sems)
    res = pl.pallas_call(
        body, name=name, grid=(2, ni), in_specs=in_specs, out_specs=out_specs, out_shape=out_shape,
        scratch_shapes=scratch, compiler_params=_cp("arbitrary", "arbitrary"))(*args)
    return (res[0], res[1]) if host is None else (res[0], res[1], res[2:])


def ffn_down_dx_swiglu(dp, wd, gu, *, tm, name, host=None):
    T = dp.shape[0]
    tm = min(tm, T)
    ni = T // tm

    def body(*refs):
        (dp_ref, w_ref, gu_ref), h_in, (o_ref,), h_out, _, h_sems = _with_host(host, 3, 1, refs)
        if host is not None:
            j, i = pl.program_id(0), pl.program_id(1)

            @pl.when((j == 0) & (i == 0))
            def _():
                host.start(h_in, h_out, h_sems)

            @pl.when((j == 1) & (i == ni - 1))
            def _():
                host.finish(h_in, h_out, h_sems)
        dh = lax.dot_general(dp_ref[...], w_ref[...], (((1,), (1,)), ((), ())), preferred_element_type=F32)
        g = gu_ref[:, :FF_HALF].astype(F32)
        u = gu_ref[:, FF_HALF:].astype(F32)
        sg = _sigmoid(g)
        t = dh * sg
        s = g * sg
        o_ref[:, :FF_HALF] = (t * u * (1.0 + g - s)).astype(BF)
        o_ref[:, FF_HALF:] = (t * g).astype(BF)

    in_specs = [pl.BlockSpec((tm, D), lambda j, i: (i, 0)), pl.BlockSpec((FF_HALF, D), lambda j, i: (j, 0)),
                pl.BlockSpec((tm, 2 * FF_HALF), lambda j, i: (i, j))]
    args = [dp, wd, gu]
    out_specs = [pl.BlockSpec((tm, 2 * FF_HALF), lambda j, i: (i, j))]
    out_shape = [jax.ShapeDtypeStruct((T, 2 * D_FF), BF)]
    scratch = []
    if host is not None:
        in_specs += [HBM_SPEC] * len(host.arrays)
        args += list(host.arrays)
        out_specs += [HBM_SPEC] * len(host.out_shape)
        out_shape += list(host.out_shape)
        scratch = list(host.sems)
    res = pl.pallas_call(
        body, name=name, grid=(2, ni), in_specs=in_specs, out_specs=out_specs, out_shape=out_shape,
        scratch_shapes=scratch, compiler_params=_cp("arbitrary", "arbitrary"))(*args)
    return res[0] if host is None else (res[0], res[1:])


def mm_nt(a, w, res, *, out_dtype, tm, tc, name, host=None):
    T, C = a.shape
    K = w.shape[0]
    tm, tc = min(tm, T), min(tc, C)
    nc = C // tc
    ni = T // tm
    n_in = 2 if res is None else 3

    def body(*refs):
        ins, h_in, (o_ref,), h_out, (acc_ref,), h_sems = _with_host(host, n_in, 1, refs)
        a_ref, w_ref = ins[:2]
        r_ref = ins[2] if res is not None else None
        c = pl.program_id(1)
        if host is not None:
            @pl.when((pl.program_id(0) == 0) & (c == 0))
            def _():
                host.start(h_in, h_out, h_sems)

            @pl.when((pl.program_id(0) == ni - 1) & (c == nc - 1))
            def _():
                host.finish(h_in, h_out, h_sems)
        part = lax.dot_general(a_ref[...].astype(BF), w_ref[...], (((1,), (1,)), ((), ())),
                               preferred_element_type=F32)

        @pl.when(c == 0)
        def _():
            acc_ref[...] = part

        @pl.when(c > 0)
        def _():
            acc_ref[...] += part

        @pl.when(c == nc - 1)
        def _():
            out = acc_ref[...]
            if res is not None:
                out = out + ALPHA * r_ref[...]
            o_ref[...] = out.astype(o_ref.dtype)

    in_specs = [pl.BlockSpec((tm, tc), lambda i, c: (i, c)), pl.BlockSpec((K, tc), lambda i, c: (0, c))]
    args = [a, w]
    if res is not None:
        in_specs.append(pl.BlockSpec((tm, K), lambda i, c: (i, 0)))
        args.append(res)
    out_specs = [pl.BlockSpec((tm, K), lambda i, c: (i, 0))]
    out_shape = [jax.ShapeDtypeStruct((T, K), out_dtype)]
    scratch = [pltpu.VMEM((tm, K), F32)]
    if host is not None:
        in_specs += [HBM_SPEC] * len(host.arrays)
        args += list(host.arrays)
        out_specs += [HBM_SPEC] * len(host.out_shape)
        out_shape += list(host.out_shape)
        scratch += list(host.sems)
    out = pl.pallas_call(
        body, name=name, grid=(ni, nc), in_specs=in_specs, out_specs=out_specs, out_shape=out_shape,
        scratch_shapes=scratch, compiler_params=_cp("arbitrary", "arbitrary"))(*args)
    return out[0] if host is None else (out[0], out[1:])


def mm_tn(a, b, *, out_dtype, tk, tn, tt, name, colsum=False):
    T, K = a.shape
    N = b.shape[1]
    tk, tn, tt = min(tk, K), min(tn, N), min(tt, T)
    nt = T // tt

    def body(*refs):
        if colsum:
            a_ref, b_ref, o_ref, cs_ref, acc_ref = refs
        else:
            a_ref, b_ref, o_ref, acc_ref = refs
        i, t = pl.program_id(1), pl.program_id(2)
        bb = b_ref[...]
        part = lax.dot_general(a_ref[...].astype(BF), bb.astype(BF), (((0,), (0,)), ((), ())),
                               preferred_element_type=F32)

        @pl.when(t == 0)
        def _():
            acc_ref[...] = part

        @pl.when(t > 0)
        def _():
            acc_ref[...] += part

        @pl.when(t == nt - 1)
        def _():
            o_ref[...] = acc_ref[...].astype(o_ref.dtype)

        if colsum:
            s = jnp.sum(bb.astype(F32), axis=0, keepdims=True)

            @pl.when((i == 0) & (t == 0))
            def _():
                cs_ref[...] = s

            @pl.when((i == 0) & (t > 0))
            def _():
                cs_ref[...] += s

    out_specs = [pl.BlockSpec((tk, tn), lambda j, i, t: (i, j))]
    out_shape = [jax.ShapeDtypeStruct((K, N), out_dtype)]
    if colsum:
        out_specs.append(pl.BlockSpec((1, tn), lambda j, i, t: (0, j)))
        out_shape.append(jax.ShapeDtypeStruct((1, N), F32))
    res = pl.pallas_call(
        body, name=name, grid=(N // tn, K // tk, nt),
        in_specs=[pl.BlockSpec((tt, tk), lambda j, i, t: (t, i)), pl.BlockSpec((tt, tn), lambda j, i, t: (t, j))],
        out_specs=out_specs, out_shape=out_shape, scratch_shapes=[pltpu.VMEM((tk, tn), F32)],
        compiler_params=_cp("arbitrary", "arbitrary", "arbitrary"))(a, b)
    return res if colsum else res[0]


def mm_res_ln(a, w, res, g, b, *, tm, name):
    T, K = a.shape
    tm = min(tm, T)

    def body(a_ref, w_ref, r_ref, g_ref, b_ref, y_ref, yb_ref, xh_ref, rs_ref):
        pre = ALPHA * r_ref[...] + jnp.dot(a_ref[...].astype(BF), w_ref[...], preferred_element_type=F32)
        mu = jnp.mean(pre, axis=-1, keepdims=True)
        cen = pre - mu
        var = jnp.mean(cen * cen, axis=-1, keepdims=True)
        rstd = lax.rsqrt(var + LN_EPS)
        xhat = cen * rstd
        y = xhat * g_ref[...] + b_ref[...]
        y_ref[...] = y
        yb_ref[...] = y.astype(BF)
        xh_ref[...] = xhat
        rs_ref[...] = rstd

    row = lambda i: (i, 0)
    fix = lambda i: (0, 0)
    return pl.pallas_call(
        body, name=name, grid=(T // tm,),
        in_specs=[pl.BlockSpec((tm, K), row), pl.BlockSpec((K, D), fix), pl.BlockSpec((tm, D), row),
                  pl.BlockSpec((1, D), fix), pl.BlockSpec((1, D), fix)],
        out_specs=[pl.BlockSpec((tm, D), row), pl.BlockSpec((tm, D), row), pl.BlockSpec((tm, D), row),
                   pl.BlockSpec((tm, 1), row)],
        out_shape=[jax.ShapeDtypeStruct((T, D), F32), jax.ShapeDtypeStruct((T, D), BF),
                   jax.ShapeDtypeStruct((T, D), F32), jax.ShapeDtypeStruct((T, 1), F32)],
        compiler_params=_cp("arbitrary"))(a, w, res, g, b)


def ln_bwd(dy, xhat, rstd, g, *, tm, name, loss_from=None, dw_of=None):
    T = xhat.shape[0]
    tm = min(tm, T)
    nt = T // tm
    with_loss = loss_from is not None
    with_dw = dw_of is not None

    def body(*refs):
        if with_dw:
            acc_ref, refs = refs[-1], refs[:-1]
            n_main_in = 5 if with_loss else 4
            a_ref = refs[n_main_in]
            dw_ref = refs[-1]
            refs = refs[:n_main_in] + refs[n_main_in + 1:-1]
        if with_loss:
            xh_ref, rs_ref, g_ref, b_ref, t_ref, dp_ref, dpb_ref, dg_ref, db_ref, ls_ref = refs
        else:
            dy_ref, xh_ref, rs_ref, g_ref, dp_ref, dpb_ref, dg_ref, db_ref = refs
        i = pl.program_id(0)
        xhat_ = xh_ref[...]
        gg = g_ref[...]
        if with_loss:
            err = xhat_ * gg + b_ref[...] - t_ref[...]
            dyv = err * (1.0 / D)
            lpart = 0.5 * jnp.sum(jnp.sum(err * err, axis=-1, keepdims=True) * (1.0 / D))
        else:
            dyv = dy_ref[...]
        dxh = dyv * gg
        m1 = jnp.mean(dxh, axis=-1, keepdims=True)
        m2 = jnp.mean(dxh * xhat_, axis=-1, keepdims=True)
        dpre = rs_ref[...] * (dxh - m1 - xhat_ * m2)
        dp_ref[...] = dpre
        dpb = dpre.astype(BF)
        dpb_ref[...] = dpb
        dgp = jnp.sum(dyv * xhat_, axis=0, keepdims=True)
        dbp = jnp.sum(dyv, axis=0, keepdims=True)
        if with_dw:
            dwp = lax.dot_general(a_ref[...], dpb, (((0,), (0,)), ((), ())), preferred_element_type=F32)

        @pl.when(i == 0)
        def _():
            dg_ref[...] = dgp
            db_ref[...] = dbp
            if with_loss:
                ls_ref[...] = jnp.full((8, 128), lpart, F32)
            if with_dw:
                acc_ref[...] = dwp

        @pl.when(i > 0)
        def _():
            dg_ref[...] += dgp
            db_ref[...] += dbp
            if with_loss:
                ls_ref[...] += jnp.full((8, 128), lpart, F32)
            if with_dw:
                acc_ref[...] += dwp

        if with_dw:
            @pl.when(i == nt - 1)
            def _():
                dw_ref[...] = acc_ref[...].astype(BF)

    row = lambda i: (i, 0)
    fix = lambda i: (0, 0)
    if with_loss:
        in_specs = [pl.BlockSpec((tm, D), row), pl.BlockSpec((tm, 1), row), pl.BlockSpec((1, D), fix),
                    pl.BlockSpec((1, D), fix), pl.BlockSpec((tm, D), row)]
        args = [xhat, rstd, g, loss_from[0], loss_from[1]]
    else:
        in_specs = [pl.BlockSpec((tm, D), row), pl.BlockSpec((tm, D), row), pl.BlockSpec((tm, 1), row),
                    pl.BlockSpec((1, D), fix)]
        args = [dy, xhat, rstd, g]
    out_specs = [pl.BlockSpec((tm, D), row), pl.BlockSpec((tm, D), row), pl.BlockSpec((1, D), fix),
                 pl.BlockSpec((1, D), fix)]
    out_shape = [jax.ShapeDtypeStruct((T, D), F32), jax.ShapeDtypeStruct((T, D), BF),
                 jax.ShapeDtypeStruct((1, D), F32), jax.ShapeDtypeStruct((1, D), F32)]
    if with_loss:
        out_specs.append(pl.BlockSpec((8, 128), fix))
        out_shape.append(jax.ShapeDtypeStruct((8, 128), F32))
    scratch = []
    if with_dw:
        K = dw_of.shape[1]
        in_specs.append(pl.BlockSpec((tm, K), row))
        args.append(dw_of)
        out_specs.append(pl.BlockSpec((K, D), fix))
        out_shape.append(jax.ShapeDtypeStruct((K, D), BF))
        scratch.append(pltpu.VMEM((K, D), F32))
    return pl.pallas_call(body, name=name, grid=(nt,), in_specs=in_specs, out_specs=out_specs,
                          out_shape=out_shape, scratch_shapes=scratch, compiler_params=_cp("arbitrary"))(*args)


def _prev_halo(tm, blk):
    return lambda i: (jnp.maximum(i * (tm // HALO) - 1, 0), blk)


def _next_halo(tm, T, blk):
    return lambda i: (jnp.minimum((i + 1) * (tm // HALO), T // HALO - 1), blk)


def _pool_p(ext, t, g):
    e = ext[:, g * POOL_GD:(g + 1) * POOL_GD]
    s = e
    for sh in (1, 2, 4, 8)[:g + 1]:
        s = s + pltpu.roll(s, sh, axis=0)
    inv = 1.0 / jnp.minimum(t + 1, POOL_WINDOWS[g]).astype(F32)
    return s[HALO:] * inv - e[HALO:]


def pool_fwd(z, pw, *, tm, name):
    T = z.shape[0]
    tm = min(tm, T)

    def body(zm_ref, zh_ref, pw_ref, o_ref):
        i = pl.program_id(0)
        keep = jnp.where(i == 0, 0.0, 1.0).astype(F32)
        ext = jnp.concatenate([zh_ref[...].astype(F32) * keep, zm_ref[...].astype(F32)], axis=0)
        t = i * tm + lax.broadcasted_iota(jnp.int32, (tm, 1), 0)
        outs = [jnp.dot(_pool_p(ext, t, g).astype(BF), pw_ref[g], preferred_element_type=F32) for g in range(4)]
        o_ref[...] = jnp.concatenate(outs, axis=1).astype(o_ref.dtype)

    return pl.pallas_call(
        body, name=name, grid=(T // tm,),
        in_specs=[pl.BlockSpec((tm, D), lambda i: (i, ZB_POOL)), pl.BlockSpec((HALO, D), _prev_halo(tm, ZB_POOL)),
                  pl.BlockSpec((4, POOL_GD, POOL_GD), lambda i: (0, 0, 0))],
        out_specs=pl.BlockSpec((tm, D), lambda i: (i, 0)),
        out_shape=jax.ShapeDtypeStruct((T, D), BF), compiler_params=_cp("arbitrary"))(z, z, pw)


def pool_bwd(dz, dyp, yp_pre, z, pw, ps, *, tm, name):
    T = z.shape[0]
    tm = min(tm, T)
    nt = T // tm

    def body(dz_in, dy_ref, dyn_ref, yp_ref, zm_ref, zh_ref, pw_ref, ps_ref, dz_ref, dpw_ref, dps_ref):
        del dz_in
        i = pl.program_id(0)
        keep_p = jnp.where(i == 0, 0.0, 1.0).astype(F32)
        keep_n = jnp.where(i == nt - 1, 0.0, 1.0).astype(F32)
        ext = jnp.concatenate([zh_ref[...].astype(F32) * keep_p, zm_ref[...].astype(F32)], axis=0)
        t = i * tm + lax.broadcasted_iota(jnp.int32, (tm, 1), 0)
        psv = ps_ref[...]
        dy = dy_ref[...].astype(F32)
        dyp_ext = jnp.concatenate([dy, dyn_ref[...].astype(F32) * keep_n], axis=0) * psv
        t_ext = i * tm + lax.broadcasted_iota(jnp.int32, (tm + HALO, 1), 0)
        dps = jnp.sum(dy * yp_ref[...].astype(F32), axis=0, keepdims=True)
        dzs, dpws = [], []
        for g in range(4):
            sl = slice(g * POOL_GD, (g + 1) * POOL_GD)
            dyg = dyp_ext[:, sl].astype(BF)
            dp = lax.dot_general(dyg, pw_ref[g], (((1,), (1,)), ((), ())), preferred_element_type=F32)
            q = dp * (1.0 / jnp.minimum(t_ext + 1, POOL_WINDOWS[g]).astype(F32))
            s = q
            for sh in (1, 2, 4, 8)[:g + 1]:
                s = s + pltpu.roll(s, tm + HALO - sh, axis=0)
            dzs.append(s[:tm] - dp[:tm])
            p = _pool_p(ext, t, g).astype(BF)
            dpws.append(lax.dot_general(p, dyg[:tm], (((0,), (0,)), ((), ())), preferred_element_type=F32))
        dz_ref[...] = jnp.concatenate(dzs, axis=1).astype(dz_ref.dtype)

        @pl.when(i == 0)
        def _():
            for g in range(4):
                dpw_ref[g] = dpws[g]
            dps_ref[...] = dps

        @pl.when(i > 0)
        def _():
            for g in range(4):
                dpw_ref[g] += dpws[g]
            dps_ref[...] += dps

    row = lambda i: (i, 0)
    return pl.pallas_call(
        body, name=name, grid=(nt,),
        in_specs=[pl.BlockSpec(memory_space=pl.ANY),
                  pl.BlockSpec((tm, D), row), pl.BlockSpec((HALO, D), _next_halo(tm, T, 0)),
                  pl.BlockSpec((tm, D), row),
                  pl.BlockSpec((tm, D), lambda i: (i, ZB_POOL)), pl.BlockSpec((HALO, D), _prev_halo(tm, ZB_POOL)),
                  pl.BlockSpec((4, POOL_GD, POOL_GD), lambda i: (0, 0, 0)), pl.BlockSpec((1, D), lambda i: (0, 0))],
        out_specs=[pl.BlockSpec((tm, D), lambda i: (i, ZB_POOL)),
                   pl.BlockSpec((4, POOL_GD, POOL_GD), lambda i: (0, 0, 0)), pl.BlockSpec((1, D), lambda i: (0, 0))],
        out_shape=[jax.ShapeDtypeStruct(dz.shape, dz.dtype), jax.ShapeDtypeStruct((4, POOL_GD, POOL_GD), F32),
                   jax.ShapeDtypeStruct((1, D), F32)],
        input_output_aliases={0: 0}, compiler_params=_cp("arbitrary"))(dz, dyp, dyp, yp_pre, z, z, pw, ps)


def _fill_ext(ext_s, halo, main, keep):
    ext_s[0:HALO, :] = halo * keep
    ext_s[HALO:, :] = main


def _lru_gates(ext_s, tm, cw, cb, wr_ref, br, wi_ref, bi, lam):
    shifted = []
    v = cb
    for k in range(4):
        zs = ext_s[pl.ds(HALO - 3 + k, tm), :]
        shifted.append(zs)
        v = v + cw[k:k + 1, :] * zs
    vb = v.astype(BF)
    rp, ip = [], []
    for h in range(LRU_HEADS):
        sl = slice(h * LRU_HD, (h + 1) * LRU_HD)
        rp.append(jnp.dot(vb[:, sl], wr_ref[h], preferred_element_type=F32))
        ip.append(jnp.dot(vb[:, sl], wi_ref[h], preferred_element_type=F32))
    r = _sigmoid(jnp.concatenate(rp, axis=1) + br)
    ig = _sigmoid(jnp.concatenate(ip, axis=1) + bi)
    sp = jnp.maximum(-lam, 0.0) + jnp.log(1.0 + jnp.exp(-jnp.abs(lam)))
    a = jnp.exp(-LRU_C * r * sp)
    om = 1.0 - a * a
    rs = lax.rsqrt(om)
    return v, vb, r, ig, a, om, rs, sp, shifted


def lru_fwd(z, cw, cb, wr, br, wi, bi, lam, wlo, *, tm, name):
    T = z.shape[0]
    tm = min(tm, T)
    nch = tm // 8

    def body(zm_ref, zh_ref, cw_ref, cb_ref, wr_ref, br_ref, wi_ref, bi_ref, lam_ref, wlo_ref, h_ref, y_ref,
             a_s, b_s, carry, ext_s):
        i = pl.program_id(0)

        @pl.when(i == 0)
        def _():
            carry[...] = jnp.zeros_like(carry)

        keep = jnp.where(i == 0, 0.0, 1.0).astype(F32)
        _fill_ext(ext_s, zh_ref[...].astype(F32), zm_ref[...].astype(F32), keep)
        v, _, _, ig, a, om, rs, _, _ = _lru_gates(ext_s, tm, cw_ref[...], cb_ref[...], wr_ref, br_ref[...], wi_ref,
                                                  bi_ref[...], lam_ref[...])
        a_s[...] = a
        b_s[...] = jnp.where(om > 0.0, om * rs, 0.0) * (ig * v)
        row = lax.broadcasted_iota(jnp.int32, (8, D), 0)

        def step(ci, hprev):
            sl = pl.ds(pl.multiple_of(ci * 8, 8), 8)
            aa, bb = a_s[sl, :], b_s[sl, :]
            for s in (1, 2, 4):
                m = row >= s
                bb = bb + aa * jnp.where(m, pltpu.roll(bb, s, axis=0), 0.0)
                aa = aa * jnp.where(m, pltpu.roll(aa, s, axis=0), 1.0)
            h = bb + aa * hprev
            h_ref[sl, :] = h
            return jnp.broadcast_to(h[7:8, :], (8, D))

        carry[...] = lax.fori_loop(0, nch, step, carry[...])
        y_ref[...] = jnp.dot(h_ref[...].astype(BF), wlo_ref[...], preferred_element_type=F32).astype(BF)

    fix2 = lambda i: (0, 0)
    fix3 = lambda i: (0, 0, 0)
    return pl.pallas_call(
        body, name=name, grid=(T // tm,),
        in_specs=[pl.BlockSpec((tm, D), lambda i: (i, ZB_LRU)), pl.BlockSpec((HALO, D), _prev_halo(tm, ZB_LRU)),
                  pl.BlockSpec((4, D), fix2), pl.BlockSpec((1, D), fix2),
                  pl.BlockSpec((LRU_HEADS, LRU_HD, LRU_HD), fix3), pl.BlockSpec((1, D), fix2),
                  pl.BlockSpec((LRU_HEADS, LRU_HD, LRU_HD), fix3), pl.BlockSpec((1, D), fix2),
                  pl.BlockSpec((1, D), fix2), pl.BlockSpec((D, D), fix2)],
        out_specs=[pl.BlockSpec((tm, D), lambda i: (i, 0)), pl.BlockSpec((tm, D), lambda i: (i, 0))],
        out_shape=[jax.ShapeDtypeStruct((T, D), F32), jax.ShapeDtypeStruct((T, D), BF)],
        scratch_shapes=[pltpu.VMEM((tm, D), F32), pltpu.VMEM((tm, D), F32), pltpu.VMEM((8, D), F32),
                        pltpu.VMEM((tm + HALO, D), F32)],
        compiler_params=_cp("arbitrary"))(z, z, cw, cb, wr, br, wi, bi, lam, wlo)


def lru_bwd(dz, dyl, z, h, cw, cb, wr, br, wi, bi, lam, wlo, *, tm, name):
    T = z.shape[0]
    tm = min(tm, T)
    nt = T // tm
    nch = tm // 8

    def body(dz_in, dy_ref, zm_ref, zh_ref, h_ref, hh_ref, cw_ref, cb_ref, wr_ref, br_ref, wi_ref, bi_ref, lam_ref,
             wlo_ref, dz_ref, dcw_ref, dcb_ref, dwr_ref, dbr_ref, dwi_ref, dbi_ref, dlam_ref, dwlo_ref,
             c_s, g_s, dh_s, dh_carry, a_ext, dv_ext, ext_s, h_ext, wlo_acc):
        del dz_in
        i = pl.program_id(0)
        ti = nt - 1 - i

        @pl.when(i == 0)
        def _():
            dh_carry[...] = jnp.zeros_like(dh_carry)
            a_ext[tm:, :] = jnp.zeros((8, D), F32)
            dv_ext[tm:, :] = jnp.zeros((HALO, D), F32)

        keep = jnp.where(ti == 0, 0.0, 1.0).astype(F32)
        _fill_ext(ext_s, zh_ref[...].astype(F32), zm_ref[...].astype(F32), keep)
        cw_ = cw_ref[...]
        lam_ = lam_ref[...]
        v, vb, r, ig, a, om, rs, sp, shifted = _lru_gates(ext_s, tm, cw_, cb_ref[...], wr_ref, br_ref[...], wi_ref,
                                                          bi_ref[...], lam_)
        mult = jnp.where(om > 0.0, om * rs, 0.0)
        a_ext[0:tm, :] = a
        c_s[...] = a_ext[pl.ds(1, tm), :]
        g_s[...] = lax.dot_general(dy_ref[...], wlo_ref[...], (((1,), (1,)), ((), ())), preferred_element_type=F32)
        row = lax.broadcasted_iota(jnp.int32, (8, D), 0)

        def step(k, nxt):
            ci = nch - 1 - k
            sl = pl.ds(pl.multiple_of(ci * 8, 8), 8)
            cc, gg = c_s[sl, :], g_s[sl, :]
            for s in (1, 2, 4):
                m = row < 8 - s
                gg = gg + cc * jnp.where(m, pltpu.roll(gg, 8 - s, axis=0), 0.0)
                cc = cc * jnp.where(m, pltpu.roll(cc, 8 - s, axis=0), 1.0)
            dh = gg + cc * nxt
            dh_s[sl, :] = dh
            return jnp.broadcast_to(dh[0:1, :], (8, D))

        dh_carry[...] = lax.fori_loop(0, nch, step, dh_carry[...])
        a_ext[tm:, :] = a[0:8, :]
        dh = dh_s[...]
        h_ext[0:8, :] = hh_ref[...] * keep
        hv = h_ref[...]
        h_ext[8:, :] = hv
        hprev = h_ext[pl.ds(7, tm), :]
        dwlo = lax.dot_general(hv.astype(BF), dy_ref[...], (((0,), (0,)), ((), ())), preferred_element_type=F32)
        iv = ig * v
        da = dh * hprev
        dmult = dh * iv
        div = dh * mult
        dlog = da * a - dmult * (a * a) * rs
        dr = dlog * (-LRU_C * sp)
        dlam = jnp.sum(dlog * r, axis=0, keepdims=True) * (LRU_C * _sigmoid(-lam_))
        di = div * v
        dv = div * ig
        drp = dr * r * (1.0 - r)
        dip = di * ig * (1.0 - ig)
        drb, dib = drp.astype(BF), dip.astype(BF)
        dvh, dwr, dwi = [], [], []
        nt_dims = (((1,), (1,)), ((), ()))
        tn_dims = (((0,), (0,)), ((), ()))
        for hd in range(LRU_HEADS):
            sl = slice(hd * LRU_HD, (hd + 1) * LRU_HD)
            dvh.append(lax.dot_general(drb[:, sl], wr_ref[hd], nt_dims, preferred_element_type=F32)
                       + lax.dot_general(dib[:, sl], wi_ref[hd], nt_dims, preferred_element_type=F32))
            dwr.append(lax.dot_general(vb[:, sl], drb[:, sl], tn_dims, preferred_element_type=F32))
            dwi.append(lax.dot_general(vb[:, sl], dib[:, sl], tn_dims, preferred_element_type=F32))
        dv = dv + jnp.concatenate(dvh, axis=1)
        dv_ext[0:tm, :] = dv
        dzl = cw_[3:4, :] * dv
        for k in range(3):
            dzl = dzl + cw_[k:k + 1, :] * dv_ext[pl.ds(3 - k, tm), :]
        dz_ref[...] = dzl.astype(dz_ref.dtype)
        dv_ext[tm:, :] = dv[:HALO]
        dcw = jnp.concatenate([jnp.sum(dv * shifted[k], axis=0, keepdims=True) for k in range(4)], axis=0)
        dcb = jnp.sum(dv, axis=0, keepdims=True)
        dbr = jnp.sum(drp, axis=0, keepdims=True)
        dbi = jnp.sum(dip, axis=0, keepdims=True)

        @pl.when(i == 0)
        def _():
            dcw_ref[...] = dcw
            dcb_ref[...] = dcb
            dbr_ref[...] = dbr
            dbi_ref[...] = dbi
            dlam_ref[...] = dlam
            wlo_acc[...] = dwlo
            for hd in range(LRU_HEADS):
                dwr_ref[hd] = dwr[hd]
                dwi_ref[hd] = dwi[hd]

        @pl.when(i > 0)
        def _():
            dcw_ref[...] += dcw
            dcb_ref[...] += dcb
            dbr_ref[...] += dbr
            dbi_ref[...] += dbi
            dlam_ref[...] += dlam
            wlo_acc[...] += dwlo
            for hd in range(LRU_HEADS):
                dwr_ref[hd] += dwr[hd]
                dwi_ref[hd] += dwi[hd]

        @pl.when(i == nt - 1)
        def _():
            dwlo_ref[...] = wlo_acc[...].astype(BF)

    fix2 = lambda i: (0, 0)
    fix3 = lambda i: (0, 0, 0)
    rev = lambda i: (nt - 1 - i, 0)
    vec = pl.BlockSpec((1, D), fix2)
    hw = pl.BlockSpec((LRU_HEADS, LRU_HD, LRU_HD), fix3)
    return pl.pallas_call(
        body, name=name, grid=(nt,),
        in_specs=[pl.BlockSpec(memory_space=pl.ANY),
                  pl.BlockSpec((tm, D), rev),
                  pl.BlockSpec((tm, D), lambda i: (nt - 1 - i, ZB_LRU)),
                  pl.BlockSpec((HALO, D), lambda i: (jnp.maximum((nt - 1 - i) * (tm // HALO) - 1, 0), ZB_LRU)),
                  pl.BlockSpec((tm, D), rev),
                  pl.BlockSpec((8, D), lambda i: (jnp.maximum((nt - 1 - i) * (tm // 8) - 1, 0), 0)),
                  pl.BlockSpec((4, D), fix2), vec, hw, vec, hw, vec, vec, pl.BlockSpec((D, D), fix2)],
        out_specs=[pl.BlockSpec((tm, D), lambda i: (nt - 1 - i, ZB_LRU)),
                   pl.BlockSpec((4, D), fix2), vec, hw, vec, hw, vec, vec, pl.BlockSpec((D, D), fix2)],
        out_shape=[jax.ShapeDtypeStruct(dz.shape, dz.dtype), jax.ShapeDtypeStruct((4, D), F32),
                   jax.ShapeDtypeStruct((1, D), F32), jax.ShapeDtypeStruct((LRU_HEADS, LRU_HD, LRU_HD), F32),
                   jax.ShapeDtypeStruct((1, D), F32), jax.ShapeDtypeStruct((LRU_HEADS, LRU_HD, LRU_HD), F32),
                   jax.ShapeDtypeStruct((1, D), F32), jax.ShapeDtypeStruct((1, D), F32),
                   jax.ShapeDtypeStruct((D, D), BF)],
        scratch_shapes=[pltpu.VMEM((tm, D), F32), pltpu.VMEM((tm, D), F32), pltpu.VMEM((tm, D), F32),
                        pltpu.VMEM((8, D), F32), pltpu.VMEM((tm + 8, D), F32), pltpu.VMEM((tm + HALO, D), F32),
                        pltpu.VMEM((tm + HALO, D), F32), pltpu.VMEM((tm + 8, D), F32), pltpu.VMEM((D, D), F32)],
        input_output_aliases={0: 0},
        compiler_params=_cp("arbitrary"))(dz, dyl, z, z, h, h, cw, cb, wr, br, wi, bi, lam, wlo)


def _sconv_cv(u_ext, sw):
    shifted = []
    cv = None
    for k in range(3):
        us = (u_ext if k == 2 else pltpu.roll(u_ext, 2 - k, axis=0))[HALO:]
        shifted.append(us)
        term = sw[k:k + 1, :] * us
        cv = term if cv is None else cv + term
    return cv, shifted


def sconv_fwd(z, sw, wso, *, tm, name):
    T = z.shape[0]
    tm = min(tm, T)

    def body(zm_ref, zh_ref, sw_ref, wso_ref, y_ref):
        i = pl.program_id(0)
        keep = jnp.where(i == 0, 0.0, 1.0).astype(F32)
        zm = zm_ref[...].astype(F32)
        zh = zh_ref[...].astype(F32)
        u_ext = jnp.concatenate([zh[:, D:2 * D] * zh[:, 2 * D:] * keep, zm[:, D:2 * D] * zm[:, 2 * D:]], axis=0)
        cv, _ = _sconv_cv(u_ext, sw_ref[...])
        s = (zm[:, :D] * cv).astype(BF)
        y_ref[...] = jnp.dot(s, wso_ref[...], preferred_element_type=F32).astype(BF)

    return pl.pallas_call(
        body, name=name, grid=(T // tm,),
        in_specs=[pl.BlockSpec((tm, 3 * D), lambda i: (i, ZB_SCONV)),
                  pl.BlockSpec((HALO, 3 * D), _prev_halo(tm, ZB_SCONV)),
                  pl.BlockSpec((3, D), lambda i: (0, 0)), pl.BlockSpec((D, D), lambda i: (0, 0))],
        out_specs=pl.BlockSpec((tm, D), lambda i: (i, 0)),
        out_shape=jax.ShapeDtypeStruct((T, D), BF),
        compiler_params=_cp("arbitrary"))(z, z, sw, wso)


def sconv_bwd(dz, dyc, z, sw, wso, *, tm, name):
    T = z.shape[0]
    tm = min(tm, T)
    nt = T // tm

    def body(dz_in, dy_ref, dyn_ref, zm_ref, zp_ref, zn_ref, sw_ref, wso_ref, dz_ref, dsw_ref, dwso_ref, acc_ref):
        del dz_in
        i = pl.program_id(0)
        keep_p = jnp.where(i == 0, 0.0, 1.0).astype(F32)
        keep_n = jnp.where(i == nt - 1, 0.0, 1.0).astype(F32)
        sw_ = sw_ref[...]
        zm = zm_ref[...].astype(F32)
        zp = zp_ref[...].astype(F32)
        zb, zc, zh = zm[:, :D], zm[:, D:2 * D], zm[:, 2 * D:]
        u_ext = jnp.concatenate([zp[:, D:2 * D] * zp[:, 2 * D:] * keep_p, zc * zh], axis=0)
        cv, shifted = _sconv_cv(u_ext, sw_)
        dy_ext = jnp.concatenate([dy_ref[...], dyn_ref[...]], axis=0)
        ds_ext = lax.dot_general(dy_ext, wso_ref[...], (((1,), (1,)), ((), ())), preferred_element_type=F32)
        zb_ext = jnp.concatenate([zb, zn_ref[...][:, :D].astype(F32) * keep_n], axis=0)
        dcv_ext = ds_ext * zb_ext
        du = sw_[2:3, :] * dcv_ext[:tm]
        for k in range(2):
            du = du + sw_[k:k + 1, :] * pltpu.roll(dcv_ext, tm + HALO - (2 - k), axis=0)[:tm]
        dz_ref[...] = jnp.concatenate([ds_ext[:tm] * cv, du * zh, du * zc], axis=1).astype(dz_ref.dtype)
        dcv = dcv_ext[:tm]
        dsw = jnp.concatenate([jnp.sum(dcv * shifted[k], axis=0, keepdims=True) for k in range(3)], axis=0)
        dwso = lax.dot_general((zb * cv).astype(BF), dy_ref[...], (((0,), (0,)), ((), ())),
                               preferred_element_type=F32)

        @pl.when(i == 0)
        def _():
            dsw_ref[...] = dsw
            acc_ref[...] = dwso

        @pl.when(i > 0)
        def _():
            dsw_ref[...] += dsw
            acc_ref[...] += dwso

        @pl.when(i == nt - 1)
        def _():
            dwso_ref[...] = acc_ref[...].astype(BF)

    return pl.pallas_call(
        body, name=name, grid=(nt,),
        in_specs=[pl.BlockSpec(memory_space=pl.ANY),
                  pl.BlockSpec((tm, D), lambda i: (i, 0)), pl.BlockSpec((HALO, D), _next_halo(tm, T, 0)),
                  pl.BlockSpec((tm, 3 * D), lambda i: (i, ZB_SCONV)),
                  pl.BlockSpec((HALO, 3 * D), _prev_halo(tm, ZB_SCONV)),
                  pl.BlockSpec((HALO, 3 * D), _next_halo(tm, T, ZB_SCONV)),
                  pl.BlockSpec((3, D), lambda i: (0, 0)), pl.BlockSpec((D, D), lambda i: (0, 0))],
        out_specs=[pl.BlockSpec((tm, 3 * D), lambda i: (i, ZB_SCONV)), pl.BlockSpec((3, D), lambda i: (0, 0)),
                   pl.BlockSpec((D, D), lambda i: (0, 0))],
        out_shape=[jax.ShapeDtypeStruct(dz.shape, dz.dtype), jax.ShapeDtypeStruct((3, D), F32),
                   jax.ShapeDtypeStruct((D, D), BF)],
        scratch_shapes=[pltpu.VMEM((D, D), F32)],
        input_output_aliases={0: 0}, compiler_params=_cp("arbitrary"))(dz, dyc, dyc, z, z, z, sw, wso)


def merge_fwd(z, yp_pre, yl, yc, ps, *, tm, name):
    T = z.shape[0]
    tm = min(tm, T)

    def body(zg_ref, yp_ref, yl_ref, yc_ref, ps_ref, o_ref):
        gts = _sigmoid(zg_ref[...].astype(F32))
        m = (gts[:, :D] * (yp_ref[...].astype(F32) * ps_ref[...]) + gts[:, D:2 * D] * yl_ref[...].astype(F32)
             + gts[:, 2 * D:] * yc_ref[...].astype(F32))
        o_ref[...] = m.astype(o_ref.dtype)

    row = lambda i: (i, 0)
    return pl.pallas_call(
        body, name=name, grid=(T // tm,),
        in_specs=[pl.BlockSpec((tm, 3 * D), lambda i: (i, ZB_GATE)), pl.BlockSpec((tm, D), row),
                  pl.BlockSpec((tm, D), row), pl.BlockSpec((tm, D), row), pl.BlockSpec((1, D), lambda i: (0, 0))],
        out_specs=pl.BlockSpec((tm, D), row), out_shape=jax.ShapeDtypeStruct((T, D), BF),
        compiler_params=_cp("arbitrary"))(z, yp_pre, yl, yc, ps)


def merge_bwd(dm, z, yp_pre, yl, yc, ps, *, tm, name):
    T = z.shape[0]
    tm = min(tm, T)

    def body(dm_ref, zg_ref, yp_ref, yl_ref, yc_ref, ps_ref, dz_ref, dyp_ref, dyl_ref, dyc_ref):
        gts = _sigmoid(zg_ref[...].astype(F32))
        dmv = dm_ref[...].astype(F32)
        ys = (yp_ref[...].astype(F32) * ps_ref[...], yl_ref[...].astype(F32), yc_ref[...].astype(F32))
        outs = (dyp_ref, dyl_ref, dyc_ref)
        dgs = []
        for j in range(3):
            gj = gts[:, j * D:(j + 1) * D]
            outs[j][...] = (dmv * gj).astype(BF)
            dgs.append(dmv * ys[j] * gj * (1.0 - gj))
        dz_ref[...] = jnp.concatenate(dgs, axis=1).astype(dz_ref.dtype)

    row = lambda i: (i, 0)
    return pl.pallas_call(
        body, name=name, grid=(T // tm,),
        in_specs=[pl.BlockSpec((tm, D), row), pl.BlockSpec((tm, 3 * D), lambda i: (i, ZB_GATE)),
                  pl.BlockSpec((tm, D), row), pl.BlockSpec((tm, D), row), pl.BlockSpec((tm, D), row),
                  pl.BlockSpec((1, D), lambda i: (0, 0))],
        out_specs=[pl.BlockSpec((tm, 3 * D), lambda i: (i, ZB_GATE)), pl.BlockSpec((tm, D), row),
                   pl.BlockSpec((tm, D), row), pl.BlockSpec((tm, D), row)],
        out_shape=[jax.ShapeDtypeStruct((T, IN_COLS), BF), jax.ShapeDtypeStruct((T, D), BF),
                   jax.ShapeDtypeStruct((T, D), BF), jax.ShapeDtypeStruct((T, D), BF)],
        compiler_params=_cp("arbitrary"))(dm, z, yp_pre, yl, yc, ps)


def _attn_probs(qh, kh):
    s = lax.dot_general(qh, kh, (((1,), (1,)), ((), ())), preferred_element_type=F32) * (X_HD ** -0.5)
    e = jnp.exp(s - jnp.max(s, axis=-1, keepdims=True))
    return e / jnp.sum(e, axis=-1, keepdims=True)


def attn_fwd(xb, wq, kb, vb, *, tm, name):
    T = xb.shape[0]
    tm = min(tm, T)

    def body(x_ref, wq_ref, k_ref, v_ref, q_ref, o_ref):
        q = jnp.dot(x_ref[...], wq_ref[...], preferred_element_type=F32).astype(BF)
        q_ref[...] = q
        outs = []
        for h in range(X_HEADS):
            sl = slice(h * X_HD, (h + 1) * X_HD)
            p = _attn_probs(q[:, sl], k_ref[:, sl])
            outs.append(jnp.dot(p.astype(BF), v_ref[:, sl], preferred_element_type=F32))
        o_ref[...] = jnp.concatenate(outs, axis=1).astype(BF)

    row = lambda i: (i, 0)
    fix = lambda i: (0, 0)
    return pl.pallas_call(
        body, name=name, grid=(T // tm,),
        in_specs=[pl.BlockSpec((tm, D), row), pl.BlockSpec((D, D), fix), pl.BlockSpec((N_MEM, D), fix),
                  pl.BlockSpec((N_MEM, D), fix)],
        out_specs=[pl.BlockSpec((tm, D), row), pl.BlockSpec((tm, D), row)],
        out_shape=[jax.ShapeDtypeStruct((T, D), BF), jax.ShapeDtypeStruct((T, D), BF)],
        compiler_params=_cp("arbitrary"))(xb, wq, kb, vb)


def attn_bwd(dxa, wo, q, kb, vb, xb, *, tm, name):
    T = q.shape[0]
    tm = min(tm, T)
    nt = T // tm

    def body(d_ref, wo_ref, q_ref, k_ref, v_ref, x_ref, dq_ref, dk_ref, dv_ref, dwq_ref, acc_ref):
        i = pl.program_id(0)
        do = lax.dot_general(d_ref[...], wo_ref[...], (((1,), (1,)), ((), ())),
                             preferred_element_type=F32).astype(BF)
        q = q_ref[...]
        dqs, dks, dvs = [], [], []
        for h in range(X_HEADS):
            sl = slice(h * X_HD, (h + 1) * X_HD)
            kh, vh = k_ref[:, sl], v_ref[:, sl]
            p = _attn_probs(q[:, sl], kh)
            dp = lax.dot_general(do[:, sl], vh, (((1,), (1,)), ((), ())), preferred_element_type=F32)
            ds = (p * (dp - jnp.sum(dp * p, axis=-1, keepdims=True)) * (X_HD ** -0.5)).astype(BF)
            dqs.append(jnp.dot(ds, kh, preferred_element_type=F32))
            dks.append(lax.dot_general(ds, q[:, sl], (((0,), (0,)), ((), ())), preferred_element_type=F32))
            dvs.append(lax.dot_general(p.astype(BF), do[:, sl], (((0,), (0,)), ((), ())),
                                       preferred_element_type=F32))
        dqb = jnp.concatenate(dqs, axis=1).astype(BF)
        dq_ref[...] = dqb
        dk = jnp.concatenate(dks, axis=1)
        dv = jnp.concatenate(dvs, axis=1)
        dwq = lax.dot_general(x_ref[...], dqb, (((0,), (0,)), ((), ())), preferred_element_type=F32)

        @pl.when(i == 0)
        def _():
            dk_ref[...] = dk
            dv_ref[...] = dv
            acc_ref[...] = dwq

        @pl.when(i > 0)
        def _():
            dk_ref[...] += dk
            dv_ref[...] += dv
            acc_ref[...] += dwq

        @pl.when(i == nt - 1)
        def _():
            dwq_ref[...] = acc_ref[...].astype(BF)

    row = lambda i: (i, 0)
    fix = lambda i: (0, 0)
    return pl.pallas_call(
        body, name=name, grid=(nt,),
        in_specs=[pl.BlockSpec((tm, D), row), pl.BlockSpec((D, D), fix), pl.BlockSpec((tm, D), row),
                  pl.BlockSpec((N_MEM, D), fix), pl.BlockSpec((N_MEM, D), fix), pl.BlockSpec((tm, D), row)],
        out_specs=[pl.BlockSpec((tm, D), row), pl.BlockSpec((N_MEM, D), fix), pl.BlockSpec((N_MEM, D), fix),
                   pl.BlockSpec((D, D), fix)],
        out_shape=[jax.ShapeDtypeStruct((T, D), BF), jax.ShapeDtypeStruct((N_MEM, D), F32),
                   jax.ShapeDtypeStruct((N_MEM, D), F32), jax.ShapeDtypeStruct((D, D), BF)],
        scratch_shapes=[pltpu.VMEM((D, D), F32)],
        compiler_params=_cp("arbitrary"))(dxa, wo, q, kb, vb, xb)


TM_MM = 1024
TM_EW = 512
TM_SEQ = 512
TT_DW = 2048


def _mem_kv(l, memb, W):
    kb = mm_nn(memb, W['xa_w_k'][l], None, out_dtype=BF, tm=N_MEM, tn=1024, name=f"l{l}_mem_k")
    vb = mm_nn(memb, W['xa_w_v'][l], None, out_dtype=BF, tm=N_MEM, tn=1024, name=f"l{l}_mem_v")
    return kb, vb


def _layer_fwd(l, x, xb, kb, vb, W, host=None, host2=None, after_in_proj=None):
    n = f"l{l}_"
    sv = {'x0': x if xb is None else xb}
    z = mm_nn(sv['x0'], W['w_in'][l], W['b_in'][l], out_dtype=BF, tm=2 * TM_MM, tn=1024, name=n + "in_proj",
              host=host)
    if host is not None:
        z, sv['hosted'] = z
        if after_in_proj is not None:
            after_in_proj(sv['hosted'])
    if kb is None:
        kb, vb = _mem_kv(l, vb, W)
    sv['kb'], sv['vb'] = kb, vb
    yp = pool_fwd(z, W['pool_w'][l], tm=TM_SEQ, name=n + "pool_fwd")
    h, yl = lru_fwd(z, W['lru_conv_w'][l], W['lru_conv_b'][l], W['lru_w_r'][l], W['lru_b_r'][l], W['lru_w_i'][l],
                    W['lru_b_i'][l], W['lru_lambda'][l], W['lru_w_out'][l], tm=TM_SEQ, name=n + "lru_fwd")
    yc = sconv_fwd(z, W['sconv_w'][l], W['sconv_w_out'][l], tm=TM_SEQ, name=n + "sconv_fwd")
    merged = merge_fwd(z, yp, yl, yc, W['pool_scale'][l], tm=TM_EW, name=n + "merge_fwd")
    x1, x1b, xh1, rs1 = mm_res_ln(merged, W['w_mix_out'][l], x, W['ln_g'][l][0:1], W['ln_b'][l][0:1], tm=TM_MM,
                                  name=n + "mix_out_ln")
    q, o = attn_fwd(x1b, W['xa_w_q'][l], kb, vb, tm=TM_EW, name=n + "attn_fwd")
    x2, x2b, xh2, rs2 = mm_res_ln(o, W['xa_w_o'][l], x1, W['ln_g'][l][1:2], W['ln_b'][l][1:2], tm=TM_MM,
                                  name=n + "attn_out_ln")
    res = ffn_in_swiglu(x2b, W['ffn_w_gu'][l], tm=TM_MM, name=n + "ffn_in", host=host2)
    gu, hdn = res[:2]
    if host2 is not None:
        sv['hosted2'] = res[2]
    x3, x3b, xh3, rs3 = mm_res_ln(hdn, W['ffn_w_down'][l], x2, W['ln_g'][l][2:3], W['ln_b'][l][2:3], tm=TM_EW,
                                  name=n + "ffn_out_ln")
    sv.update(z=z, yp=yp, h=h, yl=yl, yc=yc, merged=merged, x1b=x1b, xh1=xh1, rs1=rs1, q=q, o=o, x2b=x2b,
              xh2=xh2, rs2=rs2, gu=gu, hdn=hdn, xh3=xh3, rs3=rs3)
    return x3, x3b, sv


def _layer_bwd(l, dx3, sv, memb, kb, vb, W, loss_from=None, chain=None, last_host_fn=None):
    n = f"l{l}_"
    G = {}
    res = ln_bwd(dx3, sv['xh3'], sv['rs3'], W['ln_g'][l][2:3], tm=TM_EW, name=n + "ln3_bwd", loss_from=loss_from,
                 dw_of=sv['hdn'])
    dp3, dp3b, dg3, db3 = res[:4]
    loss = res[4] if loss_from is not None else None
    G['ffn_w_down'] = res[-1]
    host_b = host_c = None
    dgu = ffn_down_dx_swiglu(dp3b, W['ffn_w_down'][l], sv['gu'], tm=TM_EW, name=n + "ffn_down_dx",
                             host=None if chain is None else chain[0])
    if chain is not None:
        dgu, got = dgu
        host_b = chain[1](got)
    dx2 = mm_nt(dgu, W['ffn_w_gu'][l], dp3, out_dtype=F32, tm=TM_MM, tc=1408, name=n + "ffn_in_dx", host=host_b)
    if chain is not None:
        dx2, got = dx2
        host_c = chain[2](got)
    G['ffn_w_gu'] = mm_tn(sv['x2b'], dgu, out_dtype=BF, tk=1024, tn=1408, tt=TT_DW,name=n + "ffn_in_dw")

    dp2, dp2b, dg2, db2, G['xa_w_o'] = ln_bwd(dx2, sv['xh2'], sv['rs2'], W['ln_g'][l][1:2], tm=TM_MM,
                                              name=n + "ln2_bwd", dw_of=sv['o'])
    dq, dk, dv, G['xa_w_q'] = attn_bwd(dp2b, W['xa_w_o'][l], sv['q'], kb, vb, sv['x1b'], tm=TM_MM,
                                       name=n + "attn_bwd")
    dx1 = mm_nt(dq, W['xa_w_q'][l], dp2, out_dtype=F32, tm=TM_MM, tc=D, name=n + "attn_q_dx", host=host_c)
    if chain is not None:
        dx1, got = dx1
        chain[3](got)
    G['xa_w_k'] = mm_tn(memb, dk, out_dtype=BF, tk=1024, tn=1024, tt=N_MEM, name=n + "attn_k_dw")
    G['xa_w_v'] = mm_tn(memb, dv, out_dtype=BF, tk=1024, tn=1024, tt=N_MEM, name=n + "attn_v_dw")

    dp1, dp1b, dg1, db1, G['w_mix_out'] = ln_bwd(dx1, sv['xh1'], sv['rs1'], W['ln_g'][l][0:1], tm=TM_MM,
                                                 name=n + "ln1_bwd", dw_of=sv['merged'])
    dmerged = mm_nt(dp1b, W['w_mix_out'][l], None, out_dtype=BF, tm=TM_MM, tc=D, name=n + "mix_out_dx")
    z = sv['z']
    dz, dyp, dyl, dyc = merge_bwd(dmerged, z, sv['yp'], sv['yl'], sv['yc'], W['pool_scale'][l], tm=TM_EW,
                                  name=n + "merge_bwd")
    dz, G['pool_w'], G['pool_scale'] = pool_bwd(dz, dyp, sv['yp'], z, W['pool_w'][l], W['pool_scale'][l],
                                                tm=TM_SEQ, name=n + "pool_bwd")
    (dz, G['lru_conv_w'], G['lru_conv_b'], G['lru_w_r'], G['lru_b_r'], G['lru_w_i'], G['lru_b_i'], G['lru_lambda'],
     G['lru_w_out']) = lru_bwd(dz, dyl, z, sv['h'], W['lru_conv_w'][l], W['lru_conv_b'][l], W['lru_w_r'][l],
                               W['lru_b_r'][l], W['lru_w_i'][l], W['lru_b_i'][l], W['lru_lambda'][l],
                               W['lru_w_out'][l], tm=TM_SEQ, name=n + "lru_bwd")
    dz, G['sconv_w'], G['sconv_w_out'] = sconv_bwd(dz, dyc, z, W['sconv_w'][l], W['sconv_w_out'][l], tm=TM_SEQ,
                                                   name=n + "sconv_bwd")
    G['w_in'], G['b_in'] = mm_tn(sv['x0'], dz, out_dtype=BF, tk=1024, tn=1024, tt=TT_DW, name=n + "in_proj_dw",
                                 colsum=True)
    G['ln_g'] = jnp.concatenate([dg1, dg2, dg3], axis=0)
    G['ln_b'] = jnp.concatenate([db1, db2, db3], axis=0)
    last_host = None if last_host_fn is None else last_host_fn(G)
    dx0 = mm_nt(dz, W['w_in'][l], dp1, out_dtype=F32, tm=TM_MM, tc=2048, name=n + "in_proj_dx", host=last_host)
    if last_host is not None:
        dx0, G['hosted_last'] = dx0
    return dx0, G, loss


def local_step(x, mem, target, W):
    memb = mem.astype(BF)
    saves, kvs = [], []
    xf, xb = x, None
    for l in range(DEPTH):
        kb = mm_nn(memb, W['xa_w_k'][l], None, out_dtype=BF, tm=N_MEM, tn=1024, name=f"l{l}_mem_k")
        vb = mm_nn(memb, W['xa_w_v'][l], None, out_dtype=BF, tm=N_MEM, tn=1024, name=f"l{l}_mem_v")
        xf, xb, sv = _layer_fwd(l, xf, xb, kb, vb, W)
        saves.append(sv)
        kvs.append((kb, vb))
    grads = [None] * DEPTH
    dx, loss = None, None
    for l in reversed(range(DEPTH)):
        lf = (W['ln_b'][l][2:3], target) if l == DEPTH - 1 else None
        dx, grads[l], ls = _layer_bwd(l, dx, saves[l], memb, kvs[l][0], kvs[l][1], W, loss_from=lf)
        if ls is not None:
            loss = ls
    return loss, dx, grads


def _coords():
    return lax.axis_index("x"), lax.axis_index("y"), lax.axis_index("c")


FLIPS = ((1, 0), (0, 1), (1, 1))
SQUARES = ('lru_w_out', 'sconv_w_out', 'w_mix_out', 'xa_w_q', 'xa_w_k', 'xa_w_v', 'xa_w_o')
LAYER_SHAPE = {'w_in': (D, IN_COLS), 'pool_w': (4, POOL_GD, POOL_GD), 'ffn_w_gate': (4, D, D_FF // 4),
               'ffn_w_up': (4, D, D_FF // 4), 'ffn_w_down': (D_FF, D), **{n: (D, D) for n in SQUARES}}
PIECES = ('w_in', 'pool_w') + SQUARES + ('ffn_w_gate', 'ffn_w_up', 'ffn_w_down')


def _mult(v, m):
    return v if isinstance(v, int) else pl.multiple_of(v, m)


def _win(name, ref, k):
    if name == 'w_in':
        return ref.at[:, pl.ds(_mult(((2 * k + 6) % 8) * D, D), 2 * D)]
    if name == 'pool_w':
        return ref.at[:, pl.ds(_mult(k * (POOL_GD // 4), POOL_GD // 4), POOL_GD // 4), :]
    if name in ('ffn_w_gate', 'ffn_w_up'):
        return ref.at[k]
    rows = LAYER_SHAPE[name][0] // 4
    return ref.at[pl.ds(_mult(k * rows, 16), rows), :]


def _half_shape(name):
    shard = _shard_shape(name)
    return (shard[0] // 2,) + shard[1:]


def _half(name, ref, h):
    rows = _shard_shape(name)[0] // 2
    if name == 'pool_w':
        return ref.at[pl.ds(h * rows, rows)]
    return ref.at[pl.ds(_mult(h * rows, 16), rows), :]


def _shard_shape(n):
    shp = LAYER_SHAPE[n]
    if n == 'w_in':
        return (shp[0], shp[1] // 4)
    if n == 'pool_w':
        return (shp[0], shp[1] // 4, shp[2])
    if n in ('ffn_w_gate', 'ffn_w_up'):
        return shp[1:]
    return (shp[0] // 4, shp[1])


def sum_chips_small(q, recv3, chip):
    r, C = q.shape
    slot_of_xor = {2: 0, 1: 1, 3: 2}

    def body(me_ref, q_ref, r_ref, o_ref):
        me = me_ref[0]
        acc = None
        for k in range(4):
            kx = k ^ me
            term = q_ref[...]
            for xv, j in slot_of_xor.items():
                term = jnp.where(kx == xv, r_ref[j], term)
            acc = term if acc is None else acc + term
        o_ref[...] = acc

    return pl.pallas_call(
        body, name="sum_chips_small",
        grid_spec=pltpu.PrefetchScalarGridSpec(
            num_scalar_prefetch=1, grid=(1,),
            in_specs=[pl.BlockSpec((r, C), lambda i, me: (0, 0)), pl.BlockSpec((3, r, C), lambda i, me: (0, 0, 0))],
            out_specs=pl.BlockSpec((r, C), lambda i, me: (0, 0))),
        out_shape=jax.ShapeDtypeStruct((r, C), F32), compiler_params=_cp("arbitrary"))(chip, q, recv3)


def _row_tile(R, C):
    for cand in (1024, 512, 256, 128, 64, 32, 16):
        if R % cand == 0 and cand * C * 4 <= 2 * 1024 * 1024:
            return cand
    return R


def _adamw_math(w, g, m, v):
    mn = ADAM_B1 * m + (1.0 - ADAM_B1) * g
    vn = ADAM_B2 * v + (1.0 - ADAM_B2) * (g * g)
    m_hat = mn / (1.0 - ADAM_B1 ** ADAM_STEP)
    v_hat = vn / (1.0 - ADAM_B2 ** ADAM_STEP)
    return -ADAM_LR * (m_hat / (jnp.sqrt(v_hat) + ADAM_EPS) + ADAM_WD * w), mn, vn


def adamw(w, g, m, v, *, name):
    shp = w.shape
    args = [t.reshape(-1, shp[-1]) for t in (w, g, m, v)]
    R, C = args[0].shape
    tr = _row_tile(R, C)

    def body(w_ref, g_ref, m_ref, v_ref, d_ref, mo_ref, vo_ref):
        d_ref[...], mo_ref[...], vo_ref[...] = _adamw_math(w_ref[...], g_ref[...], m_ref[...], v_ref[...])

    spec = pl.BlockSpec((tr, C), lambda i: (i, 0))
    res = pl.pallas_call(
        body, name=name, grid=(R // tr,), in_specs=[spec] * 4, out_specs=[spec] * 3,
        out_shape=[jax.ShapeDtypeStruct((R, C), F32)] * 3, compiler_params=_cp("arbitrary"))(*args)
    return [r.reshape(shp) for r in res]


def _remote(src, dst, send_sems, recv_sems, k, peer):
    return pltpu.make_async_remote_copy(src_ref=src, dst_ref=dst, send_sem=send_sems.at[k], recv_sem=recv_sems.at[k],
                                        device_id=peer, device_id_type=MESH)


def gather_layer(l, shards, small=None, pieces=PIECES):
    n_p = len(pieces)
    with_small = small is not None

    def descs(h_in, h_out, sems):
        srcs = dict(zip(pieces, h_in[:n_p]))
        outs = dict(zip(pieces, h_out[:n_p]))
        ici_s, ici_r, d2d_s, d2d_r, own_s, own_r = sems
        x, y, c = _coords()
        me = 2 * x + y
        sib = (x, y, 1 - c)
        ici, fwd, fwd_in, own = [], [], [], []
        for j, (fx, fy) in enumerate(FLIPS):
            peer = (x ^ fx, y ^ fy, c)
            other = 2 * (x ^ fx) + (y ^ fy)
            for p, n in enumerate(pieces):
                k = 3 * p + j
                mine = _half(n, _win(n, outs[n], me), c)
                landed = _half(n, _win(n, outs[n], other), c)
                sib_half = _half(n, _win(n, outs[n], other), 1 - c)
                ici.append((_remote(_half(n, srcs[n].at[l], c), mine, ici_s, ici_r, k, peer),
                            _remote(_half(n, srcs[n].at[l], c), landed, ici_s, ici_r, k, peer)))
                fwd.append(_remote(landed, landed, d2d_s, d2d_r, k, sib))
                fwd_in.append(_remote(sib_half, sib_half, d2d_s, d2d_r, k, sib))
            if with_small:
                k = 3 * n_p + j
                ici.append((_remote(h_in[n_p], h_out[n_p].at[me], ici_s, ici_r, k, peer),
                            _remote(h_in[n_p], h_out[n_p].at[other], ici_s, ici_r, k, peer)))
        for p, n in enumerate(pieces):
            own.append(_remote(srcs[n].at[l], _win(n, outs[n], me), own_s, own_r, p, sib))
        if with_small:
            own.append(_remote(h_in[n_p], h_out[n_p].at[me], own_s, own_r, n_p, sib))
        return ici, fwd, fwd_in, own

    def start(h_in, h_out, sems):
        ici, _, _, own = descs(h_in, h_out, sems)
        for send, _ in ici:
            send.start()
        for cp in own:
            cp.start()

    def finish(h_in, h_out, sems):
        ici, fwd, fwd_in, own = descs(h_in, h_out, sems)
        per_chip = n_p + (1 if with_small else 0)
        for j in range(3):
            for p in range(n_p):
                ici[j * per_chip + p][1].wait_recv()
                fwd[j * n_p + p].start()
            if with_small:
                ici[j * per_chip + n_p][1].wait_recv()
        for cp in fwd_in:
            cp.wait_recv()
        for send, _ in ici:
            send.wait_send()
        for cp in fwd:
            cp.wait_send()
        for cp in own:
            cp.wait()

    arrays = [shards[n] for n in pieces] + ([small] if with_small else [])
    out_shape = [jax.ShapeDtypeStruct(LAYER_SHAPE[n], BF) for n in pieces]
    if with_small:
        out_shape.append(jax.ShapeDtypeStruct((4,) + small.shape, small.dtype))
    sems = [pltpu.SemaphoreType.DMA((3 * n_p + 3,)), pltpu.SemaphoreType.DMA((3 * n_p + 3,)),
            pltpu.SemaphoreType.DMA((3 * n_p,)), pltpu.SemaphoreType.DMA((3 * n_p,)),
            pltpu.SemaphoreType.DMA((n_p + 1,)), pltpu.SemaphoreType.DMA((n_p + 1,))]
    return Hosted(arrays, out_shape, sems, start, finish)


def both_hosted(h1, h2):
    a1, o1, s1 = len(h1.arrays), len(h1.out_shape), len(h1.sems)

    def start(h_in, h_out, sems):
        h1.start(h_in[:a1], h_out[:o1], sems[:s1])
        h2.start(h_in[a1:], h_out[o1:], sems[s1:])

    def finish(h_in, h_out, sems):
        h1.finish(h_in[:a1], h_out[:o1], sems[:s1])
        h2.finish(h_in[a1:], h_out[o1:], sems[s1:])

    return Hosted(list(h1.arrays) + list(h2.arrays), list(h1.out_shape) + list(h2.out_shape),
                  list(h1.sems) + list(h2.sems), start, finish)


def run_hosted(host, name):
    n_in, n_out = len(host.arrays), len(host.out_shape)

    def body(*refs):
        h_in, h_out, sems = refs[:n_in], refs[n_in:n_in + n_out], refs[n_in + n_out:]
        host.start(h_in, h_out, sems)
        host.finish(h_in, h_out, sems)

    return pl.pallas_call(body, name=name, in_specs=[HBM_SPEC] * n_in, out_specs=[HBM_SPEC] * n_out,
                          out_shape=list(host.out_shape), scratch_shapes=list(host.sems))(*host.arrays)


def split_halves_hosted(parts, small):
    n_p = len(PIECES)
    rh = small.shape[0] // 2

    def descs(h_in, h_out, sems):
        send_sems, recv_sems = sems
        x, y, c = _coords()
        sib = (x, y, 1 - c)
        cps = []
        for p, n in enumerate(PIECES):
            for k in range(4):
                cps.append(_remote(_half(n, _win(n, h_in[p], k), 1 - c), h_out[p].at[k], send_sems, recv_sems,
                                   4 * p + k, sib))
        cps.append(_remote(h_in[n_p].at[pl.ds(_mult((1 - c) * rh, 8), rh), :], h_out[n_p], send_sems, recv_sems,
                           4 * n_p, sib))
        return cps

    out_shape = [jax.ShapeDtypeStruct((4,) + _half_shape(n), BF) for n in PIECES]
    out_shape.append(jax.ShapeDtypeStruct((rh, small.shape[1]), small.dtype))
    sems = [pltpu.SemaphoreType.DMA((4 * n_p + 1,)), pltpu.SemaphoreType.DMA((4 * n_p + 1,))]
    return _all_at_once([parts[n] for n in PIECES] + [small], out_shape, sems, descs)


def _all_at_once(arrays, out_shape, sems, descs):
    def start(h_in, h_out, s):
        for cp in descs(h_in, h_out, s):
            cp.start()

    def finish(h_in, h_out, s):
        for cp in descs(h_in, h_out, s):
            cp.wait()

    return Hosted(arrays, out_shape, sems, start, finish)


def swap_cores_hosted(s):
    n = len(s)

    def descs(h_in, h_out, sems):
        x, y, c = _coords()
        return [_remote(h_in[i], h_out[i], sems[0], sems[1], i, (x, y, 1 - c)) for i in range(n)]

    return _all_at_once(list(s), [jax.ShapeDtypeStruct(a.shape, a.dtype) for a in s],
                        [pltpu.SemaphoreType.DMA((n,)), pltpu.SemaphoreType.DMA((n,))], descs)


SAME_SHAPE = (('w_in',), ('pool_w',), SQUARES, ('ffn_w_gate', 'ffn_w_up'), ('ffn_w_down',))


def add_halves(names, parts, theirs, core):
    name = names[0]
    half = _half_shape(name)
    zero = (0,) * len(half)
    if name == 'w_in':
        pspec = pl.BlockSpec(half, lambda k, cc: (cc[0], (k + 3) % 4))
    elif name == 'pool_w':
        pspec = pl.BlockSpec(half, lambda k, cc: (cc[0], k, 0))
    elif name in ('ffn_w_gate', 'ffn_w_up'):
        pspec = pl.BlockSpec((None,) + half, lambda k, cc: (k, cc[0], 0))
    else:
        pspec = pl.BlockSpec(half, lambda k, cc: (2 * k + cc[0], 0))
    slot = pl.BlockSpec((None,) + half, lambda k, cc: (k,) + zero)
    m = len(names)

    def body(cc_ref, *refs):
        del cc_ref
        for i in range(m):
            refs[2 * m + i][...] = (refs[i][...].astype(F32) + refs[m + i][...].astype(F32)).astype(BF)

    res = pl.pallas_call(
        body, name="add_cores_" + name,
        grid_spec=pltpu.PrefetchScalarGridSpec(num_scalar_prefetch=1, grid=(4,), in_specs=[pspec] * m + [slot] * m,
                                               out_specs=[slot] * m),
        out_shape=[jax.ShapeDtypeStruct((4,) + half, BF)] * m, compiler_params=_cp("arbitrary"))(
            core, *[parts[n] for n in names], *[theirs[n] for n in names])
    return dict(zip(names, res))


def add_halves_small(small, theirs, core):
    rh, C = theirs.shape

    def body(cc_ref, p_ref, t_ref, o_ref):
        del cc_ref
        o_ref[...] = p_ref[...] + t_ref[...]

    blk = pl.BlockSpec((rh, C), lambda i, cc: (0, 0))
    return pl.pallas_call(
        body, name="add_cores_small",
        grid_spec=pltpu.PrefetchScalarGridSpec(
            num_scalar_prefetch=1, grid=(1,),
            in_specs=[pl.BlockSpec((rh, C), lambda i, cc: (cc[0], 0)), blk], out_specs=blk),
        out_shape=jax.ShapeDtypeStruct((rh, C), F32), compiler_params=_cp("arbitrary"))(core, small, theirs)


def exchange_halves(q, qsmall):
    n_p = len(PIECES)

    def descs(h_in, h_out, sems):
        send_sems, recv_sems = sems
        x, y, c = _coords()
        cps = []
        for j, (fx, fy) in enumerate(FLIPS):
            peer = (x ^ fx, y ^ fy, c)
            other = 2 * (x ^ fx) + (y ^ fy)
            for p in range(n_p):
                cps.append(_remote(h_in[p].at[other], h_out[p].at[j], send_sems, recv_sems, 3 * p + j, peer))
            cps.append(_remote(h_in[n_p], h_out[n_p].at[j], send_sems, recv_sems, 3 * n_p + j, peer))
        return cps

    def start(h_in, h_out, sems):
        for cp in descs(h_in, h_out, sems):
            cp.start()

    def finish(h_in, h_out, sems):
        for cp in descs(h_in, h_out, sems):
            cp.wait()

    arrays = [q[n] for n in PIECES] + [qsmall]
    out_shape = [jax.ShapeDtypeStruct((3,) + _half_shape(n), BF) for n in PIECES]
    out_shape.append(jax.ShapeDtypeStruct((3,) + qsmall.shape, qsmall.dtype))
    sems = [pltpu.SemaphoreType.DMA((3 * n_p + 3,)), pltpu.SemaphoreType.DMA((3 * n_p + 3,))]
    return Hosted(arrays, out_shape, sems, start, finish)


def sum_halves(names, q, recv3, chip):
    half = _half_shape(names[0])
    zero = (0,) * len(half)
    m = len(names)

    def body(me_ref, *refs):
        del me_ref
        for i in range(m):
            acc = refs[i][...].astype(F32)
            for j in range(3):
                acc = acc + refs[m + i][j].astype(F32)
            refs[2 * m + i][...] = acc

    res = pl.pallas_call(
        body, name="sum_chips_" + names[0],
        grid_spec=pltpu.PrefetchScalarGridSpec(
            num_scalar_prefetch=1, grid=(1,),
            in_specs=[pl.BlockSpec((None,) + half, lambda i, me: (me[0],) + zero)] * m
            + [pl.BlockSpec((3,) + half, lambda i, me: (0,) + zero)] * m,
            out_specs=[pl.BlockSpec(half, lambda i, me: zero)] * m),
        out_shape=[jax.ShapeDtypeStruct(half, F32)] * m, compiler_params=_cp("arbitrary"))(
            chip, *[q[n] for n in names], *[recv3[n] for n in names])
    return dict(zip(names, res))


def adamw_halves(w, g0, m, v, *, name):
    shp = w.shape
    C = shp[-1]
    four = (DEPTH, 2, -1, C)
    w4, m4, v4 = [t.reshape(four) for t in (w, m, v)]
    gs = [t.reshape(-1, C) for pair in g0 for t in pair]
    Rh = gs[0].shape[0]
    tr = _row_tile(Rh, C)

    def body(w_ref, a0_ref, b0_ref, a1_ref, b1_ref, m_ref, v_ref, g_ref, d_ref, mo_ref, vo_ref):
        mine = pl.program_id(1) == lax.axis_index("c")
        g_l0 = jnp.where(mine, a0_ref[...], b0_ref[...])
        g_l1 = jnp.where(mine, a1_ref[...], b1_ref[...])
        g = jnp.where(pl.program_id(0) == 0, g_l0, g_l1)
        g_ref[...] = g
        d_ref[...], mo_ref[...], vo_ref[...] = _adamw_math(w_ref[...], g, m_ref[...], v_ref[...])

    lay = pl.BlockSpec((None, None, tr, C), lambda l, h, i: (l, h, i, 0))
    one = pl.BlockSpec((tr, C), lambda l, h, i: (i, 0))
    res = pl.pallas_call(
        body, name=name, grid=(DEPTH, 2, Rh // tr), in_specs=[lay, one, one, one, one, lay, lay],
        out_specs=[lay] * 4, out_shape=[jax.ShapeDtypeStruct(w4.shape, F32)] * 4,
        compiler_params=_cp("arbitrary", "arbitrary", "arbitrary"))(w4, *gs, m4, v4)
    return [r.reshape(shp) for r in res]


def _local_shape(name, full_shape):
    shp = list(full_shape)
    ax = BIG_SHARDED.get(name, SMALL_SHARDED.get(name))
    if ax is not None:
        shp[ax] //= 4
    return tuple(shp)


FULL_SHAPES = {
    'w_in': (DEPTH, D, IN_COLS), 'b_in': (DEPTH, IN_COLS), 'pool_w': (DEPTH, 4, POOL_GD, POOL_GD),
    'pool_scale': (DEPTH, D), 'lru_conv_w': (DEPTH, 4, D), 'lru_conv_b': (DEPTH, D),
    'lru_w_r': (DEPTH, LRU_HEADS, LRU_HD, LRU_HD), 'lru_b_r': (DEPTH, D),
    'lru_w_i': (DEPTH, LRU_HEADS, LRU_HD, LRU_HD), 'lru_b_i': (DEPTH, D), 'lru_lambda': (DEPTH, D),
    'lru_w_out': (DEPTH, D, D), 'sconv_w': (DEPTH, 3, D), 'sconv_w_out': (DEPTH, D, D), 'w_mix_out': (DEPTH, D, D),
    'xa_w_q': (DEPTH, D, D), 'xa_w_k': (DEPTH, D, D), 'xa_w_v': (DEPTH, D, D), 'xa_w_o': (DEPTH, D, D),
    'ffn_w_gate': (DEPTH, D, D_FF), 'ffn_w_up': (DEPTH, D, D_FF), 'ffn_w_down': (DEPTH, D_FF, D),
    'ln_g': (DEPTH, 3, D), 'ln_b': (DEPTH, 3, D)}


def _pack(arrs, names, width, dtype, row_mult):
    flat = jnp.concatenate([arrs[n].astype(dtype).reshape(-1) for n in names])
    pad = (-flat.shape[0]) % (width * row_mult)
    if pad:
        flat = jnp.concatenate([flat, jnp.zeros((pad,), dtype)])
    return flat.reshape(-1, width)


def _unpack(flat2d, names, shapes):
    flat = flat2d.reshape(-1)
    out, off = {}, 0
    for n in names:
        size = 1
        for s in shapes[n]:
            size *= s
        out[n] = flat[off:off + size].reshape(shapes[n])
        off += size
    return out


def _gathered_full(g4, names, sharded_axis):
    loc_shapes = {n: _local_shape(n, FULL_SHAPES[n]) for n in names}
    per = [_unpack(g4[k], names, loc_shapes) for k in range(4)]
    return {n: jnp.concatenate([per[k][n] for k in range(4)], axis=sharded_axis[n]) for n in names}


def _gu_joined(g4, u4):
    return jnp.concatenate([g4[0], g4[1], u4[0], u4[1], g4[2], g4[3], u4[2], u4[3]], axis=1)


def _gu_apart(a):
    w = a.shape[1] // 8
    cols = [a[:, i * w:(i + 1) * w] for i in range(8)]
    return jnp.stack([cols[0], cols[1], cols[4], cols[5]]), jnp.stack([cols[2], cols[3], cols[6], cols[7]])


SMALL_SH_NAMES = list(SMALL_SHARDED)
SMALL_ROWS = 32


def kernel(x, mem, w_in, b_in, pool_w, pool_scale, lru_conv_w, lru_conv_b, lru_w_r, lru_b_r, lru_w_i, lru_b_i, lru_lambda, lru_w_out, sconv_w, sconv_w_out, w_mix_out, xa_w_q, xa_w_k, xa_w_v, xa_w_o, ffn_w_gate, ffn_w_up, ffn_w_down, ln_g, ln_b, loss_target, m_w_in, m_b_in, m_pool_w, m_pool_scale, m_lru_conv_w, m_lru_conv_b, m_lru_w_r, m_lru_b_r, m_lru_w_i, m_lru_b_i, m_lru_lambda, m_lru_w_out, m_sconv_w, m_sconv_w_out, m_w_mix_out, m_xa_w_q, m_xa_w_k, m_xa_w_v, m_xa_w_o, m_ffn_w_gate, m_ffn_w_up, m_ffn_w_down, m_ln_g, m_ln_b, v_w_in, v_b_in, v_pool_w, v_pool_scale, v_lru_conv_w, v_lru_conv_b, v_lru_w_r, v_lru_b_r, v_lru_w_i, v_lru_b_i, v_lru_lambda, v_lru_w_out, v_sconv_w, v_sconv_w_out, v_w_mix_out, v_xa_w_q, v_xa_w_k, v_xa_w_v, v_xa_w_o, v_ffn_w_gate, v_ffn_w_up, v_ffn_w_down, v_ln_g, v_ln_b):
    loc = dict(w_in=w_in, b_in=b_in, pool_w=pool_w, pool_scale=pool_scale, lru_conv_w=lru_conv_w,
               lru_conv_b=lru_conv_b, lru_w_r=lru_w_r, lru_b_r=lru_b_r, lru_w_i=lru_w_i, lru_b_i=lru_b_i,
               lru_lambda=lru_lambda, lru_w_out=lru_w_out, sconv_w=sconv_w, sconv_w_out=sconv_w_out,
               w_mix_out=w_mix_out, xa_w_q=xa_w_q, xa_w_k=xa_w_k, xa_w_v=xa_w_v, xa_w_o=xa_w_o,
               ffn_w_gate=ffn_w_gate, ffn_w_up=ffn_w_up, ffn_w_down=ffn_w_down, ln_g=ln_g, ln_b=ln_b)
    mom = dict(w_in=m_w_in, b_in=m_b_in, pool_w=m_pool_w, pool_scale=m_pool_scale, lru_conv_w=m_lru_conv_w,
               lru_conv_b=m_lru_conv_b, lru_w_r=m_lru_w_r, lru_b_r=m_lru_b_r, lru_w_i=m_lru_w_i, lru_b_i=m_lru_b_i,
               lru_lambda=m_lru_lambda, lru_w_out=m_lru_w_out, sconv_w=m_sconv_w, sconv_w_out=m_sconv_w_out,
               w_mix_out=m_w_mix_out, xa_w_q=m_xa_w_q, xa_w_k=m_xa_w_k, xa_w_v=m_xa_w_v, xa_w_o=m_xa_w_o,
               ffn_w_gate=m_ffn_w_gate, ffn_w_up=m_ffn_w_up, ffn_w_down=m_ffn_w_down, ln_g=m_ln_g, ln_b=m_ln_b)
    var = dict(w_in=v_w_in, b_in=v_b_in, pool_w=v_pool_w, pool_scale=v_pool_scale, lru_conv_w=v_lru_conv_w,
               lru_conv_b=v_lru_conv_b, lru_w_r=v_lru_w_r, lru_b_r=v_lru_b_r, lru_w_i=v_lru_w_i, lru_b_i=v_lru_b_i,
               lru_lambda=v_lru_lambda, lru_w_out=v_lru_w_out, sconv_w=v_sconv_w, sconv_w_out=v_sconv_w_out,
               w_mix_out=v_w_mix_out, xa_w_q=v_xa_w_q, xa_w_k=v_xa_w_k, xa_w_v=v_xa_w_v, xa_w_o=v_xa_w_o,
               ffn_w_gate=v_ffn_w_gate, ffn_w_up=v_ffn_w_up, ffn_w_down=v_ffn_w_down, ln_g=v_ln_g, ln_b=v_ln_b)

    chip = 2 * lax.axis_index("x") + lax.axis_index("y")
    core = lax.axis_index("c")
    chip_arr = jnp.reshape(chip, (1,)).astype(jnp.int32)
    core_arr = jnp.reshape(core, (1,)).astype(jnp.int32)
    n_p = len(PIECES)

    shards = {n: loc[n].astype(BF) for n in PIECES}
    small = _pack(loc, SMALL_SH_NAMES, 256, F32, 8)
    first = ('w_in',)
    rest = tuple(n for n in PIECES if n not in first)
    early1 = ('w_in', 'pool_w')
    late1 = tuple(n for n in PIECES if n not in early1)
    got = run_hosted(gather_layer(0, shards, small, pieces=first), "gather_first")
    vec = _gathered_full(got[len(first)], SMALL_SH_NAMES, SMALL_SHARDED)
    W = {n: [None] * DEPTH for n in ('w_in', 'pool_w', 'ffn_w_down', 'ffn_w_gu') + SQUARES}
    W['b_in'] = [jnp.roll(b_in[l:l + 1], -2 * D, axis=1) for l in range(DEPTH)]
    for n in ('lru_conv_w', 'sconv_w', 'ln_g', 'ln_b'):
        W[n] = [vec[n][l] for l in range(DEPTH)]
    for n in ('lru_w_r', 'lru_w_i'):
        W[n] = [loc[n][l].astype(BF) for l in range(DEPTH)]
    for n in ('pool_scale', 'lru_conv_b', 'lru_b_r', 'lru_b_i', 'lru_lambda'):
        W[n] = [loc[n][l:l + 1] for l in range(DEPTH)]

    def take(l, names, arrays):
        full = dict(zip(names, arrays))
        if 'ffn_w_gate' in full:
            W['ffn_w_gu'][l] = _gu_joined(full['ffn_w_gate'], full['ffn_w_up'])
        for n in names:
            if n in W:
                W[n][l] = full[n]

    take(0, first, got[:len(first)])

    def after_in_proj(results):
        take(0, rest, results[:len(rest)])
        take(1, early1, results[len(rest):])

    xs, memb = x[0], mem[0].astype(BF)
    host_a = both_hosted(gather_layer(0, shards, pieces=rest), gather_layer(1, shards, pieces=early1))
    xf, xb, sv0 = _layer_fwd(0, xs, None, None, memb, W, host=host_a, host2=gather_layer(1, shards, pieces=late1),
                             after_in_proj=after_in_proj)
    take(1, late1, sv0['hosted2'])
    xf, xb, sv1 = _layer_fwd(1, xf, xb, None, memb, W)
    saves = [sv0, sv1]
    kvs = [(sv['kb'], sv['vb']) for sv in saves]

    def packed(G):
        g = dict(G)
        g['ffn_w_gate'], g['ffn_w_up'] = _gu_apart(g['ffn_w_gu'])
        g['pool_w'] = g['pool_w'].astype(BF)
        g['b_in'] = jnp.roll(g['b_in'], 2 * D, axis=1)
        return {n: g[n] for n in PIECES}, _pack(g, SMALL_ALL, D, F32, SMALL_ROWS)

    def add_pairs(parts, smallp, got):
        theirs = dict(zip(PIECES, got[:n_p]))
        q = {}
        for group in SAME_SHAPE:
            q.update(add_halves(group, parts, theirs, core_arr))
        qs = add_halves_small(smallp, got[n_p], core_arr)
        return q, qs, exchange_halves(q, qs)

    def sum_chips(q, qs, recv):
        recv3 = dict(zip(PIECES, recv[:n_p]))
        summed = {}
        for group in SAME_SHAPE:
            summed.update(sum_halves(group, q, recv3, chip_arr))
        return [summed[n] for n in PIECES] + [sum_chips_small(qs, recv[n_p], chip_arr)]

    lf = (W['ln_b'][1][2:3], loss_target[0])
    dx, G1, loss_blk = _layer_bwd(1, None, saves[1], memb, kvs[1][0], kvs[1][1], W, loss_from=lf)

    parts1, smallp1 = packed(G1)
    st1, st0 = {}, {}

    def after_down(got):
        st1['q'], st1['qs'], host = add_pairs(parts1, smallp1, got)
        return host

    def after_in(got):
        st1['sums'] = sum_chips(st1['q'], st1['qs'], got)
        return swap_cores_hosted(st1['sums'])

    def after_q(got):
        st1['other'] = list(got)

    def last_host_fn(G):
        parts0, smallp0 = packed(G)
        got = run_hosted(split_halves_hosted(parts0, smallp0), "split_halves")
        st0['q'], st0['qs'], host = add_pairs(parts0, smallp0, got)
        return host

    grad_x, G0, _ = _layer_bwd(0, dx, saves[0], memb, kvs[0][0], kvs[0][1], W,
                               chain=(split_halves_hosted(parts1, smallp1), after_down, after_in, after_q),
                               last_host_fn=last_host_fn)
    sums0 = sum_chips(st0['q'], st0['qs'], G0['hosted_last'])
    red = [(sums0, run_hosted(swap_cores_hosted(sums0), "swap_cores")), (st1['sums'], st1['other'])]
    loss = lax.psum(loss_blk[0, 0], ("x", "y", "c"))

    small_shapes = {n: FULL_SHAPES[n][1:] for n in SMALL_ALL}
    per_layer = []
    for l in range(DEPTH):
        mine, theirs = red[l][0][-1], red[l][1][-1]
        whole = jnp.where(core == 0, jnp.concatenate([mine, theirs]), jnp.concatenate([theirs, mine]))
        per_layer.append(_unpack(whole, SMALL_ALL, small_shapes))
    grads = {}
    for n in SMALL_ALL:
        gn = jnp.stack([per_layer[l][n] for l in range(DEPTH)])
        if n in SMALL_SHARDED:
            size = loc[n].shape[SMALL_SHARDED[n]]
            gn = lax.dynamic_slice_in_dim(gn, chip * size, size, axis=SMALL_SHARDED[n])
        grads[n] = gn

    out_d, out_m, out_v = {}, {}, {}
    for p, n in enumerate(PIECES):
        pairs = [(red[l][0][p], red[l][1][p]) for l in range(DEPTH)]
        grads[n], out_d[n], out_m[n], out_v[n] = adamw_halves(loc[n], pairs, mom[n], var[n], name="adamw_" + n)
    for n in SMALL_ALL:
        out_d[n], out_m[n], out_v[n] = adamw(loc[n], grads[n], mom[n], var[n], name="adamw_" + n)

    return (loss, grad_x[None], *[grads[n] for n in WEIGHTS], *[out_d[n] for n in WEIGHTS],
            *[out_m[n] for n in WEIGHTS], *[out_v[n] for n in WEIGHTS])
```
